```python
import jax, jax.numpy as jnp
from jax import lax
import numpy as np

D_MODEL = 1024
BATCH = 16
SEQ = 256
DEPTH = 4
DEC_BATCH = 2
DEC_SEQ = 4096
PAST_LEN = 256

GRID_W = 64
EPS = 1e-6
NEG_INF = -1e30
NA_HEADS = 6
NA_DIM = 64
NA_WIDTH = NA_HEADS * NA_DIM
NA_ROWS = 8
NA_COLS = 16
NA_QBLOCK = 16
NA_KBLOCK = 32
RPB_ROWS = 2 * NA_ROWS - 1
RPB_COLS = 2 * NA_COLS - 1
CTX_QBLOCK = 128
ML_HEADS = 4
ML_DIM = 96
ML_WIDTH = ML_HEADS * ML_DIM
ML_CHUNK = 64
ML_FORGET_BIAS = 3.0
POOL_WINDOWS = (2, 4, 8, 16)
POOL_GROUPS = 4
POOL_DIM = 64
POOL_WIDTH = POOL_GROUPS * POOL_DIM
MIX_WIDTH = NA_WIDTH + ML_WIDTH + POOL_WIDTH
N_GATE_COLS = 4 * ML_HEADS
IN_COLS = 3 * NA_WIDTH + 4 * ML_WIDTH + N_GATE_COLS + POOL_WIDTH
N_EXPERTS = 16
N_EXPERT_GROUPS = 4
EXPERTS_PER_GROUP = N_EXPERTS // N_EXPERT_GROUPS
TOP_K = 2
D_EXPERT = 512
ADA_DIM = 6 * D_MODEL

kernel_name = 'hybrid_na_mlstm_pool_moe_diffusion_step'


def _in_split_points():
    sizes = [NA_WIDTH] * 3 + [ML_WIDTH] * 4 + [N_GATE_COLS]
    pts, acc = [], 0
    for s in sizes:
        acc += s
        pts.append(acc)
    return pts


def rms_norm(x, g):
    xf = x.astype(jnp.float32)
    y = xf * lax.rsqrt(jnp.mean(xf * xf, axis=-1, keepdims=True) + EPS) * g.astype(jnp.float32)
    return y.astype(x.dtype)


def context_attention(q, k, v):
    B, S, H, d = q.shape
    nb = S // CTX_QBLOCK
    qb = q.reshape(B, nb, CTX_QBLOCK, H, d).transpose(1, 0, 2, 3, 4)

    def one_block(qi):
        s = jnp.einsum('bqhd,bkhd->bhqk', qi, k).astype(jnp.float32) * (d ** -0.5)
        p = jax.nn.softmax(s, axis=-1).astype(v.dtype)
        return jnp.einsum('bhqk,bkhd->bqhd', p, v)

    o = lax.map(one_block, qb)
    return o.transpose(1, 0, 2, 3, 4).reshape(B, S, H, d)


def neighborhood_attention(q, k, v, ck, cv, rpb):
    B, T, H, d = q.shape
    rows = T // GRID_W
    kr = min(NA_ROWS, rows)
    nj = GRID_W // NA_QBLOCK
    r = np.arange(rows)
    row_start = np.clip(r - kr // 2, 0, rows - kr)
    row_idx = row_start[:, None] + np.arange(kr)
    cols = np.arange(GRID_W)
    col_start = np.clip(cols - NA_COLS // 2, 0, GRID_W - NA_COLS)
    blk_start = np.minimum(col_start[::NA_QBLOCK], GRID_W - NA_KBLOCK)
    col_idx = blk_start[:, None] + np.arange(NA_KBLOCK)
    qcol = cols.reshape(nj, NA_QBLOCK)
    cs = col_start.reshape(nj, NA_QBLOCK)
    in_win = (col_idx[:, None, :] >= cs[:, :, None]) & (col_idx[:, None, :] < cs[:, :, None] + NA_COLS)
    dr = row_idx - r[:, None] + NA_ROWS - 1
    dc = np.clip(col_idx[:, None, :] - qcol[:, :, None] + NA_COLS - 1, 0, RPB_COLS - 1)
    bias = rpb.astype(jnp.float32)[:, dr[:, None, None, :, None], dc[None, :, :, None, :]]
    bias = jnp.where(in_win[None, None, :, :, None, :], bias, NEG_INF)
    bias = bias.transpose(1, 2, 0, 3, 4, 5).reshape(rows, nj, H, NA_QBLOCK, kr * NA_KBLOCK)
    ridx = row_idx[:, None, :, None]
    cidx = col_idx[None, :, None, :]
    kgrid = k.reshape(B, rows, GRID_W, H, d)
    vgrid = v.reshape(B, rows, GRID_W, H, d)
    kg = kgrid[:, ridx, cidx].reshape(B, rows, nj, kr * NA_KBLOCK, H, d)
    vg = vgrid[:, ridx, cidx].reshape(B, rows, nj, kr * NA_KBLOCK, H, d)
    qg = q.reshape(B, rows, nj, NA_QBLOCK, H, d)
    scale = d ** -0.5
    s_loc = jnp.einsum('brjqhd,brjkhd->brjhqk', qg, kg).astype(jnp.float32) * scale + bias[None]
    s_ctx = jnp.einsum('brjqhd,bphd->brjhqp', qg, ck).astype(jnp.float32) * scale
    n_loc = s_loc.shape[-1]
    p = jax.nn.softmax(jnp.concatenate([s_loc, s_ctx], axis=-1), axis=-1).astype(v.dtype)
    o = (jnp.einsum('brjhqk,brjkhd->brjqhd', p[..., :n_loc], vg)
         + jnp.einsum('brjhqp,bphd->brjqhd', p[..., n_loc:], cv))
    return o.reshape(B, T, H, d)


def mlstm_chunkwise(q, k, v, i_pre, f_pre, C0, n0, m0):
    B, T, H, d = q.shape
    L = ML_CHUNK
    nc = T // L
    f32 = jnp.float32

    def chunks(a):
        return a.astype(f32).reshape(B, nc, L, H, d).transpose(1, 0, 3, 2, 4)

    def gchunks(a):
        return a.astype(f32).reshape(B, nc, L, H).transpose(1, 0, 3, 2)

    qc = chunks(q) * (d ** -0.5)
    kc, vc = chunks(k), chunks(v)
    log_i = gchunks(i_pre)
    log_f = jax.nn.log_sigmoid(gchunks(f_pre))
    causal = jnp.asarray(np.tril(np.ones((L, L), dtype=bool)))

    def step(carry, xs):
        C, n, m = carry
        qt, kt, vt, li, lf = xs
        b = jnp.cumsum(lf, axis=-1)
        dmat = jnp.where(causal, b[..., :, None] - b[..., None, :] + li[..., None, :], -jnp.inf)
        inter = b + m[..., None]
        m_t = jnp.maximum(inter, jnp.max(dmat, axis=-1))
        w_intra = jnp.exp(dmat - m_t[..., None])
        w_inter = jnp.exp(inter - m_t)
        s = jnp.einsum('bhtd,bhsd->bhts', qt, kt) * w_intra
        num = w_inter[..., None] * jnp.einsum('bhtd,bhde->bhte', qt, C) + jnp.einsum('bhts,bhse->bhte', s, vt)
        den = w_inter * jnp.einsum('bhtd,bhd->bht', qt, n) + jnp.sum(s, axis=-1)
        h = num / jnp.maximum(jnp.abs(den), jnp.exp(-m_t))[..., None]
        b_end = b[..., -1]
        g = b_end[..., None] - b + li
        m_new = jnp.maximum(b_end + m, jnp.max(g, axis=-1))
        decay = jnp.exp(b_end + m - m_new)
        kw = kt * jnp.exp(g - m_new[..., None])[..., None]
        C_new = decay[..., None, None] * C + jnp.einsum('bhsd,bhse->bhde', kw, vt)
        n_new = decay[..., None] * n + jnp.sum(kw, axis=2)
        return (C_new, n_new, m_new), h

    (C, n, m), hs = lax.scan(step, (C0.astype(f32), n0.astype(f32), m0.astype(f32)),
                             (qc, kc, vc, log_i, log_f))
    h = hs.transpose(1, 0, 3, 2, 4).reshape(B, T, H, d)
    return h, (C, n, m)


def mlstm_bidirectional(q, k, v, gates, state):
    C0, n0, m0 = state
    flip = lambda a: jnp.flip(a, axis=1)
    h_f, (Cf, nf, mf) = mlstm_chunkwise(q, k, v, gates[:, :, 0], gates[:, :, 1], C0[:, 0], n0[:, 0], m0[:, 0])
    h_b, (Cb, nb, mb) = mlstm_chunkwise(flip(q), flip(k), flip(v), flip(gates[:, :, 2]), flip(gates[:, :, 3]),
                                        C0[:, 1], n0[:, 1], m0[:, 1])
    h = h_f + flip(h_b)
    return h, (jnp.stack([Cf, Cb], axis=1), jnp.stack([nf, nb], axis=1), jnp.stack([mf, mb], axis=1))


def multiscale_pool(u, w_pool, pool_scale):
    B, T, _ = u.shape
    uf = u.astype(jnp.float32).reshape(B, T, POOL_GROUPS, POOL_DIM)
    csum = jnp.concatenate([jnp.zeros((B, 1, POOL_GROUPS, POOL_DIM), jnp.float32), jnp.cumsum(uf, axis=1)], axis=1)
    t = np.arange(T)
    outs = []
    for g, w in enumerate(POOL_WINDOWS):
        lo = np.clip(t - w // 2, 0, T)
        hi = np.clip(t - w // 2 + w, 0, T)
        cnt = (hi - lo).astype(np.float32)
        mean = (csum[:, hi, g] - csum[:, lo, g]) / cnt[None, :, None]
        outs.append(mean - uf[:, :, g])
    pooled = jnp.stack(outs, axis=2)
    y = jnp.einsum('btgc,gce->btge', pooled, w_pool.astype(jnp.float32)).reshape(B, T, POOL_WIDTH)
    return (y * pool_scale.astype(jnp.float32)).astype(u.dtype)


def grouped_moe(h, w_router, b_router, w_gate, w_up, w_down):
    shp = h.shape
    x = h.reshape(-1, shp[-1])
    scores = jax.nn.sigmoid(x.astype(jnp.float32) @ w_router.astype(jnp.float32))
    sel = scores + b_router.astype(jnp.float32)
    grp = sel.reshape(-1, N_EXPERT_GROUPS, EXPERTS_PER_GROUP)
    grp_score = jnp.sum(lax.top_k(grp, TOP_K)[0], axis=-1)
    gidx = jnp.argmax(grp_score, axis=-1)
    in_grp = (jnp.arange(N_EXPERTS) // EXPERTS_PER_GROUP)[None, :] == gidx[:, None]
    _, top_idx = lax.top_k(jnp.where(in_grp, sel, -jnp.inf), TOP_K)
    top_s = jnp.take_along_axis(scores, top_idx, axis=-1)
    gates = top_s / jnp.sum(top_s, axis=-1, keepdims=True)
    combine = jnp.sum(jax.nn.one_hot(top_idx, N_EXPERTS, dtype=jnp.float32) * gates[..., None], axis=1)
    hid = jax.nn.silu(jnp.einsum('nd,edf->nef', x, w_gate)) * jnp.einsum('nd,edf->nef', x, w_up)
    hid = hid * combine[..., None].astype(hid.dtype)
    out = jnp.einsum('nef,efd->nd', hid, w_down)
    return out.reshape(shp).astype(h.dtype)


def trunk_layer(x, mod, norm1, w_in, b_in, rpb, ml_norm, w_pool, pool_scale, w_out,
                norm2, w_router, b_router, w_gate, w_up, w_down, ctx_kv, ml_state):
    B, T, _ = x.shape
    sh1, sc1, g1, sh2, sc2, g2 = jnp.split(mod, 6, axis=-1)
    h = rms_norm(x, norm1) * (1 + sc1) + sh1
    proj = h @ w_in + b_in
    qa, ka, va, qb, kb, vb, ob, gts, pin = jnp.split(proj, _in_split_points(), axis=-1)
    qa = qa.reshape(B, T, NA_HEADS, NA_DIM)
    ka = ka.reshape(B, T, NA_HEADS, NA_DIM)
    va = va.reshape(B, T, NA_HEADS, NA_DIM)
    if ctx_kv is None:
        out_a = context_attention(qa, ka, va)
    else:
        out_a = neighborhood_attention(qa, ka, va, ctx_kv[0], ctx_kv[1], rpb)
    qb = qb.reshape(B, T, ML_HEADS, ML_DIM)
    kb = kb.reshape(B, T, ML_HEADS, ML_DIM)
    vb = vb.reshape(B, T, ML_HEADS, ML_DIM)
    hb, new_ml = mlstm_bidirectional(qb, kb, vb, gts.reshape(B, T, 4, ML_HEADS), ml_state)
    hb = hb * lax.rsqrt(jnp.mean(hb * hb, axis=-1, keepdims=True) + EPS) * ml_norm.astype(jnp.float32).reshape(ML_HEADS, ML_DIM)
    out_b = (jax.nn.sigmoid(ob.astype(jnp.float32)) * hb.reshape(B, T, ML_WIDTH)).astype(x.dtype)
    out_c = multiscale_pool(pin, w_pool, pool_scale)
    mixed = jnp.concatenate([out_a.reshape(B, T, NA_WIDTH), out_b, out_c], axis=-1) @ w_out
    x = x + g1 * mixed
    h2 = rms_norm(x, norm2) * (1 + sc2) + sh2
    x = x + g2 * grouped_moe(h2, w_router, b_router, w_gate, w_up, w_down)
    return x, (ka, va), new_ml


def setup_inputs(seed: int = 0) -> dict:
    key = jax.random.key(seed)
    ks = jax.random.split(key, 32)
    nrm = lambda k, shape, s: jax.random.normal(k, shape, jnp.float32) * s
    f_cols = np.concatenate([
        3 * NA_WIDTH + 4 * ML_WIDTH + ML_HEADS + np.arange(ML_HEADS),
        3 * NA_WIDTH + 4 * ML_WIDTH + 3 * ML_HEADS + np.arange(ML_HEADS)])
    b_in = nrm(ks[13], (DEPTH, IN_COLS), 0.02).at[:, f_cols].add(ML_FORGET_BIAS)
    return {
        'x_prompt': nrm(ks[0], (BATCH, SEQ, D_MODEL), 1.0),
        'x_sample': nrm(ks[1], (DEC_BATCH, DEC_SEQ, D_MODEL), 1.0),
        'cache_k_attn': nrm(ks[2], (DEC_BATCH, DEPTH, PAST_LEN, NA_HEADS, NA_DIM), 1.0),
        'cache_v_attn': nrm(ks[3], (DEC_BATCH, DEPTH, PAST_LEN, NA_HEADS, NA_DIM), 1.0),
        'state_mlstm_C': nrm(ks[4], (DEC_BATCH, DEPTH, 2, ML_HEADS, ML_DIM, ML_DIM), 1.0),
        'state_mlstm_n': nrm(ks[5], (DEC_BATCH, DEPTH, 2, ML_HEADS, ML_DIM), 1.0),
        'state_mlstm_m': nrm(ks[6], (DEC_BATCH, DEPTH, 2, ML_HEADS), 1.0),
        'c': nrm(ks[7], (DEC_BATCH, D_MODEL), 1.0),
        'c_ctx': nrm(ks[8], (D_MODEL,), 1.0),
        'w_ada': nrm(ks[9], (DEPTH, D_MODEL, ADA_DIM), 0.5 * D_MODEL ** -0.5),
        'b_ada': nrm(ks[10], (DEPTH, ADA_DIM), 0.02),
        'norm1': 1.0 + nrm(ks[11], (DEPTH, D_MODEL), 0.02),
        'w_in': nrm(ks[12], (DEPTH, D_MODEL, IN_COLS), D_MODEL ** -0.5),
        'b_in': b_in,
        'rpb': nrm(ks[14], (DEPTH, NA_HEADS, RPB_ROWS, RPB_COLS), 0.5),
        'ml_norm': 1.0 + nrm(ks[15], (DEPTH, ML_WIDTH), 0.02),
        'w_pool': nrm(ks[16], (DEPTH, POOL_GROUPS, POOL_DIM, POOL_DIM), POOL_DIM ** -0.5),
        'pool_scale': 1.0 + nrm(ks[17], (DEPTH, POOL_WIDTH), 0.1),
        'w_out': nrm(ks[18], (DEPTH, MIX_WIDTH, D_MODEL), MIX_WIDTH ** -0.5),
        'norm2': 1.0 + nrm(ks[19], (DEPTH, D_MODEL), 0.02),
        'w_router': nrm(ks[20], (D_MODEL, N_EXPERTS), D_MODEL ** -0.5),
        'b_router': nrm(ks[21], (N_EXPERTS,), 0.01),
        'w_gate': nrm(ks[22], (DEPTH, N_EXPERTS, D_MODEL, D_EXPERT), D_MODEL ** -0.5),
        'w_up': nrm(ks[23], (DEPTH, N_EXPERTS, D_MODEL, D_EXPERT), D_MODEL ** -0.5),
        'w_down': nrm(ks[24], (DEPTH, N_EXPERTS, D_EXPERT, D_MODEL), D_EXPERT ** -0.5),
        'final_norm': 1.0 + nrm(ks[25], (D_MODEL,), 0.02),
    }


def reference(x_prompt, x_sample, cache_k_attn, cache_v_attn, state_mlstm_C, state_mlstm_n, state_mlstm_m,
              c, c_ctx, w_ada, b_ada, norm1, w_in, b_in, rpb, ml_norm, w_pool, pool_scale, w_out, norm2,
              w_router, b_router, w_gate, w_up, w_down, final_norm):
    xc = x_prompt
    xs = x_sample
    bc = x_prompt.shape[0]
    zero_state = (jnp.zeros((bc, 2, ML_HEADS, ML_DIM, ML_DIM), jnp.float32),
                  jnp.zeros((bc, 2, ML_HEADS, ML_DIM), jnp.float32),
                  jnp.zeros((bc, 2, ML_HEADS), jnp.float32))
    ks_, vs_, Cs, ns, ms = [], [], [], [], []
    for l in range(DEPTH):
        mod_ctx = jax.nn.silu(c_ctx) @ w_ada[l] + b_ada[l]
        mod_lat = (jax.nn.silu(c) @ w_ada[l] + b_ada[l])[:, None, :]
        lw = (norm1[l], w_in[l], b_in[l], rpb[l], ml_norm[l], w_pool[l], pool_scale[l], w_out[l],
              norm2[l], w_router, b_router, w_gate[l], w_up[l], w_down[l])
        xc, (k_l, v_l), (C_l, n_l, m_l) = trunk_layer(xc, mod_ctx, *lw, None, zero_state)
        ks_.append(k_l)
        vs_.append(v_l)
        Cs.append(C_l)
        ns.append(n_l)
        ms.append(m_l)
        xs, _, _ = trunk_layer(xs, mod_lat, *lw, (cache_k_attn[:, l], cache_v_attn[:, l]),
                               (state_mlstm_C[:, l], state_mlstm_n[:, l], state_mlstm_m[:, l]))
    y_prompt = rms_norm(xc, final_norm)
    y_sample = rms_norm(xs, final_norm)
    dt = x_prompt.dtype
    new_cache_k = jnp.stack(ks_, axis=1).astype(dt)
    new_cache_v = jnp.stack(vs_, axis=1).astype(dt)
    new_state_C = jnp.stack(Cs, axis=1).astype(dt)
    new_state_n = jnp.stack(ns, axis=1).astype(dt)
    new_state_m = jnp.stack(ms, axis=1).astype(dt)
    return (y_prompt, y_sample, new_cache_k, new_cache_v, new_state_C, new_state_n, new_state_m)
```

```python
import functools

import numpy as np
import jax
import jax.numpy as jnp
from jax import lax
from jax.experimental import pallas as pl
from jax.experimental.pallas import tpu as pltpu

D_MODEL = 1024
BATCH = 16
SEQ = 256
DEPTH = 4
DEC_BATCH = 2
DEC_SEQ = 4096
PAST_LEN = 256
GRID_W = 64
EPS = 1e-6
NEG_INF = -1e30
NA_HEADS = 6
NA_DIM = 64
NA_WIDTH = NA_HEADS * NA_DIM
NA_ROWS = 8
NA_COLS = 16
RPB_ROWS = 2 * NA_ROWS - 1
RPB_COLS = 2 * NA_COLS - 1
ML_HEADS = 4
ML_DIM = 96
ML_WIDTH = ML_HEADS * ML_DIM
POOL_WINDOWS = (2, 4, 8, 16)
POOL_GROUPS = 4
POOL_DIM = 64
POOL_WIDTH = POOL_GROUPS * POOL_DIM
N_GATE_COLS = 4 * ML_HEADS
N_EXPERTS = 16
N_EXPERT_GROUPS = 4
EXPERTS_PER_GROUP = N_EXPERTS // N_EXPERT_GROUPS
D_EXPERT = 512
ADA_DIM = 6 * D_MODEL

N_CTX = BATCH * SEQ
N_LAT = DEC_BATCH * DEC_SEQ
N_TOK = N_CTX + N_LAT
LANE = 128
ML_PAD = LANE
ML_PW = ML_HEADS * ML_PAD
CAUG = 2 * ML_PAD
NA_PAIRS = NA_HEADS // 2
TOK_TILE = 512
ML_CHUNK = 128
NA_QROWS = 4
NA_KROWS = NA_QROWS + NA_ROWS - 1
POOL_TILE = 256
POOL_HALO = max(POOL_WINDOWS) // 2
MOE_TILE = 1024
VMEM_LIMIT = 56 * 1024 * 1024

W_A = 3 * NA_WIDTH
W_B = 4 * ML_PW
W_MAIN = W_A + W_B + POOL_WIDTH

f32 = jnp.float32
bf16 = jnp.bfloat16
HI = lax.Precision.HIGHEST


def _nt(a, b, **kw):
    return lax.dot_general(a, b, (((1,), (1,)), ((), ())), preferred_element_type=f32, **kw)


def _mod_row(i, tile):
    n_ctx_tiles = N_CTX // tile
    per_batch = DEC_SEQ // tile
    return jnp.where(i < n_ctx_tiles, 0, 1 + (i - n_ctx_tiles) // per_batch)


def _ada_kernel(c_ref, w_ref, b_ref, o_ref):
    s = c_ref[...]
    s = s * jax.nn.sigmoid(s)
    o_ref[0] = jnp.dot(s.astype(bf16), w_ref[0].astype(bf16), preferred_element_type=f32) + b_ref[0]


def _ada(cvec, w_ada, b_ada):
    nj = ADA_DIM // D_MODEL
    return pl.pallas_call(
        _ada_kernel,
        grid=(DEPTH, nj),
        in_specs=[pl.BlockSpec((8, D_MODEL), lambda l, j: (0, 0)),
                  pl.BlockSpec((1, D_MODEL, D_MODEL), lambda l, j: (l, 0, j)),
                  pl.BlockSpec((1, 1, D_MODEL), lambda l, j: (l, 0, j))],
        out_specs=pl.BlockSpec((1, 8, D_MODEL), lambda l, j: (l, 0, j)),
        out_shape=jax.ShapeDtypeStruct((DEPTH, 8, ADA_DIM), f32),
        name="ada_mod",
    )(cvec, w_ada, b_ada.reshape(DEPTH, 1, ADA_DIM))


def _in_kernel(x_ref, mod_ref, n1_ref, w_ref, b_ref, wg_ref, bg_ref, wgt_ref, bgt_ref,
               a_ref, kv_ref, b_out_ref, g_ref, gt_ref, pin_ref):
    x = x_ref[...]
    mod = mod_ref[0]
    h = x * lax.rsqrt(jnp.mean(x * x, axis=-1, keepdims=True) + EPS) * n1_ref[...]
    h = (h * (1.0 + mod[1:2]) + mod[0:1]).astype(bf16)
    pa = jnp.dot(h, w_ref[:, 0:W_A], preferred_element_type=f32) + b_ref[:, 0:W_A]
    a_ref[...] = pa.astype(bf16)
    kv_ref[...] = pa[:, NA_WIDTH:W_A]
    for j in range(4):
        lo = W_A + j * ML_PW
        pb = jnp.dot(h, w_ref[:, lo:lo + ML_PW], preferred_element_type=f32) + b_ref[:, lo:lo + ML_PW]
        if j == 0:
            pb = pb * (ML_DIM ** -0.5)
        b_out_ref[:, j * ML_PW:(j + 1) * ML_PW] = pb.astype(bf16)
    lo = W_A + W_B
    pin_ref[...] = jnp.dot(h, w_ref[:, lo:lo + POOL_WIDTH], preferred_element_type=f32) + b_ref[:, lo:lo + POOL_WIDTH]
    g_ref[...] = jnp.dot(h, wg_ref[...], preferred_element_type=f32) + bg_ref[...]
    gt_ref[...] = _nt(wgt_ref[...], h) + bgt_ref[...]


def _in_proj(x, mods, n1, w, b, wg, bg, wgt, bgt):
    tm = TOK_TILE
    const = lambda i: (0, 0)
    return pl.pallas_call(
        _in_kernel,
        grid=(N_TOK // tm,),
        in_specs=[pl.BlockSpec((tm, D_MODEL), lambda i: (i, 0)),
                  pl.BlockSpec((1, 6, D_MODEL), lambda i: (_mod_row(i, tm), 0, 0)),
                  pl.BlockSpec((1, D_MODEL), const),
                  pl.BlockSpec((D_MODEL, W_MAIN), const),
                  pl.BlockSpec((1, W_MAIN), const),
                  pl.BlockSpec((D_MODEL, LANE), const),
                  pl.BlockSpec((1, LANE), const),
                  pl.BlockSpec((N_GATE_COLS, D_MODEL), const),
                  pl.BlockSpec((N_GATE_COLS, 1), const)],
        out_specs=[pl.BlockSpec((tm, W_A), lambda i: (i, 0)),
                   pl.BlockSpec((tm, 2 * NA_WIDTH), lambda i: (i, 0)),
                   pl.BlockSpec((tm, W_B), lambda i: (i, 0)),
                   pl.BlockSpec((tm, LANE), lambda i: (i, 0)),
                   pl.BlockSpec((N_GATE_COLS, tm), lambda i: (0, i)),
                   pl.BlockSpec((tm, POOL_WIDTH), lambda i: (i, 0))],
        out_shape=[jax.ShapeDtypeStruct((N_TOK, W_A), bf16),
                   jax.ShapeDtypeStruct((N_TOK, 2 * NA_WIDTH), f32),
                   jax.ShapeDtypeStruct((N_TOK, W_B), bf16),
                   jax.ShapeDtypeStruct((N_TOK, LANE), f32),
                   jax.ShapeDtypeStruct((N_GATE_COLS, N_TOK), f32),
                   jax.ShapeDtypeStruct((N_TOK, POOL_WIDTH), f32)],
        compiler_params=pltpu.CompilerParams(dimension_semantics=("arbitrary",), vmem_limit_bytes=VMEM_LIMIT),
        name="in_proj",
    )(x, mods, n1, w, b, wg, bg, wgt, bgt)


def _pair_attention(qp, parts):
    lane = lax.broadcasted_iota(jnp.int32, (1, LANE), 1)
    outs = []
    for j in range(2):
        in_half = (lane >= j * NA_DIM) & (lane < (j + 1) * NA_DIM)
        qm = jnp.where(in_half, qp, jnp.zeros_like(qp))
        scores = []
        for k, _, bias in parts:
            s = _nt(qm, k)
            if bias is not None:
                s = s + bias[j]
            scores.append(s)
        m = scores[0].max(axis=-1, keepdims=True)
        for s in scores[1:]:
            m = jnp.maximum(m, s.max(axis=-1, keepdims=True))
        den = None
        acc = None
        for s, (_, v, _) in zip(scores, parts):
            p = jnp.exp(s - m)
            ps = p.sum(axis=-1, keepdims=True)
            den = ps if den is None else den + ps
            o = jnp.dot(p.astype(bf16), v, preferred_element_type=f32)
            acc = o if acc is None else acc + o
        outs.append(acc / den)
    return jnp.where(lane < NA_DIM, outs[0], outs[1])


def _ctx_attn_kernel(q_ref, k_ref, v_ref, o_ref):
    for p in range(NA_PAIRS):
        sl = slice(p * LANE, (p + 1) * LANE)
        o = _pair_attention(q_ref[0, :, sl], [(k_ref[0, :, sl], v_ref[0, :, sl], None)])
        o_ref[0, :, sl] = o.astype(bf16)


def _ctx_attention(qkv):
    blk = lambda c: pl.BlockSpec((1, SEQ, NA_WIDTH), lambda b, c=c: (b, 0, c))
    return pl.pallas_call(
        _ctx_attn_kernel,
        grid=(BATCH,),
        in_specs=[blk(0), blk(1), blk(2)],
        out_specs=pl.BlockSpec((1, SEQ, NA_WIDTH), lambda b: (b, 0, 0)),
        out_shape=jax.ShapeDtypeStruct((BATCH, SEQ, NA_WIDTH), bf16),
        name="ctx_attention",
    )(qkv, qkv, qkv)


def _na_window_start(rb):
    return jnp.clip(rb * NA_QROWS - NA_ROWS // 2, 0, DEC_SEQ // GRID_W - NA_KROWS)


def _na_kernel(q_ref, k_ref, v_ref, ck_ref, cv_ref, bias_ref, o_ref):
    rb = pl.program_id(1)
    start = pl.multiple_of(_na_window_start(rb) * GRID_W, GRID_W)
    nk = NA_KROWS * GRID_W
    for p in range(NA_PAIRS):
        sl = slice(p * LANE, (p + 1) * LANE)
        parts = [(k_ref[0, pl.ds(start, nk), sl], v_ref[0, pl.ds(start, nk), sl], bias_ref[0, 2 * p:2 * p + 2]),
                 (ck_ref[0, :, sl], cv_ref[0, :, sl], None)]
        o = _pair_attention(q_ref[0, :, sl], parts)
        o_ref[0, :, sl] = o.astype(bf16)


def _na_bias_variant(rb):
    n_rb = DEC_SEQ // GRID_W // NA_QROWS
    return jnp.where(rb == 0, 0, jnp.where(rb == n_rb - 1, 2, 1))


def _neighborhood_attention(qkv, ck, cv, bias):
    nq = NA_QROWS * GRID_W
    nk = NA_KROWS * GRID_W
    n_rb = DEC_SEQ // nq
    return pl.pallas_call(
        _na_kernel,
        grid=(DEC_BATCH, n_rb),
        in_specs=[pl.BlockSpec((1, nq, NA_WIDTH), lambda b, r: (1 + b, r, 0)),
                  pl.BlockSpec((1, DEC_SEQ, NA_WIDTH), lambda b, r: (1 + b, 0, 1)),
                  pl.BlockSpec((1, DEC_SEQ, NA_WIDTH), lambda b, r: (1 + b, 0, 2)),
                  pl.BlockSpec((1, PAST_LEN, NA_WIDTH), lambda b, r: (b, 0, 0)),
                  pl.BlockSpec((1, PAST_LEN, NA_WIDTH), lambda b, r: (b, 0, 0)),
                  pl.BlockSpec((1, NA_HEADS, nq, nk), lambda b, r: (_na_bias_variant(r), 0, 0, 0))],
        out_specs=pl.BlockSpec((1, nq, NA_WIDTH), lambda b, r: (b, r, 0)),
        out_shape=jax.ShapeDtypeStruct((DEC_BATCH, DEC_SEQ, NA_WIDTH), bf16),
        compiler_params=pltpu.CompilerParams(dimension_semantics=("arbitrary", "arbitrary"),
                                             vmem_limit_bytes=VMEM_LIMIT),
        name="neighborhood_attention",
    )(qkv, qkv, qkv, ck, cv, bias)


def _na_bias_tables(rpb_l):
    rows = DEC_SEQ // GRID_W
    n_rb = rows // NA_QROWS
    tables = []
    for rb in (0, 1, n_rb - 1):
        r0 = rb * NA_QROWS
        ws = int(np.clip(r0 - NA_ROWS // 2, 0, rows - NA_KROWS))
        qr = (r0 + np.arange(NA_QROWS))[:, None, None, None]
        qc = np.arange(GRID_W)[None, :, None, None]
        kr = (ws + np.arange(NA_KROWS))[None, None, :, None]
        kc = np.arange(GRID_W)[None, None, None, :]
        row_start = np.clip(qr - NA_ROWS // 2, 0, rows - NA_ROWS)
        col_start = np.clip(qc - NA_COLS // 2, 0, GRID_W - NA_COLS)
        valid = (kr >= row_start) & (kr < row_start + NA_ROWS) & (kc >= col_start) & (kc < col_start + NA_COLS)
        dr = np.clip(kr - qr + NA_ROWS - 1, 0, RPB_ROWS - 1)
        dc = np.clip(kc - qc + NA_COLS - 1, 0, RPB_COLS - 1)
        shape = (NA_QROWS, GRID_W, NA_KROWS, GRID_W)
        dr = np.broadcast_to(dr, shape).reshape(NA_QROWS * GRID_W, NA_KROWS * GRID_W)
        dc = np.broadcast_to(dc, shape).reshape(NA_QROWS * GRID_W, NA_KROWS * GRID_W)
        valid = np.broadcast_to(valid, shape).reshape(NA_QROWS * GRID_W, NA_KROWS * GRID_W)
        t = rpb_l.astype(f32)[:, dr, dc]
        tables.append(jnp.where(valid[None], t, NEG_INF))
    return jnp.stack(tables, axis=0)


def _log_sigmoid(x):
    return -(jnp.maximum(-x, 0.0) + jnp.log(1.0 + jnp.exp(-jnp.abs(x))))


def _mlstm_kernel(qf_ref, kf_ref, vf_ref, gf_ref, gtf_ref, qb_ref, kb_ref, vb_ref, gb_ref, gtb_ref,
                  c0_ref, m0_ref, hf_ref, hb_ref, c_out_ref, m_out_ref, c_scr, m_scr, *, n_chunks):
    L = ML_CHUNK
    c = pl.program_id(1)

    @pl.when(c == 0)
    def _():
        c_scr[...] = c0_ref[0]
        m_scr[...] = m0_ref[0]

    ri = lax.broadcasted_iota(jnp.int32, (L, L), 0)
    ci = lax.broadcasted_iota(jnp.int32, (L, L), 1)
    lane = lax.broadcasted_iota(jnp.int32, (L, ML_PAD), 1)
    ones_col = jnp.where(lane == 0, 1.0, 0.0).astype(bf16)
    dirs = ((qf_ref, kf_ref, vf_ref, gf_ref, gtf_ref, hf_ref, ri >= ci),
            (qb_ref, kb_ref, vb_ref, gb_ref, gtb_ref, hb_ref, ri <= ci))
    for d, (q_ref, k_ref, v_ref, g_ref, gt_ref, h_ref, visible) in enumerate(dirs):
        g = g_ref[...][:, 0:N_GATE_COLS]
        gt = gt_ref[...]
        vis = visible.astype(f32)
        b_cols = jnp.dot(vis, _log_sigmoid(g), preferred_element_type=f32, precision=HI)
        b_rows = _nt(_log_sigmoid(gt), vis, precision=HI)
        tot = jnp.sum(_log_sigmoid(g), axis=0, keepdims=True)
        for hd in range(ML_HEADS):
            st = d * ML_HEADS + hd
            ci_ = 2 * ML_HEADS * d + hd
            cf_ = ci_ + ML_HEADS
            sl = slice(hd * ML_PAD, (hd + 1) * ML_PAD)
            bc = b_cols[:, cf_:cf_ + 1]
            br = b_rows[cf_:cf_ + 1, :]
            li_r = gt[ci_:ci_ + 1, :]
            li_c = g[:, ci_:ci_ + 1]
            m_prev = m_scr[st:st + 1, 0:1]
            dmat = jnp.where(visible, bc - br + li_r, NEG_INF)
            inter = bc + m_prev
            m_t = jnp.maximum(inter, dmat.max(axis=-1, keepdims=True))
            w_intra = jnp.exp(dmat - m_t)
            w_inter = jnp.exp(inter - m_t)
            qh = q_ref[0, :, sl]
            kh = k_ref[0, :, sl]
            v_aug = jnp.concatenate([v_ref[0, :, sl], ones_col], axis=1)
            s = (_nt(qh, kh) * w_intra).astype(bf16)
            c_aug = c_scr[st]
            na = (w_inter * jnp.dot(qh, c_aug.astype(bf16), preferred_element_type=f32)
                  + jnp.dot(s, v_aug, preferred_element_type=f32))
            den = na[:, ML_PAD:ML_PAD + 1]
            h_ref[0, :, sl] = na[:, 0:ML_PAD] / jnp.maximum(jnp.abs(den), jnp.exp(-m_t))
            b_end = tot[:, cf_:cf_ + 1]
            g_col = b_end - bc + li_c
            m_new = jnp.maximum(b_end + m_prev, g_col.max(axis=0, keepdims=True))
            decay = jnp.exp(b_end + m_prev - m_new)
            kw = (kh.astype(f32) * jnp.exp(g_col - m_new)).astype(bf16)
            upd = lax.dot_general(kw, v_aug, (((0,), (0,)), ((), ())), preferred_element_type=f32)
            c_scr[st] = decay * c_aug + upd
            m_scr[st:st + 1, :] = jnp.broadcast_to(m_new, (1, LANE))

    @pl.when(c == n_chunks - 1)
    def _():
        c_out_ref[0] = c_scr[...]
        m_out_ref[0] = m_scr[...]


def _mlstm(qkvo, gates, gates_t, c0, m0, *, seq_len, first_seq, n_seq):
    L = ML_CHUNK
    nc = seq_len // L
    per = seq_len // L
    fwd = lambda b, c: (first_seq + b) * per + c
    bwd = lambda b, c: (first_seq + b) * per + (nc - 1 - c)

    def specs(pos):
        return [pl.BlockSpec((1, L, ML_PW), lambda b, c, j=j: (pos(b, c), 0, j)) for j in range(3)] + [
            pl.BlockSpec((L, LANE), lambda b, c: (pos(b, c), 0)),
            pl.BlockSpec((N_GATE_COLS, L), lambda b, c: (0, pos(b, c)))]

    q3 = qkvo.reshape(N_TOK // L, L, W_B)
    n_str = 2 * ML_HEADS
    return pl.pallas_call(
        functools.partial(_mlstm_kernel, n_chunks=nc),
        grid=(n_seq, nc),
        in_specs=specs(fwd) + specs(bwd) + [
            pl.BlockSpec((1, n_str, ML_PAD, CAUG), lambda b, c: (b, 0, 0, 0)),
            pl.BlockSpec((1, n_str, LANE), lambda b, c: (b, 0, 0))],
        out_specs=[pl.BlockSpec((1, L, ML_PW), lambda b, c: (b * per + c, 0, 0)),
                   pl.BlockSpec((1, L, ML_PW), lambda b, c: (b * per + (nc - 1 - c), 0, 0)),
                   pl.BlockSpec((1, n_str, ML_PAD, CAUG), lambda b, c: (b, 0, 0, 0)),
                   pl.BlockSpec((1, n_str, LANE), lambda b, c: (b, 0, 0))],
        out_shape=[jax.ShapeDtypeStruct((n_seq * nc, L, ML_PW), f32),
                   jax.ShapeDtypeStruct((n_seq * nc, L, ML_PW), f32),
                   jax.ShapeDtypeStruct((n_seq, n_str, ML_PAD, CAUG), f32),
                   jax.ShapeDtypeStruct((n_seq, n_str, LANE), f32)],
        scratch_shapes=[pltpu.VMEM((n_str, ML_PAD, CAUG), f32), pltpu.VMEM((n_str, LANE), f32)],
        compiler_params=pltpu.CompilerParams(dimension_semantics=("arbitrary", "arbitrary"),
                                             vmem_limit_bytes=VMEM_LIMIT),
        name="mlstm",
    )(q3, q3, q3, gates, gates_t, q3, q3, q3, gates, gates_t, c0, m0)


def _pack_ml_state(C, n, m):
    B = C.shape[0]
    c_aug = jnp.zeros((B, 2, ML_HEADS, ML_PAD, CAUG), f32)
    c_aug = c_aug.at[:, :, :, :ML_DIM, :ML_DIM].set(C.astype(f32))
    c_aug = c_aug.at[:, :, :, :ML_DIM, ML_PAD].set(n.astype(f32))
    m_b = jnp.broadcast_to(m.astype(f32)[..., None], (B, 2, ML_HEADS, LANE))
    return c_aug.reshape(B, 2 * ML_HEADS, ML_PAD, CAUG), m_b.reshape(B, 2 * ML_HEADS, LANE)


def _unpack_ml_state(c_aug, m_b):
    B = c_aug.shape[0]
    c_aug = c_aug.reshape(B, 2, ML_HEADS, ML_PAD, CAUG)
    return (c_aug[:, :, :, :ML_DIM, :ML_DIM], c_aug[:, :, :, :ML_DIM, ML_PAD],
            m_b.reshape(B, 2, ML_HEADS, LANE)[..., 0])


def _pool_kernel(u_ref, w_ref, sc_ref, o_ref, *, seq_len, win_rows):
    tt = POOL_TILE
    t0 = pl.program_id(1) * tt
    ws = pl.multiple_of(jnp.clip(t0 - POOL_HALO, 0, seq_len - win_rows), 8)
    u_win = u_ref[0, pl.ds(ws, win_rows), :]
    u_hi = u_win.astype(bf16)
    u_lo = (u_win - u_hi.astype(f32)).astype(bf16)
    u_tile = u_ref[0, pl.ds(pl.multiple_of(t0, tt), tt), :]
    t_abs = t0 + lax.broadcasted_iota(jnp.int32, (tt, 1), 0)
    s_abs = ws + lax.broadcasted_iota(jnp.int32, (1, win_rows), 1)
    lane = lax.broadcasted_iota(jnp.int32, (1, LANE), 1)
    sums, inv_cnt = [], []
    for w in POOL_WINDOWS:
        lo = jnp.maximum(t_abs - w // 2, 0)
        hi = jnp.minimum(t_abs - w // 2 + w, seq_len)
        sums.append(jnp.where((s_abs >= lo) & (s_abs < hi), 1.0, 0.0).astype(bf16))
        inv_cnt.append(1.0 / (hi - lo).astype(f32))
    pooled = []
    for p in range(POOL_GROUPS // 2):
        sl = slice(p * LANE, (p + 1) * LANE)
        means = []
        for j in range(2):
            a = sums[2 * p + j]
            tot = (jnp.dot(a, u_hi[:, sl], preferred_element_type=f32)
                   + jnp.dot(a, u_lo[:, sl], preferred_element_type=f32))
            means.append(tot * inv_cnt[2 * p + j])
        pooled.append(jnp.where(lane < POOL_DIM, means[0], means[1]) - u_tile[:, sl])
    pooled = jnp.concatenate(pooled, axis=1).astype(bf16)
    y = jnp.dot(pooled, w_ref[...], preferred_element_type=f32) * sc_ref[...]
    o_ref[0] = y.astype(bf16)


def _pool(pin, w_bd, scale, *, seq_len, first_seq, n_seq):
    win_rows = min(seq_len, POOL_TILE + 2 * POOL_HALO)
    return pl.pallas_call(
        functools.partial(_pool_kernel, seq_len=seq_len, win_rows=win_rows),
        grid=(n_seq, seq_len // POOL_TILE),
        in_specs=[pl.BlockSpec((1, seq_len, POOL_WIDTH), lambda b, t: (first_seq + b, 0, 0)),
                  pl.BlockSpec((POOL_WIDTH, POOL_WIDTH), lambda b, t: (0, 0)),
                  pl.BlockSpec((1, POOL_WIDTH), lambda b, t: (0, 0))],
        out_specs=pl.BlockSpec((1, POOL_TILE, POOL_WIDTH), lambda b, t: (b * (seq_len // POOL_TILE) + t, 0, 0)),
        out_shape=jax.ShapeDtypeStruct((n_seq * seq_len // POOL_TILE, POOL_TILE, POOL_WIDTH), bf16),
        name="pool",
    )(pin, w_bd, scale)


def _top2_sum(a, b, c, d):
    hi1, lo1 = jnp.maximum(a, b), jnp.minimum(a, b)
    hi2, lo2 = jnp.maximum(c, d), jnp.minimum(c, d)
    return jnp.maximum(hi1, hi2) + jnp.maximum(jnp.minimum(hi1, hi2), jnp.maximum(lo1, lo2))


def _first_match(vals, target):
    idx = jnp.full_like(target, float(len(vals) - 1))
    for i in range(len(vals) - 2, -1, -1):
        idx = jnp.where(vals[i] == target, float(i), idx)
    return idx


def _pick(vals, idx):
    out = vals[-1]
    for i in range(len(vals) - 2, -1, -1):
        out = jnp.where(idx == float(i), vals[i], out)
    return out


def _route(logits_t, bias_t):
    scores = jax.nn.sigmoid(logits_t)
    sel = scores + bias_t
    row = lambda a, i: a[i:i + 1, :]
    grp = [_top2_sum(*[row(sel, EXPERTS_PER_GROUP * g + i) for i in range(EXPERTS_PER_GROUP)])
           for g in range(N_EXPERT_GROUPS)]
    best = functools.reduce(jnp.maximum, grp)
    gidx = _first_match(grp, best)
    sel_g = [_pick([row(sel, EXPERTS_PER_GROUP * g + i) for g in range(N_EXPERT_GROUPS)], gidx)
             for i in range(EXPERTS_PER_GROUP)]
    sco_g = [_pick([row(scores, EXPERTS_PER_GROUP * g + i) for g in range(N_EXPERT_GROUPS)], gidx)
             for i in range(EXPERTS_PER_GROUP)]
    i0 = _first_match(sel_g, functools.reduce(jnp.maximum, sel_g))
    rest = [jnp.where(i0 == float(i), -jnp.inf, sel_g[i]) for i in range(EXPERTS_PER_GROUP)]
    i1 = _first_match(rest, functools.reduce(jnp.maximum, rest))
    s0, s1 = _pick(sco_g, i0), _pick(sco_g, i1)
    tot = s0 + s1
    rid = lax.broadcasted_iota(jnp.int32, (LANE, logits_t.shape[1]), 0)
    rows = (EXPERTS_PER_GROUP * gidx + i0, EXPERTS_PER_GROUP * gidx + i1, s0 / tot, s1 / tot)
    out = jnp.zeros(rid.shape, f32)
    for i, r in enumerate(rows):
        out = jnp.where(rid == i, r, out)
    return out


def _out_kernel(x_ref, mod_ref, oa_ref, hf_ref, hb_ref, ob_ref, oc_ref, mln_ref, wo_ref, n2_ref, wr_ref, br_ref,
                x1_ref, h2_ref, rt_ref, rc_ref):
    tm = x_ref.shape[0]
    mod = mod_ref[0]
    hsum = hf_ref[...] + hb_ref[...]
    outs_b = []
    for hd in range(ML_HEADS):
        sl = slice(hd * ML_PAD, (hd + 1) * ML_PAD)
        hh = hsum[:, sl]
        ms = jnp.sum(hh * hh, axis=-1, keepdims=True) * (1.0 / ML_DIM)
        hn = hh * lax.rsqrt(ms + EPS) * mln_ref[:, sl]
        outs_b.append((jax.nn.sigmoid(ob_ref[:, sl].astype(f32)) * hn).astype(bf16))
    out_b = jnp.concatenate(outs_b, axis=1)
    mixed = (jnp.dot(oa_ref[...], wo_ref[0:NA_WIDTH, :], preferred_element_type=f32)
             + jnp.dot(out_b, wo_ref[NA_WIDTH:NA_WIDTH + ML_PW, :], preferred_element_type=f32)
             + jnp.dot(oc_ref[...], wo_ref[NA_WIDTH + ML_PW:, :], preferred_element_type=f32))
    x1 = x_ref[...] + mod[2:3] * mixed
    x1_ref[...] = x1
    h2 = x1 * lax.rsqrt(jnp.mean(x1 * x1, axis=-1, keepdims=True) + EPS) * n2_ref[...]
    h2 = h2 * (1.0 + mod[4:5]) + mod[3:4]
    h2_ref[...] = h2.astype(bf16)
    route_t = _route(_nt(wr_ref[...], h2, precision=HI), br_ref[...])
    rt_ref[...] = route_t[0:8]
    rc_ref[...] = route_t.T


def _out_proj(x, mods, oa, hf, hb, qkvo, oc, mln, wo, n2, wr_t, br_t):
    tm = TOK_TILE
    const = lambda i: (0, 0)
    row = lambda i: (i, 0)
    return pl.pallas_call(
        _out_kernel,
        grid=(N_TOK // tm,),
        in_specs=[pl.BlockSpec((tm, D_MODEL), row),
                  pl.BlockSpec((1, 6, D_MODEL), lambda i: (_mod_row(i, tm), 0, 0)),
                  pl.BlockSpec((tm, NA_WIDTH), row),
                  pl.BlockSpec((tm, ML_PW), row),
                  pl.BlockSpec((tm, ML_PW), row),
                  pl.BlockSpec((tm, ML_PW), lambda i: (i, 3)),
                  pl.BlockSpec((tm, POOL_WIDTH), row),
                  pl.BlockSpec((1, ML_PW), const),
                  pl.BlockSpec((NA_WIDTH + ML_PW + POOL_WIDTH, D_MODEL), const),
                  pl.BlockSpec((1, D_MODEL), const),
                  pl.BlockSpec((N_EXPERTS, D_MODEL), const),
                  pl.BlockSpec((N_EXPERTS, 1), const)],
        out_specs=[pl.BlockSpec((tm, D_MODEL), row),
                   pl.BlockSpec((tm, D_MODEL), row),
                   pl.BlockSpec((8, tm), lambda i: (0, i)),
                   pl.BlockSpec((tm, LANE), row)],
        out_shape=[jax.ShapeDtypeStruct((N_TOK, D_MODEL), f32),
                   jax.ShapeDtypeStruct((N_TOK, D_MODEL), bf16),
                   jax.ShapeDtypeStruct((8, N_TOK), f32),
                   jax.ShapeDtypeStruct((N_TOK, LANE), f32)],
        compiler_params=pltpu.CompilerParams(dimension_semantics=("arbitrary",), vmem_limit_bytes=VMEM_LIMIT),
        name="out_proj_router",
    )(x, mods, oa, hf, hb, qkvo, oc, mln, wo, n2, wr_t, br_t)


def _moe_kernel(h_ref, rc_ref, wg_ref, wu_ref, wd_ref, x1_ref, mod_ref, fn_ref, o_ref, acc_ref, *, final):
    e = pl.program_id(1)

    @pl.when(e == 0)
    def _():
        acc_ref[...] = jnp.zeros_like(acc_ref)

    h = h_ref[...]
    ef = e.astype(f32)
    rc = rc_ref[...]
    comb = jnp.where(rc[:, 0:1] == ef, rc[:, 2:3], 0.0) + jnp.where(rc[:, 1:2] == ef, rc[:, 3:4], 0.0)
    hg = jnp.dot(h, wg_ref[0], preferred_element_type=f32)
    hu = jnp.dot(h, wu_ref[0], preferred_element_type=f32)
    hid = hg * jax.nn.sigmoid(hg) * hu * comb
    acc_ref[...] += jnp.dot(hid.astype(bf16), wd_ref[0], preferred_element_type=f32)

    @pl.when(e == N_EXPERTS - 1)
    def _():
        x2 = x1_ref[...] + mod_ref[0][5:6] * acc_ref[...]
        if final:
            x2 = x2 * lax.rsqrt(jnp.mean(x2 * x2, axis=-1, keepdims=True) + EPS) * fn_ref[...]
        o_ref[...] = x2


def _moe(h2, rc, wg, wu, wd, x1, mods, fn, *, final):
    tm = MOE_TILE
    row = lambda i, e: (i, 0)
    return pl.pallas_call(
        functools.partial(_moe_kernel, final=final),
        grid=(N_TOK // tm, N_EXPERTS),
        in_specs=[pl.BlockSpec((tm, D_MODEL), row),
                  pl.BlockSpec((tm, LANE), row),
                  pl.BlockSpec((1, D_MODEL, D_EXPERT), lambda i, e: (e, 0, 0)),
                  pl.BlockSpec((1, D_MODEL, D_EXPERT), lambda i, e: (e, 0, 0)),
                  pl.BlockSpec((1, D_EXPERT, D_MODEL), lambda i, e: (e, 0, 0)),
                  pl.BlockSpec((tm, D_MODEL), row),
                  pl.BlockSpec((1, 6, D_MODEL), lambda i, e: (_mod_row(i, tm), 0, 0)),
                  pl.BlockSpec((1, D_MODEL), lambda i, e: (0, 0))],
        out_specs=pl.BlockSpec((tm, D_MODEL), row),
        out_shape=jax.ShapeDtypeStruct((N_TOK, D_MODEL), f32),
        scratch_shapes=[pltpu.VMEM((tm, D_MODEL), f32)],
        compiler_params=pltpu.CompilerParams(dimension_semantics=("arbitrary", "arbitrary"),
                                             vmem_limit_bytes=VMEM_LIMIT),
        name="moe",
    )(h2, rc, wg, wu, wd, x1, mods, fn)


def _pad_heads(w):
    lead = w.shape[:-1]
    w = w.reshape(*lead, ML_HEADS, ML_DIM)
    w = jnp.pad(w, [(0, 0)] * len(lead) + [(0, 0), (0, ML_PAD - ML_DIM)])
    return w.reshape(*lead, ML_PW)


def _pack_w_in(w, b):
    wb = jnp.concatenate([w, b[None]], axis=0)
    o = 0
    qa = wb[:, o:o + NA_WIDTH] * (NA_DIM ** -0.5)
    ka = wb[:, o + NA_WIDTH:o + 2 * NA_WIDTH]
    va = wb[:, o + 2 * NA_WIDTH:o + 3 * NA_WIDTH]
    o += 3 * NA_WIDTH
    segs_b = [_pad_heads(wb[:, o + j * ML_WIDTH:o + (j + 1) * ML_WIDTH]) for j in range(4)]
    o += 4 * ML_WIDTH
    gates = wb[:, o:o + N_GATE_COLS]
    o += N_GATE_COLS
    pool = wb[:, o:o + POOL_WIDTH]
    main = jnp.concatenate([qa, ka, va] + segs_b + [pool], axis=1)
    gates_p = jnp.pad(gates, ((0, 0), (0, LANE - N_GATE_COLS)))
    return (main[:-1].astype(bf16), main[-1:].astype(f32),
            gates_p[:-1].astype(bf16), gates_p[-1:].astype(f32),
            gates[:-1].T.astype(bf16), gates[-1][:, None].astype(f32))


def _pack_w_out(w):
    wb = w[NA_WIDTH:NA_WIDTH + ML_WIDTH].reshape(ML_HEADS, ML_DIM, D_MODEL)
    wb = jnp.pad(wb, ((0, 0), (0, ML_PAD - ML_DIM), (0, 0))).reshape(ML_PW, D_MODEL)
    return jnp.concatenate([w[:NA_WIDTH], wb, w[NA_WIDTH + ML_WIDTH:]], axis=0).astype(bf16)


def _block_diag(w):
    g, c, _ = w.shape
    eye = jnp.eye(g, dtype=w.dtype)
    return (eye[:, None, :, None] * w[:, :, None, :]).reshape(g * c, g * c)


def kernel(x_prompt, x_sample, cache_k_attn, cache_v_attn, state_mlstm_C, state_mlstm_n, state_mlstm_m, c, c_ctx,
           w_ada, b_ada, norm1, w_in, b_in, rpb, ml_norm, w_pool, pool_scale, w_out, norm2, w_router, b_router,
           w_gate, w_up, w_down, final_norm):
    dt = x_prompt.dtype
    x = jnp.concatenate([x_prompt.reshape(N_CTX, D_MODEL), x_sample.reshape(N_LAT, D_MODEL)], axis=0).astype(f32)
    cvec = jnp.concatenate([c_ctx[None], c, jnp.zeros((8 - 1 - DEC_BATCH, D_MODEL), c.dtype)], axis=0).astype(f32)
    mods_all = _ada(cvec, w_ada.astype(f32), b_ada.astype(f32))
    mods_all = mods_all[:, :1 + DEC_BATCH].reshape(DEPTH, 1 + DEC_BATCH, 6, D_MODEL)

    wr_t = w_router.astype(f32).T
    br_t = b_router.astype(f32)[:, None]
    fn = final_norm.astype(f32)[None]
    zero_c = jnp.zeros((BATCH, 2 * ML_HEADS, ML_PAD, CAUG), f32)
    zero_m = jnp.zeros((BATCH, 2 * ML_HEADS, LANE), f32)

    ks, vs, Cs, ns, ms = [], [], [], [], []
    for l in range(DEPTH):
        mods = mods_all[l]
        w_main, b_main, wg, bg, wgt, bgt = _pack_w_in(w_in[l], b_in[l])
        qkva, kv32, qkvo, gates, gates_t, pin = _in_proj(x, mods, norm1[l].astype(f32)[None], w_main, b_main,
                                                         wg, bg, wgt, bgt)
        ks.append(kv32[:N_CTX, :NA_WIDTH].reshape(BATCH, SEQ, NA_HEADS, NA_DIM))
        vs.append(kv32[:N_CTX, NA_WIDTH:].reshape(BATCH, SEQ, NA_HEADS, NA_DIM))

        oa_ctx = _ctx_attention(qkva.reshape(N_TOK // SEQ, SEQ, W_A))
        ck = (cache_k_attn[:, l].reshape(DEC_BATCH, PAST_LEN, NA_WIDTH)).astype(bf16)
        cv = (cache_v_attn[:, l].reshape(DEC_BATCH, PAST_LEN, NA_WIDTH)).astype(bf16)
        oa_lat = _neighborhood_attention(qkva.reshape(N_TOK // DEC_SEQ, DEC_SEQ, W_A), ck, cv,
                                         _na_bias_tables(rpb[l]))
        oa = jnp.concatenate([oa_ctx.reshape(N_CTX, NA_WIDTH), oa_lat.reshape(N_LAT, NA_WIDTH)], axis=0)

        hf_c, hb_c, c_fin, m_fin = _mlstm(qkvo, gates, gates_t, zero_c, zero_m,
                                          seq_len=SEQ, first_seq=0, n_seq=BATCH)
        c_l, m_l = _pack_ml_state(state_mlstm_C[:, l], state_mlstm_n[:, l], state_mlstm_m[:, l])
        hf_l, hb_l, _, _ = _mlstm(qkvo, gates, gates_t, c_l, m_l,
                                  seq_len=DEC_SEQ, first_seq=N_CTX // DEC_SEQ, n_seq=DEC_BATCH)
        hf = jnp.concatenate([hf_c.reshape(N_CTX, ML_PW), hf_l.reshape(N_LAT, ML_PW)], axis=0)
        hb = jnp.concatenate([hb_c.reshape(N_CTX, ML_PW), hb_l.reshape(N_LAT, ML_PW)], axis=0)
        C_l, n_l, m_l2 = _unpack_ml_state(c_fin, m_fin)
        Cs.append(C_l)
        ns.append(n_l)
        ms.append(m_l2)

        w_bd = _block_diag(w_pool[l].astype(f32)).astype(bf16)
        psc = pool_scale[l].astype(f32)[None]
        oc_ctx = _pool(pin.reshape(N_TOK // SEQ, SEQ, POOL_WIDTH), w_bd, psc, seq_len=SEQ, first_seq=0, n_seq=BATCH)
        oc_lat = _pool(pin.reshape(N_TOK // DEC_SEQ, DEC_SEQ, POOL_WIDTH), w_bd, psc,
                       seq_len=DEC_SEQ, first_seq=N_CTX // DEC_SEQ, n_seq=DEC_BATCH)
        oc = jnp.concatenate([oc_ctx.reshape(N_CTX, POOL_WIDTH), oc_lat.reshape(N_LAT, POOL_WIDTH)], axis=0)

        x1, h2, _, rc = _out_proj(x, mods, oa, hf, hb, qkvo, oc, _pad_heads(ml_norm[l].astype(f32))[None],
                                  _pack_w_out(w_out[l]), norm2[l].astype(f32)[None], wr_t, br_t)
        x = _moe(h2, rc, w_gate[l].astype(bf16), w_up[l].astype(bf16), w_down[l].astype(bf16), x1, mods, fn,
                 final=(l == DEPTH - 1))

    y_prompt = x[:N_CTX].reshape(BATCH, SEQ, D_MODEL).astype(dt)
    y_sample = x[N_CTX:].reshape(DEC_BATCH, DEC_SEQ, D_MODEL).astype(dt)
    return (y_prompt, y_sample,
            jnp.stack(ks, axis=1).astype(dt), jnp.stack(vs, axis=1).astype(dt),
            jnp.stack(Cs, axis=1).astype(dt), jnp.stack(ns, axis=1).astype(dt), jnp.stack(ms, axis=1).astype(dt))
```

```python
import functools

import numpy as np
import jax
import jax.numpy as jnp
from jax import lax
from jax.experimental import pallas as pl
from jax.experimental.pallas import tpu as pltpu

D_MODEL = 1024
BATCH = 16
SEQ = 256
DEPTH = 4
DEC_BATCH = 2
DEC_SEQ = 4096
PAST_LEN = 256
GRID_W = 64
EPS = 1e-6
NEG_INF = -1e30
NA_HEADS = 6
NA_DIM = 64
NA_WIDTH = NA_HEADS * NA_DIM
NA_ROWS = 8
NA_COLS = 16
RPB_ROWS = 2 * NA_ROWS - 1
RPB_COLS = 2 * NA_COLS - 1
ML_HEADS = 4
ML_DIM = 96
ML_WIDTH = ML_HEADS * ML_DIM
POOL_WINDOWS = (2, 4, 8, 16)
POOL_GROUPS = 4
POOL_DIM = 64
POOL_WIDTH = POOL_GROUPS * POOL_DIM
N_GATE_COLS = 4 * ML_HEADS
N_EXPERTS = 16
N_EXPERT_GROUPS = 4
EXPERTS_PER_GROUP = N_EXPERTS // N_EXPERT_GROUPS
D_EXPERT = 512
ADA_DIM = 6 * D_MODEL

N_CTX = BATCH * SEQ
N_LAT = DEC_BATCH * DEC_SEQ
N_TOK = N_CTX + N_LAT
LANE = 128
ML_PAD = LANE
ML_PW = ML_HEADS * ML_PAD
CAUG = 2 * ML_PAD
NA_PAIRS = NA_HEADS // 2
TOK_TILE = 512
ML_CHUNK = 128
NA_QROWS = 4
NA_KROWS = NA_QROWS + NA_ROWS - 1
POOL_TILE = 256
POOL_HALO = max(POOL_WINDOWS) // 2
MOE_TILE = 512
MOE_ROWS = 2 * N_TOK + N_EXPERTS * MOE_TILE
DISPATCH_TILE = 256
COMBINE_TILE = 256
VMEM_LIMIT = 56 * 1024 * 1024

W_A = 3 * NA_WIDTH
W_B = 4 * ML_PW
W_MAIN = W_A + W_B + POOL_WIDTH

f32 = jnp.float32
bf16 = jnp.bfloat16
HI = lax.Precision.HIGHEST


def _nt(a, b, **kw):
    return lax.dot_general(a, b, (((1,), (1,)), ((), ())), preferred_element_type=f32, **kw)


def _mod_row(i, tile):
    n_ctx_tiles = N_CTX // tile
    per_batch = DEC_SEQ // tile
    return jnp.where(i < n_ctx_tiles, 0, 1 + (i - n_ctx_tiles) // per_batch)


def _ada_kernel(c_ref, w_ref, b_ref, o_ref):
    s = c_ref[...]
    s = s * jax.nn.sigmoid(s)
    o_ref[0] = jnp.dot(s.astype(bf16), w_ref[0].astype(bf16), preferred_element_type=f32) + b_ref[0]


def _ada(cvec, w_ada, b_ada):
    nj = ADA_DIM // D_MODEL
    return pl.pallas_call(
        _ada_kernel,
        grid=(DEPTH, nj),
        in_specs=[pl.BlockSpec((8, D_MODEL), lambda l, j: (0, 0)),
                  pl.BlockSpec((1, D_MODEL, D_MODEL), lambda l, j: (l, 0, j)),
                  pl.BlockSpec((1, 1, D_MODEL), lambda l, j: (l, 0, j))],
        out_specs=pl.BlockSpec((1, 8, D_MODEL), lambda l, j: (l, 0, j)),
        out_shape=jax.ShapeDtypeStruct((DEPTH, 8, ADA_DIM), f32),
        name="ada_mod",
    )(cvec, w_ada, b_ada.reshape(DEPTH, 1, ADA_DIM))


def _in_kernel(x_ref, mod_ref, n1_ref, w_ref, b_ref, wg_ref, bg_ref, wgt_ref, bgt_ref,
               a_ref, kv_ref, b_out_ref, g_ref, gt_ref, pin_ref):
    x = x_ref[...]
    mod = mod_ref[0]
    h = x * lax.rsqrt(jnp.mean(x * x, axis=-1, keepdims=True) + EPS) * n1_ref[...]
    h = (h * (1.0 + mod[1:2]) + mod[0:1]).astype(bf16)
    pa = jnp.dot(h, w_ref[:, 0:W_A], preferred_element_type=f32) + b_ref[:, 0:W_A]
    a_ref[...] = pa.astype(bf16)
    kv_ref[...] = pa[:, NA_WIDTH:W_A]
    for j in range(4):
        lo = W_A + j * ML_PW
        pb = jnp.dot(h, w_ref[:, lo:lo + ML_PW], preferred_element_type=f32) + b_ref[:, lo:lo + ML_PW]
        if j == 0:
            pb = pb * (ML_DIM ** -0.5)
        b_out_ref[:, j * ML_PW:(j + 1) * ML_PW] = pb.astype(bf16)
    lo = W_A + W_B
    pin_ref[...] = jnp.dot(h, w_ref[:, lo:lo + POOL_WIDTH], preferred_element_type=f32) + b_ref[:, lo:lo + POOL_WIDTH]
    g_ref[...] = jnp.dot(h, wg_ref[...], preferred_element_type=f32) + bg_ref[...]
    gt_ref[...] = _nt(wgt_ref[...], h) + bgt_ref[...]


def _in_proj(x, mods, n1, w, b, wg, bg, wgt, bgt):
    tm = TOK_TILE
    const = lambda i: (0, 0)
    return pl.pallas_call(
        _in_kernel,
        grid=(N_TOK // tm,),
        in_specs=[pl.BlockSpec((tm, D_MODEL), lambda i: (i, 0)),
                  pl.BlockSpec((1, 6, D_MODEL), lambda i: (_mod_row(i, tm), 0, 0)),
                  pl.BlockSpec((1, D_MODEL), const),
                  pl.BlockSpec((D_MODEL, W_MAIN), const),
                  pl.BlockSpec((1, W_MAIN), const),
                  pl.BlockSpec((D_MODEL, LANE), const),
                  pl.BlockSpec((1, LANE), const),
                  pl.BlockSpec((N_GATE_COLS, D_MODEL), const),
                  pl.BlockSpec((N_GATE_COLS, 1), const)],
        out_specs=[pl.BlockSpec((tm, W_A), lambda i: (i, 0)),
                   pl.BlockSpec((tm, 2 * NA_WIDTH), lambda i: (i, 0)),
                   pl.BlockSpec((tm, W_B), lambda i: (i, 0)),
                   pl.BlockSpec((tm, LANE), lambda i: (i, 0)),
                   pl.BlockSpec((N_GATE_COLS, tm), lambda i: (0, i)),
                   pl.BlockSpec((tm, POOL_WIDTH), lambda i: (i, 0))],
        out_shape=[jax.ShapeDtypeStruct((N_TOK, W_A), bf16),
                   jax.ShapeDtypeStruct((N_TOK, 2 * NA_WIDTH), f32),
                   jax.ShapeDtypeStruct((N_TOK, W_B), bf16),
                   jax.ShapeDtypeStruct((N_TOK, LANE), f32),
                   jax.ShapeDtypeStruct((N_GATE_COLS, N_TOK), f32),
                   jax.ShapeDtypeStruct((N_TOK, POOL_WIDTH), f32)],
        compiler_params=pltpu.CompilerParams(dimension_semantics=("arbitrary",), vmem_limit_bytes=VMEM_LIMIT),
        name="in_proj",
    )(x, mods, n1, w, b, wg, bg, wgt, bgt)


def _pair_attention(qp, parts):
    lane = lax.broadcasted_iota(jnp.int32, (1, LANE), 1)
    outs = []
    for j in range(2):
        in_half = (lane >= j * NA_DIM) & (lane < (j + 1) * NA_DIM)
        qm = jnp.where(in_half, qp, jnp.zeros_like(qp))
        scores = []
        for k, _, bias in parts:
            s = _nt(qm, k)
            if bias is not None:
                s = s + bias[j]
            scores.append(s)
        m = scores[0].max(axis=-1, keepdims=True)
        for s in scores[1:]:
            m = jnp.maximum(m, s.max(axis=-1, keepdims=True))
        den = None
        acc = None
        for s, (_, v, _) in zip(scores, parts):
            p = jnp.exp(s - m)
            ps = p.sum(axis=-1, keepdims=True)
            den = ps if den is None else den + ps
            o = jnp.dot(p.astype(bf16), v, preferred_element_type=f32)
            acc = o if acc is None else acc + o
        outs.append(acc / den)
    return jnp.where(lane < NA_DIM, outs[0], outs[1])


def _ctx_attn_kernel(q_ref, k_ref, v_ref, o_ref):
    for p in range(NA_PAIRS):
        sl = slice(p * LANE, (p + 1) * LANE)
        o = _pair_attention(q_ref[0, :, sl], [(k_ref[0, :, sl], v_ref[0, :, sl], None)])
        o_ref[0, :, sl] = o.astype(bf16)


def _ctx_attention(qkv):
    blk = lambda c: pl.BlockSpec((1, SEQ, NA_WIDTH), lambda b, c=c: (b, 0, c))
    return pl.pallas_call(
        _ctx_attn_kernel,
        grid=(BATCH,),
        in_specs=[blk(0), blk(1), blk(2)],
        out_specs=pl.BlockSpec((1, SEQ, NA_WIDTH), lambda b: (b, 0, 0)),
        out_shape=jax.ShapeDtypeStruct((BATCH, SEQ, NA_WIDTH), bf16),
        name="ctx_attention",
    )(qkv, qkv, qkv)


def _na_window_start(rb):
    return jnp.clip(rb * NA_QROWS - NA_ROWS // 2, 0, DEC_SEQ // GRID_W - NA_KROWS)


def _na_kernel(q_ref, k_ref, v_ref, ck_ref, cv_ref, bias_ref, o_ref):
    rb = pl.program_id(1)
    start = pl.multiple_of(_na_window_start(rb) * GRID_W, GRID_W)
    nk = NA_KROWS * GRID_W
    for p in range(NA_PAIRS):
        sl = slice(p * LANE, (p + 1) * LANE)
        parts = [(k_ref[0, pl.ds(start, nk), sl], v_ref[0, pl.ds(start, nk), sl], bias_ref[0, 2 * p:2 * p + 2]),
                 (ck_ref[0, :, sl], cv_ref[0, :, sl], None)]
        o = _pair_attention(q_ref[0, :, sl], parts)
        o_ref[0, :, sl] = o.astype(bf16)


def _na_bias_variant(rb):
    n_rb = DEC_SEQ // GRID_W // NA_QROWS
    return jnp.where(rb == 0, 0, jnp.where(rb == n_rb - 1, 2, 1))


def _neighborhood_attention(qkv, ck, cv, bias, layer):
    nq = NA_QROWS * GRID_W
    nk = NA_KROWS * GRID_W
    n_rb = DEC_SEQ // nq
    return pl.pallas_call(
        _na_kernel,
        grid=(DEC_BATCH, n_rb),
        in_specs=[pl.BlockSpec((1, nq, NA_WIDTH), lambda b, r: (1 + b, r, 0)),
                  pl.BlockSpec((1, DEC_SEQ, NA_WIDTH), lambda b, r: (1 + b, 0, 1)),
                  pl.BlockSpec((1, DEC_SEQ, NA_WIDTH), lambda b, r: (1 + b, 0, 2)),
                  pl.BlockSpec((1, PAST_LEN, NA_WIDTH), lambda b, r: (b, 0, 0)),
                  pl.BlockSpec((1, PAST_LEN, NA_WIDTH), lambda b, r: (b, 0, 0)),
                  pl.BlockSpec((1, NA_HEADS, nq, nk), lambda b, r: (3 * layer + _na_bias_variant(r), 0, 0, 0))],
        out_specs=pl.BlockSpec((1, nq, NA_WIDTH), lambda b, r: (b, r, 0)),
        out_shape=jax.ShapeDtypeStruct((DEC_BATCH, DEC_SEQ, NA_WIDTH), bf16),
        compiler_params=pltpu.CompilerParams(dimension_semantics=("arbitrary", "arbitrary"),
                                             vmem_limit_bytes=VMEM_LIMIT),
        name="neighborhood_attention",
    )(qkv, qkv, qkv, ck, cv, bias)


def _na_bias_tables(rpb):
    rows = DEC_SEQ // GRID_W
    n_rb = rows // NA_QROWS
    qc = np.arange(GRID_W)[:, None]
    kc = np.arange(GRID_W)[None, :]
    dc = np.clip(kc - qc + NA_COLS - 1, 0, RPB_COLS - 1)
    col_start = np.clip(qc - NA_COLS // 2, 0, GRID_W - NA_COLS)
    col_ok = (kc >= col_start) & (kc < col_start + NA_COLS)
    pick_col = (dc[None] == np.arange(RPB_COLS)[:, None, None]).astype(np.float32)
    by_col = jnp.einsum('lhab,bqk->lhaqk', rpb.astype(f32), pick_col, precision=HI)
    tables = []
    for rb in (0, 1, n_rb - 1):
        r0 = rb * NA_QROWS
        ws = int(np.clip(r0 - NA_ROWS // 2, 0, rows - NA_KROWS))
        qr = (r0 + np.arange(NA_QROWS))[:, None]
        kr = (ws + np.arange(NA_KROWS))[None, :]
        row_start = np.clip(qr - NA_ROWS // 2, 0, rows - NA_ROWS)
        row_ok = (kr >= row_start) & (kr < row_start + NA_ROWS)
        dr = np.clip(kr - qr + NA_ROWS - 1, 0, RPB_ROWS - 1)
        pick_row = (dr[..., None] == np.arange(RPB_ROWS)).astype(np.float32)
        t = jnp.einsum('dwa,lhaqk->lhdqwk', pick_row, by_col, precision=HI)
        valid = row_ok[:, None, :, None] & col_ok[None, :, None, :]
        t = jnp.where(valid[None, None], t, NEG_INF)
        tables.append(t.reshape(DEPTH, NA_HEADS, NA_QROWS * GRID_W, NA_KROWS * GRID_W))
    return jnp.stack(tables, axis=1).reshape(DEPTH * 3, NA_HEADS, NA_QROWS * GRID_W, NA_KROWS * GRID_W)


def _log_sigmoid(x):
    return -(jnp.maximum(-x, 0.0) + jnp.log(1.0 + jnp.exp(-jnp.abs(x))))


def _mlstm_kernel(qf_ref, kf_ref, vf_ref, gf_ref, gtf_ref, qb_ref, kb_ref, vb_ref, gb_ref, gtb_ref,
                  c0_ref, m0_ref, hf_ref, hb_ref, c_out_ref, m_out_ref, c_scr, m_scr, *, n_chunks):
    L = ML_CHUNK
    c = pl.program_id(1)

    @pl.when(c == 0)
    def _():
        c_scr[...] = c0_ref[0]
        m_scr[...] = m0_ref[0]

    ri = lax.broadcasted_iota(jnp.int32, (L, L), 0)
    ci = lax.broadcasted_iota(jnp.int32, (L, L), 1)
    lane = lax.broadcasted_iota(jnp.int32, (L, ML_PAD), 1)
    ones_col = jnp.where(lane == 0, 1.0, 0.0).astype(bf16)
    dirs = ((qf_ref, kf_ref, vf_ref, gf_ref, gtf_ref, hf_ref, ri >= ci),
            (qb_ref, kb_ref, vb_ref, gb_ref, gtb_ref, hb_ref, ri <= ci))
    for d, (q_ref, k_ref, v_ref, g_ref, gt_ref, h_ref, visible) in enumerate(dirs):
        g = g_ref[...][:, 0:N_GATE_COLS]
        gt = gt_ref[...]
        vis = visible.astype(f32)
        b_cols = jnp.dot(vis, _log_sigmoid(g), preferred_element_type=f32, precision=HI)
        b_rows = _nt(_log_sigmoid(gt), vis, precision=HI)
        tot = jnp.sum(_log_sigmoid(g), axis=0, keepdims=True)
        for hd in range(ML_HEADS):
            st = d * ML_HEADS + hd
            ci_ = 2 * ML_HEADS * d + hd
            cf_ = ci_ + ML_HEADS
            sl = slice(hd * ML_PAD, (hd + 1) * ML_PAD)
            bc = b_cols[:, cf_:cf_ + 1]
            br = b_rows[cf_:cf_ + 1, :]
            li_r = gt[ci_:ci_ + 1, :]
            li_c = g[:, ci_:ci_ + 1]
            m_prev = m_scr[st:st + 1, 0:1]
            dmat = jnp.where(visible, bc - br + li_r, NEG_INF)
            inter = bc + m_prev
            m_t = jnp.maximum(inter, dmat.max(axis=-1, keepdims=True))
            w_intra = jnp.exp(dmat - m_t)
            w_inter = jnp.exp(inter - m_t)
            qh = q_ref[0, :, sl]
            kh = k_ref[0, :, sl]
            v_aug = jnp.concatenate([v_ref[0, :, sl], ones_col], axis=1)
            s = (_nt(qh, kh) * w_intra).astype(bf16)
            c_aug = c_scr[st]
            na = (w_inter * jnp.dot(qh, c_aug.astype(bf16), preferred_element_type=f32)
                  + jnp.dot(s, v_aug, preferred_element_type=f32))
            den = na[:, ML_PAD:ML_PAD + 1]
            h_ref[0, :, sl] = na[:, 0:ML_PAD] / jnp.maximum(jnp.abs(den), jnp.exp(-m_t))
            b_end = tot[:, cf_:cf_ + 1]
            g_col = b_end - bc + li_c
            m_new = jnp.maximum(b_end + m_prev, g_col.max(axis=0, keepdims=True))
            decay = jnp.exp(b_end + m_prev - m_new)
            kw = (kh.astype(f32) * jnp.exp(g_col - m_new)).astype(bf16)
            upd = lax.dot_general(kw, v_aug, (((0,), (0,)), ((), ())), preferred_element_type=f32)
            c_scr[st] = decay * c_aug + upd
            m_scr[st:st + 1, :] = jnp.broadcast_to(m_new, (1, LANE))

    @pl.when(c == n_chunks - 1)
    def _():
        c_out_ref[0] = c_scr[...]
        m_out_ref[0] = m_scr[...]


def _mlstm(qkvo, gates, gates_t, c0, m0, *, seq_len, first_seq, n_seq):
    L = ML_CHUNK
    nc = seq_len // L
    per = seq_len // L
    fwd = lambda b, c: (first_seq + b) * per + c
    bwd = lambda b, c: (first_seq + b) * per + (nc - 1 - c)

    def specs(pos):
        return [pl.BlockSpec((1, L, ML_PW), lambda b, c, j=j: (pos(b, c), 0, j)) for j in range(3)] + [
            pl.BlockSpec((L, LANE), lambda b, c: (pos(b, c), 0)),
            pl.BlockSpec((N_GATE_COLS, L), lambda b, c: (0, pos(b, c)))]

    q3 = qkvo.reshape(N_TOK // L, L, W_B)
    n_str = 2 * ML_HEADS
    return pl.pallas_call(
        functools.partial(_mlstm_kernel, n_chunks=nc),
        grid=(n_seq, nc),
        in_specs=specs(fwd) + specs(bwd) + [
            pl.BlockSpec((1, n_str, ML_PAD, CAUG), lambda b, c: (b, 0, 0, 0)),
            pl.BlockSpec((1, n_str, LANE), lambda b, c: (b, 0, 0))],
        out_specs=[pl.BlockSpec((1, L, ML_PW), lambda b, c: (b * per + c, 0, 0)),
                   pl.BlockSpec((1, L, ML_PW), lambda b, c: (b * per + (nc - 1 - c), 0, 0)),
                   pl.BlockSpec((1, n_str, ML_PAD, CAUG), lambda b, c: (b, 0, 0, 0)),
                   pl.BlockSpec((1, n_str, LANE), lambda b, c: (b, 0, 0))],
        out_shape=[jax.ShapeDtypeStruct((n_seq * nc, L, ML_PW), f32),
                   jax.ShapeDtypeStruct((n_seq * nc, L, ML_PW), f32),
                   jax.ShapeDtypeStruct((n_seq, n_str, ML_PAD, CAUG), f32),
                   jax.ShapeDtypeStruct((n_seq, n_str, LANE), f32)],
        scratch_shapes=[pltpu.VMEM((n_str, ML_PAD, CAUG), f32), pltpu.VMEM((n_str, LANE), f32)],
        compiler_params=pltpu.CompilerParams(dimension_semantics=("arbitrary", "arbitrary"),
                                             vmem_limit_bytes=VMEM_LIMIT),
        name="mlstm",
    )(q3, q3, q3, gates, gates_t, q3, q3, q3, gates, gates_t, c0, m0)


def _pack_ml_state(C, n, m):
    B = C.shape[0]
    c_aug = jnp.zeros((B, 2, ML_HEADS, ML_PAD, CAUG), f32)
    c_aug = c_aug.at[:, :, :, :ML_DIM, :ML_DIM].set(C.astype(f32))
    c_aug = c_aug.at[:, :, :, :ML_DIM, ML_PAD].set(n.astype(f32))
    m_b = jnp.broadcast_to(m.astype(f32)[..., None], (B, 2, ML_HEADS, LANE))
    return c_aug.reshape(B, 2 * ML_HEADS, ML_PAD, CAUG), m_b.reshape(B, 2 * ML_HEADS, LANE)


def _unpack_ml_state(c_aug, m_b):
    B = c_aug.shape[0]
    c_aug = c_aug.reshape(B, 2, ML_HEADS, ML_PAD, CAUG)
    return (c_aug[:, :, :, :ML_DIM, :ML_DIM], c_aug[:, :, :, :ML_DIM, ML_PAD],
            m_b.reshape(B, 2, ML_HEADS, LANE)[..., 0])


def _pool_kernel(u_ref, w_ref, sc_ref, o_ref, *, seq_len, win_rows):
    tt = POOL_TILE
    t0 = pl.program_id(1) * tt
    ws = pl.multiple_of(jnp.clip(t0 - POOL_HALO, 0, seq_len - win_rows), 8)
    u_win = u_ref[0, pl.ds(ws, win_rows), :]
    u_hi = u_win.astype(bf16)
    u_lo = (u_win - u_hi.astype(f32)).astype(bf16)
    u_tile = u_ref[0, pl.ds(pl.multiple_of(t0, tt), tt), :]
    t_abs = t0 + lax.broadcasted_iota(jnp.int32, (tt, 1), 0)
    s_abs = ws + lax.broadcasted_iota(jnp.int32, (1, win_rows), 1)
    lane = lax.broadcasted_iota(jnp.int32, (1, LANE), 1)
    sums, inv_cnt = [], []
    for w in POOL_WINDOWS:
        lo = jnp.maximum(t_abs - w // 2, 0)
        hi = jnp.minimum(t_abs - w // 2 + w, seq_len)
        sums.append(jnp.where((s_abs >= lo) & (s_abs < hi), 1.0, 0.0).astype(bf16))
        inv_cnt.append(1.0 / (hi - lo).astype(f32))
    pooled = []
    for p in range(POOL_GROUPS // 2):
        sl = slice(p * LANE, (p + 1) * LANE)
        means = []
        for j in range(2):
            a = sums[2 * p + j]
            tot = (jnp.dot(a, u_hi[:, sl], preferred_element_type=f32)
                   + jnp.dot(a, u_lo[:, sl], preferred_element_type=f32))
            means.append(tot * inv_cnt[2 * p + j])
        pooled.append(jnp.where(lane < POOL_DIM, means[0], means[1]) - u_tile[:, sl])
    pooled = jnp.concatenate(pooled, axis=1).astype(bf16)
    y = jnp.dot(pooled, w_ref[...], preferred_element_type=f32) * sc_ref[...]
    o_ref[0] = y.astype(bf16)


def _pool(pin, w_bd, scale, *, seq_len, first_seq, n_seq):
    win_rows = min(seq_len, POOL_TILE + 2 * POOL_HALO)
    return pl.pallas_call(
        functools.partial(_pool_kernel, seq_len=seq_len, win_rows=win_rows),
        grid=(n_seq, seq_len // POOL_TILE),
        in_specs=[pl.BlockSpec((1, seq_len, POOL_WIDTH), lambda b, t: (first_seq + b, 0, 0)),
                  pl.BlockSpec((POOL_WIDTH, POOL_WIDTH), lambda b, t: (0, 0)),
                  pl.BlockSpec((1, POOL_WIDTH), lambda b, t: (0, 0))],
        out_specs=pl.BlockSpec((1, POOL_TILE, POOL_WIDTH), lambda b, t: (b * (seq_len // POOL_TILE) + t, 0, 0)),
        out_shape=jax.ShapeDtypeStruct((n_seq * seq_len // POOL_TILE, POOL_TILE, POOL_WIDTH), bf16),
        name="pool",
    )(pin, w_bd, scale)


def _top2_sum(a, b, c, d):
    hi1, lo1 = jnp.maximum(a, b), jnp.minimum(a, b)
    hi2, lo2 = jnp.maximum(c, d), jnp.minimum(c, d)
    return jnp.maximum(hi1, hi2) + jnp.maximum(jnp.minimum(hi1, hi2), jnp.maximum(lo1, lo2))


def _first_match(vals, target):
    idx = jnp.full_like(target, float(len(vals) - 1))
    for i in range(len(vals) - 2, -1, -1):
        idx = jnp.where(vals[i] == target, float(i), idx)
    return idx


def _pick(vals, idx):
    out = vals[-1]
    for i in range(len(vals) - 2, -1, -1):
        out = jnp.where(idx == float(i), vals[i], out)
    return out


def _route(logits_t, bias_t):
    scores = jax.nn.sigmoid(logits_t)
    sel = scores + bias_t
    row = lambda a, i: a[i:i + 1, :]
    grp = [_top2_sum(*[row(sel, EXPERTS_PER_GROUP * g + i) for i in range(EXPERTS_PER_GROUP)])
           for g in range(N_EXPERT_GROUPS)]
    best = functools.reduce(jnp.maximum, grp)
    gidx = _first_match(grp, best)
    sel_g = [_pick([row(sel, EXPERTS_PER_GROUP * g + i) for g in range(N_EXPERT_GROUPS)], gidx)
             for i in range(EXPERTS_PER_GROUP)]
    sco_g = [_pick([row(scores, EXPERTS_PER_GROUP * g + i) for g in range(N_EXPERT_GROUPS)], gidx)
             for i in range(EXPERTS_PER_GROUP)]
    i0 = _first_match(sel_g, functools.reduce(jnp.maximum, sel_g))
    rest = [jnp.where(i0 == float(i), -jnp.inf, sel_g[i]) for i in range(EXPERTS_PER_GROUP)]
    i1 = _first_match(rest, functools.reduce(jnp.maximum, rest))
    s0, s1 = _pick(sco_g, i0), _pick(sco_g, i1)
    tot = s0 + s1
    rid = lax.broadcasted_iota(jnp.int32, (LANE, logits_t.shape[1]), 0)
    rows = (EXPERTS_PER_GROUP * gidx + i0, EXPERTS_PER_GROUP * gidx + i1, s0 / tot, s1 / tot)
    out = jnp.zeros(rid.shape, f32)
    for i, r in enumerate(rows):
        out = jnp.where(rid == i, r, out)
    return out


def _out_kernel(x_ref, mod_ref, oa_ref, hf_ref, hb_ref, ob_ref, oc_ref, mln_ref, wo_ref, n2_ref, wr_ref, br_ref,
                x1_ref, h2_ref, rt_ref, rc_ref):
    tm = x_ref.shape[0]
    mod = mod_ref[0]
    hsum = hf_ref[...] + hb_ref[...]
    outs_b = []
    for hd in range(ML_HEADS):
        sl = slice(hd * ML_PAD, (hd + 1) * ML_PAD)
        hh = hsum[:, sl]
        ms = jnp.sum(hh * hh, axis=-1, keepdims=True) * (1.0 / ML_DIM)
        hn = hh * lax.rsqrt(ms + EPS) * mln_ref[:, sl]
        outs_b.append((jax.nn.sigmoid(ob_ref[:, sl].astype(f32)) * hn).astype(bf16))
    out_b = jnp.concatenate(outs_b, axis=1)
    mixed = (jnp.dot(oa_ref[...], wo_ref[0:NA_WIDTH, :], preferred_element_type=f32)
             + jnp.dot(out_b, wo_ref[NA_WIDTH:NA_WIDTH + ML_PW, :], preferred_element_type=f32)
             + jnp.dot(oc_ref[...], wo_ref[NA_WIDTH + ML_PW:, :], preferred_element_type=f32))
    x1 = x_ref[...] + mod[2:3] * mixed
    x1_ref[...] = x1
    h2 = x1 * lax.rsqrt(jnp.mean(x1 * x1, axis=-1, keepdims=True) + EPS) * n2_ref[...]
    h2 = h2 * (1.0 + mod[4:5]) + mod[3:4]
    h2_ref[...] = h2
    route_t = _route(_nt(wr_ref[...], h2, precision=HI), br_ref[...])
    rt_ref[...] = route_t[0:8]
    rc_ref[...] = route_t.T


def _out_proj(x, mods, oa, hf, hb, qkvo, oc, mln, wo, n2, wr_t, br_t):
    tm = TOK_TILE
    const = lambda i: (0, 0)
    row = lambda i: (i, 0)
    return pl.pallas_call(
        _out_kernel,
        grid=(N_TOK // tm,),
        in_specs=[pl.BlockSpec((tm, D_MODEL), row),
                  pl.BlockSpec((1, 6, D_MODEL), lambda i: (_mod_row(i, tm), 0, 0)),
                  pl.BlockSpec((tm, NA_WIDTH), row),
                  pl.BlockSpec((tm, ML_PW), row),
                  pl.BlockSpec((tm, ML_PW), row),
                  pl.BlockSpec((tm, ML_PW), lambda i: (i, 3)),
                  pl.BlockSpec((tm, POOL_WIDTH), row),
                  pl.BlockSpec((1, ML_PW), const),
                  pl.BlockSpec((NA_WIDTH + ML_PW + POOL_WIDTH, D_MODEL), const),
                  pl.BlockSpec((1, D_MODEL), const),
                  pl.BlockSpec((N_EXPERTS, D_MODEL), const),
                  pl.BlockSpec((N_EXPERTS, 1), const)],
        out_specs=[pl.BlockSpec((tm, D_MODEL), row),
                   pl.BlockSpec((tm, D_MODEL), row),
                   pl.BlockSpec((8, tm), lambda i: (0, i)),
                   pl.BlockSpec((tm, LANE), row)],
        out_shape=[jax.ShapeDtypeStruct((N_TOK, D_MODEL), f32),
                   jax.ShapeDtypeStruct((N_TOK, D_MODEL), f32),
                   jax.ShapeDtypeStruct((8, N_TOK), f32),
                   jax.ShapeDtypeStruct((N_TOK, LANE), f32)],
        compiler_params=pltpu.CompilerParams(dimension_semantics=("arbitrary",), vmem_limit_bytes=VMEM_LIMIT),
        name="out_proj_router",
    )(x, mods, oa, hf, hb, qkvo, oc, mln, wo, n2, wr_t, br_t)


def _rank_kernel(rt_ref, pos_ref, te_ref, carry_ref):
    tm = rt_ref.shape[1]
    p = pl.program_id(0)
    i = pl.program_id(1)
    rid = lax.broadcasted_iota(jnp.int32, (N_EXPERTS, tm), 0).astype(f32)
    oh0 = rid == rt_ref[0:1, :]
    oh1 = rid == rt_ref[1:2, :]
    both = jnp.where(oh0 | oh1, 1.0, 0.0)
    counts = jnp.sum(both, axis=1, keepdims=True)

    @pl.when((p == 0) & (i == 0))
    def _():
        carry_ref[...] = jnp.zeros_like(carry_ref)

    @pl.when((p == 1) & (i == 0))
    def _():
        cnt = carry_ref[...]
        padded = jnp.floor((cnt + (MOE_TILE - 1)) * (1.0 / MOE_TILE)) * MOE_TILE
        er = lax.broadcasted_iota(jnp.int32, (N_EXPERTS, N_EXPERTS), 0)
        ec = lax.broadcasted_iota(jnp.int32, (N_EXPERTS, N_EXPERTS), 1)
        off = jnp.dot(jnp.where(ec < er, 1.0, 0.0), padded, preferred_element_type=f32, precision=HI)
        carry_ref[...] = off
        total = jnp.sum(padded, axis=0, keepdims=True)
        n_used = total * (1.0 / MOE_TILE)
        tile = lax.broadcasted_iota(jnp.int32, (1, LANE), 1).astype(f32)
        row0 = jnp.minimum(tile, n_used - 1.0) * MOE_TILE
        expert = jnp.sum(jnp.where(off <= row0, 1.0, 0.0), axis=0, keepdims=True) - 1.0
        sub = lax.broadcasted_iota(jnp.int32, (8, LANE), 0)
        te_ref[...] = jnp.where(sub == 0, expert, jnp.where(sub == 1, n_used, 0.0)).astype(jnp.int32)

    @pl.when(p == 1)
    def _():
        sr = lax.broadcasted_iota(jnp.int32, (tm, tm), 0)
        sc = lax.broadcasted_iota(jnp.int32, (tm, tm), 1)
        earlier = jnp.dot(both.astype(bf16), jnp.where(sr < sc, 1.0, 0.0).astype(bf16),
                          preferred_element_type=f32)
        before = carry_ref[:, 0:1] + earlier
        pos0 = jnp.sum(jnp.where(oh0, before, 0.0), axis=0, keepdims=True)
        pos1 = jnp.sum(jnp.where(oh1, before, 0.0), axis=0, keepdims=True)
        sub = lax.broadcasted_iota(jnp.int32, (8, tm), 0)
        pos_ref[...] = jnp.where(sub == 0, pos0, jnp.where(sub == 1, pos1, 0.0)).astype(jnp.int32)

    carry_ref[...] += counts


def _rank(route_t):
    tm = TOK_TILE
    return pl.pallas_call(
        _rank_kernel,
        grid=(2, N_TOK // tm),
        in_specs=[pl.BlockSpec((8, tm), lambda p, i: (0, i))],
        out_specs=[pl.BlockSpec((8, tm), lambda p, i: (0, i * p)),
                   pl.BlockSpec((8, LANE), lambda p, i: (0, 0))],
        out_shape=[jax.ShapeDtypeStruct((8, N_TOK), jnp.int32),
                   jax.ShapeDtypeStruct((8, LANE), jnp.int32)],
        scratch_shapes=[pltpu.VMEM((N_EXPERTS, LANE), f32)],
        compiler_params=pltpu.CompilerParams(dimension_semantics=("arbitrary", "arbitrary")),
        name="moe_rank",
    )(route_t)


def _dispatch_kernel(pos0_ref, pos1_ref, h_ref, xs_in_ref, xs_ref, sem):
    del xs_in_ref
    td = h_ref.shape[0]
    base = pl.program_id(0) * td

    def row_copy(t, pos_ref):
        return pltpu.make_async_copy(h_ref.at[pl.ds(t, 1)], xs_ref.at[pl.ds(pos_ref[base + t], 1)], sem)

    def issue(t, carry):
        row_copy(t, pos0_ref).start()
        row_copy(t, pos1_ref).start()
        return carry

    lax.fori_loop(0, td, issue, 0, unroll=8)
    for _ in range(2):
        pltpu.make_async_copy(h_ref, xs_ref.at[pl.ds(0, td)], sem).wait()


def _dispatch(pos0, pos1, h2p, xs_init):
    td = DISPATCH_TILE
    return pl.pallas_call(
        _dispatch_kernel,
        grid_spec=pltpu.PrefetchScalarGridSpec(
            num_scalar_prefetch=2,
            grid=(N_TOK // td,),
            in_specs=[pl.BlockSpec((td, D_MODEL), lambda i, p0, p1: (i, 0)),
                      pl.BlockSpec(memory_space=pl.ANY)],
            out_specs=pl.BlockSpec(memory_space=pl.ANY),
            scratch_shapes=[pltpu.SemaphoreType.DMA(())]),
        out_shape=jax.ShapeDtypeStruct(xs_init.shape, xs_init.dtype),
        input_output_aliases={3: 0},
        compiler_params=pltpu.CompilerParams(dimension_semantics=("arbitrary",)),
        name="moe_dispatch",
    )(pos0, pos1, h2p, xs_init)


def _expert_kernel(te_ref, xs_ref, wg_ref, wu_ref, wd_ref, ys_ref):
    used = pl.program_id(0) < te_ref[1, 0]

    @pl.when(jnp.logical_not(used))
    def _():
        ys_ref[...] = jnp.zeros_like(ys_ref)

    @pl.when(used)
    def _():
        x = xs_ref[...].astype(bf16)
        hg = jnp.dot(x, wg_ref[0, 0].astype(bf16), preferred_element_type=f32)
        hu = jnp.dot(x, wu_ref[0, 0].astype(bf16), preferred_element_type=f32)
        hid = (hg * jax.nn.sigmoid(hg) * hu).astype(bf16)
        ys_ref[...] = jnp.dot(hid, wd_ref[0, 0].astype(bf16), preferred_element_type=f32)


def _experts(te, xs, w_gate, w_up, w_down, layer):
    tm = MOE_TILE
    row = lambda j, te: (jnp.minimum(j, te[1, 0] - 1), 0)
    wspec = lambda shape: pl.BlockSpec((1, 1) + shape, lambda j, te: (layer, te[0, j], 0, 0))
    return pl.pallas_call(
        _expert_kernel,
        grid_spec=pltpu.PrefetchScalarGridSpec(
            num_scalar_prefetch=1,
            grid=(MOE_ROWS // tm,),
            in_specs=[pl.BlockSpec((tm, D_MODEL), row),
                      wspec((D_MODEL, D_EXPERT)), wspec((D_MODEL, D_EXPERT)), wspec((D_EXPERT, D_MODEL))],
            out_specs=pl.BlockSpec((tm, D_MODEL), lambda j, te: (j, 0))),
        out_shape=jax.ShapeDtypeStruct((MOE_ROWS, D_MODEL), f32),
        compiler_params=pltpu.CompilerParams(dimension_semantics=("arbitrary",), vmem_limit_bytes=VMEM_LIMIT),
        name="moe_experts",
    )(te, xs, w_gate, w_up, w_down)


def _combine_kernel(pos0_ref, pos1_ref, ys_ref, x1_ref, rc_ref, mod_ref, fn_ref, o_ref, buf, sem, *, final):
    tc = x1_ref.shape[0]
    i = pl.program_id(0)
    slot = i % 2

    def issue(tile, sl):
        base = tile * tc

        def body(t, carry):
            for s, pos_ref in enumerate((pos0_ref, pos1_ref)):
                pltpu.make_async_copy(ys_ref.at[pl.ds(pos_ref[base + t], 1)], buf.at[sl, s, pl.ds(t, 1)],
                                      sem.at[sl]).start()
            return carry

        lax.fori_loop(0, tc, body, 0, unroll=8)

    @pl.when(i == 0)
    def _():
        issue(0, 0)

    @pl.when(i + 1 < pl.num_programs(0))
    def _():
        issue(i + 1, 1 - slot)

    for s in range(2):
        pltpu.make_async_copy(ys_ref.at[pl.ds(0, tc)], buf.at[slot, s], sem.at[slot]).wait()
    rc = rc_ref[...]
    moe = rc[:, 2:3] * buf[slot, 0] + rc[:, 3:4] * buf[slot, 1]
    x2 = x1_ref[...] + mod_ref[0][5:6] * moe
    if final:
        x2 = x2 * lax.rsqrt(jnp.mean(x2 * x2, axis=-1, keepdims=True) + EPS) * fn_ref[...]
    o_ref[...] = x2


def _combine(pos0, pos1, ys, x1, rc, mods, fn, *, final):
    tc = COMBINE_TILE
    row = lambda i, p0, p1: (i, 0)
    return pl.pallas_call(
        functools.partial(_combine_kernel, final=final),
        grid_spec=pltpu.PrefetchScalarGridSpec(
            num_scalar_prefetch=2,
            grid=(N_TOK // tc,),
            in_specs=[pl.BlockSpec(memory_space=pl.ANY),
                      pl.BlockSpec((tc, D_MODEL), row),
                      pl.BlockSpec((tc, LANE), row),
                      pl.BlockSpec((1, 6, D_MODEL), lambda i, p0, p1: (_mod_row(i, tc), 0, 0)),
                      pl.BlockSpec((1, D_MODEL), lambda i, p0, p1: (0, 0))],
            out_specs=pl.BlockSpec((tc, D_MODEL), row),
            scratch_shapes=[pltpu.VMEM((2, 2, tc, D_MODEL), f32), pltpu.SemaphoreType.DMA((2,))]),
        out_shape=jax.ShapeDtypeStruct((N_TOK, D_MODEL), f32),
        compiler_params=pltpu.CompilerParams(dimension_semantics=("arbitrary",), vmem_limit_bytes=VMEM_LIMIT),
        name="moe_combine",
    )(pos0, pos1, ys, x1, rc, mods, fn)


def _moe(h2, route_t, rc, w_gate, w_up, w_down, layer, x1, mods, fn, xs_buf, *, final):
    pos, te = _rank(route_t)
    pos0, pos1 = pos[0], pos[1]
    xs = _dispatch(pos0, pos1, h2, xs_buf)
    ys = _experts(te, xs, w_gate, w_up, w_down, layer)
    return _combine(pos0, pos1, ys, x1, rc, mods, fn, final=final), xs


def _pad_heads(w):
    lead = w.shape[:-1]
    w = w.reshape(*lead, ML_HEADS, ML_DIM)
    w = jnp.pad(w, [(0, 0)] * len(lead) + [(0, 0), (0, ML_PAD - ML_DIM)])
    return w.reshape(*lead, ML_PW)


def _pack_w_in(w, b):
    wb = jnp.concatenate([w, b[None]], axis=0)
    o = 0
    qa = wb[:, o:o + NA_WIDTH] * (NA_DIM ** -0.5)
    ka = wb[:, o + NA_WIDTH:o + 2 * NA_WIDTH]
    va = wb[:, o + 2 * NA_WIDTH:o + 3 * NA_WIDTH]
    o += 3 * NA_WIDTH
    segs_b = [_pad_heads(wb[:, o + j * ML_WIDTH:o + (j + 1) * ML_WIDTH]) for j in range(4)]
    o += 4 * ML_WIDTH
    gates = wb[:, o:o + N_GATE_COLS]
    o += N_GATE_COLS
    pool = wb[:, o:o + POOL_WIDTH]
    main = jnp.concatenate([qa, ka, va] + segs_b + [pool], axis=1)
    gates_p = jnp.pad(gates, ((0, 0), (0, LANE - N_GATE_COLS)))
    return (main[:-1].astype(bf16), main[-1:].astype(f32),
            gates_p[:-1].astype(bf16), gates_p[-1:].astype(f32),
            gates[:-1].T.astype(bf16), gates[-1][:, None].astype(f32))


def _pack_w_out(w):
    wb = w[NA_WIDTH:NA_WIDTH + ML_WIDTH].reshape(ML_HEADS, ML_DIM, D_MODEL)
    wb = jnp.pad(wb, ((0, 0), (0, ML_PAD - ML_DIM), (0, 0))).reshape(ML_PW, D_MODEL)
    return jnp.concatenate([w[:NA_WIDTH], wb, w[NA_WIDTH + ML_WIDTH:]], axis=0).astype(bf16)


def _block_diag(w):
    g, c, _ = w.shape
    eye = jnp.eye(g, dtype=w.dtype)
    return (eye[:, None, :, None] * w[:, :, None, :]).reshape(g * c, g * c)


def kernel(x_prompt, x_sample, cache_k_attn, cache_v_attn, state_mlstm_C, state_mlstm_n, state_mlstm_m, c, c_ctx,
           w_ada, b_ada, norm1, w_in, b_in, rpb, ml_norm, w_pool, pool_scale, w_out, norm2, w_router, b_router,
           w_gate, w_up, w_down, final_norm):
    dt = x_prompt.dtype
    x = jnp.concatenate([x_prompt.reshape(N_CTX, D_MODEL), x_sample.reshape(N_LAT, D_MODEL)], axis=0).astype(f32)
    cvec = jnp.concatenate([c_ctx[None], c, jnp.zeros((8 - 1 - DEC_BATCH, D_MODEL), c.dtype)], axis=0).astype(f32)
    mods_all = _ada(cvec, w_ada.astype(f32), b_ada.astype(f32))
    mods_all = mods_all[:, :1 + DEC_BATCH].reshape(DEPTH, 1 + DEC_BATCH, 6, D_MODEL)

    wr_t = w_router.astype(f32).T
    br_t = b_router.astype(f32)[:, None]
    fn = final_norm.astype(f32)[None]
    zero_c = jnp.zeros((BATCH, 2 * ML_HEADS, ML_PAD, CAUG), f32)
    zero_m = jnp.zeros((BATCH, 2 * ML_HEADS, LANE), f32)

    na_bias = _na_bias_tables(rpb)
    xs_buf = jnp.zeros((MOE_ROWS, D_MODEL), f32)

    ks, vs, Cs, ns, ms = [], [], [], [], []
    for l in range(DEPTH):
        mods = mods_all[l]
        w_main, b_main, wg, bg, wgt, bgt = _pack_w_in(w_in[l], b_in[l])
        qkva, kv32, qkvo, gates, gates_t, pin = _in_proj(x, mods, norm1[l].astype(f32)[None], w_main, b_main,
                                                         wg, bg, wgt, bgt)
        ks.append(kv32[:N_CTX, :NA_WIDTH].reshape(BATCH, SEQ, NA_HEADS, NA_DIM))
        vs.append(kv32[:N_CTX, NA_WIDTH:].reshape(BATCH, SEQ, NA_HEADS, NA_DIM))

        oa_ctx = _ctx_attention(qkva.reshape(N_TOK // SEQ, SEQ, W_A))
        ck = (cache_k_attn[:, l].reshape(DEC_BATCH, PAST_LEN, NA_WIDTH)).astype(bf16)
        cv = (cache_v_attn[:, l].reshape(DEC_BATCH, PAST_LEN, NA_WIDTH)).astype(bf16)
        oa_lat = _neighborhood_attention(qkva.reshape(N_TOK // DEC_SEQ, DEC_SEQ, W_A), ck, cv, na_bias, l)
        oa = jnp.concatenate([oa_ctx.reshape(N_CTX, NA_WIDTH), oa_lat.reshape(N_LAT, NA_WIDTH)], axis=0)

        hf_c, hb_c, c_fin, m_fin = _mlstm(qkvo, gates, gates_t, zero_c, zero_m,
                                          seq_len=SEQ, first_seq=0, n_seq=BATCH)
        c_l, m_l = _pack_ml_state(state_mlstm_C[:, l], state_mlstm_n[:, l], state_mlstm_m[:, l])
        hf_l, hb_l, _, _ = _mlstm(qkvo, gates, gates_t, c_l, m_l,
                                  seq_len=DEC_SEQ, first_seq=N_CTX // DEC_SEQ, n_seq=DEC_BATCH)
        hf = jnp.concatenate([hf_c.reshape(N_CTX, ML_PW), hf_l.reshape(N_LAT, ML_PW)], axis=0)
        hb = jnp.concatenate([hb_c.reshape(N_CTX, ML_PW), hb_l.reshape(N_LAT, ML_PW)], axis=0)
        C_l, n_l, m_l2 = _unpack_ml_state(c_fin, m_fin)
        Cs.append(C_l)
        ns.append(n_l)
        ms.append(m_l2)

        w_bd = _block_diag(w_pool[l].astype(f32)).astype(bf16)
        psc = pool_scale[l].astype(f32)[None]
        oc_ctx = _pool(pin.reshape(N_TOK // SEQ, SEQ, POOL_WIDTH), w_bd, psc, seq_len=SEQ, first_seq=0, n_seq=BATCH)
        oc_lat = _pool(pin.reshape(N_TOK // DEC_SEQ, DEC_SEQ, POOL_WIDTH), w_bd, psc,
                       seq_len=DEC_SEQ, first_seq=N_CTX // DEC_SEQ, n_seq=DEC_BATCH)
        oc = jnp.concatenate([oc_ctx.reshape(N_CTX, POOL_WIDTH), oc_lat.reshape(N_LAT, POOL_WIDTH)], axis=0)

        x1, h2, route_t, rc = _out_proj(x, mods, oa, hf, hb, qkvo, oc, _pad_heads(ml_norm[l].astype(f32))[None],
                                        _pack_w_out(w_out[l]), norm2[l].astype(f32)[None], wr_t, br_t)
        x, xs_buf = _moe(h2, route_t, rc, w_gate, w_up, w_down, l, x1, mods, fn, xs_buf, final=(l == DEPTH - 1))

    y_prompt = x[:N_CTX].reshape(BATCH, SEQ, D_MODEL).astype(dt)
    y_sample = x[N_CTX:].reshape(DEC_BATCH, DEC_SEQ, D_MODEL).astype(dt)
    return (y_prompt, y_sample,
            jnp.stack(ks, axis=1).astype(dt), jnp.stack(vs, axis=1).astype(dt),
            jnp.stack(Cs, axis=1).astype(dt), jnp.stack(ns, axis=1).astype(dt), jnp.stack(ms, axis=1).astype(dt))
```

```python
import functools

import numpy as np
import jax
import jax.numpy as jnp
from jax import lax
from jax.experimental import pallas as pl
from jax.experimental.pallas import tpu as pltpu

D_MODEL = 1024
BATCH = 16
SEQ = 256
DEPTH = 4
DEC_BATCH = 2
DEC_SEQ = 4096
PAST_LEN = 256
GRID_W = 64
EPS = 1e-6
NEG_INF = -1e30
NA_HEADS = 6
NA_DIM = 64
NA_WIDTH = NA_HEADS * NA_DIM
NA_ROWS = 8
NA_COLS = 16
RPB_ROWS = 2 * NA_ROWS - 1
RPB_COLS = 2 * NA_COLS - 1
ML_HEADS = 4
ML_DIM = 96
ML_WIDTH = ML_HEADS * ML_DIM
POOL_WINDOWS = (2, 4, 8, 16)
POOL_GROUPS = 4
POOL_DIM = 64
POOL_WIDTH = POOL_GROUPS * POOL_DIM
N_GATE_COLS = 4 * ML_HEADS
N_EXPERTS = 16
N_EXPERT_GROUPS = 4
EXPERTS_PER_GROUP = N_EXPERTS // N_EXPERT_GROUPS
D_EXPERT = 512
ADA_DIM = 6 * D_MODEL

N_CTX = BATCH * SEQ
N_LAT = DEC_BATCH * DEC_SEQ
N_TOK = N_CTX + N_LAT
LANE = 128
ML_PAD = LANE
ML_PW = ML_HEADS * ML_PAD
CAUG = 2 * ML_PAD
NA_PAIRS = NA_HEADS // 2
TOK_TILE = 512
ML_CHUNK = 256
NA_QROWS = 4
NA_KROWS = NA_QROWS + NA_ROWS - 1
POOL_HALO = max(POOL_WINDOWS) // 2
MOE_TILE = 512
MOE_ROWS = 2 * N_TOK + N_EXPERTS * MOE_TILE
DISPATCH_TILE = 256
COMBINE_TILE = 256
VMEM_LIMIT = 56 * 1024 * 1024

W_A = 3 * NA_WIDTH
W_B = 3 * ML_PW
N_TCOLS = ML_PW + N_GATE_COLS
W_MAIN = W_A + W_B + POOL_WIDTH

f32 = jnp.float32
bf16 = jnp.bfloat16
HI = lax.Precision.HIGHEST


def _nt(a, b, **kw):
    return lax.dot_general(a, b, (((1,), (1,)), ((), ())), preferred_element_type=f32, **kw)


def _mod_row(i, tile):
    n_ctx_tiles = N_CTX // tile
    per_batch = DEC_SEQ // tile
    return jnp.where(i < n_ctx_tiles, 0, 1 + (i - n_ctx_tiles) // per_batch)


def _ada_kernel(c_ref, w_ref, b_ref, o_ref):
    s = c_ref[...]
    s = s * jax.nn.sigmoid(s)
    o_ref[0] = jnp.dot(s.astype(bf16), w_ref[0].astype(bf16), preferred_element_type=f32) + b_ref[0]


def _ada(cvec, w_ada, b_ada):
    nj = ADA_DIM // D_MODEL
    return pl.pallas_call(
        _ada_kernel,
        grid=(DEPTH, nj),
        in_specs=[pl.BlockSpec((8, D_MODEL), lambda l, j: (0, 0)),
                  pl.BlockSpec((1, D_MODEL, D_MODEL), lambda l, j: (l, 0, j)),
                  pl.BlockSpec((1, 1, D_MODEL), lambda l, j: (l, 0, j))],
        out_specs=pl.BlockSpec((1, 8, D_MODEL), lambda l, j: (l, 0, j)),
        out_shape=jax.ShapeDtypeStruct((DEPTH, 8, ADA_DIM), f32),
        name="ada_mod",
    )(cvec, w_ada, b_ada.reshape(DEPTH, 1, ADA_DIM))


def _in_kernel(x_ref, mod_ref, n1_ref, w_ref, b_ref, wg_ref, bg_ref, wt_ref, bt_ref,
               a_ref, kv_ref, b_out_ref, kt_ref, g_ref, gt_ref, pin_ref):
    x = x_ref[...]
    mod = mod_ref[0]
    h = x * lax.rsqrt(jnp.mean(x * x, axis=-1, keepdims=True) + EPS) * n1_ref[...]
    h = (h * (1.0 + mod[1:2]) + mod[0:1]).astype(bf16)
    pa = jnp.dot(h, w_ref[:, 0:W_A], preferred_element_type=f32) + b_ref[:, 0:W_A]
    a_ref[...] = pa.astype(bf16)
    kv_ref[...] = pa[:, NA_WIDTH:W_A]
    for j in range(3):
        lo = W_A + j * ML_PW
        pb = jnp.dot(h, w_ref[:, lo:lo + ML_PW], preferred_element_type=f32) + b_ref[:, lo:lo + ML_PW]
        if j == 0:
            pb = pb * (ML_DIM ** -0.5)
        b_out_ref[:, j * ML_PW:(j + 1) * ML_PW] = pb.astype(bf16)
    lo = W_A + W_B
    pin_ref[...] = jnp.dot(h, w_ref[:, lo:lo + POOL_WIDTH], preferred_element_type=f32) + b_ref[:, lo:lo + POOL_WIDTH]
    g_ref[...] = jnp.dot(h, wg_ref[...], preferred_element_type=f32) + bg_ref[...]
    t = _nt(wt_ref[...], h) + bt_ref[...]
    kt_ref[...] = t[0:ML_PW].astype(bf16)
    gt_ref[...] = t[ML_PW:N_TCOLS]


def _in_proj(x, mods, n1, w, b, wg, bg, wt, bt):
    tm = TOK_TILE
    const = lambda i: (0, 0)
    return pl.pallas_call(
        _in_kernel,
        grid=(N_TOK // tm,),
        in_specs=[pl.BlockSpec((tm, D_MODEL), lambda i: (i, 0)),
                  pl.BlockSpec((1, 6, D_MODEL), lambda i: (_mod_row(i, tm), 0, 0)),
                  pl.BlockSpec((1, D_MODEL), const),
                  pl.BlockSpec((D_MODEL, W_MAIN), const),
                  pl.BlockSpec((1, W_MAIN), const),
                  pl.BlockSpec((D_MODEL, LANE), const),
                  pl.BlockSpec((1, LANE), const),
                  pl.BlockSpec((N_TCOLS, D_MODEL), const),
                  pl.BlockSpec((N_TCOLS, 1), const)],
        out_specs=[pl.BlockSpec((tm, W_A), lambda i: (i, 0)),
                   pl.BlockSpec((tm, 2 * NA_WIDTH), lambda i: (i, 0)),
                   pl.BlockSpec((tm, W_B), lambda i: (i, 0)),
                   pl.BlockSpec((ML_PW, tm), lambda i: (0, i)),
                   pl.BlockSpec((tm, LANE), lambda i: (i, 0)),
                   pl.BlockSpec((N_GATE_COLS, tm), lambda i: (0, i)),
                   pl.BlockSpec((tm, POOL_WIDTH), lambda i: (i, 0))],
        out_shape=[jax.ShapeDtypeStruct((N_TOK, W_A), bf16),
                   jax.ShapeDtypeStruct((N_TOK, 2 * NA_WIDTH), f32),
                   jax.ShapeDtypeStruct((N_TOK, W_B), bf16),
                   jax.ShapeDtypeStruct((ML_PW, N_TOK), bf16),
                   jax.ShapeDtypeStruct((N_TOK, LANE), f32),
                   jax.ShapeDtypeStruct((N_GATE_COLS, N_TOK), f32),
                   jax.ShapeDtypeStruct((N_TOK, POOL_WIDTH), f32)],
        compiler_params=pltpu.CompilerParams(dimension_semantics=("arbitrary",), vmem_limit_bytes=VMEM_LIMIT),
        name="in_proj",
    )(x, mods, n1, w, b, wg, bg, wt, bt)


def _pair_attention(qp, parts):
    lane = lax.broadcasted_iota(jnp.int32, (1, LANE), 1)
    outs = []
    for j in range(2):
        in_half = (lane >= j * NA_DIM) & (lane < (j + 1) * NA_DIM)
        qm = jnp.where(in_half, qp, jnp.zeros_like(qp))
        scores = []
        for k, _, bias in parts:
            s = _nt(qm, k)
            if bias is not None:
                s = s + bias[j]
            scores.append(s)
        m = scores[0].max(axis=-1, keepdims=True)
        for s in scores[1:]:
            m = jnp.maximum(m, s.max(axis=-1, keepdims=True))
        den = None
        acc = None
        for s, (_, v, _) in zip(scores, parts):
            p = jnp.exp(s - m)
            ps = p.sum(axis=-1, keepdims=True)
            den = ps if den is None else den + ps
            o = jnp.dot(p.astype(bf16), v, preferred_element_type=f32)
            acc = o if acc is None else acc + o
        outs.append(acc / den)
    return jnp.where(lane < NA_DIM, outs[0], outs[1])


def _ctx_attn_kernel(q_ref, k_ref, v_ref, o_ref):
    for p in range(NA_PAIRS):
        sl = slice(p * LANE, (p + 1) * LANE)
        o = _pair_attention(q_ref[0, :, sl], [(k_ref[0, :, sl], v_ref[0, :, sl], None)])
        o_ref[0, :, sl] = o.astype(bf16)


def _ctx_attention(qkv):
    blk = lambda c: pl.BlockSpec((1, SEQ, NA_WIDTH), lambda b, c=c: (b, 0, c))
    return pl.pallas_call(
        _ctx_attn_kernel,
        grid=(BATCH,),
        in_specs=[blk(0), blk(1), blk(2)],
        out_specs=pl.BlockSpec((1, SEQ, NA_WIDTH), lambda b: (b, 0, 0)),
        out_shape=jax.ShapeDtypeStruct((BATCH, SEQ, NA_WIDTH), bf16),
        name="ctx_attention",
    )(qkv, qkv, qkv)


def _na_window_start(rb):
    return jnp.clip(rb * NA_QROWS - NA_ROWS // 2, 0, DEC_SEQ // GRID_W - NA_KROWS)


def _na_kernel(q_ref, k_ref, v_ref, ck_ref, cv_ref, bias_ref, o_ref):
    rb = pl.program_id(1)
    start = pl.multiple_of(_na_window_start(rb) * GRID_W, GRID_W)
    nk = NA_KROWS * GRID_W
    for p in range(NA_PAIRS):
        sl = slice(p * LANE, (p + 1) * LANE)
        parts = [(k_ref[0, pl.ds(start, nk), sl], v_ref[0, pl.ds(start, nk), sl], bias_ref[0, 2 * p:2 * p + 2]),
                 (ck_ref[0, :, sl], cv_ref[0, :, sl], None)]
        o = _pair_attention(q_ref[0, :, sl], parts)
        o_ref[0, :, sl] = o.astype(bf16)


def _na_bias_variant(rb):
    n_rb = DEC_SEQ // GRID_W // NA_QROWS
    return jnp.where(rb == 0, 0, jnp.where(rb == n_rb - 1, 2, 1))


def _neighborhood_attention(qkv, ck, cv, bias, layer):
    nq = NA_QROWS * GRID_W
    nk = NA_KROWS * GRID_W
    n_rb = DEC_SEQ // nq
    return pl.pallas_call(
        _na_kernel,
        grid=(DEC_BATCH, n_rb),
        in_specs=[pl.BlockSpec((1, nq, NA_WIDTH), lambda b, r: (1 + b, r, 0)),
                  pl.BlockSpec((1, DEC_SEQ, NA_WIDTH), lambda b, r: (1 + b, 0, 1)),
                  pl.BlockSpec((1, DEC_SEQ, NA_WIDTH), lambda b, r: (1 + b, 0, 2)),
                  pl.BlockSpec((1, PAST_LEN, NA_WIDTH), lambda b, r: (b, 0, 0)),
                  pl.BlockSpec((1, PAST_LEN, NA_WIDTH), lambda b, r: (b, 0, 0)),
                  pl.BlockSpec((1, NA_HEADS, nq, nk), lambda b, r: (3 * layer + _na_bias_variant(r), 0, 0, 0))],
        out_specs=pl.BlockSpec((1, nq, NA_WIDTH), lambda b, r: (b, r, 0)),
        out_shape=jax.ShapeDtypeStruct((DEC_BATCH, DEC_SEQ, NA_WIDTH), bf16),
        compiler_params=pltpu.CompilerParams(dimension_semantics=("arbitrary", "arbitrary"),
                                             vmem_limit_bytes=VMEM_LIMIT),
        name="neighborhood_attention",
    )(qkv, qkv, qkv, ck, cv, bias)


def _na_bias_tables(rpb):
    rows = DEC_SEQ // GRID_W
    n_rb = rows // NA_QROWS
    qc = np.arange(GRID_W)[:, None]
    kc = np.arange(GRID_W)[None, :]
    dc = np.clip(kc - qc + NA_COLS - 1, 0, RPB_COLS - 1)
    col_start = np.clip(qc - NA_COLS // 2, 0, GRID_W - NA_COLS)
    col_ok = (kc >= col_start) & (kc < col_start + NA_COLS)
    pick_col = (dc[None] == np.arange(RPB_COLS)[:, None, None]).astype(np.float32)
    by_col = jnp.einsum('lhab,bqk->lhqak', rpb.astype(f32), pick_col, precision=HI)
    by_col = jnp.where(col_ok[None, None, :, None, :], by_col, NEG_INF)
    by_col = jnp.pad(by_col, ((0, 0), (0, 0), (0, 0), (NA_KROWS, NA_KROWS), (0, 0)), constant_values=NEG_INF)
    tables = []
    for rb in (0, 1, n_rb - 1):
        r0 = rb * NA_QROWS
        ws = int(np.clip(r0 - NA_ROWS // 2, 0, rows - NA_KROWS))
        per_qrow = []
        for dq in range(NA_QROWS):
            qr = r0 + dq
            kr = ws + np.arange(NA_KROWS)
            row_start = int(np.clip(qr - NA_ROWS // 2, 0, rows - NA_ROWS))
            row_ok = (kr >= row_start) & (kr < row_start + NA_ROWS)
            a0 = ws - qr + NA_ROWS - 1 + NA_KROWS
            t = jnp.where(row_ok[None, None, None, :, None], by_col[:, :, :, a0:a0 + NA_KROWS, :], NEG_INF)
            per_qrow.append(t.reshape(DEPTH, NA_HEADS, GRID_W, NA_KROWS * GRID_W))
        tables.append(jnp.concatenate(per_qrow, axis=2))
    return jnp.stack(tables, axis=1).reshape(DEPTH * 3, NA_HEADS, NA_QROWS * GRID_W, NA_KROWS * GRID_W)


def _log_sigmoid(x):
    return -(jnp.maximum(-x, 0.0) + jnp.log(1.0 + jnp.exp(-jnp.abs(x))))


def _split3(x):
    hi = x.astype(bf16)
    r1 = x - hi.astype(f32)
    mid = r1.astype(bf16)
    lo = (r1 - mid.astype(f32)).astype(bf16)
    return hi, mid, lo


def _mlstm_kernel(qf_ref, vf_ref, ktf_ref, gf_ref, gtf_ref, qb_ref, vb_ref, ktb_ref, gb_ref, gtb_ref,
                  c0_ref, m0_ref, hf_ref, hb_ref, c_out_ref, m_out_ref, c_scr, m_scr):
    L = ML_CHUNK
    _, c, n_chunks, _ = _ml_schedule(pl.program_id(0))

    @pl.when(c == 0)
    def _():
        c_scr[...] = c0_ref[0]
        m_scr[...] = m0_ref[0]

    ri = lax.broadcasted_iota(jnp.int32, (L, L), 0)
    ci = lax.broadcasted_iota(jnp.int32, (L, L), 1)
    lane = lax.broadcasted_iota(jnp.int32, (L, ML_PAD), 1)
    ones_col = jnp.where(lane == 0, 1.0, 0.0).astype(bf16)
    lower = ri >= ci
    upper = ri <= ci
    lower_b = jnp.where(lower, 1.0, 0.0).astype(bf16)
    upper_b = jnp.where(upper, 1.0, 0.0).astype(bf16)
    dirs = ((qf_ref, ktf_ref, vf_ref, gf_ref, gtf_ref, hf_ref), (qb_ref, ktb_ref, vb_ref, gb_ref, gtb_ref, hb_ref))
    for d, (q_ref, kt_ref, v_ref, g_ref, gt_ref, h_ref) in enumerate(dirs):
        g = g_ref[...][:, 0:N_GATE_COLS]
        gt = gt_ref[...]
        lf_c = _log_sigmoid(g)
        lf_r = _log_sigmoid(gt)
        b_cols = sum(jnp.dot(lower_b, part, preferred_element_type=f32) for part in _split3(lf_c))
        b_rows = sum(jnp.dot(part, upper_b, preferred_element_type=f32) for part in _split3(lf_r))
        tot_c = jnp.sum(lf_c, axis=0, keepdims=True)
        tot_r = jnp.sum(lf_r, axis=1, keepdims=True)
        visible = lower
        if d == 1:
            b_cols = tot_c - b_cols + lf_c
            b_rows = tot_r - b_rows + lf_r
            visible = upper
        for hd in range(ML_HEADS):
            st = d * ML_HEADS + hd
            ci_ = 2 * ML_HEADS * d + hd
            cf_ = ci_ + ML_HEADS
            sl = slice(hd * ML_PAD, (hd + 1) * ML_PAD)
            bc = b_cols[:, cf_:cf_ + 1]
            br = b_rows[cf_:cf_ + 1, :]
            li_r = gt[ci_:ci_ + 1, :]
            m_prev = m_scr[st:st + 1, 0:1]
            dmat = jnp.where(visible, bc - br + li_r, NEG_INF)
            inter = bc + m_prev
            m_t = jnp.maximum(inter, dmat.max(axis=-1, keepdims=True))
            w_intra = jnp.exp(dmat - m_t)
            w_inter = jnp.exp(inter - m_t)
            qh = q_ref[0, :, sl]
            kht = kt_ref[sl, :]
            v_aug = jnp.concatenate([v_ref[0, :, sl], ones_col], axis=1)
            s = (jnp.dot(qh, kht, preferred_element_type=f32) * w_intra).astype(bf16)
            c_aug = c_scr[st]
            na = (w_inter * jnp.dot(qh, c_aug.astype(bf16), preferred_element_type=f32)
                  + jnp.dot(s, v_aug, preferred_element_type=f32))
            den = na[:, ML_PAD:ML_PAD + 1]
            h_ref[0, :, sl] = na[:, 0:ML_PAD] / jnp.maximum(jnp.abs(den), jnp.exp(-m_t))
            b_end = tot_r[cf_:cf_ + 1, :]
            g_row = b_end - br + li_r
            m_new = jnp.maximum(b_end + m_prev, g_row.max(axis=1, keepdims=True))
            decay = jnp.exp(b_end + m_prev - m_new)
            kwt = (kht.astype(f32) * jnp.exp(g_row - m_new)).astype(bf16)
            c_scr[st] = decay * c_aug + jnp.dot(kwt, v_aug, preferred_element_type=f32)
            m_scr[st:st + 1, :] = jnp.broadcast_to(m_new, (1, LANE))

    @pl.when(c == n_chunks - 1)
    def _():
        c_out_ref[0] = c_scr[...]
        m_out_ref[0] = m_scr[...]


def _ml_schedule(s):
    nc_ctx, nc_lat = SEQ // ML_CHUNK, DEC_SEQ // ML_CHUNK
    n_ctx_steps = BATCH * nc_ctx
    is_ctx = s < n_ctx_steps
    t = s - n_ctx_steps
    seq = jnp.where(is_ctx, s // nc_ctx, BATCH + t // nc_lat)
    c = jnp.where(is_ctx, s % nc_ctx, t % nc_lat)
    nc = jnp.where(is_ctx, nc_ctx, nc_lat)
    base = jnp.where(is_ctx, (s // nc_ctx) * nc_ctx, n_ctx_steps + (t // nc_lat) * nc_lat)
    return seq, c, nc, base


def _mlstm(qvo, kt, gates, gates_t, c0, m0):
    L = ML_CHUNK
    n_seq = BATCH + DEC_BATCH

    def fwd(s):
        _, c, _, base = _ml_schedule(s)
        return base + c

    def bwd(s):
        _, c, nc, base = _ml_schedule(s)
        return base + nc - 1 - c

    seq_of = lambda s: _ml_schedule(s)[0]

    def specs(pos):
        return [pl.BlockSpec((1, L, ML_PW), lambda s, j=j: (pos(s), 0, j)) for j in range(2)] + [
            pl.BlockSpec((ML_PW, L), lambda s: (0, pos(s))),
            pl.BlockSpec((L, LANE), lambda s: (pos(s), 0)),
            pl.BlockSpec((N_GATE_COLS, L), lambda s: (0, pos(s)))]

    q3 = qvo.reshape(N_TOK // L, L, W_B)
    n_str = 2 * ML_HEADS
    return pl.pallas_call(
        _mlstm_kernel,
        grid=(N_TOK // L,),
        in_specs=specs(fwd) + specs(bwd) + [
            pl.BlockSpec((1, n_str, ML_PAD, CAUG), lambda s: (seq_of(s), 0, 0, 0)),
            pl.BlockSpec((1, n_str, LANE), lambda s: (seq_of(s), 0, 0))],
        out_specs=[pl.BlockSpec((1, L, ML_PW), lambda s: (fwd(s), 0, 0)),
                   pl.BlockSpec((1, L, ML_PW), lambda s: (bwd(s), 0, 0)),
                   pl.BlockSpec((1, n_str, ML_PAD, CAUG), lambda s: (seq_of(s), 0, 0, 0)),
                   pl.BlockSpec((1, n_str, LANE), lambda s: (seq_of(s), 0, 0))],
        out_shape=[jax.ShapeDtypeStruct((N_TOK // L, L, ML_PW), f32),
                   jax.ShapeDtypeStruct((N_TOK // L, L, ML_PW), f32),
                   jax.ShapeDtypeStruct((n_seq, n_str, ML_PAD, CAUG), f32),
                   jax.ShapeDtypeStruct((n_seq, n_str, LANE), f32)],
        scratch_shapes=[pltpu.VMEM((n_str, ML_PAD, CAUG), f32), pltpu.VMEM((n_str, LANE), f32)],
        compiler_params=pltpu.CompilerParams(dimension_semantics=("arbitrary",), vmem_limit_bytes=VMEM_LIMIT),
        name="mlstm",
    )(q3, q3, kt, gates, gates_t, q3, q3, kt, gates, gates_t, c0, m0)


def _pack_ml_state(C, n, m):
    B = C.shape[0]
    c_aug = jnp.zeros((B, 2, ML_HEADS, ML_PAD, CAUG), f32)
    c_aug = c_aug.at[:, :, :, :ML_DIM, :ML_DIM].set(C.astype(f32))
    c_aug = c_aug.at[:, :, :, :ML_DIM, ML_PAD].set(n.astype(f32))
    m_b = jnp.broadcast_to(m.astype(f32)[..., None], (B, 2, ML_HEADS, LANE))
    return c_aug.reshape(B, 2 * ML_HEADS, ML_PAD, CAUG), m_b.reshape(B, 2 * ML_HEADS, LANE)


def _unpack_ml_state(c_aug, m_b):
    B = c_aug.shape[0]
    c_aug = c_aug.reshape(B, 2, ML_HEADS, ML_PAD, CAUG)
    return (c_aug[:, :, :, :ML_DIM, :ML_DIM], c_aug[:, :, :, :ML_DIM, ML_PAD],
            m_b.reshape(B, 2, ML_HEADS, LANE)[..., 0])


def _pool_rows(u_prev, u_cur, u_next, w_bd, scale, t0, seq_len):
    tm = u_cur.shape[0]
    u_win = jnp.concatenate([u_prev, u_cur, u_next], axis=0)
    u_hi = u_win.astype(bf16)
    u_lo = (u_win - u_hi.astype(f32)).astype(bf16)
    t_abs = t0 + lax.broadcasted_iota(jnp.int32, (tm, 1), 0)
    s_abs = t0 - POOL_HALO + lax.broadcasted_iota(jnp.int32, (1, tm + 2 * POOL_HALO), 1)
    t_loc = t_abs & (seq_len - 1)
    seq_start = t_abs - t_loc
    lane = lax.broadcasted_iota(jnp.int32, (1, LANE), 1)
    sums, inv_cnt = [], []
    for w in POOL_WINDOWS:
        lo = jnp.maximum(t_loc - w // 2, 0)
        hi = jnp.minimum(t_loc - w // 2 + w, seq_len)
        in_win = (s_abs >= seq_start + lo) & (s_abs < seq_start + hi)
        sums.append(jnp.where(in_win, 1.0, 0.0).astype(bf16))
        inv_cnt.append(1.0 / (hi - lo).astype(f32))
    pooled = []
    for p in range(POOL_GROUPS // 2):
        sl = slice(p * LANE, (p + 1) * LANE)
        means = []
        for j in range(2):
            a = sums[2 * p + j]
            tot = (jnp.dot(a, u_hi[:, sl], preferred_element_type=f32)
                   + jnp.dot(a, u_lo[:, sl], preferred_element_type=f32))
            means.append(tot * inv_cnt[2 * p + j])
        pooled.append(jnp.where(lane < POOL_DIM, means[0], means[1]) - u_cur[:, sl])
    pooled = jnp.concatenate(pooled, axis=1).astype(bf16)
    return jnp.dot(pooled, w_bd, preferred_element_type=f32) * scale


def _top2_sum(a, b, c, d):
    hi1, lo1 = jnp.maximum(a, b), jnp.minimum(a, b)
    hi2, lo2 = jnp.maximum(c, d), jnp.minimum(c, d)
    return jnp.maximum(hi1, hi2) + jnp.maximum(jnp.minimum(hi1, hi2), jnp.maximum(lo1, lo2))


def _first_match(vals, target):
    idx = jnp.full_like(target, float(len(vals) - 1))
    for i in range(len(vals) - 2, -1, -1):
        idx = jnp.where(vals[i] == target, float(i), idx)
    return idx


def _pick(vals, idx):
    out = vals[-1]
    for i in range(len(vals) - 2, -1, -1):
        out = jnp.where(idx == float(i), vals[i], out)
    return out


def _route(logits_t, bias_t):
    scores = jax.nn.sigmoid(logits_t)
    sel = scores + bias_t
    row = lambda a, i: a[i:i + 1, :]
    grp = [_top2_sum(*[row(sel, EXPERTS_PER_GROUP * g + i) for i in range(EXPERTS_PER_GROUP)])
           for g in range(N_EXPERT_GROUPS)]
    best = functools.reduce(jnp.maximum, grp)
    gidx = _first_match(grp, best)
    sel_g = [_pick([row(sel, EXPERTS_PER_GROUP * g + i) for g in range(N_EXPERT_GROUPS)], gidx)
             for i in range(EXPERTS_PER_GROUP)]
    sco_g = [_pick([row(scores, EXPERTS_PER_GROUP * g + i) for g in range(N_EXPERT_GROUPS)], gidx)
             for i in range(EXPERTS_PER_GROUP)]
    i0 = _first_match(sel_g, functools.reduce(jnp.maximum, sel_g))
    rest = [jnp.where(i0 == float(i), -jnp.inf, sel_g[i]) for i in range(EXPERTS_PER_GROUP)]
    i1 = _first_match(rest, functools.reduce(jnp.maximum, rest))
    s0, s1 = _pick(sco_g, i0), _pick(sco_g, i1)
    tot = s0 + s1
    rid = lax.broadcasted_iota(jnp.int32, (LANE, logits_t.shape[1]), 0)
    rows = (EXPERTS_PER_GROUP * gidx + i0, EXPERTS_PER_GROUP * gidx + i1, s0 / tot, s1 / tot)
    out = jnp.zeros(rid.shape, f32)
    for i, r in enumerate(rows):
        out = jnp.where(rid == i, r, out)
    return out


def _out_kernel(x_ref, mod_ref, oac_ref, oal_ref, hf_ref, hb_ref, ob_ref, up_ref, uc_ref, un_ref, wp_ref, psc_ref,
                mln_ref, wo_ref, n2_ref, wr_ref, br_ref, x1_ref, h2_ref, rt_ref, rc_ref):
    tm = x_ref.shape[0]
    i = pl.program_id(0)
    is_ctx = i < N_CTX // tm
    mod = mod_ref[0]
    out_a = jnp.where(is_ctx, oac_ref[...], oal_ref[...])
    out_c = _pool_rows(up_ref[...], uc_ref[...], un_ref[...], wp_ref[...], psc_ref[...], i * tm,
                       jnp.where(is_ctx, SEQ, DEC_SEQ)).astype(bf16)
    hsum = hf_ref[...] + hb_ref[...]
    outs_b = []
    for hd in range(ML_HEADS):
        sl = slice(hd * ML_PAD, (hd + 1) * ML_PAD)
        hh = hsum[:, sl]
        ms = jnp.sum(hh * hh, axis=-1, keepdims=True) * (1.0 / ML_DIM)
        hn = hh * lax.rsqrt(ms + EPS) * mln_ref[:, sl]
        outs_b.append((jax.nn.sigmoid(ob_ref[:, sl].astype(f32)) * hn).astype(bf16))
    out_b = jnp.concatenate(outs_b, axis=1)
    mixed = (jnp.dot(out_a, wo_ref[0:NA_WIDTH, :], preferred_element_type=f32)
             + jnp.dot(out_b, wo_ref[NA_WIDTH:NA_WIDTH + ML_PW, :], preferred_element_type=f32)
             + jnp.dot(out_c, wo_ref[NA_WIDTH + ML_PW:, :], preferred_element_type=f32))
    x1 = x_ref[...] + mod[2:3] * mixed
    x1_ref[...] = x1
    h2 = x1 * lax.rsqrt(jnp.mean(x1 * x1, axis=-1, keepdims=True) + EPS) * n2_ref[...]
    h2 = h2 * (1.0 + mod[4:5]) + mod[3:4]
    h2_ref[...] = h2
    route_t = _route(_nt(wr_ref[...], h2, precision=HI), br_ref[...])
    rt_ref[...] = route_t[0:8]
    rc_ref[...] = route_t.T


def _out_proj(x, mods, oa_ctx, oa_lat, hf, hb, qvo, pin, w_bd, psc, mln, wo, n2, wr_t, br_t):
    tm = TOK_TILE
    const = lambda i: (0, 0)
    row = lambda i: (i, 0)
    n_ctx_tiles = N_CTX // tm
    halo_blocks = tm // POOL_HALO
    return pl.pallas_call(
        _out_kernel,
        grid=(N_TOK // tm,),
        in_specs=[pl.BlockSpec((tm, D_MODEL), row),
                  pl.BlockSpec((1, 6, D_MODEL), lambda i: (_mod_row(i, tm), 0, 0)),
                  pl.BlockSpec((tm, NA_WIDTH), lambda i: (jnp.minimum(i, n_ctx_tiles - 1), 0)),
                  pl.BlockSpec((tm, NA_WIDTH), lambda i: (jnp.maximum(i - n_ctx_tiles, 0), 0)),
                  pl.BlockSpec((tm, ML_PW), row),
                  pl.BlockSpec((tm, ML_PW), row),
                  pl.BlockSpec((tm, ML_PW), lambda i: (i, 2)),
                  pl.BlockSpec((POOL_HALO, POOL_WIDTH), lambda i: (jnp.maximum(i * halo_blocks - 1, 0), 0)),
                  pl.BlockSpec((tm, POOL_WIDTH), row),
                  pl.BlockSpec((POOL_HALO, POOL_WIDTH),
                               lambda i: (jnp.minimum((i + 1) * halo_blocks, N_TOK // POOL_HALO - 1), 0)),
                  pl.BlockSpec((POOL_WIDTH, POOL_WIDTH), const),
                  pl.BlockSpec((1, POOL_WIDTH), const),
                  pl.BlockSpec((1, ML_PW), const),
                  pl.BlockSpec((NA_WIDTH + ML_PW + POOL_WIDTH, D_MODEL), const),
                  pl.BlockSpec((1, D_MODEL), const),
                  pl.BlockSpec((N_EXPERTS, D_MODEL), const),
                  pl.BlockSpec((N_EXPERTS, 1), const)],
        out_specs=[pl.BlockSpec((tm, D_MODEL), row),
                   pl.BlockSpec((tm, D_MODEL), row),
                   pl.BlockSpec((8, tm), lambda i: (0, i)),
                   pl.BlockSpec((tm, LANE), row)],
        out_shape=[jax.ShapeDtypeStruct((N_TOK, D_MODEL), f32),
                   jax.ShapeDtypeStruct((N_TOK, D_MODEL), f32),
                   jax.ShapeDtypeStruct((8, N_TOK), f32),
                   jax.ShapeDtypeStruct((N_TOK, LANE), f32)],
        compiler_params=pltpu.CompilerParams(dimension_semantics=("arbitrary",), vmem_limit_bytes=VMEM_LIMIT),
        name="out_proj_router",
    )(x, mods, oa_ctx, oa_lat, hf, hb, qvo, pin, pin, pin, w_bd, psc, mln, wo, n2, wr_t, br_t)


def _rank_kernel(rt_ref, pos_ref, te_ref, carry_ref):
    tm = rt_ref.shape[1]
    p = pl.program_id(0)
    i = pl.program_id(1)
    rid = lax.broadcasted_iota(jnp.int32, (N_EXPERTS, tm), 0).astype(f32)
    oh0 = rid == rt_ref[0:1, :]
    oh1 = rid == rt_ref[1:2, :]
    both = jnp.where(oh0 | oh1, 1.0, 0.0)
    counts = jnp.sum(both, axis=1, keepdims=True)

    @pl.when((p == 0) & (i == 0))
    def _():
        carry_ref[...] = jnp.zeros_like(carry_ref)

    @pl.when((p == 1) & (i == 0))
    def _():
        cnt = carry_ref[...]
        padded = jnp.floor((cnt + (MOE_TILE - 1)) * (1.0 / MOE_TILE)) * MOE_TILE
        er = lax.broadcasted_iota(jnp.int32, (N_EXPERTS, N_EXPERTS), 0)
        ec = lax.broadcasted_iota(jnp.int32, (N_EXPERTS, N_EXPERTS), 1)
        off = jnp.dot(jnp.where(ec < er, 1.0, 0.0), padded, preferred_element_type=f32, precision=HI)
        carry_ref[...] = off
        total = jnp.sum(padded, axis=0, keepdims=True)
        n_used = total * (1.0 / MOE_TILE)
        tile = lax.broadcasted_iota(jnp.int32, (1, LANE), 1).astype(f32)
        row0 = jnp.minimum(tile, n_used - 1.0) * MOE_TILE
        expert = jnp.sum(jnp.where(off <= row0, 1.0, 0.0), axis=0, keepdims=True) - 1.0
        sub = lax.broadcasted_iota(jnp.int32, (8, LANE), 0)
        te_ref[...] = jnp.where(sub == 0, expert, jnp.where(sub == 1, n_used, 0.0)).astype(jnp.int32)

    @pl.when(p == 1)
    def _():
        sr = lax.broadcasted_iota(jnp.int32, (tm, tm), 0)
        sc = lax.broadcasted_iota(jnp.int32, (tm, tm), 1)
        earlier = jnp.dot(both.astype(bf16), jnp.where(sr < sc, 1.0, 0.0).astype(bf16),
                          preferred_element_type=f32)
        before = carry_ref[:, 0:1] + earlier
        pos0 = jnp.sum(jnp.where(oh0, before, 0.0), axis=0, keepdims=True)
        pos1 = jnp.sum(jnp.where(oh1, before, 0.0), axis=0, keepdims=True)
        sub = lax.broadcasted_iota(jnp.int32, (8, tm), 0)
        pos_ref[...] = jnp.where(sub == 0, pos0, jnp.where(sub == 1, pos1, 0.0)).astype(jnp.int32)

    carry_ref[...] += counts


def _rank(route_t):
    tm = TOK_TILE
    return pl.pallas_call(
        _rank_kernel,
        grid=(2, N_TOK // tm),
        in_specs=[pl.BlockSpec((8, tm), lambda p, i: (0, i))],
        out_specs=[pl.BlockSpec((8, tm), lambda p, i: (0, i * p)),
                   pl.BlockSpec((8, LANE), lambda p, i: (0, 0))],
        out_shape=[jax.ShapeDtypeStruct((8, N_TOK), jnp.int32),
                   jax.ShapeDtypeStruct((8, LANE), jnp.int32)],
        scratch_shapes=[pltpu.VMEM((N_EXPERTS, LANE), f32)],
        compiler_params=pltpu.CompilerParams(dimension_semantics=("arbitrary", "arbitrary")),
        name="moe_rank",
    )(route_t)


def _dispatch_kernel(pos0_ref, pos1_ref, h_ref, xs_in_ref, xs_ref, sem):
    del xs_in_ref
    td = h_ref.shape[0]
    base = pl.program_id(0) * td

    def row_copy(t, pos_ref):
        return pltpu.make_async_copy(h_ref.at[pl.ds(t, 1)], xs_ref.at[pl.ds(pos_ref[base + t], 1)], sem)

    def issue(t, carry):
        row_copy(t, pos0_ref).start()
        row_copy(t, pos1_ref).start()
        return carry

    lax.fori_loop(0, td, issue, 0, unroll=8)
    for _ in range(2):
        pltpu.make_async_copy(h_ref, xs_ref.at[pl.ds(0, td)], sem).wait()


def _dispatch(pos0, pos1, h2p, xs_init):
    td = DISPATCH_TILE
    return pl.pallas_call(
        _dispatch_kernel,
        grid_spec=pltpu.PrefetchScalarGridSpec(
            num_scalar_prefetch=2,
            grid=(N_TOK // td,),
            in_specs=[pl.BlockSpec((td, D_MODEL), lambda i, p0, p1: (i, 0)),
                      pl.BlockSpec(memory_space=pl.ANY)],
            out_specs=pl.BlockSpec(memory_space=pl.ANY),
            scratch_shapes=[pltpu.SemaphoreType.DMA(())]),
        out_shape=jax.ShapeDtypeStruct(xs_init.shape, xs_init.dtype),
        input_output_aliases={3: 0},
        compiler_params=pltpu.CompilerParams(dimension_semantics=("arbitrary",)),
        name="moe_dispatch",
    )(pos0, pos1, h2p, xs_init)


def _expert_kernel(te_ref, xs_ref, wg_ref, wu_ref, wd_ref, ys_ref):
    used = pl.program_id(0) < te_ref[1, 0]

    @pl.when(jnp.logical_not(used))
    def _():
        ys_ref[...] = jnp.zeros_like(ys_ref)

    @pl.when(used)
    def _():
        x = xs_ref[...].astype(bf16)
        hg = jnp.dot(x, wg_ref[0, 0].astype(bf16), preferred_element_type=f32)
        hu = jnp.dot(x, wu_ref[0, 0].astype(bf16), preferred_element_type=f32)
        hid = (hg * jax.nn.sigmoid(hg) * hu).astype(bf16)
        ys_ref[...] = jnp.dot(hid, wd_ref[0, 0].astype(bf16), preferred_element_type=f32)


def _experts(te, xs, w_gate, w_up, w_down, layer):
    tm = MOE_TILE
    row = lambda j, te: (jnp.minimum(j, te[1, 0] - 1), 0)
    wspec = lambda shape: pl.BlockSpec((1, 1) + shape, lambda j, te: (layer, te[0, j], 0, 0))
    return pl.pallas_call(
        _expert_kernel,
        grid_spec=pltpu.PrefetchScalarGridSpec(
            num_scalar_prefetch=1,
            grid=(MOE_ROWS // tm,),
            in_specs=[pl.BlockSpec((tm, D_MODEL), row),
                      wspec((D_MODEL, D_EXPERT)), wspec((D_MODEL, D_EXPERT)), wspec((D_EXPERT, D_MODEL))],
            out_specs=pl.BlockSpec((tm, D_MODEL), lambda j, te: (j, 0))),
        out_shape=jax.ShapeDtypeStruct((MOE_ROWS, D_MODEL), f32),
        compiler_params=pltpu.CompilerParams(dimension_semantics=("arbitrary",), vmem_limit_bytes=VMEM_LIMIT),
        name="moe_experts",
    )(te, xs, w_gate, w_up, w_down)


def _combine_kernel(pos0_ref, pos1_ref, ys_ref, x1_ref, rc_ref, mod_ref, fn_ref, o_ref, buf, sem, *, final):
    tc = x1_ref.shape[0]
    i = pl.program_id(0)
    slot = i % 2

    def issue(tile, sl):
        base = tile * tc

        def body(t, carry):
            for s, pos_ref in enumerate((pos0_ref, pos1_ref)):
                pltpu.make_async_copy(ys_ref.at[pl.ds(pos_ref[base + t], 1)], buf.at[sl, s, pl.ds(t, 1)],
                                      sem.at[sl]).start()
            return carry

        lax.fori_loop(0, tc, body, 0, unroll=8)

    @pl.when(i == 0)
    def _():
        issue(0, 0)

    @pl.when(i + 1 < pl.num_programs(0))
    def _():
        issue(i + 1, 1 - slot)

    for s in range(2):
        pltpu.make_async_copy(ys_ref.at[pl.ds(0, tc)], buf.at[slot, s], sem.at[slot]).wait()
    rc = rc_ref[...]
    moe = rc[:, 2:3] * buf[slot, 0] + rc[:, 3:4] * buf[slot, 1]
    x2 = x1_ref[...] + mod_ref[0][5:6] * moe
    if final:
        x2 = x2 * lax.rsqrt(jnp.mean(x2 * x2, axis=-1, keepdims=True) + EPS) * fn_ref[...]
    o_ref[...] = x2


def _combine(pos0, pos1, ys, x1, rc, mods, fn, *, final):
    tc = COMBINE_TILE
    row = lambda i, p0, p1: (i, 0)
    return pl.pallas_call(
        functools.partial(_combine_kernel, final=final),
        grid_spec=pltpu.PrefetchScalarGridSpec(
            num_scalar_prefetch=2,
            grid=(N_TOK // tc,),
            in_specs=[pl.BlockSpec(memory_space=pl.ANY),
                      pl.BlockSpec((tc, D_MODEL), row),
                      pl.BlockSpec((tc, LANE), row),
                      pl.BlockSpec((1, 6, D_MODEL), lambda i, p0, p1: (_mod_row(i, tc), 0, 0)),
                      pl.BlockSpec((1, D_MODEL), lambda i, p0, p1: (0, 0))],
            out_specs=pl.BlockSpec((tc, D_MODEL), row),
            scratch_shapes=[pltpu.VMEM((2, 2, tc, D_MODEL), f32), pltpu.SemaphoreType.DMA((2,))]),
        out_shape=jax.ShapeDtypeStruct((N_TOK, D_MODEL), f32),
        compiler_params=pltpu.CompilerParams(dimension_semantics=("arbitrary",), vmem_limit_bytes=VMEM_LIMIT),
        name="moe_combine",
    )(pos0, pos1, ys, x1, rc, mods, fn)


def _moe(h2, route_t, rc, w_gate, w_up, w_down, layer, x1, mods, fn, xs_buf, *, final):
    pos, te = _rank(route_t)
    pos0, pos1 = pos[0], pos[1]
    xs = _dispatch(pos0, pos1, h2, xs_buf)
    ys = _experts(te, xs, w_gate, w_up, w_down, layer)
    return _combine(pos0, pos1, ys, x1, rc, mods, fn, final=final), xs


def _pad_heads(w):
    lead = w.shape[:-1]
    w = w.reshape(*lead, ML_HEADS, ML_DIM)
    w = jnp.pad(w, [(0, 0)] * len(lead) + [(0, 0), (0, ML_PAD - ML_DIM)])
    return w.reshape(*lead, ML_PW)


def _pack_w_in(w, b):
    wb = jnp.concatenate([w, b[None]], axis=0)
    o = 0
    qa = wb[:, o:o + NA_WIDTH] * (NA_DIM ** -0.5)
    ka = wb[:, o + NA_WIDTH:o + 2 * NA_WIDTH]
    va = wb[:, o + 2 * NA_WIDTH:o + 3 * NA_WIDTH]
    o += 3 * NA_WIDTH
    qb, kb, vb, ob = [_pad_heads(wb[:, o + j * ML_WIDTH:o + (j + 1) * ML_WIDTH]) for j in range(4)]
    o += 4 * ML_WIDTH
    gates = wb[:, o:o + N_GATE_COLS]
    o += N_GATE_COLS
    pool = wb[:, o:o + POOL_WIDTH]
    main = jnp.concatenate([qa, ka, va, qb, vb, ob, pool], axis=1)
    gates_p = jnp.pad(gates, ((0, 0), (0, LANE - N_GATE_COLS)))
    feat_major = jnp.concatenate([kb, gates], axis=1)
    return (main[:-1].astype(bf16), main[-1:].astype(f32),
            gates_p[:-1].astype(bf16), gates_p[-1:].astype(f32),
            feat_major[:-1].T.astype(bf16), feat_major[-1][:, None].astype(f32))


def _pack_w_out(w):
    wb = w[NA_WIDTH:NA_WIDTH + ML_WIDTH].reshape(ML_HEADS, ML_DIM, D_MODEL)
    wb = jnp.pad(wb, ((0, 0), (0, ML_PAD - ML_DIM), (0, 0))).reshape(ML_PW, D_MODEL)
    return jnp.concatenate([w[:NA_WIDTH], wb, w[NA_WIDTH + ML_WIDTH:]], axis=0).astype(bf16)


def _block_diag(w):
    g, c, _ = w.shape
    eye = jnp.eye(g, dtype=w.dtype)
    return (eye[:, None, :, None] * w[:, :, None, :]).reshape(g * c, g * c)


def kernel(x_prompt, x_sample, cache_k_attn, cache_v_attn, state_mlstm_C, state_mlstm_n, state_mlstm_m, c, c_ctx,
           w_ada, b_ada, norm1, w_in, b_in, rpb, ml_norm, w_pool, pool_scale, w_out, norm2, w_router, b_router,
           w_gate, w_up, w_down, final_norm):
    dt = x_prompt.dtype
    x = jnp.concatenate([x_prompt.reshape(N_CTX, D_MODEL), x_sample.reshape(N_LAT, D_MODEL)], axis=0).astype(f32)
    cvec = jnp.concatenate([c_ctx[None], c, jnp.zeros((8 - 1 - DEC_BATCH, D_MODEL), c.dtype)], axis=0).astype(f32)
    mods_all = _ada(cvec, w_ada.astype(f32), b_ada.astype(f32))
    mods_all = mods_all[:, :1 + DEC_BATCH].reshape(DEPTH, 1 + DEC_BATCH, 6, D_MODEL)

    wr_t = w_router.astype(f32).T
    br_t = b_router.astype(f32)[:, None]
    fn = final_norm.astype(f32)[None]
    zero_c = jnp.zeros((BATCH, 2 * ML_HEADS, ML_PAD, CAUG), f32)
    zero_m = jnp.zeros((BATCH, 2 * ML_HEADS, LANE), f32)

    na_bias = _na_bias_tables(rpb)
    xs_buf = jnp.zeros((MOE_ROWS, D_MODEL), f32)

    ks, vs, Cs, ns, ms = [], [], [], [], []
    for l in range(DEPTH):
        mods = mods_all[l]
        w_main, b_main, wg, bg, wgt, bgt = _pack_w_in(w_in[l], b_in[l])
        qkva, kv32, qvo, kt, gates, gates_t, pin = _in_proj(x, mods, norm1[l].astype(f32)[None], w_main, b_main,
                                                         wg, bg, wgt, bgt)
        ks.append(kv32[:N_CTX, :NA_WIDTH].reshape(BATCH, SEQ, NA_HEADS, NA_DIM))
        vs.append(kv32[:N_CTX, NA_WIDTH:].reshape(BATCH, SEQ, NA_HEADS, NA_DIM))

        oa_ctx = _ctx_attention(qkva.reshape(N_TOK // SEQ, SEQ, W_A))
        ck = (cache_k_attn[:, l].reshape(DEC_BATCH, PAST_LEN, NA_WIDTH)).astype(bf16)
        cv = (cache_v_attn[:, l].reshape(DEC_BATCH, PAST_LEN, NA_WIDTH)).astype(bf16)
        oa_lat = _neighborhood_attention(qkva.reshape(N_TOK // DEC_SEQ, DEC_SEQ, W_A), ck, cv, na_bias, l)

        c_l, m_l = _pack_ml_state(state_mlstm_C[:, l], state_mlstm_n[:, l], state_mlstm_m[:, l])
        hf, hb, c_fin, m_fin = _mlstm(qvo, kt, gates, gates_t, jnp.concatenate([zero_c, c_l], axis=0),
                                      jnp.concatenate([zero_m, m_l], axis=0))
        C_l, n_l, m_l2 = _unpack_ml_state(c_fin[:BATCH], m_fin[:BATCH])
        Cs.append(C_l)
        ns.append(n_l)
        ms.append(m_l2)

        w_bd = _block_diag(w_pool[l].astype(f32)).astype(bf16)
        psc = pool_scale[l].astype(f32)[None]
        x1, h2, route_t, rc = _out_proj(x, mods, oa_ctx.reshape(N_CTX, NA_WIDTH), oa_lat.reshape(N_LAT, NA_WIDTH),
                                        hf.reshape(N_TOK, ML_PW), hb.reshape(N_TOK, ML_PW), qvo, pin, w_bd, psc,
                                        _pad_heads(ml_norm[l].astype(f32))[None],
                                        _pack_w_out(w_out[l]), norm2[l].astype(f32)[None], wr_t, br_t)
        x, xs_buf = _moe(h2, route_t, rc, w_gate, w_up, w_down, l, x1, mods, fn, xs_buf, final=(l == DEPTH - 1))

    y_prompt = x[:N_CTX].reshape(BATCH, SEQ, D_MODEL).astype(dt)
    y_sample = x[N_CTX:].reshape(DEC_BATCH, DEC_SEQ, D_MODEL).astype(dt)
    return (y_prompt, y_sample,
            jnp.stack(ks, axis=1).astype(dt), jnp.stack(vs, axis=1).astype(dt),
            jnp.stack(Cs, axis=1).astype(dt), jnp.stack(ns, axis=1).astype(dt), jnp.stack(ms, axis=1).astype(dt))
```

```python
import functools

import numpy as np
import jax
import jax.numpy as jnp
from jax import lax
from jax.experimental import pallas as pl
from jax.experimental.pallas import tpu as pltpu

D_MODEL = 1024
BATCH = 16
SEQ = 256
DEPTH = 4
DEC_BATCH = 2
DEC_SEQ = 4096
PAST_LEN = 256
GRID_W = 64
EPS = 1e-6
NEG_INF = -1e30
NA_HEADS = 6
NA_DIM = 64
NA_WIDTH = NA_HEADS * NA_DIM
NA_ROWS = 8
NA_COLS = 16
RPB_ROWS = 2 * NA_ROWS - 1
RPB_COLS = 2 * NA_COLS - 1
ML_HEADS = 4
ML_DIM = 96
ML_WIDTH = ML_HEADS * ML_DIM
POOL_WINDOWS = (2, 4, 8, 16)
POOL_GROUPS = 4
POOL_DIM = 64
POOL_WIDTH = POOL_GROUPS * POOL_DIM
N_GATE_COLS = 4 * ML_HEADS
N_EXPERTS = 16
N_EXPERT_GROUPS = 4
EXPERTS_PER_GROUP = N_EXPERTS // N_EXPERT_GROUPS
D_EXPERT = 512
ADA_DIM = 6 * D_MODEL

N_CTX = BATCH * SEQ
N_LAT = DEC_BATCH * DEC_SEQ
N_TOK = N_CTX + N_LAT
LANE = 128
ML_PAD = LANE
ML_PW = ML_HEADS * ML_PAD
CAUG = 2 * ML_PAD
NA_PAIRS = NA_HEADS // 2
TOK_TILE = 512
ML_CHUNK = 256
NA_QROWS = 4
NA_KROWS = NA_QROWS + NA_ROWS - 1
POOL_HALO = max(POOL_WINDOWS) // 2
MOE_TILE = 512
MOE_ROWS = 2 * N_TOK + N_EXPERTS * MOE_TILE
DISPATCH_TILE = 256
COMBINE_TILE = 256
VMEM_LIMIT = 56 * 1024 * 1024

W_A = 3 * NA_WIDTH
W_B = 3 * ML_PW
N_TCOLS = ML_PW + N_GATE_COLS
W_MAIN = W_A + W_B + POOL_WIDTH

f32 = jnp.float32
bf16 = jnp.bfloat16
HI = lax.Precision.HIGHEST


def _nt(a, b, **kw):
    return lax.dot_general(a, b, (((1,), (1,)), ((), ())), preferred_element_type=f32, **kw)


def _mod_row(i, tile):
    n_ctx_tiles = N_CTX // tile
    per_batch = DEC_SEQ // tile
    return jnp.where(i < n_ctx_tiles, 0, 1 + (i - n_ctx_tiles) // per_batch)


def _ada_kernel(c_ref, w_ref, b_ref, o_ref):
    s = c_ref[...]
    s = s * jax.nn.sigmoid(s)
    o_ref[0] = jnp.dot(s.astype(bf16), w_ref[0].astype(bf16), preferred_element_type=f32) + b_ref[0]


def _ada(cvec, w_ada, b_ada):
    nj = ADA_DIM // D_MODEL
    return pl.pallas_call(
        _ada_kernel,
        grid=(DEPTH, nj),
        in_specs=[pl.BlockSpec((8, D_MODEL), lambda l, j: (0, 0)),
                  pl.BlockSpec((1, D_MODEL, D_MODEL), lambda l, j: (l, 0, j)),
                  pl.BlockSpec((1, 1, D_MODEL), lambda l, j: (l, 0, j))],
        out_specs=pl.BlockSpec((1, 8, D_MODEL), lambda l, j: (l, 0, j)),
        out_shape=jax.ShapeDtypeStruct((DEPTH, 8, ADA_DIM), f32),
        name="ada_mod",
    )(cvec, w_ada, b_ada.reshape(DEPTH, 1, ADA_DIM))


def _in_kernel(x_ref, mod_ref, n1_ref, w_ref, b_ref, wg_ref, bg_ref, wt_ref, bt_ref,
               a_ref, kv_ref, b_out_ref, kt_ref, g_ref, gt_ref, pin_ref):
    x = x_ref[...]
    mod = mod_ref[0]
    h = x * lax.rsqrt(jnp.mean(x * x, axis=-1, keepdims=True) + EPS) * n1_ref[...]
    h = (h * (1.0 + mod[1:2]) + mod[0:1]).astype(bf16)
    pa = jnp.dot(h, w_ref[:, 0:W_A], preferred_element_type=f32) + b_ref[:, 0:W_A]
    a_ref[...] = pa.astype(bf16)
    kv_ref[...] = pa[:, NA_WIDTH:W_A]
    for j in range(3):
        lo = W_A + j * ML_PW
        pb = jnp.dot(h, w_ref[:, lo:lo + ML_PW], preferred_element_type=f32) + b_ref[:, lo:lo + ML_PW]
        if j == 0:
            pb = pb * (ML_DIM ** -0.5)
        b_out_ref[:, j * ML_PW:(j + 1) * ML_PW] = pb.astype(bf16)
    lo = W_A + W_B
    pin_ref[...] = jnp.dot(h, w_ref[:, lo:lo + POOL_WIDTH], preferred_element_type=f32) + b_ref[:, lo:lo + POOL_WIDTH]
    g_ref[...] = jnp.dot(h, wg_ref[...], preferred_element_type=f32) + bg_ref[...]
    t = _nt(wt_ref[...], h) + bt_ref[...]
    kt_ref[...] = t[0:ML_PW].astype(bf16)
    gt_ref[...] = t[ML_PW:N_TCOLS]


def _in_proj(x, mods, layer, n1, w, b, wg, bg, wt, bt):
    tm = TOK_TILE
    lyr = lambda shape: pl.BlockSpec((None,) + shape, lambda i: (layer, 0, 0))
    return pl.pallas_call(
        _in_kernel,
        grid=(N_TOK // tm,),
        in_specs=[pl.BlockSpec((tm, D_MODEL), lambda i: (i, 0)),
                  pl.BlockSpec((1, 6, D_MODEL), lambda i: (_mod_row(i, tm), 0, 0)),
                  lyr((1, D_MODEL)), lyr((D_MODEL, W_MAIN)), lyr((1, W_MAIN)), lyr((D_MODEL, LANE)), lyr((1, LANE)),
                  lyr((N_TCOLS, D_MODEL)), lyr((N_TCOLS, 1))],
        out_specs=[pl.BlockSpec((tm, W_A), lambda i: (i, 0)),
                   pl.BlockSpec((tm, 2 * NA_WIDTH), lambda i: (i, 0)),
                   pl.BlockSpec((tm, W_B), lambda i: (i, 0)),
                   pl.BlockSpec((ML_PW, tm), lambda i: (0, i)),
                   pl.BlockSpec((tm, LANE), lambda i: (i, 0)),
                   pl.BlockSpec((N_GATE_COLS, tm), lambda i: (0, i)),
                   pl.BlockSpec((tm, POOL_WIDTH), lambda i: (i, 0))],
        out_shape=[jax.ShapeDtypeStruct((N_TOK, W_A), bf16),
                   jax.ShapeDtypeStruct((N_TOK, 2 * NA_WIDTH), f32),
                   jax.ShapeDtypeStruct((N_TOK, W_B), bf16),
                   jax.ShapeDtypeStruct((ML_PW, N_TOK), bf16),
                   jax.ShapeDtypeStruct((N_TOK, LANE), f32),
                   jax.ShapeDtypeStruct((N_GATE_COLS, N_TOK), f32),
                   jax.ShapeDtypeStruct((N_TOK, POOL_WIDTH), f32)],
        compiler_params=pltpu.CompilerParams(dimension_semantics=("arbitrary",), vmem_limit_bytes=VMEM_LIMIT),
        name="in_proj",
    )(x, mods, n1, w, b, wg, bg, wt, bt)


def _pair_attention(qp, parts):
    lane = lax.broadcasted_iota(jnp.int32, (1, LANE), 1)
    outs = []
    for j in range(2):
        in_half = (lane >= j * NA_DIM) & (lane < (j + 1) * NA_DIM)
        qm = jnp.where(in_half, qp, jnp.zeros_like(qp))
        scores = []
        for k, _, bias in parts:
            s = _nt(qm, k)
            if bias is not None:
                s = s + bias[j]
            scores.append(s)
        m = scores[0].max(axis=-1, keepdims=True)
        for s in scores[1:]:
            m = jnp.maximum(m, s.max(axis=-1, keepdims=True))
        den = None
        acc = None
        for s, (_, v, _) in zip(scores, parts):
            p = jnp.exp(s - m)
            ps = p.sum(axis=-1, keepdims=True)
            den = ps if den is None else den + ps
            o = jnp.dot(p.astype(bf16), v, preferred_element_type=f32)
            acc = o if acc is None else acc + o
        outs.append(acc / den)
    return jnp.where(lane < NA_DIM, outs[0], outs[1])


def _ctx_attn_kernel(q_ref, k_ref, v_ref, o_ref):
    for p in range(NA_PAIRS):
        sl = slice(p * LANE, (p + 1) * LANE)
        o = _pair_attention(q_ref[0, :, sl], [(k_ref[0, :, sl], v_ref[0, :, sl], None)])
        o_ref[0, :, sl] = o.astype(bf16)


def _ctx_attention(qkv):
    blk = lambda c: pl.BlockSpec((1, SEQ, NA_WIDTH), lambda b, c=c: (b, 0, c))
    return pl.pallas_call(
        _ctx_attn_kernel,
        grid=(BATCH,),
        in_specs=[blk(0), blk(1), blk(2)],
        out_specs=pl.BlockSpec((1, SEQ, NA_WIDTH), lambda b: (b, 0, 0)),
        out_shape=jax.ShapeDtypeStruct((BATCH, SEQ, NA_WIDTH), bf16),
        name="ctx_attention",
    )(qkv, qkv, qkv)


def _na_window_start(rb):
    return jnp.clip(rb * NA_QROWS - NA_ROWS // 2, 0, DEC_SEQ // GRID_W - NA_KROWS)


def _na_bias(tab_ref, head, rb):
    rows = DEC_SEQ // GRID_W
    ws = _na_window_start(rb)
    lane = lax.broadcasted_iota(jnp.int32, (1, NA_KROWS * GRID_W), 1)
    per_qrow = []
    for dq in range(NA_QROWS):
        qr = rb * NA_QROWS + dq
        a0 = ws - qr + (NA_ROWS - 1) + NA_KROWS
        tiles = [tab_ref[head, a0 + 2 * j] for j in range((NA_KROWS + 1) // 2)]
        t = jnp.concatenate(tiles, axis=1)[:, :NA_KROWS * GRID_W]
        lo = (jnp.clip(qr - NA_ROWS // 2, 0, rows - NA_ROWS) - ws) * GRID_W
        ok = (lane >= lo) & (lane < lo + NA_ROWS * GRID_W)
        per_qrow.append(jnp.where(ok, t, NEG_INF))
    return jnp.concatenate(per_qrow, axis=0)


def _na_kernel(q_ref, k_ref, v_ref, ck_ref, cv_ref, tab_ref, o_ref):
    rb = pl.program_id(1)
    start = pl.multiple_of(_na_window_start(rb) * GRID_W, GRID_W)
    nk = NA_KROWS * GRID_W
    for p in range(NA_PAIRS):
        sl = slice(p * LANE, (p + 1) * LANE)
        bias = [_na_bias(tab_ref.at[0], 2 * p + j, rb) for j in range(2)]
        parts = [(k_ref[0, pl.ds(start, nk), sl], v_ref[0, pl.ds(start, nk), sl], bias),
                 (ck_ref[0, :, sl], cv_ref[0, :, sl], None)]
        o = _pair_attention(q_ref[0, :, sl], parts)
        o_ref[0, :, sl] = o.astype(bf16)


def _neighborhood_attention(qkv, ck, cv, tables, layer):
    nq = NA_QROWS * GRID_W
    n_rb = DEC_SEQ // nq
    return pl.pallas_call(
        _na_kernel,
        grid=(DEC_BATCH, n_rb),
        in_specs=[pl.BlockSpec((1, nq, NA_WIDTH), lambda b, r: (1 + b, r, 0)),
                  pl.BlockSpec((1, DEC_SEQ, NA_WIDTH), lambda b, r: (1 + b, 0, 1)),
                  pl.BlockSpec((1, DEC_SEQ, NA_WIDTH), lambda b, r: (1 + b, 0, 2)),
                  pl.BlockSpec((1, PAST_LEN, NA_WIDTH), lambda b, r: (b, 0, 0)),
                  pl.BlockSpec((1, PAST_LEN, NA_WIDTH), lambda b, r: (b, 0, 0)),
                  pl.BlockSpec((1,) + tables.shape[1:], lambda b, r: (layer, 0, 0, 0, 0))],
        out_specs=pl.BlockSpec((1, nq, NA_WIDTH), lambda b, r: (b, r, 0)),
        out_shape=jax.ShapeDtypeStruct((DEC_BATCH, DEC_SEQ, NA_WIDTH), bf16),
        compiler_params=pltpu.CompilerParams(dimension_semantics=("arbitrary", "arbitrary"),
                                             vmem_limit_bytes=VMEM_LIMIT),
        name="neighborhood_attention",
    )(qkv, qkv, qkv, ck, cv, tables)


def _na_bias_tables(rpb):
    qc = np.arange(GRID_W)[:, None]
    kc = np.arange(GRID_W)[None, :]
    dc = np.clip(kc - qc + NA_COLS - 1, 0, RPB_COLS - 1)
    col_start = np.clip(qc - NA_COLS // 2, 0, GRID_W - NA_COLS)
    col_ok = (kc >= col_start) & (kc < col_start + NA_COLS)
    pick_col = (dc[None] == np.arange(RPB_COLS)[:, None, None]).astype(np.float32)
    by_col = jnp.einsum('lhab,bqk->lhaqk', rpb.astype(f32), pick_col, precision=HI)
    by_col = jnp.where(col_ok[None, None, None], by_col, NEG_INF)
    by_col = jnp.pad(by_col, ((0, 0), (0, 0), (NA_KROWS, NA_KROWS + 1), (0, 0), (0, 0)), constant_values=NEG_INF)
    return jnp.concatenate([by_col[:, :, :-1], by_col[:, :, 1:]], axis=-1)


def _log_sigmoid(x):
    return -(jnp.maximum(-x, 0.0) + jnp.log(1.0 + jnp.exp(-jnp.abs(x))))


def _split3(x):
    hi = x.astype(bf16)
    r1 = x - hi.astype(f32)
    mid = r1.astype(bf16)
    lo = (r1 - mid.astype(f32)).astype(bf16)
    return hi, mid, lo


def _mlstm_kernel(qf_ref, vf_ref, ktf_ref, gf_ref, gtf_ref, qb_ref, vb_ref, ktb_ref, gb_ref, gtb_ref,
                  c0_ref, m0_ref, hf_ref, hb_ref, c_out_ref, m_out_ref, c_scr, m_scr):
    L = ML_CHUNK
    seq, c, n_chunks, _ = _ml_schedule(pl.program_id(0))

    @pl.when(c == 0)
    def _():
        is_ctx = seq < BATCH
        c_scr[...] = jnp.where(is_ctx, 0.0, c0_ref[0])
        m_scr[...] = jnp.where(is_ctx, 0.0, m0_ref[0])

    ri = lax.broadcasted_iota(jnp.int32, (L, L), 0)
    ci = lax.broadcasted_iota(jnp.int32, (L, L), 1)
    lane = lax.broadcasted_iota(jnp.int32, (L, ML_PAD), 1)
    ones_col = jnp.where(lane == 0, 1.0, 0.0).astype(bf16)
    lower = ri >= ci
    upper = ri <= ci
    lower_b = jnp.where(lower, 1.0, 0.0).astype(bf16)
    upper_b = jnp.where(upper, 1.0, 0.0).astype(bf16)
    dirs = ((qf_ref, ktf_ref, vf_ref, gf_ref, gtf_ref, hf_ref), (qb_ref, ktb_ref, vb_ref, gb_ref, gtb_ref, hb_ref))
    for d, (q_ref, kt_ref, v_ref, g_ref, gt_ref, h_ref) in enumerate(dirs):
        g = g_ref[...][:, 0:N_GATE_COLS]
        gt = gt_ref[...]
        lf_c = _log_sigmoid(g)
        lf_r = _log_sigmoid(gt)
        b_cols = sum(jnp.dot(lower_b, part, preferred_element_type=f32) for part in _split3(lf_c))
        b_rows = sum(jnp.dot(part, upper_b, preferred_element_type=f32) for part in _split3(lf_r))
        tot_c = jnp.sum(lf_c, axis=0, keepdims=True)
        tot_r = jnp.sum(lf_r, axis=1, keepdims=True)
        visible = lower
        if d == 1:
            b_cols = tot_c - b_cols + lf_c
            b_rows = tot_r - b_rows + lf_r
            visible = upper
        for hd in range(ML_HEADS):
            st = d * ML_HEADS + hd
            ci_ = 2 * ML_HEADS * d + hd
            cf_ = ci_ + ML_HEADS
            sl = slice(hd * ML_PAD, (hd + 1) * ML_PAD)
            bc = b_cols[:, cf_:cf_ + 1]
            br = b_rows[cf_:cf_ + 1, :]
            li_r = gt[ci_:ci_ + 1, :]
            m_prev = m_scr[st:st + 1, 0:1]
            dmat = jnp.where(visible, bc - br + li_r, NEG_INF)
            inter = bc + m_prev
            m_t = jnp.maximum(inter, dmat.max(axis=-1, keepdims=True))
            w_intra = jnp.exp(dmat - m_t)
            w_inter = jnp.exp(inter - m_t)
            qh = q_ref[0, :, sl]
            kht = kt_ref[sl, :]
            v_aug = jnp.concatenate([v_ref[0, :, sl], ones_col], axis=1)
            s = (jnp.dot(qh, kht, preferred_element_type=f32) * w_intra).astype(bf16)
            c_aug = c_scr[st]
            na = (w_inter * jnp.dot(qh, c_aug.astype(bf16), preferred_element_type=f32)
                  + jnp.dot(s, v_aug, preferred_element_type=f32))
            den = na[:, ML_PAD:ML_PAD + 1]
            h_ref[0, :, sl] = na[:, 0:ML_PAD] / jnp.maximum(jnp.abs(den), jnp.exp(-m_t))
            b_end = tot_r[cf_:cf_ + 1, :]
            g_row = b_end - br + li_r
            m_new = jnp.maximum(b_end + m_prev, g_row.max(axis=1, keepdims=True))
            decay = jnp.exp(b_end + m_prev - m_new)
            kwt = (kht.astype(f32) * jnp.exp(g_row - m_new)).astype(bf16)
            c_scr[st] = decay * c_aug + jnp.dot(kwt, v_aug, preferred_element_type=f32)
            m_scr[st:st + 1, :] = jnp.broadcast_to(m_new, (1, LANE))

    @pl.when(c == n_chunks - 1)
    def _():
        c_out_ref[0] = c_scr[...]
        m_out_ref[0] = m_scr[...]


def _ml_schedule(s):
    nc_ctx, nc_lat = SEQ // ML_CHUNK, DEC_SEQ // ML_CHUNK
    n_ctx_steps = BATCH * nc_ctx
    is_ctx = s < n_ctx_steps
    t = s - n_ctx_steps
    seq = jnp.where(is_ctx, s // nc_ctx, BATCH + t // nc_lat)
    c = jnp.where(is_ctx, s % nc_ctx, t % nc_lat)
    nc = jnp.where(is_ctx, nc_ctx, nc_lat)
    base = jnp.where(is_ctx, (s // nc_ctx) * nc_ctx, n_ctx_steps + (t // nc_lat) * nc_lat)
    return seq, c, nc, base


def _mlstm(qvo, kt, gates, gates_t, c0, m0):
    L = ML_CHUNK
    n_seq = BATCH + DEC_BATCH

    def fwd(s):
        _, c, _, base = _ml_schedule(s)
        return base + c

    def bwd(s):
        _, c, nc, base = _ml_schedule(s)
        return base + nc - 1 - c

    seq_of = lambda s: _ml_schedule(s)[0]
    lat_of = lambda s: jnp.maximum(seq_of(s) - BATCH, 0)

    def specs(pos):
        return [pl.BlockSpec((1, L, ML_PW), lambda s, j=j: (pos(s), 0, j)) for j in range(2)] + [
            pl.BlockSpec((ML_PW, L), lambda s: (0, pos(s))),
            pl.BlockSpec((L, LANE), lambda s: (pos(s), 0)),
            pl.BlockSpec((N_GATE_COLS, L), lambda s: (0, pos(s)))]

    q3 = qvo.reshape(N_TOK // L, L, W_B)
    n_str = 2 * ML_HEADS
    return pl.pallas_call(
        _mlstm_kernel,
        grid=(N_TOK // L,),
        in_specs=specs(fwd) + specs(bwd) + [
            pl.BlockSpec((1, n_str, ML_PAD, CAUG), lambda s: (lat_of(s), 0, 0, 0)),
            pl.BlockSpec((1, n_str, LANE), lambda s: (lat_of(s), 0, 0))],
        out_specs=[pl.BlockSpec((1, L, ML_PW), lambda s: (fwd(s), 0, 0)),
                   pl.BlockSpec((1, L, ML_PW), lambda s: (bwd(s), 0, 0)),
                   pl.BlockSpec((1, n_str, ML_PAD, CAUG), lambda s: (seq_of(s), 0, 0, 0)),
                   pl.BlockSpec((1, n_str, LANE), lambda s: (seq_of(s), 0, 0))],
        out_shape=[jax.ShapeDtypeStruct((N_TOK // L, L, ML_PW), f32),
                   jax.ShapeDtypeStruct((N_TOK // L, L, ML_PW), f32),
                   jax.ShapeDtypeStruct((n_seq, n_str, ML_PAD, CAUG), f32),
                   jax.ShapeDtypeStruct((n_seq, n_str, LANE), f32)],
        scratch_shapes=[pltpu.VMEM((n_str, ML_PAD, CAUG), f32), pltpu.VMEM((n_str, LANE), f32)],
        compiler_params=pltpu.CompilerParams(dimension_semantics=("arbitrary",), vmem_limit_bytes=VMEM_LIMIT),
        name="mlstm",
    )(q3, q3, kt, gates, gates_t, q3, q3, kt, gates, gates_t, c0, m0)


def _pack_ml_state(C, n, m):
    B = C.shape[0]
    c_aug = jnp.zeros((B, 2, ML_HEADS, ML_PAD, CAUG), f32)
    c_aug = c_aug.at[:, :, :, :ML_DIM, :ML_DIM].set(C.astype(f32))
    c_aug = c_aug.at[:, :, :, :ML_DIM, ML_PAD].set(n.astype(f32))
    m_b = jnp.broadcast_to(m.astype(f32)[..., None], (B, 2, ML_HEADS, LANE))
    return c_aug.reshape(B, 2 * ML_HEADS, ML_PAD, CAUG), m_b.reshape(B, 2 * ML_HEADS, LANE)


def _unpack_ml_state(c_aug, m_b):
    B = c_aug.shape[0]
    c_aug = c_aug.reshape(B, 2, ML_HEADS, ML_PAD, CAUG)
    return (c_aug[:, :, :, :ML_DIM, :ML_DIM], c_aug[:, :, :, :ML_DIM, ML_PAD],
            m_b.reshape(B, 2, ML_HEADS, LANE)[..., 0])


def _pool_rows(u_prev, u_cur, u_next, w_bd, scale, t0, seq_len):
    tm = u_cur.shape[0]
    u_win = jnp.concatenate([u_prev, u_cur, u_next], axis=0)
    u_hi = u_win.astype(bf16)
    u_lo = (u_win - u_hi.astype(f32)).astype(bf16)
    t_abs = t0 + lax.broadcasted_iota(jnp.int32, (tm, 1), 0)
    s_abs = t0 - POOL_HALO + lax.broadcasted_iota(jnp.int32, (1, tm + 2 * POOL_HALO), 1)
    t_loc = t_abs & (seq_len - 1)
    seq_start = t_abs - t_loc
    lane = lax.broadcasted_iota(jnp.int32, (1, LANE), 1)
    sums, inv_cnt = [], []
    for w in POOL_WINDOWS:
        lo = jnp.maximum(t_loc - w // 2, 0)
        hi = jnp.minimum(t_loc - w // 2 + w, seq_len)
        in_win = (s_abs >= seq_start + lo) & (s_abs < seq_start + hi)
        sums.append(jnp.where(in_win, 1.0, 0.0).astype(bf16))
        inv_cnt.append(1.0 / (hi - lo).astype(f32))
    pooled = []
    for p in range(POOL_GROUPS // 2):
        sl = slice(p * LANE, (p + 1) * LANE)
        means = []
        for j in range(2):
            a = sums[2 * p + j]
            tot = (jnp.dot(a, u_hi[:, sl], preferred_element_type=f32)
                   + jnp.dot(a, u_lo[:, sl], preferred_element_type=f32))
            means.append(tot * inv_cnt[2 * p + j])
        pooled.append(jnp.where(lane < POOL_DIM, means[0], means[1]) - u_cur[:, sl])
    pooled = jnp.concatenate(pooled, axis=1).astype(bf16)
    return jnp.dot(pooled, w_bd, preferred_element_type=f32) * scale


def _top2_sum(a, b, c, d):
    hi1, lo1 = jnp.maximum(a, b), jnp.minimum(a, b)
    hi2, lo2 = jnp.maximum(c, d), jnp.minimum(c, d)
    return jnp.maximum(hi1, hi2) + jnp.maximum(jnp.minimum(hi1, hi2), jnp.maximum(lo1, lo2))


def _first_match(vals, target):
    idx = jnp.full_like(target, float(len(vals) - 1))
    for i in range(len(vals) - 2, -1, -1):
        idx = jnp.where(vals[i] == target, float(i), idx)
    return idx


def _pick(vals, idx):
    out = vals[-1]
    for i in range(len(vals) - 2, -1, -1):
        out = jnp.where(idx == float(i), vals[i], out)
    return out


def _route(logits_t, bias_t):
    scores = jax.nn.sigmoid(logits_t)
    sel = scores + bias_t
    row = lambda a, i: a[i:i + 1, :]
    grp = [_top2_sum(*[row(sel, EXPERTS_PER_GROUP * g + i) for i in range(EXPERTS_PER_GROUP)])
           for g in range(N_EXPERT_GROUPS)]
    best = functools.reduce(jnp.maximum, grp)
    gidx = _first_match(grp, best)
    sel_g = [_pick([row(sel, EXPERTS_PER_GROUP * g + i) for g in range(N_EXPERT_GROUPS)], gidx)
             for i in range(EXPERTS_PER_GROUP)]
    sco_g = [_pick([row(scores, EXPERTS_PER_GROUP * g + i) for g in range(N_EXPERT_GROUPS)], gidx)
             for i in range(EXPERTS_PER_GROUP)]
    i0 = _first_match(sel_g, functools.reduce(jnp.maximum, sel_g))
    rest = [jnp.where(i0 == float(i), -jnp.inf, sel_g[i]) for i in range(EXPERTS_PER_GROUP)]
    i1 = _first_match(rest, functools.reduce(jnp.maximum, rest))
    s0, s1 = _pick(sco_g, i0), _pick(sco_g, i1)
    tot = s0 + s1
    rid = lax.broadcasted_iota(jnp.int32, (LANE, logits_t.shape[1]), 0)
    rows = (EXPERTS_PER_GROUP * gidx + i0, EXPERTS_PER_GROUP * gidx + i1, s0 / tot, s1 / tot)
    out = jnp.zeros(rid.shape, f32)
    for i, r in enumerate(rows):
        out = jnp.where(rid == i, r, out)
    return out


def _out_kernel(x_ref, mod_ref, oac_ref, oal_ref, hf_ref, hb_ref, ob_ref, up_ref, uc_ref, un_ref, wp_ref, psc_ref,
                mln_ref, wo_ref, n2_ref, wr_ref, br_ref, x1_ref, h2_ref, rt_ref, rc_ref):
    tm = x_ref.shape[0]
    i = pl.program_id(0)
    is_ctx = i < N_CTX // tm
    mod = mod_ref[0]
    out_a = jnp.where(is_ctx, oac_ref[...], oal_ref[...])
    out_c = _pool_rows(up_ref[...], uc_ref[...], un_ref[...], wp_ref[...], psc_ref[...], i * tm,
                       jnp.where(is_ctx, SEQ, DEC_SEQ)).astype(bf16)
    hsum = hf_ref[...] + hb_ref[...]
    outs_b = []
    for hd in range(ML_HEADS):
        sl = slice(hd * ML_PAD, (hd + 1) * ML_PAD)
        hh = hsum[:, sl]
        ms = jnp.sum(hh * hh, axis=-1, keepdims=True) * (1.0 / ML_DIM)
        hn = hh * lax.rsqrt(ms + EPS) * mln_ref[:, sl]
        outs_b.append((jax.nn.sigmoid(ob_ref[:, sl].astype(f32)) * hn).astype(bf16))
    out_b = jnp.concatenate(outs_b, axis=1)
    mixed = (jnp.dot(out_a, wo_ref[0:NA_WIDTH, :], preferred_element_type=f32)
             + jnp.dot(out_b, wo_ref[NA_WIDTH:NA_WIDTH + ML_PW, :], preferred_element_type=f32)
             + jnp.dot(out_c, wo_ref[NA_WIDTH + ML_PW:, :], preferred_element_type=f32))
    x1 = x_ref[...] + mod[2:3] * mixed
    x1_ref[...] = x1
    h2 = x1 * lax.rsqrt(jnp.mean(x1 * x1, axis=-1, keepdims=True) + EPS) * n2_ref[...]
    h2 = h2 * (1.0 + mod[4:5]) + mod[3:4]
    h2_ref[...] = h2
    route_t = _route(_nt(wr_ref[...], h2, precision=HI), br_ref[...])
    rt_ref[...] = route_t[0:8]
    rc_ref[...] = route_t.T


def _out_proj(x, mods, layer, oa_ctx, oa_lat, hf, hb, qvo, pin, w_bd, psc, mln, wo, n2, wr_t, br_t):
    tm = TOK_TILE
    const = lambda i: (0, 0)
    lyr = lambda shape: pl.BlockSpec((None,) + shape, lambda i: (layer, 0, 0))
    row = lambda i: (i, 0)
    n_ctx_tiles = N_CTX // tm
    halo_blocks = tm // POOL_HALO
    return pl.pallas_call(
        _out_kernel,
        grid=(N_TOK // tm,),
        in_specs=[pl.BlockSpec((tm, D_MODEL), row),
                  pl.BlockSpec((1, 6, D_MODEL), lambda i: (_mod_row(i, tm), 0, 0)),
                  pl.BlockSpec((tm, NA_WIDTH), lambda i: (jnp.minimum(i, n_ctx_tiles - 1), 0)),
                  pl.BlockSpec((tm, NA_WIDTH), lambda i: (jnp.maximum(i - n_ctx_tiles, 0), 0)),
                  pl.BlockSpec((tm, ML_PW), row),
                  pl.BlockSpec((tm, ML_PW), row),
                  pl.BlockSpec((tm, ML_PW), lambda i: (i, 2)),
                  pl.BlockSpec((POOL_HALO, POOL_WIDTH), lambda i: (jnp.maximum(i * halo_blocks - 1, 0), 0)),
                  pl.BlockSpec((tm, POOL_WIDTH), row),
                  pl.BlockSpec((POOL_HALO, POOL_WIDTH),
                               lambda i: (jnp.minimum((i + 1) * halo_blocks, N_TOK // POOL_HALO - 1), 0)),
                  lyr((POOL_WIDTH, POOL_WIDTH)), lyr((1, POOL_WIDTH)), lyr((1, ML_PW)),
                  lyr((NA_WIDTH + ML_PW + POOL_WIDTH, D_MODEL)), lyr((1, D_MODEL)),
                  pl.BlockSpec((N_EXPERTS, D_MODEL), const),
                  pl.BlockSpec((N_EXPERTS, 1), const)],
        out_specs=[pl.BlockSpec((tm, D_MODEL), row),
                   pl.BlockSpec((tm, D_MODEL), row),
                   pl.BlockSpec((8, tm), lambda i: (0, i)),
                   pl.BlockSpec((tm, LANE), row)],
        out_shape=[jax.ShapeDtypeStruct((N_TOK, D_MODEL), f32),
                   jax.ShapeDtypeStruct((N_TOK, D_MODEL), f32),
                   jax.ShapeDtypeStruct((8, N_TOK), f32),
                   jax.ShapeDtypeStruct((N_TOK, LANE), f32)],
        compiler_params=pltpu.CompilerParams(dimension_semantics=("arbitrary",), vmem_limit_bytes=VMEM_LIMIT),
        name="out_proj_router",
    )(x, mods, oa_ctx, oa_lat, hf, hb, qvo, pin, pin, pin, w_bd, psc, mln, wo, n2, wr_t, br_t)


def _rank_kernel(rt_ref, pos_ref, te_ref, carry_ref):
    tm = rt_ref.shape[1]
    p = pl.program_id(0)
    i = pl.program_id(1)
    rid = lax.broadcasted_iota(jnp.int32, (N_EXPERTS, tm), 0).astype(f32)
    oh0 = rid == rt_ref[0:1, :]
    oh1 = rid == rt_ref[1:2, :]
    both = jnp.where(oh0 | oh1, 1.0, 0.0)
    counts = jnp.sum(both, axis=1, keepdims=True)

    @pl.when((p == 0) & (i == 0))
    def _():
        carry_ref[...] = jnp.zeros_like(carry_ref)

    @pl.when((p == 1) & (i == 0))
    def _():
        cnt = carry_ref[...]
        padded = jnp.floor((cnt + (MOE_TILE - 1)) * (1.0 / MOE_TILE)) * MOE_TILE
        er = lax.broadcasted_iota(jnp.int32, (N_EXPERTS, N_EXPERTS), 0)
        ec = lax.broadcasted_iota(jnp.int32, (N_EXPERTS, N_EXPERTS), 1)
        off = jnp.dot(jnp.where(ec < er, 1.0, 0.0), padded, preferred_element_type=f32, precision=HI)
        carry_ref[...] = off
        total = jnp.sum(padded, axis=0, keepdims=True)
        n_used = total * (1.0 / MOE_TILE)
        tile = lax.broadcasted_iota(jnp.int32, (1, LANE), 1).astype(f32)
        row0 = jnp.minimum(tile, n_used - 1.0) * MOE_TILE
        expert = jnp.sum(jnp.where(off <= row0, 1.0, 0.0), axis=0, keepdims=True) - 1.0
        sub = lax.broadcasted_iota(jnp.int32, (8, LANE), 0)
        te_ref[...] = jnp.where(sub == 0, expert, jnp.where(sub == 1, n_used, 0.0)).astype(jnp.int32)

    @pl.when(p == 1)
    def _():
        sr = lax.broadcasted_iota(jnp.int32, (tm, tm), 0)
        sc = lax.broadcasted_iota(jnp.int32, (tm, tm), 1)
        earlier = jnp.dot(both.astype(bf16), jnp.where(sr < sc, 1.0, 0.0).astype(bf16),
                          preferred_element_type=f32)
        before = carry_ref[:, 0:1] + earlier
        pos0 = jnp.sum(jnp.where(oh0, before, 0.0), axis=0, keepdims=True)
        pos1 = jnp.sum(jnp.where(oh1, before, 0.0), axis=0, keepdims=True)
        sub = lax.broadcasted_iota(jnp.int32, (8, tm), 0)
        pos_ref[...] = jnp.where(sub == 0, pos0, jnp.where(sub == 1, pos1, 0.0)).astype(jnp.int32)

    carry_ref[...] += counts


def _rank(route_t):
    tm = TOK_TILE
    return pl.pallas_call(
        _rank_kernel,
        grid=(2, N_TOK // tm),
        in_specs=[pl.BlockSpec((8, tm), lambda p, i: (0, i))],
        out_specs=[pl.BlockSpec((8, tm), lambda p, i: (0, i * p)),
                   pl.BlockSpec((8, LANE), lambda p, i: (0, 0))],
        out_shape=[jax.ShapeDtypeStruct((8, N_TOK), jnp.int32),
                   jax.ShapeDtypeStruct((8, LANE), jnp.int32)],
        scratch_shapes=[pltpu.VMEM((N_EXPERTS, LANE), f32)],
        compiler_params=pltpu.CompilerParams(dimension_semantics=("arbitrary", "arbitrary")),
        name="moe_rank",
    )(route_t)


def _dispatch_kernel(pos0_ref, pos1_ref, h_ref, xs_in_ref, xs_ref, sem):
    del xs_in_ref
    td = h_ref.shape[0]
    base = pl.program_id(0) * td

    def row_copy(t, pos_ref):
        return pltpu.make_async_copy(h_ref.at[pl.ds(t, 1)], xs_ref.at[pl.ds(pos_ref[base + t], 1)], sem)

    def issue(t, carry):
        row_copy(t, pos0_ref).start()
        row_copy(t, pos1_ref).start()
        return carry

    lax.fori_loop(0, td, issue, 0, unroll=8)
    for _ in range(2):
        pltpu.make_async_copy(h_ref, xs_ref.at[pl.ds(0, td)], sem).wait()


def _dispatch(pos0, pos1, h2p, xs_init):
    td = DISPATCH_TILE
    return pl.pallas_call(
        _dispatch_kernel,
        grid_spec=pltpu.PrefetchScalarGridSpec(
            num_scalar_prefetch=2,
            grid=(N_TOK // td,),
            in_specs=[pl.BlockSpec((td, D_MODEL), lambda i, p0, p1: (i, 0)),
                      pl.BlockSpec(memory_space=pl.ANY)],
            out_specs=pl.BlockSpec(memory_space=pl.ANY),
            scratch_shapes=[pltpu.SemaphoreType.DMA(())]),
        out_shape=jax.ShapeDtypeStruct(xs_init.shape, xs_init.dtype),
        input_output_aliases={3: 0},
        compiler_params=pltpu.CompilerParams(dimension_semantics=("arbitrary",)),
        name="moe_dispatch",
    )(pos0, pos1, h2p, xs_init)


def _expert_kernel(te_ref, xs_ref, wg_ref, wu_ref, wd_ref, ys_ref):
    used = pl.program_id(0) < te_ref[1, 0]

    @pl.when(jnp.logical_not(used))
    def _():
        ys_ref[...] = jnp.zeros_like(ys_ref)

    @pl.when(used)
    def _():
        x = xs_ref[...].astype(bf16)
        hg = jnp.dot(x, wg_ref[0, 0].astype(bf16), preferred_element_type=f32)
        hu = jnp.dot(x, wu_ref[0, 0].astype(bf16), preferred_element_type=f32)
        hid = (hg * jax.nn.sigmoid(hg) * hu).astype(bf16)
        ys_ref[...] = jnp.dot(hid, wd_ref[0, 0].astype(bf16), preferred_element_type=f32)


def _experts(te, xs, w_gate, w_up, w_down, layer):
    tm = MOE_TILE
    row = lambda j, te: (jnp.minimum(j, te[1, 0] - 1), 0)
    wspec = lambda shape: pl.BlockSpec((1, 1) + shape, lambda j, te: (layer, te[0, j], 0, 0))
    return pl.pallas_call(
        _expert_kernel,
        grid_spec=pltpu.PrefetchScalarGridSpec(
            num_scalar_prefetch=1,
            grid=(MOE_ROWS // tm,),
            in_specs=[pl.BlockSpec((tm, D_MODEL), row),
                      wspec((D_MODEL, D_EXPERT)), wspec((D_MODEL, D_EXPERT)), wspec((D_EXPERT, D_MODEL))],
            out_specs=pl.BlockSpec((tm, D_MODEL), lambda j, te: (j, 0))),
        out_shape=jax.ShapeDtypeStruct((MOE_ROWS, D_MODEL), f32),
        compiler_params=pltpu.CompilerParams(dimension_semantics=("arbitrary",), vmem_limit_bytes=VMEM_LIMIT),
        name="moe_experts",
    )(te, xs, w_gate, w_up, w_down)


def _combine_kernel(pos0_ref, pos1_ref, ys_ref, x1_ref, rc_ref, mod_ref, fn_ref, o_ref, buf, sem, *, final):
    tc = x1_ref.shape[0]
    i = pl.program_id(0)
    slot = i % 2

    def issue(tile, sl):
        base = tile * tc

        def body(t, carry):
            for s, pos_ref in enumerate((pos0_ref, pos1_ref)):
                pltpu.make_async_copy(ys_ref.at[pl.ds(pos_ref[base + t], 1)], buf.at[sl, s, pl.ds(t, 1)],
                                      sem.at[sl]).start()
            return carry

        lax.fori_loop(0, tc, body, 0, unroll=8)

    @pl.when(i == 0)
    def _():
        issue(0, 0)

    @pl.when(i + 1 < pl.num_programs(0))
    def _():
        issue(i + 1, 1 - slot)

    for s in range(2):
        pltpu.make_async_copy(ys_ref.at[pl.ds(0, tc)], buf.at[slot, s], sem.at[slot]).wait()
    rc = rc_ref[...]
    moe = rc[:, 2:3] * buf[slot, 0] + rc[:, 3:4] * buf[slot, 1]
    x2 = x1_ref[...] + mod_ref[0][5:6] * moe
    if final:
        x2 = x2 * lax.rsqrt(jnp.mean(x2 * x2, axis=-1, keepdims=True) + EPS) * fn_ref[...]
    o_ref[...] = x2


def _combine(pos0, pos1, ys, x1, rc, mods, fn, *, final):
    tc = COMBINE_TILE
    row = lambda i, p0, p1: (i, 0)
    return pl.pallas_call(
        functools.partial(_combine_kernel, final=final),
        grid_spec=pltpu.PrefetchScalarGridSpec(
            num_scalar_prefetch=2,
            grid=(N_TOK // tc,),
            in_specs=[pl.BlockSpec(memory_space=pl.ANY),
                      pl.BlockSpec((tc, D_MODEL), row),
                      pl.BlockSpec((tc, LANE), row),
                      pl.BlockSpec((1, 6, D_MODEL), lambda i, p0, p1: (_mod_row(i, tc), 0, 0)),
                      pl.BlockSpec((1, D_MODEL), lambda i, p0, p1: (0, 0))],
            out_specs=pl.BlockSpec((tc, D_MODEL), row),
            scratch_shapes=[pltpu.VMEM((2, 2, tc, D_MODEL), f32), pltpu.SemaphoreType.DMA((2,))]),
        out_shape=jax.ShapeDtypeStruct((N_TOK, D_MODEL), f32),
        compiler_params=pltpu.CompilerParams(dimension_semantics=("arbitrary",), vmem_limit_bytes=VMEM_LIMIT),
        name="moe_combine",
    )(pos0, pos1, ys, x1, rc, mods, fn)


def _moe(h2, route_t, rc, w_gate, w_up, w_down, layer, x1, mods, fn, xs_buf, *, final):
    pos, te = _rank(route_t)
    pos0, pos1 = pos[0], pos[1]
    xs = _dispatch(pos0, pos1, h2, xs_buf)
    ys = _experts(te, xs, w_gate, w_up, w_down, layer)
    return _combine(pos0, pos1, ys, x1, rc, mods, fn, final=final), xs


def _pad_heads(w):
    lead = w.shape[:-1]
    w = w.reshape(*lead, ML_HEADS, ML_DIM)
    w = jnp.pad(w, [(0, 0)] * len(lead) + [(0, 0), (0, ML_PAD - ML_DIM)])
    return w.reshape(*lead, ML_PW)


def _pack_w_in(w, b):
    wb = jnp.concatenate([w, b[:, None]], axis=1)
    o = 0
    qa = wb[..., o:o + NA_WIDTH] * (NA_DIM ** -0.5)
    ka = wb[..., o + NA_WIDTH:o + 2 * NA_WIDTH]
    va = wb[..., o + 2 * NA_WIDTH:o + 3 * NA_WIDTH]
    o += 3 * NA_WIDTH
    qb, kb, vb, ob = [_pad_heads(wb[..., o + j * ML_WIDTH:o + (j + 1) * ML_WIDTH]) for j in range(4)]
    o += 4 * ML_WIDTH
    gates = wb[..., o:o + N_GATE_COLS]
    o += N_GATE_COLS
    pool = wb[..., o:o + POOL_WIDTH]
    main = jnp.concatenate([qa, ka, va, qb, vb, ob, pool], axis=-1)
    gates_p = jnp.pad(gates, ((0, 0), (0, 0), (0, LANE - N_GATE_COLS)))
    feat_major = jnp.swapaxes(jnp.concatenate([kb, gates], axis=-1), 1, 2)
    return (main[:, :-1].astype(bf16), main[:, -1:].astype(f32),
            gates_p[:, :-1].astype(bf16), gates_p[:, -1:].astype(f32),
            feat_major[:, :, :-1].astype(bf16), feat_major[:, :, -1:].astype(f32))


def _pack_w_out(w):
    n_l = w.shape[0]
    wb = w[:, NA_WIDTH:NA_WIDTH + ML_WIDTH].reshape(n_l, ML_HEADS, ML_DIM, D_MODEL)
    wb = jnp.pad(wb, ((0, 0), (0, 0), (0, ML_PAD - ML_DIM), (0, 0))).reshape(n_l, ML_PW, D_MODEL)
    return jnp.concatenate([w[:, :NA_WIDTH], wb, w[:, NA_WIDTH + ML_WIDTH:]], axis=1).astype(bf16)


def _block_diag(w):
    n_l, g, c, _ = w.shape
    eye = jnp.eye(g, dtype=w.dtype)
    return (eye[None, :, None, :, None] * w[:, :, :, None, :]).reshape(n_l, g * c, g * c)


def kernel(x_prompt, x_sample, cache_k_attn, cache_v_attn, state_mlstm_C, state_mlstm_n, state_mlstm_m, c, c_ctx,
           w_ada, b_ada, norm1, w_in, b_in, rpb, ml_norm, w_pool, pool_scale, w_out, norm2, w_router, b_router,
           w_gate, w_up, w_down, final_norm):
    dt = x_prompt.dtype
    x = jnp.concatenate([x_prompt.reshape(N_CTX, D_MODEL), x_sample.reshape(N_LAT, D_MODEL)], axis=0).astype(f32)
    cvec = jnp.concatenate([c_ctx[None], c, jnp.zeros((8 - 1 - DEC_BATCH, D_MODEL), c.dtype)], axis=0).astype(f32)
    mods_all = _ada(cvec, w_ada.astype(f32), b_ada.astype(f32))
    mods_all = mods_all[:, :1 + DEC_BATCH].reshape(DEPTH, 1 + DEC_BATCH, 6, D_MODEL)

    wr_t = w_router.astype(f32).T
    br_t = b_router.astype(f32)[:, None]
    fn = final_norm.astype(f32)[None]

    na_bias = _na_bias_tables(rpb)
    xs_buf = jnp.zeros((MOE_ROWS, D_MODEL), f32)
    in_params = (norm1.astype(f32)[:, None],) + _pack_w_in(w_in, b_in)
    out_params = (_block_diag(w_pool.astype(f32)).astype(bf16), pool_scale.astype(f32)[:, None],
                  _pad_heads(ml_norm.astype(f32))[:, None], _pack_w_out(w_out), norm2.astype(f32)[:, None])

    ks, vs, Cs, ns, ms = [], [], [], [], []
    for l in range(DEPTH):
        mods = mods_all[l]
        qkva, kv32, qvo, kt, gates, gates_t, pin = _in_proj(x, mods, l, *in_params)
        ks.append(kv32[:N_CTX, :NA_WIDTH].reshape(BATCH, SEQ, NA_HEADS, NA_DIM))
        vs.append(kv32[:N_CTX, NA_WIDTH:].reshape(BATCH, SEQ, NA_HEADS, NA_DIM))

        oa_ctx = _ctx_attention(qkva.reshape(N_TOK // SEQ, SEQ, W_A))
        ck = (cache_k_attn[:, l].reshape(DEC_BATCH, PAST_LEN, NA_WIDTH)).astype(bf16)
        cv = (cache_v_attn[:, l].reshape(DEC_BATCH, PAST_LEN, NA_WIDTH)).astype(bf16)
        oa_lat = _neighborhood_attention(qkva.reshape(N_TOK // DEC_SEQ, DEC_SEQ, W_A), ck, cv, na_bias, l)

        c_l, m_l = _pack_ml_state(state_mlstm_C[:, l], state_mlstm_n[:, l], state_mlstm_m[:, l])
        hf, hb, c_fin, m_fin = _mlstm(qvo, kt, gates, gates_t, c_l, m_l)
        C_l, n_l, m_l2 = _unpack_ml_state(c_fin[:BATCH], m_fin[:BATCH])
        Cs.append(C_l)
        ns.append(n_l)
        ms.append(m_l2)

        x1, h2, route_t, rc = _out_proj(x, mods, l, oa_ctx.reshape(N_CTX, NA_WIDTH), oa_lat.reshape(N_LAT, NA_WIDTH),
                                        hf.reshape(N_TOK, ML_PW), hb.reshape(N_TOK, ML_PW), qvo, pin,
                                        *out_params, wr_t, br_t)
        x, xs_buf = _moe(h2, route_t, rc, w_gate, w_up, w_down, l, x1, mods, fn, xs_buf, final=(l == DEPTH - 1))

    y_prompt = x[:N_CTX].reshape(BATCH, SEQ, D_MODEL).astype(dt)
    y_sample = x[N_CTX:].reshape(DEC_BATCH, DEC_SEQ, D_MODEL).astype(dt)
    return (y_prompt, y_sample,
            jnp.stack(ks, axis=1).astype(dt), jnp.stack(vs, axis=1).astype(dt),
            jnp.stack(Cs, axis=1).astype(dt), jnp.stack(ns, axis=1).astype(dt), jnp.stack(ms, axis=1).astype(dt))
```

```python
import functools

import numpy as np
import jax
import jax.numpy as jnp
from jax import lax
from jax.experimental import pallas as pl
from jax.experimental.pallas import tpu as pltpu

D_MODEL = 1024
BATCH = 16
SEQ = 256
DEPTH = 4
DEC_BATCH = 2
DEC_SEQ = 4096
PAST_LEN = 256
GRID_W = 64
EPS = 1e-6
NEG_INF = -1e30
NA_HEADS = 6
NA_DIM = 64
NA_WIDTH = NA_HEADS * NA_DIM
NA_ROWS = 8
NA_COLS = 16
RPB_ROWS = 2 * NA_ROWS - 1
RPB_COLS = 2 * NA_COLS - 1
ML_HEADS = 4
ML_DIM = 96
ML_WIDTH = ML_HEADS * ML_DIM
POOL_WINDOWS = (2, 4, 8, 16)
POOL_GROUPS = 4
POOL_DIM = 64
POOL_WIDTH = POOL_GROUPS * POOL_DIM
N_GATE_COLS = 4 * ML_HEADS
N_EXPERTS = 16
N_EXPERT_GROUPS = 4
EXPERTS_PER_GROUP = N_EXPERTS // N_EXPERT_GROUPS
D_EXPERT = 512
ADA_DIM = 6 * D_MODEL

N_CTX = BATCH * SEQ
N_LAT = DEC_BATCH * DEC_SEQ
N_TOK = N_CTX + N_LAT
LANE = 128
ML_PAD = LANE
ML_PW = ML_HEADS * ML_PAD
CAUG = 2 * ML_PAD
NA_PAIRS = NA_HEADS // 2
TOK_TILE = 512
ML_CHUNK = 256
NA_QROWS = 4
NA_KROWS = NA_QROWS + NA_ROWS - 1
POOL_HALO = max(POOL_WINDOWS) // 2
POOL_BLOCK = 128
MOE_TILE = 512
MOE_ROWS = 2 * N_TOK + N_EXPERTS * MOE_TILE
DISPATCH_TILE = 256
COMBINE_TILE = 256
VMEM_LIMIT = 56 * 1024 * 1024

W_A = 3 * NA_WIDTH
W_B = 3 * ML_PW
N_TCOLS = ML_PW + N_GATE_COLS
W_MAIN = W_A + W_B + POOL_WIDTH

f32 = jnp.float32
bf16 = jnp.bfloat16
HI = lax.Precision.HIGHEST


def _nt(a, b, **kw):
    return lax.dot_general(a, b, (((1,), (1,)), ((), ())), preferred_element_type=f32, **kw)


def _mod_row(i, tile):
    n_ctx_tiles = N_CTX // tile
    per_batch = DEC_SEQ // tile
    return jnp.where(i < n_ctx_tiles, 0, 1 + (i - n_ctx_tiles) // per_batch)


def _ada_kernel(c_ref, w_ref, b_ref, o_ref):
    s = c_ref[...]
    s = s * jax.nn.sigmoid(s)
    o_ref[0] = jnp.dot(s.astype(bf16), w_ref[0].astype(bf16), preferred_element_type=f32) + b_ref[0]


def _ada(cvec, w_ada, b_ada):
    nj = ADA_DIM // D_MODEL
    return pl.pallas_call(
        _ada_kernel,
        grid=(DEPTH, nj),
        in_specs=[pl.BlockSpec((8, D_MODEL), lambda l, j: (0, 0)),
                  pl.BlockSpec((1, D_MODEL, D_MODEL), lambda l, j: (l, 0, j)),
                  pl.BlockSpec((1, 1, D_MODEL), lambda l, j: (l, 0, j))],
        out_specs=pl.BlockSpec((1, 8, D_MODEL), lambda l, j: (l, 0, j)),
        out_shape=jax.ShapeDtypeStruct((DEPTH, 8, ADA_DIM), f32),
        name="ada_mod",
    )(cvec, w_ada, b_ada.reshape(DEPTH, 1, ADA_DIM))


def _in_kernel(x_ref, mod_ref, n1_ref, w_ref, b_ref, wg_ref, bg_ref, wt_ref, bt_ref,
               a_ref, kv_ref, b_out_ref, kt_ref, g_ref, gt_ref, pin_ref):
    x = x_ref[...]
    mod = mod_ref[0]
    h = x * lax.rsqrt(jnp.mean(x * x, axis=-1, keepdims=True) + EPS) * n1_ref[...]
    h = (h * (1.0 + mod[1:2]) + mod[0:1]).astype(bf16)
    pa = jnp.dot(h, w_ref[:, 0:W_A], preferred_element_type=f32) + b_ref[:, 0:W_A]
    a_ref[...] = pa.astype(bf16)
    kv_ref[...] = pa[:, NA_WIDTH:W_A]
    for j in range(3):
        lo = W_A + j * ML_PW
        pb = jnp.dot(h, w_ref[:, lo:lo + ML_PW], preferred_element_type=f32) + b_ref[:, lo:lo + ML_PW]
        if j == 0:
            pb = pb * (ML_DIM ** -0.5)
        b_out_ref[:, j * ML_PW:(j + 1) * ML_PW] = pb.astype(bf16)
    lo = W_A + W_B
    pin_ref[...] = jnp.dot(h, w_ref[:, lo:lo + POOL_WIDTH], preferred_element_type=f32) + b_ref[:, lo:lo + POOL_WIDTH]
    g_ref[...] = jnp.dot(h, wg_ref[...], preferred_element_type=f32) + bg_ref[...]
    t = _nt(wt_ref[...], h) + bt_ref[...]
    kt_ref[...] = t[0:ML_PW].astype(bf16)
    gt_ref[...] = t[ML_PW:N_TCOLS]


def _in_proj(x, mods, layer, n1, w, b, wg, bg, wt, bt):
    tm = TOK_TILE
    lyr = lambda shape: pl.BlockSpec((None,) + shape, lambda i: (layer, 0, 0))
    return pl.pallas_call(
        _in_kernel,
        grid=(N_TOK // tm,),
        in_specs=[pl.BlockSpec((tm, D_MODEL), lambda i: (i, 0)),
                  pl.BlockSpec((1, 6, D_MODEL), lambda i: (_mod_row(i, tm), 0, 0)),
                  lyr((1, D_MODEL)), lyr((D_MODEL, W_MAIN)), lyr((1, W_MAIN)), lyr((D_MODEL, LANE)), lyr((1, LANE)),
                  lyr((N_TCOLS, D_MODEL)), lyr((N_TCOLS, 1))],
        out_specs=[pl.BlockSpec((tm, W_A), lambda i: (i, 0)),
                   pl.BlockSpec((tm, 2 * NA_WIDTH), lambda i: (i, 0)),
                   pl.BlockSpec((tm, W_B), lambda i: (i, 0)),
                   pl.BlockSpec((ML_PW, tm), lambda i: (0, i)),
                   pl.BlockSpec((tm, LANE), lambda i: (i, 0)),
                   pl.BlockSpec((N_GATE_COLS, tm), lambda i: (0, i)),
                   pl.BlockSpec((tm, POOL_WIDTH), lambda i: (i, 0))],
        out_shape=[jax.ShapeDtypeStruct((N_TOK, W_A), bf16),
                   jax.ShapeDtypeStruct((N_TOK, 2 * NA_WIDTH), f32),
                   jax.ShapeDtypeStruct((N_TOK, W_B), bf16),
                   jax.ShapeDtypeStruct((ML_PW, N_TOK), bf16),
                   jax.ShapeDtypeStruct((N_TOK, LANE), f32),
                   jax.ShapeDtypeStruct((N_GATE_COLS, N_TOK), f32),
                   jax.ShapeDtypeStruct((N_TOK, POOL_WIDTH), f32)],
        compiler_params=pltpu.CompilerParams(dimension_semantics=("arbitrary",), vmem_limit_bytes=VMEM_LIMIT),
        name="in_proj",
    )(x, mods, n1, w, b, wg, bg, wt, bt)


def _pair_attention(qp, parts):
    lane = lax.broadcasted_iota(jnp.int32, (1, LANE), 1)
    outs = []
    for j in range(2):
        in_half = (lane >= j * NA_DIM) & (lane < (j + 1) * NA_DIM)
        qm = jnp.where(in_half, qp, jnp.zeros_like(qp))
        scores = []
        for k, _, bias in parts:
            s = _nt(qm, k)
            if bias is not None:
                s = s + bias[j]
            scores.append(s)
        m = scores[0].max(axis=-1, keepdims=True)
        for s in scores[1:]:
            m = jnp.maximum(m, s.max(axis=-1, keepdims=True))
        den = None
        acc = None
        for s, (_, v, _) in zip(scores, parts):
            p = jnp.exp(s - m)
            ps = p.sum(axis=-1, keepdims=True)
            den = ps if den is None else den + ps
            o = jnp.dot(p.astype(bf16), v, preferred_element_type=f32)
            acc = o if acc is None else acc + o
        outs.append(acc / den)
    return jnp.where(lane < NA_DIM, outs[0], outs[1])


def _ctx_attn_kernel(q_ref, k_ref, v_ref, o_ref):
    for p in range(NA_PAIRS):
        sl = slice(p * LANE, (p + 1) * LANE)
        o = _pair_attention(q_ref[0, :, sl], [(k_ref[0, :, sl], v_ref[0, :, sl], None)])
        o_ref[0, :, sl] = o.astype(bf16)


def _ctx_attention(qkv):
    blk = lambda c: pl.BlockSpec((1, SEQ, NA_WIDTH), lambda b, c=c: (b, 0, c))
    return pl.pallas_call(
        _ctx_attn_kernel,
        grid=(BATCH,),
        in_specs=[blk(0), blk(1), blk(2)],
        out_specs=pl.BlockSpec((1, SEQ, NA_WIDTH), lambda b: (b, 0, 0)),
        out_shape=jax.ShapeDtypeStruct((BATCH, SEQ, NA_WIDTH), bf16),
        name="ctx_attention",
    )(qkv, qkv, qkv)


def _na_window_start(rb):
    return jnp.clip(rb * NA_QROWS - NA_ROWS // 2, 0, DEC_SEQ // GRID_W - NA_KROWS)


def _na_bias(tab_ref, head, rb):
    rows = DEC_SEQ // GRID_W
    ws = _na_window_start(rb)
    lane = lax.broadcasted_iota(jnp.int32, (1, NA_KROWS * GRID_W), 1)
    per_qrow = []
    for dq in range(NA_QROWS):
        qr = rb * NA_QROWS + dq
        a0 = ws - qr + (NA_ROWS - 1) + NA_KROWS
        tiles = [tab_ref[head, a0 + 2 * j] for j in range((NA_KROWS + 1) // 2)]
        t = jnp.concatenate(tiles, axis=1)[:, :NA_KROWS * GRID_W]
        lo = (jnp.clip(qr - NA_ROWS // 2, 0, rows - NA_ROWS) - ws) * GRID_W
        ok = (lane >= lo) & (lane < lo + NA_ROWS * GRID_W)
        per_qrow.append(jnp.where(ok, t, NEG_INF))
    return jnp.concatenate(per_qrow, axis=0)


def _na_kernel(q_ref, k_ref, v_ref, ck_ref, cv_ref, tab_ref, o_ref):
    rb = pl.program_id(1)
    start = pl.multiple_of(_na_window_start(rb) * GRID_W, GRID_W)
    nk = NA_KROWS * GRID_W
    for p in range(NA_PAIRS):
        sl = slice(p * LANE, (p + 1) * LANE)
        bias = [_na_bias(tab_ref.at[0], 2 * p + j, rb) for j in range(2)]
        parts = [(k_ref[0, pl.ds(start, nk), sl], v_ref[0, pl.ds(start, nk), sl], bias),
                 (ck_ref[0, :, sl], cv_ref[0, :, sl], None)]
        o = _pair_attention(q_ref[0, :, sl], parts)
        o_ref[0, :, sl] = o.astype(bf16)


def _neighborhood_attention(qkv, ck, cv, tables, layer):
    nq = NA_QROWS * GRID_W
    n_rb = DEC_SEQ // nq
    return pl.pallas_call(
        _na_kernel,
        grid=(DEC_BATCH, n_rb),
        in_specs=[pl.BlockSpec((1, nq, NA_WIDTH), lambda b, r: (1 + b, r, 0)),
                  pl.BlockSpec((1, DEC_SEQ, NA_WIDTH), lambda b, r: (1 + b, 0, 1)),
                  pl.BlockSpec((1, DEC_SEQ, NA_WIDTH), lambda b, r: (1 + b, 0, 2)),
                  pl.BlockSpec((1, PAST_LEN, NA_WIDTH), lambda b, r: (b, 0, 0)),
                  pl.BlockSpec((1, PAST_LEN, NA_WIDTH), lambda b, r: (b, 0, 0)),
                  pl.BlockSpec((1,) + tables.shape[1:], lambda b, r: (layer, 0, 0, 0, 0))],
        out_specs=pl.BlockSpec((1, nq, NA_WIDTH), lambda b, r: (b, r, 0)),
        out_shape=jax.ShapeDtypeStruct((DEC_BATCH, DEC_SEQ, NA_WIDTH), bf16),
        compiler_params=pltpu.CompilerParams(dimension_semantics=("arbitrary", "arbitrary"),
                                             vmem_limit_bytes=VMEM_LIMIT),
        name="neighborhood_attention",
    )(qkv, qkv, qkv, ck, cv, tables)


def _na_bias_tables(rpb):
    qc = np.arange(GRID_W)[:, None]
    kc = np.arange(GRID_W)[None, :]
    dc = np.clip(kc - qc + NA_COLS - 1, 0, RPB_COLS - 1)
    col_start = np.clip(qc - NA_COLS // 2, 0, GRID_W - NA_COLS)
    col_ok = (kc >= col_start) & (kc < col_start + NA_COLS)
    pick_col = (dc[None] == np.arange(RPB_COLS)[:, None, None]).astype(np.float32)
    by_col = jnp.einsum('lhab,bqk->lhaqk', rpb.astype(f32), pick_col, precision=HI)
    by_col = jnp.where(col_ok[None, None, None], by_col, NEG_INF)
    by_col = jnp.pad(by_col, ((0, 0), (0, 0), (NA_KROWS, NA_KROWS + 1), (0, 0), (0, 0)), constant_values=NEG_INF)
    return jnp.concatenate([by_col[:, :, :-1], by_col[:, :, 1:]], axis=-1)


def _log_sigmoid(x):
    return -(jnp.maximum(-x, 0.0) + jnp.log(1.0 + jnp.exp(-jnp.abs(x))))


def _split3(x):
    hi = x.astype(bf16)
    r1 = x - hi.astype(f32)
    mid = r1.astype(bf16)
    lo = (r1 - mid.astype(f32)).astype(bf16)
    return hi, mid, lo


def _mlstm_kernel(qf_ref, vf_ref, ktf_ref, gf_ref, gtf_ref, qb_ref, vb_ref, ktb_ref, gb_ref, gtb_ref,
                  c0_ref, m0_ref, hf_ref, hb_ref, c_out_ref, m_out_ref, c_scr, m_scr):
    L = ML_CHUNK
    seq, c, n_chunks, _ = _ml_schedule(pl.program_id(0))

    @pl.when(c == 0)
    def _():
        is_ctx = seq < BATCH
        c_scr[...] = jnp.where(is_ctx, 0.0, c0_ref[0])
        m_scr[...] = jnp.where(is_ctx, 0.0, m0_ref[0])

    ri = lax.broadcasted_iota(jnp.int32, (L, L), 0)
    ci = lax.broadcasted_iota(jnp.int32, (L, L), 1)
    lane = lax.broadcasted_iota(jnp.int32, (L, ML_PAD), 1)
    ones_col = jnp.where(lane == 0, 1.0, 0.0).astype(bf16)
    lower = ri >= ci
    upper = ri <= ci
    lower_b = jnp.where(lower, 1.0, 0.0).astype(bf16)
    upper_b = jnp.where(upper, 1.0, 0.0).astype(bf16)
    dirs = ((qf_ref, ktf_ref, vf_ref, gf_ref, gtf_ref, hf_ref), (qb_ref, ktb_ref, vb_ref, gb_ref, gtb_ref, hb_ref))
    for d, (q_ref, kt_ref, v_ref, g_ref, gt_ref, h_ref) in enumerate(dirs):
        g = g_ref[...][:, 0:N_GATE_COLS]
        gt = gt_ref[...]
        lf_c = _log_sigmoid(g)
        lf_r = _log_sigmoid(gt)
        b_cols = sum(jnp.dot(lower_b, part, preferred_element_type=f32) for part in _split3(lf_c))
        b_rows = sum(jnp.dot(part, upper_b, preferred_element_type=f32) for part in _split3(lf_r))
        tot_c = jnp.sum(lf_c, axis=0, keepdims=True)
        tot_r = jnp.sum(lf_r, axis=1, keepdims=True)
        visible = lower
        if d == 1:
            b_cols = tot_c - b_cols + lf_c
            b_rows = tot_r - b_rows + lf_r
            visible = upper
        for hd in range(ML_HEADS):
            st = d * ML_HEADS + hd
            ci_ = 2 * ML_HEADS * d + hd
            cf_ = ci_ + ML_HEADS
            sl = slice(hd * ML_PAD, (hd + 1) * ML_PAD)
            bc = b_cols[:, cf_:cf_ + 1]
            br = b_rows[cf_:cf_ + 1, :]
            li_r = gt[ci_:ci_ + 1, :]
            m_prev = m_scr[st:st + 1, 0:1]
            dmat = jnp.where(visible, bc - br + li_r, NEG_INF)
            inter = bc + m_prev
            m_t = jnp.maximum(inter, dmat.max(axis=-1, keepdims=True))
            w_intra = jnp.exp(dmat - m_t)
            w_inter = jnp.exp(inter - m_t)
            qh = q_ref[0, :, sl]
            kht = kt_ref[sl, :]
            v_aug = jnp.concatenate([v_ref[0, :, sl], ones_col], axis=1)
            s = (jnp.dot(qh, kht, preferred_element_type=f32) * w_intra).astype(bf16)
            c_aug = c_scr[st]
            na = (w_inter * jnp.dot(qh, c_aug.astype(bf16), preferred_element_type=f32)
                  + jnp.dot(s, v_aug, preferred_element_type=f32))
            den = na[:, ML_PAD:ML_PAD + 1]
            h_ref[0, :, sl] = na[:, 0:ML_PAD] / jnp.maximum(jnp.abs(den), jnp.exp(-m_t))
            b_end = tot_r[cf_:cf_ + 1, :]
            g_row = b_end - br + li_r
            m_new = jnp.maximum(b_end + m_prev, g_row.max(axis=1, keepdims=True))
            decay = jnp.exp(b_end + m_prev - m_new)
            kwt = (kht.astype(f32) * jnp.exp(g_row - m_new)).astype(bf16)
            c_scr[st] = decay * c_aug + jnp.dot(kwt, v_aug, preferred_element_type=f32)
            m_scr[st:st + 1, :] = jnp.broadcast_to(m_new, (1, LANE))

    @pl.when(c == n_chunks - 1)
    def _():
        c_out_ref[0] = c_scr[...]
        m_out_ref[0] = m_scr[...]


def _ml_schedule(s):
    nc_ctx, nc_lat = SEQ // ML_CHUNK, DEC_SEQ // ML_CHUNK
    n_ctx_steps = BATCH * nc_ctx
    is_ctx = s < n_ctx_steps
    t = s - n_ctx_steps
    seq = jnp.where(is_ctx, s // nc_ctx, BATCH + t // nc_lat)
    c = jnp.where(is_ctx, s % nc_ctx, t % nc_lat)
    nc = jnp.where(is_ctx, nc_ctx, nc_lat)
    base = jnp.where(is_ctx, (s // nc_ctx) * nc_ctx, n_ctx_steps + (t // nc_lat) * nc_lat)
    return seq, c, nc, base


def _mlstm(qvo, kt, gates, gates_t, c0, m0):
    L = ML_CHUNK
    n_seq = BATCH + DEC_BATCH

    def fwd(s):
        _, c, _, base = _ml_schedule(s)
        return base + c

    def bwd(s):
        _, c, nc, base = _ml_schedule(s)
        return base + nc - 1 - c

    seq_of = lambda s: _ml_schedule(s)[0]
    lat_of = lambda s: jnp.maximum(seq_of(s) - BATCH, 0)

    def specs(pos):
        return [pl.BlockSpec((1, L, ML_PW), lambda s, j=j: (pos(s), 0, j)) for j in range(2)] + [
            pl.BlockSpec((ML_PW, L), lambda s: (0, pos(s))),
            pl.BlockSpec((L, LANE), lambda s: (pos(s), 0)),
            pl.BlockSpec((N_GATE_COLS, L), lambda s: (0, pos(s)))]

    q3 = qvo.reshape(N_TOK // L, L, W_B)
    n_str = 2 * ML_HEADS
    return pl.pallas_call(
        _mlstm_kernel,
        grid=(N_TOK // L,),
        in_specs=specs(fwd) + specs(bwd) + [
            pl.BlockSpec((1, n_str, ML_PAD, CAUG), lambda s: (lat_of(s), 0, 0, 0)),
            pl.BlockSpec((1, n_str, LANE), lambda s: (lat_of(s), 0, 0))],
        out_specs=[pl.BlockSpec((1, L, ML_PW), lambda s: (fwd(s), 0, 0)),
                   pl.BlockSpec((1, L, ML_PW), lambda s: (bwd(s), 0, 0)),
                   pl.BlockSpec((1, n_str, ML_PAD, CAUG), lambda s: (seq_of(s), 0, 0, 0)),
                   pl.BlockSpec((1, n_str, LANE), lambda s: (seq_of(s), 0, 0))],
        out_shape=[jax.ShapeDtypeStruct((N_TOK // L, L, ML_PW), f32),
                   jax.ShapeDtypeStruct((N_TOK // L, L, ML_PW), f32),
                   jax.ShapeDtypeStruct((n_seq, n_str, ML_PAD, CAUG), f32),
                   jax.ShapeDtypeStruct((n_seq, n_str, LANE), f32)],
        scratch_shapes=[pltpu.VMEM((n_str, ML_PAD, CAUG), f32), pltpu.VMEM((n_str, LANE), f32)],
        compiler_params=pltpu.CompilerParams(dimension_semantics=("arbitrary",), vmem_limit_bytes=VMEM_LIMIT),
        name="mlstm",
    )(q3, q3, kt, gates, gates_t, q3, q3, kt, gates, gates_t, c0, m0)


def _pack_ml_state(C, n, m):
    B = C.shape[0]
    c_aug = jnp.zeros((B, 2, ML_HEADS, ML_PAD, CAUG), f32)
    c_aug = c_aug.at[:, :, :, :ML_DIM, :ML_DIM].set(C.astype(f32))
    c_aug = c_aug.at[:, :, :, :ML_DIM, ML_PAD].set(n.astype(f32))
    m_b = jnp.broadcast_to(m.astype(f32)[..., None], (B, 2, ML_HEADS, LANE))
    return c_aug.reshape(B, 2 * ML_HEADS, ML_PAD, CAUG), m_b.reshape(B, 2 * ML_HEADS, LANE)


def _unpack_ml_state(c_aug, m_b):
    B = c_aug.shape[0]
    c_aug = c_aug.reshape(B, 2, ML_HEADS, ML_PAD, CAUG)
    return (c_aug[:, :, :, :ML_DIM, :ML_DIM], c_aug[:, :, :, :ML_DIM, ML_PAD],
            m_b.reshape(B, 2, ML_HEADS, LANE)[..., 0])


def _pool_rows(u_prev, u_cur, u_next, w_bd, scale, t0, seq_len):
    tm = u_cur.shape[0]
    u_win = jnp.concatenate([u_prev, u_cur, u_next], axis=0)
    u_hi = u_win.astype(bf16)
    u_lo = (u_win - u_hi.astype(f32)).astype(bf16)
    lane = lax.broadcasted_iota(jnp.int32, (1, LANE), 1)
    blocks = []
    for r0 in range(0, tm, POOL_BLOCK):
        win = slice(r0, r0 + POOL_BLOCK + 2 * POOL_HALO)
        t_abs = t0 + r0 + lax.broadcasted_iota(jnp.int32, (POOL_BLOCK, 1), 0)
        s_abs = t0 + r0 - POOL_HALO + lax.broadcasted_iota(jnp.int32, (1, POOL_BLOCK + 2 * POOL_HALO), 1)
        t_loc = t_abs & (seq_len - 1)
        seq_start = t_abs - t_loc
        means = []
        for w in POOL_WINDOWS:
            lo = jnp.maximum(t_loc - w // 2, 0)
            hi = jnp.minimum(t_loc - w // 2 + w, seq_len)
            in_win = (s_abs >= seq_start + lo) & (s_abs < seq_start + hi)
            means.append((jnp.where(in_win, 1.0, 0.0).astype(bf16), 1.0 / (hi - lo).astype(f32)))
        pooled = []
        for p in range(POOL_GROUPS // 2):
            sl = slice(p * LANE, (p + 1) * LANE)
            halves = []
            for a, inv_cnt in means[2 * p:2 * p + 2]:
                tot = (jnp.dot(a, u_hi[win, sl], preferred_element_type=f32)
                       + jnp.dot(a, u_lo[win, sl], preferred_element_type=f32))
                halves.append(tot * inv_cnt)
            pooled.append(jnp.where(lane < POOL_DIM, halves[0], halves[1]) - u_cur[r0:r0 + POOL_BLOCK, sl])
        blocks.append(jnp.concatenate(pooled, axis=1))
    pooled = jnp.concatenate(blocks, axis=0).astype(bf16)
    return jnp.dot(pooled, w_bd, preferred_element_type=f32) * scale


def _top2_sum(a, b, c, d):
    hi1, lo1 = jnp.maximum(a, b), jnp.minimum(a, b)
    hi2, lo2 = jnp.maximum(c, d), jnp.minimum(c, d)
    return jnp.maximum(hi1, hi2) + jnp.maximum(jnp.minimum(hi1, hi2), jnp.maximum(lo1, lo2))


def _first_match(vals, target):
    idx = jnp.full_like(target, float(len(vals) - 1))
    for i in range(len(vals) - 2, -1, -1):
        idx = jnp.where(vals[i] == target, float(i), idx)
    return idx


def _pick(vals, idx):
    out = vals[-1]
    for i in range(len(vals) - 2, -1, -1):
        out = jnp.where(idx == float(i), vals[i], out)
    return out


def _route(logits_t, bias_t):
    scores = jax.nn.sigmoid(logits_t)
    sel = scores + bias_t
    row = lambda a, i: a[i:i + 1, :]
    grp = [_top2_sum(*[row(sel, EXPERTS_PER_GROUP * g + i) for i in range(EXPERTS_PER_GROUP)])
           for g in range(N_EXPERT_GROUPS)]
    best = functools.reduce(jnp.maximum, grp)
    gidx = _first_match(grp, best)
    sel_g = [_pick([row(sel, EXPERTS_PER_GROUP * g + i) for g in range(N_EXPERT_GROUPS)], gidx)
             for i in range(EXPERTS_PER_GROUP)]
    sco_g = [_pick([row(scores, EXPERTS_PER_GROUP * g + i) for g in range(N_EXPERT_GROUPS)], gidx)
             for i in range(EXPERTS_PER_GROUP)]
    i0 = _first_match(sel_g, functools.reduce(jnp.maximum, sel_g))
    rest = [jnp.where(i0 == float(i), -jnp.inf, sel_g[i]) for i in range(EXPERTS_PER_GROUP)]
    i1 = _first_match(rest, functools.reduce(jnp.maximum, rest))
    s0, s1 = _pick(sco_g, i0), _pick(sco_g, i1)
    tot = s0 + s1
    rid = lax.broadcasted_iota(jnp.int32, (LANE, logits_t.shape[1]), 0)
    rows = (EXPERTS_PER_GROUP * gidx + i0, EXPERTS_PER_GROUP * gidx + i1, s0 / tot, s1 / tot)
    out = jnp.zeros(rid.shape, f32)
    for i, r in enumerate(rows):
        out = jnp.where(rid == i, r, out)
    return out


def _out_kernel(x_ref, mod_ref, oac_ref, oal_ref, hf_ref, hb_ref, ob_ref, up_ref, uc_ref, un_ref, wp_ref, psc_ref,
                mln_ref, wo_ref, n2_ref, wr_ref, br_ref, x1_ref, h2_ref, rt_ref, rc_ref):
    tm = x_ref.shape[0]
    i = pl.program_id(0)
    is_ctx = i < N_CTX // tm
    mod = mod_ref[0]
    out_a = jnp.where(is_ctx, oac_ref[...], oal_ref[...])
    out_c = _pool_rows(up_ref[...], uc_ref[...], un_ref[...], wp_ref[...], psc_ref[...], i * tm,
                       jnp.where(is_ctx, SEQ, DEC_SEQ)).astype(bf16)
    hsum = hf_ref[...] + hb_ref[...]
    outs_b = []
    for hd in range(ML_HEADS):
        sl = slice(hd * ML_PAD, (hd + 1) * ML_PAD)
        hh = hsum[:, sl]
        ms = jnp.sum(hh * hh, axis=-1, keepdims=True) * (1.0 / ML_DIM)
        hn = hh * lax.rsqrt(ms + EPS) * mln_ref[:, sl]
        outs_b.append((jax.nn.sigmoid(ob_ref[:, sl].astype(f32)) * hn).astype(bf16))
    out_b = jnp.concatenate(outs_b, axis=1)
    mixed = (jnp.dot(out_a, wo_ref[0:NA_WIDTH, :], preferred_element_type=f32)
             + jnp.dot(out_b, wo_ref[NA_WIDTH:NA_WIDTH + ML_PW, :], preferred_element_type=f32)
             + jnp.dot(out_c, wo_ref[NA_WIDTH + ML_PW:, :], preferred_element_type=f32))
    x1 = x_ref[...] + mod[2:3] * mixed
    x1_ref[...] = x1
    h2 = x1 * lax.rsqrt(jnp.mean(x1 * x1, axis=-1, keepdims=True) + EPS) * n2_ref[...]
    h2 = h2 * (1.0 + mod[4:5]) + mod[3:4]
    h2_ref[...] = h2
    route_t = _route(_nt(wr_ref[...], h2, precision=HI), br_ref[...])
    rt_ref[...] = route_t[0:8]
    rc_ref[...] = route_t.T


def _out_proj(x, mods, layer, oa_ctx, oa_lat, hf, hb, qvo, pin, w_bd, psc, mln, wo, n2, wr_t, br_t):
    tm = TOK_TILE
    const = lambda i: (0, 0)
    lyr = lambda shape: pl.BlockSpec((None,) + shape, lambda i: (layer, 0, 0))
    row = lambda i: (i, 0)
    n_ctx_tiles = N_CTX // tm
    halo_blocks = tm // POOL_HALO
    return pl.pallas_call(
        _out_kernel,
        grid=(N_TOK // tm,),
        in_specs=[pl.BlockSpec((tm, D_MODEL), row),
                  pl.BlockSpec((1, 6, D_MODEL), lambda i: (_mod_row(i, tm), 0, 0)),
                  pl.BlockSpec((tm, NA_WIDTH), lambda i: (jnp.minimum(i, n_ctx_tiles - 1), 0)),
                  pl.BlockSpec((tm, NA_WIDTH), lambda i: (jnp.maximum(i - n_ctx_tiles, 0), 0)),
                  pl.BlockSpec((tm, ML_PW), row),
                  pl.BlockSpec((tm, ML_PW), row),
                  pl.BlockSpec((tm, ML_PW), lambda i: (i, 2)),
                  pl.BlockSpec((POOL_HALO, POOL_WIDTH), lambda i: (jnp.maximum(i * halo_blocks - 1, 0), 0)),
                  pl.BlockSpec((tm, POOL_WIDTH), row),
                  pl.BlockSpec((POOL_HALO, POOL_WIDTH),
                               lambda i: (jnp.minimum((i + 1) * halo_blocks, N_TOK // POOL_HALO - 1), 0)),
                  lyr((POOL_WIDTH, POOL_WIDTH)), lyr((1, POOL_WIDTH)), lyr((1, ML_PW)),
                  lyr((NA_WIDTH + ML_PW + POOL_WIDTH, D_MODEL)), lyr((1, D_MODEL)),
                  pl.BlockSpec((N_EXPERTS, D_MODEL), const),
                  pl.BlockSpec((N_EXPERTS, 1), const)],
        out_specs=[pl.BlockSpec((tm, D_MODEL), row),
                   pl.BlockSpec((tm, D_MODEL), row),
                   pl.BlockSpec((8, tm), lambda i: (0, i)),
                   pl.BlockSpec((tm, LANE), row)],
        out_shape=[jax.ShapeDtypeStruct((N_TOK, D_MODEL), f32),
                   jax.ShapeDtypeStruct((N_TOK, D_MODEL), f32),
                   jax.ShapeDtypeStruct((8, N_TOK), f32),
                   jax.ShapeDtypeStruct((N_TOK, LANE), f32)],
        compiler_params=pltpu.CompilerParams(dimension_semantics=("arbitrary",), vmem_limit_bytes=VMEM_LIMIT),
        name="out_proj_router",
    )(x, mods, oa_ctx, oa_lat, hf, hb, qvo, pin, pin, pin, w_bd, psc, mln, wo, n2, wr_t, br_t)


def _rank_kernel(rt_ref, pos_ref, te_ref, carry_ref):
    tm = rt_ref.shape[1]
    p = pl.program_id(0)
    i = pl.program_id(1)
    rid = lax.broadcasted_iota(jnp.int32, (N_EXPERTS, tm), 0).astype(f32)
    oh0 = rid == rt_ref[0:1, :]
    oh1 = rid == rt_ref[1:2, :]
    both = jnp.where(oh0 | oh1, 1.0, 0.0)
    counts = jnp.sum(both, axis=1, keepdims=True)

    @pl.when((p == 0) & (i == 0))
    def _():
        carry_ref[...] = jnp.zeros_like(carry_ref)

    @pl.when((p == 1) & (i == 0))
    def _():
        cnt = carry_ref[...]
        padded = jnp.floor((cnt + (MOE_TILE - 1)) * (1.0 / MOE_TILE)) * MOE_TILE
        er = lax.broadcasted_iota(jnp.int32, (N_EXPERTS, N_EXPERTS), 0)
        ec = lax.broadcasted_iota(jnp.int32, (N_EXPERTS, N_EXPERTS), 1)
        off = jnp.dot(jnp.where(ec < er, 1.0, 0.0), padded, preferred_element_type=f32, precision=HI)
        carry_ref[...] = off
        total = jnp.sum(padded, axis=0, keepdims=True)
        n_used = total * (1.0 / MOE_TILE)
        tile = lax.broadcasted_iota(jnp.int32, (1, LANE), 1).astype(f32)
        row0 = jnp.minimum(tile, n_used - 1.0) * MOE_TILE
        expert = jnp.sum(jnp.where(off <= row0, 1.0, 0.0), axis=0, keepdims=True) - 1.0
        sub = lax.broadcasted_iota(jnp.int32, (8, LANE), 0)
        te_ref[...] = jnp.where(sub == 0, expert, jnp.where(sub == 1, n_used, 0.0)).astype(jnp.int32)

    @pl.when(p == 1)
    def _():
        sr = lax.broadcasted_iota(jnp.int32, (tm, tm), 0)
        sc = lax.broadcasted_iota(jnp.int32, (tm, tm), 1)
        earlier = jnp.dot(both.astype(bf16), jnp.where(sr < sc, 1.0, 0.0).astype(bf16),
                          preferred_element_type=f32)
        before = carry_ref[:, 0:1] + earlier
        pos0 = jnp.sum(jnp.where(oh0, before, 0.0), axis=0, keepdims=True)
        pos1 = jnp.sum(jnp.where(oh1, before, 0.0), axis=0, keepdims=True)
        sub = lax.broadcasted_iota(jnp.int32, (8, tm), 0)
        pos_ref[...] = jnp.where(sub == 0, pos0, jnp.where(sub == 1, pos1, 0.0)).astype(jnp.int32)

    carry_ref[...] += counts


def _rank(route_t):
    tm = TOK_TILE
    return pl.pallas_call(
        _rank_kernel,
        grid=(2, N_TOK // tm),
        in_specs=[pl.BlockSpec((8, tm), lambda p, i: (0, i))],
        out_specs=[pl.BlockSpec((8, tm), lambda p, i: (0, i * p)),
                   pl.BlockSpec((8, LANE), lambda p, i: (0, 0))],
        out_shape=[jax.ShapeDtypeStruct((8, N_TOK), jnp.int32),
                   jax.ShapeDtypeStruct((8, LANE), jnp.int32)],
        scratch_shapes=[pltpu.VMEM((N_EXPERTS, LANE), f32)],
        compiler_params=pltpu.CompilerParams(dimension_semantics=("arbitrary", "arbitrary")),
        name="moe_rank",
    )(route_t)


def _dispatch_kernel(pos0_ref, pos1_ref, h_ref, xs_in_ref, xs_ref, sem):
    del xs_in_ref
    td = h_ref.shape[0]
    base = pl.program_id(0) * td

    def row_copy(t, pos_ref):
        return pltpu.make_async_copy(h_ref.at[pl.ds(t, 1)], xs_ref.at[pl.ds(pos_ref[base + t], 1)], sem)

    def issue(t, carry):
        row_copy(t, pos0_ref).start()
        row_copy(t, pos1_ref).start()
        return carry

    lax.fori_loop(0, td, issue, 0, unroll=8)
    for _ in range(2):
        pltpu.make_async_copy(h_ref, xs_ref.at[pl.ds(0, td)], sem).wait()


def _dispatch(pos0, pos1, h2p, xs_init):
    td = DISPATCH_TILE
    return pl.pallas_call(
        _dispatch_kernel,
        grid_spec=pltpu.PrefetchScalarGridSpec(
            num_scalar_prefetch=2,
            grid=(N_TOK // td,),
            in_specs=[pl.BlockSpec((td, D_MODEL), lambda i, p0, p1: (i, 0)),
                      pl.BlockSpec(memory_space=pl.ANY)],
            out_specs=pl.BlockSpec(memory_space=pl.ANY),
            scratch_shapes=[pltpu.SemaphoreType.DMA(())]),
        out_shape=jax.ShapeDtypeStruct(xs_init.shape, xs_init.dtype),
        input_output_aliases={3: 0},
        compiler_params=pltpu.CompilerParams(dimension_semantics=("arbitrary",)),
        name="moe_dispatch",
    )(pos0, pos1, h2p, xs_init)


def _expert_kernel(te_ref, xs_ref, wg_ref, wu_ref, wd_ref, ys_ref, wg_bf, wu_bf, wd_bf):
    j = pl.program_id(0)
    used = j < te_ref[1, 0]
    new_expert = jnp.logical_or(j == 0, te_ref[0, j] != te_ref[0, jnp.maximum(j - 1, 0)])

    @pl.when(jnp.logical_not(used))
    def _():
        ys_ref[...] = jnp.zeros_like(ys_ref)

    @pl.when(used & new_expert)
    def _():
        wg_bf[...] = wg_ref[0, 0].astype(bf16)
        wu_bf[...] = wu_ref[0, 0].astype(bf16)
        wd_bf[...] = wd_ref[0, 0].astype(bf16)

    @pl.when(used)
    def _():
        x = xs_ref[...].astype(bf16)
        hg = jnp.dot(x, wg_bf[...], preferred_element_type=f32)
        hu = jnp.dot(x, wu_bf[...], preferred_element_type=f32)
        hid = (hg * jax.nn.sigmoid(hg) * hu).astype(bf16)
        ys_ref[...] = jnp.dot(hid, wd_bf[...], preferred_element_type=f32)


def _experts(te, xs, w_gate, w_up, w_down, layer):
    tm = MOE_TILE
    row = lambda j, te: (jnp.minimum(j, te[1, 0] - 1), 0)
    wspec = lambda shape: pl.BlockSpec((1, 1) + shape, lambda j, te: (layer, te[0, j], 0, 0))
    return pl.pallas_call(
        _expert_kernel,
        grid_spec=pltpu.PrefetchScalarGridSpec(
            num_scalar_prefetch=1,
            grid=(MOE_ROWS // tm,),
            in_specs=[pl.BlockSpec((tm, D_MODEL), row),
                      wspec((D_MODEL, D_EXPERT)), wspec((D_MODEL, D_EXPERT)), wspec((D_EXPERT, D_MODEL))],
            out_specs=pl.BlockSpec((tm, D_MODEL), lambda j, te: (j, 0)),
            scratch_shapes=[pltpu.VMEM((D_MODEL, D_EXPERT), bf16), pltpu.VMEM((D_MODEL, D_EXPERT), bf16),
                            pltpu.VMEM((D_EXPERT, D_MODEL), bf16)]),
        out_shape=jax.ShapeDtypeStruct((MOE_ROWS, D_MODEL), f32),
        compiler_params=pltpu.CompilerParams(dimension_semantics=("arbitrary",), vmem_limit_bytes=VMEM_LIMIT),
        name="moe_experts",
    )(te, xs, w_gate, w_up, w_down)


def _combine_kernel(pos0_ref, pos1_ref, ys_ref, x1_ref, rc_ref, mod_ref, fn_ref, *rest, final):
    o_refs, (buf, sem) = rest[:-2], rest[-2:]
    tc = x1_ref.shape[0]
    i = pl.program_id(0)
    slot = i % 2

    def issue(tile, sl):
        base = tile * tc

        def body(t, carry):
            for s, pos_ref in enumerate((pos0_ref, pos1_ref)):
                pltpu.make_async_copy(ys_ref.at[pl.ds(pos_ref[base + t], 1)], buf.at[sl, s, pl.ds(t, 1)],
                                      sem.at[sl]).start()
            return carry

        lax.fori_loop(0, tc, body, 0, unroll=8)

    @pl.when(i == 0)
    def _():
        issue(0, 0)

    @pl.when(i + 1 < pl.num_programs(0))
    def _():
        issue(i + 1, 1 - slot)

    for s in range(2):
        pltpu.make_async_copy(ys_ref.at[pl.ds(0, tc)], buf.at[slot, s], sem.at[slot]).wait()
    rc = rc_ref[...]
    moe = rc[:, 2:3] * buf[slot, 0] + rc[:, 3:4] * buf[slot, 1]
    x2 = x1_ref[...] + mod_ref[0][5:6] * moe
    if not final:
        o_refs[0][...] = x2
        return
    y = x2 * lax.rsqrt(jnp.mean(x2 * x2, axis=-1, keepdims=True) + EPS) * fn_ref[...]
    is_ctx = i < N_CTX // tc

    @pl.when(is_ctx)
    def _():
        o_refs[0][...] = y

    @pl.when(jnp.logical_not(is_ctx))
    def _():
        o_refs[1][...] = y


def _combine(pos0, pos1, ys, x1, rc, mods, fn, *, final):
    tc = COMBINE_TILE
    row = lambda i, p0, p1: (i, 0)
    n_ctx_tiles = N_CTX // tc
    if final:
        out_specs = [pl.BlockSpec((tc, D_MODEL), lambda i, p0, p1: (jnp.minimum(i, n_ctx_tiles - 1), 0)),
                     pl.BlockSpec((tc, D_MODEL), lambda i, p0, p1: (jnp.maximum(i - n_ctx_tiles, 0), 0))]
        out_shape = [jax.ShapeDtypeStruct((N_CTX, D_MODEL), f32), jax.ShapeDtypeStruct((N_LAT, D_MODEL), f32)]
    else:
        out_specs = pl.BlockSpec((tc, D_MODEL), row)
        out_shape = jax.ShapeDtypeStruct((N_TOK, D_MODEL), f32)
    return pl.pallas_call(
        functools.partial(_combine_kernel, final=final),
        grid_spec=pltpu.PrefetchScalarGridSpec(
            num_scalar_prefetch=2,
            grid=(N_TOK // tc,),
            in_specs=[pl.BlockSpec(memory_space=pl.ANY),
                      pl.BlockSpec((tc, D_MODEL), row),
                      pl.BlockSpec((tc, LANE), row),
                      pl.BlockSpec((1, 6, D_MODEL), lambda i, p0, p1: (_mod_row(i, tc), 0, 0)),
                      pl.BlockSpec((1, D_MODEL), lambda i, p0, p1: (0, 0))],
            out_specs=out_specs,
            scratch_shapes=[pltpu.VMEM((2, 2, tc, D_MODEL), f32), pltpu.SemaphoreType.DMA((2,))]),
        out_shape=out_shape,
        compiler_params=pltpu.CompilerParams(dimension_semantics=("arbitrary",), vmem_limit_bytes=VMEM_LIMIT),
        name="moe_combine",
    )(pos0, pos1, ys, x1, rc, mods, fn)


def _moe(h2, route_t, rc, w_gate, w_up, w_down, layer, x1, mods, fn, xs_buf, *, final):
    pos, te = _rank(route_t)
    pos0, pos1 = pos[0], pos[1]
    xs = _dispatch(pos0, pos1, h2, xs_buf)
    ys = _experts(te, xs, w_gate, w_up, w_down, layer)
    return _combine(pos0, pos1, ys, x1, rc, mods, fn, final=final), xs


def _pad_heads(w):
    lead = w.shape[:-1]
    w = w.reshape(*lead, ML_HEADS, ML_DIM)
    w = jnp.pad(w, [(0, 0)] * len(lead) + [(0, 0), (0, ML_PAD - ML_DIM)])
    return w.reshape(*lead, ML_PW)


def _pack_in_cols(wb):
    o = 0
    qa = wb[..., o:o + NA_WIDTH] * (NA_DIM ** -0.5)
    ka = wb[..., o + NA_WIDTH:o + 2 * NA_WIDTH]
    va = wb[..., o + 2 * NA_WIDTH:o + 3 * NA_WIDTH]
    o += 3 * NA_WIDTH
    qb, kb, vb, ob = [_pad_heads(wb[..., o + j * ML_WIDTH:o + (j + 1) * ML_WIDTH]) for j in range(4)]
    o += 4 * ML_WIDTH
    gates = wb[..., o:o + N_GATE_COLS]
    o += N_GATE_COLS
    pool = wb[..., o:o + POOL_WIDTH]
    main = jnp.concatenate([qa, ka, va, qb, vb, ob, pool], axis=-1)
    gates_p = jnp.pad(gates, [(0, 0)] * (gates.ndim - 1) + [(0, LANE - N_GATE_COLS)])
    return main, gates_p, jnp.concatenate([kb, gates], axis=-1)


def _pack_w_in(w, b):
    w_main, w_gates, w_feat = _pack_in_cols(w)
    b_main, b_gates, b_feat = _pack_in_cols(b.astype(f32))
    return (w_main.astype(bf16), b_main[:, None], w_gates.astype(bf16), b_gates[:, None],
            jnp.swapaxes(w_feat, 1, 2).astype(bf16), b_feat[:, :, None])


def _pack_w_out(w):
    n_l = w.shape[0]
    wb = w[:, NA_WIDTH:NA_WIDTH + ML_WIDTH].reshape(n_l, ML_HEADS, ML_DIM, D_MODEL)
    wb = jnp.pad(wb, ((0, 0), (0, 0), (0, ML_PAD - ML_DIM), (0, 0))).reshape(n_l, ML_PW, D_MODEL)
    return jnp.concatenate([w[:, :NA_WIDTH], wb, w[:, NA_WIDTH + ML_WIDTH:]], axis=1).astype(bf16)


def _block_diag(w):
    n_l, g, c, _ = w.shape
    eye = jnp.eye(g, dtype=w.dtype)
    return (eye[None, :, None, :, None] * w[:, :, :, None, :]).reshape(n_l, g * c, g * c)


def kernel(x_prompt, x_sample, cache_k_attn, cache_v_attn, state_mlstm_C, state_mlstm_n, state_mlstm_m, c, c_ctx,
           w_ada, b_ada, norm1, w_in, b_in, rpb, ml_norm, w_pool, pool_scale, w_out, norm2, w_router, b_router,
           w_gate, w_up, w_down, final_norm):
    dt = x_prompt.dtype
    x = jnp.concatenate([x_prompt.reshape(N_CTX, D_MODEL), x_sample.reshape(N_LAT, D_MODEL)], axis=0).astype(f32)
    cvec = jnp.concatenate([c_ctx[None], c, jnp.zeros((8 - 1 - DEC_BATCH, D_MODEL), c.dtype)], axis=0).astype(f32)
    mods_all = _ada(cvec, w_ada.astype(f32), b_ada.astype(f32))
    mods_all = mods_all[:, :1 + DEC_BATCH].reshape(DEPTH, 1 + DEC_BATCH, 6, D_MODEL)

    wr_t = w_router.astype(f32).T
    br_t = b_router.astype(f32)[:, None]
    fn = final_norm.astype(f32)[None]

    na_bias = _na_bias_tables(rpb)
    xs_buf = jnp.zeros((MOE_ROWS, D_MODEL), f32)
    in_params = (norm1.astype(f32)[:, None],) + _pack_w_in(w_in, b_in)
    out_params = (_block_diag(w_pool.astype(f32)).astype(bf16), pool_scale.astype(f32)[:, None],
                  _pad_heads(ml_norm.astype(f32))[:, None], _pack_w_out(w_out), norm2.astype(f32)[:, None])

    ks, vs, Cs, ns, ms = [], [], [], [], []
    for l in range(DEPTH):
        mods = mods_all[l]
        qkva, kv32, qvo, kt, gates, gates_t, pin = _in_proj(x, mods, l, *in_params)
        ks.append(kv32[:N_CTX, :NA_WIDTH].reshape(BATCH, SEQ, NA_HEADS, NA_DIM))
        vs.append(kv32[:N_CTX, NA_WIDTH:].reshape(BATCH, SEQ, NA_HEADS, NA_DIM))

        oa_ctx = _ctx_attention(qkva.reshape(N_TOK // SEQ, SEQ, W_A))
        ck = (cache_k_attn[:, l].reshape(DEC_BATCH, PAST_LEN, NA_WIDTH)).astype(bf16)
        cv = (cache_v_attn[:, l].reshape(DEC_BATCH, PAST_LEN, NA_WIDTH)).astype(bf16)
        oa_lat = _neighborhood_attention(qkva.reshape(N_TOK // DEC_SEQ, DEC_SEQ, W_A), ck, cv, na_bias, l)

        c_l, m_l = _pack_ml_state(state_mlstm_C[:, l], state_mlstm_n[:, l], state_mlstm_m[:, l])
        hf, hb, c_fin, m_fin = _mlstm(qvo, kt, gates, gates_t, c_l, m_l)
        C_l, n_l, m_l2 = _unpack_ml_state(c_fin[:BATCH], m_fin[:BATCH])
        Cs.append(C_l)
        ns.append(n_l)
        ms.append(m_l2)

        x1, h2, route_t, rc = _out_proj(x, mods, l, oa_ctx.reshape(N_CTX, NA_WIDTH), oa_lat.reshape(N_LAT, NA_WIDTH),
                                        hf.reshape(N_TOK, ML_PW), hb.reshape(N_TOK, ML_PW), qvo, pin,
                                        *out_params, wr_t, br_t)
        x, xs_buf = _moe(h2, route_t, rc, w_gate, w_up, w_down, l, x1, mods, fn, xs_buf, final=(l == DEPTH - 1))

    y_prompt = x[0].reshape(BATCH, SEQ, D_MODEL).astype(dt)
    y_sample = x[1].reshape(DEC_BATCH, DEC_SEQ, D_MODEL).astype(dt)
    return (y_prompt, y_sample,
            jnp.stack(ks, axis=1).astype(dt), jnp.stack(vs, axis=1).astype(dt),
            jnp.stack(Cs, axis=1).astype(dt), jnp.stack(ns, axis=1).astype(dt), jnp.stack(ms, axis=1).astype(dt))
```

```python
import functools

import numpy as np
import jax
import jax.numpy as jnp
from jax import lax
from jax.experimental import pallas as pl
from jax.experimental.pallas import tpu as pltpu

D_MODEL = 1024
BATCH = 16
SEQ = 256
DEPTH = 4
DEC_BATCH = 2
DEC_SEQ = 4096
PAST_LEN = 256
GRID_W = 64
EPS = 1e-6
NEG_INF = -1e30
NA_HEADS = 6
NA_DIM = 64
NA_WIDTH = NA_HEADS * NA_DIM
NA_ROWS = 8
NA_COLS = 16
RPB_ROWS = 2 * NA_ROWS - 1
RPB_COLS = 2 * NA_COLS - 1
ML_HEADS = 4
ML_DIM = 96
ML_WIDTH = ML_HEADS * ML_DIM
POOL_WINDOWS = (2, 4, 8, 16)
POOL_GROUPS = 4
POOL_DIM = 64
POOL_WIDTH = POOL_GROUPS * POOL_DIM
N_GATE_COLS = 4 * ML_HEADS
N_EXPERTS = 16
N_EXPERT_GROUPS = 4
EXPERTS_PER_GROUP = N_EXPERTS // N_EXPERT_GROUPS
D_EXPERT = 512
ADA_DIM = 6 * D_MODEL

N_CTX = BATCH * SEQ
N_LAT = DEC_BATCH * DEC_SEQ
N_TOK = N_CTX + N_LAT
LANE = 128
ML_PAD = LANE
ML_PW = ML_HEADS * ML_PAD
CAUG = ML_PAD
NA_PAIRS = NA_HEADS // 2
TOK_TILE = 512
ML_CHUNK = 256
NA_QROWS = 4
NA_KROWS = NA_QROWS + NA_ROWS - 1
POOL_HALO = max(POOL_WINDOWS) // 2
POOL_BLOCK = 128
MOE_TILE = 512
MOE_ROWS = 2 * N_TOK + N_EXPERTS * MOE_TILE
DISPATCH_TILE = 256
COMBINE_TILE = 256
VMEM_LIMIT = 56 * 1024 * 1024

W_A = 3 * NA_WIDTH
W_B = 3 * ML_PW
N_TCOLS = ML_PW + N_GATE_COLS
W_MAIN = W_A + W_B + POOL_WIDTH

f32 = jnp.float32
bf16 = jnp.bfloat16
HI = lax.Precision.HIGHEST


def _nt(a, b, **kw):
    return lax.dot_general(a, b, (((1,), (1,)), ((), ())), preferred_element_type=f32, **kw)


def _mod_row(i, tile):
    n_ctx_tiles = N_CTX // tile
    per_batch = DEC_SEQ // tile
    return jnp.where(i < n_ctx_tiles, 0, 1 + (i - n_ctx_tiles) // per_batch)


def _ada_kernel(c_ref, w_ref, b_ref, o_ref):
    s = c_ref[...]
    s = s * jax.nn.sigmoid(s)
    o_ref[0] = jnp.dot(s.astype(bf16), w_ref[0].astype(bf16), preferred_element_type=f32) + b_ref[0]


def _ada(cvec, w_ada, b_ada):
    nj = ADA_DIM // D_MODEL
    return pl.pallas_call(
        _ada_kernel,
        grid=(DEPTH, nj),
        in_specs=[pl.BlockSpec((8, D_MODEL), lambda l, j: (0, 0)),
                  pl.BlockSpec((1, D_MODEL, D_MODEL), lambda l, j: (l, 0, j)),
                  pl.BlockSpec((1, 1, D_MODEL), lambda l, j: (l, 0, j))],
        out_specs=pl.BlockSpec((1, 8, D_MODEL), lambda l, j: (l, 0, j)),
        out_shape=jax.ShapeDtypeStruct((DEPTH, 8, ADA_DIM), f32),
        name="ada_mod",
    )(cvec, w_ada, b_ada.reshape(DEPTH, 1, ADA_DIM))


def _in_kernel(x_ref, *refs):
    _in_body(x_ref[...], *refs)


def _moe_in_kernel(pos0_ref, pos1_ref, ys_ref, x1_ref, rc_ref, mod_prev_ref, *refs):
    in_refs, x_out_ref, out_refs, (buf, sem) = refs[:8], refs[8], refs[9:-2], refs[-2:]
    x = _moe_residual(pos0_ref, pos1_ref, ys_ref, x1_ref, rc_ref, mod_prev_ref, buf, sem)
    x_out_ref[...] = x
    _in_body(x, *in_refs, *out_refs)


def _in_body(x, mod_ref, n1_ref, w_ref, b_ref, wg_ref, bg_ref, wt_ref, bt_ref,
             a_ref, kv_ref, b_out_ref, kt_ref, g_ref, gt_ref, pin_ref):
    mod = mod_ref[0]
    h = x * lax.rsqrt(jnp.mean(x * x, axis=-1, keepdims=True) + EPS) * n1_ref[...]
    h = (h * (1.0 + mod[1:2]) + mod[0:1]).astype(bf16)
    pa = jnp.dot(h, w_ref[:, 0:W_A], preferred_element_type=f32) + b_ref[:, 0:W_A]
    a_ref[...] = pa.astype(bf16)
    kv_ref[...] = pa[:, NA_WIDTH:W_A]
    for j in range(3):
        lo = W_A + j * ML_PW
        pb = jnp.dot(h, w_ref[:, lo:lo + ML_PW], preferred_element_type=f32) + b_ref[:, lo:lo + ML_PW]
        if j == 0:
            pb = pb * (ML_DIM ** -0.5)
        b_out_ref[:, j * ML_PW:(j + 1) * ML_PW] = pb.astype(bf16)
    lo = W_A + W_B
    pin_ref[...] = jnp.dot(h, w_ref[:, lo:lo + POOL_WIDTH], preferred_element_type=f32) + b_ref[:, lo:lo + POOL_WIDTH]
    g_ref[...] = jnp.dot(h, wg_ref[...], preferred_element_type=f32) + bg_ref[...]
    t = _nt(wt_ref[...], h) + bt_ref[...]
    kt_ref[...] = t[0:ML_PW].astype(bf16)
    gt_ref[...] = t[ML_PW:N_TCOLS]


def _in_proj_specs(layer):
    tm = TOK_TILE
    lyr = lambda shape: pl.BlockSpec((None,) + shape, lambda i, *_: (layer, 0, 0))
    rows = lambda width: pl.BlockSpec((tm, width), lambda i, *_: (i, 0))
    cols = lambda height: pl.BlockSpec((height, tm), lambda i, *_: (0, i))
    param_specs = [pl.BlockSpec((1, 6, D_MODEL), lambda i, *_: (_mod_row(i, tm), 0, 0)),
                   lyr((1, D_MODEL)), lyr((D_MODEL, W_MAIN)), lyr((1, W_MAIN)), lyr((D_MODEL, LANE)), lyr((1, LANE)),
                   lyr((N_TCOLS, D_MODEL)), lyr((N_TCOLS, 1))]
    out_specs = [rows(W_A), rows(2 * NA_WIDTH), rows(W_B), cols(ML_PW), rows(LANE), cols(N_GATE_COLS),
                 rows(POOL_WIDTH)]
    out_shape = [jax.ShapeDtypeStruct((N_TOK, W_A), bf16),
                 jax.ShapeDtypeStruct((N_TOK, 2 * NA_WIDTH), f32),
                 jax.ShapeDtypeStruct((N_TOK, W_B), bf16),
                 jax.ShapeDtypeStruct((ML_PW, N_TOK), bf16),
                 jax.ShapeDtypeStruct((N_TOK, LANE), f32),
                 jax.ShapeDtypeStruct((N_GATE_COLS, N_TOK), f32),
                 jax.ShapeDtypeStruct((N_TOK, POOL_WIDTH), f32)]
    return rows, param_specs, out_specs, out_shape


def _in_proj(x, mods, layer, *params):
    rows, param_specs, out_specs, out_shape = _in_proj_specs(layer)
    return pl.pallas_call(
        _in_kernel,
        grid=(N_TOK // TOK_TILE,),
        in_specs=[rows(D_MODEL)] + param_specs,
        out_specs=out_specs,
        out_shape=out_shape,
        compiler_params=pltpu.CompilerParams(dimension_semantics=("arbitrary",), vmem_limit_bytes=VMEM_LIMIT),
        name="in_proj",
    )(x, mods, *params)


def _moe_in_proj(pos0, pos1, ys, x1, rc, mods_prev, mods, layer, *params):
    tm = TOK_TILE
    rows, param_specs, out_specs, out_shape = _in_proj_specs(layer)
    return pl.pallas_call(
        _moe_in_kernel,
        grid_spec=pltpu.PrefetchScalarGridSpec(
            num_scalar_prefetch=2,
            grid=(N_TOK // tm,),
            in_specs=[pl.BlockSpec(memory_space=pl.ANY), rows(D_MODEL), rows(LANE),
                      pl.BlockSpec((1, 6, D_MODEL), lambda i, *_: (_mod_row(i, tm), 0, 0))] + param_specs,
            out_specs=[rows(D_MODEL)] + out_specs,
            scratch_shapes=[pltpu.VMEM((2, 2, tm, D_MODEL), f32), pltpu.SemaphoreType.DMA((2,))]),
        out_shape=[jax.ShapeDtypeStruct((N_TOK, D_MODEL), f32)] + out_shape,
        compiler_params=pltpu.CompilerParams(dimension_semantics=("arbitrary",), vmem_limit_bytes=VMEM_LIMIT),
        name="moe_combine_in_proj",
    )(pos0, pos1, ys, x1, rc, mods_prev, mods, *params)


def _pair_attention(qp, parts):
    lane = lax.broadcasted_iota(jnp.int32, (1, LANE), 1)
    outs = []
    for j in range(2):
        in_half = (lane >= j * NA_DIM) & (lane < (j + 1) * NA_DIM)
        qm = jnp.where(in_half, qp, jnp.zeros_like(qp))
        scores = []
        for k, _, bias in parts:
            s = _nt(qm, k)
            if bias is not None:
                s = s + bias[j]
            scores.append(s)
        m = scores[0].max(axis=-1, keepdims=True)
        for s in scores[1:]:
            m = jnp.maximum(m, s.max(axis=-1, keepdims=True))
        den = None
        acc = None
        for s, (_, v, _) in zip(scores, parts):
            p = jnp.exp(s - m)
            ps = p.sum(axis=-1, keepdims=True)
            den = ps if den is None else den + ps
            o = jnp.dot(p.astype(bf16), v, preferred_element_type=f32)
            acc = o if acc is None else acc + o
        outs.append(acc / den)
    return jnp.where(lane < NA_DIM, outs[0], outs[1])


def _ctx_attn_kernel(q_ref, k_ref, v_ref, o_ref):
    for p in range(NA_PAIRS):
        sl = slice(p * LANE, (p + 1) * LANE)
        o = _pair_attention(q_ref[0, :, sl], [(k_ref[0, :, sl], v_ref[0, :, sl], None)])
        o_ref[0, :, sl] = o.astype(bf16)


def _ctx_attention(qkv):
    blk = lambda c: pl.BlockSpec((1, SEQ, NA_WIDTH), lambda b, c=c: (b, 0, c))
    return pl.pallas_call(
        _ctx_attn_kernel,
        grid=(BATCH,),
        in_specs=[blk(0), blk(1), blk(2)],
        out_specs=pl.BlockSpec((1, SEQ, NA_WIDTH), lambda b: (b, 0, 0)),
        out_shape=jax.ShapeDtypeStruct((BATCH, SEQ, NA_WIDTH), bf16),
        name="ctx_attention",
    )(qkv, qkv, qkv)


def _na_window_start(rb):
    return jnp.clip(rb * NA_QROWS - NA_ROWS // 2, 0, DEC_SEQ // GRID_W - NA_KROWS)


def _na_bias(tab_ref, head, rb):
    rows = DEC_SEQ // GRID_W
    ws = _na_window_start(rb)
    lane = lax.broadcasted_iota(jnp.int32, (1, NA_KROWS * GRID_W), 1)
    per_qrow = []
    for dq in range(NA_QROWS):
        qr = rb * NA_QROWS + dq
        a0 = ws - qr + (NA_ROWS - 1) + NA_KROWS
        tiles = [tab_ref[head, a0 + 2 * j] for j in range((NA_KROWS + 1) // 2)]
        t = jnp.concatenate(tiles, axis=1)[:, :NA_KROWS * GRID_W]
        lo = (jnp.clip(qr - NA_ROWS // 2, 0, rows - NA_ROWS) - ws) * GRID_W
        ok = (lane >= lo) & (lane < lo + NA_ROWS * GRID_W)
        per_qrow.append(jnp.where(ok, t, NEG_INF))
    return jnp.concatenate(per_qrow, axis=0)


def _na_kernel(q_ref, k_ref, v_ref, ck_ref, cv_ref, tab_ref, o_ref):
    rb = pl.program_id(1)
    start = pl.multiple_of(_na_window_start(rb) * GRID_W, GRID_W)
    nk = NA_KROWS * GRID_W
    for p in range(NA_PAIRS):
        sl = slice(p * LANE, (p + 1) * LANE)
        bias = [_na_bias(tab_ref.at[0], 2 * p + j, rb) for j in range(2)]
        parts = [(k_ref[0, pl.ds(start, nk), sl], v_ref[0, pl.ds(start, nk), sl], bias),
                 (ck_ref[0, :, sl], cv_ref[0, :, sl], None)]
        o = _pair_attention(q_ref[0, :, sl], parts)
        o_ref[0, :, sl] = o.astype(bf16)


def _neighborhood_attention(qkv, ck, cv, tables, layer):
    nq = NA_QROWS * GRID_W
    n_rb = DEC_SEQ // nq
    return pl.pallas_call(
        _na_kernel,
        grid=(DEC_BATCH, n_rb),
        in_specs=[pl.BlockSpec((1, nq, NA_WIDTH), lambda b, r: (1 + b, r, 0)),
                  pl.BlockSpec((1, DEC_SEQ, NA_WIDTH), lambda b, r: (1 + b, 0, 1)),
                  pl.BlockSpec((1, DEC_SEQ, NA_WIDTH), lambda b, r: (1 + b, 0, 2)),
                  pl.BlockSpec((1, PAST_LEN, NA_WIDTH), lambda b, r: (b, 0, 0)),
                  pl.BlockSpec((1, PAST_LEN, NA_WIDTH), lambda b, r: (b, 0, 0)),
                  pl.BlockSpec((1,) + tables.shape[1:], lambda b, r: (layer, 0, 0, 0, 0))],
        out_specs=pl.BlockSpec((1, nq, NA_WIDTH), lambda b, r: (b, r, 0)),
        out_shape=jax.ShapeDtypeStruct((DEC_BATCH, DEC_SEQ, NA_WIDTH), bf16),
        compiler_params=pltpu.CompilerParams(dimension_semantics=("arbitrary", "arbitrary"),
                                             vmem_limit_bytes=VMEM_LIMIT),
        name="neighborhood_attention",
    )(qkv, qkv, qkv, ck, cv, tables)


def _na_bias_tables(rpb):
    qc = np.arange(GRID_W)[:, None]
    kc = np.arange(GRID_W)[None, :]
    dc = np.clip(kc - qc + NA_COLS - 1, 0, RPB_COLS - 1)
    col_start = np.clip(qc - NA_COLS // 2, 0, GRID_W - NA_COLS)
    col_ok = (kc >= col_start) & (kc < col_start + NA_COLS)
    pick_col = (dc[None] == np.arange(RPB_COLS)[:, None, None]).astype(np.float32)
    by_col = jnp.einsum('lhab,bqk->lhaqk', rpb.astype(f32), pick_col, precision=HI)
    by_col = jnp.where(col_ok[None, None, None], by_col, NEG_INF)
    by_col = jnp.pad(by_col, ((0, 0), (0, 0), (NA_KROWS, NA_KROWS + 1), (0, 0), (0, 0)), constant_values=NEG_INF)
    return jnp.concatenate([by_col[:, :, :-1], by_col[:, :, 1:]], axis=-1)


def _log_sigmoid(x):
    return -(jnp.maximum(-x, 0.0) + jnp.log(1.0 + jnp.exp(-jnp.abs(x))))


def _split3(x):
    hi = x.astype(bf16)
    r1 = x - hi.astype(f32)
    mid = r1.astype(bf16)
    lo = (r1 - mid.astype(f32)).astype(bf16)
    return hi, mid, lo


def _mlstm_kernel(qf_ref, vf_ref, ktf_ref, gf_ref, gtf_ref, qb_ref, vb_ref, ktb_ref, gb_ref, gtb_ref,
                  c0_ref, m0_ref, hf_ref, hb_ref, c_out_ref, m_out_ref, c_scr, m_scr):
    L = ML_CHUNK
    seq, c, n_chunks, _ = _ml_schedule(pl.program_id(0))

    @pl.when(c == 0)
    def _():
        is_ctx = seq < BATCH
        c_scr[...] = jnp.where(is_ctx, 0.0, c0_ref[0])
        m_scr[...] = jnp.where(is_ctx, 0.0, m0_ref[0])

    ri = lax.broadcasted_iota(jnp.int32, (L, L), 0)
    ci = lax.broadcasted_iota(jnp.int32, (L, L), 1)
    lane = lax.broadcasted_iota(jnp.int32, (L, ML_PAD), 1)
    is_ncol = lane == ML_DIM
    lower = ri >= ci
    upper = ri <= ci
    lower_b = jnp.where(lower, 1.0, 0.0).astype(bf16)
    upper_b = jnp.where(upper, 1.0, 0.0).astype(bf16)
    dirs = ((qf_ref, ktf_ref, vf_ref, gf_ref, gtf_ref, hf_ref), (qb_ref, ktb_ref, vb_ref, gb_ref, gtb_ref, hb_ref))
    for d, (q_ref, kt_ref, v_ref, g_ref, gt_ref, h_ref) in enumerate(dirs):
        g = g_ref[...][:, 0:N_GATE_COLS]
        gt = gt_ref[...]
        lf_c = _log_sigmoid(g)
        lf_r = _log_sigmoid(gt)
        b_cols = sum(jnp.dot(lower_b, part, preferred_element_type=f32) for part in _split3(lf_c))
        b_rows = sum(jnp.dot(part, upper_b, preferred_element_type=f32) for part in _split3(lf_r))
        tot_c = jnp.sum(lf_c, axis=0, keepdims=True)
        tot_r = jnp.sum(lf_r, axis=1, keepdims=True)
        visible = lower
        if d == 1:
            b_cols = tot_c - b_cols + lf_c
            b_rows = tot_r - b_rows + lf_r
            visible = upper
        for hd in range(ML_HEADS):
            st = d * ML_HEADS + hd
            ci_ = 2 * ML_HEADS * d + hd
            cf_ = ci_ + ML_HEADS
            sl = slice(hd * ML_PAD, (hd + 1) * ML_PAD)
            bc = b_cols[:, cf_:cf_ + 1]
            br = b_rows[cf_:cf_ + 1, :]
            li_r = gt[ci_:ci_ + 1, :]
            m_prev = m_scr[st:st + 1, 0:1]
            dmat = jnp.where(visible, bc - br + li_r, NEG_INF)
            inter = bc + m_prev
            m_t = jnp.maximum(inter, dmat.max(axis=-1, keepdims=True))
            w_intra = jnp.exp(dmat - m_t)
            w_inter = jnp.exp(inter - m_t)
            qh = q_ref[0, :, sl]
            kht = kt_ref[sl, :]
            v_aug = jnp.where(is_ncol, jnp.ones((), bf16), v_ref[0, :, sl])
            s = (jnp.dot(qh, kht, preferred_element_type=f32) * w_intra).astype(bf16)
            c_aug = c_scr[st]
            na = (w_inter * jnp.dot(qh, c_aug.astype(bf16), preferred_element_type=f32)
                  + jnp.dot(s, v_aug, preferred_element_type=f32))
            den = na[:, ML_DIM:ML_DIM + 1]
            h_ref[0, :, sl] = jnp.where(lane < ML_DIM, na / jnp.maximum(jnp.abs(den), jnp.exp(-m_t)), 0.0)
            b_end = tot_r[cf_:cf_ + 1, :]
            g_row = b_end - br + li_r
            m_new = jnp.maximum(b_end + m_prev, g_row.max(axis=1, keepdims=True))
            decay = jnp.exp(b_end + m_prev - m_new)
            kwt = (kht.astype(f32) * jnp.exp(g_row - m_new)).astype(bf16)
            c_scr[st] = decay * c_aug + jnp.dot(kwt, v_aug, preferred_element_type=f32)
            m_scr[st:st + 1, :] = jnp.broadcast_to(m_new, (1, LANE))

    @pl.when(c == n_chunks - 1)
    def _():
        c_out_ref[0] = c_scr[...]
        m_out_ref[0] = m_scr[...]


def _ml_schedule(s):
    nc_ctx, nc_lat = SEQ // ML_CHUNK, DEC_SEQ // ML_CHUNK
    n_ctx_steps = BATCH * nc_ctx
    is_ctx = s < n_ctx_steps
    t = s - n_ctx_steps
    seq = jnp.where(is_ctx, s // nc_ctx, BATCH + t // nc_lat)
    c = jnp.where(is_ctx, s % nc_ctx, t % nc_lat)
    nc = jnp.where(is_ctx, nc_ctx, nc_lat)
    base = jnp.where(is_ctx, (s // nc_ctx) * nc_ctx, n_ctx_steps + (t // nc_lat) * nc_lat)
    return seq, c, nc, base


def _mlstm(qvo, kt, gates, gates_t, c0, m0):
    L = ML_CHUNK
    n_seq = BATCH + DEC_BATCH

    def fwd(s):
        _, c, _, base = _ml_schedule(s)
        return base + c

    def bwd(s):
        _, c, nc, base = _ml_schedule(s)
        return base + nc - 1 - c

    seq_of = lambda s: _ml_schedule(s)[0]
    lat_of = lambda s: jnp.maximum(seq_of(s) - BATCH, 0)

    def specs(pos):
        return [pl.BlockSpec((1, L, ML_PW), lambda s, j=j: (pos(s), 0, j)) for j in range(2)] + [
            pl.BlockSpec((ML_PW, L), lambda s: (0, pos(s))),
            pl.BlockSpec((L, LANE), lambda s: (pos(s), 0)),
            pl.BlockSpec((N_GATE_COLS, L), lambda s: (0, pos(s)))]

    q3 = qvo.reshape(N_TOK // L, L, W_B)
    n_str = 2 * ML_HEADS
    return pl.pallas_call(
        _mlstm_kernel,
        grid=(N_TOK // L,),
        in_specs=specs(fwd) + specs(bwd) + [
            pl.BlockSpec((1, n_str, ML_PAD, CAUG), lambda s: (lat_of(s), 0, 0, 0)),
            pl.BlockSpec((1, n_str, LANE), lambda s: (lat_of(s), 0, 0))],
        out_specs=[pl.BlockSpec((1, L, ML_PW), lambda s: (fwd(s), 0, 0)),
                   pl.BlockSpec((1, L, ML_PW), lambda s: (bwd(s), 0, 0)),
                   pl.BlockSpec((1, n_str, ML_PAD, CAUG), lambda s: (seq_of(s), 0, 0, 0)),
                   pl.BlockSpec((1, n_str, LANE), lambda s: (seq_of(s), 0, 0))],
        out_shape=[jax.ShapeDtypeStruct((N_TOK // L, L, ML_PW), f32),
                   jax.ShapeDtypeStruct((N_TOK // L, L, ML_PW), f32),
                   jax.ShapeDtypeStruct((n_seq, n_str, ML_PAD, CAUG), f32),
                   jax.ShapeDtypeStruct((n_seq, n_str, LANE), f32)],
        scratch_shapes=[pltpu.VMEM((n_str, ML_PAD, CAUG), f32), pltpu.VMEM((n_str, LANE), f32)],
        compiler_params=pltpu.CompilerParams(dimension_semantics=("arbitrary",), vmem_limit_bytes=VMEM_LIMIT),
        name="mlstm",
    )(q3, q3, kt, gates, gates_t, q3, q3, kt, gates, gates_t, c0, m0)


def _pack_ml_state(C, n, m):
    B = C.shape[0]
    c_aug = jnp.zeros((B, 2, ML_HEADS, ML_PAD, CAUG), f32)
    c_aug = c_aug.at[:, :, :, :ML_DIM, :ML_DIM].set(C.astype(f32))
    c_aug = c_aug.at[:, :, :, :ML_DIM, ML_DIM].set(n.astype(f32))
    m_b = jnp.broadcast_to(m.astype(f32)[..., None], (B, 2, ML_HEADS, LANE))
    return c_aug.reshape(B, 2 * ML_HEADS, ML_PAD, CAUG), m_b.reshape(B, 2 * ML_HEADS, LANE)


def _unpack_ml_state(c_aug, m_b):
    B = c_aug.shape[0]
    c_aug = c_aug.reshape(B, 2, ML_HEADS, ML_PAD, CAUG)
    return (c_aug[:, :, :, :ML_DIM, :ML_DIM], c_aug[:, :, :, :ML_DIM, ML_DIM],
            m_b.reshape(B, 2, ML_HEADS, LANE)[..., 0])


def _pool_rows(u_prev, u_cur, u_next, w_bd, scale, t0, seq_len):
    tm = u_cur.shape[0]
    u_win = jnp.concatenate([u_prev, u_cur, u_next], axis=0)
    u_hi = u_win.astype(bf16)
    u_lo = (u_win - u_hi.astype(f32)).astype(bf16)
    lane = lax.broadcasted_iota(jnp.int32, (1, LANE), 1)
    blocks = []
    for r0 in range(0, tm, POOL_BLOCK):
        win = slice(r0, r0 + POOL_BLOCK + 2 * POOL_HALO)
        t_abs = t0 + r0 + lax.broadcasted_iota(jnp.int32, (POOL_BLOCK, 1), 0)
        s_abs = t0 + r0 - POOL_HALO + lax.broadcasted_iota(jnp.int32, (1, POOL_BLOCK + 2 * POOL_HALO), 1)
        t_loc = t_abs & (seq_len - 1)
        seq_start = t_abs - t_loc
        means = []
        for w in POOL_WINDOWS:
            lo = jnp.maximum(t_loc - w // 2, 0)
            hi = jnp.minimum(t_loc - w // 2 + w, seq_len)
            in_win = (s_abs >= seq_start + lo) & (s_abs < seq_start + hi)
            means.append((jnp.where(in_win, 1.0, 0.0).astype(bf16), 1.0 / (hi - lo).astype(f32)))
        pooled = []
        for p in range(POOL_GROUPS // 2):
            sl = slice(p * LANE, (p + 1) * LANE)
            halves = []
            for a, inv_cnt in means[2 * p:2 * p + 2]:
                tot = (jnp.dot(a, u_hi[win, sl], preferred_element_type=f32)
                       + jnp.dot(a, u_lo[win, sl], preferred_element_type=f32))
                halves.append(tot * inv_cnt)
            pooled.append(jnp.where(lane < POOL_DIM, halves[0], halves[1]) - u_cur[r0:r0 + POOL_BLOCK, sl])
        blocks.append(jnp.concatenate(pooled, axis=1))
    pooled = jnp.concatenate(blocks, axis=0).astype(bf16)
    return jnp.dot(pooled, w_bd, preferred_element_type=f32) * scale


def _top2_sum(a, b, c, d):
    hi1, lo1 = jnp.maximum(a, b), jnp.minimum(a, b)
    hi2, lo2 = jnp.maximum(c, d), jnp.minimum(c, d)
    return jnp.maximum(hi1, hi2) + jnp.maximum(jnp.minimum(hi1, hi2), jnp.maximum(lo1, lo2))


def _first_match(vals, target):
    idx = jnp.full_like(target, float(len(vals) - 1))
    for i in range(len(vals) - 2, -1, -1):
        idx = jnp.where(vals[i] == target, float(i), idx)
    return idx


def _pick(vals, idx):
    out = vals[-1]
    for i in range(len(vals) - 2, -1, -1):
        out = jnp.where(idx == float(i), vals[i], out)
    return out


def _route(logits_t, bias_t):
    scores = jax.nn.sigmoid(logits_t)
    sel = scores + bias_t
    row = lambda a, i: a[i:i + 1, :]
    grp = [_top2_sum(*[row(sel, EXPERTS_PER_GROUP * g + i) for i in range(EXPERTS_PER_GROUP)])
           for g in range(N_EXPERT_GROUPS)]
    best = functools.reduce(jnp.maximum, grp)
    gidx = _first_match(grp, best)
    sel_g = [_pick([row(sel, EXPERTS_PER_GROUP * g + i) for g in range(N_EXPERT_GROUPS)], gidx)
             for i in range(EXPERTS_PER_GROUP)]
    sco_g = [_pick([row(scores, EXPERTS_PER_GROUP * g + i) for g in range(N_EXPERT_GROUPS)], gidx)
             for i in range(EXPERTS_PER_GROUP)]
    i0 = _first_match(sel_g, functools.reduce(jnp.maximum, sel_g))
    rest = [jnp.where(i0 == float(i), -jnp.inf, sel_g[i]) for i in range(EXPERTS_PER_GROUP)]
    i1 = _first_match(rest, functools.reduce(jnp.maximum, rest))
    s0, s1 = _pick(sco_g, i0), _pick(sco_g, i1)
    tot = s0 + s1
    rid = lax.broadcasted_iota(jnp.int32, (LANE, logits_t.shape[1]), 0)
    rows = (EXPERTS_PER_GROUP * gidx + i0, EXPERTS_PER_GROUP * gidx + i1, s0 / tot, s1 / tot)
    out = jnp.zeros(rid.shape, f32)
    for i, r in enumerate(rows):
        out = jnp.where(rid == i, r, out)
    return out


def _out_kernel(x_ref, mod_ref, oac_ref, oal_ref, hf_ref, hb_ref, ob_ref, up_ref, uc_ref, un_ref, wp_ref, psc_ref,
                mln_ref, wo_ref, n2_ref, wr_ref, br_ref, x1_ref, h2_ref, rt_ref, rc_ref):
    tm = x_ref.shape[0]
    i = pl.program_id(0)
    is_ctx = i < N_CTX // tm
    mod = mod_ref[0]
    out_a = jnp.where(is_ctx, oac_ref[...], oal_ref[...])
    out_c = _pool_rows(up_ref[...], uc_ref[...], un_ref[...], wp_ref[...], psc_ref[...], i * tm,
                       jnp.where(is_ctx, SEQ, DEC_SEQ)).astype(bf16)
    hsum = hf_ref[...] + hb_ref[...]
    outs_b = []
    for hd in range(ML_HEADS):
        sl = slice(hd * ML_PAD, (hd + 1) * ML_PAD)
        hh = hsum[:, sl]
        ms = jnp.sum(hh * hh, axis=-1, keepdims=True) * (1.0 / ML_DIM)
        hn = hh * lax.rsqrt(ms + EPS) * mln_ref[:, sl]
        outs_b.append((jax.nn.sigmoid(ob_ref[:, sl].astype(f32)) * hn).astype(bf16))
    out_b = jnp.concatenate(outs_b, axis=1)
    mixed = (jnp.dot(out_a, wo_ref[0:NA_WIDTH, :], preferred_element_type=f32)
             + jnp.dot(out_b, wo_ref[NA_WIDTH:NA_WIDTH + ML_PW, :], preferred_element_type=f32)
             + jnp.dot(out_c, wo_ref[NA_WIDTH + ML_PW:, :], preferred_element_type=f32))
    x1 = x_ref[...] + mod[2:3] * mixed
    x1_ref[...] = x1
    h2 = x1 * lax.rsqrt(jnp.mean(x1 * x1, axis=-1, keepdims=True) + EPS) * n2_ref[...]
    h2 = h2 * (1.0 + mod[4:5]) + mod[3:4]
    h2_ref[...] = h2
    route_t = _route(_nt(wr_ref[...], h2, precision=HI), br_ref[...])
    rt_ref[...] = route_t[0:8]
    rc_ref[...] = route_t.T


def _out_proj(x, mods, layer, oa_ctx, oa_lat, hf, hb, qvo, pin, w_bd, psc, mln, wo, n2, wr_t, br_t):
    tm = TOK_TILE
    const = lambda i: (0, 0)
    lyr = lambda shape: pl.BlockSpec((None,) + shape, lambda i: (layer, 0, 0))
    row = lambda i: (i, 0)
    n_ctx_tiles = N_CTX // tm
    halo_blocks = tm // POOL_HALO
    return pl.pallas_call(
        _out_kernel,
        grid=(N_TOK // tm,),
        in_specs=[pl.BlockSpec((tm, D_MODEL), row),
                  pl.BlockSpec((1, 6, D_MODEL), lambda i: (_mod_row(i, tm), 0, 0)),
                  pl.BlockSpec((tm, NA_WIDTH), lambda i: (jnp.minimum(i, n_ctx_tiles - 1), 0)),
                  pl.BlockSpec((tm, NA_WIDTH), lambda i: (jnp.maximum(i - n_ctx_tiles, 0), 0)),
                  pl.BlockSpec((tm, ML_PW), row),
                  pl.BlockSpec((tm, ML_PW), row),
                  pl.BlockSpec((tm, ML_PW), lambda i: (i, 2)),
                  pl.BlockSpec((POOL_HALO, POOL_WIDTH), lambda i: (jnp.maximum(i * halo_blocks - 1, 0), 0)),
                  pl.BlockSpec((tm, POOL_WIDTH), row),
                  pl.BlockSpec((POOL_HALO, POOL_WIDTH),
                               lambda i: (jnp.minimum((i + 1) * halo_blocks, N_TOK // POOL_HALO - 1), 0)),
                  lyr((POOL_WIDTH, POOL_WIDTH)), lyr((1, POOL_WIDTH)), lyr((1, ML_PW)),
                  lyr((NA_WIDTH + ML_PW + POOL_WIDTH, D_MODEL)), lyr((1, D_MODEL)),
                  pl.BlockSpec((N_EXPERTS, D_MODEL), const),
                  pl.BlockSpec((N_EXPERTS, 1), const)],
        out_specs=[pl.BlockSpec((tm, D_MODEL), row),
                   pl.BlockSpec((tm, D_MODEL), row),
                   pl.BlockSpec((8, tm), lambda i: (0, i)),
                   pl.BlockSpec((tm, LANE), row)],
        out_shape=[jax.ShapeDtypeStruct((N_TOK, D_MODEL), f32),
                   jax.ShapeDtypeStruct((N_TOK, D_MODEL), f32),
                   jax.ShapeDtypeStruct((8, N_TOK), f32),
                   jax.ShapeDtypeStruct((N_TOK, LANE), f32)],
        compiler_params=pltpu.CompilerParams(dimension_semantics=("arbitrary",), vmem_limit_bytes=VMEM_LIMIT),
        name="out_proj_router",
    )(x, mods, oa_ctx, oa_lat, hf, hb, qvo, pin, pin, pin, w_bd, psc, mln, wo, n2, wr_t, br_t)


def _rank_kernel(rt_ref, pos_ref, te_ref, carry_ref):
    tm = rt_ref.shape[1]
    p = pl.program_id(0)
    i = pl.program_id(1)
    rid = lax.broadcasted_iota(jnp.int32, (N_EXPERTS, tm), 0).astype(f32)
    oh0 = rid == rt_ref[0:1, :]
    oh1 = rid == rt_ref[1:2, :]
    both = jnp.where(oh0 | oh1, 1.0, 0.0)
    counts = jnp.sum(both, axis=1, keepdims=True)

    @pl.when((p == 0) & (i == 0))
    def _():
        carry_ref[...] = jnp.zeros_like(carry_ref)

    @pl.when((p == 1) & (i == 0))
    def _():
        cnt = carry_ref[...]
        padded = jnp.floor((cnt + (MOE_TILE - 1)) * (1.0 / MOE_TILE)) * MOE_TILE
        er = lax.broadcasted_iota(jnp.int32, (N_EXPERTS, N_EXPERTS), 0)
        ec = lax.broadcasted_iota(jnp.int32, (N_EXPERTS, N_EXPERTS), 1)
        off = jnp.dot(jnp.where(ec < er, 1.0, 0.0), padded, preferred_element_type=f32, precision=HI)
        carry_ref[...] = off
        total = jnp.sum(padded, axis=0, keepdims=True)
        n_used = total * (1.0 / MOE_TILE)
        tile = lax.broadcasted_iota(jnp.int32, (1, LANE), 1).astype(f32)
        row0 = jnp.minimum(tile, n_used - 1.0) * MOE_TILE
        expert = jnp.sum(jnp.where(off <= row0, 1.0, 0.0), axis=0, keepdims=True) - 1.0
        sub = lax.broadcasted_iota(jnp.int32, (8, LANE), 0)
        te_ref[...] = jnp.where(sub == 0, expert, jnp.where(sub == 1, n_used, 0.0)).astype(jnp.int32)

    @pl.when(p == 1)
    def _():
        sr = lax.broadcasted_iota(jnp.int32, (tm, tm), 0)
        sc = lax.broadcasted_iota(jnp.int32, (tm, tm), 1)
        earlier = jnp.dot(both.astype(bf16), jnp.where(sr < sc, 1.0, 0.0).astype(bf16),
                          preferred_element_type=f32)
        before = carry_ref[:, 0:1] + earlier
        pos0 = jnp.sum(jnp.where(oh0, before, 0.0), axis=0, keepdims=True)
        pos1 = jnp.sum(jnp.where(oh1, before, 0.0), axis=0, keepdims=True)
        sub = lax.broadcasted_iota(jnp.int32, (8, tm), 0)
        pos_ref[...] = jnp.where(sub == 0, pos0, jnp.where(sub == 1, pos1, 0.0)).astype(jnp.int32)

    carry_ref[...] += counts


def _rank(route_t):
    tm = TOK_TILE
    return pl.pallas_call(
        _rank_kernel,
        grid=(2, N_TOK // tm),
        in_specs=[pl.BlockSpec((8, tm), lambda p, i: (0, i))],
        out_specs=[pl.BlockSpec((8, tm), lambda p, i: (0, i * p)),
                   pl.BlockSpec((8, LANE), lambda p, i: (0, 0))],
        out_shape=[jax.ShapeDtypeStruct((8, N_TOK), jnp.int32),
                   jax.ShapeDtypeStruct((8, LANE), jnp.int32)],
        scratch_shapes=[pltpu.VMEM((N_EXPERTS, LANE), f32)],
        compiler_params=pltpu.CompilerParams(dimension_semantics=("arbitrary", "arbitrary")),
        name="moe_rank",
    )(route_t)


def _dispatch_kernel(pos0_ref, pos1_ref, h_ref, xs_in_ref, xs_ref, sem):
    del xs_in_ref
    td = h_ref.shape[0]
    base = pl.program_id(0) * td

    def row_copy(t, pos_ref):
        return pltpu.make_async_copy(h_ref.at[pl.ds(t, 1)], xs_ref.at[pl.ds(pos_ref[base + t], 1)], sem)

    def issue(t, carry):
        row_copy(t, pos0_ref).start()
        row_copy(t, pos1_ref).start()
        return carry

    lax.fori_loop(0, td, issue, 0, unroll=8)
    for _ in range(2):
        pltpu.make_async_copy(h_ref, xs_ref.at[pl.ds(0, td)], sem).wait()


def _dispatch(pos0, pos1, h2p, xs_init):
    td = DISPATCH_TILE
    return pl.pallas_call(
        _dispatch_kernel,
        grid_spec=pltpu.PrefetchScalarGridSpec(
            num_scalar_prefetch=2,
            grid=(N_TOK // td,),
            in_specs=[pl.BlockSpec((td, D_MODEL), lambda i, p0, p1: (i, 0)),
                      pl.BlockSpec(memory_space=pl.ANY)],
            out_specs=pl.BlockSpec(memory_space=pl.ANY),
            scratch_shapes=[pltpu.SemaphoreType.DMA(())]),
        out_shape=jax.ShapeDtypeStruct(xs_init.shape, xs_init.dtype),
        input_output_aliases={3: 0},
        compiler_params=pltpu.CompilerParams(dimension_semantics=("arbitrary",)),
        name="moe_dispatch",
    )(pos0, pos1, h2p, xs_init)


def _expert_kernel(te_ref, xs_ref, wg_ref, wu_ref, wd_ref, ys_ref, wg_bf, wu_bf, wd_bf):
    j = pl.program_id(0)
    used = j < te_ref[1, 0]
    new_expert = jnp.logical_or(j == 0, te_ref[0, j] != te_ref[0, jnp.maximum(j - 1, 0)])

    @pl.when(jnp.logical_not(used))
    def _():
        ys_ref[...] = jnp.zeros_like(ys_ref)

    @pl.when(used & new_expert)
    def _():
        wg_bf[...] = wg_ref[0, 0].astype(bf16)
        wu_bf[...] = wu_ref[0, 0].astype(bf16)
        wd_bf[...] = wd_ref[0, 0].astype(bf16)

    @pl.when(used)
    def _():
        x = xs_ref[...].astype(bf16)
        hg = jnp.dot(x, wg_bf[...], preferred_element_type=f32)
        hu = jnp.dot(x, wu_bf[...], preferred_element_type=f32)
        hid = (hg * jax.nn.sigmoid(hg) * hu).astype(bf16)
        ys_ref[...] = jnp.dot(hid, wd_bf[...], preferred_element_type=f32)


def _experts(te, xs, w_gate, w_up, w_down, layer):
    tm = MOE_TILE
    row = lambda j, te: (jnp.minimum(j, te[1, 0] - 1), 0)
    wspec = lambda shape: pl.BlockSpec((1, 1) + shape, lambda j, te: (layer, te[0, j], 0, 0))
    return pl.pallas_call(
        _expert_kernel,
        grid_spec=pltpu.PrefetchScalarGridSpec(
            num_scalar_prefetch=1,
            grid=(MOE_ROWS // tm,),
            in_specs=[pl.BlockSpec((tm, D_MODEL), row),
                      wspec((D_MODEL, D_EXPERT)), wspec((D_MODEL, D_EXPERT)), wspec((D_EXPERT, D_MODEL))],
            out_specs=pl.BlockSpec((tm, D_MODEL), lambda j, te: (j, 0)),
            scratch_shapes=[pltpu.VMEM((D_MODEL, D_EXPERT), bf16), pltpu.VMEM((D_MODEL, D_EXPERT), bf16),
                            pltpu.VMEM((D_EXPERT, D_MODEL), bf16)]),
        out_shape=jax.ShapeDtypeStruct((MOE_ROWS, D_MODEL), f32),
        compiler_params=pltpu.CompilerParams(dimension_semantics=("arbitrary",), vmem_limit_bytes=VMEM_LIMIT),
        name="moe_experts",
    )(te, xs, w_gate, w_up, w_down)


def _gather_expert_rows(pos0_ref, pos1_ref, ys_ref, buf, sem):
    rows = buf.shape[2]
    i = pl.program_id(0)
    slot = i % 2

    def issue(tile, sl):
        base = tile * rows

        def body(t, carry):
            for s, pos_ref in enumerate((pos0_ref, pos1_ref)):
                pltpu.make_async_copy(ys_ref.at[pl.ds(pos_ref[base + t], 1)], buf.at[sl, s, pl.ds(t, 1)],
                                      sem.at[sl]).start()
            return carry

        lax.fori_loop(0, rows, body, 0, unroll=8)

    @pl.when(i == 0)
    def _():
        issue(0, 0)

    @pl.when(i + 1 < pl.num_programs(0))
    def _():
        issue(i + 1, 1 - slot)

    for s in range(2):
        pltpu.make_async_copy(ys_ref.at[pl.ds(0, rows)], buf.at[slot, s], sem.at[slot]).wait()
    return buf[slot, 0], buf[slot, 1]


def _moe_residual(pos0_ref, pos1_ref, ys_ref, x1_ref, rc_ref, mod_ref, buf, sem):
    y0, y1 = _gather_expert_rows(pos0_ref, pos1_ref, ys_ref, buf, sem)
    rc = rc_ref[...]
    return x1_ref[...] + mod_ref[0][5:6] * (rc[:, 2:3] * y0 + rc[:, 3:4] * y1)


def _final_kernel(pos0_ref, pos1_ref, ys_ref, x1_ref, rc_ref, mod_ref, fn_ref, yc_ref, yl_ref, buf, sem):
    x2 = _moe_residual(pos0_ref, pos1_ref, ys_ref, x1_ref, rc_ref, mod_ref, buf, sem)
    y = x2 * lax.rsqrt(jnp.mean(x2 * x2, axis=-1, keepdims=True) + EPS) * fn_ref[...]
    is_ctx = pl.program_id(0) < N_CTX // x1_ref.shape[0]

    @pl.when(is_ctx)
    def _():
        yc_ref[...] = y

    @pl.when(jnp.logical_not(is_ctx))
    def _():
        yl_ref[...] = y


def _final_combine(pos0, pos1, ys, x1, rc, mods, fn):
    tc = COMBINE_TILE
    row = lambda i, p0, p1: (i, 0)
    n_ctx_tiles = N_CTX // tc
    return pl.pallas_call(
        _final_kernel,
        grid_spec=pltpu.PrefetchScalarGridSpec(
            num_scalar_prefetch=2,
            grid=(N_TOK // tc,),
            in_specs=[pl.BlockSpec(memory_space=pl.ANY),
                      pl.BlockSpec((tc, D_MODEL), row),
                      pl.BlockSpec((tc, LANE), row),
                      pl.BlockSpec((1, 6, D_MODEL), lambda i, p0, p1: (_mod_row(i, tc), 0, 0)),
                      pl.BlockSpec((1, D_MODEL), lambda i, p0, p1: (0, 0))],
            out_specs=[pl.BlockSpec((tc, D_MODEL), lambda i, p0, p1: (jnp.minimum(i, n_ctx_tiles - 1), 0)),
                       pl.BlockSpec((tc, D_MODEL), lambda i, p0, p1: (jnp.maximum(i - n_ctx_tiles, 0), 0))],
            scratch_shapes=[pltpu.VMEM((2, 2, tc, D_MODEL), f32), pltpu.SemaphoreType.DMA((2,))]),
        out_shape=[jax.ShapeDtypeStruct((N_CTX, D_MODEL), f32), jax.ShapeDtypeStruct((N_LAT, D_MODEL), f32)],
        compiler_params=pltpu.CompilerParams(dimension_semantics=("arbitrary",), vmem_limit_bytes=VMEM_LIMIT),
        name="moe_combine_final",
    )(pos0, pos1, ys, x1, rc, mods, fn)


def _moe_experts(h2, route_t, w_gate, w_up, w_down, layer, xs_buf):
    pos, te = _rank(route_t)
    pos0, pos1 = pos[0], pos[1]
    xs = _dispatch(pos0, pos1, h2, xs_buf)
    return pos0, pos1, _experts(te, xs, w_gate, w_up, w_down, layer), xs


def _pad_heads(w):
    lead = w.shape[:-1]
    w = w.reshape(*lead, ML_HEADS, ML_DIM)
    w = jnp.pad(w, [(0, 0)] * len(lead) + [(0, 0), (0, ML_PAD - ML_DIM)])
    return w.reshape(*lead, ML_PW)


def _pack_in_cols(wb):
    o = 0
    qa = wb[..., o:o + NA_WIDTH] * (NA_DIM ** -0.5)
    ka = wb[..., o + NA_WIDTH:o + 2 * NA_WIDTH]
    va = wb[..., o + 2 * NA_WIDTH:o + 3 * NA_WIDTH]
    o += 3 * NA_WIDTH
    qb, kb, vb, ob = [_pad_heads(wb[..., o + j * ML_WIDTH:o + (j + 1) * ML_WIDTH]) for j in range(4)]
    o += 4 * ML_WIDTH
    gates = wb[..., o:o + N_GATE_COLS]
    o += N_GATE_COLS
    pool = wb[..., o:o + POOL_WIDTH]
    main = jnp.concatenate([qa, ka, va, qb, vb, ob, pool], axis=-1)
    gates_p = jnp.pad(gates, [(0, 0)] * (gates.ndim - 1) + [(0, LANE - N_GATE_COLS)])
    return main, gates_p, jnp.concatenate([kb, gates], axis=-1)


def _pack_w_in(w, b):
    w_main, w_gates, w_feat = _pack_in_cols(w)
    b_main, b_gates, b_feat = _pack_in_cols(b.astype(f32))
    return (w_main.astype(bf16), b_main[:, None], w_gates.astype(bf16), b_gates[:, None],
            jnp.swapaxes(w_feat, 1, 2).astype(bf16), b_feat[:, :, None])


def _pack_w_out(w):
    n_l = w.shape[0]
    wb = w[:, NA_WIDTH:NA_WIDTH + ML_WIDTH].reshape(n_l, ML_HEADS, ML_DIM, D_MODEL)
    wb = jnp.pad(wb, ((0, 0), (0, 0), (0, ML_PAD - ML_DIM), (0, 0))).reshape(n_l, ML_PW, D_MODEL)
    return jnp.concatenate([w[:, :NA_WIDTH], wb, w[:, NA_WIDTH + ML_WIDTH:]], axis=1).astype(bf16)


def _block_diag(w):
    n_l, g, c, _ = w.shape
    eye = jnp.eye(g, dtype=w.dtype)
    return (eye[None, :, None, :, None] * w[:, :, :, None, :]).reshape(n_l, g * c, g * c)


def kernel(x_prompt, x_sample, cache_k_attn, cache_v_attn, state_mlstm_C, state_mlstm_n, state_mlstm_m, c, c_ctx,
           w_ada, b_ada, norm1, w_in, b_in, rpb, ml_norm, w_pool, pool_scale, w_out, norm2, w_router, b_router,
           w_gate, w_up, w_down, final_norm):
    dt = x_prompt.dtype
    x = jnp.concatenate([x_prompt.reshape(N_CTX, D_MODEL), x_sample.reshape(N_LAT, D_MODEL)], axis=0).astype(f32)
    cvec = jnp.concatenate([c_ctx[None], c, jnp.zeros((8 - 1 - DEC_BATCH, D_MODEL), c.dtype)], axis=0).astype(f32)
    mods_all = _ada(cvec, w_ada.astype(f32), b_ada.astype(f32))
    mods_all = mods_all[:, :1 + DEC_BATCH].reshape(DEPTH, 1 + DEC_BATCH, 6, D_MODEL)

    wr_t = w_router.astype(f32).T
    br_t = b_router.astype(f32)[:, None]
    fn = final_norm.astype(f32)[None]

    na_bias = _na_bias_tables(rpb)
    xs_buf = jnp.zeros((MOE_ROWS, D_MODEL), f32)
    in_params = (norm1.astype(f32)[:, None],) + _pack_w_in(w_in, b_in)
    out_params = (_block_diag(w_pool.astype(f32)).astype(bf16), pool_scale.astype(f32)[:, None],
                  _pad_heads(ml_norm.astype(f32))[:, None], _pack_w_out(w_out), norm2.astype(f32)[:, None])

    ks, vs, Cs, ns, ms = [], [], [], [], []
    pending = None
    for l in range(DEPTH):
        mods = mods_all[l]
        if pending is None:
            qkva, kv32, qvo, kt, gates, gates_t, pin = _in_proj(x, mods, l, *in_params)
        else:
            x, qkva, kv32, qvo, kt, gates, gates_t, pin = _moe_in_proj(*pending, mods_all[l - 1], mods, l, *in_params)
        ks.append(kv32[:N_CTX, :NA_WIDTH].reshape(BATCH, SEQ, NA_HEADS, NA_DIM))
        vs.append(kv32[:N_CTX, NA_WIDTH:].reshape(BATCH, SEQ, NA_HEADS, NA_DIM))

        oa_ctx = _ctx_attention(qkva.reshape(N_TOK // SEQ, SEQ, W_A))
        ck = (cache_k_attn[:, l].reshape(DEC_BATCH, PAST_LEN, NA_WIDTH)).astype(bf16)
        cv = (cache_v_attn[:, l].reshape(DEC_BATCH, PAST_LEN, NA_WIDTH)).astype(bf16)
        oa_lat = _neighborhood_attention(qkva.reshape(N_TOK // DEC_SEQ, DEC_SEQ, W_A), ck, cv, na_bias, l)

        c_l, m_l = _pack_ml_state(state_mlstm_C[:, l], state_mlstm_n[:, l], state_mlstm_m[:, l])
        hf, hb, c_fin, m_fin = _mlstm(qvo, kt, gates, gates_t, c_l, m_l)
        C_l, n_l, m_l2 = _unpack_ml_state(c_fin[:BATCH], m_fin[:BATCH])
        Cs.append(C_l)
        ns.append(n_l)
        ms.append(m_l2)

        x1, h2, route_t, rc = _out_proj(x, mods, l, oa_ctx.reshape(N_CTX, NA_WIDTH), oa_lat.reshape(N_LAT, NA_WIDTH),
                                        hf.reshape(N_TOK, ML_PW), hb.reshape(N_TOK, ML_PW), qvo, pin,
                                        *out_params, wr_t, br_t)
        pos0, pos1, ys, xs_buf = _moe_experts(h2, route_t, w_gate, w_up, w_down, l, xs_buf)
        pending = (pos0, pos1, ys, x1, rc)

    x = _final_combine(*pending, mods_all[DEPTH - 1], fn)
    y_prompt = x[0].reshape(BATCH, SEQ, D_MODEL).astype(dt)
    y_sample = x[1].reshape(DEC_BATCH, DEC_SEQ, D_MODEL).astype(dt)
    return (y_prompt, y_sample,
            jnp.stack(ks, axis=1).astype(dt), jnp.stack(vs, axis=1).astype(dt),
            jnp.stack(Cs, axis=1).astype(dt), jnp.stack(ns, axis=1).astype(dt), jnp.stack(ms, axis=1).astype(dt))
```

```python
import functools

import numpy as np
import jax
import jax.numpy as jnp
from jax import lax
from jax.experimental import pallas as pl
from jax.experimental.pallas import tpu as pltpu

D_MODEL = 1024
BATCH = 16
SEQ = 256
DEPTH = 4
DEC_BATCH = 2
DEC_SEQ = 4096
PAST_LEN = 256
GRID_W = 64
EPS = 1e-6
NEG_INF = -1e30
NA_HEADS = 6
NA_DIM = 64
NA_WIDTH = NA_HEADS * NA_DIM
NA_ROWS = 8
NA_COLS = 16
RPB_ROWS = 2 * NA_ROWS - 1
RPB_COLS = 2 * NA_COLS - 1
ML_HEADS = 4
ML_DIM = 96
ML_WIDTH = ML_HEADS * ML_DIM
POOL_WINDOWS = (2, 4, 8, 16)
POOL_GROUPS = 4
POOL_DIM = 64
POOL_WIDTH = POOL_GROUPS * POOL_DIM
N_GATE_COLS = 4 * ML_HEADS
N_EXPERTS = 16
N_EXPERT_GROUPS = 4
EXPERTS_PER_GROUP = N_EXPERTS // N_EXPERT_GROUPS
D_EXPERT = 512
ADA_DIM = 6 * D_MODEL

N_CTX = BATCH * SEQ
N_LAT = DEC_BATCH * DEC_SEQ
N_TOK = N_CTX + N_LAT
LANE = 128
ML_PAD = LANE
ML_PW = ML_HEADS * ML_PAD
CAUG = ML_PAD
NA_PAIRS = NA_HEADS // 2
TOK_TILE = 512
ML_CHUNK = 256
NA_QROWS = 4
NA_KROWS = NA_QROWS + NA_ROWS - 1
POOL_HALO = max(POOL_WINDOWS) // 2
POOL_BLOCK = 128
MOE_TILE = 512
MOE_CHUNK = 8
MOE_LOCAL_ROWS = -(-(2 * TOK_TILE + N_EXPERTS * (MOE_CHUNK - 1)) // LANE) * LANE
MOE_ROWS = -(-(2 * N_TOK + (N_TOK // TOK_TILE) * N_EXPERTS * (MOE_CHUNK - 1) + N_EXPERTS * (MOE_TILE - 1))
             // MOE_TILE) * MOE_TILE
COMBINE_TILE = 256
VMEM_LIMIT = 56 * 1024 * 1024

W_A = 3 * NA_WIDTH
W_B = 3 * ML_PW
N_TCOLS = ML_PW + N_GATE_COLS
W_MAIN = W_A + W_B + POOL_WIDTH

f32 = jnp.float32
bf16 = jnp.bfloat16
HI = lax.Precision.HIGHEST


def _nt(a, b, **kw):
    return lax.dot_general(a, b, (((1,), (1,)), ((), ())), preferred_element_type=f32, **kw)


def _mod_row(i, tile):
    n_ctx_tiles = N_CTX // tile
    per_batch = DEC_SEQ // tile
    return jnp.where(i < n_ctx_tiles, 0, 1 + (i - n_ctx_tiles) // per_batch)


def _ada_kernel(c_ref, w_ref, b_ref, o_ref):
    s = c_ref[...]
    s = s * jax.nn.sigmoid(s)
    o_ref[0] = jnp.dot(s.astype(bf16), w_ref[0].astype(bf16), preferred_element_type=f32) + b_ref[0]


def _ada(cvec, w_ada, b_ada):
    nj = ADA_DIM // D_MODEL
    return pl.pallas_call(
        _ada_kernel,
        grid=(DEPTH, nj),
        in_specs=[pl.BlockSpec((8, D_MODEL), lambda l, j: (0, 0)),
                  pl.BlockSpec((1, D_MODEL, D_MODEL), lambda l, j: (l, 0, j)),
                  pl.BlockSpec((1, 1, D_MODEL), lambda l, j: (l, 0, j))],
        out_specs=pl.BlockSpec((1, 8, D_MODEL), lambda l, j: (l, 0, j)),
        out_shape=jax.ShapeDtypeStruct((DEPTH, 8, ADA_DIM), f32),
        name="ada_mod",
    )(cvec, w_ada, b_ada.reshape(DEPTH, 1, ADA_DIM))


def _in_kernel(x_ref, *refs):
    _in_body(x_ref[...], *refs)


def _moe_in_kernel(pos0_ref, pos1_ref, ys_ref, x1_ref, rc_ref, mod_prev_ref, *refs):
    in_refs, x_out_ref, out_refs, (buf, sem) = refs[:8], refs[8], refs[9:-2], refs[-2:]
    x = _moe_residual(pos0_ref, pos1_ref, ys_ref, x1_ref, rc_ref, mod_prev_ref, buf, sem)
    x_out_ref[...] = x
    _in_body(x, *in_refs, *out_refs)


def _in_body(x, mod_ref, n1_ref, w_ref, b_ref, wg_ref, bg_ref, wt_ref, bt_ref,
             a_ref, kv_ref, b_out_ref, kt_ref, g_ref, gt_ref, pin_ref):
    mod = mod_ref[0]
    h = x * lax.rsqrt(jnp.mean(x * x, axis=-1, keepdims=True) + EPS) * n1_ref[...]
    h = (h * (1.0 + mod[1:2]) + mod[0:1]).astype(bf16)
    pa = jnp.dot(h, w_ref[:, 0:W_A], preferred_element_type=f32) + b_ref[:, 0:W_A]
    a_ref[...] = pa.astype(bf16)
    kv_ref[...] = pa[:, NA_WIDTH:W_A]
    for j in range(3):
        lo = W_A + j * ML_PW
        pb = jnp.dot(h, w_ref[:, lo:lo + ML_PW], preferred_element_type=f32) + b_ref[:, lo:lo + ML_PW]
        if j == 0:
            pb = pb * (ML_DIM ** -0.5)
        b_out_ref[:, j * ML_PW:(j + 1) * ML_PW] = pb.astype(bf16)
    lo = W_A + W_B
    pin_ref[...] = jnp.dot(h, w_ref[:, lo:lo + POOL_WIDTH], preferred_element_type=f32) + b_ref[:, lo:lo + POOL_WIDTH]
    g_ref[...] = jnp.dot(h, wg_ref[...], preferred_element_type=f32) + bg_ref[...]
    t = _nt(wt_ref[...], h) + bt_ref[...]
    kt_ref[...] = t[0:ML_PW].astype(bf16)
    gt_ref[...] = t[ML_PW:N_TCOLS]


def _in_proj_specs(layer):
    tm = TOK_TILE
    lyr = lambda shape: pl.BlockSpec((None,) + shape, lambda i, *_: (layer, 0, 0))
    rows = lambda width: pl.BlockSpec((tm, width), lambda i, *_: (i, 0))
    cols = lambda height: pl.BlockSpec((height, tm), lambda i, *_: (0, i))
    param_specs = [pl.BlockSpec((1, 6, D_MODEL), lambda i, *_: (_mod_row(i, tm), 0, 0)),
                   lyr((1, D_MODEL)), lyr((D_MODEL, W_MAIN)), lyr((1, W_MAIN)), lyr((D_MODEL, LANE)), lyr((1, LANE)),
                   lyr((N_TCOLS, D_MODEL)), lyr((N_TCOLS, 1))]
    out_specs = [rows(W_A), rows(2 * NA_WIDTH), rows(W_B), cols(ML_PW), rows(LANE), cols(N_GATE_COLS),
                 rows(POOL_WIDTH)]
    out_shape = [jax.ShapeDtypeStruct((N_TOK, W_A), bf16),
                 jax.ShapeDtypeStruct((N_TOK, 2 * NA_WIDTH), f32),
                 jax.ShapeDtypeStruct((N_TOK, W_B), bf16),
                 jax.ShapeDtypeStruct((ML_PW, N_TOK), bf16),
                 jax.ShapeDtypeStruct((N_TOK, LANE), f32),
                 jax.ShapeDtypeStruct((N_GATE_COLS, N_TOK), f32),
                 jax.ShapeDtypeStruct((N_TOK, POOL_WIDTH), f32)]
    return rows, param_specs, out_specs, out_shape


def _in_proj(x, mods, layer, *params):
    rows, param_specs, out_specs, out_shape = _in_proj_specs(layer)
    return pl.pallas_call(
        _in_kernel,
        grid=(N_TOK // TOK_TILE,),
        in_specs=[rows(D_MODEL)] + param_specs,
        out_specs=out_specs,
        out_shape=out_shape,
        compiler_params=pltpu.CompilerParams(dimension_semantics=("arbitrary",), vmem_limit_bytes=VMEM_LIMIT),
        name="in_proj",
    )(x, mods, *params)


def _moe_in_proj(pos0, pos1, ys, x1, rc, mods_prev, mods, layer, *params):
    tm = TOK_TILE
    rows, param_specs, out_specs, out_shape = _in_proj_specs(layer)
    return pl.pallas_call(
        _moe_in_kernel,
        grid_spec=pltpu.PrefetchScalarGridSpec(
            num_scalar_prefetch=2,
            grid=(N_TOK // tm,),
            in_specs=[pl.BlockSpec(memory_space=pl.ANY), rows(D_MODEL), rows(LANE),
                      pl.BlockSpec((1, 6, D_MODEL), lambda i, *_: (_mod_row(i, tm), 0, 0))] + param_specs,
            out_specs=[rows(D_MODEL)] + out_specs,
            scratch_shapes=[pltpu.VMEM((2, 2, tm, D_MODEL), f32), pltpu.SemaphoreType.DMA((2,))]),
        out_shape=[jax.ShapeDtypeStruct((N_TOK, D_MODEL), f32)] + out_shape,
        compiler_params=pltpu.CompilerParams(dimension_semantics=("arbitrary",), vmem_limit_bytes=VMEM_LIMIT),
        name="moe_combine_in_proj",
    )(pos0, pos1, ys, x1, rc, mods_prev, mods, *params)


def _pair_attention(qp, parts):
    lane = lax.broadcasted_iota(jnp.int32, (1, LANE), 1)
    outs = []
    for j in range(2):
        in_half = (lane >= j * NA_DIM) & (lane < (j + 1) * NA_DIM)
        qm = jnp.where(in_half, qp, jnp.zeros_like(qp))
        scores = []
        for k, _, bias in parts:
            s = _nt(qm, k)
            if bias is not None:
                s = s + bias[j]
            scores.append(s)
        m = scores[0].max(axis=-1, keepdims=True)
        for s in scores[1:]:
            m = jnp.maximum(m, s.max(axis=-1, keepdims=True))
        den = None
        acc = None
        for s, (_, v, _) in zip(scores, parts):
            p = jnp.exp(s - m)
            ps = p.sum(axis=-1, keepdims=True)
            den = ps if den is None else den + ps
            o = jnp.dot(p.astype(bf16), v, preferred_element_type=f32)
            acc = o if acc is None else acc + o
        outs.append(acc / den)
    return jnp.where(lane < NA_DIM, outs[0], outs[1])


def _ctx_attn_kernel(q_ref, k_ref, v_ref, o_ref):
    for p in range(NA_PAIRS):
        sl = slice(p * LANE, (p + 1) * LANE)
        o = _pair_attention(q_ref[0, :, sl], [(k_ref[0, :, sl], v_ref[0, :, sl], None)])
        o_ref[0, :, sl] = o.astype(bf16)


def _ctx_attention(qkv):
    blk = lambda c: pl.BlockSpec((1, SEQ, NA_WIDTH), lambda b, c=c: (b, 0, c))
    return pl.pallas_call(
        _ctx_attn_kernel,
        grid=(BATCH,),
        in_specs=[blk(0), blk(1), blk(2)],
        out_specs=pl.BlockSpec((1, SEQ, NA_WIDTH), lambda b: (b, 0, 0)),
        out_shape=jax.ShapeDtypeStruct((BATCH, SEQ, NA_WIDTH), bf16),
        name="ctx_attention",
    )(qkv, qkv, qkv)


def _na_window_start(rb):
    return jnp.clip(rb * NA_QROWS - NA_ROWS // 2, 0, DEC_SEQ // GRID_W - NA_KROWS)


def _na_bias(tab_ref, head, rb):
    rows = DEC_SEQ // GRID_W
    ws = _na_window_start(rb)
    lane = lax.broadcasted_iota(jnp.int32, (1, NA_KROWS * GRID_W), 1)
    per_qrow = []
    for dq in range(NA_QROWS):
        qr = rb * NA_QROWS + dq
        a0 = ws - qr + (NA_ROWS - 1) + NA_KROWS
        tiles = [tab_ref[head, a0 + 2 * j] for j in range((NA_KROWS + 1) // 2)]
        t = jnp.concatenate(tiles, axis=1)[:, :NA_KROWS * GRID_W]
        lo = (jnp.clip(qr - NA_ROWS // 2, 0, rows - NA_ROWS) - ws) * GRID_W
        ok = (lane >= lo) & (lane < lo + NA_ROWS * GRID_W)
        per_qrow.append(jnp.where(ok, t, NEG_INF))
    return jnp.concatenate(per_qrow, axis=0)


def _na_kernel(q_ref, k_ref, v_ref, ck_ref, cv_ref, tab_ref, o_ref):
    rb = pl.program_id(1)
    start = pl.multiple_of(_na_window_start(rb) * GRID_W, GRID_W)
    nk = NA_KROWS * GRID_W
    for p in range(NA_PAIRS):
        sl = slice(p * LANE, (p + 1) * LANE)
        bias = [_na_bias(tab_ref.at[0], 2 * p + j, rb) for j in range(2)]
        parts = [(k_ref[0, pl.ds(start, nk), sl], v_ref[0, pl.ds(start, nk), sl], bias),
                 (ck_ref[0, :, sl], cv_ref[0, :, sl], None)]
        o = _pair_attention(q_ref[0, :, sl], parts)
        o_ref[0, :, sl] = o.astype(bf16)


def _neighborhood_attention(qkv, ck, cv, tables, layer):
    nq = NA_QROWS * GRID_W
    n_rb = DEC_SEQ // nq
    return pl.pallas_call(
        _na_kernel,
        grid=(DEC_BATCH, n_rb),
        in_specs=[pl.BlockSpec((1, nq, NA_WIDTH), lambda b, r: (1 + b, r, 0)),
                  pl.BlockSpec((1, DEC_SEQ, NA_WIDTH), lambda b, r: (1 + b, 0, 1)),
                  pl.BlockSpec((1, DEC_SEQ, NA_WIDTH), lambda b, r: (1 + b, 0, 2)),
                  pl.BlockSpec((1, PAST_LEN, NA_WIDTH), lambda b, r: (b, 0, 0)),
                  pl.BlockSpec((1, PAST_LEN, NA_WIDTH), lambda b, r: (b, 0, 0)),
                  pl.BlockSpec((1,) + tables.shape[1:], lambda b, r: (layer, 0, 0, 0, 0))],
        out_specs=pl.BlockSpec((1, nq, NA_WIDTH), lambda b, r: (b, r, 0)),
        out_shape=jax.ShapeDtypeStruct((DEC_BATCH, DEC_SEQ, NA_WIDTH), bf16),
        compiler_params=pltpu.CompilerParams(dimension_semantics=("arbitrary", "arbitrary"),
                                             vmem_limit_bytes=VMEM_LIMIT),
        name="neighborhood_attention",
    )(qkv, qkv, qkv, ck, cv, tables)


def _na_bias_tables(rpb):
    qc = np.arange(GRID_W)[:, None]
    kc = np.arange(GRID_W)[None, :]
    dc = np.clip(kc - qc + NA_COLS - 1, 0, RPB_COLS - 1)
    col_start = np.clip(qc - NA_COLS // 2, 0, GRID_W - NA_COLS)
    col_ok = (kc >= col_start) & (kc < col_start + NA_COLS)
    pick_col = (dc[None] == np.arange(RPB_COLS)[:, None, None]).astype(np.float32)
    by_col = jnp.einsum('lhab,bqk->lhaqk', rpb.astype(f32), pick_col, precision=HI)
    by_col = jnp.where(col_ok[None, None, None], by_col, NEG_INF)
    by_col = jnp.pad(by_col, ((0, 0), (0, 0), (NA_KROWS, NA_KROWS + 1), (0, 0), (0, 0)), constant_values=NEG_INF)
    return jnp.concatenate([by_col[:, :, :-1], by_col[:, :, 1:]], axis=-1)


def _log_sigmoid(x):
    return -(jnp.maximum(-x, 0.0) + jnp.log(1.0 + jnp.exp(-jnp.abs(x))))


def _split3(x):
    hi = x.astype(bf16)
    r1 = x - hi.astype(f32)
    mid = r1.astype(bf16)
    lo = (r1 - mid.astype(f32)).astype(bf16)
    return hi, mid, lo


def _mlstm_kernel(qf_ref, vf_ref, ktf_ref, gf_ref, gtf_ref, qb_ref, vb_ref, ktb_ref, gb_ref, gtb_ref,
                  c0_ref, m0_ref, hf_ref, hb_ref, c_out_ref, m_out_ref, c_scr, m_scr):
    L = ML_CHUNK
    seq, c, n_chunks, _ = _ml_schedule(pl.program_id(0))

    @pl.when(c == 0)
    def _():
        is_ctx = seq < BATCH
        c_scr[...] = jnp.where(is_ctx, 0.0, c0_ref[0])
        m_scr[...] = jnp.where(is_ctx, 0.0, m0_ref[0])

    ri = lax.broadcasted_iota(jnp.int32, (L, L), 0)
    ci = lax.broadcasted_iota(jnp.int32, (L, L), 1)
    lane = lax.broadcasted_iota(jnp.int32, (L, ML_PAD), 1)
    is_ncol = lane == ML_DIM
    lower = ri >= ci
    upper = ri <= ci
    lower_b = jnp.where(lower, 1.0, 0.0).astype(bf16)
    upper_b = jnp.where(upper, 1.0, 0.0).astype(bf16)
    dirs = ((qf_ref, ktf_ref, vf_ref, gf_ref, gtf_ref, hf_ref), (qb_ref, ktb_ref, vb_ref, gb_ref, gtb_ref, hb_ref))
    for d, (q_ref, kt_ref, v_ref, g_ref, gt_ref, h_ref) in enumerate(dirs):
        g = g_ref[...][:, 0:N_GATE_COLS]
        gt = gt_ref[...]
        lf_c = _log_sigmoid(g)
        lf_r = _log_sigmoid(gt)
        b_cols = sum(jnp.dot(lower_b, part, preferred_element_type=f32) for part in _split3(lf_c))
        b_rows = sum(jnp.dot(part, upper_b, preferred_element_type=f32) for part in _split3(lf_r))
        tot_c = jnp.sum(lf_c, axis=0, keepdims=True)
        tot_r = jnp.sum(lf_r, axis=1, keepdims=True)
        visible = lower
        if d == 1:
            b_cols = tot_c - b_cols + lf_c
            b_rows = tot_r - b_rows + lf_r
            visible = upper
        for hd in range(ML_HEADS):
            st = d * ML_HEADS + hd
            ci_ = 2 * ML_HEADS * d + hd
            cf_ = ci_ + ML_HEADS
            sl = slice(hd * ML_PAD, (hd + 1) * ML_PAD)
            bc = b_cols[:, cf_:cf_ + 1]
            br = b_rows[cf_:cf_ + 1, :]
            li_r = gt[ci_:ci_ + 1, :]
            m_prev = m_scr[st:st + 1, 0:1]
            dmat = jnp.where(visible, bc - br + li_r, NEG_INF)
            inter = bc + m_prev
            m_t = jnp.maximum(inter, dmat.max(axis=-1, keepdims=True))
            w_intra = jnp.exp(dmat - m_t)
            w_inter = jnp.exp(inter - m_t)
            qh = q_ref[0, :, sl]
            kht = kt_ref[sl, :]
            v_aug = jnp.where(is_ncol, jnp.ones((), bf16), v_ref[0, :, sl])
            s = (jnp.dot(qh, kht, preferred_element_type=f32) * w_intra).astype(bf16)
            c_aug = c_scr[st]
            na = (w_inter * jnp.dot(qh, c_aug.astype(bf16), preferred_element_type=f32)
                  + jnp.dot(s, v_aug, preferred_element_type=f32))
            den = na[:, ML_DIM:ML_DIM + 1]
            h_ref[0, :, sl] = jnp.where(lane < ML_DIM, na / jnp.maximum(jnp.abs(den), jnp.exp(-m_t)), 0.0)
            b_end = tot_r[cf_:cf_ + 1, :]
            g_row = b_end - br + li_r
            m_new = jnp.maximum(b_end + m_prev, g_row.max(axis=1, keepdims=True))
            decay = jnp.exp(b_end + m_prev - m_new)
            kwt = (kht.astype(f32) * jnp.exp(g_row - m_new)).astype(bf16)
            c_scr[st] = decay * c_aug + jnp.dot(kwt, v_aug, preferred_element_type=f32)
            m_scr[st:st + 1, :] = jnp.broadcast_to(m_new, (1, LANE))

    @pl.when(c == n_chunks - 1)
    def _():
        c_out_ref[0] = c_scr[...]
        m_out_ref[0] = m_scr[...]


def _ml_schedule(s):
    nc_ctx, nc_lat = SEQ // ML_CHUNK, DEC_SEQ // ML_CHUNK
    n_ctx_steps = BATCH * nc_ctx
    is_ctx = s < n_ctx_steps
    t = s - n_ctx_steps
    seq = jnp.where(is_ctx, s // nc_ctx, BATCH + t // nc_lat)
    c = jnp.where(is_ctx, s % nc_ctx, t % nc_lat)
    nc = jnp.where(is_ctx, nc_ctx, nc_lat)
    base = jnp.where(is_ctx, (s // nc_ctx) * nc_ctx, n_ctx_steps + (t // nc_lat) * nc_lat)
    return seq, c, nc, base


def _mlstm(qvo, kt, gates, gates_t, c0, m0):
    L = ML_CHUNK
    n_seq = BATCH + DEC_BATCH

    def fwd(s):
        _, c, _, base = _ml_schedule(s)
        return base + c

    def bwd(s):
        _, c, nc, base = _ml_schedule(s)
        return base + nc - 1 - c

    seq_of = lambda s: _ml_schedule(s)[0]
    lat_of = lambda s: jnp.maximum(seq_of(s) - BATCH, 0)

    def specs(pos):
        return [pl.BlockSpec((1, L, ML_PW), lambda s, j=j: (pos(s), 0, j)) for j in range(2)] + [
            pl.BlockSpec((ML_PW, L), lambda s: (0, pos(s))),
            pl.BlockSpec((L, LANE), lambda s: (pos(s), 0)),
            pl.BlockSpec((N_GATE_COLS, L), lambda s: (0, pos(s)))]

    q3 = qvo.reshape(N_TOK // L, L, W_B)
    n_str = 2 * ML_HEADS
    return pl.pallas_call(
        _mlstm_kernel,
        grid=(N_TOK // L,),
        in_specs=specs(fwd) + specs(bwd) + [
            pl.BlockSpec((1, n_str, ML_PAD, CAUG), lambda s: (lat_of(s), 0, 0, 0)),
            pl.BlockSpec((1, n_str, LANE), lambda s: (lat_of(s), 0, 0))],
        out_specs=[pl.BlockSpec((1, L, ML_PW), lambda s: (fwd(s), 0, 0)),
                   pl.BlockSpec((1, L, ML_PW), lambda s: (bwd(s), 0, 0)),
                   pl.BlockSpec((1, n_str, ML_PAD, CAUG), lambda s: (seq_of(s), 0, 0, 0)),
                   pl.BlockSpec((1, n_str, LANE), lambda s: (seq_of(s), 0, 0))],
        out_shape=[jax.ShapeDtypeStruct((N_TOK // L, L, ML_PW), f32),
                   jax.ShapeDtypeStruct((N_TOK // L, L, ML_PW), f32),
                   jax.ShapeDtypeStruct((n_seq, n_str, ML_PAD, CAUG), f32),
                   jax.ShapeDtypeStruct((n_seq, n_str, LANE), f32)],
        scratch_shapes=[pltpu.VMEM((n_str, ML_PAD, CAUG), f32), pltpu.VMEM((n_str, LANE), f32)],
        compiler_params=pltpu.CompilerParams(dimension_semantics=("arbitrary",), vmem_limit_bytes=VMEM_LIMIT),
        name="mlstm",
    )(q3, q3, kt, gates, gates_t, q3, q3, kt, gates, gates_t, c0, m0)


def _pack_ml_state(C, n, m):
    B = C.shape[0]
    c_aug = jnp.zeros((B, 2, ML_HEADS, ML_PAD, CAUG), f32)
    c_aug = c_aug.at[:, :, :, :ML_DIM, :ML_DIM].set(C.astype(f32))
    c_aug = c_aug.at[:, :, :, :ML_DIM, ML_DIM].set(n.astype(f32))
    m_b = jnp.broadcast_to(m.astype(f32)[..., None], (B, 2, ML_HEADS, LANE))
    return c_aug.reshape(B, 2 * ML_HEADS, ML_PAD, CAUG), m_b.reshape(B, 2 * ML_HEADS, LANE)


def _unpack_ml_state(c_aug, m_b):
    B = c_aug.shape[0]
    c_aug = c_aug.reshape(B, 2, ML_HEADS, ML_PAD, CAUG)
    return (c_aug[:, :, :, :ML_DIM, :ML_DIM], c_aug[:, :, :, :ML_DIM, ML_DIM],
            m_b.reshape(B, 2, ML_HEADS, LANE)[..., 0])


def _pool_rows(u_prev, u_cur, u_next, w_bd, scale, t0, seq_len):
    tm = u_cur.shape[0]
    u_win = jnp.concatenate([u_prev, u_cur, u_next], axis=0)
    u_hi = u_win.astype(bf16)
    u_lo = (u_win - u_hi.astype(f32)).astype(bf16)
    lane = lax.broadcasted_iota(jnp.int32, (1, LANE), 1)
    blocks = []
    for r0 in range(0, tm, POOL_BLOCK):
        win = slice(r0, r0 + POOL_BLOCK + 2 * POOL_HALO)
        t_abs = t0 + r0 + lax.broadcasted_iota(jnp.int32, (POOL_BLOCK, 1), 0)
        s_abs = t0 + r0 - POOL_HALO + lax.broadcasted_iota(jnp.int32, (1, POOL_BLOCK + 2 * POOL_HALO), 1)
        t_loc = t_abs & (seq_len - 1)
        seq_start = t_abs - t_loc
        means = []
        for w in POOL_WINDOWS:
            lo = jnp.maximum(t_loc - w // 2, 0)
            hi = jnp.minimum(t_loc - w // 2 + w, seq_len)
            in_win = (s_abs >= seq_start + lo) & (s_abs < seq_start + hi)
            means.append((jnp.where(in_win, 1.0, 0.0).astype(bf16), 1.0 / (hi - lo).astype(f32)))
        pooled = []
        for p in range(POOL_GROUPS // 2):
            sl = slice(p * LANE, (p + 1) * LANE)
            halves = []
            for a, inv_cnt in means[2 * p:2 * p + 2]:
                tot = (jnp.dot(a, u_hi[win, sl], preferred_element_type=f32)
                       + jnp.dot(a, u_lo[win, sl], preferred_element_type=f32))
                halves.append(tot * inv_cnt)
            pooled.append(jnp.where(lane < POOL_DIM, halves[0], halves[1]) - u_cur[r0:r0 + POOL_BLOCK, sl])
        blocks.append(jnp.concatenate(pooled, axis=1))
    pooled = jnp.concatenate(blocks, axis=0).astype(bf16)
    return jnp.dot(pooled, w_bd, preferred_element_type=f32) * scale


def _top2_sum(a, b, c, d):
    hi1, lo1 = jnp.maximum(a, b), jnp.minimum(a, b)
    hi2, lo2 = jnp.maximum(c, d), jnp.minimum(c, d)
    return jnp.maximum(hi1, hi2) + jnp.maximum(jnp.minimum(hi1, hi2), jnp.maximum(lo1, lo2))


def _first_match(vals, target):
    idx = jnp.full_like(target, float(len(vals) - 1))
    for i in range(len(vals) - 2, -1, -1):
        idx = jnp.where(vals[i] == target, float(i), idx)
    return idx


def _pick(vals, idx):
    out = vals[-1]
    for i in range(len(vals) - 2, -1, -1):
        out = jnp.where(idx == float(i), vals[i], out)
    return out


def _route(logits_t, bias_t):
    scores = jax.nn.sigmoid(logits_t)
    sel = scores + bias_t
    row = lambda a, i: a[i:i + 1, :]
    grp = [_top2_sum(*[row(sel, EXPERTS_PER_GROUP * g + i) for i in range(EXPERTS_PER_GROUP)])
           for g in range(N_EXPERT_GROUPS)]
    best = functools.reduce(jnp.maximum, grp)
    gidx = _first_match(grp, best)
    sel_g = [_pick([row(sel, EXPERTS_PER_GROUP * g + i) for g in range(N_EXPERT_GROUPS)], gidx)
             for i in range(EXPERTS_PER_GROUP)]
    sco_g = [_pick([row(scores, EXPERTS_PER_GROUP * g + i) for g in range(N_EXPERT_GROUPS)], gidx)
             for i in range(EXPERTS_PER_GROUP)]
    i0 = _first_match(sel_g, functools.reduce(jnp.maximum, sel_g))
    rest = [jnp.where(i0 == float(i), -jnp.inf, sel_g[i]) for i in range(EXPERTS_PER_GROUP)]
    i1 = _first_match(rest, functools.reduce(jnp.maximum, rest))
    s0, s1 = _pick(sco_g, i0), _pick(sco_g, i1)
    tot = s0 + s1
    rid = lax.broadcasted_iota(jnp.int32, (LANE, logits_t.shape[1]), 0)
    rows = (EXPERTS_PER_GROUP * gidx + i0, EXPERTS_PER_GROUP * gidx + i1, s0 / tot, s1 / tot)
    out = jnp.zeros(rid.shape, f32)
    for i, r in enumerate(rows):
        out = jnp.where(rid == i, r, out)
    return out


def _out_kernel(x_ref, mod_ref, oac_ref, oal_ref, hf_ref, hb_ref, ob_ref, up_ref, uc_ref, un_ref, wp_ref, psc_ref,
                mln_ref, wo_ref, n2_ref, wr_ref, br_ref, x1_ref, h2_ref, rt_ref, rc_ref):
    tm = x_ref.shape[0]
    i = pl.program_id(0)
    is_ctx = i < N_CTX // tm
    mod = mod_ref[0]
    out_a = jnp.where(is_ctx, oac_ref[...], oal_ref[...])
    out_c = _pool_rows(up_ref[...], uc_ref[...], un_ref[...], wp_ref[...], psc_ref[...], i * tm,
                       jnp.where(is_ctx, SEQ, DEC_SEQ)).astype(bf16)
    hsum = hf_ref[...] + hb_ref[...]
    outs_b = []
    for hd in range(ML_HEADS):
        sl = slice(hd * ML_PAD, (hd + 1) * ML_PAD)
        hh = hsum[:, sl]
        ms = jnp.sum(hh * hh, axis=-1, keepdims=True) * (1.0 / ML_DIM)
        hn = hh * lax.rsqrt(ms + EPS) * mln_ref[:, sl]
        outs_b.append((jax.nn.sigmoid(ob_ref[:, sl].astype(f32)) * hn).astype(bf16))
    out_b = jnp.concatenate(outs_b, axis=1)
    mixed = (jnp.dot(out_a, wo_ref[0:NA_WIDTH, :], preferred_element_type=f32)
             + jnp.dot(out_b, wo_ref[NA_WIDTH:NA_WIDTH + ML_PW, :], preferred_element_type=f32)
             + jnp.dot(out_c, wo_ref[NA_WIDTH + ML_PW:, :], preferred_element_type=f32))
    x1 = x_ref[...] + mod[2:3] * mixed
    x1_ref[...] = x1
    h2 = x1 * lax.rsqrt(jnp.mean(x1 * x1, axis=-1, keepdims=True) + EPS) * n2_ref[...]
    h2 = h2 * (1.0 + mod[4:5]) + mod[3:4]
    h2_ref[...] = h2.astype(bf16)
    route_t = _route(_nt(wr_ref[...], h2, precision=HI), br_ref[...])
    rt_ref[...] = route_t[0:8]
    rc_ref[...] = route_t.T


def _out_proj(x, mods, layer, oa_ctx, oa_lat, hf, hb, qvo, pin, w_bd, psc, mln, wo, n2, wr_t, br_t):
    tm = TOK_TILE
    const = lambda i: (0, 0)
    lyr = lambda shape: pl.BlockSpec((None,) + shape, lambda i: (layer, 0, 0))
    row = lambda i: (i, 0)
    n_ctx_tiles = N_CTX // tm
    halo_blocks = tm // POOL_HALO
    return pl.pallas_call(
        _out_kernel,
        grid=(N_TOK // tm,),
        in_specs=[pl.BlockSpec((tm, D_MODEL), row),
                  pl.BlockSpec((1, 6, D_MODEL), lambda i: (_mod_row(i, tm), 0, 0)),
                  pl.BlockSpec((tm, NA_WIDTH), lambda i: (jnp.minimum(i, n_ctx_tiles - 1), 0)),
                  pl.BlockSpec((tm, NA_WIDTH), lambda i: (jnp.maximum(i - n_ctx_tiles, 0), 0)),
                  pl.BlockSpec((tm, ML_PW), row),
                  pl.BlockSpec((tm, ML_PW), row),
                  pl.BlockSpec((tm, ML_PW), lambda i: (i, 2)),
                  pl.BlockSpec((POOL_HALO, POOL_WIDTH), lambda i: (jnp.maximum(i * halo_blocks - 1, 0), 0)),
                  pl.BlockSpec((tm, POOL_WIDTH), row),
                  pl.BlockSpec((POOL_HALO, POOL_WIDTH),
                               lambda i: (jnp.minimum((i + 1) * halo_blocks, N_TOK // POOL_HALO - 1), 0)),
                  lyr((POOL_WIDTH, POOL_WIDTH)), lyr((1, POOL_WIDTH)), lyr((1, ML_PW)),
                  lyr((NA_WIDTH + ML_PW + POOL_WIDTH, D_MODEL)), lyr((1, D_MODEL)),
                  pl.BlockSpec((N_EXPERTS, D_MODEL), const),
                  pl.BlockSpec((N_EXPERTS, 1), const)],
        out_specs=[pl.BlockSpec((tm, D_MODEL), row),
                   pl.BlockSpec((tm, D_MODEL), row),
                   pl.BlockSpec((8, tm), lambda i: (0, i)),
                   pl.BlockSpec((tm, LANE), row)],
        out_shape=[jax.ShapeDtypeStruct((N_TOK, D_MODEL), f32),
                   jax.ShapeDtypeStruct((N_TOK, D_MODEL), bf16),
                   jax.ShapeDtypeStruct((8, N_TOK), f32),
                   jax.ShapeDtypeStruct((N_TOK, LANE), f32)],
        compiler_params=pltpu.CompilerParams(dimension_semantics=("arbitrary",), vmem_limit_bytes=VMEM_LIMIT),
        name="out_proj_router",
    )(x, mods, oa_ctx, oa_lat, hf, hb, qvo, pin, pin, pin, w_bd, psc, mln, wo, n2, wr_t, br_t)


def _ceil_to(x, m):
    return jnp.floor((x + (m - 1)) * (1.0 / m)) * m


def _prefix_over_experts(v):
    er = lax.broadcasted_iota(jnp.int32, (N_EXPERTS, N_EXPERTS), 0)
    ec = lax.broadcasted_iota(jnp.int32, (N_EXPERTS, N_EXPERTS), 1)
    return jnp.dot(jnp.where(ec < er, 1.0, 0.0), v, preferred_element_type=f32, precision=HI)


def _experts_to_lanes(v):
    sub = lax.broadcasted_iota(jnp.int32, (N_EXPERTS, LANE), 0)
    lane = lax.broadcasted_iota(jnp.int32, (N_EXPERTS, LANE), 1)
    return jnp.sum(jnp.where(sub == lane, v, 0.0), axis=0, keepdims=True)


def _rank_kernel(rt_ref, pos_ref, te_ref, tab_ref, carry_ref):
    tm = rt_ref.shape[1]
    p = pl.program_id(0)
    i = pl.program_id(1)
    rid = lax.broadcasted_iota(jnp.int32, (N_EXPERTS, tm), 0).astype(f32)
    oh0 = rid == rt_ref[0:1, :]
    oh1 = rid == rt_ref[1:2, :]
    both = jnp.where(oh0 | oh1, 1.0, 0.0)
    runs = jnp.broadcast_to(_ceil_to(jnp.sum(both, axis=1, keepdims=True), MOE_CHUNK), (N_EXPERTS, LANE))

    @pl.when((p == 0) & (i == 0))
    def _():
        carry_ref[...] = jnp.zeros_like(carry_ref)

    @pl.when((p == 1) & (i == 0))
    def _():
        padded = _ceil_to(carry_ref[...], MOE_TILE)
        off = _prefix_over_experts(padded)
        carry_ref[...] = off
        total = jnp.sum(padded, axis=0, keepdims=True)
        n_used = total * (1.0 / MOE_TILE)
        tile = lax.broadcasted_iota(jnp.int32, (1, LANE), 1).astype(f32)
        row0 = jnp.minimum(tile, n_used - 1.0) * MOE_TILE
        expert = jnp.sum(jnp.where(off <= row0, 1.0, 0.0), axis=0, keepdims=True) - 1.0
        sub = lax.broadcasted_iota(jnp.int32, (8, LANE), 0)
        te_ref[...] = jnp.where(sub == 0, expert, jnp.where(sub == 1, n_used, 0.0)).astype(jnp.int32)

    @pl.when(p == 1)
    def _():
        sr = lax.broadcasted_iota(jnp.int32, (tm, tm), 0)
        sc = lax.broadcasted_iota(jnp.int32, (tm, tm), 1)
        earlier = jnp.dot(both.astype(bf16), jnp.where(sr < sc, 1.0, 0.0).astype(bf16),
                          preferred_element_type=f32)
        g_off = carry_ref[...]
        l_off = _prefix_over_experts(runs)
        g_row = g_off[:, 0:1] + earlier
        l_row = l_off[:, 0:1] + earlier
        pick = lambda oh, v: jnp.sum(jnp.where(oh, v, 0.0), axis=0, keepdims=True)
        rows = (pick(oh0, g_row), pick(oh1, g_row), pick(oh0, l_row), pick(oh1, l_row))
        sub = lax.broadcasted_iota(jnp.int32, (8, tm), 0)
        out = jnp.zeros((8, tm), f32)
        for k, r in enumerate(rows):
            out = jnp.where(sub == k, r, out)
        pos_ref[...] = out.astype(jnp.int32)
        sub = lax.broadcasted_iota(jnp.int32, (8, LANE), 0)
        tab = jnp.zeros((8, LANE), f32)
        for k, v in enumerate((runs * (1.0 / MOE_CHUNK), l_off, g_off)):
            tab = jnp.where(sub == k, _experts_to_lanes(v), tab)
        tab_ref[0] = tab.astype(jnp.int32)

    carry_ref[...] += runs


def _rank(route_t):
    tm = TOK_TILE
    n_tiles = N_TOK // tm
    return pl.pallas_call(
        _rank_kernel,
        grid=(2, n_tiles),
        in_specs=[pl.BlockSpec((8, tm), lambda p, i: (0, i))],
        out_specs=[pl.BlockSpec((8, tm), lambda p, i: (0, i * p)),
                   pl.BlockSpec((8, LANE), lambda p, i: (0, 0)),
                   pl.BlockSpec((1, 8, LANE), lambda p, i: (i * p, 0, 0))],
        out_shape=[jax.ShapeDtypeStruct((8, N_TOK), jnp.int32),
                   jax.ShapeDtypeStruct((8, LANE), jnp.int32),
                   jax.ShapeDtypeStruct((n_tiles, 8, LANE), jnp.int32)],
        scratch_shapes=[pltpu.VMEM((N_EXPERTS, LANE), f32)],
        compiler_params=pltpu.CompilerParams(dimension_semantics=("arbitrary", "arbitrary")),
        name="moe_rank",
    )(route_t)


def _dispatch_kernel(tab_ref, h_ref, rows_ref, xs_in_ref, xs_ref, loc, sem):
    del xs_in_ref
    tm = h_ref.shape[0]
    i = pl.program_id(0)
    rid = lax.broadcasted_iota(jnp.int32, (MOE_LOCAL_ROWS, tm), 0)
    sel = (rid == rows_ref[2:3, :]) | (rid == rows_ref[3:4, :])
    loc[...] = jnp.dot(jnp.where(sel, 1.0, 0.0).astype(bf16), h_ref[...], preferred_element_type=f32)

    def chunk_copy(src_row, dst_row):
        return pltpu.make_async_copy(loc.at[pl.ds(pl.multiple_of(src_row, MOE_CHUNK), MOE_CHUNK)],
                                     xs_ref.at[pl.ds(pl.multiple_of(dst_row, MOE_CHUNK), MOE_CHUNK)], sem)

    n_total = 0
    for e in range(N_EXPERTS):
        n_chunks, l_off, g_off = (tab_ref[(3 * i + k) * N_EXPERTS + e] for k in range(3))

        def issue(c, carry, l_off=l_off, g_off=g_off):
            chunk_copy(l_off + c * MOE_CHUNK, g_off + c * MOE_CHUNK).start()
            return carry

        lax.fori_loop(0, n_chunks, issue, 0)
        n_total = n_total + n_chunks

    def wait_one(c, carry):
        chunk_copy(0, 0).wait()
        return carry

    lax.fori_loop(0, n_total, wait_one, 0)


def _dispatch(run_table, h2, rows, xs_init):
    tm = TOK_TILE
    return pl.pallas_call(
        _dispatch_kernel,
        grid_spec=pltpu.PrefetchScalarGridSpec(
            num_scalar_prefetch=1,
            grid=(N_TOK // tm,),
            in_specs=[pl.BlockSpec((tm, D_MODEL), lambda i, tab: (i, 0)),
                      pl.BlockSpec((8, tm), lambda i, tab: (0, i)),
                      pl.BlockSpec(memory_space=pl.ANY)],
            out_specs=pl.BlockSpec(memory_space=pl.ANY),
            scratch_shapes=[pltpu.VMEM((MOE_LOCAL_ROWS, D_MODEL), f32), pltpu.SemaphoreType.DMA(())]),
        out_shape=jax.ShapeDtypeStruct(xs_init.shape, xs_init.dtype),
        input_output_aliases={3: 0},
        compiler_params=pltpu.CompilerParams(dimension_semantics=("arbitrary",), vmem_limit_bytes=VMEM_LIMIT),
        name="moe_dispatch",
    )(run_table, h2, rows, xs_init)


def _expert_kernel(te_ref, xs_ref, wg_ref, wu_ref, wd_ref, ys_ref, wg_bf, wu_bf, wd_bf):
    j = pl.program_id(0)
    used = j < te_ref[1, 0]
    new_expert = jnp.logical_or(j == 0, te_ref[0, j] != te_ref[0, jnp.maximum(j - 1, 0)])

    @pl.when(jnp.logical_not(used))
    def _():
        ys_ref[...] = jnp.zeros_like(ys_ref)

    @pl.when(used & new_expert)
    def _():
        wg_bf[...] = wg_ref[0, 0].astype(bf16)
        wu_bf[...] = wu_ref[0, 0].astype(bf16)
        wd_bf[...] = wd_ref[0, 0].astype(bf16)

    @pl.when(used)
    def _():
        x = xs_ref[...].astype(bf16)
        hg = jnp.dot(x, wg_bf[...], preferred_element_type=f32)
        hu = jnp.dot(x, wu_bf[...], preferred_element_type=f32)
        hid = (hg * jax.nn.sigmoid(hg) * hu).astype(bf16)
        ys_ref[...] = jnp.dot(hid, wd_bf[...], preferred_element_type=f32)


def _experts(te, xs, w_gate, w_up, w_down, layer):
    tm = MOE_TILE
    row = lambda j, te: (jnp.minimum(j, te[1, 0] - 1), 0)
    wspec = lambda shape: pl.BlockSpec((1, 1) + shape, lambda j, te: (layer, te[0, j], 0, 0))
    return pl.pallas_call(
        _expert_kernel,
        grid_spec=pltpu.PrefetchScalarGridSpec(
            num_scalar_prefetch=1,
            grid=(MOE_ROWS // tm,),
            in_specs=[pl.BlockSpec((tm, D_MODEL), row),
                      wspec((D_MODEL, D_EXPERT)), wspec((D_MODEL, D_EXPERT)), wspec((D_EXPERT, D_MODEL))],
            out_specs=pl.BlockSpec((tm, D_MODEL), lambda j, te: (j, 0)),
            scratch_shapes=[pltpu.VMEM((D_MODEL, D_EXPERT), bf16), pltpu.VMEM((D_MODEL, D_EXPERT), bf16),
                            pltpu.VMEM((D_EXPERT, D_MODEL), bf16)]),
        out_shape=jax.ShapeDtypeStruct((MOE_ROWS, D_MODEL), f32),
        compiler_params=pltpu.CompilerParams(dimension_semantics=("arbitrary",), vmem_limit_bytes=VMEM_LIMIT),
        name="moe_experts",
    )(te, xs, w_gate, w_up, w_down)


def _gather_expert_rows(pos0_ref, pos1_ref, ys_ref, buf, sem):
    rows = buf.shape[2]
    i = pl.program_id(0)
    slot = i % 2

    def issue(tile, sl):
        base = tile * rows

        def body(t, carry):
            for s, pos_ref in enumerate((pos0_ref, pos1_ref)):
                pltpu.make_async_copy(ys_ref.at[pl.ds(pos_ref[base + t], 1)], buf.at[sl, s, pl.ds(t, 1)],
                                      sem.at[sl]).start()
            return carry

        lax.fori_loop(0, rows, body, 0, unroll=8)

    @pl.when(i == 0)
    def _():
        issue(0, 0)

    @pl.when(i + 1 < pl.num_programs(0))
    def _():
        issue(i + 1, 1 - slot)

    for s in range(2):
        pltpu.make_async_copy(ys_ref.at[pl.ds(0, rows)], buf.at[slot, s], sem.at[slot]).wait()
    return buf[slot, 0], buf[slot, 1]


def _moe_residual(pos0_ref, pos1_ref, ys_ref, x1_ref, rc_ref, mod_ref, buf, sem):
    y0, y1 = _gather_expert_rows(pos0_ref, pos1_ref, ys_ref, buf, sem)
    rc = rc_ref[...]
    return x1_ref[...] + mod_ref[0][5:6] * (rc[:, 2:3] * y0 + rc[:, 3:4] * y1)


def _final_kernel(pos0_ref, pos1_ref, ys_ref, x1_ref, rc_ref, mod_ref, fn_ref, yc_ref, yl_ref, buf, sem):
    x2 = _moe_residual(pos0_ref, pos1_ref, ys_ref, x1_ref, rc_ref, mod_ref, buf, sem)
    y = x2 * lax.rsqrt(jnp.mean(x2 * x2, axis=-1, keepdims=True) + EPS) * fn_ref[...]
    is_ctx = pl.program_id(0) < N_CTX // x1_ref.shape[0]

    @pl.when(is_ctx)
    def _():
        yc_ref[...] = y

    @pl.when(jnp.logical_not(is_ctx))
    def _():
        yl_ref[...] = y


def _final_combine(pos0, pos1, ys, x1, rc, mods, fn):
    tc = COMBINE_TILE
    row = lambda i, p0, p1: (i, 0)
    n_ctx_tiles = N_CTX // tc
    return pl.pallas_call(
        _final_kernel,
        grid_spec=pltpu.PrefetchScalarGridSpec(
            num_scalar_prefetch=2,
            grid=(N_TOK // tc,),
            in_specs=[pl.BlockSpec(memory_space=pl.ANY),
                      pl.BlockSpec((tc, D_MODEL), row),
                      pl.BlockSpec((tc, LANE), row),
                      pl.BlockSpec((1, 6, D_MODEL), lambda i, p0, p1: (_mod_row(i, tc), 0, 0)),
                      pl.BlockSpec((1, D_MODEL), lambda i, p0, p1: (0, 0))],
            out_specs=[pl.BlockSpec((tc, D_MODEL), lambda i, p0, p1: (jnp.minimum(i, n_ctx_tiles - 1), 0)),
                       pl.BlockSpec((tc, D_MODEL), lambda i, p0, p1: (jnp.maximum(i - n_ctx_tiles, 0), 0))],
            scratch_shapes=[pltpu.VMEM((2, 2, tc, D_MODEL), f32), pltpu.SemaphoreType.DMA((2,))]),
        out_shape=[jax.ShapeDtypeStruct((N_CTX, D_MODEL), f32), jax.ShapeDtypeStruct((N_LAT, D_MODEL), f32)],
        compiler_params=pltpu.CompilerParams(dimension_semantics=("arbitrary",), vmem_limit_bytes=VMEM_LIMIT),
        name="moe_combine_final",
    )(pos0, pos1, ys, x1, rc, mods, fn)


def _moe_experts(h2, route_t, w_gate, w_up, w_down, layer, xs_buf):
    rows, te, runs = _rank(route_t)
    xs = _dispatch(runs[:, :3, :N_EXPERTS].reshape(-1), h2, rows, xs_buf)
    return rows[0], rows[1], _experts(te, xs, w_gate, w_up, w_down, layer), xs


def _pad_heads(w):
    lead = w.shape[:-1]
    w = w.reshape(*lead, ML_HEADS, ML_DIM)
    w = jnp.pad(w, [(0, 0)] * len(lead) + [(0, 0), (0, ML_PAD - ML_DIM)])
    return w.reshape(*lead, ML_PW)


def _pack_in_cols(wb):
    o = 0
    qa = wb[..., o:o + NA_WIDTH] * (NA_DIM ** -0.5)
    ka = wb[..., o + NA_WIDTH:o + 2 * NA_WIDTH]
    va = wb[..., o + 2 * NA_WIDTH:o + 3 * NA_WIDTH]
    o += 3 * NA_WIDTH
    qb, kb, vb, ob = [_pad_heads(wb[..., o + j * ML_WIDTH:o + (j + 1) * ML_WIDTH]) for j in range(4)]
    o += 4 * ML_WIDTH
    gates = wb[..., o:o + N_GATE_COLS]
    o += N_GATE_COLS
    pool = wb[..., o:o + POOL_WIDTH]
    main = jnp.concatenate([qa, ka, va, qb, vb, ob, pool], axis=-1)
    gates_p = jnp.pad(gates, [(0, 0)] * (gates.ndim - 1) + [(0, LANE - N_GATE_COLS)])
    return main, gates_p, jnp.concatenate([kb, gates], axis=-1)


def _pack_w_in(w, b):
    w_main, w_gates, w_feat = _pack_in_cols(w)
    b_main, b_gates, b_feat = _pack_in_cols(b.astype(f32))
    return (w_main.astype(bf16), b_main[:, None], w_gates.astype(bf16), b_gates[:, None],
            jnp.swapaxes(w_feat, 1, 2).astype(bf16), b_feat[:, :, None])


def _pack_w_out(w):
    n_l = w.shape[0]
    wb = w[:, NA_WIDTH:NA_WIDTH + ML_WIDTH].reshape(n_l, ML_HEADS, ML_DIM, D_MODEL)
    wb = jnp.pad(wb, ((0, 0), (0, 0), (0, ML_PAD - ML_DIM), (0, 0))).reshape(n_l, ML_PW, D_MODEL)
    return jnp.concatenate([w[:, :NA_WIDTH], wb, w[:, NA_WIDTH + ML_WIDTH:]], axis=1).astype(bf16)


def _block_diag(w):
    n_l, g, c, _ = w.shape
    eye = jnp.eye(g, dtype=w.dtype)
    return (eye[None, :, None, :, None] * w[:, :, :, None, :]).reshape(n_l, g * c, g * c)


def kernel(x_prompt, x_sample, cache_k_attn, cache_v_attn, state_mlstm_C, state_mlstm_n, state_mlstm_m, c, c_ctx,
           w_ada, b_ada, norm1, w_in, b_in, rpb, ml_norm, w_pool, pool_scale, w_out, norm2, w_router, b_router,
           w_gate, w_up, w_down, final_norm):
    dt = x_prompt.dtype
    x = jnp.concatenate([x_prompt.reshape(N_CTX, D_MODEL), x_sample.reshape(N_LAT, D_MODEL)], axis=0).astype(f32)
    cvec = jnp.concatenate([c_ctx[None], c, jnp.zeros((8 - 1 - DEC_BATCH, D_MODEL), c.dtype)], axis=0).astype(f32)
    mods_all = _ada(cvec, w_ada.astype(f32), b_ada.astype(f32))
    mods_all = mods_all[:, :1 + DEC_BATCH].reshape(DEPTH, 1 + DEC_BATCH, 6, D_MODEL)

    wr_t = w_router.astype(f32).T
    br_t = b_router.astype(f32)[:, None]
    fn = final_norm.astype(f32)[None]

    na_bias = _na_bias_tables(rpb)
    xs_buf = jnp.zeros((MOE_ROWS, D_MODEL), f32)
    in_params = (norm1.astype(f32)[:, None],) + _pack_w_in(w_in, b_in)
    out_params = (_block_diag(w_pool.astype(f32)).astype(bf16), pool_scale.astype(f32)[:, None],
                  _pad_heads(ml_norm.astype(f32))[:, None], _pack_w_out(w_out), norm2.astype(f32)[:, None])

    ks, vs, Cs, ns, ms = [], [], [], [], []
    pending = None
    for l in range(DEPTH):
        mods = mods_all[l]
        if pending is None:
            qkva, kv32, qvo, kt, gates, gates_t, pin = _in_proj(x, mods, l, *in_params)
        else:
            x, qkva, kv32, qvo, kt, gates, gates_t, pin = _moe_in_proj(*pending, mods_all[l - 1], mods, l, *in_params)
        ks.append(kv32[:N_CTX, :NA_WIDTH].reshape(BATCH, SEQ, NA_HEADS, NA_DIM))
        vs.append(kv32[:N_CTX, NA_WIDTH:].reshape(BATCH, SEQ, NA_HEADS, NA_DIM))

        oa_ctx = _ctx_attention(qkva.reshape(N_TOK // SEQ, SEQ, W_A))
        ck = (cache_k_attn[:, l].reshape(DEC_BATCH, PAST_LEN, NA_WIDTH)).astype(bf16)
        cv = (cache_v_attn[:, l].reshape(DEC_BATCH, PAST_LEN, NA_WIDTH)).astype(bf16)
        oa_lat = _neighborhood_attention(qkva.reshape(N_TOK // DEC_SEQ, DEC_SEQ, W_A), ck, cv, na_bias, l)

        c_l, m_l = _pack_ml_state(state_mlstm_C[:, l], state_mlstm_n[:, l], state_mlstm_m[:, l])
        hf, hb, c_fin, m_fin = _mlstm(qvo, kt, gates, gates_t, c_l, m_l)
        C_l, n_l, m_l2 = _unpack_ml_state(c_fin[:BATCH], m_fin[:BATCH])
        Cs.append(C_l)
        ns.append(n_l)
        ms.append(m_l2)

        x1, h2, route_t, rc = _out_proj(x, mods, l, oa_ctx.reshape(N_CTX, NA_WIDTH), oa_lat.reshape(N_LAT, NA_WIDTH),
                                        hf.reshape(N_TOK, ML_PW), hb.reshape(N_TOK, ML_PW), qvo, pin,
                                        *out_params, wr_t, br_t)
        pos0, pos1, ys, xs_buf = _moe_experts(h2, route_t, w_gate, w_up, w_down, l, xs_buf)
        pending = (pos0, pos1, ys, x1, rc)

    x = _final_combine(*pending, mods_all[DEPTH - 1], fn)
    y_prompt = x[0].reshape(BATCH, SEQ, D_MODEL).astype(dt)
    y_sample = x[1].reshape(DEC_BATCH, DEC_SEQ, D_MODEL).astype(dt)
    return (y_prompt, y_sample,
            jnp.stack(ks, axis=1).astype(dt), jnp.stack(vs, axis=1).astype(dt),
            jnp.stack(Cs, axis=1).astype(dt), jnp.stack(ns, axis=1).astype(dt), jnp.stack(ms, axis=1).astype(dt))
```

```python
import functools

import numpy as np
import jax
import jax.numpy as jnp
from jax import lax
from jax.experimental import pallas as pl
from jax.experimental.pallas import tpu as pltpu

D_MODEL = 1024
BATCH = 16
SEQ = 256
DEPTH = 4
DEC_BATCH = 2
DEC_SEQ = 4096
PAST_LEN = 256
GRID_W = 64
EPS = 1e-6
NEG_INF = -1e30
NA_HEADS = 6
NA_DIM = 64
NA_WIDTH = NA_HEADS * NA_DIM
NA_ROWS = 8
NA_COLS = 16
RPB_ROWS = 2 * NA_ROWS - 1
RPB_COLS = 2 * NA_COLS - 1
ML_HEADS = 4
ML_DIM = 96
ML_WIDTH = ML_HEADS * ML_DIM
POOL_WINDOWS = (2, 4, 8, 16)
POOL_GROUPS = 4
POOL_DIM = 64
POOL_WIDTH = POOL_GROUPS * POOL_DIM
N_GATE_COLS = 4 * ML_HEADS
N_EXPERTS = 16
N_EXPERT_GROUPS = 4
EXPERTS_PER_GROUP = N_EXPERTS // N_EXPERT_GROUPS
D_EXPERT = 512
ADA_DIM = 6 * D_MODEL

N_CTX = BATCH * SEQ
N_LAT = DEC_BATCH * DEC_SEQ
N_TOK = N_CTX + N_LAT
LANE = 128
ML_PAD = LANE
ML_PW = ML_HEADS * ML_PAD
CAUG = ML_PAD
NA_PAIRS = NA_HEADS // 2
TOK_TILE = 512
ML_CHUNK = 256
NA_QROWS = 4
NA_KROWS = NA_QROWS + NA_ROWS - 1
POOL_HALO = max(POOL_WINDOWS) // 2
POOL_BLOCK = 128
MOE_TILE = 512
MOE_CHUNK = 16
MOE_LOCAL_ROWS = -(-(2 * TOK_TILE + N_EXPERTS * (MOE_CHUNK - 1)) // LANE) * LANE
MOE_ROWS = -(-(2 * N_TOK + (N_TOK // TOK_TILE) * N_EXPERTS * (MOE_CHUNK - 1) + N_EXPERTS * (MOE_TILE - 1))
             // MOE_TILE) * MOE_TILE
COMBINE_TILE = 256
VMEM_LIMIT = 56 * 1024 * 1024

W_A = 3 * NA_WIDTH
W_B = 3 * ML_PW
N_TCOLS = ML_PW + N_GATE_COLS
W_MAIN = W_A + W_B + POOL_WIDTH

f32 = jnp.float32
bf16 = jnp.bfloat16
HI = lax.Precision.HIGHEST


def _nt(a, b, **kw):
    return lax.dot_general(a, b, (((1,), (1,)), ((), ())), preferred_element_type=f32, **kw)


def _mod_row(i, tile):
    n_ctx_tiles = N_CTX // tile
    per_batch = DEC_SEQ // tile
    return jnp.where(i < n_ctx_tiles, 0, 1 + (i - n_ctx_tiles) // per_batch)


def _ada_kernel(c_ref, w_ref, b_ref, o_ref):
    s = c_ref[...]
    s = s * jax.nn.sigmoid(s)
    o_ref[0] = jnp.dot(s.astype(bf16), w_ref[0].astype(bf16), preferred_element_type=f32) + b_ref[0]


def _ada(cvec, w_ada, b_ada):
    nj = ADA_DIM // D_MODEL
    return pl.pallas_call(
        _ada_kernel,
        grid=(DEPTH, nj),
        in_specs=[pl.BlockSpec((8, D_MODEL), lambda l, j: (0, 0)),
                  pl.BlockSpec((1, D_MODEL, D_MODEL), lambda l, j: (l, 0, j)),
                  pl.BlockSpec((1, 1, D_MODEL), lambda l, j: (l, 0, j))],
        out_specs=pl.BlockSpec((1, 8, D_MODEL), lambda l, j: (l, 0, j)),
        out_shape=jax.ShapeDtypeStruct((DEPTH, 8, ADA_DIM), f32),
        name="ada_mod",
    )(cvec, w_ada, b_ada.reshape(DEPTH, 1, ADA_DIM))


def _in_kernel(x_ref, *refs):
    _in_body(x_ref[...], *refs)


def _moe_in_kernel(pos0_ref, pos1_ref, ys_ref, x1_ref, rc_ref, mod_prev_ref, *refs):
    in_refs, x_out_ref, out_refs, (buf, sem) = refs[:8], refs[8], refs[9:-2], refs[-2:]
    x = _moe_residual(pos0_ref, pos1_ref, ys_ref, x1_ref, rc_ref, mod_prev_ref, buf, sem)
    x_out_ref[...] = x
    _in_body(x, *in_refs, *out_refs)


def _in_body(x, mod_ref, n1_ref, w_ref, b_ref, wg_ref, bg_ref, wt_ref, bt_ref,
             a_ref, kv_ref, b_out_ref, kt_ref, g_ref, gt_ref, pin_ref):
    mod = mod_ref[0]
    h = x * lax.rsqrt(jnp.mean(x * x, axis=-1, keepdims=True) + EPS) * n1_ref[...]
    h = (h * (1.0 + mod[1:2]) + mod[0:1]).astype(bf16)
    pa = jnp.dot(h, w_ref[:, 0:W_A], preferred_element_type=f32) + b_ref[:, 0:W_A]
    a_ref[...] = pa.astype(bf16)
    kv_ref[...] = pa[:, NA_WIDTH:W_A]
    for j in range(3):
        lo = W_A + j * ML_PW
        pb = jnp.dot(h, w_ref[:, lo:lo + ML_PW], preferred_element_type=f32) + b_ref[:, lo:lo + ML_PW]
        if j == 0:
            pb = pb * (ML_DIM ** -0.5)
        b_out_ref[:, j * ML_PW:(j + 1) * ML_PW] = pb.astype(bf16)
    lo = W_A + W_B
    pin_ref[...] = jnp.dot(h, w_ref[:, lo:lo + POOL_WIDTH], preferred_element_type=f32) + b_ref[:, lo:lo + POOL_WIDTH]
    g_ref[...] = jnp.dot(h, wg_ref[...], preferred_element_type=f32) + bg_ref[...]
    t = _nt(wt_ref[...], h) + bt_ref[...]
    kt_ref[...] = t[0:ML_PW].astype(bf16)
    gt_ref[...] = t[ML_PW:N_TCOLS]


def _in_proj_specs(layer):
    tm = TOK_TILE
    lyr = lambda shape: pl.BlockSpec((None,) + shape, lambda i, *_: (layer, 0, 0))
    rows = lambda width: pl.BlockSpec((tm, width), lambda i, *_: (i, 0))
    cols = lambda height: pl.BlockSpec((height, tm), lambda i, *_: (0, i))
    param_specs = [pl.BlockSpec((1, 6, D_MODEL), lambda i, *_: (_mod_row(i, tm), 0, 0)),
                   lyr((1, D_MODEL)), lyr((D_MODEL, W_MAIN)), lyr((1, W_MAIN)), lyr((D_MODEL, LANE)), lyr((1, LANE)),
                   lyr((N_TCOLS, D_MODEL)), lyr((N_TCOLS, 1))]
    out_specs = [rows(W_A), rows(2 * NA_WIDTH), rows(W_B), cols(ML_PW), rows(LANE), cols(N_GATE_COLS),
                 rows(POOL_WIDTH)]
    out_shape = [jax.ShapeDtypeStruct((N_TOK, W_A), bf16),
                 jax.ShapeDtypeStruct((N_TOK, 2 * NA_WIDTH), f32),
                 jax.ShapeDtypeStruct((N_TOK, W_B), bf16),
                 jax.ShapeDtypeStruct((ML_PW, N_TOK), bf16),
                 jax.ShapeDtypeStruct((N_TOK, LANE), f32),
                 jax.ShapeDtypeStruct((N_GATE_COLS, N_TOK), f32),
                 jax.ShapeDtypeStruct((N_TOK, POOL_WIDTH), f32)]
    return rows, param_specs, out_specs, out_shape


def _in_proj(x, mods, layer, *params):
    rows, param_specs, out_specs, out_shape = _in_proj_specs(layer)
    return pl.pallas_call(
        _in_kernel,
        grid=(N_TOK // TOK_TILE,),
        in_specs=[rows(D_MODEL)] + param_specs,
        out_specs=out_specs,
        out_shape=out_shape,
        compiler_params=pltpu.CompilerParams(dimension_semantics=("arbitrary",), vmem_limit_bytes=VMEM_LIMIT),
        name="in_proj",
    )(x, mods, *params)


def _moe_in_proj(pos0, pos1, ys, x1, rc, mods_prev, mods, layer, *params):
    tm = TOK_TILE
    rows, param_specs, out_specs, out_shape = _in_proj_specs(layer)
    return pl.pallas_call(
        _moe_in_kernel,
        grid_spec=pltpu.PrefetchScalarGridSpec(
            num_scalar_prefetch=2,
            grid=(N_TOK // tm,),
            in_specs=[pl.BlockSpec(memory_space=pl.ANY), rows(D_MODEL), rows(LANE),
                      pl.BlockSpec((1, 6, D_MODEL), lambda i, *_: (_mod_row(i, tm), 0, 0))] + param_specs,
            out_specs=[rows(D_MODEL)] + out_specs,
            scratch_shapes=[pltpu.VMEM((2, 2, tm, D_MODEL), f32), pltpu.SemaphoreType.DMA((2,))]),
        out_shape=[jax.ShapeDtypeStruct((N_TOK, D_MODEL), f32)] + out_shape,
        compiler_params=pltpu.CompilerParams(dimension_semantics=("arbitrary",), vmem_limit_bytes=VMEM_LIMIT),
        name="moe_combine_in_proj",
    )(pos0, pos1, ys, x1, rc, mods_prev, mods, *params)


def _pair_attention(qp, parts):
    lane = lax.broadcasted_iota(jnp.int32, (1, LANE), 1)
    outs = []
    for j in range(2):
        in_half = (lane >= j * NA_DIM) & (lane < (j + 1) * NA_DIM)
        qm = jnp.where(in_half, qp, jnp.zeros_like(qp))
        scores = []
        for k, _, bias in parts:
            s = _nt(qm, k)
            if bias is not None:
                s = s + bias[j]
            scores.append(s)
        m = scores[0].max(axis=-1, keepdims=True)
        for s in scores[1:]:
            m = jnp.maximum(m, s.max(axis=-1, keepdims=True))
        den = None
        acc = None
        for s, (_, v, _) in zip(scores, parts):
            p = jnp.exp(s - m)
            ps = p.sum(axis=-1, keepdims=True)
            den = ps if den is None else den + ps
            o = jnp.dot(p.astype(bf16), v, preferred_element_type=f32)
            acc = o if acc is None else acc + o
        outs.append(acc / den)
    return jnp.where(lane < NA_DIM, outs[0], outs[1])


def _ctx_attn_kernel(q_ref, k_ref, v_ref, o_ref):
    for p in range(NA_PAIRS):
        sl = slice(p * LANE, (p + 1) * LANE)
        o = _pair_attention(q_ref[0, :, sl], [(k_ref[0, :, sl], v_ref[0, :, sl], None)])
        o_ref[0, :, sl] = o.astype(bf16)


def _ctx_attention(qkv):
    blk = lambda c: pl.BlockSpec((1, SEQ, NA_WIDTH), lambda b, c=c: (b, 0, c))
    return pl.pallas_call(
        _ctx_attn_kernel,
        grid=(BATCH,),
        in_specs=[blk(0), blk(1), blk(2)],
        out_specs=pl.BlockSpec((1, SEQ, NA_WIDTH), lambda b: (b, 0, 0)),
        out_shape=jax.ShapeDtypeStruct((BATCH, SEQ, NA_WIDTH), bf16),
        name="ctx_attention",
    )(qkv, qkv, qkv)


def _na_window_start(rb):
    return jnp.clip(rb * NA_QROWS - NA_ROWS // 2, 0, DEC_SEQ // GRID_W - NA_KROWS)


def _na_bias(tab_ref, head, rb):
    rows = DEC_SEQ // GRID_W
    ws = _na_window_start(rb)
    lane = lax.broadcasted_iota(jnp.int32, (1, NA_KROWS * GRID_W), 1)
    per_qrow = []
    for dq in range(NA_QROWS):
        qr = rb * NA_QROWS + dq
        a0 = ws - qr + (NA_ROWS - 1) + NA_KROWS
        tiles = [tab_ref[head, a0 + 2 * j] for j in range((NA_KROWS + 1) // 2)]
        t = jnp.concatenate(tiles, axis=1)[:, :NA_KROWS * GRID_W]
        lo = (jnp.clip(qr - NA_ROWS // 2, 0, rows - NA_ROWS) - ws) * GRID_W
        ok = (lane >= lo) & (lane < lo + NA_ROWS * GRID_W)
        per_qrow.append(jnp.where(ok, t, NEG_INF))
    return jnp.concatenate(per_qrow, axis=0)


def _na_kernel(q_ref, k_ref, v_ref, ck_ref, cv_ref, tab_ref, o_ref):
    rb = pl.program_id(1)
    start = pl.multiple_of(_na_window_start(rb) * GRID_W, GRID_W)
    nk = NA_KROWS * GRID_W
    for p in range(NA_PAIRS):
        sl = slice(p * LANE, (p + 1) * LANE)
        bias = [_na_bias(tab_ref.at[0], 2 * p + j, rb) for j in range(2)]
        parts = [(k_ref[0, pl.ds(start, nk), sl], v_ref[0, pl.ds(start, nk), sl], bias),
                 (ck_ref[0, :, sl], cv_ref[0, :, sl], None)]
        o = _pair_attention(q_ref[0, :, sl], parts)
        o_ref[0, :, sl] = o.astype(bf16)


def _neighborhood_attention(qkv, ck, cv, tables, layer):
    nq = NA_QROWS * GRID_W
    n_rb = DEC_SEQ // nq
    return pl.pallas_call(
        _na_kernel,
        grid=(DEC_BATCH, n_rb),
        in_specs=[pl.BlockSpec((1, nq, NA_WIDTH), lambda b, r: (1 + b, r, 0)),
                  pl.BlockSpec((1, DEC_SEQ, NA_WIDTH), lambda b, r: (1 + b, 0, 1)),
                  pl.BlockSpec((1, DEC_SEQ, NA_WIDTH), lambda b, r: (1 + b, 0, 2)),
                  pl.BlockSpec((1, PAST_LEN, NA_WIDTH), lambda b, r: (b, 0, 0)),
                  pl.BlockSpec((1, PAST_LEN, NA_WIDTH), lambda b, r: (b, 0, 0)),
                  pl.BlockSpec((1,) + tables.shape[1:], lambda b, r: (layer, 0, 0, 0, 0))],
        out_specs=pl.BlockSpec((1, nq, NA_WIDTH), lambda b, r: (b, r, 0)),
        out_shape=jax.ShapeDtypeStruct((DEC_BATCH, DEC_SEQ, NA_WIDTH), bf16),
        compiler_params=pltpu.CompilerParams(dimension_semantics=("arbitrary", "arbitrary"),
                                             vmem_limit_bytes=VMEM_LIMIT),
        name="neighborhood_attention",
    )(qkv, qkv, qkv, ck, cv, tables)


def _na_bias_tables(rpb):
    qc = np.arange(GRID_W)[:, None]
    kc = np.arange(GRID_W)[None, :]
    dc = np.clip(kc - qc + NA_COLS - 1, 0, RPB_COLS - 1)
    col_start = np.clip(qc - NA_COLS // 2, 0, GRID_W - NA_COLS)
    col_ok = (kc >= col_start) & (kc < col_start + NA_COLS)
    pick_col = (dc[None] == np.arange(RPB_COLS)[:, None, None]).astype(np.float32)
    by_col = jnp.einsum('lhab,bqk->lhaqk', rpb.astype(f32), pick_col, precision=HI)
    by_col = jnp.where(col_ok[None, None, None], by_col, NEG_INF)
    by_col = jnp.pad(by_col, ((0, 0), (0, 0), (NA_KROWS, NA_KROWS + 1), (0, 0), (0, 0)), constant_values=NEG_INF)
    return jnp.concatenate([by_col[:, :, :-1], by_col[:, :, 1:]], axis=-1)


def _log_sigmoid(x):
    return -(jnp.maximum(-x, 0.0) + jnp.log(1.0 + jnp.exp(-jnp.abs(x))))


def _split3(x):
    hi = x.astype(bf16)
    r1 = x - hi.astype(f32)
    mid = r1.astype(bf16)
    lo = (r1 - mid.astype(f32)).astype(bf16)
    return hi, mid, lo


def _mlstm_kernel(qf_ref, vf_ref, ktf_ref, gf_ref, gtf_ref, qb_ref, vb_ref, ktb_ref, gb_ref, gtb_ref,
                  c0_ref, m0_ref, hf_ref, hb_ref, c_out_ref, m_out_ref, c_scr, m_scr):
    L = ML_CHUNK
    seq, c, n_chunks, _ = _ml_schedule(pl.program_id(0))

    @pl.when(c == 0)
    def _():
        is_ctx = seq < BATCH
        c_scr[...] = jnp.where(is_ctx, 0.0, c0_ref[0])
        m_scr[...] = jnp.where(is_ctx, 0.0, m0_ref[0])

    ri = lax.broadcasted_iota(jnp.int32, (L, L), 0)
    ci = lax.broadcasted_iota(jnp.int32, (L, L), 1)
    lane = lax.broadcasted_iota(jnp.int32, (L, ML_PAD), 1)
    is_ncol = lane == ML_DIM
    lower = ri >= ci
    upper = ri <= ci
    lower_b = jnp.where(lower, 1.0, 0.0).astype(bf16)
    upper_b = jnp.where(upper, 1.0, 0.0).astype(bf16)
    dirs = ((qf_ref, ktf_ref, vf_ref, gf_ref, gtf_ref, hf_ref), (qb_ref, ktb_ref, vb_ref, gb_ref, gtb_ref, hb_ref))
    for d, (q_ref, kt_ref, v_ref, g_ref, gt_ref, h_ref) in enumerate(dirs):
        g = g_ref[...][:, 0:N_GATE_COLS]
        gt = gt_ref[...]
        lf_c = _log_sigmoid(g)
        lf_r = _log_sigmoid(gt)
        b_cols = sum(jnp.dot(lower_b, part, preferred_element_type=f32) for part in _split3(lf_c))
        b_rows = sum(jnp.dot(part, upper_b, preferred_element_type=f32) for part in _split3(lf_r))
        tot_c = jnp.sum(lf_c, axis=0, keepdims=True)
        tot_r = jnp.sum(lf_r, axis=1, keepdims=True)
        visible = lower
        if d == 1:
            b_cols = tot_c - b_cols + lf_c
            b_rows = tot_r - b_rows + lf_r
            visible = upper
        for hd in range(ML_HEADS):
            st = d * ML_HEADS + hd
            ci_ = 2 * ML_HEADS * d + hd
            cf_ = ci_ + ML_HEADS
            sl = slice(hd * ML_PAD, (hd + 1) * ML_PAD)
            bc = b_cols[:, cf_:cf_ + 1]
            br = b_rows[cf_:cf_ + 1, :]
            li_r = gt[ci_:ci_ + 1, :]
            m_prev = m_scr[st:st + 1, 0:1]
            dmat = jnp.where(visible, bc - br + li_r, NEG_INF)
            inter = bc + m_prev
            m_t = jnp.maximum(inter, dmat.max(axis=-1, keepdims=True))
            w_intra = jnp.exp(dmat - m_t)
            w_inter = jnp.exp(inter - m_t)
            qh = q_ref[0, :, sl]
            kht = kt_ref[sl, :]
            v_aug = jnp.where(is_ncol, jnp.ones((), bf16), v_ref[0, :, sl])
            s = (jnp.dot(qh, kht, preferred_element_type=f32) * w_intra).astype(bf16)
            c_aug = c_scr[st]
            na = (w_inter * jnp.dot(qh, c_aug.astype(bf16), preferred_element_type=f32)
                  + jnp.dot(s, v_aug, preferred_element_type=f32))
            den = na[:, ML_DIM:ML_DIM + 1]
            h_ref[0, :, sl] = jnp.where(lane < ML_DIM, na / jnp.maximum(jnp.abs(den), jnp.exp(-m_t)), 0.0)
            b_end = tot_r[cf_:cf_ + 1, :]
            g_row = b_end - br + li_r
            m_new = jnp.maximum(b_end + m_prev, g_row.max(axis=1, keepdims=True))
            decay = jnp.exp(b_end + m_prev - m_new)
            kwt = (kht.astype(f32) * jnp.exp(g_row - m_new)).astype(bf16)
            c_scr[st] = decay * c_aug + jnp.dot(kwt, v_aug, preferred_element_type=f32)
            m_scr[st:st + 1, :] = jnp.broadcast_to(m_new, (1, LANE))

    @pl.when(c == n_chunks - 1)
    def _():
        c_out_ref[0] = c_scr[...]
        m_out_ref[0] = m_scr[...]


def _ml_schedule(s):
    nc_ctx, nc_lat = SEQ // ML_CHUNK, DEC_SEQ // ML_CHUNK
    n_ctx_steps = BATCH * nc_ctx
    is_ctx = s < n_ctx_steps
    t = s - n_ctx_steps
    seq = jnp.where(is_ctx, s // nc_ctx, BATCH + t // nc_lat)
    c = jnp.where(is_ctx, s % nc_ctx, t % nc_lat)
    nc = jnp.where(is_ctx, nc_ctx, nc_lat)
    base = jnp.where(is_ctx, (s // nc_ctx) * nc_ctx, n_ctx_steps + (t // nc_lat) * nc_lat)
    return seq, c, nc, base


def _mlstm(qvo, kt, gates, gates_t, c0, m0):
    L = ML_CHUNK
    n_seq = BATCH + DEC_BATCH

    def fwd(s):
        _, c, _, base = _ml_schedule(s)
        return base + c

    def bwd(s):
        _, c, nc, base = _ml_schedule(s)
        return base + nc - 1 - c

    seq_of = lambda s: _ml_schedule(s)[0]
    lat_of = lambda s: jnp.maximum(seq_of(s) - BATCH, 0)

    def specs(pos):
        return [pl.BlockSpec((1, L, ML_PW), lambda s, j=j: (pos(s), 0, j)) for j in range(2)] + [
            pl.BlockSpec((ML_PW, L), lambda s: (0, pos(s))),
            pl.BlockSpec((L, LANE), lambda s: (pos(s), 0)),
            pl.BlockSpec((N_GATE_COLS, L), lambda s: (0, pos(s)))]

    q3 = qvo.reshape(N_TOK // L, L, W_B)
    n_str = 2 * ML_HEADS
    return pl.pallas_call(
        _mlstm_kernel,
        grid=(N_TOK // L,),
        in_specs=specs(fwd) + specs(bwd) + [
            pl.BlockSpec((1, n_str, ML_PAD, CAUG), lambda s: (lat_of(s), 0, 0, 0)),
            pl.BlockSpec((1, n_str, LANE), lambda s: (lat_of(s), 0, 0))],
        out_specs=[pl.BlockSpec((1, L, ML_PW), lambda s: (fwd(s), 0, 0)),
                   pl.BlockSpec((1, L, ML_PW), lambda s: (bwd(s), 0, 0)),
                   pl.BlockSpec((1, n_str, ML_PAD, CAUG), lambda s: (seq_of(s), 0, 0, 0)),
                   pl.BlockSpec((1, n_str, LANE), lambda s: (seq_of(s), 0, 0))],
        out_shape=[jax.ShapeDtypeStruct((N_TOK // L, L, ML_PW), f32),
                   jax.ShapeDtypeStruct((N_TOK // L, L, ML_PW), f32),
                   jax.ShapeDtypeStruct((n_seq, n_str, ML_PAD, CAUG), f32),
                   jax.ShapeDtypeStruct((n_seq, n_str, LANE), f32)],
        scratch_shapes=[pltpu.VMEM((n_str, ML_PAD, CAUG), f32), pltpu.VMEM((n_str, LANE), f32)],
        compiler_params=pltpu.CompilerParams(dimension_semantics=("arbitrary",), vmem_limit_bytes=VMEM_LIMIT),
        name="mlstm",
    )(q3, q3, kt, gates, gates_t, q3, q3, kt, gates, gates_t, c0, m0)


def _pack_ml_state(C, n, m):
    B = C.shape[0]
    c_aug = jnp.zeros((B, 2, ML_HEADS, ML_PAD, CAUG), f32)
    c_aug = c_aug.at[:, :, :, :ML_DIM, :ML_DIM].set(C.astype(f32))
    c_aug = c_aug.at[:, :, :, :ML_DIM, ML_DIM].set(n.astype(f32))
    m_b = jnp.broadcast_to(m.astype(f32)[..., None], (B, 2, ML_HEADS, LANE))
    return c_aug.reshape(B, 2 * ML_HEADS, ML_PAD, CAUG), m_b.reshape(B, 2 * ML_HEADS, LANE)


def _unpack_ml_state(c_aug, m_b):
    B = c_aug.shape[0]
    c_aug = c_aug.reshape(B, 2, ML_HEADS, ML_PAD, CAUG)
    return (c_aug[:, :, :, :ML_DIM, :ML_DIM], c_aug[:, :, :, :ML_DIM, ML_DIM],
            m_b.reshape(B, 2, ML_HEADS, LANE)[..., 0])


def _pool_rows(u_prev, u_cur, u_next, w_bd, scale, t0, seq_len):
    tm = u_cur.shape[0]
    u_win = jnp.concatenate([u_prev, u_cur, u_next], axis=0)
    u_hi = u_win.astype(bf16)
    u_lo = (u_win - u_hi.astype(f32)).astype(bf16)
    lane = lax.broadcasted_iota(jnp.int32, (1, LANE), 1)
    blocks = []
    for r0 in range(0, tm, POOL_BLOCK):
        win = slice(r0, r0 + POOL_BLOCK + 2 * POOL_HALO)
        t_abs = t0 + r0 + lax.broadcasted_iota(jnp.int32, (POOL_BLOCK, 1), 0)
        s_abs = t0 + r0 - POOL_HALO + lax.broadcasted_iota(jnp.int32, (1, POOL_BLOCK + 2 * POOL_HALO), 1)
        t_loc = t_abs & (seq_len - 1)
        seq_start = t_abs - t_loc
        means = []
        for w in POOL_WINDOWS:
            lo = jnp.maximum(t_loc - w // 2, 0)
            hi = jnp.minimum(t_loc - w // 2 + w, seq_len)
            in_win = (s_abs >= seq_start + lo) & (s_abs < seq_start + hi)
            means.append((jnp.where(in_win, 1.0, 0.0).astype(bf16), 1.0 / (hi - lo).astype(f32)))
        pooled = []
        for p in range(POOL_GROUPS // 2):
            sl = slice(p * LANE, (p + 1) * LANE)
            halves = []
            for a, inv_cnt in means[2 * p:2 * p + 2]:
                tot = (jnp.dot(a, u_hi[win, sl], preferred_element_type=f32)
                       + jnp.dot(a, u_lo[win, sl], preferred_element_type=f32))
                halves.append(tot * inv_cnt)
            pooled.append(jnp.where(lane < POOL_DIM, halves[0], halves[1]) - u_cur[r0:r0 + POOL_BLOCK, sl])
        blocks.append(jnp.concatenate(pooled, axis=1))
    pooled = jnp.concatenate(blocks, axis=0).astype(bf16)
    return jnp.dot(pooled, w_bd, preferred_element_type=f32) * scale


def _top2_sum(a, b, c, d):
    hi1, lo1 = jnp.maximum(a, b), jnp.minimum(a, b)
    hi2, lo2 = jnp.maximum(c, d), jnp.minimum(c, d)
    return jnp.maximum(hi1, hi2) + jnp.maximum(jnp.minimum(hi1, hi2), jnp.maximum(lo1, lo2))


def _first_match(vals, target):
    idx = jnp.full_like(target, float(len(vals) - 1))
    for i in range(len(vals) - 2, -1, -1):
        idx = jnp.where(vals[i] == target, float(i), idx)
    return idx


def _pick(vals, idx):
    out = vals[-1]
    for i in range(len(vals) - 2, -1, -1):
        out = jnp.where(idx == float(i), vals[i], out)
    return out


def _route(logits_t, bias_t):
    scores = jax.nn.sigmoid(logits_t)
    sel = scores + bias_t
    row = lambda a, i: a[i:i + 1, :]
    grp = [_top2_sum(*[row(sel, EXPERTS_PER_GROUP * g + i) for i in range(EXPERTS_PER_GROUP)])
           for g in range(N_EXPERT_GROUPS)]
    best = functools.reduce(jnp.maximum, grp)
    gidx = _first_match(grp, best)
    sel_g = [_pick([row(sel, EXPERTS_PER_GROUP * g + i) for g in range(N_EXPERT_GROUPS)], gidx)
             for i in range(EXPERTS_PER_GROUP)]
    sco_g = [_pick([row(scores, EXPERTS_PER_GROUP * g + i) for g in range(N_EXPERT_GROUPS)], gidx)
             for i in range(EXPERTS_PER_GROUP)]
    i0 = _first_match(sel_g, functools.reduce(jnp.maximum, sel_g))
    rest = [jnp.where(i0 == float(i), -jnp.inf, sel_g[i]) for i in range(EXPERTS_PER_GROUP)]
    i1 = _first_match(rest, functools.reduce(jnp.maximum, rest))
    s0, s1 = _pick(sco_g, i0), _pick(sco_g, i1)
    tot = s0 + s1
    rid = lax.broadcasted_iota(jnp.int32, (LANE, logits_t.shape[1]), 0)
    rows = (EXPERTS_PER_GROUP * gidx + i0, EXPERTS_PER_GROUP * gidx + i1, s0 / tot, s1 / tot)
    out = jnp.zeros(rid.shape, f32)
    for i, r in enumerate(rows):
        out = jnp.where(rid == i, r, out)
    return out


def _out_kernel(x_ref, mod_ref, oac_ref, oal_ref, hf_ref, hb_ref, ob_ref, up_ref, uc_ref, un_ref, wp_ref, psc_ref,
                mln_ref, wo_ref, n2_ref, wr_ref, br_ref, x1_ref, h2_ref, rt_ref, rc_ref):
    tm = x_ref.shape[0]
    i = pl.program_id(0)
    is_ctx = i < N_CTX // tm
    mod = mod_ref[0]
    out_a = jnp.where(is_ctx, oac_ref[...], oal_ref[...])
    out_c = _pool_rows(up_ref[...], uc_ref[...], un_ref[...], wp_ref[...], psc_ref[...], i * tm,
                       jnp.where(is_ctx, SEQ, DEC_SEQ)).astype(bf16)
    hsum = hf_ref[...] + hb_ref[...]
    outs_b = []
    for hd in range(ML_HEADS):
        sl = slice(hd * ML_PAD, (hd + 1) * ML_PAD)
        hh = hsum[:, sl]
        ms = jnp.sum(hh * hh, axis=-1, keepdims=True) * (1.0 / ML_DIM)
        hn = hh * lax.rsqrt(ms + EPS) * mln_ref[:, sl]
        outs_b.append((jax.nn.sigmoid(ob_ref[:, sl].astype(f32)) * hn).astype(bf16))
    out_b = jnp.concatenate(outs_b, axis=1)
    mixed = (jnp.dot(out_a, wo_ref[0:NA_WIDTH, :], preferred_element_type=f32)
             + jnp.dot(out_b, wo_ref[NA_WIDTH:NA_WIDTH + ML_PW, :], preferred_element_type=f32)
             + jnp.dot(out_c, wo_ref[NA_WIDTH + ML_PW:, :], preferred_element_type=f32))
    x1 = x_ref[...] + mod[2:3] * mixed
    x1_ref[...] = x1
    h2 = x1 * lax.rsqrt(jnp.mean(x1 * x1, axis=-1, keepdims=True) + EPS) * n2_ref[...]
    h2 = h2 * (1.0 + mod[4:5]) + mod[3:4]
    h2_ref[...] = h2.astype(bf16)
    route_t = _route(_nt(wr_ref[...], h2, precision=HI), br_ref[...])
    rt_ref[...] = route_t[0:8]
    rc_ref[...] = route_t.T


def _out_proj(x, mods, layer, oa_ctx, oa_lat, hf, hb, qvo, pin, w_bd, psc, mln, wo, n2, wr_t, br_t):
    tm = TOK_TILE
    const = lambda i: (0, 0)
    lyr = lambda shape: pl.BlockSpec((None,) + shape, lambda i: (layer, 0, 0))
    row = lambda i: (i, 0)
    n_ctx_tiles = N_CTX // tm
    halo_blocks = tm // POOL_HALO
    return pl.pallas_call(
        _out_kernel,
        grid=(N_TOK // tm,),
        in_specs=[pl.BlockSpec((tm, D_MODEL), row),
                  pl.BlockSpec((1, 6, D_MODEL), lambda i: (_mod_row(i, tm), 0, 0)),
                  pl.BlockSpec((tm, NA_WIDTH), lambda i: (jnp.minimum(i, n_ctx_tiles - 1), 0)),
                  pl.BlockSpec((tm, NA_WIDTH), lambda i: (jnp.maximum(i - n_ctx_tiles, 0), 0)),
                  pl.BlockSpec((tm, ML_PW), row),
                  pl.BlockSpec((tm, ML_PW), row),
                  pl.BlockSpec((tm, ML_PW), lambda i: (i, 2)),
                  pl.BlockSpec((POOL_HALO, POOL_WIDTH), lambda i: (jnp.maximum(i * halo_blocks - 1, 0), 0)),
                  pl.BlockSpec((tm, POOL_WIDTH), row),
                  pl.BlockSpec((POOL_HALO, POOL_WIDTH),
                               lambda i: (jnp.minimum((i + 1) * halo_blocks, N_TOK // POOL_HALO - 1), 0)),
                  lyr((POOL_WIDTH, POOL_WIDTH)), lyr((1, POOL_WIDTH)), lyr((1, ML_PW)),
                  lyr((NA_WIDTH + ML_PW + POOL_WIDTH, D_MODEL)), lyr((1, D_MODEL)),
                  pl.BlockSpec((N_EXPERTS, D_MODEL), const),
                  pl.BlockSpec((N_EXPERTS, 1), const)],
        out_specs=[pl.BlockSpec((tm, D_MODEL), row),
                   pl.BlockSpec((tm, D_MODEL), row),
                   pl.BlockSpec((8, tm), lambda i: (0, i)),
                   pl.BlockSpec((tm, LANE), row)],
        out_shape=[jax.ShapeDtypeStruct((N_TOK, D_MODEL), f32),
                   jax.ShapeDtypeStruct((N_TOK, D_MODEL), bf16),
                   jax.ShapeDtypeStruct((8, N_TOK), f32),
                   jax.ShapeDtypeStruct((N_TOK, LANE), f32)],
        compiler_params=pltpu.CompilerParams(dimension_semantics=("arbitrary",), vmem_limit_bytes=VMEM_LIMIT),
        name="out_proj_router",
    )(x, mods, oa_ctx, oa_lat, hf, hb, qvo, pin, pin, pin, w_bd, psc, mln, wo, n2, wr_t, br_t)


def _ceil_to(x, m):
    return jnp.floor((x + (m - 1)) * (1.0 / m)) * m


def _prefix_over_experts(v):
    er = lax.broadcasted_iota(jnp.int32, (N_EXPERTS, N_EXPERTS), 0)
    ec = lax.broadcasted_iota(jnp.int32, (N_EXPERTS, N_EXPERTS), 1)
    return jnp.dot(jnp.where(ec < er, 1.0, 0.0), v, preferred_element_type=f32, precision=HI)


def _experts_to_lanes(v):
    sub = lax.broadcasted_iota(jnp.int32, (N_EXPERTS, LANE), 0)
    lane = lax.broadcasted_iota(jnp.int32, (N_EXPERTS, LANE), 1)
    return jnp.sum(jnp.where(sub == lane, v, 0.0), axis=0, keepdims=True)


def _rank_kernel(rt_ref, pos_ref, te_ref, tab_ref, carry_ref):
    tm = rt_ref.shape[1]
    p = pl.program_id(0)
    i = pl.program_id(1)
    rid = lax.broadcasted_iota(jnp.int32, (N_EXPERTS, tm), 0).astype(f32)
    oh0 = rid == rt_ref[0:1, :]
    oh1 = rid == rt_ref[1:2, :]
    both = jnp.where(oh0 | oh1, 1.0, 0.0)
    runs = jnp.broadcast_to(_ceil_to(jnp.sum(both, axis=1, keepdims=True), MOE_CHUNK), (N_EXPERTS, LANE))

    @pl.when((p == 0) & (i == 0))
    def _():
        carry_ref[...] = jnp.zeros_like(carry_ref)

    @pl.when((p == 1) & (i == 0))
    def _():
        padded = _ceil_to(carry_ref[...], MOE_TILE)
        off = _prefix_over_experts(padded)
        carry_ref[...] = off
        total = jnp.sum(padded, axis=0, keepdims=True)
        n_used = total * (1.0 / MOE_TILE)
        tile = lax.broadcasted_iota(jnp.int32, (1, LANE), 1).astype(f32)
        row0 = jnp.minimum(tile, n_used - 1.0) * MOE_TILE
        expert = jnp.sum(jnp.where(off <= row0, 1.0, 0.0), axis=0, keepdims=True) - 1.0
        sub = lax.broadcasted_iota(jnp.int32, (8, LANE), 0)
        te_ref[...] = jnp.where(sub == 0, expert, jnp.where(sub == 1, n_used, 0.0)).astype(jnp.int32)

    @pl.when(p == 1)
    def _():
        sr = lax.broadcasted_iota(jnp.int32, (tm, tm), 0)
        sc = lax.broadcasted_iota(jnp.int32, (tm, tm), 1)
        earlier = jnp.dot(both.astype(bf16), jnp.where(sr < sc, 1.0, 0.0).astype(bf16),
                          preferred_element_type=f32)
        g_off = carry_ref[...]
        l_off = _prefix_over_experts(runs)
        g_row = g_off[:, 0:1] + earlier
        l_row = l_off[:, 0:1] + earlier
        pick = lambda oh, v: jnp.sum(jnp.where(oh, v, 0.0), axis=0, keepdims=True)
        rows = (pick(oh0, g_row), pick(oh1, g_row), pick(oh0, l_row), pick(oh1, l_row))
        sub = lax.broadcasted_iota(jnp.int32, (8, tm), 0)
        out = jnp.zeros((8, tm), f32)
        for k, r in enumerate(rows):
            out = jnp.where(sub == k, r, out)
        pos_ref[...] = out.astype(jnp.int32)
        sub = lax.broadcasted_iota(jnp.int32, (8, LANE), 0)
        tab = jnp.zeros((8, LANE), f32)
        for k, v in enumerate((runs * (1.0 / MOE_CHUNK), l_off, g_off)):
            tab = jnp.where(sub == k, _experts_to_lanes(v), tab)
        tab_ref[0] = tab.astype(jnp.int32)

    carry_ref[...] += runs


def _rank(route_t):
    tm = TOK_TILE
    n_tiles = N_TOK // tm
    return pl.pallas_call(
        _rank_kernel,
        grid=(2, n_tiles),
        in_specs=[pl.BlockSpec((8, tm), lambda p, i: (0, i))],
        out_specs=[pl.BlockSpec((8, tm), lambda p, i: (0, i * p)),
                   pl.BlockSpec((8, LANE), lambda p, i: (0, 0)),
                   pl.BlockSpec((1, 8, LANE), lambda p, i: (i * p, 0, 0))],
        out_shape=[jax.ShapeDtypeStruct((8, N_TOK), jnp.int32),
                   jax.ShapeDtypeStruct((8, LANE), jnp.int32),
                   jax.ShapeDtypeStruct((n_tiles, 8, LANE), jnp.int32)],
        scratch_shapes=[pltpu.VMEM((N_EXPERTS, LANE), f32)],
        compiler_params=pltpu.CompilerParams(dimension_semantics=("arbitrary", "arbitrary")),
        name="moe_rank",
    )(route_t)


def _dispatch_kernel(tab_ref, h_ref, rows_ref, xs_in_ref, xs_ref, loc, sem):
    del xs_in_ref
    tm = h_ref.shape[0]
    i = pl.program_id(0)
    rid = lax.broadcasted_iota(jnp.int32, (MOE_LOCAL_ROWS, tm), 0)
    sel = (rid == rows_ref[2:3, :]) | (rid == rows_ref[3:4, :])
    loc[...] = jnp.dot(jnp.where(sel, 1.0, 0.0).astype(bf16), h_ref[...], preferred_element_type=f32).astype(bf16)

    def chunk_copy(src_row, dst_row):
        return pltpu.make_async_copy(loc.at[pl.ds(pl.multiple_of(src_row, MOE_CHUNK), MOE_CHUNK)],
                                     xs_ref.at[pl.ds(pl.multiple_of(dst_row, MOE_CHUNK), MOE_CHUNK)], sem)

    n_total = 0
    for e in range(N_EXPERTS):
        n_chunks, l_off, g_off = (tab_ref[(3 * i + k) * N_EXPERTS + e] for k in range(3))

        def issue(c, carry, l_off=l_off, g_off=g_off):
            chunk_copy(l_off + c * MOE_CHUNK, g_off + c * MOE_CHUNK).start()
            return carry

        lax.fori_loop(0, n_chunks, issue, 0)
        n_total = n_total + n_chunks

    def wait_one(c, carry):
        chunk_copy(0, 0).wait()
        return carry

    lax.fori_loop(0, n_total, wait_one, 0)


def _dispatch(run_table, h2, rows, xs_init):
    tm = TOK_TILE
    return pl.pallas_call(
        _dispatch_kernel,
        grid_spec=pltpu.PrefetchScalarGridSpec(
            num_scalar_prefetch=1,
            grid=(N_TOK // tm,),
            in_specs=[pl.BlockSpec((tm, D_MODEL), lambda i, tab: (i, 0)),
                      pl.BlockSpec((8, tm), lambda i, tab: (0, i)),
                      pl.BlockSpec(memory_space=pl.ANY)],
            out_specs=pl.BlockSpec(memory_space=pl.ANY),
            scratch_shapes=[pltpu.VMEM((MOE_LOCAL_ROWS, D_MODEL), bf16), pltpu.SemaphoreType.DMA(())]),
        out_shape=jax.ShapeDtypeStruct(xs_init.shape, xs_init.dtype),
        input_output_aliases={3: 0},
        compiler_params=pltpu.CompilerParams(dimension_semantics=("arbitrary",), vmem_limit_bytes=VMEM_LIMIT),
        name="moe_dispatch",
    )(run_table, h2, rows, xs_init)


def _expert_kernel(te_ref, xs_ref, wg_ref, wu_ref, wd_ref, ys_ref, wg_bf, wu_bf, wd_bf):
    j = pl.program_id(0)
    used = j < te_ref[1, 0]
    new_expert = jnp.logical_or(j == 0, te_ref[0, j] != te_ref[0, jnp.maximum(j - 1, 0)])

    @pl.when(jnp.logical_not(used))
    def _():
        ys_ref[...] = jnp.zeros_like(ys_ref)

    @pl.when(used & new_expert)
    def _():
        wg_bf[...] = wg_ref[0, 0].astype(bf16)
        wu_bf[...] = wu_ref[0, 0].astype(bf16)
        wd_bf[...] = wd_ref[0, 0].astype(bf16)

    @pl.when(used)
    def _():
        x = xs_ref[...]
        hg = jnp.dot(x, wg_bf[...], preferred_element_type=f32)
        hu = jnp.dot(x, wu_bf[...], preferred_element_type=f32)
        hid = (hg * jax.nn.sigmoid(hg) * hu).astype(bf16)
        ys_ref[...] = jnp.dot(hid, wd_bf[...], preferred_element_type=f32)


def _experts(te, xs, w_gate, w_up, w_down, layer):
    tm = MOE_TILE
    row = lambda j, te: (jnp.minimum(j, te[1, 0] - 1), 0)
    wspec = lambda shape: pl.BlockSpec((1, 1) + shape, lambda j, te: (layer, te[0, j], 0, 0))
    return pl.pallas_call(
        _expert_kernel,
        grid_spec=pltpu.PrefetchScalarGridSpec(
            num_scalar_prefetch=1,
            grid=(MOE_ROWS // tm,),
            in_specs=[pl.BlockSpec((tm, D_MODEL), row),
                      wspec((D_MODEL, D_EXPERT)), wspec((D_MODEL, D_EXPERT)), wspec((D_EXPERT, D_MODEL))],
            out_specs=pl.BlockSpec((tm, D_MODEL), lambda j, te: (j, 0)),
            scratch_shapes=[pltpu.VMEM((D_MODEL, D_EXPERT), bf16), pltpu.VMEM((D_MODEL, D_EXPERT), bf16),
                            pltpu.VMEM((D_EXPERT, D_MODEL), bf16)]),
        out_shape=jax.ShapeDtypeStruct((MOE_ROWS, D_MODEL), f32),
        compiler_params=pltpu.CompilerParams(dimension_semantics=("arbitrary",), vmem_limit_bytes=VMEM_LIMIT),
        name="moe_experts",
    )(te, xs, w_gate, w_up, w_down)


def _gather_expert_rows(pos0_ref, pos1_ref, ys_ref, buf, sem):
    rows = buf.shape[2]
    i = pl.program_id(0)
    slot = i % 2

    def issue(tile, sl):
        base = tile * rows

        def body(t, carry):
            for s, pos_ref in enumerate((pos0_ref, pos1_ref)):
                pltpu.make_async_copy(ys_ref.at[pl.ds(pos_ref[base + t], 1)], buf.at[sl, s, pl.ds(t, 1)],
                                      sem.at[sl]).start()
            return carry

        lax.fori_loop(0, rows, body, 0, unroll=8)

    @pl.when(i == 0)
    def _():
        issue(0, 0)

    @pl.when(i + 1 < pl.num_programs(0))
    def _():
        issue(i + 1, 1 - slot)

    for s in range(2):
        pltpu.make_async_copy(ys_ref.at[pl.ds(0, rows)], buf.at[slot, s], sem.at[slot]).wait()
    return buf[slot, 0], buf[slot, 1]


def _moe_residual(pos0_ref, pos1_ref, ys_ref, x1_ref, rc_ref, mod_ref, buf, sem):
    y0, y1 = _gather_expert_rows(pos0_ref, pos1_ref, ys_ref, buf, sem)
    rc = rc_ref[...]
    return x1_ref[...] + mod_ref[0][5:6] * (rc[:, 2:3] * y0 + rc[:, 3:4] * y1)


def _final_kernel(pos0_ref, pos1_ref, ys_ref, x1_ref, rc_ref, mod_ref, fn_ref, yc_ref, yl_ref, buf, sem):
    x2 = _moe_residual(pos0_ref, pos1_ref, ys_ref, x1_ref, rc_ref, mod_ref, buf, sem)
    y = x2 * lax.rsqrt(jnp.mean(x2 * x2, axis=-1, keepdims=True) + EPS) * fn_ref[...]
    is_ctx = pl.program_id(0) < N_CTX // x1_ref.shape[0]

    @pl.when(is_ctx)
    def _():
        yc_ref[...] = y

    @pl.when(jnp.logical_not(is_ctx))
    def _():
        yl_ref[...] = y


def _final_combine(pos0, pos1, ys, x1, rc, mods, fn):
    tc = COMBINE_TILE
    row = lambda i, p0, p1: (i, 0)
    n_ctx_tiles = N_CTX // tc
    return pl.pallas_call(
        _final_kernel,
        grid_spec=pltpu.PrefetchScalarGridSpec(
            num_scalar_prefetch=2,
            grid=(N_TOK // tc,),
            in_specs=[pl.BlockSpec(memory_space=pl.ANY),
                      pl.BlockSpec((tc, D_MODEL), row),
                      pl.BlockSpec((tc, LANE), row),
                      pl.BlockSpec((1, 6, D_MODEL), lambda i, p0, p1: (_mod_row(i, tc), 0, 0)),
                      pl.BlockSpec((1, D_MODEL), lambda i, p0, p1: (0, 0))],
            out_specs=[pl.BlockSpec((tc, D_MODEL), lambda i, p0, p1: (jnp.minimum(i, n_ctx_tiles - 1), 0)),
                       pl.BlockSpec((tc, D_MODEL), lambda i, p0, p1: (jnp.maximum(i - n_ctx_tiles, 0), 0))],
            scratch_shapes=[pltpu.VMEM((2, 2, tc, D_MODEL), f32), pltpu.SemaphoreType.DMA((2,))]),
        out_shape=[jax.ShapeDtypeStruct((N_CTX, D_MODEL), f32), jax.ShapeDtypeStruct((N_LAT, D_MODEL), f32)],
        compiler_params=pltpu.CompilerParams(dimension_semantics=("arbitrary",), vmem_limit_bytes=VMEM_LIMIT),
        name="moe_combine_final",
    )(pos0, pos1, ys, x1, rc, mods, fn)


def _moe_experts(h2, route_t, w_gate, w_up, w_down, layer, xs_buf):
    rows, te, runs = _rank(route_t)
    xs = _dispatch(runs[:, :3, :N_EXPERTS].reshape(-1), h2, rows, xs_buf)
    return rows[0], rows[1], _experts(te, xs, w_gate, w_up, w_down, layer), xs


def _pad_heads(w):
    lead = w.shape[:-1]
    w = w.reshape(*lead, ML_HEADS, ML_DIM)
    w = jnp.pad(w, [(0, 0)] * len(lead) + [(0, 0), (0, ML_PAD - ML_DIM)])
    return w.reshape(*lead, ML_PW)


def _pack_in_cols(wb):
    o = 0
    qa = wb[..., o:o + NA_WIDTH] * (NA_DIM ** -0.5)
    ka = wb[..., o + NA_WIDTH:o + 2 * NA_WIDTH]
    va = wb[..., o + 2 * NA_WIDTH:o + 3 * NA_WIDTH]
    o += 3 * NA_WIDTH
    qb, kb, vb, ob = [_pad_heads(wb[..., o + j * ML_WIDTH:o + (j + 1) * ML_WIDTH]) for j in range(4)]
    o += 4 * ML_WIDTH
    gates = wb[..., o:o + N_GATE_COLS]
    o += N_GATE_COLS
    pool = wb[..., o:o + POOL_WIDTH]
    main = jnp.concatenate([qa, ka, va, qb, vb, ob, pool], axis=-1)
    gates_p = jnp.pad(gates, [(0, 0)] * (gates.ndim - 1) + [(0, LANE - N_GATE_COLS)])
    return main, gates_p, jnp.concatenate([kb, gates], axis=-1)


def _pack_w_in(w, b):
    w_main, w_gates, w_feat = _pack_in_cols(w)
    b_main, b_gates, b_feat = _pack_in_cols(b.astype(f32))
    return (w_main.astype(bf16), b_main[:, None], w_gates.astype(bf16), b_gates[:, None],
            jnp.swapaxes(w_feat, 1, 2).astype(bf16), b_feat[:, :, None])


def _pack_w_out(w):
    n_l = w.shape[0]
    wb = w[:, NA_WIDTH:NA_WIDTH + ML_WIDTH].reshape(n_l, ML_HEADS, ML_DIM, D_MODEL)
    wb = jnp.pad(wb, ((0, 0), (0, 0), (0, ML_PAD - ML_DIM), (0, 0))).reshape(n_l, ML_PW, D_MODEL)
    return jnp.concatenate([w[:, :NA_WIDTH], wb, w[:, NA_WIDTH + ML_WIDTH:]], axis=1).astype(bf16)


def _block_diag(w):
    n_l, g, c, _ = w.shape
    eye = jnp.eye(g, dtype=w.dtype)
    return (eye[None, :, None, :, None] * w[:, :, :, None, :]).reshape(n_l, g * c, g * c)


def kernel(x_prompt, x_sample, cache_k_attn, cache_v_attn, state_mlstm_C, state_mlstm_n, state_mlstm_m, c, c_ctx,
           w_ada, b_ada, norm1, w_in, b_in, rpb, ml_norm, w_pool, pool_scale, w_out, norm2, w_router, b_router,
           w_gate, w_up, w_down, final_norm):
    dt = x_prompt.dtype
    x = jnp.concatenate([x_prompt.reshape(N_CTX, D_MODEL), x_sample.reshape(N_LAT, D_MODEL)], axis=0).astype(f32)
    cvec = jnp.concatenate([c_ctx[None], c, jnp.zeros((8 - 1 - DEC_BATCH, D_MODEL), c.dtype)], axis=0).astype(f32)
    mods_all = _ada(cvec, w_ada.astype(f32), b_ada.astype(f32))
    mods_all = mods_all[:, :1 + DEC_BATCH].reshape(DEPTH, 1 + DEC_BATCH, 6, D_MODEL)

    wr_t = w_router.astype(f32).T
    br_t = b_router.astype(f32)[:, None]
    fn = final_norm.astype(f32)[None]

    na_bias = _na_bias_tables(rpb)
    xs_buf = jnp.zeros((MOE_ROWS, D_MODEL), bf16)
    in_params = (norm1.astype(f32)[:, None],) + _pack_w_in(w_in, b_in)
    out_params = (_block_diag(w_pool.astype(f32)).astype(bf16), pool_scale.astype(f32)[:, None],
                  _pad_heads(ml_norm.astype(f32))[:, None], _pack_w_out(w_out), norm2.astype(f32)[:, None])

    ks, vs, Cs, ns, ms = [], [], [], [], []
    pending = None
    for l in range(DEPTH):
        mods = mods_all[l]
        if pending is None:
            qkva, kv32, qvo, kt, gates, gates_t, pin = _in_proj(x, mods, l, *in_params)
        else:
            x, qkva, kv32, qvo, kt, gates, gates_t, pin = _moe_in_proj(*pending, mods_all[l - 1], mods, l, *in_params)
        ks.append(kv32[:N_CTX, :NA_WIDTH].reshape(BATCH, SEQ, NA_HEADS, NA_DIM))
        vs.append(kv32[:N_CTX, NA_WIDTH:].reshape(BATCH, SEQ, NA_HEADS, NA_DIM))

        oa_ctx = _ctx_attention(qkva.reshape(N_TOK // SEQ, SEQ, W_A))
        ck = (cache_k_attn[:, l].reshape(DEC_BATCH, PAST_LEN, NA_WIDTH)).astype(bf16)
        cv = (cache_v_attn[:, l].reshape(DEC_BATCH, PAST_LEN, NA_WIDTH)).astype(bf16)
        oa_lat = _neighborhood_attention(qkva.reshape(N_TOK // DEC_SEQ, DEC_SEQ, W_A), ck, cv, na_bias, l)

        c_l, m_l = _pack_ml_state(state_mlstm_C[:, l], state_mlstm_n[:, l], state_mlstm_m[:, l])
        hf, hb, c_fin, m_fin = _mlstm(qvo, kt, gates, gates_t, c_l, m_l)
        C_l, n_l, m_l2 = _unpack_ml_state(c_fin[:BATCH], m_fin[:BATCH])
        Cs.append(C_l)
        ns.append(n_l)
        ms.append(m_l2)

        x1, h2, route_t, rc = _out_proj(x, mods, l, oa_ctx.reshape(N_CTX, NA_WIDTH), oa_lat.reshape(N_LAT, NA_WIDTH),
                                        hf.reshape(N_TOK, ML_PW), hb.reshape(N_TOK, ML_PW), qvo, pin,
                                        *out_params, wr_t, br_t)
        pos0, pos1, ys, xs_buf = _moe_experts(h2, route_t, w_gate, w_up, w_down, l, xs_buf)
        pending = (pos0, pos1, ys, x1, rc)

    x = _final_combine(*pending, mods_all[DEPTH - 1], fn)
    y_prompt = x[0].reshape(BATCH, SEQ, D_MODEL).astype(dt)
    y_sample = x[1].reshape(DEC_BATCH, DEC_SEQ, D_MODEL).astype(dt)
    return (y_prompt, y_sample,
            jnp.stack(ks, axis=1).astype(dt), jnp.stack(vs, axis=1).astype(dt),
            jnp.stack(Cs, axis=1).astype(dt), jnp.stack(ns, axis=1).astype(dt), jnp.stack(ms, axis=1).astype(dt))
```

```python
import functools

import numpy as np
import jax
import jax.numpy as jnp
from jax import lax
from jax.experimental import pallas as pl
from jax.experimental.pallas import tpu as pltpu

D_MODEL = 1024
BATCH = 16
SEQ = 256
DEPTH = 4
DEC_BATCH = 2
DEC_SEQ = 4096
PAST_LEN = 256
GRID_W = 64
EPS = 1e-6
NEG_INF = -1e30
NA_HEADS = 6
NA_DIM = 64
NA_WIDTH = NA_HEADS * NA_DIM
NA_ROWS = 8
NA_COLS = 16
RPB_ROWS = 2 * NA_ROWS - 1
RPB_COLS = 2 * NA_COLS - 1
ML_HEADS = 4
ML_DIM = 96
ML_WIDTH = ML_HEADS * ML_DIM
POOL_WINDOWS = (2, 4, 8, 16)
POOL_GROUPS = 4
POOL_DIM = 64
POOL_WIDTH = POOL_GROUPS * POOL_DIM
N_GATE_COLS = 4 * ML_HEADS
N_EXPERTS = 16
N_EXPERT_GROUPS = 4
EXPERTS_PER_GROUP = N_EXPERTS // N_EXPERT_GROUPS
D_EXPERT = 512
ADA_DIM = 6 * D_MODEL

N_CTX = BATCH * SEQ
N_LAT = DEC_BATCH * DEC_SEQ
N_TOK = N_CTX + N_LAT
LANE = 128
ML_PAD = LANE
ML_PW = ML_HEADS * ML_PAD
CAUG = ML_PAD
NA_PAIRS = NA_HEADS // 2
TOK_TILE = 512
ML_CHUNK = 256
NA_QROWS = 4
NA_KROWS = NA_QROWS + NA_ROWS - 1
POOL_HALO = max(POOL_WINDOWS) // 2
POOL_BLOCK = 128
MOE_TILE = 512
MOE_CHUNK = 16
MOE_LOCAL_ROWS = -(-(2 * TOK_TILE + N_EXPERTS * (MOE_CHUNK - 1)) // LANE) * LANE
MOE_ROWS = -(-(2 * N_TOK + (N_TOK // TOK_TILE) * N_EXPERTS * (MOE_CHUNK - 1) + N_EXPERTS * (MOE_TILE - 1))
             // MOE_TILE) * MOE_TILE
COMBINE_TILE = 256
VMEM_LIMIT = 56 * 1024 * 1024

W_A = 3 * NA_WIDTH
W_B = 3 * ML_PW
N_TCOLS = ML_PW + N_GATE_COLS
W_MAIN = W_A + W_B + POOL_WIDTH

f32 = jnp.float32
bf16 = jnp.bfloat16
HI = lax.Precision.HIGHEST


def _nt(a, b, **kw):
    return lax.dot_general(a, b, (((1,), (1,)), ((), ())), preferred_element_type=f32, **kw)


def _mod_row(i, tile):
    n_ctx_tiles = N_CTX // tile
    per_batch = DEC_SEQ // tile
    return jnp.where(i < n_ctx_tiles, 0, 1 + (i - n_ctx_tiles) // per_batch)


def _ada_kernel(c_ref, w_ref, b_ref, o_ref):
    s = c_ref[...]
    s = s * jax.nn.sigmoid(s)
    o_ref[0] = jnp.dot(s.astype(bf16), w_ref[0].astype(bf16), preferred_element_type=f32) + b_ref[0]


def _ada(cvec, w_ada, b_ada):
    nj = ADA_DIM // D_MODEL
    return pl.pallas_call(
        _ada_kernel,
        grid=(DEPTH, nj),
        in_specs=[pl.BlockSpec((8, D_MODEL), lambda l, j: (0, 0)),
                  pl.BlockSpec((1, D_MODEL, D_MODEL), lambda l, j: (l, 0, j)),
                  pl.BlockSpec((1, 1, D_MODEL), lambda l, j: (l, 0, j))],
        out_specs=pl.BlockSpec((1, 8, D_MODEL), lambda l, j: (l, 0, j)),
        out_shape=jax.ShapeDtypeStruct((DEPTH, 8, ADA_DIM), f32),
        name="ada_mod",
    )(cvec, w_ada, b_ada.reshape(DEPTH, 1, ADA_DIM))


def _in_kernel(x_ref, *refs):
    _in_body(x_ref[...], *refs)


def _moe_in_kernel(pos0_ref, pos1_ref, ys_ref, x1_ref, rc_ref, mod_prev_ref, *refs):
    in_refs, x_out_ref, out_refs, (buf, sem) = refs[:8], refs[8], refs[9:-2], refs[-2:]
    x = _moe_residual(pos0_ref, pos1_ref, ys_ref, x1_ref, rc_ref, mod_prev_ref, buf, sem)
    x_out_ref[...] = x
    _in_body(x, *in_refs, *out_refs)


def _in_body(x, mod_ref, n1_ref, w_ref, b_ref, wg_ref, bg_ref, wt_ref, bt_ref,
             a_ref, kv_ref, b_out_ref, kt_ref, g_ref, gt_ref, pin_ref):
    mod = mod_ref[0]
    h = x * lax.rsqrt(jnp.mean(x * x, axis=-1, keepdims=True) + EPS) * n1_ref[...]
    h = (h * (1.0 + mod[1:2]) + mod[0:1]).astype(bf16)
    pa = jnp.dot(h, w_ref[:, 0:W_A], preferred_element_type=f32) + b_ref[:, 0:W_A]
    a_ref[...] = pa.astype(bf16)
    kv_ref[...] = pa[:, NA_WIDTH:W_A]
    for j in range(3):
        lo = W_A + j * ML_PW
        pb = jnp.dot(h, w_ref[:, lo:lo + ML_PW], preferred_element_type=f32) + b_ref[:, lo:lo + ML_PW]
        if j == 0:
            pb = pb * (ML_DIM ** -0.5)
        b_out_ref[:, j * ML_PW:(j + 1) * ML_PW] = pb.astype(bf16)
    lo = W_A + W_B
    pin_ref[...] = jnp.dot(h, w_ref[:, lo:lo + POOL_WIDTH], preferred_element_type=f32) + b_ref[:, lo:lo + POOL_WIDTH]
    g_ref[...] = jnp.dot(h, wg_ref[...], preferred_element_type=f32) + bg_ref[...]
    t = _nt(wt_ref[...], h) + bt_ref[...]
    kt_ref[...] = t[0:ML_PW].astype(bf16)
    gt_ref[...] = t[ML_PW:N_TCOLS]


def _in_proj_specs(layer):
    tm = TOK_TILE
    lyr = lambda shape: pl.BlockSpec((None,) + shape, lambda i, *_: (layer, 0, 0))
    rows = lambda width: pl.BlockSpec((tm, width), lambda i, *_: (i, 0))
    cols = lambda height: pl.BlockSpec((height, tm), lambda i, *_: (0, i))
    param_specs = [pl.BlockSpec((1, 6, D_MODEL), lambda i, *_: (_mod_row(i, tm), 0, 0)),
                   lyr((1, D_MODEL)), lyr((D_MODEL, W_MAIN)), lyr((1, W_MAIN)), lyr((D_MODEL, LANE)), lyr((1, LANE)),
                   lyr((N_TCOLS, D_MODEL)), lyr((N_TCOLS, 1))]
    out_specs = [rows(W_A), rows(2 * NA_WIDTH), rows(W_B), cols(ML_PW), rows(LANE), cols(N_GATE_COLS),
                 rows(POOL_WIDTH)]
    out_shape = [jax.ShapeDtypeStruct((N_TOK, W_A), bf16),
                 jax.ShapeDtypeStruct((N_TOK, 2 * NA_WIDTH), f32),
                 jax.ShapeDtypeStruct((N_TOK, W_B), bf16),
                 jax.ShapeDtypeStruct((ML_PW, N_TOK), bf16),
                 jax.ShapeDtypeStruct((N_TOK, LANE), f32),
                 jax.ShapeDtypeStruct((N_GATE_COLS, N_TOK), f32),
                 jax.ShapeDtypeStruct((N_TOK, POOL_WIDTH), f32)]
    return rows, param_specs, out_specs, out_shape


def _in_proj(x, mods, layer, *params):
    rows, param_specs, out_specs, out_shape = _in_proj_specs(layer)
    return pl.pallas_call(
        _in_kernel,
        grid=(N_TOK // TOK_TILE,),
        in_specs=[rows(D_MODEL)] + param_specs,
        out_specs=out_specs,
        out_shape=out_shape,
        compiler_params=pltpu.CompilerParams(dimension_semantics=("arbitrary",), vmem_limit_bytes=VMEM_LIMIT),
        name="in_proj",
    )(x, mods, *params)


def _moe_in_proj(pos0, pos1, ys, x1, rc, mods_prev, mods, layer, *params):
    tm = TOK_TILE
    rows, param_specs, out_specs, out_shape = _in_proj_specs(layer)
    return pl.pallas_call(
        _moe_in_kernel,
        grid_spec=pltpu.PrefetchScalarGridSpec(
            num_scalar_prefetch=2,
            grid=(N_TOK // tm,),
            in_specs=[pl.BlockSpec(memory_space=pl.ANY), rows(D_MODEL), rows(LANE),
                      pl.BlockSpec((1, 6, D_MODEL), lambda i, *_: (_mod_row(i, tm), 0, 0))] + param_specs,
            out_specs=[rows(D_MODEL)] + out_specs,
            scratch_shapes=[pltpu.VMEM((2, 2, tm, D_MODEL), f32), pltpu.SemaphoreType.DMA((2,))]),
        out_shape=[jax.ShapeDtypeStruct((N_TOK, D_MODEL), f32)] + out_shape,
        compiler_params=pltpu.CompilerParams(dimension_semantics=("arbitrary",), vmem_limit_bytes=VMEM_LIMIT),
        name="moe_combine_in_proj",
    )(pos0, pos1, ys, x1, rc, mods_prev, mods, *params)


def _pair_attention(qp, parts):
    lane = lax.broadcasted_iota(jnp.int32, (1, LANE), 1)
    outs = []
    for j in range(2):
        in_half = (lane >= j * NA_DIM) & (lane < (j + 1) * NA_DIM)
        qm = jnp.where(in_half, qp, jnp.zeros_like(qp))
        scores = []
        for k, _, bias in parts:
            s = _nt(qm, k)
            if bias is not None:
                s = s + bias[j]
            scores.append(s)
        m = scores[0].max(axis=-1, keepdims=True)
        for s in scores[1:]:
            m = jnp.maximum(m, s.max(axis=-1, keepdims=True))
        den = None
        acc = None
        for s, (_, v, _) in zip(scores, parts):
            p = jnp.exp(s - m)
            ps = p.sum(axis=-1, keepdims=True)
            den = ps if den is None else den + ps
            o = jnp.dot(p.astype(bf16), v, preferred_element_type=f32)
            acc = o if acc is None else acc + o
        outs.append(acc / den)
    return jnp.where(lane < NA_DIM, outs[0], outs[1])


def _ctx_attn_kernel(q_ref, k_ref, v_ref, o_ref):
    for p in range(NA_PAIRS):
        sl = slice(p * LANE, (p + 1) * LANE)
        o = _pair_attention(q_ref[0, :, sl], [(k_ref[0, :, sl], v_ref[0, :, sl], None)])
        o_ref[0, :, sl] = o.astype(bf16)


def _ctx_attention(qkv):
    blk = lambda c: pl.BlockSpec((1, SEQ, NA_WIDTH), lambda b, c=c: (b, 0, c))
    return pl.pallas_call(
        _ctx_attn_kernel,
        grid=(BATCH,),
        in_specs=[blk(0), blk(1), blk(2)],
        out_specs=pl.BlockSpec((1, SEQ, NA_WIDTH), lambda b: (b, 0, 0)),
        out_shape=jax.ShapeDtypeStruct((BATCH, SEQ, NA_WIDTH), bf16),
        name="ctx_attention",
    )(qkv, qkv, qkv)


def _na_window_start(rb):
    return jnp.clip(rb * NA_QROWS - NA_ROWS // 2, 0, DEC_SEQ // GRID_W - NA_KROWS)


def _na_bias(tab_ref, head, rb):
    rows = DEC_SEQ // GRID_W
    ws = _na_window_start(rb)
    lane = lax.broadcasted_iota(jnp.int32, (1, NA_KROWS * GRID_W), 1)
    per_qrow = []
    for dq in range(NA_QROWS):
        qr = rb * NA_QROWS + dq
        a0 = ws - qr + (NA_ROWS - 1) + NA_KROWS
        tiles = [tab_ref[head, a0 + 2 * j] for j in range((NA_KROWS + 1) // 2)]
        t = jnp.concatenate(tiles, axis=1)[:, :NA_KROWS * GRID_W]
        lo = (jnp.clip(qr - NA_ROWS // 2, 0, rows - NA_ROWS) - ws) * GRID_W
        ok = (lane >= lo) & (lane < lo + NA_ROWS * GRID_W)
        per_qrow.append(jnp.where(ok, t, NEG_INF))
    return jnp.concatenate(per_qrow, axis=0)


def _na_kernel(q_ref, k_ref, v_ref, ck_ref, cv_ref, tab_ref, o_ref):
    rb = pl.program_id(1)
    start = pl.multiple_of(_na_window_start(rb) * GRID_W, GRID_W)
    nk = NA_KROWS * GRID_W
    for p in range(NA_PAIRS):
        sl = slice(p * LANE, (p + 1) * LANE)
        bias = [_na_bias(tab_ref.at[0], 2 * p + j, rb) for j in range(2)]
        parts = [(k_ref[0, pl.ds(start, nk), sl], v_ref[0, pl.ds(start, nk), sl], bias),
                 (ck_ref[0, :, sl], cv_ref[0, :, sl], None)]
        o = _pair_attention(q_ref[0, :, sl], parts)
        o_ref[0, :, sl] = o.astype(bf16)


def _neighborhood_attention(qkv, ck, cv, tables, layer):
    nq = NA_QROWS * GRID_W
    n_rb = DEC_SEQ // nq
    return pl.pallas_call(
        _na_kernel,
        grid=(DEC_BATCH, n_rb),
        in_specs=[pl.BlockSpec((1, nq, NA_WIDTH), lambda b, r: (1 + b, r, 0)),
                  pl.BlockSpec((1, DEC_SEQ, NA_WIDTH), lambda b, r: (1 + b, 0, 1)),
                  pl.BlockSpec((1, DEC_SEQ, NA_WIDTH), lambda b, r: (1 + b, 0, 2)),
                  pl.BlockSpec((1, PAST_LEN, NA_WIDTH), lambda b, r: (b, 0, 0)),
                  pl.BlockSpec((1, PAST_LEN, NA_WIDTH), lambda b, r: (b, 0, 0)),
                  pl.BlockSpec((1,) + tables.shape[1:], lambda b, r: (layer, 0, 0, 0, 0))],
        out_specs=pl.BlockSpec((1, nq, NA_WIDTH), lambda b, r: (b, r, 0)),
        out_shape=jax.ShapeDtypeStruct((DEC_BATCH, DEC_SEQ, NA_WIDTH), bf16),
        compiler_params=pltpu.CompilerParams(dimension_semantics=("arbitrary", "arbitrary"),
                                             vmem_limit_bytes=VMEM_LIMIT),
        name="neighborhood_attention",
    )(qkv, qkv, qkv, ck, cv, tables)


def _na_bias_tables(rpb):
    qc = np.arange(GRID_W)[:, None]
    kc = np.arange(GRID_W)[None, :]
    dc = np.clip(kc - qc + NA_COLS - 1, 0, RPB_COLS - 1)
    col_start = np.clip(qc - NA_COLS // 2, 0, GRID_W - NA_COLS)
    col_ok = (kc >= col_start) & (kc < col_start + NA_COLS)
    pick_col = (dc[None] == np.arange(RPB_COLS)[:, None, None]).astype(np.float32)
    by_col = jnp.einsum('lhab,bqk->lhaqk', rpb.astype(f32), pick_col, precision=HI)
    by_col = jnp.where(col_ok[None, None, None], by_col, NEG_INF)
    by_col = jnp.pad(by_col, ((0, 0), (0, 0), (NA_KROWS, NA_KROWS + 1), (0, 0), (0, 0)), constant_values=NEG_INF)
    return jnp.concatenate([by_col[:, :, :-1], by_col[:, :, 1:]], axis=-1)


def _log_sigmoid(x):
    return -(jnp.maximum(-x, 0.0) + jnp.log(1.0 + jnp.exp(-jnp.abs(x))))


def _split3(x):
    hi = x.astype(bf16)
    r1 = x - hi.astype(f32)
    mid = r1.astype(bf16)
    lo = (r1 - mid.astype(f32)).astype(bf16)
    return hi, mid, lo


def _mlstm_kernel(qf_ref, vf_ref, ktf_ref, gf_ref, gtf_ref, qb_ref, vb_ref, ktb_ref, gb_ref, gtb_ref,
                  c0_ref, m0_ref, hf_ref, hb_ref, c_out_ref, m_out_ref, c_scr, m_scr):
    L = ML_CHUNK
    seq, c, n_chunks, _ = _ml_schedule(pl.program_id(0))

    @pl.when(c == 0)
    def _():
        is_ctx = seq < BATCH
        c_scr[...] = jnp.where(is_ctx, 0.0, c0_ref[0])
        m_scr[...] = jnp.where(is_ctx, 0.0, m0_ref[0])

    ri = lax.broadcasted_iota(jnp.int32, (L, L), 0)
    ci = lax.broadcasted_iota(jnp.int32, (L, L), 1)
    lane = lax.broadcasted_iota(jnp.int32, (L, ML_PAD), 1)
    is_ncol = lane == ML_DIM
    lower = ri >= ci
    upper = ri <= ci
    lower_b = jnp.where(lower, 1.0, 0.0).astype(bf16)
    upper_b = jnp.where(upper, 1.0, 0.0).astype(bf16)
    dirs = ((qf_ref, ktf_ref, vf_ref, gf_ref, gtf_ref, hf_ref), (qb_ref, ktb_ref, vb_ref, gb_ref, gtb_ref, hb_ref))
    for d, (q_ref, kt_ref, v_ref, g_ref, gt_ref, h_ref) in enumerate(dirs):
        g = g_ref[...][:, 0:N_GATE_COLS]
        gt = gt_ref[...]
        lf_c = _log_sigmoid(g)
        lf_r = _log_sigmoid(gt)
        b_cols = sum(jnp.dot(lower_b, part, preferred_element_type=f32) for part in _split3(lf_c))
        b_rows = sum(jnp.dot(part, upper_b, preferred_element_type=f32) for part in _split3(lf_r))
        tot_c = jnp.sum(lf_c, axis=0, keepdims=True)
        tot_r = jnp.sum(lf_r, axis=1, keepdims=True)
        visible = lower
        if d == 1:
            b_cols = tot_c - b_cols + lf_c
            b_rows = tot_r - b_rows + lf_r
            visible = upper
        for hd in range(ML_HEADS):
            st = d * ML_HEADS + hd
            ci_ = 2 * ML_HEADS * d + hd
            cf_ = ci_ + ML_HEADS
            sl = slice(hd * ML_PAD, (hd + 1) * ML_PAD)
            bc = b_cols[:, cf_:cf_ + 1]
            br = b_rows[cf_:cf_ + 1, :]
            li_r = gt[ci_:ci_ + 1, :]
            m_prev = m_scr[st:st + 1, 0:1]
            dmat = jnp.where(visible, bc - br + li_r, NEG_INF)
            inter = bc + m_prev
            m_t = jnp.maximum(inter, dmat.max(axis=-1, keepdims=True))
            w_intra = jnp.exp(dmat - m_t)
            w_inter = jnp.exp(inter - m_t)
            qh = q_ref[0, :, sl]
            kht = kt_ref[sl, :]
            v_aug = jnp.where(is_ncol, jnp.ones((), bf16), v_ref[0, :, sl])
            s = (jnp.dot(qh, kht, preferred_element_type=f32) * w_intra).astype(bf16)
            c_aug = c_scr[st]
            na = (w_inter * jnp.dot(qh, c_aug.astype(bf16), preferred_element_type=f32)
                  + jnp.dot(s, v_aug, preferred_element_type=f32))
            den = na[:, ML_DIM:ML_DIM + 1]
            h_ref[0, :, sl] = jnp.where(lane < ML_DIM, na / jnp.maximum(jnp.abs(den), jnp.exp(-m_t)), 0.0)
            b_end = tot_r[cf_:cf_ + 1, :]
            g_row = b_end - br + li_r
            m_new = jnp.maximum(b_end + m_prev, g_row.max(axis=1, keepdims=True))
            decay = jnp.exp(b_end + m_prev - m_new)
            kwt = (kht.astype(f32) * jnp.exp(g_row - m_new)).astype(bf16)
            c_scr[st] = decay * c_aug + jnp.dot(kwt, v_aug, preferred_element_type=f32)
            m_scr[st:st + 1, :] = jnp.broadcast_to(m_new, (1, LANE))

    @pl.when(c == n_chunks - 1)
    def _():
        c_out_ref[0] = c_scr[...]
        m_out_ref[0] = m_scr[...]


def _ml_schedule(s):
    nc_ctx, nc_lat = SEQ // ML_CHUNK, DEC_SEQ // ML_CHUNK
    n_ctx_steps = BATCH * nc_ctx
    is_ctx = s < n_ctx_steps
    t = s - n_ctx_steps
    seq = jnp.where(is_ctx, s // nc_ctx, BATCH + t // nc_lat)
    c = jnp.where(is_ctx, s % nc_ctx, t % nc_lat)
    nc = jnp.where(is_ctx, nc_ctx, nc_lat)
    base = jnp.where(is_ctx, (s // nc_ctx) * nc_ctx, n_ctx_steps + (t // nc_lat) * nc_lat)
    return seq, c, nc, base


def _mlstm(qvo, kt, gates, gates_t, c0, m0):
    L = ML_CHUNK
    n_seq = BATCH + DEC_BATCH

    def fwd(s):
        _, c, _, base = _ml_schedule(s)
        return base + c

    def bwd(s):
        _, c, nc, base = _ml_schedule(s)
        return base + nc - 1 - c

    seq_of = lambda s: _ml_schedule(s)[0]
    lat_of = lambda s: jnp.maximum(seq_of(s) - BATCH, 0)

    def specs(pos):
        return [pl.BlockSpec((1, L, ML_PW), lambda s, j=j: (pos(s), 0, j)) for j in range(2)] + [
            pl.BlockSpec((ML_PW, L), lambda s: (0, pos(s))),
            pl.BlockSpec((L, LANE), lambda s: (pos(s), 0)),
            pl.BlockSpec((N_GATE_COLS, L), lambda s: (0, pos(s)))]

    q3 = qvo.reshape(N_TOK // L, L, W_B)
    n_str = 2 * ML_HEADS
    return pl.pallas_call(
        _mlstm_kernel,
        grid=(N_TOK // L,),
        in_specs=specs(fwd) + specs(bwd) + [
            pl.BlockSpec((1, n_str, ML_PAD, CAUG), lambda s: (lat_of(s), 0, 0, 0)),
            pl.BlockSpec((1, n_str, LANE), lambda s: (lat_of(s), 0, 0))],
        out_specs=[pl.BlockSpec((1, L, ML_PW), lambda s: (fwd(s), 0, 0)),
                   pl.BlockSpec((1, L, ML_PW), lambda s: (bwd(s), 0, 0)),
                   pl.BlockSpec((1, n_str, ML_PAD, CAUG), lambda s: (seq_of(s), 0, 0, 0)),
                   pl.BlockSpec((1, n_str, LANE), lambda s: (seq_of(s), 0, 0))],
        out_shape=[jax.ShapeDtypeStruct((N_TOK // L, L, ML_PW), f32),
                   jax.ShapeDtypeStruct((N_TOK // L, L, ML_PW), f32),
                   jax.ShapeDtypeStruct((n_seq, n_str, ML_PAD, CAUG), f32),
                   jax.ShapeDtypeStruct((n_seq, n_str, LANE), f32)],
        scratch_shapes=[pltpu.VMEM((n_str, ML_PAD, CAUG), f32), pltpu.VMEM((n_str, LANE), f32)],
        compiler_params=pltpu.CompilerParams(dimension_semantics=("arbitrary",), vmem_limit_bytes=VMEM_LIMIT),
        name="mlstm",
    )(q3, q3, kt, gates, gates_t, q3, q3, kt, gates, gates_t, c0, m0)


def _pack_ml_state(C, n, m):
    B = C.shape[0]
    c_aug = jnp.zeros((B, 2, ML_HEADS, ML_PAD, CAUG), f32)
    c_aug = c_aug.at[:, :, :, :ML_DIM, :ML_DIM].set(C.astype(f32))
    c_aug = c_aug.at[:, :, :, :ML_DIM, ML_DIM].set(n.astype(f32))
    m_b = jnp.broadcast_to(m.astype(f32)[..., None], (B, 2, ML_HEADS, LANE))
    return c_aug.reshape(B, 2 * ML_HEADS, ML_PAD, CAUG), m_b.reshape(B, 2 * ML_HEADS, LANE)


def _unpack_ml_state(c_aug, m_b):
    B = c_aug.shape[0]
    c_aug = c_aug.reshape(B, 2, ML_HEADS, ML_PAD, CAUG)
    return (c_aug[:, :, :, :ML_DIM, :ML_DIM], c_aug[:, :, :, :ML_DIM, ML_DIM],
            m_b.reshape(B, 2, ML_HEADS, LANE)[..., 0])


def _pool_rows(u_prev, u_cur, u_next, w_bd, scale, t0, seq_len):
    tm = u_cur.shape[0]
    u_win = jnp.concatenate([u_prev, u_cur, u_next], axis=0)
    u_hi = u_win.astype(bf16)
    u_lo = (u_win - u_hi.astype(f32)).astype(bf16)
    lane = lax.broadcasted_iota(jnp.int32, (1, LANE), 1)
    blocks = []
    for r0 in range(0, tm, POOL_BLOCK):
        win = slice(r0, r0 + POOL_BLOCK + 2 * POOL_HALO)
        t_abs = t0 + r0 + lax.broadcasted_iota(jnp.int32, (POOL_BLOCK, 1), 0)
        s_abs = t0 + r0 - POOL_HALO + lax.broadcasted_iota(jnp.int32, (1, POOL_BLOCK + 2 * POOL_HALO), 1)
        t_loc = t_abs & (seq_len - 1)
        seq_start = t_abs - t_loc
        means = []
        for w in POOL_WINDOWS:
            lo = jnp.maximum(t_loc - w // 2, 0)
            hi = jnp.minimum(t_loc - w // 2 + w, seq_len)
            in_win = (s_abs >= seq_start + lo) & (s_abs < seq_start + hi)
            means.append((jnp.where(in_win, 1.0, 0.0).astype(bf16), 1.0 / (hi - lo).astype(f32)))
        pooled = []
        for p in range(POOL_GROUPS // 2):
            sl = slice(p * LANE, (p + 1) * LANE)
            halves = []
            for a, inv_cnt in means[2 * p:2 * p + 2]:
                tot = (jnp.dot(a, u_hi[win, sl], preferred_element_type=f32)
                       + jnp.dot(a, u_lo[win, sl], preferred_element_type=f32))
                halves.append(tot * inv_cnt)
            pooled.append(jnp.where(lane < POOL_DIM, halves[0], halves[1]) - u_cur[r0:r0 + POOL_BLOCK, sl])
        blocks.append(jnp.concatenate(pooled, axis=1))
    pooled = jnp.concatenate(blocks, axis=0).astype(bf16)
    return jnp.dot(pooled, w_bd, preferred_element_type=f32) * scale


def _top2_sum(a, b, c, d):
    hi1, lo1 = jnp.maximum(a, b), jnp.minimum(a, b)
    hi2, lo2 = jnp.maximum(c, d), jnp.minimum(c, d)
    return jnp.maximum(hi1, hi2) + jnp.maximum(jnp.minimum(hi1, hi2), jnp.maximum(lo1, lo2))


def _first_match(vals, target):
    idx = jnp.full_like(target, float(len(vals) - 1))
    for i in range(len(vals) - 2, -1, -1):
        idx = jnp.where(vals[i] == target, float(i), idx)
    return idx


def _pick(vals, idx):
    out = vals[-1]
    for i in range(len(vals) - 2, -1, -1):
        out = jnp.where(idx == float(i), vals[i], out)
    return out


def _route(logits_t, bias_t):
    scores = jax.nn.sigmoid(logits_t)
    sel = scores + bias_t
    row = lambda a, i: a[i:i + 1, :]
    grp = [_top2_sum(*[row(sel, EXPERTS_PER_GROUP * g + i) for i in range(EXPERTS_PER_GROUP)])
           for g in range(N_EXPERT_GROUPS)]
    best = functools.reduce(jnp.maximum, grp)
    gidx = _first_match(grp, best)
    sel_g = [_pick([row(sel, EXPERTS_PER_GROUP * g + i) for g in range(N_EXPERT_GROUPS)], gidx)
             for i in range(EXPERTS_PER_GROUP)]
    sco_g = [_pick([row(scores, EXPERTS_PER_GROUP * g + i) for g in range(N_EXPERT_GROUPS)], gidx)
             for i in range(EXPERTS_PER_GROUP)]
    i0 = _first_match(sel_g, functools.reduce(jnp.maximum, sel_g))
    rest = [jnp.where(i0 == float(i), -jnp.inf, sel_g[i]) for i in range(EXPERTS_PER_GROUP)]
    i1 = _first_match(rest, functools.reduce(jnp.maximum, rest))
    s0, s1 = _pick(sco_g, i0), _pick(sco_g, i1)
    tot = s0 + s1
    rid = lax.broadcasted_iota(jnp.int32, (LANE, logits_t.shape[1]), 0)
    rows = (EXPERTS_PER_GROUP * gidx + i0, EXPERTS_PER_GROUP * gidx + i1, s0 / tot, s1 / tot)
    out = jnp.zeros(rid.shape, f32)
    for i, r in enumerate(rows):
        out = jnp.where(rid == i, r, out)
    return out


def _out_kernel(x_ref, mod_ref, oac_ref, oal_ref, hf_ref, hb_ref, ob_ref, up_ref, uc_ref, un_ref, wp_ref, psc_ref,
                mln_ref, wo_ref, n2_ref, wr_ref, br_ref, x1_ref, h2_ref, rt_ref, rc_ref):
    tm = x_ref.shape[0]
    i = pl.program_id(0)
    is_ctx = i < N_CTX // tm
    mod = mod_ref[0]
    out_a = jnp.where(is_ctx, oac_ref[...], oal_ref[...])
    out_c = _pool_rows(up_ref[...], uc_ref[...], un_ref[...], wp_ref[...], psc_ref[...], i * tm,
                       jnp.where(is_ctx, SEQ, DEC_SEQ)).astype(bf16)
    hsum = hf_ref[...] + hb_ref[...]
    outs_b = []
    for hd in range(ML_HEADS):
        sl = slice(hd * ML_PAD, (hd + 1) * ML_PAD)
        hh = hsum[:, sl]
        ms = jnp.sum(hh * hh, axis=-1, keepdims=True) * (1.0 / ML_DIM)
        hn = hh * lax.rsqrt(ms + EPS) * mln_ref[:, sl]
        outs_b.append((jax.nn.sigmoid(ob_ref[:, sl].astype(f32)) * hn).astype(bf16))
    out_b = jnp.concatenate(outs_b, axis=1)
    mixed = (jnp.dot(out_a, wo_ref[0:NA_WIDTH, :], preferred_element_type=f32)
             + jnp.dot(out_b, wo_ref[NA_WIDTH:NA_WIDTH + ML_PW, :], preferred_element_type=f32)
             + jnp.dot(out_c, wo_ref[NA_WIDTH + ML_PW:, :], preferred_element_type=f32))
    x1 = x_ref[...] + mod[2:3] * mixed
    x1_ref[...] = x1
    h2 = x1 * lax.rsqrt(jnp.mean(x1 * x1, axis=-1, keepdims=True) + EPS) * n2_ref[...]
    h2 = h2 * (1.0 + mod[4:5]) + mod[3:4]
    h2_ref[...] = h2.astype(bf16)
    route_t = _route(_nt(wr_ref[...], h2, precision=HI), br_ref[...])
    rt_ref[...] = route_t[0:8]
    rc_ref[...] = route_t.T


def _out_proj(x, mods, layer, oa_ctx, oa_lat, hf, hb, qvo, pin, w_bd, psc, mln, wo, n2, wr_t, br_t):
    tm = TOK_TILE
    const = lambda i: (0, 0)
    lyr = lambda shape: pl.BlockSpec((None,) + shape, lambda i: (layer, 0, 0))
    row = lambda i: (i, 0)
    n_ctx_tiles = N_CTX // tm
    halo_blocks = tm // POOL_HALO
    return pl.pallas_call(
        _out_kernel,
        grid=(N_TOK // tm,),
        in_specs=[pl.BlockSpec((tm, D_MODEL), row),
                  pl.BlockSpec((1, 6, D_MODEL), lambda i: (_mod_row(i, tm), 0, 0)),
                  pl.BlockSpec((tm, NA_WIDTH), lambda i: (jnp.minimum(i, n_ctx_tiles - 1), 0)),
                  pl.BlockSpec((tm, NA_WIDTH), lambda i: (jnp.maximum(i - n_ctx_tiles, 0), 0)),
                  pl.BlockSpec((tm, ML_PW), row),
                  pl.BlockSpec((tm, ML_PW), row),
                  pl.BlockSpec((tm, ML_PW), lambda i: (i, 2)),
                  pl.BlockSpec((POOL_HALO, POOL_WIDTH), lambda i: (jnp.maximum(i * halo_blocks - 1, 0), 0)),
                  pl.BlockSpec((tm, POOL_WIDTH), row),
                  pl.BlockSpec((POOL_HALO, POOL_WIDTH),
                               lambda i: (jnp.minimum((i + 1) * halo_blocks, N_TOK // POOL_HALO - 1), 0)),
                  lyr((POOL_WIDTH, POOL_WIDTH)), lyr((1, POOL_WIDTH)), lyr((1, ML_PW)),
                  lyr((NA_WIDTH + ML_PW + POOL_WIDTH, D_MODEL)), lyr((1, D_MODEL)),
                  pl.BlockSpec((N_EXPERTS, D_MODEL), const),
                  pl.BlockSpec((N_EXPERTS, 1), const)],
        out_specs=[pl.BlockSpec((tm, D_MODEL), row),
                   pl.BlockSpec((tm, D_MODEL), row),
                   pl.BlockSpec((8, tm), lambda i: (0, i)),
                   pl.BlockSpec((tm, LANE), row)],
        out_shape=[jax.ShapeDtypeStruct((N_TOK, D_MODEL), f32),
                   jax.ShapeDtypeStruct((N_TOK, D_MODEL), bf16),
                   jax.ShapeDtypeStruct((8, N_TOK), f32),
                   jax.ShapeDtypeStruct((N_TOK, LANE), f32)],
        compiler_params=pltpu.CompilerParams(dimension_semantics=("arbitrary",), vmem_limit_bytes=VMEM_LIMIT),
        name="out_proj_router",
    )(x, mods, oa_ctx, oa_lat, hf, hb, qvo, pin, pin, pin, w_bd, psc, mln, wo, n2, wr_t, br_t)


def _ceil_to(x, m):
    return jnp.floor((x + (m - 1)) * (1.0 / m)) * m


def _prefix_over_experts(v):
    er = lax.broadcasted_iota(jnp.int32, (N_EXPERTS, N_EXPERTS), 0)
    ec = lax.broadcasted_iota(jnp.int32, (N_EXPERTS, N_EXPERTS), 1)
    return jnp.dot(jnp.where(ec < er, 1.0, 0.0), v, preferred_element_type=f32, precision=HI)


def _experts_to_lanes(v):
    sub = lax.broadcasted_iota(jnp.int32, (N_EXPERTS, LANE), 0)
    lane = lax.broadcasted_iota(jnp.int32, (N_EXPERTS, LANE), 1)
    return jnp.sum(jnp.where(sub == lane, v, 0.0), axis=0, keepdims=True)


def _rank_kernel(rt_ref, pos_ref, te_ref, tab_ref, carry_ref):
    tm = rt_ref.shape[1]
    p = pl.program_id(0)
    i = pl.program_id(1)
    rid = lax.broadcasted_iota(jnp.int32, (N_EXPERTS, tm), 0).astype(f32)
    oh0 = rid == rt_ref[0:1, :]
    oh1 = rid == rt_ref[1:2, :]
    both = jnp.where(oh0 | oh1, 1.0, 0.0)
    runs = jnp.broadcast_to(_ceil_to(jnp.sum(both, axis=1, keepdims=True), MOE_CHUNK), (N_EXPERTS, LANE))

    @pl.when((p == 0) & (i == 0))
    def _():
        carry_ref[...] = jnp.zeros_like(carry_ref)

    @pl.when((p == 1) & (i == 0))
    def _():
        padded = _ceil_to(carry_ref[...], MOE_TILE)
        off = _prefix_over_experts(padded)
        carry_ref[...] = off
        total = jnp.sum(padded, axis=0, keepdims=True)
        n_used = total * (1.0 / MOE_TILE)
        tile = lax.broadcasted_iota(jnp.int32, (1, LANE), 1).astype(f32)
        row0 = jnp.minimum(tile, n_used - 1.0) * MOE_TILE
        expert = jnp.sum(jnp.where(off <= row0, 1.0, 0.0), axis=0, keepdims=True) - 1.0
        sub = lax.broadcasted_iota(jnp.int32, (8, LANE), 0)
        te_ref[...] = jnp.where(sub == 0, expert, jnp.where(sub == 1, n_used, 0.0)).astype(jnp.int32)

    @pl.when(p == 1)
    def _():
        sr = lax.broadcasted_iota(jnp.int32, (tm, tm), 0)
        sc = lax.broadcasted_iota(jnp.int32, (tm, tm), 1)
        earlier = jnp.dot(both.astype(bf16), jnp.where(sr < sc, 1.0, 0.0).astype(bf16),
                          preferred_element_type=f32)
        g_off = carry_ref[...]
        l_off = _prefix_over_experts(runs)
        g_row = g_off[:, 0:1] + earlier
        l_row = l_off[:, 0:1] + earlier
        pick = lambda oh, v: jnp.sum(jnp.where(oh, v, 0.0), axis=0, keepdims=True)
        rows = (pick(oh0, g_row), pick(oh1, g_row), pick(oh0, l_row), pick(oh1, l_row))
        sub = lax.broadcasted_iota(jnp.int32, (8, tm), 0)
        out = jnp.zeros((8, tm), f32)
        for k, r in enumerate(rows):
            out = jnp.where(sub == k, r, out)
        pos_ref[...] = out.astype(jnp.int32)
        sub = lax.broadcasted_iota(jnp.int32, (8, LANE), 0)
        tab = jnp.zeros((8, LANE), f32)
        for k, v in enumerate((runs * (1.0 / MOE_CHUNK), l_off, g_off)):
            tab = jnp.where(sub == k, _experts_to_lanes(v), tab)
        tab_ref[0] = tab.astype(jnp.int32)

    carry_ref[...] += runs


def _rank(route_t):
    tm = TOK_TILE
    n_tiles = N_TOK // tm
    return pl.pallas_call(
        _rank_kernel,
        grid=(2, n_tiles),
        in_specs=[pl.BlockSpec((8, tm), lambda p, i: (0, i))],
        out_specs=[pl.BlockSpec((8, tm), lambda p, i: (0, i * p)),
                   pl.BlockSpec((8, LANE), lambda p, i: (0, 0)),
                   pl.BlockSpec((1, 8, LANE), lambda p, i: (i * p, 0, 0))],
        out_shape=[jax.ShapeDtypeStruct((8, N_TOK), jnp.int32),
                   jax.ShapeDtypeStruct((8, LANE), jnp.int32),
                   jax.ShapeDtypeStruct((n_tiles, 8, LANE), jnp.int32)],
        scratch_shapes=[pltpu.VMEM((N_EXPERTS, LANE), f32)],
        compiler_params=pltpu.CompilerParams(dimension_semantics=("arbitrary", "arbitrary")),
        name="moe_rank",
    )(route_t)


def _dispatch_kernel(tab_ref, h_ref, rows_ref, xs_in_ref, xs_ref, loc, sem):
    del xs_in_ref
    tm = h_ref.shape[0]
    i = pl.program_id(0)
    rid = lax.broadcasted_iota(jnp.int32, (MOE_LOCAL_ROWS, tm), 0)
    sel = (rid == rows_ref[2:3, :]) | (rid == rows_ref[3:4, :])
    loc[...] = jnp.dot(jnp.where(sel, 1.0, 0.0).astype(bf16), h_ref[...], preferred_element_type=f32).astype(bf16)

    def chunk_copy(src_row, dst_row):
        return pltpu.make_async_copy(loc.at[pl.ds(pl.multiple_of(src_row, MOE_CHUNK), MOE_CHUNK)],
                                     xs_ref.at[pl.ds(pl.multiple_of(dst_row, MOE_CHUNK), MOE_CHUNK)], sem)

    n_total = 0
    for e in range(N_EXPERTS):
        n_chunks, l_off, g_off = (tab_ref[(3 * i + k) * N_EXPERTS + e] for k in range(3))

        def issue(c, carry, l_off=l_off, g_off=g_off):
            chunk_copy(l_off + c * MOE_CHUNK, g_off + c * MOE_CHUNK).start()
            return carry

        lax.fori_loop(0, n_chunks, issue, 0)
        n_total = n_total + n_chunks

    def wait_one(c, carry):
        chunk_copy(0, 0).wait()
        return carry

    lax.fori_loop(0, n_total, wait_one, 0)


def _dispatch(run_table, h2, rows, xs_init):
    tm = TOK_TILE
    return pl.pallas_call(
        _dispatch_kernel,
        grid_spec=pltpu.PrefetchScalarGridSpec(
            num_scalar_prefetch=1,
            grid=(N_TOK // tm,),
            in_specs=[pl.BlockSpec((tm, D_MODEL), lambda i, tab: (i, 0)),
                      pl.BlockSpec((8, tm), lambda i, tab: (0, i)),
                      pl.BlockSpec(memory_space=pl.ANY)],
            out_specs=pl.BlockSpec(memory_space=pl.ANY),
            scratch_shapes=[pltpu.VMEM((MOE_LOCAL_ROWS, D_MODEL), bf16), pltpu.SemaphoreType.DMA(())]),
        out_shape=jax.ShapeDtypeStruct(xs_init.shape, xs_init.dtype),
        input_output_aliases={3: 0},
        compiler_params=pltpu.CompilerParams(dimension_semantics=("arbitrary",), vmem_limit_bytes=VMEM_LIMIT),
        name="moe_dispatch",
    )(run_table, h2, rows, xs_init)


def _expert_kernel(te_ref, xs_ref, wg_hbm, wu_hbm, wd_hbm, ys_ref, wg_f32, wu_f32, wd_f32, wg_bf, wu_bf, wd_bf,
                   slot_ref, sem, *, layer):
    j = pl.program_id(0)
    n_used = te_ref[1, 0]
    used = j < n_used
    expert = te_ref[0, j]
    new_expert = jnp.logical_or(j == 0, expert != te_ref[0, jnp.maximum(j - 1, 0)])

    def weight_copies(e, slot):
        return [pltpu.make_async_copy(hbm.at[layer, e], buf.at[slot], sem.at[slot])
                for hbm, buf in ((wg_hbm, wg_f32), (wu_hbm, wu_f32), (wd_hbm, wd_f32))]

    @pl.when(j == 0)
    def _():
        slot_ref[0] = 1
        for cp in weight_copies(expert, 0):
            cp.start()

    @pl.when(jnp.logical_not(used))
    def _():
        ys_ref[...] = jnp.zeros_like(ys_ref)

    @pl.when(used & new_expert)
    def _():
        slot = 1 - slot_ref[0]
        slot_ref[0] = slot
        for cp in weight_copies(expert, slot):
            cp.wait()
        wg_bf[...] = wg_f32[slot].astype(bf16)
        wu_bf[...] = wu_f32[slot].astype(bf16)
        wd_bf[...] = wd_f32[slot].astype(bf16)
        nxt = lax.while_loop(lambda t: (t < n_used) & (te_ref[0, jnp.minimum(t, LANE - 1)] == expert),
                             lambda t: t + 1, j + 1)

        @pl.when(nxt < n_used)
        def _():
            for cp in weight_copies(te_ref[0, nxt], 1 - slot):
                cp.start()

    @pl.when(used)
    def _():
        x = xs_ref[...]
        hg = jnp.dot(x, wg_bf[...], preferred_element_type=f32)
        hu = jnp.dot(x, wu_bf[...], preferred_element_type=f32)
        hid = (hg * jax.nn.sigmoid(hg) * hu).astype(bf16)
        ys_ref[...] = jnp.dot(hid, wd_bf[...], preferred_element_type=f32)


def _experts(te, xs, w_gate, w_up, w_down, layer):
    tm = MOE_TILE
    row = lambda j, te: (jnp.minimum(j, te[1, 0] - 1), 0)
    hbm = pl.BlockSpec(memory_space=pl.ANY)
    return pl.pallas_call(
        functools.partial(_expert_kernel, layer=layer),
        grid_spec=pltpu.PrefetchScalarGridSpec(
            num_scalar_prefetch=1,
            grid=(MOE_ROWS // tm,),
            in_specs=[pl.BlockSpec((tm, D_MODEL), row), hbm, hbm, hbm],
            out_specs=pl.BlockSpec((tm, D_MODEL), lambda j, te: (j, 0)),
            scratch_shapes=[pltpu.VMEM((2, D_MODEL, D_EXPERT), f32), pltpu.VMEM((2, D_MODEL, D_EXPERT), f32),
                            pltpu.VMEM((2, D_EXPERT, D_MODEL), f32),
                            pltpu.VMEM((D_MODEL, D_EXPERT), bf16), pltpu.VMEM((D_MODEL, D_EXPERT), bf16),
                            pltpu.VMEM((D_EXPERT, D_MODEL), bf16),
                            pltpu.SMEM((1,), jnp.int32), pltpu.SemaphoreType.DMA((2,))]),
        out_shape=jax.ShapeDtypeStruct((MOE_ROWS, D_MODEL), f32),
        compiler_params=pltpu.CompilerParams(dimension_semantics=("arbitrary",), vmem_limit_bytes=VMEM_LIMIT),
        name="moe_experts",
    )(te, xs, w_gate, w_up, w_down)


def _gather_expert_rows(pos0_ref, pos1_ref, ys_ref, buf, sem):
    rows = buf.shape[2]
    i = pl.program_id(0)
    slot = i % 2

    def issue(tile, sl):
        base = tile * rows

        def body(t, carry):
            for s, pos_ref in enumerate((pos0_ref, pos1_ref)):
                pltpu.make_async_copy(ys_ref.at[pl.ds(pos_ref[base + t], 1)], buf.at[sl, s, pl.ds(t, 1)],
                                      sem.at[sl]).start()
            return carry

        lax.fori_loop(0, rows, body, 0, unroll=8)

    @pl.when(i == 0)
    def _():
        issue(0, 0)

    @pl.when(i + 1 < pl.num_programs(0))
    def _():
        issue(i + 1, 1 - slot)

    for s in range(2):
        pltpu.make_async_copy(ys_ref.at[pl.ds(0, rows)], buf.at[slot, s], sem.at[slot]).wait()
    return buf[slot, 0], buf[slot, 1]


def _moe_residual(pos0_ref, pos1_ref, ys_ref, x1_ref, rc_ref, mod_ref, buf, sem):
    y0, y1 = _gather_expert_rows(pos0_ref, pos1_ref, ys_ref, buf, sem)
    rc = rc_ref[...]
    return x1_ref[...] + mod_ref[0][5:6] * (rc[:, 2:3] * y0 + rc[:, 3:4] * y1)


def _final_kernel(pos0_ref, pos1_ref, ys_ref, x1_ref, rc_ref, mod_ref, fn_ref, yc_ref, yl_ref, buf, sem):
    x2 = _moe_residual(pos0_ref, pos1_ref, ys_ref, x1_ref, rc_ref, mod_ref, buf, sem)
    y = x2 * lax.rsqrt(jnp.mean(x2 * x2, axis=-1, keepdims=True) + EPS) * fn_ref[...]
    is_ctx = pl.program_id(0) < N_CTX // x1_ref.shape[0]

    @pl.when(is_ctx)
    def _():
        yc_ref[...] = y

    @pl.when(jnp.logical_not(is_ctx))
    def _():
        yl_ref[...] = y


def _final_combine(pos0, pos1, ys, x1, rc, mods, fn):
    tc = COMBINE_TILE
    row = lambda i, p0, p1: (i, 0)
    n_ctx_tiles = N_CTX // tc
    return pl.pallas_call(
        _final_kernel,
        grid_spec=pltpu.PrefetchScalarGridSpec(
            num_scalar_prefetch=2,
            grid=(N_TOK // tc,),
            in_specs=[pl.BlockSpec(memory_space=pl.ANY),
                      pl.BlockSpec((tc, D_MODEL), row),
                      pl.BlockSpec((tc, LANE), row),
                      pl.BlockSpec((1, 6, D_MODEL), lambda i, p0, p1: (_mod_row(i, tc), 0, 0)),
                      pl.BlockSpec((1, D_MODEL), lambda i, p0, p1: (0, 0))],
            out_specs=[pl.BlockSpec((tc, D_MODEL), lambda i, p0, p1: (jnp.minimum(i, n_ctx_tiles - 1), 0)),
                       pl.BlockSpec((tc, D_MODEL), lambda i, p0, p1: (jnp.maximum(i - n_ctx_tiles, 0), 0))],
            scratch_shapes=[pltpu.VMEM((2, 2, tc, D_MODEL), f32), pltpu.SemaphoreType.DMA((2,))]),
        out_shape=[jax.ShapeDtypeStruct((N_CTX, D_MODEL), f32), jax.ShapeDtypeStruct((N_LAT, D_MODEL), f32)],
        compiler_params=pltpu.CompilerParams(dimension_semantics=("arbitrary",), vmem_limit_bytes=VMEM_LIMIT),
        name="moe_combine_final",
    )(pos0, pos1, ys, x1, rc, mods, fn)


def _moe_experts(h2, route_t, w_gate, w_up, w_down, layer, xs_buf):
    rows, te, runs = _rank(route_t)
    xs = _dispatch(runs[:, :3, :N_EXPERTS].reshape(-1), h2, rows, xs_buf)
    return rows[0], rows[1], _experts(te, xs, w_gate, w_up, w_down, layer), xs


def _pad_heads(w):
    lead = w.shape[:-1]
    w = w.reshape(*lead, ML_HEADS, ML_DIM)
    w = jnp.pad(w, [(0, 0)] * len(lead) + [(0, 0), (0, ML_PAD - ML_DIM)])
    return w.reshape(*lead, ML_PW)


def _pack_in_cols(wb):
    o = 0
    qa = wb[..., o:o + NA_WIDTH] * (NA_DIM ** -0.5)
    ka = wb[..., o + NA_WIDTH:o + 2 * NA_WIDTH]
    va = wb[..., o + 2 * NA_WIDTH:o + 3 * NA_WIDTH]
    o += 3 * NA_WIDTH
    qb, kb, vb, ob = [_pad_heads(wb[..., o + j * ML_WIDTH:o + (j + 1) * ML_WIDTH]) for j in range(4)]
    o += 4 * ML_WIDTH
    gates = wb[..., o:o + N_GATE_COLS]
    o += N_GATE_COLS
    pool = wb[..., o:o + POOL_WIDTH]
    main = jnp.concatenate([qa, ka, va, qb, vb, ob, pool], axis=-1)
    gates_p = jnp.pad(gates, [(0, 0)] * (gates.ndim - 1) + [(0, LANE - N_GATE_COLS)])
    return main, gates_p, jnp.concatenate([kb, gates], axis=-1)


def _pack_w_in(w, b):
    w_main, w_gates, w_feat = _pack_in_cols(w)
    b_main, b_gates, b_feat = _pack_in_cols(b.astype(f32))
    return (w_main.astype(bf16), b_main[:, None], w_gates.astype(bf16), b_gates[:, None],
            jnp.swapaxes(w_feat, 1, 2).astype(bf16), b_feat[:, :, None])


def _pack_w_out(w):
    n_l = w.shape[0]
    wb = w[:, NA_WIDTH:NA_WIDTH + ML_WIDTH].reshape(n_l, ML_HEADS, ML_DIM, D_MODEL)
    wb = jnp.pad(wb, ((0, 0), (0, 0), (0, ML_PAD - ML_DIM), (0, 0))).reshape(n_l, ML_PW, D_MODEL)
    return jnp.concatenate([w[:, :NA_WIDTH], wb, w[:, NA_WIDTH + ML_WIDTH:]], axis=1).astype(bf16)


def _block_diag(w):
    n_l, g, c, _ = w.shape
    eye = jnp.eye(g, dtype=w.dtype)
    return (eye[None, :, None, :, None] * w[:, :, :, None, :]).reshape(n_l, g * c, g * c)


def kernel(x_prompt, x_sample, cache_k_attn, cache_v_attn, state_mlstm_C, state_mlstm_n, state_mlstm_m, c, c_ctx,
           w_ada, b_ada, norm1, w_in, b_in, rpb, ml_norm, w_pool, pool_scale, w_out, norm2, w_router, b_router,
           w_gate, w_up, w_down, final_norm):
    dt = x_prompt.dtype
    x = jnp.concatenate([x_prompt.reshape(N_CTX, D_MODEL), x_sample.reshape(N_LAT, D_MODEL)], axis=0).astype(f32)
    cvec = jnp.concatenate([c_ctx[None], c, jnp.zeros((8 - 1 - DEC_BATCH, D_MODEL), c.dtype)], axis=0).astype(f32)
    mods_all = _ada(cvec, w_ada.astype(f32), b_ada.astype(f32))
    mods_all = mods_all[:, :1 + DEC_BATCH].reshape(DEPTH, 1 + DEC_BATCH, 6, D_MODEL)

    wr_t = w_router.astype(f32).T
    br_t = b_router.astype(f32)[:, None]
    fn = final_norm.astype(f32)[None]

    na_bias = _na_bias_tables(rpb)
    xs_buf = jnp.zeros((MOE_ROWS, D_MODEL), bf16)
    in_params = (norm1.astype(f32)[:, None],) + _pack_w_in(w_in, b_in)
    out_params = (_block_diag(w_pool.astype(f32)).astype(bf16), pool_scale.astype(f32)[:, None],
                  _pad_heads(ml_norm.astype(f32))[:, None], _pack_w_out(w_out), norm2.astype(f32)[:, None])

    ks, vs, Cs, ns, ms = [], [], [], [], []
    pending = None
    for l in range(DEPTH):
        mods = mods_all[l]
        if pending is None:
            qkva, kv32, qvo, kt, gates, gates_t, pin = _in_proj(x, mods, l, *in_params)
        else:
            x, qkva, kv32, qvo, kt, gates, gates_t, pin = _moe_in_proj(*pending, mods_all[l - 1], mods, l, *in_params)
        ks.append(kv32[:N_CTX, :NA_WIDTH].reshape(BATCH, SEQ, NA_HEADS, NA_DIM))
        vs.append(kv32[:N_CTX, NA_WIDTH:].reshape(BATCH, SEQ, NA_HEADS, NA_DIM))

        oa_ctx = _ctx_attention(qkva.reshape(N_TOK // SEQ, SEQ, W_A))
        ck = (cache_k_attn[:, l].reshape(DEC_BATCH, PAST_LEN, NA_WIDTH)).astype(bf16)
        cv = (cache_v_attn[:, l].reshape(DEC_BATCH, PAST_LEN, NA_WIDTH)).astype(bf16)
        oa_lat = _neighborhood_attention(qkva.reshape(N_TOK // DEC_SEQ, DEC_SEQ, W_A), ck, cv, na_bias, l)

        c_l, m_l = _pack_ml_state(state_mlstm_C[:, l], state_mlstm_n[:, l], state_mlstm_m[:, l])
        hf, hb, c_fin, m_fin = _mlstm(qvo, kt, gates, gates_t, c_l, m_l)
        C_l, n_l, m_l2 = _unpack_ml_state(c_fin[:BATCH], m_fin[:BATCH])
        Cs.append(C_l)
        ns.append(n_l)
        ms.append(m_l2)

        x1, h2, route_t, rc = _out_proj(x, mods, l, oa_ctx.reshape(N_CTX, NA_WIDTH), oa_lat.reshape(N_LAT, NA_WIDTH),
                                        hf.reshape(N_TOK, ML_PW), hb.reshape(N_TOK, ML_PW), qvo, pin,
                                        *out_params, wr_t, br_t)
        pos0, pos1, ys, xs_buf = _moe_experts(h2, route_t, w_gate, w_up, w_down, l, xs_buf)
        pending = (pos0, pos1, ys, x1, rc)

    x = _final_combine(*pending, mods_all[DEPTH - 1], fn)
    y_prompt = x[0].reshape(BATCH, SEQ, D_MODEL).astype(dt)
    y_sample = x[1].reshape(DEC_BATCH, DEC_SEQ, D_MODEL).astype(dt)
    return (y_prompt, y_sample,
            jnp.stack(ks, axis=1).astype(dt), jnp.stack(vs, axis=1).astype(dt),
            jnp.stack(Cs, axis=1).astype(dt), jnp.stack(ns, axis=1).astype(dt), jnp.stack(ms, axis=1).astype(dt))
```

```python
import functools

import numpy as np
import jax
import jax.numpy as jnp
from jax import lax
from jax.experimental import pallas as pl
from jax.experimental.pallas import tpu as pltpu

D_MODEL = 1024
BATCH = 16
SEQ = 256
DEPTH = 4
DEC_BATCH = 2
DEC_SEQ = 4096
PAST_LEN = 256
GRID_W = 64
EPS = 1e-6
NEG_INF = -1e30
NA_HEADS = 6
NA_DIM = 64
NA_WIDTH = NA_HEADS * NA_DIM
NA_ROWS = 8
NA_COLS = 16
RPB_ROWS = 2 * NA_ROWS - 1
RPB_COLS = 2 * NA_COLS - 1
ML_HEADS = 4
ML_DIM = 96
ML_WIDTH = ML_HEADS * ML_DIM
POOL_WINDOWS = (2, 4, 8, 16)
POOL_GROUPS = 4
POOL_DIM = 64
POOL_WIDTH = POOL_GROUPS * POOL_DIM
N_GATE_COLS = 4 * ML_HEADS
N_EXPERTS = 16
N_EXPERT_GROUPS = 4
EXPERTS_PER_GROUP = N_EXPERTS // N_EXPERT_GROUPS
D_EXPERT = 512
ADA_DIM = 6 * D_MODEL

N_CTX = BATCH * SEQ
N_LAT = DEC_BATCH * DEC_SEQ
N_TOK = N_CTX + N_LAT
LANE = 128
ML_PAD = LANE
ML_PW = ML_HEADS * ML_PAD
CAUG = ML_PAD
NA_PAIRS = NA_HEADS // 2
TOK_TILE = 512
ML_CHUNK = 256
NA_QROWS = 4
NA_KROWS = NA_QROWS + NA_ROWS - 1
POOL_HALO = max(POOL_WINDOWS) // 2
POOL_BLOCK = 128
MOE_TILE = 512
MOE_CHUNK = 16
MOE_LOCAL_ROWS = -(-(2 * TOK_TILE + N_EXPERTS * (MOE_CHUNK - 1)) // LANE) * LANE
MOE_ROWS = -(-(2 * N_TOK + (N_TOK // TOK_TILE) * N_EXPERTS * (MOE_CHUNK - 1) + N_EXPERTS * (MOE_TILE - 1))
             // MOE_TILE) * MOE_TILE
COMBINE_TILE = 256
VMEM_LIMIT = 56 * 1024 * 1024

W_A = 3 * NA_WIDTH
W_B = 3 * ML_PW
N_TCOLS = ML_PW + N_GATE_COLS
W_MAIN = W_A + W_B + POOL_WIDTH

f32 = jnp.float32
bf16 = jnp.bfloat16
HI = lax.Precision.HIGHEST


def _nt(a, b, **kw):
    return lax.dot_general(a, b, (((1,), (1,)), ((), ())), preferred_element_type=f32, **kw)


def _mod_row(i, tile):
    n_ctx_tiles = N_CTX // tile
    per_batch = DEC_SEQ // tile
    return jnp.where(i < n_ctx_tiles, 0, 1 + (i - n_ctx_tiles) // per_batch)


def _ada_kernel(c_ref, w_ref, b_ref, o_ref):
    s = c_ref[...]
    s = s * jax.nn.sigmoid(s)
    o_ref[0] = jnp.dot(s.astype(bf16), w_ref[0].astype(bf16), preferred_element_type=f32) + b_ref[0]


def _ada(cvec, w_ada, b_ada):
    nj = ADA_DIM // D_MODEL
    return pl.pallas_call(
        _ada_kernel,
        grid=(DEPTH, nj),
        in_specs=[pl.BlockSpec((8, D_MODEL), lambda l, j: (0, 0)),
                  pl.BlockSpec((1, D_MODEL, D_MODEL), lambda l, j: (l, 0, j)),
                  pl.BlockSpec((1, 1, D_MODEL), lambda l, j: (l, 0, j))],
        out_specs=pl.BlockSpec((1, 8, D_MODEL), lambda l, j: (l, 0, j)),
        out_shape=jax.ShapeDtypeStruct((DEPTH, 8, ADA_DIM), f32),
        name="ada_mod",
    )(cvec, w_ada, b_ada.reshape(DEPTH, 1, ADA_DIM))


def _in_kernel(xc_ref, xl_ref, *refs):
    is_ctx = pl.program_id(0) < N_CTX // xc_ref.shape[0]
    _in_body(jnp.where(is_ctx, xc_ref[...], xl_ref[...]), *refs)


def _moe_in_kernel(pos0_ref, pos1_ref, ys_ref, x1_ref, rc_ref, mod_prev_ref, *refs):
    in_refs, x_out_ref, out_refs, (buf, sem) = refs[:9], refs[9], refs[10:-2], refs[-2:]
    x = _moe_residual(pos0_ref, pos1_ref, ys_ref, x1_ref, rc_ref, mod_prev_ref, buf, sem)
    x_out_ref[...] = x
    _in_body(x, *in_refs, *out_refs)


def _in_body(x, mod_ref, n1_ref, w_ref, b_ref, wg_ref, bg_ref, wt_ref, bt_ref, kv_in_ref,
             a_ref, kv_ref, b_out_ref, kt_ref, g_ref, gt_ref, pin_ref):
    del kv_in_ref
    mod = mod_ref[0]
    h = x * lax.rsqrt(jnp.mean(x * x, axis=-1, keepdims=True) + EPS) * n1_ref[...]
    h = (h * (1.0 + mod[1:2]) + mod[0:1]).astype(bf16)
    pa = jnp.dot(h, w_ref[:, 0:W_A], preferred_element_type=f32) + b_ref[:, 0:W_A]
    a_ref[...] = pa.astype(bf16)

    @pl.when(pl.program_id(0) < N_CTX // x.shape[0])
    def _():
        kv_ref[...] = pa[:, NA_WIDTH:W_A]

    for j in range(3):
        lo = W_A + j * ML_PW
        pb = jnp.dot(h, w_ref[:, lo:lo + ML_PW], preferred_element_type=f32) + b_ref[:, lo:lo + ML_PW]
        if j == 0:
            pb = pb * (ML_DIM ** -0.5)
        b_out_ref[:, j * ML_PW:(j + 1) * ML_PW] = pb.astype(bf16)
    lo = W_A + W_B
    pin_ref[...] = jnp.dot(h, w_ref[:, lo:lo + POOL_WIDTH], preferred_element_type=f32) + b_ref[:, lo:lo + POOL_WIDTH]
    g_ref[...] = jnp.dot(h, wg_ref[...], preferred_element_type=f32) + bg_ref[...]
    t = _nt(wt_ref[...], h) + bt_ref[...]
    kt_ref[...] = t[0:ML_PW].astype(bf16)
    gt_ref[...] = t[ML_PW:N_TCOLS]


def _in_proj_specs(layer):
    tm = TOK_TILE
    lyr = lambda shape: pl.BlockSpec((None,) + shape, lambda i, *_: (layer, 0, 0))
    rows = lambda width: pl.BlockSpec((tm, width), lambda i, *_: (i, 0))
    cols = lambda height: pl.BlockSpec((height, tm), lambda i, *_: (0, i))
    param_specs = [pl.BlockSpec((1, 6, D_MODEL), lambda i, *_: (_mod_row(i, tm), 0, 0)),
                   lyr((1, D_MODEL)), lyr((D_MODEL, W_MAIN)), lyr((1, W_MAIN)), lyr((D_MODEL, LANE)), lyr((1, LANE)),
                   lyr((N_TCOLS, D_MODEL)), lyr((N_TCOLS, 1)), pl.BlockSpec(memory_space=pl.ANY)]
    kv_spec = pl.BlockSpec((None, tm, 2 * NA_WIDTH), lambda i, *_: (layer, jnp.minimum(i, N_CTX // tm - 1), 0))
    out_specs = [rows(W_A), kv_spec, rows(W_B), cols(ML_PW), rows(LANE), cols(N_GATE_COLS), rows(POOL_WIDTH)]
    out_shape = [jax.ShapeDtypeStruct((N_TOK, W_A), bf16),
                 jax.ShapeDtypeStruct((DEPTH, N_CTX, 2 * NA_WIDTH), f32),
                 jax.ShapeDtypeStruct((N_TOK, W_B), bf16),
                 jax.ShapeDtypeStruct((ML_PW, N_TOK), bf16),
                 jax.ShapeDtypeStruct((N_TOK, LANE), f32),
                 jax.ShapeDtypeStruct((N_GATE_COLS, N_TOK), f32),
                 jax.ShapeDtypeStruct((N_TOK, POOL_WIDTH), f32)]
    return rows, param_specs, out_specs, out_shape


def _in_proj(x_ctx, x_lat, mods, layer, *params):
    tm = TOK_TILE
    n_ctx_tiles = N_CTX // tm
    rows, param_specs, out_specs, out_shape = _in_proj_specs(layer)
    return pl.pallas_call(
        _in_kernel,
        grid=(N_TOK // tm,),
        in_specs=[pl.BlockSpec((tm, D_MODEL), lambda i: (jnp.minimum(i, n_ctx_tiles - 1), 0)),
                  pl.BlockSpec((tm, D_MODEL), lambda i: (jnp.maximum(i - n_ctx_tiles, 0), 0))] + param_specs,
        out_specs=out_specs,
        out_shape=out_shape,
        input_output_aliases={2 + len(param_specs) - 1: 1},
        compiler_params=pltpu.CompilerParams(dimension_semantics=("arbitrary",), vmem_limit_bytes=VMEM_LIMIT),
        name="in_proj",
    )(x_ctx, x_lat, mods, *params)


def _moe_in_proj(pos0, pos1, ys, x1, rc, mods_prev, mods, layer, *params):
    tm = TOK_TILE
    rows, param_specs, out_specs, out_shape = _in_proj_specs(layer)
    return pl.pallas_call(
        _moe_in_kernel,
        grid_spec=pltpu.PrefetchScalarGridSpec(
            num_scalar_prefetch=2,
            grid=(N_TOK // tm,),
            in_specs=[pl.BlockSpec(memory_space=pl.ANY), rows(D_MODEL), rows(LANE),
                      pl.BlockSpec((1, 6, D_MODEL), lambda i, *_: (_mod_row(i, tm), 0, 0))] + param_specs,
            out_specs=[rows(D_MODEL)] + out_specs,
            scratch_shapes=[pltpu.VMEM((2, 2, tm, D_MODEL), f32), pltpu.SemaphoreType.DMA((2,))]),
        out_shape=[jax.ShapeDtypeStruct((N_TOK, D_MODEL), f32)] + out_shape,
        input_output_aliases={6 + len(param_specs) - 1: 2},
        compiler_params=pltpu.CompilerParams(dimension_semantics=("arbitrary",), vmem_limit_bytes=VMEM_LIMIT),
        name="moe_combine_in_proj",
    )(pos0, pos1, ys, x1, rc, mods_prev, mods, *params)


def _pair_attention(qp, parts):
    lane = lax.broadcasted_iota(jnp.int32, (1, LANE), 1)
    outs = []
    for j in range(2):
        in_half = (lane >= j * NA_DIM) & (lane < (j + 1) * NA_DIM)
        qm = jnp.where(in_half, qp, jnp.zeros_like(qp))
        scores = []
        for k, _, bias in parts:
            s = _nt(qm, k)
            if bias is not None:
                s = s + bias[j]
            scores.append(s)
        m = scores[0].max(axis=-1, keepdims=True)
        for s in scores[1:]:
            m = jnp.maximum(m, s.max(axis=-1, keepdims=True))
        den = None
        acc = None
        for s, (_, v, _) in zip(scores, parts):
            p = jnp.exp(s - m)
            ps = p.sum(axis=-1, keepdims=True)
            den = ps if den is None else den + ps
            o = jnp.dot(p.astype(bf16), v, preferred_element_type=f32)
            acc = o if acc is None else acc + o
        outs.append(acc / den)
    return jnp.where(lane < NA_DIM, outs[0], outs[1])


def _ctx_attn_kernel(q_ref, k_ref, v_ref, o_ref):
    for p in range(NA_PAIRS):
        sl = slice(p * LANE, (p + 1) * LANE)
        o = _pair_attention(q_ref[0, :, sl], [(k_ref[0, :, sl], v_ref[0, :, sl], None)])
        o_ref[0, :, sl] = o.astype(bf16)


def _ctx_attention(qkv):
    blk = lambda c: pl.BlockSpec((1, SEQ, NA_WIDTH), lambda b, c=c: (b, 0, c))
    return pl.pallas_call(
        _ctx_attn_kernel,
        grid=(BATCH,),
        in_specs=[blk(0), blk(1), blk(2)],
        out_specs=pl.BlockSpec((1, SEQ, NA_WIDTH), lambda b: (b, 0, 0)),
        out_shape=jax.ShapeDtypeStruct((BATCH, SEQ, NA_WIDTH), bf16),
        name="ctx_attention",
    )(qkv, qkv, qkv)


def _na_window_start(rb):
    return jnp.clip(rb * NA_QROWS - NA_ROWS // 2, 0, DEC_SEQ // GRID_W - NA_KROWS)


def _na_bias(tab_ref, head, rb):
    rows = DEC_SEQ // GRID_W
    ws = _na_window_start(rb)
    lane = lax.broadcasted_iota(jnp.int32, (1, NA_KROWS * GRID_W), 1)
    per_qrow = []
    for dq in range(NA_QROWS):
        qr = rb * NA_QROWS + dq
        a0 = ws - qr + (NA_ROWS - 1) + NA_KROWS
        tiles = [tab_ref[head, a0 + 2 * j] for j in range((NA_KROWS + 1) // 2)]
        t = jnp.concatenate(tiles, axis=1)[:, :NA_KROWS * GRID_W]
        lo = (jnp.clip(qr - NA_ROWS // 2, 0, rows - NA_ROWS) - ws) * GRID_W
        ok = (lane >= lo) & (lane < lo + NA_ROWS * GRID_W)
        per_qrow.append(jnp.where(ok, t, NEG_INF))
    return jnp.concatenate(per_qrow, axis=0)


def _na_kernel(q_ref, k_ref, v_ref, ck_ref, cv_ref, tab_ref, o_ref):
    rb = pl.program_id(1)
    start = pl.multiple_of(_na_window_start(rb) * GRID_W, GRID_W)
    nk = NA_KROWS * GRID_W
    for p in range(NA_PAIRS):
        sl = slice(p * LANE, (p + 1) * LANE)
        bias = [_na_bias(tab_ref.at[0], 2 * p + j, rb) for j in range(2)]
        parts = [(k_ref[0, pl.ds(start, nk), sl], v_ref[0, pl.ds(start, nk), sl], bias),
                 (ck_ref[0, :, sl], cv_ref[0, :, sl], None)]
        o = _pair_attention(q_ref[0, :, sl], parts)
        o_ref[0, :, sl] = o.astype(bf16)


def _neighborhood_attention(qkv, ck, cv, tables, layer):
    nq = NA_QROWS * GRID_W
    n_rb = DEC_SEQ // nq
    return pl.pallas_call(
        _na_kernel,
        grid=(DEC_BATCH, n_rb),
        in_specs=[pl.BlockSpec((1, nq, NA_WIDTH), lambda b, r: (1 + b, r, 0)),
                  pl.BlockSpec((1, DEC_SEQ, NA_WIDTH), lambda b, r: (1 + b, 0, 1)),
                  pl.BlockSpec((1, DEC_SEQ, NA_WIDTH), lambda b, r: (1 + b, 0, 2)),
                  pl.BlockSpec((1, PAST_LEN, NA_WIDTH), lambda b, r: (b, 0, 0)),
                  pl.BlockSpec((1, PAST_LEN, NA_WIDTH), lambda b, r: (b, 0, 0)),
                  pl.BlockSpec((1,) + tables.shape[1:], lambda b, r: (layer, 0, 0, 0, 0))],
        out_specs=pl.BlockSpec((1, nq, NA_WIDTH), lambda b, r: (b, r, 0)),
        out_shape=jax.ShapeDtypeStruct((DEC_BATCH, DEC_SEQ, NA_WIDTH), bf16),
        compiler_params=pltpu.CompilerParams(dimension_semantics=("arbitrary", "arbitrary"),
                                             vmem_limit_bytes=VMEM_LIMIT),
        name="neighborhood_attention",
    )(qkv, qkv, qkv, ck, cv, tables)


def _na_bias_tables(rpb):
    qc = np.arange(GRID_W)[:, None]
    kc = np.arange(GRID_W)[None, :]
    dc = np.clip(kc - qc + NA_COLS - 1, 0, RPB_COLS - 1)
    col_start = np.clip(qc - NA_COLS // 2, 0, GRID_W - NA_COLS)
    col_ok = (kc >= col_start) & (kc < col_start + NA_COLS)
    pick_col = (dc[None] == np.arange(RPB_COLS)[:, None, None]).astype(np.float32)
    rpb_pad = jnp.pad(rpb.astype(f32), ((0, 0), (0, 0), (NA_KROWS, NA_KROWS + 1), (0, 0)))
    row_ok = (np.arange(rpb_pad.shape[2]) >= NA_KROWS) & (np.arange(rpb_pad.shape[2]) < NA_KROWS + RPB_ROWS)
    by_col = jnp.einsum('lhab,bqk->lhaqk', rpb_pad, pick_col, precision=HI)
    by_col = jnp.where(row_ok[None, None, :, None, None] & col_ok[None, None, None], by_col, NEG_INF)
    return jnp.concatenate([by_col[:, :, :-1], by_col[:, :, 1:]], axis=-1)


def _log_sigmoid(x):
    return -(jnp.maximum(-x, 0.0) + jnp.log(1.0 + jnp.exp(-jnp.abs(x))))


def _split3(x):
    hi = x.astype(bf16)
    r1 = x - hi.astype(f32)
    mid = r1.astype(bf16)
    lo = (r1 - mid.astype(f32)).astype(bf16)
    return hi, mid, lo


def _mlstm_kernel(qf_ref, vf_ref, ktf_ref, gf_ref, gtf_ref, qb_ref, vb_ref, ktb_ref, gb_ref, gtb_ref,
                  c0_ref, m0_ref, hf_ref, hb_ref, c_out_ref, m_out_ref, c_scr, m_scr):
    L = ML_CHUNK
    seq, c, n_chunks, _ = _ml_schedule(pl.program_id(0))

    @pl.when(c == 0)
    def _():
        is_ctx = seq < BATCH
        c_scr[...] = jnp.where(is_ctx, 0.0, c0_ref[0])
        m_scr[...] = jnp.where(is_ctx, 0.0, m0_ref[0])

    ri = lax.broadcasted_iota(jnp.int32, (L, L), 0)
    ci = lax.broadcasted_iota(jnp.int32, (L, L), 1)
    lane = lax.broadcasted_iota(jnp.int32, (L, ML_PAD), 1)
    is_ncol = lane == ML_DIM
    lower = ri >= ci
    upper = ri <= ci
    lower_b = jnp.where(lower, 1.0, 0.0).astype(bf16)
    upper_b = jnp.where(upper, 1.0, 0.0).astype(bf16)
    dirs = ((qf_ref, ktf_ref, vf_ref, gf_ref, gtf_ref, hf_ref), (qb_ref, ktb_ref, vb_ref, gb_ref, gtb_ref, hb_ref))
    for d, (q_ref, kt_ref, v_ref, g_ref, gt_ref, h_ref) in enumerate(dirs):
        g = g_ref[...][:, 0:N_GATE_COLS]
        gt = gt_ref[...]
        lf_c = _log_sigmoid(g)
        lf_r = _log_sigmoid(gt)
        b_cols = sum(jnp.dot(lower_b, part, preferred_element_type=f32) for part in _split3(lf_c))
        b_rows = sum(jnp.dot(part, upper_b, preferred_element_type=f32) for part in _split3(lf_r))
        tot_c = jnp.sum(lf_c, axis=0, keepdims=True)
        tot_r = jnp.sum(lf_r, axis=1, keepdims=True)
        visible = lower
        if d == 1:
            b_cols = tot_c - b_cols + lf_c
            b_rows = tot_r - b_rows + lf_r
            visible = upper
        for hd in range(ML_HEADS):
            st = d * ML_HEADS + hd
            ci_ = 2 * ML_HEADS * d + hd
            cf_ = ci_ + ML_HEADS
            sl = slice(hd * ML_PAD, (hd + 1) * ML_PAD)
            bc = b_cols[:, cf_:cf_ + 1]
            br = b_rows[cf_:cf_ + 1, :]
            li_r = gt[ci_:ci_ + 1, :]
            m_prev = m_scr[st:st + 1, 0:1]
            dmat = jnp.where(visible, bc - br + li_r, NEG_INF)
            inter = bc + m_prev
            m_t = jnp.maximum(inter, dmat.max(axis=-1, keepdims=True))
            w_intra = jnp.exp(dmat - m_t)
            w_inter = jnp.exp(inter - m_t)
            qh = q_ref[0, :, sl]
            kht = kt_ref[sl, :]
            v_aug = jnp.where(is_ncol, jnp.ones((), bf16), v_ref[0, :, sl])
            s = (jnp.dot(qh, kht, preferred_element_type=f32) * w_intra).astype(bf16)
            c_aug = c_scr[st]
            na = (w_inter * jnp.dot(qh, c_aug.astype(bf16), preferred_element_type=f32)
                  + jnp.dot(s, v_aug, preferred_element_type=f32))
            den = na[:, ML_DIM:ML_DIM + 1]
            h_ref[0, :, sl] = jnp.where(lane < ML_DIM, na / jnp.maximum(jnp.abs(den), jnp.exp(-m_t)), 0.0)
            b_end = tot_r[cf_:cf_ + 1, :]
            g_row = b_end - br + li_r
            m_new = jnp.maximum(b_end + m_prev, g_row.max(axis=1, keepdims=True))
            decay = jnp.exp(b_end + m_prev - m_new)
            kwt = (kht.astype(f32) * jnp.exp(g_row - m_new)).astype(bf16)
            c_scr[st] = decay * c_aug + jnp.dot(kwt, v_aug, preferred_element_type=f32)
            m_scr[st:st + 1, :] = jnp.broadcast_to(m_new, (1, LANE))

    @pl.when(c == n_chunks - 1)
    def _():
        c_out_ref[0] = c_scr[...]
        m_out_ref[0] = m_scr[...]


def _ml_schedule(s):
    nc_ctx, nc_lat = SEQ // ML_CHUNK, DEC_SEQ // ML_CHUNK
    n_ctx_steps = BATCH * nc_ctx
    is_ctx = s < n_ctx_steps
    t = s - n_ctx_steps
    seq = jnp.where(is_ctx, s // nc_ctx, BATCH + t // nc_lat)
    c = jnp.where(is_ctx, s % nc_ctx, t % nc_lat)
    nc = jnp.where(is_ctx, nc_ctx, nc_lat)
    base = jnp.where(is_ctx, (s // nc_ctx) * nc_ctx, n_ctx_steps + (t // nc_lat) * nc_lat)
    return seq, c, nc, base


def _mlstm(qvo, kt, gates, gates_t, c0, m0):
    L = ML_CHUNK
    n_seq = BATCH + DEC_BATCH

    def fwd(s):
        _, c, _, base = _ml_schedule(s)
        return base + c

    def bwd(s):
        _, c, nc, base = _ml_schedule(s)
        return base + nc - 1 - c

    seq_of = lambda s: _ml_schedule(s)[0]
    lat_of = lambda s: jnp.maximum(seq_of(s) - BATCH, 0)

    def specs(pos):
        return [pl.BlockSpec((1, L, ML_PW), lambda s, j=j: (pos(s), 0, j)) for j in range(2)] + [
            pl.BlockSpec((ML_PW, L), lambda s: (0, pos(s))),
            pl.BlockSpec((L, LANE), lambda s: (pos(s), 0)),
            pl.BlockSpec((N_GATE_COLS, L), lambda s: (0, pos(s)))]

    q3 = qvo.reshape(N_TOK // L, L, W_B)
    n_str = 2 * ML_HEADS
    return pl.pallas_call(
        _mlstm_kernel,
        grid=(N_TOK // L,),
        in_specs=specs(fwd) + specs(bwd) + [
            pl.BlockSpec((1, n_str, ML_PAD, CAUG), lambda s: (lat_of(s), 0, 0, 0)),
            pl.BlockSpec((1, n_str, LANE), lambda s: (lat_of(s), 0, 0))],
        out_specs=[pl.BlockSpec((1, L, ML_PW), lambda s: (fwd(s), 0, 0)),
                   pl.BlockSpec((1, L, ML_PW), lambda s: (bwd(s), 0, 0)),
                   pl.BlockSpec((1, n_str, ML_PAD, CAUG), lambda s: (seq_of(s), 0, 0, 0)),
                   pl.BlockSpec((1, n_str, LANE), lambda s: (seq_of(s), 0, 0))],
        out_shape=[jax.ShapeDtypeStruct((N_TOK // L, L, ML_PW), f32),
                   jax.ShapeDtypeStruct((N_TOK // L, L, ML_PW), f32),
                   jax.ShapeDtypeStruct((n_seq, n_str, ML_PAD, CAUG), f32),
                   jax.ShapeDtypeStruct((n_seq, n_str, LANE), f32)],
        scratch_shapes=[pltpu.VMEM((n_str, ML_PAD, CAUG), f32), pltpu.VMEM((n_str, LANE), f32)],
        compiler_params=pltpu.CompilerParams(dimension_semantics=("arbitrary",), vmem_limit_bytes=VMEM_LIMIT),
        name="mlstm",
    )(q3, q3, kt, gates, gates_t, q3, q3, kt, gates, gates_t, c0, m0)


def _pack_ml_state(C, n, m):
    B = C.shape[0]
    c_aug = jnp.zeros((B, 2, ML_HEADS, ML_PAD, CAUG), f32)
    c_aug = c_aug.at[:, :, :, :ML_DIM, :ML_DIM].set(C.astype(f32))
    c_aug = c_aug.at[:, :, :, :ML_DIM, ML_DIM].set(n.astype(f32))
    m_b = jnp.broadcast_to(m.astype(f32)[..., None], (B, 2, ML_HEADS, LANE))
    return c_aug.reshape(B, 2 * ML_HEADS, ML_PAD, CAUG), m_b.reshape(B, 2 * ML_HEADS, LANE)


def _unpack_ml_state(c_aug, m_b):
    B = c_aug.shape[0]
    c_aug = c_aug.reshape(B, 2, ML_HEADS, ML_PAD, CAUG)
    return (c_aug[:, :, :, :ML_DIM, :ML_DIM], c_aug[:, :, :, :ML_DIM, ML_DIM],
            m_b.reshape(B, 2, ML_HEADS, LANE)[..., 0])


def _pool_rows(u_prev, u_cur, u_next, w_bd, scale, t0, seq_len):
    tm = u_cur.shape[0]
    u_win = jnp.concatenate([u_prev, u_cur, u_next], axis=0)
    u_hi = u_win.astype(bf16)
    u_lo = (u_win - u_hi.astype(f32)).astype(bf16)
    lane = lax.broadcasted_iota(jnp.int32, (1, LANE), 1)
    blocks = []
    for r0 in range(0, tm, POOL_BLOCK):
        win = slice(r0, r0 + POOL_BLOCK + 2 * POOL_HALO)
        t_abs = t0 + r0 + lax.broadcasted_iota(jnp.int32, (POOL_BLOCK, 1), 0)
        s_abs = t0 + r0 - POOL_HALO + lax.broadcasted_iota(jnp.int32, (1, POOL_BLOCK + 2 * POOL_HALO), 1)
        t_loc = t_abs & (seq_len - 1)
        seq_start = t_abs - t_loc
        means = []
        for w in POOL_WINDOWS:
            lo = jnp.maximum(t_loc - w // 2, 0)
            hi = jnp.minimum(t_loc - w // 2 + w, seq_len)
            in_win = (s_abs >= seq_start + lo) & (s_abs < seq_start + hi)
            means.append((jnp.where(in_win, 1.0, 0.0).astype(bf16), 1.0 / (hi - lo).astype(f32)))
        pooled = []
        for p in range(POOL_GROUPS // 2):
            sl = slice(p * LANE, (p + 1) * LANE)
            halves = []
            for a, inv_cnt in means[2 * p:2 * p + 2]:
                tot = (jnp.dot(a, u_hi[win, sl], preferred_element_type=f32)
                       + jnp.dot(a, u_lo[win, sl], preferred_element_type=f32))
                halves.append(tot * inv_cnt)
            pooled.append(jnp.where(lane < POOL_DIM, halves[0], halves[1]) - u_cur[r0:r0 + POOL_BLOCK, sl])
        blocks.append(jnp.concatenate(pooled, axis=1))
    pooled = jnp.concatenate(blocks, axis=0).astype(bf16)
    return jnp.dot(pooled, w_bd, preferred_element_type=f32) * scale


def _top2_sum(a, b, c, d):
    hi1, lo1 = jnp.maximum(a, b), jnp.minimum(a, b)
    hi2, lo2 = jnp.maximum(c, d), jnp.minimum(c, d)
    return jnp.maximum(hi1, hi2) + jnp.maximum(jnp.minimum(hi1, hi2), jnp.maximum(lo1, lo2))


def _first_match(vals, target):
    idx = jnp.full_like(target, float(len(vals) - 1))
    for i in range(len(vals) - 2, -1, -1):
        idx = jnp.where(vals[i] == target, float(i), idx)
    return idx


def _pick(vals, idx):
    out = vals[-1]
    for i in range(len(vals) - 2, -1, -1):
        out = jnp.where(idx == float(i), vals[i], out)
    return out


def _route(logits_t, bias_t):
    scores = jax.nn.sigmoid(logits_t)
    sel = scores + bias_t
    row = lambda a, i: a[i:i + 1, :]
    grp = [_top2_sum(*[row(sel, EXPERTS_PER_GROUP * g + i) for i in range(EXPERTS_PER_GROUP)])
           for g in range(N_EXPERT_GROUPS)]
    best = functools.reduce(jnp.maximum, grp)
    gidx = _first_match(grp, best)
    sel_g = [_pick([row(sel, EXPERTS_PER_GROUP * g + i) for g in range(N_EXPERT_GROUPS)], gidx)
             for i in range(EXPERTS_PER_GROUP)]
    sco_g = [_pick([row(scores, EXPERTS_PER_GROUP * g + i) for g in range(N_EXPERT_GROUPS)], gidx)
             for i in range(EXPERTS_PER_GROUP)]
    i0 = _first_match(sel_g, functools.reduce(jnp.maximum, sel_g))
    rest = [jnp.where(i0 == float(i), -jnp.inf, sel_g[i]) for i in range(EXPERTS_PER_GROUP)]
    i1 = _first_match(rest, functools.reduce(jnp.maximum, rest))
    s0, s1 = _pick(sco_g, i0), _pick(sco_g, i1)
    tot = s0 + s1
    rid = lax.broadcasted_iota(jnp.int32, (LANE, logits_t.shape[1]), 0)
    rows = (EXPERTS_PER_GROUP * gidx + i0, EXPERTS_PER_GROUP * gidx + i1, s0 / tot, s1 / tot)
    out = jnp.zeros(rid.shape, f32)
    for i, r in enumerate(rows):
        out = jnp.where(rid == i, r, out)
    return out


def _out_kernel(xc_ref, xl_ref, mod_ref, oac_ref, oal_ref, hf_ref, hb_ref, ob_ref, up_ref, uc_ref, un_ref, wp_ref, psc_ref,
                mln_ref, wo_ref, n2_ref, wr_ref, br_ref, x1_ref, h2_ref, rt_ref, rc_ref):
    tm = xc_ref.shape[0]
    i = pl.program_id(0)
    is_ctx = i < N_CTX // tm
    mod = mod_ref[0]
    out_a = jnp.where(is_ctx, oac_ref[...], oal_ref[...])
    out_c = _pool_rows(up_ref[...], uc_ref[...], un_ref[...], wp_ref[...], psc_ref[...], i * tm,
                       jnp.where(is_ctx, SEQ, DEC_SEQ)).astype(bf16)
    hsum = hf_ref[...] + hb_ref[...]
    outs_b = []
    for hd in range(ML_HEADS):
        sl = slice(hd * ML_PAD, (hd + 1) * ML_PAD)
        hh = hsum[:, sl]
        ms = jnp.sum(hh * hh, axis=-1, keepdims=True) * (1.0 / ML_DIM)
        hn = hh * lax.rsqrt(ms + EPS) * mln_ref[:, sl]
        outs_b.append((jax.nn.sigmoid(ob_ref[:, sl].astype(f32)) * hn).astype(bf16))
    out_b = jnp.concatenate(outs_b, axis=1)
    mixed = (jnp.dot(out_a, wo_ref[0:NA_WIDTH, :], preferred_element_type=f32)
             + jnp.dot(out_b, wo_ref[NA_WIDTH:NA_WIDTH + ML_PW, :], preferred_element_type=f32)
             + jnp.dot(out_c, wo_ref[NA_WIDTH + ML_PW:, :], preferred_element_type=f32))
    x1 = jnp.where(is_ctx, xc_ref[...], xl_ref[...]) + mod[2:3] * mixed
    x1_ref[...] = x1
    h2 = x1 * lax.rsqrt(jnp.mean(x1 * x1, axis=-1, keepdims=True) + EPS) * n2_ref[...]
    h2 = h2 * (1.0 + mod[4:5]) + mod[3:4]
    h2_ref[...] = h2.astype(bf16)
    route_t = _route(_nt(wr_ref[...], h2, precision=HI), br_ref[...])
    rt_ref[...] = route_t[0:8]
    rc_ref[...] = route_t.T


def _out_proj(x_ctx, x_lat, x_lat_block0, mods, layer, oa_ctx, oa_lat, hf, hb, qvo, pin, w_bd, psc, mln, wo, n2,
              wr_t, br_t):
    tm = TOK_TILE
    const = lambda i: (0, 0)
    lyr = lambda shape: pl.BlockSpec((None,) + shape, lambda i: (layer, 0, 0))
    row = lambda i: (i, 0)
    n_ctx_tiles = N_CTX // tm
    halo_blocks = tm // POOL_HALO
    return pl.pallas_call(
        _out_kernel,
        grid=(N_TOK // tm,),
        in_specs=[pl.BlockSpec((tm, D_MODEL), lambda i: (jnp.minimum(i, n_ctx_tiles - 1), 0)),
                  pl.BlockSpec((tm, D_MODEL), lambda i: (jnp.maximum(i - n_ctx_tiles, 0) + x_lat_block0, 0)),
                  pl.BlockSpec((1, 6, D_MODEL), lambda i: (_mod_row(i, tm), 0, 0)),
                  pl.BlockSpec((tm, NA_WIDTH), lambda i: (jnp.minimum(i, n_ctx_tiles - 1), 0)),
                  pl.BlockSpec((tm, NA_WIDTH), lambda i: (jnp.maximum(i - n_ctx_tiles, 0), 0)),
                  pl.BlockSpec((tm, ML_PW), row),
                  pl.BlockSpec((tm, ML_PW), row),
                  pl.BlockSpec((tm, ML_PW), lambda i: (i, 2)),
                  pl.BlockSpec((POOL_HALO, POOL_WIDTH), lambda i: (jnp.maximum(i * halo_blocks - 1, 0), 0)),
                  pl.BlockSpec((tm, POOL_WIDTH), row),
                  pl.BlockSpec((POOL_HALO, POOL_WIDTH),
                               lambda i: (jnp.minimum((i + 1) * halo_blocks, N_TOK // POOL_HALO - 1), 0)),
                  lyr((POOL_WIDTH, POOL_WIDTH)), lyr((1, POOL_WIDTH)), lyr((1, ML_PW)),
                  lyr((NA_WIDTH + ML_PW + POOL_WIDTH, D_MODEL)), lyr((1, D_MODEL)),
                  pl.BlockSpec((N_EXPERTS, D_MODEL), const),
                  pl.BlockSpec((N_EXPERTS, 1), const)],
        out_specs=[pl.BlockSpec((tm, D_MODEL), row),
                   pl.BlockSpec((tm, D_MODEL), row),
                   pl.BlockSpec((8, tm), lambda i: (0, i)),
                   pl.BlockSpec((tm, LANE), row)],
        out_shape=[jax.ShapeDtypeStruct((N_TOK, D_MODEL), f32),
                   jax.ShapeDtypeStruct((N_TOK, D_MODEL), bf16),
                   jax.ShapeDtypeStruct((8, N_TOK), f32),
                   jax.ShapeDtypeStruct((N_TOK, LANE), f32)],
        compiler_params=pltpu.CompilerParams(dimension_semantics=("arbitrary",), vmem_limit_bytes=VMEM_LIMIT),
        name="out_proj_router",
    )(x_ctx, x_lat, mods, oa_ctx, oa_lat, hf, hb, qvo, pin, pin, pin, w_bd, psc, mln, wo, n2, wr_t, br_t)


def _ceil_to(x, m):
    return jnp.floor((x + (m - 1)) * (1.0 / m)) * m


def _prefix_over_experts(v):
    er = lax.broadcasted_iota(jnp.int32, (N_EXPERTS, N_EXPERTS), 0)
    ec = lax.broadcasted_iota(jnp.int32, (N_EXPERTS, N_EXPERTS), 1)
    return jnp.dot(jnp.where(ec < er, 1.0, 0.0), v, preferred_element_type=f32, precision=HI)


def _experts_to_lanes(v):
    sub = lax.broadcasted_iota(jnp.int32, (N_EXPERTS, LANE), 0)
    lane = lax.broadcasted_iota(jnp.int32, (N_EXPERTS, LANE), 1)
    return jnp.sum(jnp.where(sub == lane, v, 0.0), axis=0, keepdims=True)


def _rank_kernel(rt_ref, pos_ref, te_ref, tab_ref, carry_ref):
    tm = rt_ref.shape[1]
    p = pl.program_id(0)
    i = pl.program_id(1)
    rid = lax.broadcasted_iota(jnp.int32, (N_EXPERTS, tm), 0).astype(f32)
    oh0 = rid == rt_ref[0:1, :]
    oh1 = rid == rt_ref[1:2, :]
    both = jnp.where(oh0 | oh1, 1.0, 0.0)
    runs = jnp.broadcast_to(_ceil_to(jnp.sum(both, axis=1, keepdims=True), MOE_CHUNK), (N_EXPERTS, LANE))

    @pl.when((p == 0) & (i == 0))
    def _():
        carry_ref[...] = jnp.zeros_like(carry_ref)

    @pl.when((p == 1) & (i == 0))
    def _():
        padded = _ceil_to(carry_ref[...], MOE_TILE)
        off = _prefix_over_experts(padded)
        carry_ref[...] = off
        total = jnp.sum(padded, axis=0, keepdims=True)
        n_used = total * (1.0 / MOE_TILE)
        tile = lax.broadcasted_iota(jnp.int32, (1, LANE), 1).astype(f32)
        row0 = jnp.minimum(tile, n_used - 1.0) * MOE_TILE
        expert = jnp.sum(jnp.where(off <= row0, 1.0, 0.0), axis=0, keepdims=True) - 1.0
        sub = lax.broadcasted_iota(jnp.int32, (8, LANE), 0)
        te_ref[...] = jnp.where(sub == 0, expert, jnp.where(sub == 1, n_used, 0.0)).astype(jnp.int32)

    @pl.when(p == 1)
    def _():
        sr = lax.broadcasted_iota(jnp.int32, (tm, tm), 0)
        sc = lax.broadcasted_iota(jnp.int32, (tm, tm), 1)
        earlier = jnp.dot(both.astype(bf16), jnp.where(sr < sc, 1.0, 0.0).astype(bf16),
                          preferred_element_type=f32)
        g_off = carry_ref[...]
        l_off = _prefix_over_experts(runs)
        g_row = g_off[:, 0:1] + earlier
        l_row = l_off[:, 0:1] + earlier
        pick = lambda oh, v: jnp.sum(jnp.where(oh, v, 0.0), axis=0, keepdims=True)
        rows = (pick(oh0, g_row), pick(oh1, g_row), pick(oh0, l_row), pick(oh1, l_row))
        sub = lax.broadcasted_iota(jnp.int32, (8, tm), 0)
        out = jnp.zeros((8, tm), f32)
        for k, r in enumerate(rows):
            out = jnp.where(sub == k, r, out)
        pos_ref[...] = out.astype(jnp.int32)
        sub = lax.broadcasted_iota(jnp.int32, (8, LANE), 0)
        tab = jnp.zeros((8, LANE), f32)
        for k, v in enumerate((runs * (1.0 / MOE_CHUNK), l_off, g_off)):
            tab = jnp.where(sub == k, _experts_to_lanes(v), tab)
        tab_ref[0] = tab.astype(jnp.int32)

    carry_ref[...] += runs


def _rank(route_t):
    tm = TOK_TILE
    n_tiles = N_TOK // tm
    return pl.pallas_call(
        _rank_kernel,
        grid=(2, n_tiles),
        in_specs=[pl.BlockSpec((8, tm), lambda p, i: (0, i))],
        out_specs=[pl.BlockSpec((8, tm), lambda p, i: (0, i * p)),
                   pl.BlockSpec((8, LANE), lambda p, i: (0, 0)),
                   pl.BlockSpec((1, 8, LANE), lambda p, i: (i * p, 0, 0))],
        out_shape=[jax.ShapeDtypeStruct((8, N_TOK), jnp.int32),
                   jax.ShapeDtypeStruct((8, LANE), jnp.int32),
                   jax.ShapeDtypeStruct((n_tiles, 8, LANE), jnp.int32)],
        scratch_shapes=[pltpu.VMEM((N_EXPERTS, LANE), f32)],
        compiler_params=pltpu.CompilerParams(dimension_semantics=("arbitrary", "arbitrary")),
        name="moe_rank",
    )(route_t)


def _dispatch_kernel(tab_ref, h_ref, rows_ref, xs_in_ref, xs_ref, loc, sem):
    del xs_in_ref
    tm = h_ref.shape[0]
    i = pl.program_id(0)
    rid = lax.broadcasted_iota(jnp.int32, (MOE_LOCAL_ROWS, tm), 0)
    sel = (rid == rows_ref[2:3, :]) | (rid == rows_ref[3:4, :])
    loc[...] = jnp.dot(jnp.where(sel, 1.0, 0.0).astype(bf16), h_ref[...], preferred_element_type=f32).astype(bf16)

    def chunk_copy(src_row, dst_row):
        return pltpu.make_async_copy(loc.at[pl.ds(pl.multiple_of(src_row, MOE_CHUNK), MOE_CHUNK)],
                                     xs_ref.at[pl.ds(pl.multiple_of(dst_row, MOE_CHUNK), MOE_CHUNK)], sem)

    n_total = 0
    for e in range(N_EXPERTS):
        n_chunks, l_off, g_off = (tab_ref[(3 * i + k) * N_EXPERTS + e] for k in range(3))

        def issue(c, carry, l_off=l_off, g_off=g_off):
            chunk_copy(l_off + c * MOE_CHUNK, g_off + c * MOE_CHUNK).start()
            return carry

        lax.fori_loop(0, n_chunks, issue, 0)
        n_total = n_total + n_chunks

    def wait_one(c, carry):
        chunk_copy(0, 0).wait()
        return carry

    lax.fori_loop(0, n_total, wait_one, 0)


def _dispatch(run_table, h2, rows, xs_init):
    tm = TOK_TILE
    return pl.pallas_call(
        _dispatch_kernel,
        grid_spec=pltpu.PrefetchScalarGridSpec(
            num_scalar_prefetch=1,
            grid=(N_TOK // tm,),
            in_specs=[pl.BlockSpec((tm, D_MODEL), lambda i, tab: (i, 0)),
                      pl.BlockSpec((8, tm), lambda i, tab: (0, i)),
                      pl.BlockSpec(memory_space=pl.ANY)],
            out_specs=pl.BlockSpec(memory_space=pl.ANY),
            scratch_shapes=[pltpu.VMEM((MOE_LOCAL_ROWS, D_MODEL), bf16), pltpu.SemaphoreType.DMA(())]),
        out_shape=jax.ShapeDtypeStruct(xs_init.shape, xs_init.dtype),
        input_output_aliases={3: 0},
        compiler_params=pltpu.CompilerParams(dimension_semantics=("arbitrary",), vmem_limit_bytes=VMEM_LIMIT),
        name="moe_dispatch",
    )(run_table, h2, rows, xs_init)


def _expert_kernel(te_ref, xs_ref, wg_hbm, wu_hbm, wd_hbm, ys_ref, wg_f32, wu_f32, wd_f32, wg_bf, wu_bf, wd_bf,
                   slot_ref, sem, *, layer):
    j = pl.program_id(0)
    n_used = te_ref[1, 0]
    used = j < n_used
    expert = te_ref[0, j]
    new_expert = jnp.logical_or(j == 0, expert != te_ref[0, jnp.maximum(j - 1, 0)])

    def weight_copies(e, slot):
        return [pltpu.make_async_copy(hbm.at[layer, e], buf.at[slot], sem.at[slot])
                for hbm, buf in ((wg_hbm, wg_f32), (wu_hbm, wu_f32), (wd_hbm, wd_f32))]

    @pl.when(j == 0)
    def _():
        slot_ref[0] = 1
        for cp in weight_copies(expert, 0):
            cp.start()

    @pl.when(jnp.logical_not(used))
    def _():
        ys_ref[...] = jnp.zeros_like(ys_ref)

    @pl.when(used & new_expert)
    def _():
        slot = 1 - slot_ref[0]
        slot_ref[0] = slot
        for cp in weight_copies(expert, slot):
            cp.wait()
        wg_bf[...] = wg_f32[slot].astype(bf16)
        wu_bf[...] = wu_f32[slot].astype(bf16)
        wd_bf[...] = wd_f32[slot].astype(bf16)
        nxt = lax.while_loop(lambda t: (t < n_used) & (te_ref[0, jnp.minimum(t, LANE - 1)] == expert),
                             lambda t: t + 1, j + 1)

        @pl.when(nxt < n_used)
        def _():
            for cp in weight_copies(te_ref[0, nxt], 1 - slot):
                cp.start()

    @pl.when(used)
    def _():
        x = xs_ref[...]
        hg = jnp.dot(x, wg_bf[...], preferred_element_type=f32)
        hu = jnp.dot(x, wu_bf[...], preferred_element_type=f32)
        hid = (hg * jax.nn.sigmoid(hg) * hu).astype(bf16)
        ys_ref[...] = jnp.dot(hid, wd_bf[...], preferred_element_type=f32)


def _experts(te, xs, w_gate, w_up, w_down, layer):
    tm = MOE_TILE
    row = lambda j, te: (jnp.minimum(j, te[1, 0] - 1), 0)
    hbm = pl.BlockSpec(memory_space=pl.ANY)
    return pl.pallas_call(
        functools.partial(_expert_kernel, layer=layer),
        grid_spec=pltpu.PrefetchScalarGridSpec(
            num_scalar_prefetch=1,
            grid=(MOE_ROWS // tm,),
            in_specs=[pl.BlockSpec((tm, D_MODEL), row), hbm, hbm, hbm],
            out_specs=pl.BlockSpec((tm, D_MODEL), lambda j, te: (j, 0)),
            scratch_shapes=[pltpu.VMEM((2, D_MODEL, D_EXPERT), f32), pltpu.VMEM((2, D_MODEL, D_EXPERT), f32),
                            pltpu.VMEM((2, D_EXPERT, D_MODEL), f32),
                            pltpu.VMEM((D_MODEL, D_EXPERT), bf16), pltpu.VMEM((D_MODEL, D_EXPERT), bf16),
                            pltpu.VMEM((D_EXPERT, D_MODEL), bf16),
                            pltpu.SMEM((1,), jnp.int32), pltpu.SemaphoreType.DMA((2,))]),
        out_shape=jax.ShapeDtypeStruct((MOE_ROWS, D_MODEL), f32),
        compiler_params=pltpu.CompilerParams(dimension_semantics=("arbitrary",), vmem_limit_bytes=VMEM_LIMIT),
        name="moe_experts",
    )(te, xs, w_gate, w_up, w_down)


def _gather_expert_rows(pos0_ref, pos1_ref, ys_ref, buf, sem):
    rows = buf.shape[2]
    i = pl.program_id(0)
    slot = i % 2

    def issue(tile, sl):
        base = tile * rows

        def body(t, carry):
            for s, pos_ref in enumerate((pos0_ref, pos1_ref)):
                pltpu.make_async_copy(ys_ref.at[pl.ds(pos_ref[base + t], 1)], buf.at[sl, s, pl.ds(t, 1)],
                                      sem.at[sl]).start()
            return carry

        lax.fori_loop(0, rows, body, 0, unroll=8)

    @pl.when(i == 0)
    def _():
        issue(0, 0)

    @pl.when(i + 1 < pl.num_programs(0))
    def _():
        issue(i + 1, 1 - slot)

    for s in range(2):
        pltpu.make_async_copy(ys_ref.at[pl.ds(0, rows)], buf.at[slot, s], sem.at[slot]).wait()
    return buf[slot, 0], buf[slot, 1]


def _moe_residual(pos0_ref, pos1_ref, ys_ref, x1_ref, rc_ref, mod_ref, buf, sem):
    y0, y1 = _gather_expert_rows(pos0_ref, pos1_ref, ys_ref, buf, sem)
    rc = rc_ref[...]
    return x1_ref[...] + mod_ref[0][5:6] * (rc[:, 2:3] * y0 + rc[:, 3:4] * y1)


def _final_kernel(pos0_ref, pos1_ref, ys_ref, x1_ref, rc_ref, mod_ref, fn_ref, yc_ref, yl_ref, buf, sem):
    x2 = _moe_residual(pos0_ref, pos1_ref, ys_ref, x1_ref, rc_ref, mod_ref, buf, sem)
    y = x2 * lax.rsqrt(jnp.mean(x2 * x2, axis=-1, keepdims=True) + EPS) * fn_ref[...]
    is_ctx = pl.program_id(0) < N_CTX // x1_ref.shape[0]

    @pl.when(is_ctx)
    def _():
        yc_ref[...] = y

    @pl.when(jnp.logical_not(is_ctx))
    def _():
        yl_ref[...] = y


def _final_combine(pos0, pos1, ys, x1, rc, mods, fn):
    tc = COMBINE_TILE
    row = lambda i, p0, p1: (i, 0)
    n_ctx_tiles = N_CTX // tc
    return pl.pallas_call(
        _final_kernel,
        grid_spec=pltpu.PrefetchScalarGridSpec(
            num_scalar_prefetch=2,
            grid=(N_TOK // tc,),
            in_specs=[pl.BlockSpec(memory_space=pl.ANY),
                      pl.BlockSpec((tc, D_MODEL), row),
                      pl.BlockSpec((tc, LANE), row),
                      pl.BlockSpec((1, 6, D_MODEL), lambda i, p0, p1: (_mod_row(i, tc), 0, 0)),
                      pl.BlockSpec((1, D_MODEL), lambda i, p0, p1: (0, 0))],
            out_specs=[pl.BlockSpec((tc, D_MODEL), lambda i, p0, p1: (jnp.minimum(i, n_ctx_tiles - 1), 0)),
                       pl.BlockSpec((tc, D_MODEL), lambda i, p0, p1: (jnp.maximum(i - n_ctx_tiles, 0), 0))],
            scratch_shapes=[pltpu.VMEM((2, 2, tc, D_MODEL), f32), pltpu.SemaphoreType.DMA((2,))]),
        out_shape=[jax.ShapeDtypeStruct((N_CTX, D_MODEL), f32), jax.ShapeDtypeStruct((N_LAT, D_MODEL), f32)],
        compiler_params=pltpu.CompilerParams(dimension_semantics=("arbitrary",), vmem_limit_bytes=VMEM_LIMIT),
        name="moe_combine_final",
    )(pos0, pos1, ys, x1, rc, mods, fn)


def _moe_experts(h2, route_t, w_gate, w_up, w_down, layer, xs_buf):
    rows, te, runs = _rank(route_t)
    xs = _dispatch(runs[:, :3, :N_EXPERTS].reshape(-1), h2, rows, xs_buf)
    return rows[0], rows[1], _experts(te, xs, w_gate, w_up, w_down, layer), xs


def _pad_heads(w):
    lead = w.shape[:-1]
    w = w.reshape(*lead, ML_HEADS, ML_DIM)
    w = jnp.pad(w, [(0, 0)] * len(lead) + [(0, 0), (0, ML_PAD - ML_DIM)])
    return w.reshape(*lead, ML_PW)


def _pack_in_cols(wb):
    o = 0
    qa = wb[..., o:o + NA_WIDTH] * (NA_DIM ** -0.5)
    ka = wb[..., o + NA_WIDTH:o + 2 * NA_WIDTH]
    va = wb[..., o + 2 * NA_WIDTH:o + 3 * NA_WIDTH]
    o += 3 * NA_WIDTH
    qb, kb, vb, ob = [_pad_heads(wb[..., o + j * ML_WIDTH:o + (j + 1) * ML_WIDTH]) for j in range(4)]
    o += 4 * ML_WIDTH
    gates = wb[..., o:o + N_GATE_COLS]
    o += N_GATE_COLS
    pool = wb[..., o:o + POOL_WIDTH]
    main = jnp.concatenate([qa, ka, va, qb, vb, ob, pool], axis=-1)
    gates_p = jnp.pad(gates, [(0, 0)] * (gates.ndim - 1) + [(0, LANE - N_GATE_COLS)])
    return main, gates_p, jnp.concatenate([kb, gates], axis=-1)


def _pack_w_in(w, b):
    w_main, w_gates, w_feat = _pack_in_cols(w)
    b_main, b_gates, b_feat = _pack_in_cols(b.astype(f32))
    return (w_main.astype(bf16), b_main[:, None], w_gates.astype(bf16), b_gates[:, None],
            jnp.swapaxes(w_feat, 1, 2).astype(bf16), b_feat[:, :, None])


def _pack_w_out(w):
    n_l = w.shape[0]
    wb = w[:, NA_WIDTH:NA_WIDTH + ML_WIDTH].reshape(n_l, ML_HEADS, ML_DIM, D_MODEL)
    wb = jnp.pad(wb, ((0, 0), (0, 0), (0, ML_PAD - ML_DIM), (0, 0))).reshape(n_l, ML_PW, D_MODEL)
    return jnp.concatenate([w[:, :NA_WIDTH], wb, w[:, NA_WIDTH + ML_WIDTH:]], axis=1).astype(bf16)


def _block_diag(w):
    n_l, g, c, _ = w.shape
    eye = jnp.eye(g, dtype=w.dtype)
    return (eye[None, :, None, :, None] * w[:, :, :, None, :]).reshape(n_l, g * c, g * c)


def kernel(x_prompt, x_sample, cache_k_attn, cache_v_attn, state_mlstm_C, state_mlstm_n, state_mlstm_m, c, c_ctx,
           w_ada, b_ada, norm1, w_in, b_in, rpb, ml_norm, w_pool, pool_scale, w_out, norm2, w_router, b_router,
           w_gate, w_up, w_down, final_norm):
    dt = x_prompt.dtype
    x_ctx = x_prompt.reshape(N_CTX, D_MODEL).astype(f32)
    x_lat = x_sample.reshape(N_LAT, D_MODEL).astype(f32)
    x_lat_block0 = 0
    cvec = jnp.concatenate([c_ctx[None], c, jnp.zeros((8 - 1 - DEC_BATCH, D_MODEL), c.dtype)], axis=0).astype(f32)
    mods_all = _ada(cvec, w_ada.astype(f32), b_ada.astype(f32))
    mods_all = mods_all[:, :1 + DEC_BATCH].reshape(DEPTH, 1 + DEC_BATCH, 6, D_MODEL)

    wr_t = w_router.astype(f32).T
    br_t = b_router.astype(f32)[:, None]
    fn = final_norm.astype(f32)[None]

    na_bias = _na_bias_tables(rpb)
    xs_buf = jnp.zeros((MOE_ROWS, D_MODEL), bf16)
    in_params = (norm1.astype(f32)[:, None],) + _pack_w_in(w_in, b_in)
    out_params = (_block_diag(w_pool.astype(f32)).astype(bf16), pool_scale.astype(f32)[:, None],
                  _pad_heads(ml_norm.astype(f32))[:, None], _pack_w_out(w_out), norm2.astype(f32)[:, None])

    kv_cache = jnp.zeros((DEPTH, N_CTX, 2 * NA_WIDTH), f32)
    Cs, ns, ms = [], [], []
    pending = None
    for l in range(DEPTH):
        mods = mods_all[l]
        if pending is None:
            qkva, kv_cache, qvo, kt, gates, gates_t, pin = _in_proj(x_ctx, x_lat, mods, l, *in_params, kv_cache)
        else:
            x, qkva, kv_cache, qvo, kt, gates, gates_t, pin = _moe_in_proj(*pending, mods_all[l - 1], mods, l,
                                                                           *in_params, kv_cache)
            x_ctx, x_lat, x_lat_block0 = x, x, N_CTX // TOK_TILE

        oa_ctx = _ctx_attention(qkva.reshape(N_TOK // SEQ, SEQ, W_A))
        ck = (cache_k_attn[:, l].reshape(DEC_BATCH, PAST_LEN, NA_WIDTH)).astype(bf16)
        cv = (cache_v_attn[:, l].reshape(DEC_BATCH, PAST_LEN, NA_WIDTH)).astype(bf16)
        oa_lat = _neighborhood_attention(qkva.reshape(N_TOK // DEC_SEQ, DEC_SEQ, W_A), ck, cv, na_bias, l)

        c_l, m_l = _pack_ml_state(state_mlstm_C[:, l], state_mlstm_n[:, l], state_mlstm_m[:, l])
        hf, hb, c_fin, m_fin = _mlstm(qvo, kt, gates, gates_t, c_l, m_l)
        C_l, n_l, m_l2 = _unpack_ml_state(c_fin[:BATCH], m_fin[:BATCH])
        Cs.append(C_l)
        ns.append(n_l)
        ms.append(m_l2)

        x1, h2, route_t, rc = _out_proj(x_ctx, x_lat, x_lat_block0, mods, l,
                                        oa_ctx.reshape(N_CTX, NA_WIDTH), oa_lat.reshape(N_LAT, NA_WIDTH),
                                        hf.reshape(N_TOK, ML_PW), hb.reshape(N_TOK, ML_PW), qvo, pin,
                                        *out_params, wr_t, br_t)
        pos0, pos1, ys, xs_buf = _moe_experts(h2, route_t, w_gate, w_up, w_down, l, xs_buf)
        pending = (pos0, pos1, ys, x1, rc)

    x = _final_combine(*pending, mods_all[DEPTH - 1], fn)
    y_prompt = x[0].reshape(BATCH, SEQ, D_MODEL).astype(dt)
    y_sample = x[1].reshape(DEC_BATCH, DEC_SEQ, D_MODEL).astype(dt)
    new_kv = kv_cache.reshape(DEPTH, BATCH, SEQ, 2, NA_HEADS, NA_DIM).transpose(3, 1, 0, 2, 4, 5)
    return (y_prompt, y_sample, new_kv[0].astype(dt), new_kv[1].astype(dt),
            jnp.stack(Cs, axis=1).astype(dt), jnp.stack(ns, axis=1).astype(dt), jnp.stack(ms, axis=1).astype(dt))
```

```python
import functools

import numpy as np
import jax
import jax.numpy as jnp
from jax import lax
from jax.experimental import pallas as pl
from jax.experimental.pallas import tpu as pltpu

D_MODEL = 1024
BATCH = 16
SEQ = 256
DEPTH = 4
DEC_BATCH = 2
DEC_SEQ = 4096
PAST_LEN = 256
GRID_W = 64
EPS = 1e-6
NEG_INF = -1e30
NA_HEADS = 6
NA_DIM = 64
NA_WIDTH = NA_HEADS * NA_DIM
NA_ROWS = 8
NA_COLS = 16
RPB_ROWS = 2 * NA_ROWS - 1
RPB_COLS = 2 * NA_COLS - 1
ML_HEADS = 4
ML_DIM = 96
ML_WIDTH = ML_HEADS * ML_DIM
POOL_WINDOWS = (2, 4, 8, 16)
POOL_GROUPS = 4
POOL_DIM = 64
POOL_WIDTH = POOL_GROUPS * POOL_DIM
N_GATE_COLS = 4 * ML_HEADS
N_EXPERTS = 16
N_EXPERT_GROUPS = 4
EXPERTS_PER_GROUP = N_EXPERTS // N_EXPERT_GROUPS
D_EXPERT = 512
ADA_DIM = 6 * D_MODEL

N_CTX = BATCH * SEQ
N_LAT = DEC_BATCH * DEC_SEQ
N_TOK = N_CTX + N_LAT
LANE = 128
ML_PAD = LANE
ML_PW = ML_HEADS * ML_PAD
CAUG = ML_PAD
NA_PAIRS = NA_HEADS // 2
TOK_TILE = 512
ML_CHUNK = 256
NA_QROWS = 4
NA_KROWS = NA_QROWS + NA_ROWS - 1
POOL_HALO = max(POOL_WINDOWS) // 2
POOL_BLOCK = 128
MOE_TILE = 512
MOE_CHUNK = 16
MOE_LOCAL_ROWS = -(-(2 * TOK_TILE + N_EXPERTS * (MOE_CHUNK - 1)) // LANE) * LANE
MOE_ROWS = -(-(2 * N_TOK + (N_TOK // TOK_TILE) * N_EXPERTS * (MOE_CHUNK - 1) + N_EXPERTS * (MOE_TILE - 1))
             // MOE_TILE) * MOE_TILE
COMBINE_TILE = 256
VMEM_LIMIT = 56 * 1024 * 1024

W_A = 3 * NA_WIDTH
W_B = 3 * ML_PW
N_TCOLS = ML_PW + N_GATE_COLS
W_MAIN = W_A + W_B + POOL_WIDTH

f32 = jnp.float32
bf16 = jnp.bfloat16
HI = lax.Precision.HIGHEST


def _nt(a, b, **kw):
    return lax.dot_general(a, b, (((1,), (1,)), ((), ())), preferred_element_type=f32, **kw)


def _mod_row(i, tile):
    n_ctx_tiles = N_CTX // tile
    per_batch = DEC_SEQ // tile
    return jnp.where(i < n_ctx_tiles, 0, 1 + (i - n_ctx_tiles) // per_batch)


def _ada_kernel(c_ref, w_ref, b_ref, o_ref):
    s = c_ref[...]
    s = s * jax.nn.sigmoid(s)
    o_ref[0] = jnp.dot(s.astype(bf16), w_ref[0].astype(bf16), preferred_element_type=f32) + b_ref[0]


def _ada(cvec, w_ada, b_ada):
    nj = ADA_DIM // D_MODEL
    return pl.pallas_call(
        _ada_kernel,
        grid=(DEPTH, nj),
        in_specs=[pl.BlockSpec((8, D_MODEL), lambda l, j: (0, 0)),
                  pl.BlockSpec((1, D_MODEL, D_MODEL), lambda l, j: (l, 0, j)),
                  pl.BlockSpec((1, 1, D_MODEL), lambda l, j: (l, 0, j))],
        out_specs=pl.BlockSpec((1, 8, D_MODEL), lambda l, j: (l, 0, j)),
        out_shape=jax.ShapeDtypeStruct((DEPTH, 8, ADA_DIM), f32),
        name="ada_mod",
    )(cvec, w_ada, b_ada.reshape(DEPTH, 1, ADA_DIM))


def _in_kernel(xc_ref, xl_ref, *refs):
    is_ctx = pl.program_id(0) < N_CTX // xc_ref.shape[0]
    _in_body(jnp.where(is_ctx, xc_ref[...], xl_ref[...]), *refs)


def _moe_in_kernel(pos0_ref, pos1_ref, ys_ref, x1_ref, rc_ref, mod_prev_ref, *refs):
    in_refs, x_out_ref, out_refs, (buf, sem) = refs[:9], refs[9], refs[10:-2], refs[-2:]
    x = _moe_residual(pos0_ref, pos1_ref, ys_ref, x1_ref, rc_ref, mod_prev_ref, buf, sem)
    x_out_ref[...] = x
    _in_body(x, *in_refs, *out_refs)


def _in_body(x, mod_ref, n1_ref, w_ref, b_ref, wg_ref, bg_ref, wt_ref, bt_ref, kv_in_ref,
             a_ref, kv_ref, b_out_ref, kt_ref, g_ref, gt_ref, pin_ref):
    del kv_in_ref
    mod = mod_ref[0]
    h = x * lax.rsqrt(jnp.mean(x * x, axis=-1, keepdims=True) + EPS) * n1_ref[...]
    h = (h * (1.0 + mod[1:2]) + mod[0:1]).astype(bf16)
    pa = jnp.dot(h, w_ref[:, 0:W_A], preferred_element_type=f32) + b_ref[:, 0:W_A]
    a_ref[...] = pa.astype(bf16)
    kv_ref[...] = pa[:, NA_WIDTH:W_A]
    for j in range(3):
        lo = W_A + j * ML_PW
        pb = jnp.dot(h, w_ref[:, lo:lo + ML_PW], preferred_element_type=f32) + b_ref[:, lo:lo + ML_PW]
        if j == 0:
            pb = pb * (ML_DIM ** -0.5)
        b_out_ref[:, j * ML_PW:(j + 1) * ML_PW] = pb.astype(bf16)
    lo = W_A + W_B
    pin_ref[...] = jnp.dot(h, w_ref[:, lo:lo + POOL_WIDTH], preferred_element_type=f32) + b_ref[:, lo:lo + POOL_WIDTH]
    g_ref[...] = jnp.dot(h, wg_ref[...], preferred_element_type=f32) + bg_ref[...]
    t = _nt(wt_ref[...], h) + bt_ref[...]
    kt_ref[...] = t[0:ML_PW].astype(bf16)
    gt_ref[...] = t[ML_PW:N_TCOLS]


def _in_proj_specs(layer):
    tm = TOK_TILE
    lyr = lambda shape: pl.BlockSpec((None,) + shape, lambda i, *_: (layer, 0, 0))
    rows = lambda width: pl.BlockSpec((tm, width), lambda i, *_: (i, 0))
    cols = lambda height: pl.BlockSpec((height, tm), lambda i, *_: (0, i))
    param_specs = [pl.BlockSpec((1, 6, D_MODEL), lambda i, *_: (_mod_row(i, tm), 0, 0)),
                   lyr((1, D_MODEL)), lyr((D_MODEL, W_MAIN)), lyr((1, W_MAIN)), lyr((D_MODEL, LANE)), lyr((1, LANE)),
                   lyr((N_TCOLS, D_MODEL)), lyr((N_TCOLS, 1)), pl.BlockSpec(memory_space=pl.ANY)]
    kv_spec = pl.BlockSpec((None, tm, 2 * NA_WIDTH), lambda i, *_: (layer, jnp.minimum(i, N_CTX // tm), 0))
    out_specs = [rows(W_A), kv_spec, rows(W_B), cols(ML_PW), rows(LANE), cols(N_GATE_COLS), rows(POOL_WIDTH)]
    out_shape = [jax.ShapeDtypeStruct((N_TOK, W_A), bf16),
                 jax.ShapeDtypeStruct((DEPTH, N_CTX + tm, 2 * NA_WIDTH), f32),
                 jax.ShapeDtypeStruct((N_TOK, W_B), bf16),
                 jax.ShapeDtypeStruct((ML_PW, N_TOK), bf16),
                 jax.ShapeDtypeStruct((N_TOK, LANE), f32),
                 jax.ShapeDtypeStruct((N_GATE_COLS, N_TOK), f32),
                 jax.ShapeDtypeStruct((N_TOK, POOL_WIDTH), f32)]
    return rows, param_specs, out_specs, out_shape


def _in_proj(x_ctx, x_lat, mods, layer, *params):
    tm = TOK_TILE
    n_ctx_tiles = N_CTX // tm
    rows, param_specs, out_specs, out_shape = _in_proj_specs(layer)
    return pl.pallas_call(
        _in_kernel,
        grid=(N_TOK // tm,),
        in_specs=[pl.BlockSpec((tm, D_MODEL), lambda i: (jnp.minimum(i, n_ctx_tiles - 1), 0)),
                  pl.BlockSpec((tm, D_MODEL), lambda i: (jnp.maximum(i - n_ctx_tiles, 0), 0))] + param_specs,
        out_specs=out_specs,
        out_shape=out_shape,
        input_output_aliases={2 + len(param_specs) - 1: 1},
        compiler_params=pltpu.CompilerParams(dimension_semantics=("arbitrary",), vmem_limit_bytes=VMEM_LIMIT),
        name="in_proj",
    )(x_ctx, x_lat, mods, *params)


def _moe_in_proj(pos0, pos1, ys, x1, rc, mods_prev, mods, layer, *params):
    tm = TOK_TILE
    rows, param_specs, out_specs, out_shape = _in_proj_specs(layer)
    return pl.pallas_call(
        _moe_in_kernel,
        grid_spec=pltpu.PrefetchScalarGridSpec(
            num_scalar_prefetch=2,
            grid=(N_TOK // tm,),
            in_specs=[pl.BlockSpec(memory_space=pl.ANY), rows(D_MODEL), rows(LANE),
                      pl.BlockSpec((1, 6, D_MODEL), lambda i, *_: (_mod_row(i, tm), 0, 0))] + param_specs,
            out_specs=[rows(D_MODEL)] + out_specs,
            scratch_shapes=[pltpu.VMEM((2, 2, tm, D_MODEL), f32), pltpu.SemaphoreType.DMA((2,))]),
        out_shape=[jax.ShapeDtypeStruct((N_TOK, D_MODEL), f32)] + out_shape,
        input_output_aliases={6 + len(param_specs) - 1: 2},
        compiler_params=pltpu.CompilerParams(dimension_semantics=("arbitrary",), vmem_limit_bytes=VMEM_LIMIT),
        name="moe_combine_in_proj",
    )(pos0, pos1, ys, x1, rc, mods_prev, mods, *params)


def _pair_attention(qp, parts):
    lane = lax.broadcasted_iota(jnp.int32, (1, LANE), 1)
    outs = []
    for j in range(2):
        in_half = (lane >= j * NA_DIM) & (lane < (j + 1) * NA_DIM)
        qm = jnp.where(in_half, qp, jnp.zeros_like(qp))
        scores = []
        for k, _, bias in parts:
            s = _nt(qm, k)
            if bias is not None:
                s = s + bias[j]
            scores.append(s)
        m = scores[0].max(axis=-1, keepdims=True)
        for s in scores[1:]:
            m = jnp.maximum(m, s.max(axis=-1, keepdims=True))
        den = None
        acc = None
        for s, (_, v, _) in zip(scores, parts):
            p = jnp.exp(s - m)
            ps = p.sum(axis=-1, keepdims=True)
            den = ps if den is None else den + ps
            o = jnp.dot(p.astype(bf16), v, preferred_element_type=f32)
            acc = o if acc is None else acc + o
        outs.append(acc / den)
    return jnp.where(lane < NA_DIM, outs[0], outs[1])


def _ctx_attn_kernel(q_ref, k_ref, v_ref, o_ref):
    for p in range(NA_PAIRS):
        sl = slice(p * LANE, (p + 1) * LANE)
        o = _pair_attention(q_ref[0, :, sl], [(k_ref[0, :, sl], v_ref[0, :, sl], None)])
        o_ref[0, :, sl] = o.astype(bf16)


def _ctx_attention(qkv):
    blk = lambda c: pl.BlockSpec((1, SEQ, NA_WIDTH), lambda b, c=c: (b, 0, c))
    return pl.pallas_call(
        _ctx_attn_kernel,
        grid=(BATCH,),
        in_specs=[blk(0), blk(1), blk(2)],
        out_specs=pl.BlockSpec((1, SEQ, NA_WIDTH), lambda b: (b, 0, 0)),
        out_shape=jax.ShapeDtypeStruct((BATCH, SEQ, NA_WIDTH), bf16),
        name="ctx_attention",
    )(qkv, qkv, qkv)


def _na_window_start(rb):
    return jnp.clip(rb * NA_QROWS - NA_ROWS // 2, 0, DEC_SEQ // GRID_W - NA_KROWS)


def _na_bias(tab_ref, head, rb):
    rows = DEC_SEQ // GRID_W
    ws = _na_window_start(rb)
    lane = lax.broadcasted_iota(jnp.int32, (1, NA_KROWS * GRID_W), 1)
    per_qrow = []
    for dq in range(NA_QROWS):
        qr = rb * NA_QROWS + dq
        a0 = ws - qr + (NA_ROWS - 1) + NA_KROWS
        tiles = [tab_ref[head, a0 + 2 * j] for j in range((NA_KROWS + 1) // 2)]
        t = jnp.concatenate(tiles, axis=1)[:, :NA_KROWS * GRID_W]
        lo = (jnp.clip(qr - NA_ROWS // 2, 0, rows - NA_ROWS) - ws) * GRID_W
        ok = (lane >= lo) & (lane < lo + NA_ROWS * GRID_W)
        per_qrow.append(jnp.where(ok, t, NEG_INF))
    return jnp.concatenate(per_qrow, axis=0)


def _na_kernel(q_ref, k_ref, v_ref, ck_ref, cv_ref, tab_ref, o_ref):
    rb = pl.program_id(1)
    start = pl.multiple_of(_na_window_start(rb) * GRID_W, GRID_W)
    nk = NA_KROWS * GRID_W
    for p in range(NA_PAIRS):
        sl = slice(p * LANE, (p + 1) * LANE)
        bias = [_na_bias(tab_ref.at[0], 2 * p + j, rb) for j in range(2)]
        parts = [(k_ref[0, pl.ds(start, nk), sl], v_ref[0, pl.ds(start, nk), sl], bias),
                 (ck_ref[0, :, sl], cv_ref[0, :, sl], None)]
        o = _pair_attention(q_ref[0, :, sl], parts)
        o_ref[0, :, sl] = o.astype(bf16)


def _neighborhood_attention(qkv, ck, cv, tables, layer):
    nq = NA_QROWS * GRID_W
    n_rb = DEC_SEQ // nq
    return pl.pallas_call(
        _na_kernel,
        grid=(DEC_BATCH, n_rb),
        in_specs=[pl.BlockSpec((1, nq, NA_WIDTH), lambda b, r: (1 + b, r, 0)),
                  pl.BlockSpec((1, DEC_SEQ, NA_WIDTH), lambda b, r: (1 + b, 0, 1)),
                  pl.BlockSpec((1, DEC_SEQ, NA_WIDTH), lambda b, r: (1 + b, 0, 2)),
                  pl.BlockSpec((1, PAST_LEN, NA_WIDTH), lambda b, r: (b, 0, 0)),
                  pl.BlockSpec((1, PAST_LEN, NA_WIDTH), lambda b, r: (b, 0, 0)),
                  pl.BlockSpec((1,) + tables.shape[1:], lambda b, r: (layer, 0, 0, 0, 0))],
        out_specs=pl.BlockSpec((1, nq, NA_WIDTH), lambda b, r: (b, r, 0)),
        out_shape=jax.ShapeDtypeStruct((DEC_BATCH, DEC_SEQ, NA_WIDTH), bf16),
        compiler_params=pltpu.CompilerParams(dimension_semantics=("arbitrary", "arbitrary"),
                                             vmem_limit_bytes=VMEM_LIMIT),
        name="neighborhood_attention",
    )(qkv, qkv, qkv, ck, cv, tables)


def _na_bias_tables(rpb):
    qc = np.arange(GRID_W)[:, None]
    kc = np.arange(GRID_W)[None, :]
    dc = np.clip(kc - qc + NA_COLS - 1, 0, RPB_COLS - 1)
    col_start = np.clip(qc - NA_COLS // 2, 0, GRID_W - NA_COLS)
    col_ok = (kc >= col_start) & (kc < col_start + NA_COLS)
    pick_col = (dc[None] == np.arange(RPB_COLS)[:, None, None]).astype(np.float32)
    rpb_pad = jnp.pad(rpb.astype(f32), ((0, 0), (0, 0), (NA_KROWS, NA_KROWS + 1), (0, 0)))
    row_ok = (np.arange(rpb_pad.shape[2]) >= NA_KROWS) & (np.arange(rpb_pad.shape[2]) < NA_KROWS + RPB_ROWS)
    by_col = jnp.einsum('lhab,bqk->lhaqk', rpb_pad, pick_col, precision=HI)
    by_col = jnp.where(row_ok[None, None, :, None, None] & col_ok[None, None, None], by_col, NEG_INF)
    return jnp.concatenate([by_col[:, :, :-1], by_col[:, :, 1:]], axis=-1)


def _log_sigmoid(x):
    return -(jnp.maximum(-x, 0.0) + jnp.log(1.0 + jnp.exp(-jnp.abs(x))))


def _split3(x):
    hi = x.astype(bf16)
    r1 = x - hi.astype(f32)
    mid = r1.astype(bf16)
    lo = (r1 - mid.astype(f32)).astype(bf16)
    return hi, mid, lo


def _mlstm_kernel(qf_ref, vf_ref, ktf_ref, gf_ref, gtf_ref, qb_ref, vb_ref, ktb_ref, gb_ref, gtb_ref,
                  c0_ref, m0_ref, hf_ref, hb_ref, c_out_ref, m_out_ref, c_scr, m_scr):
    L = ML_CHUNK
    seq, c, n_chunks, _ = _ml_schedule(pl.program_id(0))

    @pl.when(c == 0)
    def _():
        is_ctx = seq < BATCH
        c_scr[...] = jnp.where(is_ctx, 0.0, c0_ref[0])
        m_scr[...] = jnp.where(is_ctx, 0.0, m0_ref[0])

    ri = lax.broadcasted_iota(jnp.int32, (L, L), 0)
    ci = lax.broadcasted_iota(jnp.int32, (L, L), 1)
    lane = lax.broadcasted_iota(jnp.int32, (L, ML_PAD), 1)
    is_ncol = lane == ML_DIM
    lower = ri >= ci
    upper = ri <= ci
    lower_b = jnp.where(lower, 1.0, 0.0).astype(bf16)
    upper_b = jnp.where(upper, 1.0, 0.0).astype(bf16)
    dirs = ((qf_ref, ktf_ref, vf_ref, gf_ref, gtf_ref, hf_ref), (qb_ref, ktb_ref, vb_ref, gb_ref, gtb_ref, hb_ref))
    for d, (q_ref, kt_ref, v_ref, g_ref, gt_ref, h_ref) in enumerate(dirs):
        g = g_ref[...][:, 0:N_GATE_COLS]
        gt = gt_ref[...]
        lf_c = _log_sigmoid(g)
        lf_r = _log_sigmoid(gt)
        b_cols = sum(jnp.dot(lower_b, part, preferred_element_type=f32) for part in _split3(lf_c))
        b_rows = sum(jnp.dot(part, upper_b, preferred_element_type=f32) for part in _split3(lf_r))
        tot_c = jnp.sum(lf_c, axis=0, keepdims=True)
        tot_r = jnp.sum(lf_r, axis=1, keepdims=True)
        visible = lower
        if d == 1:
            b_cols = tot_c - b_cols + lf_c
            b_rows = tot_r - b_rows + lf_r
            visible = upper
        for hd in range(ML_HEADS):
            st = d * ML_HEADS + hd
            ci_ = 2 * ML_HEADS * d + hd
            cf_ = ci_ + ML_HEADS
            sl = slice(hd * ML_PAD, (hd + 1) * ML_PAD)
            bc = b_cols[:, cf_:cf_ + 1]
            br = b_rows[cf_:cf_ + 1, :]
            li_r = gt[ci_:ci_ + 1, :]
            m_prev = m_scr[st:st + 1, 0:1]
            dmat = jnp.where(visible, bc - br + li_r, NEG_INF)
            inter = bc + m_prev
            m_t = jnp.maximum(inter, dmat.max(axis=-1, keepdims=True))
            w_intra = jnp.exp(dmat - m_t)
            w_inter = jnp.exp(inter - m_t)
            qh = q_ref[0, :, sl]
            kht = kt_ref[sl, :]
            v_aug = jnp.where(is_ncol, jnp.ones((), bf16), v_ref[0, :, sl])
            s = (jnp.dot(qh, kht, preferred_element_type=f32) * w_intra).astype(bf16)
            c_aug = c_scr[st]
            na = (w_inter * jnp.dot(qh, c_aug.astype(bf16), preferred_element_type=f32)
                  + jnp.dot(s, v_aug, preferred_element_type=f32))
            den = na[:, ML_DIM:ML_DIM + 1]
            h_ref[0, :, sl] = jnp.where(lane < ML_DIM, na / jnp.maximum(jnp.abs(den), jnp.exp(-m_t)), 0.0)
            b_end = tot_r[cf_:cf_ + 1, :]
            g_row = b_end - br + li_r
            m_new = jnp.maximum(b_end + m_prev, g_row.max(axis=1, keepdims=True))
            decay = jnp.exp(b_end + m_prev - m_new)
            kwt = (kht.astype(f32) * jnp.exp(g_row - m_new)).astype(bf16)
            c_scr[st] = decay * c_aug + jnp.dot(kwt, v_aug, preferred_element_type=f32)
            m_scr[st:st + 1, :] = jnp.broadcast_to(m_new, (1, LANE))

    @pl.when(c == n_chunks - 1)
    def _():
        c_out_ref[0] = c_scr[...]
        m_out_ref[0] = m_scr[...]


def _ml_schedule(s):
    nc_ctx, nc_lat = SEQ // ML_CHUNK, DEC_SEQ // ML_CHUNK
    n_ctx_steps = BATCH * nc_ctx
    is_ctx = s < n_ctx_steps
    t = s - n_ctx_steps
    seq = jnp.where(is_ctx, s // nc_ctx, BATCH + t // nc_lat)
    c = jnp.where(is_ctx, s % nc_ctx, t % nc_lat)
    nc = jnp.where(is_ctx, nc_ctx, nc_lat)
    base = jnp.where(is_ctx, (s // nc_ctx) * nc_ctx, n_ctx_steps + (t // nc_lat) * nc_lat)
    return seq, c, nc, base


def _mlstm(qvo, kt, gates, gates_t, c0, m0):
    L = ML_CHUNK
    n_seq = BATCH + DEC_BATCH

    def fwd(s):
        _, c, _, base = _ml_schedule(s)
        return base + c

    def bwd(s):
        _, c, nc, base = _ml_schedule(s)
        return base + nc - 1 - c

    seq_of = lambda s: _ml_schedule(s)[0]
    lat_of = lambda s: jnp.maximum(seq_of(s) - BATCH, 0)

    def specs(pos):
        return [pl.BlockSpec((1, L, ML_PW), lambda s, j=j: (pos(s), 0, j)) for j in range(2)] + [
            pl.BlockSpec((ML_PW, L), lambda s: (0, pos(s))),
            pl.BlockSpec((L, LANE), lambda s: (pos(s), 0)),
            pl.BlockSpec((N_GATE_COLS, L), lambda s: (0, pos(s)))]

    q3 = qvo.reshape(N_TOK // L, L, W_B)
    n_str = 2 * ML_HEADS
    return pl.pallas_call(
        _mlstm_kernel,
        grid=(N_TOK // L,),
        in_specs=specs(fwd) + specs(bwd) + [
            pl.BlockSpec((1, n_str, ML_PAD, CAUG), lambda s: (lat_of(s), 0, 0, 0)),
            pl.BlockSpec((1, n_str, LANE), lambda s: (lat_of(s), 0, 0))],
        out_specs=[pl.BlockSpec((1, L, ML_PW), lambda s: (fwd(s), 0, 0)),
                   pl.BlockSpec((1, L, ML_PW), lambda s: (bwd(s), 0, 0)),
                   pl.BlockSpec((1, n_str, ML_PAD, CAUG), lambda s: (seq_of(s), 0, 0, 0)),
                   pl.BlockSpec((1, n_str, LANE), lambda s: (seq_of(s), 0, 0))],
        out_shape=[jax.ShapeDtypeStruct((N_TOK // L, L, ML_PW), f32),
                   jax.ShapeDtypeStruct((N_TOK // L, L, ML_PW), f32),
                   jax.ShapeDtypeStruct((n_seq, n_str, ML_PAD, CAUG), f32),
                   jax.ShapeDtypeStruct((n_seq, n_str, LANE), f32)],
        scratch_shapes=[pltpu.VMEM((n_str, ML_PAD, CAUG), f32), pltpu.VMEM((n_str, LANE), f32)],
        compiler_params=pltpu.CompilerParams(dimension_semantics=("arbitrary",), vmem_limit_bytes=VMEM_LIMIT),
        name="mlstm",
    )(q3, q3, kt, gates, gates_t, q3, q3, kt, gates, gates_t, c0, m0)


def _pack_ml_state(C, n, m):
    B = C.shape[0]
    c_aug = jnp.zeros((B, 2, ML_HEADS, ML_PAD, CAUG), f32)
    c_aug = c_aug.at[:, :, :, :ML_DIM, :ML_DIM].set(C.astype(f32))
    c_aug = c_aug.at[:, :, :, :ML_DIM, ML_DIM].set(n.astype(f32))
    m_b = jnp.broadcast_to(m.astype(f32)[..., None], (B, 2, ML_HEADS, LANE))
    return c_aug.reshape(B, 2 * ML_HEADS, ML_PAD, CAUG), m_b.reshape(B, 2 * ML_HEADS, LANE)


def _unpack_ml_state(c_aug, m_b):
    B = c_aug.shape[0]
    c_aug = c_aug.reshape(B, 2, ML_HEADS, ML_PAD, CAUG)
    return (c_aug[:, :, :, :ML_DIM, :ML_DIM], c_aug[:, :, :, :ML_DIM, ML_DIM],
            m_b.reshape(B, 2, ML_HEADS, LANE)[..., 0])


def _pool_rows(u_prev, u_cur, u_next, w_bd, scale, t0, seq_len):
    tm = u_cur.shape[0]
    u_win = jnp.concatenate([u_prev, u_cur, u_next], axis=0)
    u_hi = u_win.astype(bf16)
    u_lo = (u_win - u_hi.astype(f32)).astype(bf16)
    lane = lax.broadcasted_iota(jnp.int32, (1, LANE), 1)
    blocks = []
    for r0 in range(0, tm, POOL_BLOCK):
        win = slice(r0, r0 + POOL_BLOCK + 2 * POOL_HALO)
        t_abs = t0 + r0 + lax.broadcasted_iota(jnp.int32, (POOL_BLOCK, 1), 0)
        s_abs = t0 + r0 - POOL_HALO + lax.broadcasted_iota(jnp.int32, (1, POOL_BLOCK + 2 * POOL_HALO), 1)
        t_loc = t_abs & (seq_len - 1)
        seq_start = t_abs - t_loc
        means = []
        for w in POOL_WINDOWS:
            lo = jnp.maximum(t_loc - w // 2, 0)
            hi = jnp.minimum(t_loc - w // 2 + w, seq_len)
            in_win = (s_abs >= seq_start + lo) & (s_abs < seq_start + hi)
            means.append((jnp.where(in_win, 1.0, 0.0).astype(bf16), 1.0 / (hi - lo).astype(f32)))
        pooled = []
        for p in range(POOL_GROUPS // 2):
            sl = slice(p * LANE, (p + 1) * LANE)
            halves = []
            for a, inv_cnt in means[2 * p:2 * p + 2]:
                tot = (jnp.dot(a, u_hi[win, sl], preferred_element_type=f32)
                       + jnp.dot(a, u_lo[win, sl], preferred_element_type=f32))
                halves.append(tot * inv_cnt)
            pooled.append(jnp.where(lane < POOL_DIM, halves[0], halves[1]) - u_cur[r0:r0 + POOL_BLOCK, sl])
        blocks.append(jnp.concatenate(pooled, axis=1))
    pooled = jnp.concatenate(blocks, axis=0).astype(bf16)
    return jnp.dot(pooled, w_bd, preferred_element_type=f32) * scale


def _top2_sum(a, b, c, d):
    hi1, lo1 = jnp.maximum(a, b), jnp.minimum(a, b)
    hi2, lo2 = jnp.maximum(c, d), jnp.minimum(c, d)
    return jnp.maximum(hi1, hi2) + jnp.maximum(jnp.minimum(hi1, hi2), jnp.maximum(lo1, lo2))


def _first_match(vals, target):
    idx = jnp.full_like(target, float(len(vals) - 1))
    for i in range(len(vals) - 2, -1, -1):
        idx = jnp.where(vals[i] == target, float(i), idx)
    return idx


def _pick(vals, idx):
    out = vals[-1]
    for i in range(len(vals) - 2, -1, -1):
        out = jnp.where(idx == float(i), vals[i], out)
    return out


def _route(logits_t, bias_t):
    scores = jax.nn.sigmoid(logits_t)
    sel = scores + bias_t
    row = lambda a, i: a[i:i + 1, :]
    grp = [_top2_sum(*[row(sel, EXPERTS_PER_GROUP * g + i) for i in range(EXPERTS_PER_GROUP)])
           for g in range(N_EXPERT_GROUPS)]
    best = functools.reduce(jnp.maximum, grp)
    gidx = _first_match(grp, best)
    sel_g = [_pick([row(sel, EXPERTS_PER_GROUP * g + i) for g in range(N_EXPERT_GROUPS)], gidx)
             for i in range(EXPERTS_PER_GROUP)]
    sco_g = [_pick([row(scores, EXPERTS_PER_GROUP * g + i) for g in range(N_EXPERT_GROUPS)], gidx)
             for i in range(EXPERTS_PER_GROUP)]
    i0 = _first_match(sel_g, functools.reduce(jnp.maximum, sel_g))
    rest = [jnp.where(i0 == float(i), -jnp.inf, sel_g[i]) for i in range(EXPERTS_PER_GROUP)]
    i1 = _first_match(rest, functools.reduce(jnp.maximum, rest))
    s0, s1 = _pick(sco_g, i0), _pick(sco_g, i1)
    tot = s0 + s1
    rid = lax.broadcasted_iota(jnp.int32, (LANE, logits_t.shape[1]), 0)
    rows = (EXPERTS_PER_GROUP * gidx + i0, EXPERTS_PER_GROUP * gidx + i1, s0 / tot, s1 / tot)
    out = jnp.zeros(rid.shape, f32)
    for i, r in enumerate(rows):
        out = jnp.where(rid == i, r, out)
    return out


def _out_kernel(xc_ref, xl_ref, mod_ref, oac_ref, oal_ref, hf_ref, hb_ref, ob_ref, up_ref, uc_ref, un_ref, wp_ref, psc_ref,
                mln_ref, wo_ref, n2_ref, wr_ref, br_ref, x1_ref, h2_ref, rt_ref, rc_ref):
    tm = xc_ref.shape[0]
    i = pl.program_id(0)
    is_ctx = i < N_CTX // tm
    mod = mod_ref[0]
    out_a = jnp.where(is_ctx, oac_ref[...], oal_ref[...])
    out_c = _pool_rows(up_ref[...], uc_ref[...], un_ref[...], wp_ref[...], psc_ref[...], i * tm,
                       jnp.where(is_ctx, SEQ, DEC_SEQ)).astype(bf16)
    hsum = hf_ref[...] + hb_ref[...]
    outs_b = []
    for hd in range(ML_HEADS):
        sl = slice(hd * ML_PAD, (hd + 1) * ML_PAD)
        hh = hsum[:, sl]
        ms = jnp.sum(hh * hh, axis=-1, keepdims=True) * (1.0 / ML_DIM)
        hn = hh * lax.rsqrt(ms + EPS) * mln_ref[:, sl]
        outs_b.append((jax.nn.sigmoid(ob_ref[:, sl].astype(f32)) * hn).astype(bf16))
    out_b = jnp.concatenate(outs_b, axis=1)
    mixed = (jnp.dot(out_a, wo_ref[0:NA_WIDTH, :], preferred_element_type=f32)
             + jnp.dot(out_b, wo_ref[NA_WIDTH:NA_WIDTH + ML_PW, :], preferred_element_type=f32)
             + jnp.dot(out_c, wo_ref[NA_WIDTH + ML_PW:, :], preferred_element_type=f32))
    x1 = jnp.where(is_ctx, xc_ref[...], xl_ref[...]) + mod[2:3] * mixed
    x1_ref[...] = x1
    h2 = x1 * lax.rsqrt(jnp.mean(x1 * x1, axis=-1, keepdims=True) + EPS) * n2_ref[...]
    h2 = h2 * (1.0 + mod[4:5]) + mod[3:4]
    h2_ref[...] = h2.astype(bf16)
    route_t = _route(_nt(wr_ref[...], h2, precision=HI), br_ref[...])
    rt_ref[...] = route_t[0:8]
    rc_ref[...] = route_t.T


def _out_proj(x_ctx, x_lat, x_lat_block0, mods, layer, oa_ctx, oa_lat, hf, hb, qvo, pin, w_bd, psc, mln, wo, n2,
              wr_t, br_t):
    tm = TOK_TILE
    const = lambda i: (0, 0)
    lyr = lambda shape: pl.BlockSpec((None,) + shape, lambda i: (layer, 0, 0))
    row = lambda i: (i, 0)
    n_ctx_tiles = N_CTX // tm
    halo_blocks = tm // POOL_HALO
    return pl.pallas_call(
        _out_kernel,
        grid=(N_TOK // tm,),
        in_specs=[pl.BlockSpec((tm, D_MODEL), lambda i: (jnp.minimum(i, n_ctx_tiles - 1), 0)),
                  pl.BlockSpec((tm, D_MODEL), lambda i: (jnp.maximum(i - n_ctx_tiles, 0) + x_lat_block0, 0)),
                  pl.BlockSpec((1, 6, D_MODEL), lambda i: (_mod_row(i, tm), 0, 0)),
                  pl.BlockSpec((tm, NA_WIDTH), lambda i: (jnp.minimum(i, n_ctx_tiles - 1), 0)),
                  pl.BlockSpec((tm, NA_WIDTH), lambda i: (jnp.maximum(i - n_ctx_tiles, 0), 0)),
                  pl.BlockSpec((tm, ML_PW), row),
                  pl.BlockSpec((tm, ML_PW), row),
                  pl.BlockSpec((tm, ML_PW), lambda i: (i, 2)),
                  pl.BlockSpec((POOL_HALO, POOL_WIDTH), lambda i: (jnp.maximum(i * halo_blocks - 1, 0), 0)),
                  pl.BlockSpec((tm, POOL_WIDTH), row),
                  pl.BlockSpec((POOL_HALO, POOL_WIDTH),
                               lambda i: (jnp.minimum((i + 1) * halo_blocks, N_TOK // POOL_HALO - 1), 0)),
                  lyr((POOL_WIDTH, POOL_WIDTH)), lyr((1, POOL_WIDTH)), lyr((1, ML_PW)),
                  lyr((NA_WIDTH + ML_PW + POOL_WIDTH, D_MODEL)), lyr((1, D_MODEL)),
                  pl.BlockSpec((N_EXPERTS, D_MODEL), const),
                  pl.BlockSpec((N_EXPERTS, 1), const)],
        out_specs=[pl.BlockSpec((tm, D_MODEL), row),
                   pl.BlockSpec((tm, D_MODEL), row),
                   pl.BlockSpec((8, tm), lambda i: (0, i)),
                   pl.BlockSpec((tm, LANE), row)],
        out_shape=[jax.ShapeDtypeStruct((N_TOK, D_MODEL), f32),
                   jax.ShapeDtypeStruct((N_TOK, D_MODEL), bf16),
                   jax.ShapeDtypeStruct((8, N_TOK), f32),
                   jax.ShapeDtypeStruct((N_TOK, LANE), f32)],
        compiler_params=pltpu.CompilerParams(dimension_semantics=("arbitrary",), vmem_limit_bytes=VMEM_LIMIT),
        name="out_proj_router",
    )(x_ctx, x_lat, mods, oa_ctx, oa_lat, hf, hb, qvo, pin, pin, pin, w_bd, psc, mln, wo, n2, wr_t, br_t)


def _ceil_to(x, m):
    return jnp.floor((x + (m - 1)) * (1.0 / m)) * m


def _prefix_over_experts(v):
    er = lax.broadcasted_iota(jnp.int32, (N_EXPERTS, N_EXPERTS), 0)
    ec = lax.broadcasted_iota(jnp.int32, (N_EXPERTS, N_EXPERTS), 1)
    return jnp.dot(jnp.where(ec < er, 1.0, 0.0), v, preferred_element_type=f32, precision=HI)


def _experts_to_lanes(v):
    sub = lax.broadcasted_iota(jnp.int32, (N_EXPERTS, LANE), 0)
    lane = lax.broadcasted_iota(jnp.int32, (N_EXPERTS, LANE), 1)
    return jnp.sum(jnp.where(sub == lane, v, 0.0), axis=0, keepdims=True)


def _rank_kernel(rt_ref, pos_ref, te_ref, tab_ref, carry_ref):
    tm = rt_ref.shape[1]
    p = pl.program_id(0)
    i = pl.program_id(1)
    rid = lax.broadcasted_iota(jnp.int32, (N_EXPERTS, tm), 0).astype(f32)
    oh0 = rid == rt_ref[0:1, :]
    oh1 = rid == rt_ref[1:2, :]
    both = jnp.where(oh0 | oh1, 1.0, 0.0)
    runs = jnp.broadcast_to(_ceil_to(jnp.sum(both, axis=1, keepdims=True), MOE_CHUNK), (N_EXPERTS, LANE))

    @pl.when((p == 0) & (i == 0))
    def _():
        carry_ref[...] = jnp.zeros_like(carry_ref)

    @pl.when((p == 1) & (i == 0))
    def _():
        padded = _ceil_to(carry_ref[...], MOE_TILE)
        off = _prefix_over_experts(padded)
        carry_ref[...] = off
        total = jnp.sum(padded, axis=0, keepdims=True)
        n_used = total * (1.0 / MOE_TILE)
        tile = lax.broadcasted_iota(jnp.int32, (1, LANE), 1).astype(f32)
        row0 = jnp.minimum(tile, n_used - 1.0) * MOE_TILE
        expert = jnp.sum(jnp.where(off <= row0, 1.0, 0.0), axis=0, keepdims=True) - 1.0
        sub = lax.broadcasted_iota(jnp.int32, (8, LANE), 0)
        te_ref[...] = jnp.where(sub == 0, expert, jnp.where(sub == 1, n_used, 0.0)).astype(jnp.int32)

    @pl.when(p == 1)
    def _():
        sr = lax.broadcasted_iota(jnp.int32, (tm, tm), 0)
        sc = lax.broadcasted_iota(jnp.int32, (tm, tm), 1)
        earlier = jnp.dot(both.astype(bf16), jnp.where(sr < sc, 1.0, 0.0).astype(bf16),
                          preferred_element_type=f32)
        g_off = carry_ref[...]
        l_off = _prefix_over_experts(runs)
        g_row = g_off[:, 0:1] + earlier
        l_row = l_off[:, 0:1] + earlier
        pick = lambda oh, v: jnp.sum(jnp.where(oh, v, 0.0), axis=0, keepdims=True)
        rows = (pick(oh0, g_row), pick(oh1, g_row), pick(oh0, l_row), pick(oh1, l_row))
        sub = lax.broadcasted_iota(jnp.int32, (8, tm), 0)
        out = jnp.zeros((8, tm), f32)
        for k, r in enumerate(rows):
            out = jnp.where(sub == k, r, out)
        pos_ref[...] = out.astype(jnp.int32)
        sub = lax.broadcasted_iota(jnp.int32, (8, LANE), 0)
        tab = jnp.zeros((8, LANE), f32)
        for k, v in enumerate((runs * (1.0 / MOE_CHUNK), l_off, g_off)):
            tab = jnp.where(sub == k, _experts_to_lanes(v), tab)
        tab_ref[0] = tab.astype(jnp.int32)

    carry_ref[...] += runs


def _rank(route_t):
    tm = TOK_TILE
    n_tiles = N_TOK // tm
    return pl.pallas_call(
        _rank_kernel,
        grid=(2, n_tiles),
        in_specs=[pl.BlockSpec((8, tm), lambda p, i: (0, i))],
        out_specs=[pl.BlockSpec((8, tm), lambda p, i: (0, i * p)),
                   pl.BlockSpec((8, LANE), lambda p, i: (0, 0)),
                   pl.BlockSpec((1, 8, LANE), lambda p, i: (i * p, 0, 0))],
        out_shape=[jax.ShapeDtypeStruct((8, N_TOK), jnp.int32),
                   jax.ShapeDtypeStruct((8, LANE), jnp.int32),
                   jax.ShapeDtypeStruct((n_tiles, 8, LANE), jnp.int32)],
        scratch_shapes=[pltpu.VMEM((N_EXPERTS, LANE), f32)],
        compiler_params=pltpu.CompilerParams(dimension_semantics=("arbitrary", "arbitrary")),
        name="moe_rank",
    )(route_t)


def _dispatch_kernel(tab_ref, h_ref, rows_ref, xs_in_ref, xs_ref, loc, sem):
    del xs_in_ref
    tm = h_ref.shape[0]
    i = pl.program_id(0)
    rid = lax.broadcasted_iota(jnp.int32, (MOE_LOCAL_ROWS, tm), 0)
    sel = (rid == rows_ref[2:3, :]) | (rid == rows_ref[3:4, :])
    loc[...] = jnp.dot(jnp.where(sel, 1.0, 0.0).astype(bf16), h_ref[...], preferred_element_type=f32).astype(bf16)

    def chunk_copy(src_row, dst_row):
        return pltpu.make_async_copy(loc.at[pl.ds(pl.multiple_of(src_row, MOE_CHUNK), MOE_CHUNK)],
                                     xs_ref.at[pl.ds(pl.multiple_of(dst_row, MOE_CHUNK), MOE_CHUNK)], sem)

    n_total = 0
    for e in range(N_EXPERTS):
        n_chunks, l_off, g_off = (tab_ref[(3 * i + k) * N_EXPERTS + e] for k in range(3))

        def issue(c, carry, l_off=l_off, g_off=g_off):
            chunk_copy(l_off + c * MOE_CHUNK, g_off + c * MOE_CHUNK).start()
            return carry

        lax.fori_loop(0, n_chunks, issue, 0)
        n_total = n_total + n_chunks

    def wait_one(c, carry):
        chunk_copy(0, 0).wait()
        return carry

    lax.fori_loop(0, n_total, wait_one, 0)


def _dispatch(run_table, h2, rows, xs_init):
    tm = TOK_TILE
    return pl.pallas_call(
        _dispatch_kernel,
        grid_spec=pltpu.PrefetchScalarGridSpec(
            num_scalar_prefetch=1,
            grid=(N_TOK // tm,),
            in_specs=[pl.BlockSpec((tm, D_MODEL), lambda i, tab: (i, 0)),
                      pl.BlockSpec((8, tm), lambda i, tab: (0, i)),
                      pl.BlockSpec(memory_space=pl.ANY)],
            out_specs=pl.BlockSpec(memory_space=pl.ANY),
            scratch_shapes=[pltpu.VMEM((MOE_LOCAL_ROWS, D_MODEL), bf16), pltpu.SemaphoreType.DMA(())]),
        out_shape=jax.ShapeDtypeStruct(xs_init.shape, xs_init.dtype),
        input_output_aliases={3: 0},
        compiler_params=pltpu.CompilerParams(dimension_semantics=("arbitrary",), vmem_limit_bytes=VMEM_LIMIT),
        name="moe_dispatch",
    )(run_table, h2, rows, xs_init)


def _expert_kernel(te_ref, xs_ref, wg_hbm, wu_hbm, wd_hbm, ys_ref, wg_f32, wu_f32, wd_f32, wg_bf, wu_bf, wd_bf,
                   slot_ref, sem, *, layer):
    j = pl.program_id(0)
    n_used = te_ref[1, 0]
    used = j < n_used
    expert = te_ref[0, j]
    new_expert = jnp.logical_or(j == 0, expert != te_ref[0, jnp.maximum(j - 1, 0)])

    def weight_copies(e, slot):
        return [pltpu.make_async_copy(hbm.at[layer, e], buf.at[slot], sem.at[slot])
                for hbm, buf in ((wg_hbm, wg_f32), (wu_hbm, wu_f32), (wd_hbm, wd_f32))]

    @pl.when(j == 0)
    def _():
        slot_ref[0] = 1
        for cp in weight_copies(expert, 0):
            cp.start()

    @pl.when(jnp.logical_not(used))
    def _():
        ys_ref[...] = jnp.zeros_like(ys_ref)

    @pl.when(used & new_expert)
    def _():
        slot = 1 - slot_ref[0]
        slot_ref[0] = slot
        for cp in weight_copies(expert, slot):
            cp.wait()
        wg_bf[...] = wg_f32[slot].astype(bf16)
        wu_bf[...] = wu_f32[slot].astype(bf16)
        wd_bf[...] = wd_f32[slot].astype(bf16)
        nxt = lax.while_loop(lambda t: (t < n_used) & (te_ref[0, jnp.minimum(t, LANE - 1)] == expert),
                             lambda t: t + 1, j + 1)

        @pl.when(nxt < n_used)
        def _():
            for cp in weight_copies(te_ref[0, nxt], 1 - slot):
                cp.start()

    @pl.when(used)
    def _():
        x = xs_ref[...]
        hg = jnp.dot(x, wg_bf[...], preferred_element_type=f32)
        hu = jnp.dot(x, wu_bf[...], preferred_element_type=f32)
        hid = (hg * jax.nn.sigmoid(hg) * hu).astype(bf16)
        ys_ref[...] = jnp.dot(hid, wd_bf[...], preferred_element_type=f32)


def _experts(te, xs, w_gate, w_up, w_down, layer):
    tm = MOE_TILE
    row = lambda j, te: (jnp.minimum(j, te[1, 0] - 1), 0)
    hbm = pl.BlockSpec(memory_space=pl.ANY)
    return pl.pallas_call(
        functools.partial(_expert_kernel, layer=layer),
        grid_spec=pltpu.PrefetchScalarGridSpec(
            num_scalar_prefetch=1,
            grid=(MOE_ROWS // tm,),
            in_specs=[pl.BlockSpec((tm, D_MODEL), row), hbm, hbm, hbm],
            out_specs=pl.BlockSpec((tm, D_MODEL), lambda j, te: (j, 0)),
            scratch_shapes=[pltpu.VMEM((2, D_MODEL, D_EXPERT), f32), pltpu.VMEM((2, D_MODEL, D_EXPERT), f32),
                            pltpu.VMEM((2, D_EXPERT, D_MODEL), f32),
                            pltpu.VMEM((D_MODEL, D_EXPERT), bf16), pltpu.VMEM((D_MODEL, D_EXPERT), bf16),
                            pltpu.VMEM((D_EXPERT, D_MODEL), bf16),
                            pltpu.SMEM((1,), jnp.int32), pltpu.SemaphoreType.DMA((2,))]),
        out_shape=jax.ShapeDtypeStruct((MOE_ROWS, D_MODEL), f32),
        compiler_params=pltpu.CompilerParams(dimension_semantics=("arbitrary",), vmem_limit_bytes=VMEM_LIMIT),
        name="moe_experts",
    )(te, xs, w_gate, w_up, w_down)


def _gather_expert_rows(pos0_ref, pos1_ref, ys_ref, buf, sem):
    rows = buf.shape[2]
    i = pl.program_id(0)
    slot = i % 2

    def issue(tile, sl):
        base = tile * rows

        def body(t, carry):
            for s, pos_ref in enumerate((pos0_ref, pos1_ref)):
                pltpu.make_async_copy(ys_ref.at[pl.ds(pos_ref[base + t], 1)], buf.at[sl, s, pl.ds(t, 1)],
                                      sem.at[sl]).start()
            return carry

        lax.fori_loop(0, rows, body, 0, unroll=8)

    @pl.when(i == 0)
    def _():
        issue(0, 0)

    @pl.when(i + 1 < pl.num_programs(0))
    def _():
        issue(i + 1, 1 - slot)

    for s in range(2):
        pltpu.make_async_copy(ys_ref.at[pl.ds(0, rows)], buf.at[slot, s], sem.at[slot]).wait()
    return buf[slot, 0], buf[slot, 1]


def _moe_residual(pos0_ref, pos1_ref, ys_ref, x1_ref, rc_ref, mod_ref, buf, sem):
    y0, y1 = _gather_expert_rows(pos0_ref, pos1_ref, ys_ref, buf, sem)
    rc = rc_ref[...]
    return x1_ref[...] + mod_ref[0][5:6] * (rc[:, 2:3] * y0 + rc[:, 3:4] * y1)


def _final_kernel(pos0_ref, pos1_ref, ys_ref, x1_ref, rc_ref, mod_ref, fn_ref, yc_ref, yl_ref, buf, sem):
    x2 = _moe_residual(pos0_ref, pos1_ref, ys_ref, x1_ref, rc_ref, mod_ref, buf, sem)
    y = x2 * lax.rsqrt(jnp.mean(x2 * x2, axis=-1, keepdims=True) + EPS) * fn_ref[...]
    is_ctx = pl.program_id(0) < N_CTX // x1_ref.shape[0]

    @pl.when(is_ctx)
    def _():
        yc_ref[...] = y

    @pl.when(jnp.logical_not(is_ctx))
    def _():
        yl_ref[...] = y


def _final_combine(pos0, pos1, ys, x1, rc, mods, fn):
    tc = COMBINE_TILE
    row = lambda i, p0, p1: (i, 0)
    n_ctx_tiles = N_CTX // tc
    return pl.pallas_call(
        _final_kernel,
        grid_spec=pltpu.PrefetchScalarGridSpec(
            num_scalar_prefetch=2,
            grid=(N_TOK // tc,),
            in_specs=[pl.BlockSpec(memory_space=pl.ANY),
                      pl.BlockSpec((tc, D_MODEL), row),
                      pl.BlockSpec((tc, LANE), row),
                      pl.BlockSpec((1, 6, D_MODEL), lambda i, p0, p1: (_mod_row(i, tc), 0, 0)),
                      pl.BlockSpec((1, D_MODEL), lambda i, p0, p1: (0, 0))],
            out_specs=[pl.BlockSpec((tc, D_MODEL), lambda i, p0, p1: (jnp.minimum(i, n_ctx_tiles - 1), 0)),
                       pl.BlockSpec((tc, D_MODEL), lambda i, p0, p1: (jnp.maximum(i - n_ctx_tiles, 0), 0))],
            scratch_shapes=[pltpu.VMEM((2, 2, tc, D_MODEL), f32), pltpu.SemaphoreType.DMA((2,))]),
        out_shape=[jax.ShapeDtypeStruct((N_CTX, D_MODEL), f32), jax.ShapeDtypeStruct((N_LAT, D_MODEL), f32)],
        compiler_params=pltpu.CompilerParams(dimension_semantics=("arbitrary",), vmem_limit_bytes=VMEM_LIMIT),
        name="moe_combine_final",
    )(pos0, pos1, ys, x1, rc, mods, fn)


def _moe_experts(h2, route_t, w_gate, w_up, w_down, layer, xs_buf):
    rows, te, runs = _rank(route_t)
    xs = _dispatch(runs[:, :3, :N_EXPERTS].reshape(-1), h2, rows, xs_buf)
    return rows[0], rows[1], _experts(te, xs, w_gate, w_up, w_down, layer), xs


def _pad_heads(w):
    lead = w.shape[:-1]
    w = w.reshape(*lead, ML_HEADS, ML_DIM)
    w = jnp.pad(w, [(0, 0)] * len(lead) + [(0, 0), (0, ML_PAD - ML_DIM)])
    return w.reshape(*lead, ML_PW)


def _pack_in_cols(wb):
    o = 0
    qa = wb[..., o:o + NA_WIDTH] * (NA_DIM ** -0.5)
    ka = wb[..., o + NA_WIDTH:o + 2 * NA_WIDTH]
    va = wb[..., o + 2 * NA_WIDTH:o + 3 * NA_WIDTH]
    o += 3 * NA_WIDTH
    qb, kb, vb, ob = [_pad_heads(wb[..., o + j * ML_WIDTH:o + (j + 1) * ML_WIDTH]) for j in range(4)]
    o += 4 * ML_WIDTH
    gates = wb[..., o:o + N_GATE_COLS]
    o += N_GATE_COLS
    pool = wb[..., o:o + POOL_WIDTH]
    main = jnp.concatenate([qa, ka, va, qb, vb, ob, pool], axis=-1)
    gates_p = jnp.pad(gates, [(0, 0)] * (gates.ndim - 1) + [(0, LANE - N_GATE_COLS)])
    return main, gates_p, jnp.concatenate([kb, gates], axis=-1)


def _pack_w_in(w, b):
    w_main, w_gates, w_feat = _pack_in_cols(w)
    b_main, b_gates, b_feat = _pack_in_cols(b.astype(f32))
    return (w_main.astype(bf16), b_main[:, None], w_gates.astype(bf16), b_gates[:, None],
            jnp.swapaxes(w_feat, 1, 2).astype(bf16), b_feat[:, :, None])


def _pack_w_out(w):
    n_l = w.shape[0]
    wb = w[:, NA_WIDTH:NA_WIDTH + ML_WIDTH].reshape(n_l, ML_HEADS, ML_DIM, D_MODEL)
    wb = jnp.pad(wb, ((0, 0), (0, 0), (0, ML_PAD - ML_DIM), (0, 0))).reshape(n_l, ML_PW, D_MODEL)
    return jnp.concatenate([w[:, :NA_WIDTH], wb, w[:, NA_WIDTH + ML_WIDTH:]], axis=1).astype(bf16)


def _block_diag(w):
    n_l, g, c, _ = w.shape
    eye = jnp.eye(g, dtype=w.dtype)
    return (eye[None, :, None, :, None] * w[:, :, :, None, :]).reshape(n_l, g * c, g * c)


def kernel(x_prompt, x_sample, cache_k_attn, cache_v_attn, state_mlstm_C, state_mlstm_n, state_mlstm_m, c, c_ctx,
           w_ada, b_ada, norm1, w_in, b_in, rpb, ml_norm, w_pool, pool_scale, w_out, norm2, w_router, b_router,
           w_gate, w_up, w_down, final_norm):
    dt = x_prompt.dtype
    x_ctx = x_prompt.reshape(N_CTX, D_MODEL).astype(f32)
    x_lat = x_sample.reshape(N_LAT, D_MODEL).astype(f32)
    x_lat_block0 = 0
    cvec = jnp.concatenate([c_ctx[None], c, jnp.zeros((8 - 1 - DEC_BATCH, D_MODEL), c.dtype)], axis=0).astype(f32)
    mods_all = _ada(cvec, w_ada.astype(f32), b_ada.astype(f32))
    mods_all = mods_all[:, :1 + DEC_BATCH].reshape(DEPTH, 1 + DEC_BATCH, 6, D_MODEL)

    wr_t = w_router.astype(f32).T
    br_t = b_router.astype(f32)[:, None]
    fn = final_norm.astype(f32)[None]

    na_bias = _na_bias_tables(rpb)
    xs_buf = jnp.zeros((MOE_ROWS, D_MODEL), bf16)
    in_params = (norm1.astype(f32)[:, None],) + _pack_w_in(w_in, b_in)
    out_params = (_block_diag(w_pool.astype(f32)).astype(bf16), pool_scale.astype(f32)[:, None],
                  _pad_heads(ml_norm.astype(f32))[:, None], _pack_w_out(w_out), norm2.astype(f32)[:, None])

    kv_cache = jnp.zeros((DEPTH, N_CTX + TOK_TILE, 2 * NA_WIDTH), f32)
    Cs, ns, ms = [], [], []
    pending = None
    for l in range(DEPTH):
        mods = mods_all[l]
        if pending is None:
            qkva, kv_cache, qvo, kt, gates, gates_t, pin = _in_proj(x_ctx, x_lat, mods, l, *in_params, kv_cache)
        else:
            x, qkva, kv_cache, qvo, kt, gates, gates_t, pin = _moe_in_proj(*pending, mods_all[l - 1], mods, l,
                                                                           *in_params, kv_cache)
            x_ctx, x_lat, x_lat_block0 = x, x, N_CTX // TOK_TILE

        oa_ctx = _ctx_attention(qkva.reshape(N_TOK // SEQ, SEQ, W_A))
        ck = (cache_k_attn[:, l].reshape(DEC_BATCH, PAST_LEN, NA_WIDTH)).astype(bf16)
        cv = (cache_v_attn[:, l].reshape(DEC_BATCH, PAST_LEN, NA_WIDTH)).astype(bf16)
        oa_lat = _neighborhood_attention(qkva.reshape(N_TOK // DEC_SEQ, DEC_SEQ, W_A), ck, cv, na_bias, l)

        c_l, m_l = _pack_ml_state(state_mlstm_C[:, l], state_mlstm_n[:, l], state_mlstm_m[:, l])
        hf, hb, c_fin, m_fin = _mlstm(qvo, kt, gates, gates_t, c_l, m_l)
        C_l, n_l, m_l2 = _unpack_ml_state(c_fin[:BATCH], m_fin[:BATCH])
        Cs.append(C_l)
        ns.append(n_l)
        ms.append(m_l2)

        x1, h2, route_t, rc = _out_proj(x_ctx, x_lat, x_lat_block0, mods, l,
                                        oa_ctx.reshape(N_CTX, NA_WIDTH), oa_lat.reshape(N_LAT, NA_WIDTH),
                                        hf.reshape(N_TOK, ML_PW), hb.reshape(N_TOK, ML_PW), qvo, pin,
                                        *out_params, wr_t, br_t)
        pos0, pos1, ys, xs_buf = _moe_experts(h2, route_t, w_gate, w_up, w_down, l, xs_buf)
        pending = (pos0, pos1, ys, x1, rc)

    x = _final_combine(*pending, mods_all[DEPTH - 1], fn)
    y_prompt = x[0].reshape(BATCH, SEQ, D_MODEL).astype(dt)
    y_sample = x[1].reshape(DEC_BATCH, DEC_SEQ, D_MODEL).astype(dt)
    new_kv = kv_cache[:, :N_CTX].reshape(DEPTH, BATCH, SEQ, 2, NA_HEADS, NA_DIM).transpose(3, 1, 0, 2, 4, 5)
    return (y_prompt, y_sample, new_kv[0].astype(dt), new_kv[1].astype(dt),
            jnp.stack(Cs, axis=1).astype(dt), jnp.stack(ns, axis=1).astype(dt), jnp.stack(ms, axis=1).astype(dt))
```

```python
import functools

import numpy as np
import jax
import jax.numpy as jnp
from jax import lax
from jax.experimental import pallas as pl
from jax.experimental.pallas import tpu as pltpu

D_MODEL = 1024
BATCH = 16
SEQ = 256
DEPTH = 4
DEC_BATCH = 2
DEC_SEQ = 4096
PAST_LEN = 256
GRID_W = 64
EPS = 1e-6
NEG_INF = -1e30
NA_HEADS = 6
NA_DIM = 64
NA_WIDTH = NA_HEADS * NA_DIM
NA_ROWS = 8
NA_COLS = 16
RPB_ROWS = 2 * NA_ROWS - 1
RPB_COLS = 2 * NA_COLS - 1
ML_HEADS = 4
ML_DIM = 96
ML_WIDTH = ML_HEADS * ML_DIM
POOL_WINDOWS = (2, 4, 8, 16)
POOL_GROUPS = 4
POOL_DIM = 64
POOL_WIDTH = POOL_GROUPS * POOL_DIM
N_GATE_COLS = 4 * ML_HEADS
N_EXPERTS = 16
N_EXPERT_GROUPS = 4
EXPERTS_PER_GROUP = N_EXPERTS // N_EXPERT_GROUPS
D_EXPERT = 512
ADA_DIM = 6 * D_MODEL

N_CTX = BATCH * SEQ
N_LAT = DEC_BATCH * DEC_SEQ
N_TOK = N_CTX + N_LAT
LANE = 128
ML_PAD = LANE
ML_PW = ML_HEADS * ML_PAD
CAUG = ML_PAD
NA_PAIRS = NA_HEADS // 2
TOK_TILE = 512
ML_CHUNK = 256
NA_QROWS = 4
NA_KROWS = NA_QROWS + NA_ROWS - 1
POOL_HALO = max(POOL_WINDOWS) // 2
POOL_BLOCK = 128
MOE_TILE = 512
MOE_CHUNK = 16
MOE_LOCAL_ROWS = -(-(2 * TOK_TILE + N_EXPERTS * (MOE_CHUNK - 1)) // LANE) * LANE
MOE_ROWS = -(-(2 * N_TOK + (N_TOK // TOK_TILE) * N_EXPERTS * (MOE_CHUNK - 1) + N_EXPERTS * (MOE_TILE - 1))
             // MOE_TILE) * MOE_TILE
COMBINE_TILE = 256
VMEM_LIMIT = 56 * 1024 * 1024

W_A = 3 * NA_WIDTH
W_B = 3 * ML_PW
N_TCOLS = ML_PW + N_GATE_COLS
W_MAIN = W_A + W_B + POOL_WIDTH

f32 = jnp.float32
bf16 = jnp.bfloat16
HI = lax.Precision.HIGHEST


def _nt(a, b, **kw):
    return lax.dot_general(a, b, (((1,), (1,)), ((), ())), preferred_element_type=f32, **kw)


def _mod_row(i, tile):
    n_ctx_tiles = N_CTX // tile
    per_batch = DEC_SEQ // tile
    return jnp.where(i < n_ctx_tiles, 0, 1 + (i - n_ctx_tiles) // per_batch)


def _ada_kernel(c_ref, w_ref, b_ref, o_ref):
    s = c_ref[...]
    s = s * jax.nn.sigmoid(s)
    o_ref[0] = jnp.dot(s.astype(bf16), w_ref[0].astype(bf16), preferred_element_type=f32) + b_ref[0]


def _ada(cvec, w_ada, b_ada):
    nj = ADA_DIM // D_MODEL
    return pl.pallas_call(
        _ada_kernel,
        grid=(DEPTH, nj),
        in_specs=[pl.BlockSpec((8, D_MODEL), lambda l, j: (0, 0)),
                  pl.BlockSpec((1, D_MODEL, D_MODEL), lambda l, j: (l, 0, j)),
                  pl.BlockSpec((1, 1, D_MODEL), lambda l, j: (l, 0, j))],
        out_specs=pl.BlockSpec((1, 8, D_MODEL), lambda l, j: (l, 0, j)),
        out_shape=jax.ShapeDtypeStruct((DEPTH, 8, ADA_DIM), f32),
        name="ada_mod",
    )(cvec, w_ada, b_ada.reshape(DEPTH, 1, ADA_DIM))


def _in_kernel(xc_ref, xl_ref, *refs):
    is_ctx = pl.program_id(0) < N_CTX // xc_ref.shape[0]
    _in_body(jnp.where(is_ctx, xc_ref[...], xl_ref[...]), *refs)


def _moe_in_kernel(pos0_ref, pos1_ref, ys_ref, x1_ref, rc_ref, mod_prev_ref, *refs):
    in_refs, x_out_ref, out_refs, (buf, sem) = refs[:9], refs[9], refs[10:-2], refs[-2:]
    x = _moe_residual(pos0_ref, pos1_ref, ys_ref, x1_ref, rc_ref, mod_prev_ref, buf, sem)
    x_out_ref[...] = x
    _in_body(x, *in_refs, *out_refs)


def _in_body(x, mod_ref, n1_ref, w_ref, b_ref, wg_ref, bg_ref, wt_ref, bt_ref, kv_in_ref,
             a_ref, kv_ref, b_out_ref, kt_ref, g_ref, gt_ref, pin_ref):
    del kv_in_ref
    mod = mod_ref[0]
    h = x * lax.rsqrt(jnp.mean(x * x, axis=-1, keepdims=True) + EPS) * n1_ref[...]
    h = (h * (1.0 + mod[1:2]) + mod[0:1]).astype(bf16)
    pa = jnp.dot(h, w_ref[:, 0:W_A], preferred_element_type=f32) + b_ref[:, 0:W_A]
    a_ref[...] = pa.astype(bf16)
    kv_ref[...] = pa[:, NA_WIDTH:W_A]
    for j in range(3):
        lo = W_A + j * ML_PW
        pb = jnp.dot(h, w_ref[:, lo:lo + ML_PW], preferred_element_type=f32) + b_ref[:, lo:lo + ML_PW]
        if j == 0:
            pb = pb * (ML_DIM ** -0.5)
        b_out_ref[:, j * ML_PW:(j + 1) * ML_PW] = pb.astype(bf16)
    lo = W_A + W_B
    pin_ref[...] = jnp.dot(h, w_ref[:, lo:lo + POOL_WIDTH], preferred_element_type=f32) + b_ref[:, lo:lo + POOL_WIDTH]
    g_ref[...] = jnp.dot(h, wg_ref[...], preferred_element_type=f32) + bg_ref[...]
    t = _nt(wt_ref[...], h) + bt_ref[...]
    kt_ref[...] = t[0:ML_PW].astype(bf16)
    gt_ref[...] = t[ML_PW:N_TCOLS]


def _in_proj_specs(layer):
    tm = TOK_TILE
    lyr = lambda shape: pl.BlockSpec((None,) + shape, lambda i, *_: (layer, 0, 0))
    rows = lambda width: pl.BlockSpec((tm, width), lambda i, *_: (i, 0))
    cols = lambda height: pl.BlockSpec((height, tm), lambda i, *_: (0, i))
    param_specs = [pl.BlockSpec((1, 6, D_MODEL), lambda i, *_: (_mod_row(i, tm), 0, 0)),
                   lyr((1, D_MODEL)), lyr((D_MODEL, W_MAIN)), lyr((1, W_MAIN)), lyr((D_MODEL, LANE)), lyr((1, LANE)),
                   lyr((N_TCOLS, D_MODEL)), lyr((N_TCOLS, 1)), pl.BlockSpec(memory_space=pl.ANY)]
    kv_spec = pl.BlockSpec((None, tm, 2 * NA_WIDTH), lambda i, *_: (layer, jnp.minimum(i, N_CTX // tm), 0))
    out_specs = [rows(W_A), kv_spec, rows(W_B), cols(ML_PW), rows(LANE), cols(N_GATE_COLS), rows(POOL_WIDTH)]
    out_shape = [jax.ShapeDtypeStruct((N_TOK, W_A), bf16),
                 jax.ShapeDtypeStruct((DEPTH, N_CTX + tm, 2 * NA_WIDTH), f32),
                 jax.ShapeDtypeStruct((N_TOK, W_B), bf16),
                 jax.ShapeDtypeStruct((ML_PW, N_TOK), bf16),
                 jax.ShapeDtypeStruct((N_TOK, LANE), f32),
                 jax.ShapeDtypeStruct((N_GATE_COLS, N_TOK), f32),
                 jax.ShapeDtypeStruct((N_TOK, POOL_WIDTH), f32)]
    return rows, param_specs, out_specs, out_shape


def _in_proj(x_ctx, x_lat, mods, layer, *params):
    tm = TOK_TILE
    n_ctx_tiles = N_CTX // tm
    rows, param_specs, out_specs, out_shape = _in_proj_specs(layer)
    return pl.pallas_call(
        _in_kernel,
        grid=(N_TOK // tm,),
        in_specs=[pl.BlockSpec((tm, D_MODEL), lambda i: (jnp.minimum(i, n_ctx_tiles - 1), 0)),
                  pl.BlockSpec((tm, D_MODEL), lambda i: (jnp.maximum(i - n_ctx_tiles, 0), 0))] + param_specs,
        out_specs=out_specs,
        out_shape=out_shape,
        input_output_aliases={2 + len(param_specs) - 1: 1},
        compiler_params=pltpu.CompilerParams(dimension_semantics=("arbitrary",), vmem_limit_bytes=VMEM_LIMIT),
        name="in_proj",
    )(x_ctx, x_lat, mods, *params)


def _moe_in_proj(pos0, pos1, ys, x1, rc, mods_prev, mods, layer, *params):
    tm = TOK_TILE
    rows, param_specs, out_specs, out_shape = _in_proj_specs(layer)
    return pl.pallas_call(
        _moe_in_kernel,
        grid_spec=pltpu.PrefetchScalarGridSpec(
            num_scalar_prefetch=2,
            grid=(N_TOK // tm,),
            in_specs=[pl.BlockSpec(memory_space=pl.ANY), rows(D_MODEL), rows(LANE),
                      pl.BlockSpec((1, 6, D_MODEL), lambda i, *_: (_mod_row(i, tm), 0, 0))] + param_specs,
            out_specs=[rows(D_MODEL)] + out_specs,
            scratch_shapes=[pltpu.VMEM((2, 2, tm, D_MODEL), f32), pltpu.SemaphoreType.DMA((2,))]),
        out_shape=[jax.ShapeDtypeStruct((N_TOK, D_MODEL), f32)] + out_shape,
        input_output_aliases={6 + len(param_specs) - 1: 2},
        compiler_params=pltpu.CompilerParams(dimension_semantics=("arbitrary",), vmem_limit_bytes=VMEM_LIMIT),
        name="moe_combine_in_proj",
    )(pos0, pos1, ys, x1, rc, mods_prev, mods, *params)


def _pair_attention(qp, parts):
    lane = lax.broadcasted_iota(jnp.int32, (1, LANE), 1)
    outs = []
    for j in range(2):
        in_half = (lane >= j * NA_DIM) & (lane < (j + 1) * NA_DIM)
        qm = jnp.where(in_half, qp, jnp.zeros_like(qp))
        scores = []
        for k, _, bias in parts:
            s = _nt(qm, k)
            if bias is not None:
                s = s + bias[j]
            scores.append(s)
        m = scores[0].max(axis=-1, keepdims=True)
        for s in scores[1:]:
            m = jnp.maximum(m, s.max(axis=-1, keepdims=True))
        den = None
        acc = None
        for s, (_, v, _) in zip(scores, parts):
            p = jnp.exp(s - m)
            ps = p.sum(axis=-1, keepdims=True)
            den = ps if den is None else den + ps
            o = jnp.dot(p.astype(bf16), v, preferred_element_type=f32)
            acc = o if acc is None else acc + o
        outs.append(acc / den)
    return jnp.where(lane < NA_DIM, outs[0], outs[1])


def _ctx_attn_kernel(q_ref, k_ref, v_ref, o_ref):
    for p in range(NA_PAIRS):
        sl = slice(p * LANE, (p + 1) * LANE)
        o = _pair_attention(q_ref[0, :, sl], [(k_ref[0, :, sl], v_ref[0, :, sl], None)])
        o_ref[0, :, sl] = o.astype(bf16)


def _ctx_attention(qkv):
    blk = lambda c: pl.BlockSpec((1, SEQ, NA_WIDTH), lambda b, c=c: (b, 0, c))
    return pl.pallas_call(
        _ctx_attn_kernel,
        grid=(BATCH,),
        in_specs=[blk(0), blk(1), blk(2)],
        out_specs=pl.BlockSpec((1, SEQ, NA_WIDTH), lambda b: (b, 0, 0)),
        out_shape=jax.ShapeDtypeStruct((BATCH, SEQ, NA_WIDTH), bf16),
        name="ctx_attention",
    )(qkv, qkv, qkv)


def _na_window_start(rb):
    return jnp.clip(rb * NA_QROWS - NA_ROWS // 2, 0, DEC_SEQ // GRID_W - NA_KROWS)


def _na_bias(tab_ref, head, rb):
    rows = DEC_SEQ // GRID_W
    ws = _na_window_start(rb)
    lane = lax.broadcasted_iota(jnp.int32, (1, NA_KROWS * GRID_W), 1)
    per_qrow = []
    for dq in range(NA_QROWS):
        qr = rb * NA_QROWS + dq
        a0 = ws - qr + (NA_ROWS - 1) + NA_KROWS
        tiles = [tab_ref[head, a0 + 2 * j] for j in range((NA_KROWS + 1) // 2)]
        t = jnp.concatenate(tiles, axis=1)[:, :NA_KROWS * GRID_W]
        lo = (jnp.clip(qr - NA_ROWS // 2, 0, rows - NA_ROWS) - ws) * GRID_W
        ok = (lane >= lo) & (lane < lo + NA_ROWS * GRID_W)
        per_qrow.append(jnp.where(ok, t, NEG_INF))
    return jnp.concatenate(per_qrow, axis=0)


def _na_kernel(q_ref, k_ref, v_ref, ck_ref, cv_ref, tab_ref, o_ref):
    rb = pl.program_id(1)
    start = pl.multiple_of(_na_window_start(rb) * GRID_W, GRID_W)
    nk = NA_KROWS * GRID_W
    for p in range(NA_PAIRS):
        sl = slice(p * LANE, (p + 1) * LANE)
        bias = [_na_bias(tab_ref.at[0], 2 * p + j, rb) for j in range(2)]
        parts = [(k_ref[0, pl.ds(start, nk), sl], v_ref[0, pl.ds(start, nk), sl], bias),
                 (ck_ref[0, :, sl], cv_ref[0, :, sl], None)]
        o = _pair_attention(q_ref[0, :, sl], parts)
        o_ref[0, :, sl] = o.astype(bf16)


def _neighborhood_attention(qkv, ck, cv, tables, layer):
    nq = NA_QROWS * GRID_W
    n_rb = DEC_SEQ // nq
    return pl.pallas_call(
        _na_kernel,
        grid=(DEC_BATCH, n_rb),
        in_specs=[pl.BlockSpec((1, nq, NA_WIDTH), lambda b, r: (1 + b, r, 0)),
                  pl.BlockSpec((1, DEC_SEQ, NA_WIDTH), lambda b, r: (1 + b, 0, 1)),
                  pl.BlockSpec((1, DEC_SEQ, NA_WIDTH), lambda b, r: (1 + b, 0, 2)),
                  pl.BlockSpec((1, PAST_LEN, NA_WIDTH), lambda b, r: (b, 0, 0)),
                  pl.BlockSpec((1, PAST_LEN, NA_WIDTH), lambda b, r: (b, 0, 0)),
                  pl.BlockSpec((1,) + tables.shape[1:], lambda b, r: (layer, 0, 0, 0, 0))],
        out_specs=pl.BlockSpec((1, nq, NA_WIDTH), lambda b, r: (b, r, 0)),
        out_shape=jax.ShapeDtypeStruct((DEC_BATCH, DEC_SEQ, NA_WIDTH), bf16),
        compiler_params=pltpu.CompilerParams(dimension_semantics=("arbitrary", "arbitrary"),
                                             vmem_limit_bytes=VMEM_LIMIT),
        name="neighborhood_attention",
    )(qkv, qkv, qkv, ck, cv, tables)


def _na_bias_tables(rpb):
    qc = np.arange(GRID_W)[:, None]
    kc = np.arange(GRID_W)[None, :]
    dc = np.clip(kc - qc + NA_COLS - 1, 0, RPB_COLS - 1)
    col_start = np.clip(qc - NA_COLS // 2, 0, GRID_W - NA_COLS)
    col_ok = (kc >= col_start) & (kc < col_start + NA_COLS)
    pick_col = (dc[None] == np.arange(RPB_COLS)[:, None, None]).astype(np.float32)
    rpb_pad = jnp.pad(rpb.astype(f32), ((0, 0), (0, 0), (NA_KROWS, NA_KROWS + 1), (0, 0)))
    n_a = rpb_pad.shape[2] - 1
    rows2 = jnp.stack([rpb_pad[:, :, :-1], rpb_pad[:, :, 1:]], axis=3)
    tiles = jnp.einsum('lhajb,bqk->lhaqjk', rows2, pick_col, precision=HI)
    a_pad = np.arange(n_a)[:, None] + np.arange(2)[None, :]
    row_ok = (a_pad >= NA_KROWS) & (a_pad < NA_KROWS + RPB_ROWS)
    ok = row_ok[:, None, :, None] & col_ok[None, :, None, :]
    return jnp.where(ok[None, None], tiles, NEG_INF).reshape(DEPTH, NA_HEADS, n_a, GRID_W, 2 * GRID_W)


def _log_sigmoid(x):
    return -(jnp.maximum(-x, 0.0) + jnp.log(1.0 + jnp.exp(-jnp.abs(x))))


def _split3(x):
    hi = x.astype(bf16)
    r1 = x - hi.astype(f32)
    mid = r1.astype(bf16)
    lo = (r1 - mid.astype(f32)).astype(bf16)
    return hi, mid, lo


def _mlstm_kernel(qf_ref, vf_ref, ktf_ref, gf_ref, gtf_ref, qb_ref, vb_ref, ktb_ref, gb_ref, gtb_ref,
                  c0_ref, m0_ref, hf_ref, hb_ref, c_out_ref, m_out_ref, c_scr, m_scr):
    L = ML_CHUNK
    seq, c, n_chunks, _ = _ml_schedule(pl.program_id(0))

    @pl.when(c == 0)
    def _():
        is_ctx = seq < BATCH
        c_scr[...] = jnp.where(is_ctx, 0.0, c0_ref[0])
        m_scr[...] = jnp.where(is_ctx, 0.0, m0_ref[0])

    ri = lax.broadcasted_iota(jnp.int32, (L, L), 0)
    ci = lax.broadcasted_iota(jnp.int32, (L, L), 1)
    lane = lax.broadcasted_iota(jnp.int32, (L, ML_PAD), 1)
    is_ncol = lane == ML_DIM
    lower = ri >= ci
    upper = ri <= ci
    lower_b = jnp.where(lower, 1.0, 0.0).astype(bf16)
    upper_b = jnp.where(upper, 1.0, 0.0).astype(bf16)
    dirs = ((qf_ref, ktf_ref, vf_ref, gf_ref, gtf_ref, hf_ref), (qb_ref, ktb_ref, vb_ref, gb_ref, gtb_ref, hb_ref))
    for d, (q_ref, kt_ref, v_ref, g_ref, gt_ref, h_ref) in enumerate(dirs):
        g = g_ref[...][:, 0:N_GATE_COLS]
        gt = gt_ref[...]
        lf_c = _log_sigmoid(g)
        lf_r = _log_sigmoid(gt)
        b_cols = sum(jnp.dot(lower_b, part, preferred_element_type=f32) for part in _split3(lf_c))
        b_rows = sum(jnp.dot(part, upper_b, preferred_element_type=f32) for part in _split3(lf_r))
        tot_c = jnp.sum(lf_c, axis=0, keepdims=True)
        tot_r = jnp.sum(lf_r, axis=1, keepdims=True)
        visible = lower
        if d == 1:
            b_cols = tot_c - b_cols + lf_c
            b_rows = tot_r - b_rows + lf_r
            visible = upper
        for hd in range(ML_HEADS):
            st = d * ML_HEADS + hd
            ci_ = 2 * ML_HEADS * d + hd
            cf_ = ci_ + ML_HEADS
            sl = slice(hd * ML_PAD, (hd + 1) * ML_PAD)
            bc = b_cols[:, cf_:cf_ + 1]
            br = b_rows[cf_:cf_ + 1, :]
            li_r = gt[ci_:ci_ + 1, :]
            m_prev = m_scr[st:st + 1, 0:1]
            dmat = jnp.where(visible, bc - br + li_r, NEG_INF)
            inter = bc + m_prev
            m_t = jnp.maximum(inter, dmat.max(axis=-1, keepdims=True))
            w_intra = jnp.exp(dmat - m_t)
            w_inter = jnp.exp(inter - m_t)
            qh = q_ref[0, :, sl]
            kht = kt_ref[sl, :]
            v_aug = jnp.where(is_ncol, jnp.ones((), bf16), v_ref[0, :, sl])
            s = (jnp.dot(qh, kht, preferred_element_type=f32) * w_intra).astype(bf16)
            c_aug = c_scr[st]
            na = (w_inter * jnp.dot(qh, c_aug.astype(bf16), preferred_element_type=f32)
                  + jnp.dot(s, v_aug, preferred_element_type=f32))
            den = na[:, ML_DIM:ML_DIM + 1]
            h_ref[0, :, sl] = jnp.where(lane < ML_DIM, na / jnp.maximum(jnp.abs(den), jnp.exp(-m_t)), 0.0)
            b_end = tot_r[cf_:cf_ + 1, :]
            g_row = b_end - br + li_r
            m_new = jnp.maximum(b_end + m_prev, g_row.max(axis=1, keepdims=True))
            decay = jnp.exp(b_end + m_prev - m_new)
            kwt = (kht.astype(f32) * jnp.exp(g_row - m_new)).astype(bf16)
            c_scr[st] = decay * c_aug + jnp.dot(kwt, v_aug, preferred_element_type=f32)
            m_scr[st:st + 1, :] = jnp.broadcast_to(m_new, (1, LANE))

    @pl.when(c == n_chunks - 1)
    def _():
        c_out_ref[0] = c_scr[...]
        m_out_ref[0] = m_scr[...]


def _ml_schedule(s):
    nc_ctx, nc_lat = SEQ // ML_CHUNK, DEC_SEQ // ML_CHUNK
    n_ctx_steps = BATCH * nc_ctx
    is_ctx = s < n_ctx_steps
    t = s - n_ctx_steps
    seq = jnp.where(is_ctx, s // nc_ctx, BATCH + t // nc_lat)
    c = jnp.where(is_ctx, s % nc_ctx, t % nc_lat)
    nc = jnp.where(is_ctx, nc_ctx, nc_lat)
    base = jnp.where(is_ctx, (s // nc_ctx) * nc_ctx, n_ctx_steps + (t // nc_lat) * nc_lat)
    return seq, c, nc, base


def _mlstm(qvo, kt, gates, gates_t, c0, m0):
    L = ML_CHUNK
    n_seq = BATCH + DEC_BATCH

    def fwd(s):
        _, c, _, base = _ml_schedule(s)
        return base + c

    def bwd(s):
        _, c, nc, base = _ml_schedule(s)
        return base + nc - 1 - c

    seq_of = lambda s: _ml_schedule(s)[0]
    lat_of = lambda s: jnp.maximum(seq_of(s) - BATCH, 0)

    def specs(pos):
        return [pl.BlockSpec((1, L, ML_PW), lambda s, j=j: (pos(s), 0, j)) for j in range(2)] + [
            pl.BlockSpec((ML_PW, L), lambda s: (0, pos(s))),
            pl.BlockSpec((L, LANE), lambda s: (pos(s), 0)),
            pl.BlockSpec((N_GATE_COLS, L), lambda s: (0, pos(s)))]

    q3 = qvo.reshape(N_TOK // L, L, W_B)
    n_str = 2 * ML_HEADS
    return pl.pallas_call(
        _mlstm_kernel,
        grid=(N_TOK // L,),
        in_specs=specs(fwd) + specs(bwd) + [
            pl.BlockSpec((1, n_str, ML_PAD, CAUG), lambda s: (lat_of(s), 0, 0, 0)),
            pl.BlockSpec((1, n_str, LANE), lambda s: (lat_of(s), 0, 0))],
        out_specs=[pl.BlockSpec((1, L, ML_PW), lambda s: (fwd(s), 0, 0)),
                   pl.BlockSpec((1, L, ML_PW), lambda s: (bwd(s), 0, 0)),
                   pl.BlockSpec((1, n_str, ML_PAD, CAUG), lambda s: (seq_of(s), 0, 0, 0)),
                   pl.BlockSpec((1, n_str, LANE), lambda s: (seq_of(s), 0, 0))],
        out_shape=[jax.ShapeDtypeStruct((N_TOK // L, L, ML_PW), f32),
                   jax.ShapeDtypeStruct((N_TOK // L, L, ML_PW), f32),
                   jax.ShapeDtypeStruct((n_seq, n_str, ML_PAD, CAUG), f32),
                   jax.ShapeDtypeStruct((n_seq, n_str, LANE), f32)],
        scratch_shapes=[pltpu.VMEM((n_str, ML_PAD, CAUG), f32), pltpu.VMEM((n_str, LANE), f32)],
        compiler_params=pltpu.CompilerParams(dimension_semantics=("arbitrary",), vmem_limit_bytes=VMEM_LIMIT),
        name="mlstm",
    )(q3, q3, kt, gates, gates_t, q3, q3, kt, gates, gates_t, c0, m0)


def _pack_ml_state(C, n, m):
    B = C.shape[0]
    c_aug = jnp.zeros((B, 2, ML_HEADS, ML_PAD, CAUG), f32)
    c_aug = c_aug.at[:, :, :, :ML_DIM, :ML_DIM].set(C.astype(f32))
    c_aug = c_aug.at[:, :, :, :ML_DIM, ML_DIM].set(n.astype(f32))
    m_b = jnp.broadcast_to(m.astype(f32)[..., None], (B, 2, ML_HEADS, LANE))
    return c_aug.reshape(B, 2 * ML_HEADS, ML_PAD, CAUG), m_b.reshape(B, 2 * ML_HEADS, LANE)


def _unpack_ml_state(c_aug, m_b):
    B = c_aug.shape[0]
    c_aug = c_aug.reshape(B, 2, ML_HEADS, ML_PAD, CAUG)
    return (c_aug[:, :, :, :ML_DIM, :ML_DIM], c_aug[:, :, :, :ML_DIM, ML_DIM],
            m_b.reshape(B, 2, ML_HEADS, LANE)[..., 0])


def _pool_rows(u_prev, u_cur, u_next, w_bd, scale, t0, seq_len):
    tm = u_cur.shape[0]
    u_win = jnp.concatenate([u_prev, u_cur, u_next], axis=0)
    u_hi = u_win.astype(bf16)
    u_lo = (u_win - u_hi.astype(f32)).astype(bf16)
    lane = lax.broadcasted_iota(jnp.int32, (1, LANE), 1)
    blocks = []
    for r0 in range(0, tm, POOL_BLOCK):
        win = slice(r0, r0 + POOL_BLOCK + 2 * POOL_HALO)
        t_abs = t0 + r0 + lax.broadcasted_iota(jnp.int32, (POOL_BLOCK, 1), 0)
        s_abs = t0 + r0 - POOL_HALO + lax.broadcasted_iota(jnp.int32, (1, POOL_BLOCK + 2 * POOL_HALO), 1)
        t_loc = t_abs & (seq_len - 1)
        seq_start = t_abs - t_loc
        means = []
        for w in POOL_WINDOWS:
            lo = jnp.maximum(t_loc - w // 2, 0)
            hi = jnp.minimum(t_loc - w // 2 + w, seq_len)
            in_win = (s_abs >= seq_start + lo) & (s_abs < seq_start + hi)
            means.append((jnp.where(in_win, 1.0, 0.0).astype(bf16), 1.0 / (hi - lo).astype(f32)))
        pooled = []
        for p in range(POOL_GROUPS // 2):
            sl = slice(p * LANE, (p + 1) * LANE)
            halves = []
            for a, inv_cnt in means[2 * p:2 * p + 2]:
                tot = (jnp.dot(a, u_hi[win, sl], preferred_element_type=f32)
                       + jnp.dot(a, u_lo[win, sl], preferred_element_type=f32))
                halves.append(tot * inv_cnt)
            pooled.append(jnp.where(lane < POOL_DIM, halves[0], halves[1]) - u_cur[r0:r0 + POOL_BLOCK, sl])
        blocks.append(jnp.concatenate(pooled, axis=1))
    pooled = jnp.concatenate(blocks, axis=0).astype(bf16)
    return jnp.dot(pooled, w_bd, preferred_element_type=f32) * scale


def _top2_sum(a, b, c, d):
    hi1, lo1 = jnp.maximum(a, b), jnp.minimum(a, b)
    hi2, lo2 = jnp.maximum(c, d), jnp.minimum(c, d)
    return jnp.maximum(hi1, hi2) + jnp.maximum(jnp.minimum(hi1, hi2), jnp.maximum(lo1, lo2))


def _first_match(vals, target):
    idx = jnp.full_like(target, float(len(vals) - 1))
    for i in range(len(vals) - 2, -1, -1):
        idx = jnp.where(vals[i] == target, float(i), idx)
    return idx


def _pick(vals, idx):
    out = vals[-1]
    for i in range(len(vals) - 2, -1, -1):
        out = jnp.where(idx == float(i), vals[i], out)
    return out


def _route(logits_t, bias_t):
    scores = jax.nn.sigmoid(logits_t)
    sel = scores + bias_t
    row = lambda a, i: a[i:i + 1, :]
    grp = [_top2_sum(*[row(sel, EXPERTS_PER_GROUP * g + i) for i in range(EXPERTS_PER_GROUP)])
           for g in range(N_EXPERT_GROUPS)]
    best = functools.reduce(jnp.maximum, grp)
    gidx = _first_match(grp, best)
    sel_g = [_pick([row(sel, EXPERTS_PER_GROUP * g + i) for g in range(N_EXPERT_GROUPS)], gidx)
             for i in range(EXPERTS_PER_GROUP)]
    sco_g = [_pick([row(scores, EXPERTS_PER_GROUP * g + i) for g in range(N_EXPERT_GROUPS)], gidx)
             for i in range(EXPERTS_PER_GROUP)]
    i0 = _first_match(sel_g, functools.reduce(jnp.maximum, sel_g))
    rest = [jnp.where(i0 == float(i), -jnp.inf, sel_g[i]) for i in range(EXPERTS_PER_GROUP)]
    i1 = _first_match(rest, functools.reduce(jnp.maximum, rest))
    s0, s1 = _pick(sco_g, i0), _pick(sco_g, i1)
    tot = s0 + s1
    rid = lax.broadcasted_iota(jnp.int32, (LANE, logits_t.shape[1]), 0)
    rows = (EXPERTS_PER_GROUP * gidx + i0, EXPERTS_PER_GROUP * gidx + i1, s0 / tot, s1 / tot)
    out = jnp.zeros(rid.shape, f32)
    for i, r in enumerate(rows):
        out = jnp.where(rid == i, r, out)
    return out


def _out_kernel(xc_ref, xl_ref, mod_ref, oac_ref, oal_ref, hf_ref, hb_ref, ob_ref, up_ref, uc_ref, un_ref, wp_ref, psc_ref,
                mln_ref, wo_ref, n2_ref, wr_ref, br_ref, x1_ref, h2_ref, rt_ref, rc_ref):
    tm = xc_ref.shape[0]
    i = pl.program_id(0)
    is_ctx = i < N_CTX // tm
    mod = mod_ref[0]
    out_a = jnp.where(is_ctx, oac_ref[...], oal_ref[...])
    out_c = _pool_rows(up_ref[...], uc_ref[...], un_ref[...], wp_ref[...], psc_ref[...], i * tm,
                       jnp.where(is_ctx, SEQ, DEC_SEQ)).astype(bf16)
    hsum = hf_ref[...] + hb_ref[...]
    outs_b = []
    for hd in range(ML_HEADS):
        sl = slice(hd * ML_PAD, (hd + 1) * ML_PAD)
        hh = hsum[:, sl]
        ms = jnp.sum(hh * hh, axis=-1, keepdims=True) * (1.0 / ML_DIM)
        hn = hh * lax.rsqrt(ms + EPS) * mln_ref[:, sl]
        outs_b.append((jax.nn.sigmoid(ob_ref[:, sl].astype(f32)) * hn).astype(bf16))
    out_b = jnp.concatenate(outs_b, axis=1)
    mixed = (jnp.dot(out_a, wo_ref[0:NA_WIDTH, :], preferred_element_type=f32)
             + jnp.dot(out_b, wo_ref[NA_WIDTH:NA_WIDTH + ML_PW, :], preferred_element_type=f32)
             + jnp.dot(out_c, wo_ref[NA_WIDTH + ML_PW:, :], preferred_element_type=f32))
    x1 = jnp.where(is_ctx, xc_ref[...], xl_ref[...]) + mod[2:3] * mixed
    x1_ref[...] = x1
    h2 = x1 * lax.rsqrt(jnp.mean(x1 * x1, axis=-1, keepdims=True) + EPS) * n2_ref[...]
    h2 = h2 * (1.0 + mod[4:5]) + mod[3:4]
    h2_ref[...] = h2.astype(bf16)
    h_hi = h2.astype(bf16)
    h_lo = (h2 - h_hi.astype(f32)).astype(bf16)
    w_hi = wr_ref[...].astype(bf16)
    w_lo = (wr_ref[...] - w_hi.astype(f32)).astype(bf16)
    route_t = _route(_nt(w_hi, h_hi) + (_nt(w_hi, h_lo) + _nt(w_lo, h_hi)), br_ref[...])
    rt_ref[...] = route_t[0:8]
    rc_ref[...] = route_t.T


def _out_proj(x_ctx, x_lat, x_lat_block0, mods, layer, oa_ctx, oa_lat, hf, hb, qvo, pin, w_bd, psc, mln, wo, n2,
              wr_t, br_t):
    tm = TOK_TILE
    const = lambda i: (0, 0)
    lyr = lambda shape: pl.BlockSpec((None,) + shape, lambda i: (layer, 0, 0))
    row = lambda i: (i, 0)
    n_ctx_tiles = N_CTX // tm
    halo_blocks = tm // POOL_HALO
    return pl.pallas_call(
        _out_kernel,
        grid=(N_TOK // tm,),
        in_specs=[pl.BlockSpec((tm, D_MODEL), lambda i: (jnp.minimum(i, n_ctx_tiles - 1), 0)),
                  pl.BlockSpec((tm, D_MODEL), lambda i: (jnp.maximum(i - n_ctx_tiles, 0) + x_lat_block0, 0)),
                  pl.BlockSpec((1, 6, D_MODEL), lambda i: (_mod_row(i, tm), 0, 0)),
                  pl.BlockSpec((tm, NA_WIDTH), lambda i: (jnp.minimum(i, n_ctx_tiles - 1), 0)),
                  pl.BlockSpec((tm, NA_WIDTH), lambda i: (jnp.maximum(i - n_ctx_tiles, 0), 0)),
                  pl.BlockSpec((tm, ML_PW), row),
                  pl.BlockSpec((tm, ML_PW), row),
                  pl.BlockSpec((tm, ML_PW), lambda i: (i, 2)),
                  pl.BlockSpec((POOL_HALO, POOL_WIDTH), lambda i: (jnp.maximum(i * halo_blocks - 1, 0), 0)),
                  pl.BlockSpec((tm, POOL_WIDTH), row),
                  pl.BlockSpec((POOL_HALO, POOL_WIDTH),
                               lambda i: (jnp.minimum((i + 1) * halo_blocks, N_TOK // POOL_HALO - 1), 0)),
                  lyr((POOL_WIDTH, POOL_WIDTH)), lyr((1, POOL_WIDTH)), lyr((1, ML_PW)),
                  lyr((NA_WIDTH + ML_PW + POOL_WIDTH, D_MODEL)), lyr((1, D_MODEL)),
                  pl.BlockSpec((N_EXPERTS, D_MODEL), const),
                  pl.BlockSpec((N_EXPERTS, 1), const)],
        out_specs=[pl.BlockSpec((tm, D_MODEL), row),
                   pl.BlockSpec((tm, D_MODEL), row),
                   pl.BlockSpec((8, tm), lambda i: (0, i)),
                   pl.BlockSpec((tm, LANE), row)],
        out_shape=[jax.ShapeDtypeStruct((N_TOK, D_MODEL), f32),
                   jax.ShapeDtypeStruct((N_TOK, D_MODEL), bf16),
                   jax.ShapeDtypeStruct((8, N_TOK), f32),
                   jax.ShapeDtypeStruct((N_TOK, LANE), f32)],
        compiler_params=pltpu.CompilerParams(dimension_semantics=("arbitrary",), vmem_limit_bytes=VMEM_LIMIT),
        name="out_proj_router",
    )(x_ctx, x_lat, mods, oa_ctx, oa_lat, hf, hb, qvo, pin, pin, pin, w_bd, psc, mln, wo, n2, wr_t, br_t)


def _ceil_to(x, m):
    return jnp.floor((x + (m - 1)) * (1.0 / m)) * m


def _prefix_over_experts(v):
    er = lax.broadcasted_iota(jnp.int32, (N_EXPERTS, N_EXPERTS), 0)
    ec = lax.broadcasted_iota(jnp.int32, (N_EXPERTS, N_EXPERTS), 1)
    return jnp.dot(jnp.where(ec < er, 1.0, 0.0), v, preferred_element_type=f32, precision=HI)


def _experts_to_lanes(v):
    sub = lax.broadcasted_iota(jnp.int32, (N_EXPERTS, LANE), 0)
    lane = lax.broadcasted_iota(jnp.int32, (N_EXPERTS, LANE), 1)
    return jnp.sum(jnp.where(sub == lane, v, 0.0), axis=0, keepdims=True)


def _expert_hits(rt):
    rid = lax.broadcasted_iota(jnp.int32, (N_EXPERTS, rt.shape[1]), 0).astype(f32)
    oh0 = rid == rt[0:1, :]
    oh1 = rid == rt[1:2, :]
    both = jnp.where(oh0 | oh1, 1.0, 0.0)
    runs = jnp.broadcast_to(_ceil_to(jnp.sum(both, axis=1, keepdims=True), MOE_CHUNK), (N_EXPERTS, LANE))
    return oh0, oh1, both, runs


def _rank_kernel(rt_all_ref, rt_ref, pos_ref, te_ref, tab_ref, carry_ref):
    tm = rt_ref.shape[1]
    step = pl.program_id(0)

    @pl.when(step == 0)
    def _():
        totals = jnp.zeros((N_EXPERTS, LANE), f32)
        for i in range(rt_all_ref.shape[1] // tm):
            totals = totals + _expert_hits(rt_all_ref[:, i * tm:(i + 1) * tm])[3]
        padded = _ceil_to(totals, MOE_TILE)
        off = _prefix_over_experts(padded)
        carry_ref[...] = off
        total = jnp.sum(padded, axis=0, keepdims=True)
        n_used = total * (1.0 / MOE_TILE)
        tile = lax.broadcasted_iota(jnp.int32, (1, LANE), 1).astype(f32)
        row0 = jnp.minimum(tile, n_used - 1.0) * MOE_TILE
        expert = jnp.sum(jnp.where(off <= row0, 1.0, 0.0), axis=0, keepdims=True) - 1.0
        sub = lax.broadcasted_iota(jnp.int32, (8, LANE), 0)
        te_ref[...] = jnp.where(sub == 0, expert, jnp.where(sub == 1, n_used, 0.0)).astype(jnp.int32)

    @pl.when(step > 0)
    def _():
        oh0, oh1, both, runs = _expert_hits(rt_ref[...])
        sr = lax.broadcasted_iota(jnp.int32, (tm, tm), 0)
        sc = lax.broadcasted_iota(jnp.int32, (tm, tm), 1)
        earlier = jnp.dot(both.astype(bf16), jnp.where(sr < sc, 1.0, 0.0).astype(bf16),
                          preferred_element_type=f32)
        g_off = carry_ref[...]
        l_off = _prefix_over_experts(runs)
        g_row = g_off[:, 0:1] + earlier
        l_row = l_off[:, 0:1] + earlier
        pick = lambda oh, v: jnp.sum(jnp.where(oh, v, 0.0), axis=0, keepdims=True)
        rows = (pick(oh0, g_row), pick(oh1, g_row), pick(oh0, l_row), pick(oh1, l_row))
        sub = lax.broadcasted_iota(jnp.int32, (8, tm), 0)
        out = jnp.zeros((8, tm), f32)
        for k, r in enumerate(rows):
            out = jnp.where(sub == k, r, out)
        pos_ref[...] = out.astype(jnp.int32)
        sub = lax.broadcasted_iota(jnp.int32, (8, LANE), 0)
        tab = jnp.zeros((8, LANE), f32)
        for k, v in enumerate((runs * (1.0 / MOE_CHUNK), l_off, g_off)):
            tab = jnp.where(sub == k, _experts_to_lanes(v), tab)
        tab_ref[0] = tab.astype(jnp.int32)
        carry_ref[...] = g_off + runs


def _rank(route_t):
    tm = TOK_TILE
    n_tiles = N_TOK // tm
    tile_of = lambda s: jnp.maximum(s - 1, 0)
    return pl.pallas_call(
        _rank_kernel,
        grid=(1 + n_tiles,),
        in_specs=[pl.BlockSpec((8, N_TOK), lambda s: (0, 0)),
                  pl.BlockSpec((8, tm), lambda s: (0, tile_of(s)))],
        out_specs=[pl.BlockSpec((8, tm), lambda s: (0, tile_of(s))),
                   pl.BlockSpec((8, LANE), lambda s: (0, 0)),
                   pl.BlockSpec((1, 8, LANE), lambda s: (tile_of(s), 0, 0))],
        out_shape=[jax.ShapeDtypeStruct((8, N_TOK), jnp.int32),
                   jax.ShapeDtypeStruct((8, LANE), jnp.int32),
                   jax.ShapeDtypeStruct((n_tiles, 8, LANE), jnp.int32)],
        scratch_shapes=[pltpu.VMEM((N_EXPERTS, LANE), f32)],
        compiler_params=pltpu.CompilerParams(dimension_semantics=("arbitrary",)),
        name="moe_rank",
    )(route_t, route_t)


def _dispatch_kernel(tab_ref, h_ref, rows_ref, xs_in_ref, xs_ref, loc, sem):
    del xs_in_ref
    tm = h_ref.shape[0]
    i = pl.program_id(0)
    rid = lax.broadcasted_iota(jnp.int32, (MOE_LOCAL_ROWS, tm), 0)
    sel = (rid == rows_ref[2:3, :]) | (rid == rows_ref[3:4, :])
    loc[...] = jnp.dot(jnp.where(sel, 1.0, 0.0).astype(bf16), h_ref[...], preferred_element_type=f32).astype(bf16)

    def chunk_copy(src_row, dst_row):
        return pltpu.make_async_copy(loc.at[pl.ds(pl.multiple_of(src_row, MOE_CHUNK), MOE_CHUNK)],
                                     xs_ref.at[pl.ds(pl.multiple_of(dst_row, MOE_CHUNK), MOE_CHUNK)], sem)

    n_total = 0
    for e in range(N_EXPERTS):
        n_chunks, l_off, g_off = (tab_ref[(3 * i + k) * N_EXPERTS + e] for k in range(3))

        def issue(c, carry, l_off=l_off, g_off=g_off):
            chunk_copy(l_off + c * MOE_CHUNK, g_off + c * MOE_CHUNK).start()
            return carry

        lax.fori_loop(0, n_chunks, issue, 0)
        n_total = n_total + n_chunks

    def wait_one(c, carry):
        chunk_copy(0, 0).wait()
        return carry

    lax.fori_loop(0, n_total, wait_one, 0)


def _dispatch(run_table, h2, rows, xs_init):
    tm = TOK_TILE
    return pl.pallas_call(
        _dispatch_kernel,
        grid_spec=pltpu.PrefetchScalarGridSpec(
            num_scalar_prefetch=1,
            grid=(N_TOK // tm,),
            in_specs=[pl.BlockSpec((tm, D_MODEL), lambda i, tab: (i, 0)),
                      pl.BlockSpec((8, tm), lambda i, tab: (0, i)),
                      pl.BlockSpec(memory_space=pl.ANY)],
            out_specs=pl.BlockSpec(memory_space=pl.ANY),
            scratch_shapes=[pltpu.VMEM((MOE_LOCAL_ROWS, D_MODEL), bf16), pltpu.SemaphoreType.DMA(())]),
        out_shape=jax.ShapeDtypeStruct(xs_init.shape, xs_init.dtype),
        input_output_aliases={3: 0},
        compiler_params=pltpu.CompilerParams(dimension_semantics=("arbitrary",), vmem_limit_bytes=VMEM_LIMIT),
        name="moe_dispatch",
    )(run_table, h2, rows, xs_init)


def _expert_kernel(te_ref, xs_ref, wg_hbm, wu_hbm, wd_hbm, ys_ref, wg_f32, wu_f32, wd_f32, wg_bf, wu_bf, wd_bf,
                   slot_ref, sem, *, layer):
    j = pl.program_id(0)
    n_used = te_ref[1, 0]
    used = j < n_used
    expert = te_ref[0, j]
    new_expert = jnp.logical_or(j == 0, expert != te_ref[0, jnp.maximum(j - 1, 0)])

    def weight_copies(e, slot):
        return [pltpu.make_async_copy(hbm.at[layer, e], buf.at[slot], sem.at[slot])
                for hbm, buf in ((wg_hbm, wg_f32), (wu_hbm, wu_f32), (wd_hbm, wd_f32))]

    @pl.when(j == 0)
    def _():
        slot_ref[0] = 1
        for cp in weight_copies(expert, 0):
            cp.start()

    @pl.when(jnp.logical_not(used))
    def _():
        ys_ref[...] = jnp.zeros_like(ys_ref)

    @pl.when(used & new_expert)
    def _():
        slot = 1 - slot_ref[0]
        slot_ref[0] = slot
        for cp in weight_copies(expert, slot):
            cp.wait()
        wg_bf[...] = wg_f32[slot].astype(bf16)
        wu_bf[...] = wu_f32[slot].astype(bf16)
        wd_bf[...] = wd_f32[slot].astype(bf16)
        nxt = lax.while_loop(lambda t: (t < n_used) & (te_ref[0, jnp.minimum(t, LANE - 1)] == expert),
                             lambda t: t + 1, j + 1)

        @pl.when(nxt < n_used)
        def _():
            for cp in weight_copies(te_ref[0, nxt], 1 - slot):
                cp.start()

    @pl.when(used)
    def _():
        x = xs_ref[...]
        hg = jnp.dot(x, wg_bf[...], preferred_element_type=f32)
        hu = jnp.dot(x, wu_bf[...], preferred_element_type=f32)
        hid = (hg * jax.nn.sigmoid(hg) * hu).astype(bf16)
        ys_ref[...] = jnp.dot(hid, wd_bf[...], preferred_element_type=f32)


def _experts(te, xs, w_gate, w_up, w_down, layer):
    tm = MOE_TILE
    row = lambda j, te: (jnp.minimum(j, te[1, 0] - 1), 0)
    hbm = pl.BlockSpec(memory_space=pl.ANY)
    return pl.pallas_call(
        functools.partial(_expert_kernel, layer=layer),
        grid_spec=pltpu.PrefetchScalarGridSpec(
            num_scalar_prefetch=1,
            grid=(MOE_ROWS // tm,),
            in_specs=[pl.BlockSpec((tm, D_MODEL), row), hbm, hbm, hbm],
            out_specs=pl.BlockSpec((tm, D_MODEL), lambda j, te: (j, 0)),
            scratch_shapes=[pltpu.VMEM((2, D_MODEL, D_EXPERT), f32), pltpu.VMEM((2, D_MODEL, D_EXPERT), f32),
                            pltpu.VMEM((2, D_EXPERT, D_MODEL), f32),
                            pltpu.VMEM((D_MODEL, D_EXPERT), bf16), pltpu.VMEM((D_MODEL, D_EXPERT), bf16),
                            pltpu.VMEM((D_EXPERT, D_MODEL), bf16),
                            pltpu.SMEM((1,), jnp.int32), pltpu.SemaphoreType.DMA((2,))]),
        out_shape=jax.ShapeDtypeStruct((MOE_ROWS, D_MODEL), f32),
        compiler_params=pltpu.CompilerParams(dimension_semantics=("arbitrary",), vmem_limit_bytes=VMEM_LIMIT),
        name="moe_experts",
    )(te, xs, w_gate, w_up, w_down)


def _gather_expert_rows(pos0_ref, pos1_ref, ys_ref, buf, sem):
    rows = buf.shape[2]
    i = pl.program_id(0)
    slot = i % 2

    def issue(tile, sl):
        base = tile * rows

        def body(t, carry):
            for s, pos_ref in enumerate((pos0_ref, pos1_ref)):
                pltpu.make_async_copy(ys_ref.at[pl.ds(pos_ref[base + t], 1)], buf.at[sl, s, pl.ds(t, 1)],
                                      sem.at[sl]).start()
            return carry

        lax.fori_loop(0, rows, body, 0, unroll=8)

    @pl.when(i == 0)
    def _():
        issue(0, 0)

    @pl.when(i + 1 < pl.num_programs(0))
    def _():
        issue(i + 1, 1 - slot)

    for s in range(2):
        pltpu.make_async_copy(ys_ref.at[pl.ds(0, rows)], buf.at[slot, s], sem.at[slot]).wait()
    return buf[slot, 0], buf[slot, 1]


def _moe_residual(pos0_ref, pos1_ref, ys_ref, x1_ref, rc_ref, mod_ref, buf, sem):
    y0, y1 = _gather_expert_rows(pos0_ref, pos1_ref, ys_ref, buf, sem)
    rc = rc_ref[...]
    return x1_ref[...] + mod_ref[0][5:6] * (rc[:, 2:3] * y0 + rc[:, 3:4] * y1)


def _final_kernel(pos0_ref, pos1_ref, ys_ref, x1_ref, rc_ref, mod_ref, fn_ref, yc_ref, yl_ref, buf, sem):
    x2 = _moe_residual(pos0_ref, pos1_ref, ys_ref, x1_ref, rc_ref, mod_ref, buf, sem)
    y = x2 * lax.rsqrt(jnp.mean(x2 * x2, axis=-1, keepdims=True) + EPS) * fn_ref[...]
    is_ctx = pl.program_id(0) < N_CTX // x1_ref.shape[0]

    @pl.when(is_ctx)
    def _():
        yc_ref[...] = y

    @pl.when(jnp.logical_not(is_ctx))
    def _():
        yl_ref[...] = y


def _final_combine(pos0, pos1, ys, x1, rc, mods, fn):
    tc = COMBINE_TILE
    row = lambda i, p0, p1: (i, 0)
    n_ctx_tiles = N_CTX // tc
    return pl.pallas_call(
        _final_kernel,
        grid_spec=pltpu.PrefetchScalarGridSpec(
            num_scalar_prefetch=2,
            grid=(N_TOK // tc,),
            in_specs=[pl.BlockSpec(memory_space=pl.ANY),
                      pl.BlockSpec((tc, D_MODEL), row),
                      pl.BlockSpec((tc, LANE), row),
                      pl.BlockSpec((1, 6, D_MODEL), lambda i, p0, p1: (_mod_row(i, tc), 0, 0)),
                      pl.BlockSpec((1, D_MODEL), lambda i, p0, p1: (0, 0))],
            out_specs=[pl.BlockSpec((tc, D_MODEL), lambda i, p0, p1: (jnp.minimum(i, n_ctx_tiles - 1), 0)),
                       pl.BlockSpec((tc, D_MODEL), lambda i, p0, p1: (jnp.maximum(i - n_ctx_tiles, 0), 0))],
            scratch_shapes=[pltpu.VMEM((2, 2, tc, D_MODEL), f32), pltpu.SemaphoreType.DMA((2,))]),
        out_shape=[jax.ShapeDtypeStruct((N_CTX, D_MODEL), f32), jax.ShapeDtypeStruct((N_LAT, D_MODEL), f32)],
        compiler_params=pltpu.CompilerParams(dimension_semantics=("arbitrary",), vmem_limit_bytes=VMEM_LIMIT),
        name="moe_combine_final",
    )(pos0, pos1, ys, x1, rc, mods, fn)


def _moe_experts(h2, route_t, w_gate, w_up, w_down, layer, xs_buf):
    rows, te, runs = _rank(route_t)
    xs = _dispatch(runs[:, :3, :N_EXPERTS].reshape(-1), h2, rows, xs_buf)
    return rows[0], rows[1], _experts(te, xs, w_gate, w_up, w_down, layer), xs


def _pad_heads(w):
    lead = w.shape[:-1]
    w = w.reshape(*lead, ML_HEADS, ML_DIM)
    w = jnp.pad(w, [(0, 0)] * len(lead) + [(0, 0), (0, ML_PAD - ML_DIM)])
    return w.reshape(*lead, ML_PW)


def _pack_in_cols(wb):
    o = 0
    qa = wb[..., o:o + NA_WIDTH] * (NA_DIM ** -0.5)
    ka = wb[..., o + NA_WIDTH:o + 2 * NA_WIDTH]
    va = wb[..., o + 2 * NA_WIDTH:o + 3 * NA_WIDTH]
    o += 3 * NA_WIDTH
    qb, kb, vb, ob = [_pad_heads(wb[..., o + j * ML_WIDTH:o + (j + 1) * ML_WIDTH]) for j in range(4)]
    o += 4 * ML_WIDTH
    gates = wb[..., o:o + N_GATE_COLS]
    o += N_GATE_COLS
    pool = wb[..., o:o + POOL_WIDTH]
    main = jnp.concatenate([qa, ka, va, qb, vb, ob, pool], axis=-1)
    gates_p = jnp.pad(gates, [(0, 0)] * (gates.ndim - 1) + [(0, LANE - N_GATE_COLS)])
    return main, gates_p, jnp.concatenate([kb, gates], axis=-1)


def _pack_w_in(w, b):
    w_main, w_gates, w_feat = _pack_in_cols(w)
    b_main, b_gates, b_feat = _pack_in_cols(b.astype(f32))
    return (w_main.astype(bf16), b_main[:, None], w_gates.astype(bf16), b_gates[:, None],
            jnp.swapaxes(w_feat, 1, 2).astype(bf16), b_feat[:, :, None])


def _pack_w_out(w):
    n_l = w.shape[0]
    wb = w[:, NA_WIDTH:NA_WIDTH + ML_WIDTH].reshape(n_l, ML_HEADS, ML_DIM, D_MODEL)
    wb = jnp.pad(wb, ((0, 0), (0, 0), (0, ML_PAD - ML_DIM), (0, 0))).reshape(n_l, ML_PW, D_MODEL)
    return jnp.concatenate([w[:, :NA_WIDTH], wb, w[:, NA_WIDTH + ML_WIDTH:]], axis=1).astype(bf16)


def _block_diag(w):
    n_l, g, c, _ = w.shape
    eye = jnp.eye(g, dtype=w.dtype)
    return (eye[None, :, None, :, None] * w[:, :, :, None, :]).reshape(n_l, g * c, g * c)


def kernel(x_prompt, x_sample, cache_k_attn, cache_v_attn, state_mlstm_C, state_mlstm_n, state_mlstm_m, c, c_ctx,
           w_ada, b_ada, norm1, w_in, b_in, rpb, ml_norm, w_pool, pool_scale, w_out, norm2, w_router, b_router,
           w_gate, w_up, w_down, final_norm):
    dt = x_prompt.dtype
    x_ctx = x_prompt.reshape(N_CTX, D_MODEL).astype(f32)
    x_lat = x_sample.reshape(N_LAT, D_MODEL).astype(f32)
    x_lat_block0 = 0
    cvec = jnp.concatenate([c_ctx[None], c, jnp.zeros((8 - 1 - DEC_BATCH, D_MODEL), c.dtype)], axis=0).astype(f32)
    mods_all = _ada(cvec, w_ada.astype(f32), b_ada.astype(f32))
    mods_all = mods_all[:, :1 + DEC_BATCH].reshape(DEPTH, 1 + DEC_BATCH, 6, D_MODEL)

    wr_t = w_router.astype(f32).T
    br_t = b_router.astype(f32)[:, None]
    fn = final_norm.astype(f32)[None]

    na_bias = _na_bias_tables(rpb)
    xs_buf = jnp.zeros((MOE_ROWS, D_MODEL), bf16)
    in_params = (norm1.astype(f32)[:, None],) + _pack_w_in(w_in, b_in)
    out_params = (_block_diag(w_pool.astype(f32)).astype(bf16), pool_scale.astype(f32)[:, None],
                  _pad_heads(ml_norm.astype(f32))[:, None], _pack_w_out(w_out), norm2.astype(f32)[:, None])

    kv_cache = jnp.zeros((DEPTH, N_CTX + TOK_TILE, 2 * NA_WIDTH), f32)
    Cs, ns, ms = [], [], []
    pending = None
    for l in range(DEPTH):
        mods = mods_all[l]
        if pending is None:
            qkva, kv_cache, qvo, kt, gates, gates_t, pin = _in_proj(x_ctx, x_lat, mods, l, *in_params, kv_cache)
        else:
            x, qkva, kv_cache, qvo, kt, gates, gates_t, pin = _moe_in_proj(*pending, mods_all[l - 1], mods, l,
                                                                           *in_params, kv_cache)
            x_ctx, x_lat, x_lat_block0 = x, x, N_CTX // TOK_TILE

        oa_ctx = _ctx_attention(qkva.reshape(N_TOK // SEQ, SEQ, W_A))
        ck = (cache_k_attn[:, l].reshape(DEC_BATCH, PAST_LEN, NA_WIDTH)).astype(bf16)
        cv = (cache_v_attn[:, l].reshape(DEC_BATCH, PAST_LEN, NA_WIDTH)).astype(bf16)
        oa_lat = _neighborhood_attention(qkva.reshape(N_TOK // DEC_SEQ, DEC_SEQ, W_A), ck, cv, na_bias, l)

        c_l, m_l = _pack_ml_state(state_mlstm_C[:, l], state_mlstm_n[:, l], state_mlstm_m[:, l])
        hf, hb, c_fin, m_fin = _mlstm(qvo, kt, gates, gates_t, c_l, m_l)
        C_l, n_l, m_l2 = _unpack_ml_state(c_fin[:BATCH], m_fin[:BATCH])
        Cs.append(C_l)
        ns.append(n_l)
        ms.append(m_l2)

        x1, h2, route_t, rc = _out_proj(x_ctx, x_lat, x_lat_block0, mods, l,
                                        oa_ctx.reshape(N_CTX, NA_WIDTH), oa_lat.reshape(N_LAT, NA_WIDTH),
                                        hf.reshape(N_TOK, ML_PW), hb.reshape(N_TOK, ML_PW), qvo, pin,
                                        *out_params, wr_t, br_t)
        pos0, pos1, ys, xs_buf = _moe_experts(h2, route_t, w_gate, w_up, w_down, l, xs_buf)
        pending = (pos0, pos1, ys, x1, rc)

    x = _final_combine(*pending, mods_all[DEPTH - 1], fn)
    y_prompt = x[0].reshape(BATCH, SEQ, D_MODEL).astype(dt)
    y_sample = x[1].reshape(DEC_BATCH, DEC_SEQ, D_MODEL).astype(dt)
    new_kv = kv_cache[:, :N_CTX].reshape(DEPTH, BATCH, SEQ, 2, NA_HEADS, NA_DIM).transpose(3, 1, 0, 2, 4, 5)
    return (y_prompt, y_sample, new_kv[0].astype(dt), new_kv[1].astype(dt),
            jnp.stack(Cs, axis=1).astype(dt), jnp.stack(ns, axis=1).astype(dt), jnp.stack(ms, axis=1).astype(dt))
```

```python
import functools

import numpy as np
import jax
import jax.numpy as jnp
from jax import lax
from jax.experimental import pallas as pl
from jax.experimental.pallas import tpu as pltpu

D_MODEL = 1024
BATCH = 16
SEQ = 256
DEPTH = 4
DEC_BATCH = 2
DEC_SEQ = 4096
PAST_LEN = 256
GRID_W = 64
EPS = 1e-6
NEG_INF = -1e30
NA_HEADS = 6
NA_DIM = 64
NA_WIDTH = NA_HEADS * NA_DIM
NA_ROWS = 8
NA_COLS = 16
RPB_ROWS = 2 * NA_ROWS - 1
RPB_COLS = 2 * NA_COLS - 1
ML_HEADS = 4
ML_DIM = 96
ML_WIDTH = ML_HEADS * ML_DIM
POOL_WINDOWS = (2, 4, 8, 16)
POOL_GROUPS = 4
POOL_DIM = 64
POOL_WIDTH = POOL_GROUPS * POOL_DIM
N_GATE_COLS = 4 * ML_HEADS
N_EXPERTS = 16
N_EXPERT_GROUPS = 4
EXPERTS_PER_GROUP = N_EXPERTS // N_EXPERT_GROUPS
D_EXPERT = 512
ADA_DIM = 6 * D_MODEL

N_CTX = BATCH * SEQ
N_LAT = DEC_BATCH * DEC_SEQ
N_TOK = N_CTX + N_LAT
LANE = 128
ML_PAD = LANE
ML_PW = ML_HEADS * ML_PAD
CAUG = ML_PAD
NA_PAIRS = NA_HEADS // 2
TOK_TILE = 512
ML_CHUNK = 256
NA_QROWS = 4
NA_KROWS = NA_QROWS + NA_ROWS - 1
POOL_HALO = max(POOL_WINDOWS) // 2
POOL_BLOCK = 128
MOE_TILE = 512
MOE_CHUNK = 16
MOE_LOCAL_ROWS = -(-(2 * TOK_TILE + N_EXPERTS * (MOE_CHUNK - 1)) // LANE) * LANE
MOE_ROWS = -(-(2 * N_TOK + (N_TOK // TOK_TILE) * N_EXPERTS * (MOE_CHUNK - 1) + N_EXPERTS * (MOE_TILE - 1))
             // MOE_TILE) * MOE_TILE
COMBINE_TILE = 256
VMEM_LIMIT = 56 * 1024 * 1024

W_A = 3 * NA_WIDTH
W_B = 3 * ML_PW
N_TCOLS = ML_PW + N_GATE_COLS
W_MAIN = W_A + W_B + POOL_WIDTH

f32 = jnp.float32
bf16 = jnp.bfloat16
HI = lax.Precision.HIGHEST


def _nt(a, b, **kw):
    return lax.dot_general(a, b, (((1,), (1,)), ((), ())), preferred_element_type=f32, **kw)


def _mod_row(i, tile):
    n_ctx_tiles = N_CTX // tile
    per_batch = DEC_SEQ // tile
    return jnp.where(i < n_ctx_tiles, 0, 1 + (i - n_ctx_tiles) // per_batch)


def _ada_kernel(c_ref, w_ref, b_ref, o_ref):
    s = c_ref[...]
    s = s * jax.nn.sigmoid(s)
    o_ref[0] = jnp.dot(s.astype(bf16), w_ref[0].astype(bf16), preferred_element_type=f32) + b_ref[0]


def _ada(cvec, w_ada, b_ada):
    nj = ADA_DIM // D_MODEL
    return pl.pallas_call(
        _ada_kernel,
        grid=(DEPTH, nj),
        in_specs=[pl.BlockSpec((8, D_MODEL), lambda l, j: (0, 0)),
                  pl.BlockSpec((1, D_MODEL, D_MODEL), lambda l, j: (l, 0, j)),
                  pl.BlockSpec((1, 1, D_MODEL), lambda l, j: (l, 0, j))],
        out_specs=pl.BlockSpec((1, 8, D_MODEL), lambda l, j: (l, 0, j)),
        out_shape=jax.ShapeDtypeStruct((DEPTH, 8, ADA_DIM), f32),
        name="ada_mod",
    )(cvec, w_ada, b_ada.reshape(DEPTH, 1, ADA_DIM))


def _in_kernel(xc_ref, xl_ref, *refs):
    is_ctx = pl.program_id(0) < N_CTX // xc_ref.shape[0]
    _in_body(jnp.where(is_ctx, xc_ref[...], xl_ref[...]), *refs)


def _moe_in_kernel(pos0_ref, pos1_ref, ys_ref, x1_ref, rc_ref, mod_prev_ref, *refs):
    in_refs, x_out_ref, out_refs, (buf, sem) = refs[:9], refs[9], refs[10:-2], refs[-2:]
    x = _moe_residual(pos0_ref, pos1_ref, ys_ref, x1_ref, rc_ref, mod_prev_ref, buf, sem)
    x_out_ref[...] = x
    _in_body(x, *in_refs, *out_refs)


def _in_body(x, mod_ref, n1_ref, w_ref, b_ref, wg_ref, bg_ref, wt_ref, bt_ref, kv_in_ref,
             a_ref, kv_ref, b_out_ref, kt_ref, g_ref, gt_ref, pin_ref):
    del kv_in_ref
    mod = mod_ref[0]
    h = x * lax.rsqrt(jnp.mean(x * x, axis=-1, keepdims=True) + EPS) * n1_ref[...]
    h = (h * (1.0 + mod[1:2]) + mod[0:1]).astype(bf16)
    pa = jnp.dot(h, w_ref[:, 0:W_A], preferred_element_type=f32) + b_ref[:, 0:W_A]
    a_ref[...] = pa.astype(bf16)
    kv_ref[...] = pa[:, NA_WIDTH:W_A]
    for j in range(3):
        lo = W_A + j * ML_PW
        pb = jnp.dot(h, w_ref[:, lo:lo + ML_PW], preferred_element_type=f32) + b_ref[:, lo:lo + ML_PW]
        if j == 0:
            pb = pb * (ML_DIM ** -0.5)
        b_out_ref[:, j * ML_PW:(j + 1) * ML_PW] = pb.astype(bf16)
    lo = W_A + W_B
    pin_ref[...] = jnp.dot(h, w_ref[:, lo:lo + POOL_WIDTH], preferred_element_type=f32) + b_ref[:, lo:lo + POOL_WIDTH]
    g_ref[...] = jnp.dot(h, wg_ref[...], preferred_element_type=f32) + bg_ref[...]
    t = _nt(wt_ref[...], h) + bt_ref[...]
    kt_ref[...] = t[0:ML_PW].astype(bf16)
    gt_ref[...] = t[ML_PW:N_TCOLS]


def _in_proj_specs(layer):
    tm = TOK_TILE
    lyr = lambda shape: pl.BlockSpec((None,) + shape, lambda i, *_: (layer, 0, 0))
    rows = lambda width: pl.BlockSpec((tm, width), lambda i, *_: (i, 0))
    cols = lambda height: pl.BlockSpec((height, tm), lambda i, *_: (0, i))
    param_specs = [pl.BlockSpec((1, 6, D_MODEL), lambda i, *_: (_mod_row(i, tm), 0, 0)),
                   lyr((1, D_MODEL)), lyr((D_MODEL, W_MAIN)), lyr((1, W_MAIN)), lyr((D_MODEL, LANE)), lyr((1, LANE)),
                   lyr((N_TCOLS, D_MODEL)), lyr((N_TCOLS, 1)), pl.BlockSpec(memory_space=pl.ANY)]
    kv_spec = pl.BlockSpec((None, tm, 2 * NA_WIDTH), lambda i, *_: (layer, jnp.minimum(i, N_CTX // tm), 0))
    out_specs = [rows(W_A), kv_spec, rows(W_B), cols(ML_PW), rows(LANE), cols(N_GATE_COLS), rows(POOL_WIDTH)]
    out_shape = [jax.ShapeDtypeStruct((N_TOK, W_A), bf16),
                 jax.ShapeDtypeStruct((DEPTH, N_CTX + tm, 2 * NA_WIDTH), f32),
                 jax.ShapeDtypeStruct((N_TOK, W_B), bf16),
                 jax.ShapeDtypeStruct((ML_PW, N_TOK), bf16),
                 jax.ShapeDtypeStruct((N_TOK, LANE), f32),
                 jax.ShapeDtypeStruct((N_GATE_COLS, N_TOK), f32),
                 jax.ShapeDtypeStruct((N_TOK, POOL_WIDTH), f32)]
    return rows, param_specs, out_specs, out_shape


def _in_proj(x_ctx, x_lat, mods, layer, *params):
    tm = TOK_TILE
    n_ctx_tiles = N_CTX // tm
    rows, param_specs, out_specs, out_shape = _in_proj_specs(layer)
    return pl.pallas_call(
        _in_kernel,
        grid=(N_TOK // tm,),
        in_specs=[pl.BlockSpec((tm, D_MODEL), lambda i: (jnp.minimum(i, n_ctx_tiles - 1), 0)),
                  pl.BlockSpec((tm, D_MODEL), lambda i: (jnp.maximum(i - n_ctx_tiles, 0), 0))] + param_specs,
        out_specs=out_specs,
        out_shape=out_shape,
        input_output_aliases={2 + len(param_specs) - 1: 1},
        compiler_params=pltpu.CompilerParams(dimension_semantics=("arbitrary",), vmem_limit_bytes=VMEM_LIMIT),
        name="in_proj",
    )(x_ctx, x_lat, mods, *params)


def _moe_in_proj(pos0, pos1, ys, x1, rc, mods_prev, mods, layer, *params):
    tm = TOK_TILE
    rows, param_specs, out_specs, out_shape = _in_proj_specs(layer)
    return pl.pallas_call(
        _moe_in_kernel,
        grid_spec=pltpu.PrefetchScalarGridSpec(
            num_scalar_prefetch=2,
            grid=(N_TOK // tm,),
            in_specs=[pl.BlockSpec(memory_space=pl.ANY), rows(D_MODEL), rows(LANE),
                      pl.BlockSpec((1, 6, D_MODEL), lambda i, *_: (_mod_row(i, tm), 0, 0))] + param_specs,
            out_specs=[rows(D_MODEL)] + out_specs,
            scratch_shapes=[pltpu.VMEM((2, 2, tm, D_MODEL), f32), pltpu.SemaphoreType.DMA((2,))]),
        out_shape=[jax.ShapeDtypeStruct((N_TOK, D_MODEL), f32)] + out_shape,
        input_output_aliases={6 + len(param_specs) - 1: 2},
        compiler_params=pltpu.CompilerParams(dimension_semantics=("arbitrary",), vmem_limit_bytes=VMEM_LIMIT),
        name="moe_combine_in_proj",
    )(pos0, pos1, ys, x1, rc, mods_prev, mods, *params)


def _pair_attention(qp, parts):
    lane = lax.broadcasted_iota(jnp.int32, (1, LANE), 1)
    outs = []
    for j in range(2):
        in_half = (lane >= j * NA_DIM) & (lane < (j + 1) * NA_DIM)
        qm = jnp.where(in_half, qp, jnp.zeros_like(qp))
        scores = []
        for k, _, bias in parts:
            s = _nt(qm, k)
            if bias is not None:
                s = s + bias[j]
            scores.append(s)
        m = scores[0].max(axis=-1, keepdims=True)
        for s in scores[1:]:
            m = jnp.maximum(m, s.max(axis=-1, keepdims=True))
        den = None
        acc = None
        for s, (_, v, _) in zip(scores, parts):
            p = jnp.exp(s - m)
            ps = p.sum(axis=-1, keepdims=True)
            den = ps if den is None else den + ps
            o = jnp.dot(p.astype(bf16), v, preferred_element_type=f32)
            acc = o if acc is None else acc + o
        outs.append(acc / den)
    return jnp.where(lane < NA_DIM, outs[0], outs[1])


def _ctx_attn_kernel(q_ref, k_ref, v_ref, o_ref):
    for p in range(NA_PAIRS):
        sl = slice(p * LANE, (p + 1) * LANE)
        o = _pair_attention(q_ref[0, :, sl], [(k_ref[0, :, sl], v_ref[0, :, sl], None)])
        o_ref[0, :, sl] = o.astype(bf16)


def _ctx_attention(qkv):
    blk = lambda c: pl.BlockSpec((1, SEQ, NA_WIDTH), lambda b, c=c: (b, 0, c))
    return pl.pallas_call(
        _ctx_attn_kernel,
        grid=(BATCH,),
        in_specs=[blk(0), blk(1), blk(2)],
        out_specs=pl.BlockSpec((1, SEQ, NA_WIDTH), lambda b: (b, 0, 0)),
        out_shape=jax.ShapeDtypeStruct((BATCH, SEQ, NA_WIDTH), bf16),
        name="ctx_attention",
    )(qkv, qkv, qkv)


def _na_window_start(rb):
    return jnp.clip(rb * NA_QROWS - NA_ROWS // 2, 0, DEC_SEQ // GRID_W - NA_KROWS)


def _na_bias(tab_ref, head, rb):
    rows = DEC_SEQ // GRID_W
    ws = _na_window_start(rb)
    lane = lax.broadcasted_iota(jnp.int32, (1, NA_KROWS * GRID_W), 1)
    per_qrow = []
    for dq in range(NA_QROWS):
        qr = rb * NA_QROWS + dq
        a0 = ws - qr + (NA_ROWS - 1) + NA_KROWS
        tiles = [tab_ref[head, a0 + 2 * j] for j in range((NA_KROWS + 1) // 2)]
        t = jnp.concatenate(tiles, axis=1)[:, :NA_KROWS * GRID_W]
        lo = (jnp.clip(qr - NA_ROWS // 2, 0, rows - NA_ROWS) - ws) * GRID_W
        ok = (lane >= lo) & (lane < lo + NA_ROWS * GRID_W)
        per_qrow.append(jnp.where(ok, t, NEG_INF))
    return jnp.concatenate(per_qrow, axis=0)


def _na_kernel(q_ref, k_ref, v_ref, ck_ref, cv_ref, tab_ref, o_ref):
    rb = pl.program_id(1)
    start = pl.multiple_of(_na_window_start(rb) * GRID_W, GRID_W)
    nk = NA_KROWS * GRID_W
    for p in range(NA_PAIRS):
        sl = slice(p * LANE, (p + 1) * LANE)
        bias = [_na_bias(tab_ref.at[0], 2 * p + j, rb) for j in range(2)]
        parts = [(k_ref[0, pl.ds(start, nk), sl], v_ref[0, pl.ds(start, nk), sl], bias),
                 (ck_ref[0, :, sl], cv_ref[0, :, sl], None)]
        o = _pair_attention(q_ref[0, :, sl], parts)
        o_ref[0, :, sl] = o.astype(bf16)


def _neighborhood_attention(qkv, ck, cv, tables, layer):
    nq = NA_QROWS * GRID_W
    n_rb = DEC_SEQ // nq
    return pl.pallas_call(
        _na_kernel,
        grid=(DEC_BATCH, n_rb),
        in_specs=[pl.BlockSpec((1, nq, NA_WIDTH), lambda b, r: (1 + b, r, 0)),
                  pl.BlockSpec((1, DEC_SEQ, NA_WIDTH), lambda b, r: (1 + b, 0, 1)),
                  pl.BlockSpec((1, DEC_SEQ, NA_WIDTH), lambda b, r: (1 + b, 0, 2)),
                  pl.BlockSpec((1, PAST_LEN, NA_WIDTH), lambda b, r: (b, 0, 0)),
                  pl.BlockSpec((1, PAST_LEN, NA_WIDTH), lambda b, r: (b, 0, 0)),
                  pl.BlockSpec((1,) + tables.shape[1:], lambda b, r: (layer, 0, 0, 0, 0))],
        out_specs=pl.BlockSpec((1, nq, NA_WIDTH), lambda b, r: (b, r, 0)),
        out_shape=jax.ShapeDtypeStruct((DEC_BATCH, DEC_SEQ, NA_WIDTH), bf16),
        compiler_params=pltpu.CompilerParams(dimension_semantics=("arbitrary", "arbitrary"),
                                             vmem_limit_bytes=VMEM_LIMIT),
        name="neighborhood_attention",
    )(qkv, qkv, qkv, ck, cv, tables)


def _na_bias_tables(rpb):
    qc = np.arange(GRID_W)[:, None]
    kc = np.arange(GRID_W)[None, :]
    dc = np.clip(kc - qc + NA_COLS - 1, 0, RPB_COLS - 1)
    col_start = np.clip(qc - NA_COLS // 2, 0, GRID_W - NA_COLS)
    col_ok = (kc >= col_start) & (kc < col_start + NA_COLS)
    pick_col = (dc[None] == np.arange(RPB_COLS)[:, None, None]).astype(np.float32)
    rpb_pad = jnp.pad(rpb.astype(f32), ((0, 0), (0, 0), (NA_KROWS, NA_KROWS + 1), (0, 0)))
    n_a = rpb_pad.shape[2] - 1
    rows2 = jnp.stack([rpb_pad[:, :, :-1], rpb_pad[:, :, 1:]], axis=3)
    pick2 = np.zeros((2, RPB_COLS, GRID_W, 2 * GRID_W), np.float32)
    for j in range(2):
        pick2[j, :, :, j * GRID_W:(j + 1) * GRID_W] = pick_col
    tiles = jnp.einsum('lhajb,jbqc->lhaqc', rows2, pick2, precision=HI)
    a_pad = np.arange(n_a)[:, None] + np.arange(2)[None, :]
    row_ok = (a_pad >= NA_KROWS) & (a_pad < NA_KROWS + RPB_ROWS)
    ok = (row_ok[:, None, :, None] & col_ok[None, :, None, :]).reshape(n_a, GRID_W, 2 * GRID_W)
    return jnp.where(ok[None, None], tiles, NEG_INF)


def _log_sigmoid(x):
    return -(jnp.maximum(-x, 0.0) + jnp.log(1.0 + jnp.exp(-jnp.abs(x))))


def _split3(x):
    hi = x.astype(bf16)
    r1 = x - hi.astype(f32)
    mid = r1.astype(bf16)
    lo = (r1 - mid.astype(f32)).astype(bf16)
    return hi, mid, lo


def _mlstm_kernel(qf_ref, vf_ref, ktf_ref, gf_ref, gtf_ref, qb_ref, vb_ref, ktb_ref, gb_ref, gtb_ref,
                  c0_ref, m0_ref, hf_ref, hb_ref, c_out_ref, m_out_ref, c_scr, m_scr):
    L = ML_CHUNK
    seq, c, n_chunks, _ = _ml_schedule(pl.program_id(0))

    @pl.when(c == 0)
    def _():
        is_ctx = seq < BATCH
        c_scr[...] = jnp.where(is_ctx, 0.0, c0_ref[0])
        m_scr[...] = jnp.where(is_ctx, 0.0, m0_ref[0])

    ri = lax.broadcasted_iota(jnp.int32, (L, L), 0)
    ci = lax.broadcasted_iota(jnp.int32, (L, L), 1)
    lane = lax.broadcasted_iota(jnp.int32, (L, ML_PAD), 1)
    is_ncol = lane == ML_DIM
    lower = ri >= ci
    upper = ri <= ci
    lower_b = jnp.where(lower, 1.0, 0.0).astype(bf16)
    upper_b = jnp.where(upper, 1.0, 0.0).astype(bf16)
    dirs = ((qf_ref, ktf_ref, vf_ref, gf_ref, gtf_ref, hf_ref), (qb_ref, ktb_ref, vb_ref, gb_ref, gtb_ref, hb_ref))
    for d, (q_ref, kt_ref, v_ref, g_ref, gt_ref, h_ref) in enumerate(dirs):
        g = g_ref[...][:, 0:N_GATE_COLS]
        gt = gt_ref[...]
        lf_c = _log_sigmoid(g)
        lf_r = _log_sigmoid(gt)
        b_cols = sum(jnp.dot(lower_b, part, preferred_element_type=f32) for part in _split3(lf_c))
        b_rows = sum(jnp.dot(part, upper_b, preferred_element_type=f32) for part in _split3(lf_r))
        tot_c = jnp.sum(lf_c, axis=0, keepdims=True)
        tot_r = jnp.sum(lf_r, axis=1, keepdims=True)
        visible = lower
        if d == 1:
            b_cols = tot_c - b_cols + lf_c
            b_rows = tot_r - b_rows + lf_r
            visible = upper
        for hd in range(ML_HEADS):
            st = d * ML_HEADS + hd
            ci_ = 2 * ML_HEADS * d + hd
            cf_ = ci_ + ML_HEADS
            sl = slice(hd * ML_PAD, (hd + 1) * ML_PAD)
            bc = b_cols[:, cf_:cf_ + 1]
            br = b_rows[cf_:cf_ + 1, :]
            li_r = gt[ci_:ci_ + 1, :]
            m_prev = m_scr[st:st + 1, 0:1]
            dmat = jnp.where(visible, bc - br + li_r, NEG_INF)
            inter = bc + m_prev
            m_t = jnp.maximum(inter, dmat.max(axis=-1, keepdims=True))
            w_intra = jnp.exp(dmat - m_t)
            w_inter = jnp.exp(inter - m_t)
            qh = q_ref[0, :, sl]
            kht = kt_ref[sl, :]
            v_aug = jnp.where(is_ncol, jnp.ones((), bf16), v_ref[0, :, sl])
            s = (jnp.dot(qh, kht, preferred_element_type=f32) * w_intra).astype(bf16)
            c_aug = c_scr[st]
            na = (w_inter * jnp.dot(qh, c_aug.astype(bf16), preferred_element_type=f32)
                  + jnp.dot(s, v_aug, preferred_element_type=f32))
            den = na[:, ML_DIM:ML_DIM + 1]
            h_ref[0, :, sl] = jnp.where(lane < ML_DIM, na / jnp.maximum(jnp.abs(den), jnp.exp(-m_t)), 0.0)
            b_end = tot_r[cf_:cf_ + 1, :]
            g_row = b_end - br + li_r
            m_new = jnp.maximum(b_end + m_prev, g_row.max(axis=1, keepdims=True))
            decay = jnp.exp(b_end + m_prev - m_new)
            kwt = (kht.astype(f32) * jnp.exp(g_row - m_new)).astype(bf16)
            c_scr[st] = decay * c_aug + jnp.dot(kwt, v_aug, preferred_element_type=f32)
            m_scr[st:st + 1, :] = jnp.broadcast_to(m_new, (1, LANE))

    @pl.when(c == n_chunks - 1)
    def _():
        c_out_ref[0] = c_scr[...]
        m_out_ref[0] = m_scr[...]


def _ml_schedule(s):
    nc_ctx, nc_lat = SEQ // ML_CHUNK, DEC_SEQ // ML_CHUNK
    n_ctx_steps = BATCH * nc_ctx
    is_ctx = s < n_ctx_steps
    t = s - n_ctx_steps
    seq = jnp.where(is_ctx, s // nc_ctx, BATCH + t // nc_lat)
    c = jnp.where(is_ctx, s % nc_ctx, t % nc_lat)
    nc = jnp.where(is_ctx, nc_ctx, nc_lat)
    base = jnp.where(is_ctx, (s // nc_ctx) * nc_ctx, n_ctx_steps + (t // nc_lat) * nc_lat)
    return seq, c, nc, base


def _mlstm(qvo, kt, gates, gates_t, c0, m0):
    L = ML_CHUNK
    n_seq = BATCH + DEC_BATCH

    def fwd(s):
        _, c, _, base = _ml_schedule(s)
        return base + c

    def bwd(s):
        _, c, nc, base = _ml_schedule(s)
        return base + nc - 1 - c

    seq_of = lambda s: _ml_schedule(s)[0]
    lat_of = lambda s: jnp.maximum(seq_of(s) - BATCH, 0)

    def specs(pos):
        return [pl.BlockSpec((1, L, ML_PW), lambda s, j=j: (pos(s), 0, j)) for j in range(2)] + [
            pl.BlockSpec((ML_PW, L), lambda s: (0, pos(s))),
            pl.BlockSpec((L, LANE), lambda s: (pos(s), 0)),
            pl.BlockSpec((N_GATE_COLS, L), lambda s: (0, pos(s)))]

    q3 = qvo.reshape(N_TOK // L, L, W_B)
    n_str = 2 * ML_HEADS
    return pl.pallas_call(
        _mlstm_kernel,
        grid=(N_TOK // L,),
        in_specs=specs(fwd) + specs(bwd) + [
            pl.BlockSpec((1, n_str, ML_PAD, CAUG), lambda s: (lat_of(s), 0, 0, 0)),
            pl.BlockSpec((1, n_str, LANE), lambda s: (lat_of(s), 0, 0))],
        out_specs=[pl.BlockSpec((1, L, ML_PW), lambda s: (fwd(s), 0, 0)),
                   pl.BlockSpec((1, L, ML_PW), lambda s: (bwd(s), 0, 0)),
                   pl.BlockSpec((1, n_str, ML_PAD, CAUG), lambda s: (seq_of(s), 0, 0, 0)),
                   pl.BlockSpec((1, n_str, LANE), lambda s: (seq_of(s), 0, 0))],
        out_shape=[jax.ShapeDtypeStruct((N_TOK // L, L, ML_PW), f32),
                   jax.ShapeDtypeStruct((N_TOK // L, L, ML_PW), f32),
                   jax.ShapeDtypeStruct((n_seq, n_str, ML_PAD, CAUG), f32),
                   jax.ShapeDtypeStruct((n_seq, n_str, LANE), f32)],
        scratch_shapes=[pltpu.VMEM((n_str, ML_PAD, CAUG), f32), pltpu.VMEM((n_str, LANE), f32)],
        compiler_params=pltpu.CompilerParams(dimension_semantics=("arbitrary",), vmem_limit_bytes=VMEM_LIMIT),
        name="mlstm",
    )(q3, q3, kt, gates, gates_t, q3, q3, kt, gates, gates_t, c0, m0)


def _pack_ml_state(C, n, m):
    B = C.shape[0]
    c_aug = jnp.zeros((B, 2, ML_HEADS, ML_PAD, CAUG), f32)
    c_aug = c_aug.at[:, :, :, :ML_DIM, :ML_DIM].set(C.astype(f32))
    c_aug = c_aug.at[:, :, :, :ML_DIM, ML_DIM].set(n.astype(f32))
    m_b = jnp.broadcast_to(m.astype(f32)[..., None], (B, 2, ML_HEADS, LANE))
    return c_aug.reshape(B, 2 * ML_HEADS, ML_PAD, CAUG), m_b.reshape(B, 2 * ML_HEADS, LANE)


def _unpack_ml_state(c_aug, m_b):
    B = c_aug.shape[0]
    c_aug = c_aug.reshape(B, 2, ML_HEADS, ML_PAD, CAUG)
    return (c_aug[:, :, :, :ML_DIM, :ML_DIM], c_aug[:, :, :, :ML_DIM, ML_DIM],
            m_b.reshape(B, 2, ML_HEADS, LANE)[..., 0])


def _pool_rows(u_prev, u_cur, u_next, w_bd, scale, t0, seq_len):
    tm = u_cur.shape[0]
    u_win = jnp.concatenate([u_prev, u_cur, u_next], axis=0)
    u_hi = u_win.astype(bf16)
    u_lo = (u_win - u_hi.astype(f32)).astype(bf16)
    lane = lax.broadcasted_iota(jnp.int32, (1, LANE), 1)
    blocks = []
    for r0 in range(0, tm, POOL_BLOCK):
        win = slice(r0, r0 + POOL_BLOCK + 2 * POOL_HALO)
        t_abs = t0 + r0 + lax.broadcasted_iota(jnp.int32, (POOL_BLOCK, 1), 0)
        s_abs = t0 + r0 - POOL_HALO + lax.broadcasted_iota(jnp.int32, (1, POOL_BLOCK + 2 * POOL_HALO), 1)
        t_loc = t_abs & (seq_len - 1)
        seq_start = t_abs - t_loc
        means = []
        for w in POOL_WINDOWS:
            lo = jnp.maximum(t_loc - w // 2, 0)
            hi = jnp.minimum(t_loc - w // 2 + w, seq_len)
            in_win = (s_abs >= seq_start + lo) & (s_abs < seq_start + hi)
            means.append((jnp.where(in_win, 1.0, 0.0).astype(bf16), 1.0 / (hi - lo).astype(f32)))
        pooled = []
        for p in range(POOL_GROUPS // 2):
            sl = slice(p * LANE, (p + 1) * LANE)
            halves = []
            for a, inv_cnt in means[2 * p:2 * p + 2]:
                tot = (jnp.dot(a, u_hi[win, sl], preferred_element_type=f32)
                       + jnp.dot(a, u_lo[win, sl], preferred_element_type=f32))
                halves.append(tot * inv_cnt)
            pooled.append(jnp.where(lane < POOL_DIM, halves[0], halves[1]) - u_cur[r0:r0 + POOL_BLOCK, sl])
        blocks.append(jnp.concatenate(pooled, axis=1))
    pooled = jnp.concatenate(blocks, axis=0).astype(bf16)
    return jnp.dot(pooled, w_bd, preferred_element_type=f32) * scale


def _top2_sum(a, b, c, d):
    hi1, lo1 = jnp.maximum(a, b), jnp.minimum(a, b)
    hi2, lo2 = jnp.maximum(c, d), jnp.minimum(c, d)
    return jnp.maximum(hi1, hi2) + jnp.maximum(jnp.minimum(hi1, hi2), jnp.maximum(lo1, lo2))


def _first_match(vals, target):
    idx = jnp.full_like(target, float(len(vals) - 1))
    for i in range(len(vals) - 2, -1, -1):
        idx = jnp.where(vals[i] == target, float(i), idx)
    return idx


def _pick(vals, idx):
    out = vals[-1]
    for i in range(len(vals) - 2, -1, -1):
        out = jnp.where(idx == float(i), vals[i], out)
    return out


def _route(logits_t, bias_t):
    scores = jax.nn.sigmoid(logits_t)
    sel = scores + bias_t
    row = lambda a, i: a[i:i + 1, :]
    grp = [_top2_sum(*[row(sel, EXPERTS_PER_GROUP * g + i) for i in range(EXPERTS_PER_GROUP)])
           for g in range(N_EXPERT_GROUPS)]
    best = functools.reduce(jnp.maximum, grp)
    gidx = _first_match(grp, best)
    sel_g = [_pick([row(sel, EXPERTS_PER_GROUP * g + i) for g in range(N_EXPERT_GROUPS)], gidx)
             for i in range(EXPERTS_PER_GROUP)]
    sco_g = [_pick([row(scores, EXPERTS_PER_GROUP * g + i) for g in range(N_EXPERT_GROUPS)], gidx)
             for i in range(EXPERTS_PER_GROUP)]
    i0 = _first_match(sel_g, functools.reduce(jnp.maximum, sel_g))
    rest = [jnp.where(i0 == float(i), -jnp.inf, sel_g[i]) for i in range(EXPERTS_PER_GROUP)]
    i1 = _first_match(rest, functools.reduce(jnp.maximum, rest))
    s0, s1 = _pick(sco_g, i0), _pick(sco_g, i1)
    tot = s0 + s1
    rid = lax.broadcasted_iota(jnp.int32, (LANE, logits_t.shape[1]), 0)
    rows = (EXPERTS_PER_GROUP * gidx + i0, EXPERTS_PER_GROUP * gidx + i1, s0 / tot, s1 / tot)
    out = jnp.zeros(rid.shape, f32)
    for i, r in enumerate(rows):
        out = jnp.where(rid == i, r, out)
    return out


def _out_kernel(xc_ref, xl_ref, mod_ref, oac_ref, oal_ref, hf_ref, hb_ref, ob_ref, up_ref, uc_ref, un_ref, wp_ref, psc_ref,
                mln_ref, wo_ref, n2_ref, wr_ref, br_ref, x1_ref, h2_ref, rt_ref, rc_ref):
    tm = xc_ref.shape[0]
    i = pl.program_id(0)
    is_ctx = i < N_CTX // tm
    mod = mod_ref[0]
    out_a = jnp.where(is_ctx, oac_ref[...], oal_ref[...])
    out_c = _pool_rows(up_ref[...], uc_ref[...], un_ref[...], wp_ref[...], psc_ref[...], i * tm,
                       jnp.where(is_ctx, SEQ, DEC_SEQ)).astype(bf16)
    hsum = hf_ref[...] + hb_ref[...]
    outs_b = []
    for hd in range(ML_HEADS):
        sl = slice(hd * ML_PAD, (hd + 1) * ML_PAD)
        hh = hsum[:, sl]
        ms = jnp.sum(hh * hh, axis=-1, keepdims=True) * (1.0 / ML_DIM)
        hn = hh * lax.rsqrt(ms + EPS) * mln_ref[:, sl]
        outs_b.append((jax.nn.sigmoid(ob_ref[:, sl].astype(f32)) * hn).astype(bf16))
    out_b = jnp.concatenate(outs_b, axis=1)
    mixed = (jnp.dot(out_a, wo_ref[0:NA_WIDTH, :], preferred_element_type=f32)
             + jnp.dot(out_b, wo_ref[NA_WIDTH:NA_WIDTH + ML_PW, :], preferred_element_type=f32)
             + jnp.dot(out_c, wo_ref[NA_WIDTH + ML_PW:, :], preferred_element_type=f32))
    x1 = jnp.where(is_ctx, xc_ref[...], xl_ref[...]) + mod[2:3] * mixed
    x1_ref[...] = x1
    h2 = x1 * lax.rsqrt(jnp.mean(x1 * x1, axis=-1, keepdims=True) + EPS) * n2_ref[...]
    h2 = h2 * (1.0 + mod[4:5]) + mod[3:4]
    h2_ref[...] = h2.astype(bf16)
    h_hi = h2.astype(bf16)
    h_lo = (h2 - h_hi.astype(f32)).astype(bf16)
    w_hi = wr_ref[...].astype(bf16)
    w_lo = (wr_ref[...] - w_hi.astype(f32)).astype(bf16)
    route_t = _route(_nt(w_hi, h_hi) + (_nt(w_hi, h_lo) + _nt(w_lo, h_hi)), br_ref[...])
    rt_ref[...] = route_t[0:8]
    rc_ref[...] = route_t.T


def _out_proj(x_ctx, x_lat, x_lat_block0, mods, layer, oa_ctx, oa_lat, hf, hb, qvo, pin, w_bd, psc, mln, wo, n2,
              wr_t, br_t):
    tm = TOK_TILE
    const = lambda i: (0, 0)
    lyr = lambda shape: pl.BlockSpec((None,) + shape, lambda i: (layer, 0, 0))
    row = lambda i: (i, 0)
    n_ctx_tiles = N_CTX // tm
    halo_blocks = tm // POOL_HALO
    return pl.pallas_call(
        _out_kernel,
        grid=(N_TOK // tm,),
        in_specs=[pl.BlockSpec((tm, D_MODEL), lambda i: (jnp.minimum(i, n_ctx_tiles - 1), 0)),
                  pl.BlockSpec((tm, D_MODEL), lambda i: (jnp.maximum(i - n_ctx_tiles, 0) + x_lat_block0, 0)),
                  pl.BlockSpec((1, 6, D_MODEL), lambda i: (_mod_row(i, tm), 0, 0)),
                  pl.BlockSpec((tm, NA_WIDTH), lambda i: (jnp.minimum(i, n_ctx_tiles - 1), 0)),
                  pl.BlockSpec((tm, NA_WIDTH), lambda i: (jnp.maximum(i - n_ctx_tiles, 0), 0)),
                  pl.BlockSpec((tm, ML_PW), row),
                  pl.BlockSpec((tm, ML_PW), row),
                  pl.BlockSpec((tm, ML_PW), lambda i: (i, 2)),
                  pl.BlockSpec((POOL_HALO, POOL_WIDTH), lambda i: (jnp.maximum(i * halo_blocks - 1, 0), 0)),
                  pl.BlockSpec((tm, POOL_WIDTH), row),
                  pl.BlockSpec((POOL_HALO, POOL_WIDTH),
                               lambda i: (jnp.minimum((i + 1) * halo_blocks, N_TOK // POOL_HALO - 1), 0)),
                  lyr((POOL_WIDTH, POOL_WIDTH)), lyr((1, POOL_WIDTH)), lyr((1, ML_PW)),
                  lyr((NA_WIDTH + ML_PW + POOL_WIDTH, D_MODEL)), lyr((1, D_MODEL)),
                  pl.BlockSpec((N_EXPERTS, D_MODEL), const),
                  pl.BlockSpec((N_EXPERTS, 1), const)],
        out_specs=[pl.BlockSpec((tm, D_MODEL), row),
                   pl.BlockSpec((tm, D_MODEL), row),
                   pl.BlockSpec((8, tm), lambda i: (0, i)),
                   pl.BlockSpec((tm, LANE), row)],
        out_shape=[jax.ShapeDtypeStruct((N_TOK, D_MODEL), f32),
                   jax.ShapeDtypeStruct((N_TOK, D_MODEL), bf16),
                   jax.ShapeDtypeStruct((8, N_TOK), f32),
                   jax.ShapeDtypeStruct((N_TOK, LANE), f32)],
        compiler_params=pltpu.CompilerParams(dimension_semantics=("arbitrary",), vmem_limit_bytes=VMEM_LIMIT),
        name="out_proj_router",
    )(x_ctx, x_lat, mods, oa_ctx, oa_lat, hf, hb, qvo, pin, pin, pin, w_bd, psc, mln, wo, n2, wr_t, br_t)


def _ceil_to(x, m):
    return jnp.floor((x + (m - 1)) * (1.0 / m)) * m


def _prefix_over_experts(v):
    er = lax.broadcasted_iota(jnp.int32, (N_EXPERTS, N_EXPERTS), 0)
    ec = lax.broadcasted_iota(jnp.int32, (N_EXPERTS, N_EXPERTS), 1)
    return jnp.dot(jnp.where(ec < er, 1.0, 0.0), v, preferred_element_type=f32, precision=HI)


def _experts_to_lanes(v):
    sub = lax.broadcasted_iota(jnp.int32, (N_EXPERTS, LANE), 0)
    lane = lax.broadcasted_iota(jnp.int32, (N_EXPERTS, LANE), 1)
    return jnp.sum(jnp.where(sub == lane, v, 0.0), axis=0, keepdims=True)


def _expert_hits(rt):
    rid = lax.broadcasted_iota(jnp.int32, (N_EXPERTS, rt.shape[1]), 0).astype(f32)
    oh0 = rid == rt[0:1, :]
    oh1 = rid == rt[1:2, :]
    both = jnp.where(oh0 | oh1, 1.0, 0.0)
    runs = jnp.broadcast_to(_ceil_to(jnp.sum(both, axis=1, keepdims=True), MOE_CHUNK), (N_EXPERTS, LANE))
    return oh0, oh1, both, runs


def _rank_kernel(rt_all_ref, rt_ref, pos_ref, te_ref, tab_ref, carry_ref):
    tm = rt_ref.shape[1]
    step = pl.program_id(0)

    @pl.when(step == 0)
    def _():
        totals = jnp.zeros((N_EXPERTS, LANE), f32)
        for i in range(rt_all_ref.shape[1] // tm):
            totals = totals + _expert_hits(rt_all_ref[:, i * tm:(i + 1) * tm])[3]
        padded = _ceil_to(totals, MOE_TILE)
        off = _prefix_over_experts(padded)
        carry_ref[...] = off
        total = jnp.sum(padded, axis=0, keepdims=True)
        n_used = total * (1.0 / MOE_TILE)
        tile = lax.broadcasted_iota(jnp.int32, (1, LANE), 1).astype(f32)
        row0 = jnp.minimum(tile, n_used - 1.0) * MOE_TILE
        expert = jnp.sum(jnp.where(off <= row0, 1.0, 0.0), axis=0, keepdims=True) - 1.0
        sub = lax.broadcasted_iota(jnp.int32, (8, LANE), 0)
        te_ref[...] = jnp.where(sub == 0, expert, jnp.where(sub == 1, n_used, 0.0)).astype(jnp.int32)

    @pl.when(step > 0)
    def _():
        oh0, oh1, both, runs = _expert_hits(rt_ref[...])
        sr = lax.broadcasted_iota(jnp.int32, (tm, tm), 0)
        sc = lax.broadcasted_iota(jnp.int32, (tm, tm), 1)
        earlier = jnp.dot(both.astype(bf16), jnp.where(sr < sc, 1.0, 0.0).astype(bf16),
                          preferred_element_type=f32)
        g_off = carry_ref[...]
        l_off = _prefix_over_experts(runs)
        g_row = g_off[:, 0:1] + earlier
        l_row = l_off[:, 0:1] + earlier
        pick = lambda oh, v: jnp.sum(jnp.where(oh, v, 0.0), axis=0, keepdims=True)
        rows = (pick(oh0, g_row), pick(oh1, g_row), pick(oh0, l_row), pick(oh1, l_row))
        sub = lax.broadcasted_iota(jnp.int32, (8, tm), 0)
        out = jnp.zeros((8, tm), f32)
        for k, r in enumerate(rows):
            out = jnp.where(sub == k, r, out)
        pos_ref[...] = out.astype(jnp.int32)
        sub = lax.broadcasted_iota(jnp.int32, (8, LANE), 0)
        tab = jnp.zeros((8, LANE), f32)
        for k, v in enumerate((runs * (1.0 / MOE_CHUNK), l_off, g_off)):
            tab = jnp.where(sub == k, _experts_to_lanes(v), tab)
        tab_ref[0] = tab.astype(jnp.int32)
        carry_ref[...] = g_off + runs


def _rank(route_t):
    tm = TOK_TILE
    n_tiles = N_TOK // tm
    tile_of = lambda s: jnp.maximum(s - 1, 0)
    return pl.pallas_call(
        _rank_kernel,
        grid=(1 + n_tiles,),
        in_specs=[pl.BlockSpec((8, N_TOK), lambda s: (0, 0)),
                  pl.BlockSpec((8, tm), lambda s: (0, tile_of(s)))],
        out_specs=[pl.BlockSpec((8, tm), lambda s: (0, tile_of(s))),
                   pl.BlockSpec((8, LANE), lambda s: (0, 0)),
                   pl.BlockSpec((1, 8, LANE), lambda s: (tile_of(s), 0, 0))],
        out_shape=[jax.ShapeDtypeStruct((8, N_TOK), jnp.int32),
                   jax.ShapeDtypeStruct((8, LANE), jnp.int32),
                   jax.ShapeDtypeStruct((n_tiles, 8, LANE), jnp.int32)],
        scratch_shapes=[pltpu.VMEM((N_EXPERTS, LANE), f32)],
        compiler_params=pltpu.CompilerParams(dimension_semantics=("arbitrary",)),
        name="moe_rank",
    )(route_t, route_t)


def _dispatch_kernel(tab_ref, h_ref, rows_ref, xs_in_ref, xs_ref, loc, sem):
    del xs_in_ref
    tm = h_ref.shape[0]
    i = pl.program_id(0)
    rid = lax.broadcasted_iota(jnp.int32, (MOE_LOCAL_ROWS, tm), 0)
    sel = (rid == rows_ref[2:3, :]) | (rid == rows_ref[3:4, :])
    loc[...] = jnp.dot(jnp.where(sel, 1.0, 0.0).astype(bf16), h_ref[...], preferred_element_type=f32).astype(bf16)

    def chunk_copy(src_row, dst_row):
        return pltpu.make_async_copy(loc.at[pl.ds(pl.multiple_of(src_row, MOE_CHUNK), MOE_CHUNK)],
                                     xs_ref.at[pl.ds(pl.multiple_of(dst_row, MOE_CHUNK), MOE_CHUNK)], sem)

    n_total = 0
    for e in range(N_EXPERTS):
        n_chunks, l_off, g_off = (tab_ref[(3 * i + k) * N_EXPERTS + e] for k in range(3))

        def issue(c, carry, l_off=l_off, g_off=g_off):
            chunk_copy(l_off + c * MOE_CHUNK, g_off + c * MOE_CHUNK).start()
            return carry

        lax.fori_loop(0, n_chunks, issue, 0)
        n_total = n_total + n_chunks

    def wait_one(c, carry):
        chunk_copy(0, 0).wait()
        return carry

    lax.fori_loop(0, n_total, wait_one, 0)


def _dispatch(run_table, h2, rows, xs_init):
    tm = TOK_TILE
    return pl.pallas_call(
        _dispatch_kernel,
        grid_spec=pltpu.PrefetchScalarGridSpec(
            num_scalar_prefetch=1,
            grid=(N_TOK // tm,),
            in_specs=[pl.BlockSpec((tm, D_MODEL), lambda i, tab: (i, 0)),
                      pl.BlockSpec((8, tm), lambda i, tab: (0, i)),
                      pl.BlockSpec(memory_space=pl.ANY)],
            out_specs=pl.BlockSpec(memory_space=pl.ANY),
            scratch_shapes=[pltpu.VMEM((MOE_LOCAL_ROWS, D_MODEL), bf16), pltpu.SemaphoreType.DMA(())]),
        out_shape=jax.ShapeDtypeStruct(xs_init.shape, xs_init.dtype),
        input_output_aliases={3: 0},
        compiler_params=pltpu.CompilerParams(dimension_semantics=("arbitrary",), vmem_limit_bytes=VMEM_LIMIT),
        name="moe_dispatch",
    )(run_table, h2, rows, xs_init)


def _expert_kernel(te_ref, xs_ref, wg_hbm, wu_hbm, wd_hbm, ys_ref, wg_f32, wu_f32, wd_f32, wg_bf, wu_bf, wd_bf,
                   slot_ref, sem, *, layer):
    j = pl.program_id(0)
    n_used = te_ref[1, 0]
    used = j < n_used
    expert = te_ref[0, j]
    new_expert = jnp.logical_or(j == 0, expert != te_ref[0, jnp.maximum(j - 1, 0)])

    def weight_copies(e, slot):
        return [pltpu.make_async_copy(hbm.at[layer, e], buf.at[slot], sem.at[slot])
                for hbm, buf in ((wg_hbm, wg_f32), (wu_hbm, wu_f32), (wd_hbm, wd_f32))]

    @pl.when(j == 0)
    def _():
        slot_ref[0] = 1
        for cp in weight_copies(expert, 0):
            cp.start()

    @pl.when(jnp.logical_not(used))
    def _():
        ys_ref[...] = jnp.zeros_like(ys_ref)

    @pl.when(used & new_expert)
    def _():
        slot = 1 - slot_ref[0]
        slot_ref[0] = slot
        for cp in weight_copies(expert, slot):
            cp.wait()
        wg_bf[...] = wg_f32[slot].astype(bf16)
        wu_bf[...] = wu_f32[slot].astype(bf16)
        wd_bf[...] = wd_f32[slot].astype(bf16)
        nxt = lax.while_loop(lambda t: (t < n_used) & (te_ref[0, jnp.minimum(t, LANE - 1)] == expert),
                             lambda t: t + 1, j + 1)

        @pl.when(nxt < n_used)
        def _():
            for cp in weight_copies(te_ref[0, nxt], 1 - slot):
                cp.start()

    @pl.when(used)
    def _():
        x = xs_ref[...]
        hg = jnp.dot(x, wg_bf[...], preferred_element_type=f32)
        hu = jnp.dot(x, wu_bf[...], preferred_element_type=f32)
        hid = (hg * jax.nn.sigmoid(hg) * hu).astype(bf16)
        ys_ref[...] = jnp.dot(hid, wd_bf[...], preferred_element_type=f32)


def _experts(te, xs, w_gate, w_up, w_down, layer):
    tm = MOE_TILE
    row = lambda j, te: (jnp.minimum(j, te[1, 0] - 1), 0)
    hbm = pl.BlockSpec(memory_space=pl.ANY)
    return pl.pallas_call(
        functools.partial(_expert_kernel, layer=layer),
        grid_spec=pltpu.PrefetchScalarGridSpec(
            num_scalar_prefetch=1,
            grid=(MOE_ROWS // tm,),
            in_specs=[pl.BlockSpec((tm, D_MODEL), row), hbm, hbm, hbm],
            out_specs=pl.BlockSpec((tm, D_MODEL), lambda j, te: (j, 0)),
            scratch_shapes=[pltpu.VMEM((2, D_MODEL, D_EXPERT), f32), pltpu.VMEM((2, D_MODEL, D_EXPERT), f32),
                            pltpu.VMEM((2, D_EXPERT, D_MODEL), f32),
                            pltpu.VMEM((D_MODEL, D_EXPERT), bf16), pltpu.VMEM((D_MODEL, D_EXPERT), bf16),
                            pltpu.VMEM((D_EXPERT, D_MODEL), bf16),
                            pltpu.SMEM((1,), jnp.int32), pltpu.SemaphoreType.DMA((2,))]),
        out_shape=jax.ShapeDtypeStruct((MOE_ROWS, D_MODEL), f32),
        compiler_params=pltpu.CompilerParams(dimension_semantics=("arbitrary",), vmem_limit_bytes=VMEM_LIMIT),
        name="moe_experts",
    )(te, xs, w_gate, w_up, w_down)


def _gather_expert_rows(pos0_ref, pos1_ref, ys_ref, buf, sem):
    rows = buf.shape[2]
    i = pl.program_id(0)
    slot = i % 2

    def issue(tile, sl):
        base = tile * rows

        def body(t, carry):
            for s, pos_ref in enumerate((pos0_ref, pos1_ref)):
                pltpu.make_async_copy(ys_ref.at[pl.ds(pos_ref[base + t], 1)], buf.at[sl, s, pl.ds(t, 1)],
                                      sem.at[sl]).start()
            return carry

        lax.fori_loop(0, rows, body, 0, unroll=8)

    @pl.when(i == 0)
    def _():
        issue(0, 0)

    @pl.when(i + 1 < pl.num_programs(0))
    def _():
        issue(i + 1, 1 - slot)

    for s in range(2):
        pltpu.make_async_copy(ys_ref.at[pl.ds(0, rows)], buf.at[slot, s], sem.at[slot]).wait()
    return buf[slot, 0], buf[slot, 1]


def _moe_residual(pos0_ref, pos1_ref, ys_ref, x1_ref, rc_ref, mod_ref, buf, sem):
    y0, y1 = _gather_expert_rows(pos0_ref, pos1_ref, ys_ref, buf, sem)
    rc = rc_ref[...]
    return x1_ref[...] + mod_ref[0][5:6] * (rc[:, 2:3] * y0 + rc[:, 3:4] * y1)


def _final_kernel(pos0_ref, pos1_ref, ys_ref, x1_ref, rc_ref, mod_ref, fn_ref, yc_ref, yl_ref, buf, sem):
    x2 = _moe_residual(pos0_ref, pos1_ref, ys_ref, x1_ref, rc_ref, mod_ref, buf, sem)
    y = x2 * lax.rsqrt(jnp.mean(x2 * x2, axis=-1, keepdims=True) + EPS) * fn_ref[...]
    is_ctx = pl.program_id(0) < N_CTX // x1_ref.shape[0]

    @pl.when(is_ctx)
    def _():
        yc_ref[...] = y

    @pl.when(jnp.logical_not(is_ctx))
    def _():
        yl_ref[...] = y


def _final_combine(pos0, pos1, ys, x1, rc, mods, fn):
    tc = COMBINE_TILE
    row = lambda i, p0, p1: (i, 0)
    n_ctx_tiles = N_CTX // tc
    return pl.pallas_call(
        _final_kernel,
        grid_spec=pltpu.PrefetchScalarGridSpec(
            num_scalar_prefetch=2,
            grid=(N_TOK // tc,),
            in_specs=[pl.BlockSpec(memory_space=pl.ANY),
                      pl.BlockSpec((tc, D_MODEL), row),
                      pl.BlockSpec((tc, LANE), row),
                      pl.BlockSpec((1, 6, D_MODEL), lambda i, p0, p1: (_mod_row(i, tc), 0, 0)),
                      pl.BlockSpec((1, D_MODEL), lambda i, p0, p1: (0, 0))],
            out_specs=[pl.BlockSpec((tc, D_MODEL), lambda i, p0, p1: (jnp.minimum(i, n_ctx_tiles - 1), 0)),
                       pl.BlockSpec((tc, D_MODEL), lambda i, p0, p1: (jnp.maximum(i - n_ctx_tiles, 0), 0))],
            scratch_shapes=[pltpu.VMEM((2, 2, tc, D_MODEL), f32), pltpu.SemaphoreType.DMA((2,))]),
        out_shape=[jax.ShapeDtypeStruct((N_CTX, D_MODEL), f32), jax.ShapeDtypeStruct((N_LAT, D_MODEL), f32)],
        compiler_params=pltpu.CompilerParams(dimension_semantics=("arbitrary",), vmem_limit_bytes=VMEM_LIMIT),
        name="moe_combine_final",
    )(pos0, pos1, ys, x1, rc, mods, fn)


def _moe_experts(h2, route_t, w_gate, w_up, w_down, layer, xs_buf):
    rows, te, runs = _rank(route_t)
    xs = _dispatch(runs[:, :3, :N_EXPERTS].reshape(-1), h2, rows, xs_buf)
    return rows[0], rows[1], _experts(te, xs, w_gate, w_up, w_down, layer), xs


def _pad_heads(w):
    lead = w.shape[:-1]
    w = w.reshape(*lead, ML_HEADS, ML_DIM)
    w = jnp.pad(w, [(0, 0)] * len(lead) + [(0, 0), (0, ML_PAD - ML_DIM)])
    return w.reshape(*lead, ML_PW)


def _pack_in_cols(wb):
    o = 0
    qa = wb[..., o:o + NA_WIDTH] * (NA_DIM ** -0.5)
    ka = wb[..., o + NA_WIDTH:o + 2 * NA_WIDTH]
    va = wb[..., o + 2 * NA_WIDTH:o + 3 * NA_WIDTH]
    o += 3 * NA_WIDTH
    qb, kb, vb, ob = [_pad_heads(wb[..., o + j * ML_WIDTH:o + (j + 1) * ML_WIDTH]) for j in range(4)]
    o += 4 * ML_WIDTH
    gates = wb[..., o:o + N_GATE_COLS]
    o += N_GATE_COLS
    pool = wb[..., o:o + POOL_WIDTH]
    main = jnp.concatenate([qa, ka, va, qb, vb, ob, pool], axis=-1)
    gates_p = jnp.pad(gates, [(0, 0)] * (gates.ndim - 1) + [(0, LANE - N_GATE_COLS)])
    return main, gates_p, jnp.concatenate([kb, gates], axis=-1)


def _pack_w_in(w, b):
    w_main, w_gates, w_feat = _pack_in_cols(w)
    b_main, b_gates, b_feat = _pack_in_cols(b.astype(f32))
    return (w_main.astype(bf16), b_main[:, None], w_gates.astype(bf16), b_gates[:, None],
            jnp.swapaxes(w_feat, 1, 2).astype(bf16), b_feat[:, :, None])


def _pack_w_out(w):
    n_l = w.shape[0]
    wb = w[:, NA_WIDTH:NA_WIDTH + ML_WIDTH].reshape(n_l, ML_HEADS, ML_DIM, D_MODEL)
    wb = jnp.pad(wb, ((0, 0), (0, 0), (0, ML_PAD - ML_DIM), (0, 0))).reshape(n_l, ML_PW, D_MODEL)
    return jnp.concatenate([w[:, :NA_WIDTH], wb, w[:, NA_WIDTH + ML_WIDTH:]], axis=1).astype(bf16)


def _block_diag(w):
    n_l, g, c, _ = w.shape
    eye = jnp.eye(g, dtype=w.dtype)
    return (eye[None, :, None, :, None] * w[:, :, :, None, :]).reshape(n_l, g * c, g * c)


def kernel(x_prompt, x_sample, cache_k_attn, cache_v_attn, state_mlstm_C, state_mlstm_n, state_mlstm_m, c, c_ctx,
           w_ada, b_ada, norm1, w_in, b_in, rpb, ml_norm, w_pool, pool_scale, w_out, norm2, w_router, b_router,
           w_gate, w_up, w_down, final_norm):
    dt = x_prompt.dtype
    x_ctx = x_prompt.reshape(N_CTX, D_MODEL).astype(f32)
    x_lat = x_sample.reshape(N_LAT, D_MODEL).astype(f32)
    x_lat_block0 = 0
    cvec = jnp.concatenate([c_ctx[None], c, jnp.zeros((8 - 1 - DEC_BATCH, D_MODEL), c.dtype)], axis=0).astype(f32)
    mods_all = _ada(cvec, w_ada.astype(f32), b_ada.astype(f32))
    mods_all = mods_all[:, :1 + DEC_BATCH].reshape(DEPTH, 1 + DEC_BATCH, 6, D_MODEL)

    wr_t = w_router.astype(f32).T
    br_t = b_router.astype(f32)[:, None]
    fn = final_norm.astype(f32)[None]

    na_bias = _na_bias_tables(rpb)
    xs_buf = jnp.zeros((MOE_ROWS, D_MODEL), bf16)
    in_params = (norm1.astype(f32)[:, None],) + _pack_w_in(w_in, b_in)
    out_params = (_block_diag(w_pool.astype(f32)).astype(bf16), pool_scale.astype(f32)[:, None],
                  _pad_heads(ml_norm.astype(f32))[:, None], _pack_w_out(w_out), norm2.astype(f32)[:, None])

    kv_cache = jnp.zeros((DEPTH, N_CTX + TOK_TILE, 2 * NA_WIDTH), f32)
    Cs, ns, ms = [], [], []
    pending = None
    for l in range(DEPTH):
        mods = mods_all[l]
        if pending is None:
            qkva, kv_cache, qvo, kt, gates, gates_t, pin = _in_proj(x_ctx, x_lat, mods, l, *in_params, kv_cache)
        else:
            x, qkva, kv_cache, qvo, kt, gates, gates_t, pin = _moe_in_proj(*pending, mods_all[l - 1], mods, l,
                                                                           *in_params, kv_cache)
            x_ctx, x_lat, x_lat_block0 = x, x, N_CTX // TOK_TILE

        oa_ctx = _ctx_attention(qkva.reshape(N_TOK // SEQ, SEQ, W_A))
        ck = (cache_k_attn[:, l].reshape(DEC_BATCH, PAST_LEN, NA_WIDTH)).astype(bf16)
        cv = (cache_v_attn[:, l].reshape(DEC_BATCH, PAST_LEN, NA_WIDTH)).astype(bf16)
        oa_lat = _neighborhood_attention(qkva.reshape(N_TOK // DEC_SEQ, DEC_SEQ, W_A), ck, cv, na_bias, l)

        c_l, m_l = _pack_ml_state(state_mlstm_C[:, l], state_mlstm_n[:, l], state_mlstm_m[:, l])
        hf, hb, c_fin, m_fin = _mlstm(qvo, kt, gates, gates_t, c_l, m_l)
        C_l, n_l, m_l2 = _unpack_ml_state(c_fin[:BATCH], m_fin[:BATCH])
        Cs.append(C_l)
        ns.append(n_l)
        ms.append(m_l2)

        x1, h2, route_t, rc = _out_proj(x_ctx, x_lat, x_lat_block0, mods, l,
                                        oa_ctx.reshape(N_CTX, NA_WIDTH), oa_lat.reshape(N_LAT, NA_WIDTH),
                                        hf.reshape(N_TOK, ML_PW), hb.reshape(N_TOK, ML_PW), qvo, pin,
                                        *out_params, wr_t, br_t)
        pos0, pos1, ys, xs_buf = _moe_experts(h2, route_t, w_gate, w_up, w_down, l, xs_buf)
        pending = (pos0, pos1, ys, x1, rc)

    x = _final_combine(*pending, mods_all[DEPTH - 1], fn)
    y_prompt = x[0].reshape(BATCH, SEQ, D_MODEL).astype(dt)
    y_sample = x[1].reshape(DEC_BATCH, DEC_SEQ, D_MODEL).astype(dt)
    new_kv = kv_cache[:, :N_CTX].reshape(DEPTH, BATCH, SEQ, 2, NA_HEADS, NA_DIM).transpose(3, 1, 0, 2, 4, 5)
    return (y_prompt, y_sample, new_kv[0].astype(dt), new_kv[1].astype(dt),
            jnp.stack(Cs, axis=1).astype(dt), jnp.stack(ns, axis=1).astype(dt), jnp.stack(ms, axis=1).astype(dt))
```

```python
import functools

import numpy as np
import jax
import jax.numpy as jnp
from jax import lax
from jax.experimental import pallas as pl
from jax.experimental.pallas import tpu as pltpu

D_MODEL = 1024
BATCH = 16
SEQ = 256
DEPTH = 4
DEC_BATCH = 2
DEC_SEQ = 4096
PAST_LEN = 256
GRID_W = 64
EPS = 1e-6
NEG_INF = -1e30
NA_HEADS = 6
NA_DIM = 64
NA_WIDTH = NA_HEADS * NA_DIM
NA_ROWS = 8
NA_COLS = 16
RPB_ROWS = 2 * NA_ROWS - 1
RPB_COLS = 2 * NA_COLS - 1
ML_HEADS = 4
ML_DIM = 96
ML_WIDTH = ML_HEADS * ML_DIM
POOL_WINDOWS = (2, 4, 8, 16)
POOL_GROUPS = 4
POOL_DIM = 64
POOL_WIDTH = POOL_GROUPS * POOL_DIM
N_GATE_COLS = 4 * ML_HEADS
N_EXPERTS = 16
N_EXPERT_GROUPS = 4
EXPERTS_PER_GROUP = N_EXPERTS // N_EXPERT_GROUPS
D_EXPERT = 512
ADA_DIM = 6 * D_MODEL

N_CTX = BATCH * SEQ
N_LAT = DEC_BATCH * DEC_SEQ
N_TOK = N_CTX + N_LAT
LANE = 128
ML_PAD = LANE
ML_PW = ML_HEADS * ML_PAD
CAUG = ML_PAD
NA_PAIRS = NA_HEADS // 2
TOK_TILE = 512
ML_CHUNK = 256
NA_QROWS = 4
NA_KROWS = NA_QROWS + NA_ROWS - 1
POOL_HALO = max(POOL_WINDOWS) // 2
POOL_BLOCK = 128
MOE_TILE = 512
MOE_CHUNK = 16
MOE_LOCAL_ROWS = -(-(2 * TOK_TILE + N_EXPERTS * (MOE_CHUNK - 1)) // LANE) * LANE
MOE_ROWS = -(-(2 * N_TOK + (N_TOK // TOK_TILE) * N_EXPERTS * (MOE_CHUNK - 1) + N_EXPERTS * (MOE_TILE - 1))
             // MOE_TILE) * MOE_TILE
COMBINE_TILE = 256
VMEM_LIMIT = 56 * 1024 * 1024

W_A = 3 * NA_WIDTH
W_B = 3 * ML_PW
N_TCOLS = ML_PW + N_GATE_COLS
W_MAIN = W_A + W_B + POOL_WIDTH

f32 = jnp.float32
bf16 = jnp.bfloat16
HI = lax.Precision.HIGHEST


def _nt(a, b, **kw):
    return lax.dot_general(a, b, (((1,), (1,)), ((), ())), preferred_element_type=f32, **kw)


def _mod_row(i, tile):
    n_ctx_tiles = N_CTX // tile
    per_batch = DEC_SEQ // tile
    return jnp.where(i < n_ctx_tiles, 0, 1 + (i - n_ctx_tiles) // per_batch)


def _ada_kernel(c_ref, w_ref, b_ref, o_ref):
    s = c_ref[...]
    s = s * jax.nn.sigmoid(s)
    o_ref[0] = jnp.dot(s.astype(bf16), w_ref[0].astype(bf16), preferred_element_type=f32) + b_ref[0]


def _ada(cvec, w_ada, b_ada):
    nj = ADA_DIM // D_MODEL
    return pl.pallas_call(
        _ada_kernel,
        grid=(DEPTH, nj),
        in_specs=[pl.BlockSpec((8, D_MODEL), lambda l, j: (0, 0)),
                  pl.BlockSpec((1, D_MODEL, D_MODEL), lambda l, j: (l, 0, j)),
                  pl.BlockSpec((1, 1, D_MODEL), lambda l, j: (l, 0, j))],
        out_specs=pl.BlockSpec((1, 8, D_MODEL), lambda l, j: (l, 0, j)),
        out_shape=jax.ShapeDtypeStruct((DEPTH, 8, ADA_DIM), f32),
        name="ada_mod",
    )(cvec, w_ada, b_ada.reshape(DEPTH, 1, ADA_DIM))


def _in_kernel(xc_ref, xl_ref, *refs):
    is_ctx = pl.program_id(0) < N_CTX // xc_ref.shape[0]
    _in_body(jnp.where(is_ctx, xc_ref[...], xl_ref[...]), *refs)


def _moe_in_kernel(pos0_ref, pos1_ref, ys_ref, x1_ref, rc_ref, mod_prev_ref, *refs):
    in_refs, x_out_ref, out_refs, (buf, sem) = refs[:10], refs[10], refs[11:-2], refs[-2:]
    x = _moe_residual(pos0_ref, pos1_ref, ys_ref, x1_ref, rc_ref, mod_prev_ref, buf, sem)
    x_out_ref[...] = x
    _in_body(x, *in_refs, *out_refs)


def _in_body(x, mod_ref, n1_ref, w_ref, b_ref, wg_ref, bg_ref, wt_ref, bt_ref, k_in_ref, v_in_ref,
             a_ref, k_ref, v_ref, b_out_ref, kt_ref, g_ref, gt_ref, pin_ref):
    del k_in_ref, v_in_ref
    mod = mod_ref[0]
    h = x * lax.rsqrt(jnp.mean(x * x, axis=-1, keepdims=True) + EPS) * n1_ref[...]
    h = (h * (1.0 + mod[1:2]) + mod[0:1]).astype(bf16)
    pa = jnp.dot(h, w_ref[:, 0:W_A], preferred_element_type=f32) + b_ref[:, 0:W_A]
    a_ref[...] = pa.astype(bf16)
    k_ref[...] = pa[:, NA_WIDTH:2 * NA_WIDTH].reshape(k_ref.shape)
    v_ref[...] = pa[:, 2 * NA_WIDTH:W_A].reshape(v_ref.shape)
    for j in range(3):
        lo = W_A + j * ML_PW
        pb = jnp.dot(h, w_ref[:, lo:lo + ML_PW], preferred_element_type=f32) + b_ref[:, lo:lo + ML_PW]
        if j == 0:
            pb = pb * (ML_DIM ** -0.5)
        b_out_ref[:, j * ML_PW:(j + 1) * ML_PW] = pb.astype(bf16)
    lo = W_A + W_B
    pin_ref[...] = jnp.dot(h, w_ref[:, lo:lo + POOL_WIDTH], preferred_element_type=f32) + b_ref[:, lo:lo + POOL_WIDTH]
    g_ref[...] = jnp.dot(h, wg_ref[...], preferred_element_type=f32) + bg_ref[...]
    t = _nt(wt_ref[...], h) + bt_ref[...]
    kt_ref[...] = t[0:ML_PW].astype(bf16)
    gt_ref[...] = t[ML_PW:N_TCOLS]


def _in_proj_specs(layer):
    tm = TOK_TILE
    lyr = lambda shape: pl.BlockSpec((None,) + shape, lambda i, *_: (layer, 0, 0))
    rows = lambda width: pl.BlockSpec((tm, width), lambda i, *_: (i, 0))
    cols = lambda height: pl.BlockSpec((height, tm), lambda i, *_: (0, i))
    param_specs = [pl.BlockSpec((1, 6, D_MODEL), lambda i, *_: (_mod_row(i, tm), 0, 0)),
                   lyr((1, D_MODEL)), lyr((D_MODEL, W_MAIN)), lyr((1, W_MAIN)), lyr((D_MODEL, LANE)), lyr((1, LANE)),
                   lyr((N_TCOLS, D_MODEL)), lyr((N_TCOLS, 1)),
                   pl.BlockSpec(memory_space=pl.ANY), pl.BlockSpec(memory_space=pl.ANY)]
    seqs = tm // SEQ
    kv_spec = pl.BlockSpec((seqs, None, SEQ, NA_WIDTH), lambda i, *_: (jnp.minimum(i, N_CTX // tm), layer, 0, 0))
    kv_shape = jax.ShapeDtypeStruct((BATCH + seqs, DEPTH, SEQ, NA_WIDTH), f32)
    out_specs = [rows(W_A), kv_spec, kv_spec, rows(W_B), cols(ML_PW), rows(LANE), cols(N_GATE_COLS),
                 rows(POOL_WIDTH)]
    out_shape = [jax.ShapeDtypeStruct((N_TOK, W_A), bf16), kv_shape, kv_shape,
                 jax.ShapeDtypeStruct((N_TOK, W_B), bf16),
                 jax.ShapeDtypeStruct((ML_PW, N_TOK), bf16),
                 jax.ShapeDtypeStruct((N_TOK, LANE), f32),
                 jax.ShapeDtypeStruct((N_GATE_COLS, N_TOK), f32),
                 jax.ShapeDtypeStruct((N_TOK, POOL_WIDTH), f32)]
    return rows, param_specs, out_specs, out_shape


def _in_proj(x_ctx, x_lat, mods, layer, *params):
    tm = TOK_TILE
    n_ctx_tiles = N_CTX // tm
    rows, param_specs, out_specs, out_shape = _in_proj_specs(layer)
    return pl.pallas_call(
        _in_kernel,
        grid=(N_TOK // tm,),
        in_specs=[pl.BlockSpec((tm, D_MODEL), lambda i: (jnp.minimum(i, n_ctx_tiles - 1), 0)),
                  pl.BlockSpec((tm, D_MODEL), lambda i: (jnp.maximum(i - n_ctx_tiles, 0), 0))] + param_specs,
        out_specs=out_specs,
        out_shape=out_shape,
        input_output_aliases={2 + len(param_specs) - 2: 1, 2 + len(param_specs) - 1: 2},
        compiler_params=pltpu.CompilerParams(dimension_semantics=("arbitrary",), vmem_limit_bytes=VMEM_LIMIT),
        name="in_proj",
    )(x_ctx, x_lat, mods, *params)


def _moe_in_proj(pos0, pos1, ys, x1, rc, mods_prev, mods, layer, *params):
    tm = TOK_TILE
    rows, param_specs, out_specs, out_shape = _in_proj_specs(layer)
    return pl.pallas_call(
        _moe_in_kernel,
        grid_spec=pltpu.PrefetchScalarGridSpec(
            num_scalar_prefetch=2,
            grid=(N_TOK // tm,),
            in_specs=[pl.BlockSpec(memory_space=pl.ANY), rows(D_MODEL), rows(LANE),
                      pl.BlockSpec((1, 6, D_MODEL), lambda i, *_: (_mod_row(i, tm), 0, 0))] + param_specs,
            out_specs=[rows(D_MODEL)] + out_specs,
            scratch_shapes=[pltpu.VMEM((2, 2, tm, D_MODEL), f32), pltpu.SemaphoreType.DMA((2,))]),
        out_shape=[jax.ShapeDtypeStruct((N_TOK, D_MODEL), f32)] + out_shape,
        input_output_aliases={6 + len(param_specs) - 2: 2, 6 + len(param_specs) - 1: 3},
        compiler_params=pltpu.CompilerParams(dimension_semantics=("arbitrary",), vmem_limit_bytes=VMEM_LIMIT),
        name="moe_combine_in_proj",
    )(pos0, pos1, ys, x1, rc, mods_prev, mods, *params)


def _pair_attention(qp, parts):
    lane = lax.broadcasted_iota(jnp.int32, (1, LANE), 1)
    outs = []
    for j in range(2):
        in_half = (lane >= j * NA_DIM) & (lane < (j + 1) * NA_DIM)
        qm = jnp.where(in_half, qp, jnp.zeros_like(qp))
        scores = []
        for k, _, bias in parts:
            s = _nt(qm, k)
            if bias is not None:
                s = s + bias[j]
            scores.append(s)
        m = scores[0].max(axis=-1, keepdims=True)
        for s in scores[1:]:
            m = jnp.maximum(m, s.max(axis=-1, keepdims=True))
        den = None
        acc = None
        for s, (_, v, _) in zip(scores, parts):
            p = jnp.exp(s - m)
            ps = p.sum(axis=-1, keepdims=True)
            den = ps if den is None else den + ps
            o = jnp.dot(p.astype(bf16), v, preferred_element_type=f32)
            acc = o if acc is None else acc + o
        outs.append(acc / den)
    return jnp.where(lane < NA_DIM, outs[0], outs[1])


def _ctx_attn_kernel(q_ref, k_ref, v_ref, o_ref):
    for p in range(NA_PAIRS):
        sl = slice(p * LANE, (p + 1) * LANE)
        o = _pair_attention(q_ref[0, :, sl], [(k_ref[0, :, sl], v_ref[0, :, sl], None)])
        o_ref[0, :, sl] = o.astype(bf16)


def _ctx_attention(qkv):
    blk = lambda c: pl.BlockSpec((1, SEQ, NA_WIDTH), lambda b, c=c: (b, 0, c))
    return pl.pallas_call(
        _ctx_attn_kernel,
        grid=(BATCH,),
        in_specs=[blk(0), blk(1), blk(2)],
        out_specs=pl.BlockSpec((1, SEQ, NA_WIDTH), lambda b: (b, 0, 0)),
        out_shape=jax.ShapeDtypeStruct((BATCH, SEQ, NA_WIDTH), bf16),
        name="ctx_attention",
    )(qkv, qkv, qkv)


def _na_window_start(rb):
    return jnp.clip(rb * NA_QROWS - NA_ROWS // 2, 0, DEC_SEQ // GRID_W - NA_KROWS)


def _na_bias(tab_ref, head, rb):
    rows = DEC_SEQ // GRID_W
    ws = _na_window_start(rb)
    lane = lax.broadcasted_iota(jnp.int32, (1, NA_KROWS * GRID_W), 1)
    per_qrow = []
    for dq in range(NA_QROWS):
        qr = rb * NA_QROWS + dq
        a0 = ws - qr + (NA_ROWS - 1) + NA_KROWS
        tiles = [tab_ref[head, a0 + 2 * j] for j in range((NA_KROWS + 1) // 2)]
        t = jnp.concatenate(tiles, axis=1)[:, :NA_KROWS * GRID_W]
        lo = (jnp.clip(qr - NA_ROWS // 2, 0, rows - NA_ROWS) - ws) * GRID_W
        ok = (lane >= lo) & (lane < lo + NA_ROWS * GRID_W)
        per_qrow.append(jnp.where(ok, t, NEG_INF))
    return jnp.concatenate(per_qrow, axis=0)


def _na_kernel(q_ref, k_ref, v_ref, ck_ref, cv_ref, tab_ref, o_ref):
    rb = pl.program_id(1)
    start = pl.multiple_of(_na_window_start(rb) * GRID_W, GRID_W)
    nk = NA_KROWS * GRID_W
    for p in range(NA_PAIRS):
        sl = slice(p * LANE, (p + 1) * LANE)
        bias = [_na_bias(tab_ref.at[0], 2 * p + j, rb) for j in range(2)]
        parts = [(k_ref[0, pl.ds(start, nk), sl], v_ref[0, pl.ds(start, nk), sl], bias),
                 (ck_ref[0, :, sl], cv_ref[0, :, sl], None)]
        o = _pair_attention(q_ref[0, :, sl], parts)
        o_ref[0, :, sl] = o.astype(bf16)


def _neighborhood_attention(qkv, ck, cv, tables, layer):
    nq = NA_QROWS * GRID_W
    n_rb = DEC_SEQ // nq
    return pl.pallas_call(
        _na_kernel,
        grid=(DEC_BATCH, n_rb),
        in_specs=[pl.BlockSpec((1, nq, NA_WIDTH), lambda b, r: (1 + b, r, 0)),
                  pl.BlockSpec((1, DEC_SEQ, NA_WIDTH), lambda b, r: (1 + b, 0, 1)),
                  pl.BlockSpec((1, DEC_SEQ, NA_WIDTH), lambda b, r: (1 + b, 0, 2)),
                  pl.BlockSpec((1, PAST_LEN, NA_WIDTH), lambda b, r: (b, 0, 0)),
                  pl.BlockSpec((1, PAST_LEN, NA_WIDTH), lambda b, r: (b, 0, 0)),
                  pl.BlockSpec((1,) + tables.shape[1:], lambda b, r: (layer, 0, 0, 0, 0))],
        out_specs=pl.BlockSpec((1, nq, NA_WIDTH), lambda b, r: (b, r, 0)),
        out_shape=jax.ShapeDtypeStruct((DEC_BATCH, DEC_SEQ, NA_WIDTH), bf16),
        compiler_params=pltpu.CompilerParams(dimension_semantics=("arbitrary", "arbitrary"),
                                             vmem_limit_bytes=VMEM_LIMIT),
        name="neighborhood_attention",
    )(qkv, qkv, qkv, ck, cv, tables)


def _na_bias_tables(rpb):
    qc = np.arange(GRID_W)[:, None]
    kc = np.arange(GRID_W)[None, :]
    dc = np.clip(kc - qc + NA_COLS - 1, 0, RPB_COLS - 1)
    col_start = np.clip(qc - NA_COLS // 2, 0, GRID_W - NA_COLS)
    col_ok = (kc >= col_start) & (kc < col_start + NA_COLS)
    pick_col = (dc[None] == np.arange(RPB_COLS)[:, None, None]).astype(np.float32)
    rpb_pad = jnp.pad(rpb.astype(f32), ((0, 0), (0, 0), (NA_KROWS, NA_KROWS + 1), (0, 0)))
    n_a = rpb_pad.shape[2] - 1
    rows2 = jnp.stack([rpb_pad[:, :, :-1], rpb_pad[:, :, 1:]], axis=3)
    pick2 = np.zeros((2, RPB_COLS, GRID_W, 2 * GRID_W), np.float32)
    for j in range(2):
        pick2[j, :, :, j * GRID_W:(j + 1) * GRID_W] = pick_col
    tiles = jnp.einsum('lhajb,jbqc->lhaqc', rows2, pick2, precision=HI)
    a_pad = np.arange(n_a)[:, None] + np.arange(2)[None, :]
    row_ok = (a_pad >= NA_KROWS) & (a_pad < NA_KROWS + RPB_ROWS)
    ok = (row_ok[:, None, :, None] & col_ok[None, :, None, :]).reshape(n_a, GRID_W, 2 * GRID_W)
    return jnp.where(ok[None, None], tiles, NEG_INF)


def _log_sigmoid(x):
    return -(jnp.maximum(-x, 0.0) + jnp.log(1.0 + jnp.exp(-jnp.abs(x))))


def _split3(x):
    hi = x.astype(bf16)
    r1 = x - hi.astype(f32)
    mid = r1.astype(bf16)
    lo = (r1 - mid.astype(f32)).astype(bf16)
    return hi, mid, lo


def _mlstm_kernel(qf_ref, vf_ref, ktf_ref, gf_ref, gtf_ref, qb_ref, vb_ref, ktb_ref, gb_ref, gtb_ref,
                  c0_ref, m0_ref, hf_ref, hb_ref, c_out_ref, m_out_ref, c_scr, m_scr):
    L = ML_CHUNK
    seq, c, n_chunks, _ = _ml_schedule(pl.program_id(0))

    @pl.when(c == 0)
    def _():
        is_ctx = seq < BATCH
        c_scr[...] = jnp.where(is_ctx, 0.0, c0_ref[0])
        m_scr[...] = jnp.where(is_ctx, 0.0, m0_ref[0])

    ri = lax.broadcasted_iota(jnp.int32, (L, L), 0)
    ci = lax.broadcasted_iota(jnp.int32, (L, L), 1)
    lane = lax.broadcasted_iota(jnp.int32, (L, ML_PAD), 1)
    is_ncol = lane == ML_DIM
    lower = ri >= ci
    upper = ri <= ci
    lower_b = jnp.where(lower, 1.0, 0.0).astype(bf16)
    upper_b = jnp.where(upper, 1.0, 0.0).astype(bf16)
    dirs = ((qf_ref, ktf_ref, vf_ref, gf_ref, gtf_ref, hf_ref), (qb_ref, ktb_ref, vb_ref, gb_ref, gtb_ref, hb_ref))
    for d, (q_ref, kt_ref, v_ref, g_ref, gt_ref, h_ref) in enumerate(dirs):
        g = g_ref[...][:, 0:N_GATE_COLS]
        gt = gt_ref[...]
        lf_c = _log_sigmoid(g)
        lf_r = _log_sigmoid(gt)
        b_cols = sum(jnp.dot(lower_b, part, preferred_element_type=f32) for part in _split3(lf_c))
        b_rows = sum(jnp.dot(part, upper_b, preferred_element_type=f32) for part in _split3(lf_r))
        tot_c = jnp.sum(lf_c, axis=0, keepdims=True)
        tot_r = jnp.sum(lf_r, axis=1, keepdims=True)
        visible = lower
        if d == 1:
            b_cols = tot_c - b_cols + lf_c
            b_rows = tot_r - b_rows + lf_r
            visible = upper
        for hd in range(ML_HEADS):
            st = d * ML_HEADS + hd
            ci_ = 2 * ML_HEADS * d + hd
            cf_ = ci_ + ML_HEADS
            sl = slice(hd * ML_PAD, (hd + 1) * ML_PAD)
            bc = b_cols[:, cf_:cf_ + 1]
            br = b_rows[cf_:cf_ + 1, :]
            li_r = gt[ci_:ci_ + 1, :]
            m_prev = m_scr[st:st + 1, 0:1]
            dmat = jnp.where(visible, bc - br + li_r, NEG_INF)
            inter = bc + m_prev
            m_t = jnp.maximum(inter, dmat.max(axis=-1, keepdims=True))
            w_intra = jnp.exp(dmat - m_t)
            w_inter = jnp.exp(inter - m_t)
            qh = q_ref[0, :, sl]
            kht = kt_ref[sl, :]
            v_aug = jnp.where(is_ncol, jnp.ones((), bf16), v_ref[0, :, sl])
            s = (jnp.dot(qh, kht, preferred_element_type=f32) * w_intra).astype(bf16)
            c_aug = c_scr[st]
            na = (w_inter * jnp.dot(qh, c_aug.astype(bf16), preferred_element_type=f32)
                  + jnp.dot(s, v_aug, preferred_element_type=f32))
            den = na[:, ML_DIM:ML_DIM + 1]
            h_ref[0, :, sl] = jnp.where(lane < ML_DIM, na / jnp.maximum(jnp.abs(den), jnp.exp(-m_t)), 0.0)
            b_end = tot_r[cf_:cf_ + 1, :]
            g_row = b_end - br + li_r
            m_new = jnp.maximum(b_end + m_prev, g_row.max(axis=1, keepdims=True))
            decay = jnp.exp(b_end + m_prev - m_new)
            kwt = (kht.astype(f32) * jnp.exp(g_row - m_new)).astype(bf16)
            c_scr[st] = decay * c_aug + jnp.dot(kwt, v_aug, preferred_element_type=f32)
            m_scr[st:st + 1, :] = jnp.broadcast_to(m_new, (1, LANE))

    @pl.when(c == n_chunks - 1)
    def _():
        c_out_ref[0] = c_scr[...]
        m_out_ref[0] = m_scr[...]


def _ml_schedule(s):
    nc_ctx, nc_lat = SEQ // ML_CHUNK, DEC_SEQ // ML_CHUNK
    n_ctx_steps = BATCH * nc_ctx
    is_ctx = s < n_ctx_steps
    t = s - n_ctx_steps
    seq = jnp.where(is_ctx, s // nc_ctx, BATCH + t // nc_lat)
    c = jnp.where(is_ctx, s % nc_ctx, t % nc_lat)
    nc = jnp.where(is_ctx, nc_ctx, nc_lat)
    base = jnp.where(is_ctx, (s // nc_ctx) * nc_ctx, n_ctx_steps + (t // nc_lat) * nc_lat)
    return seq, c, nc, base


def _mlstm(qvo, kt, gates, gates_t, c0, m0):
    L = ML_CHUNK
    n_seq = BATCH + DEC_BATCH

    def fwd(s):
        _, c, _, base = _ml_schedule(s)
        return base + c

    def bwd(s):
        _, c, nc, base = _ml_schedule(s)
        return base + nc - 1 - c

    seq_of = lambda s: _ml_schedule(s)[0]
    lat_of = lambda s: jnp.maximum(seq_of(s) - BATCH, 0)

    def specs(pos):
        return [pl.BlockSpec((1, L, ML_PW), lambda s, j=j: (pos(s), 0, j)) for j in range(2)] + [
            pl.BlockSpec((ML_PW, L), lambda s: (0, pos(s))),
            pl.BlockSpec((L, LANE), lambda s: (pos(s), 0)),
            pl.BlockSpec((N_GATE_COLS, L), lambda s: (0, pos(s)))]

    q3 = qvo.reshape(N_TOK // L, L, W_B)
    n_str = 2 * ML_HEADS
    return pl.pallas_call(
        _mlstm_kernel,
        grid=(N_TOK // L,),
        in_specs=specs(fwd) + specs(bwd) + [
            pl.BlockSpec((1, n_str, ML_PAD, CAUG), lambda s: (lat_of(s), 0, 0, 0)),
            pl.BlockSpec((1, n_str, LANE), lambda s: (lat_of(s), 0, 0))],
        out_specs=[pl.BlockSpec((1, L, ML_PW), lambda s: (fwd(s), 0, 0)),
                   pl.BlockSpec((1, L, ML_PW), lambda s: (bwd(s), 0, 0)),
                   pl.BlockSpec((1, n_str, ML_PAD, CAUG), lambda s: (seq_of(s), 0, 0, 0)),
                   pl.BlockSpec((1, n_str, LANE), lambda s: (seq_of(s), 0, 0))],
        out_shape=[jax.ShapeDtypeStruct((N_TOK // L, L, ML_PW), f32),
                   jax.ShapeDtypeStruct((N_TOK // L, L, ML_PW), f32),
                   jax.ShapeDtypeStruct((n_seq, n_str, ML_PAD, CAUG), f32),
                   jax.ShapeDtypeStruct((n_seq, n_str, LANE), f32)],
        scratch_shapes=[pltpu.VMEM((n_str, ML_PAD, CAUG), f32), pltpu.VMEM((n_str, LANE), f32)],
        compiler_params=pltpu.CompilerParams(dimension_semantics=("arbitrary",), vmem_limit_bytes=VMEM_LIMIT),
        name="mlstm",
    )(q3, q3, kt, gates, gates_t, q3, q3, kt, gates, gates_t, c0, m0)


def _pack_ml_state(C, n, m):
    B = C.shape[0]
    c_aug = jnp.zeros((B, 2, ML_HEADS, ML_PAD, CAUG), f32)
    c_aug = c_aug.at[:, :, :, :ML_DIM, :ML_DIM].set(C.astype(f32))
    c_aug = c_aug.at[:, :, :, :ML_DIM, ML_DIM].set(n.astype(f32))
    m_b = jnp.broadcast_to(m.astype(f32)[..., None], (B, 2, ML_HEADS, LANE))
    return c_aug.reshape(B, 2 * ML_HEADS, ML_PAD, CAUG), m_b.reshape(B, 2 * ML_HEADS, LANE)


def _unpack_ml_state(c_aug, m_b):
    B = c_aug.shape[0]
    c_aug = c_aug.reshape(B, 2, ML_HEADS, ML_PAD, CAUG)
    return (c_aug[:, :, :, :ML_DIM, :ML_DIM], c_aug[:, :, :, :ML_DIM, ML_DIM],
            m_b.reshape(B, 2, ML_HEADS, LANE)[..., 0])


def _pool_rows(u_prev, u_cur, u_next, w_bd, scale, t0, seq_len):
    tm = u_cur.shape[0]
    u_win = jnp.concatenate([u_prev, u_cur, u_next], axis=0)
    u_hi = u_win.astype(bf16)
    u_lo = (u_win - u_hi.astype(f32)).astype(bf16)
    lane = lax.broadcasted_iota(jnp.int32, (1, LANE), 1)
    blocks = []
    for r0 in range(0, tm, POOL_BLOCK):
        win = slice(r0, r0 + POOL_BLOCK + 2 * POOL_HALO)
        t_abs = t0 + r0 + lax.broadcasted_iota(jnp.int32, (POOL_BLOCK, 1), 0)
        s_abs = t0 + r0 - POOL_HALO + lax.broadcasted_iota(jnp.int32, (1, POOL_BLOCK + 2 * POOL_HALO), 1)
        t_loc = t_abs & (seq_len - 1)
        seq_start = t_abs - t_loc
        means = []
        for w in POOL_WINDOWS:
            lo = jnp.maximum(t_loc - w // 2, 0)
            hi = jnp.minimum(t_loc - w // 2 + w, seq_len)
            in_win = (s_abs >= seq_start + lo) & (s_abs < seq_start + hi)
            means.append((jnp.where(in_win, 1.0, 0.0).astype(bf16), 1.0 / (hi - lo).astype(f32)))
        pooled = []
        for p in range(POOL_GROUPS // 2):
            sl = slice(p * LANE, (p + 1) * LANE)
            halves = []
            for a, inv_cnt in means[2 * p:2 * p + 2]:
                tot = (jnp.dot(a, u_hi[win, sl], preferred_element_type=f32)
                       + jnp.dot(a, u_lo[win, sl], preferred_element_type=f32))
                halves.append(tot * inv_cnt)
            pooled.append(jnp.where(lane < POOL_DIM, halves[0], halves[1]) - u_cur[r0:r0 + POOL_BLOCK, sl])
        blocks.append(jnp.concatenate(pooled, axis=1))
    pooled = jnp.concatenate(blocks, axis=0).astype(bf16)
    return jnp.dot(pooled, w_bd, preferred_element_type=f32) * scale


def _top2_sum(a, b, c, d):
    hi1, lo1 = jnp.maximum(a, b), jnp.minimum(a, b)
    hi2, lo2 = jnp.maximum(c, d), jnp.minimum(c, d)
    return jnp.maximum(hi1, hi2) + jnp.maximum(jnp.minimum(hi1, hi2), jnp.maximum(lo1, lo2))


def _first_match(vals, target):
    idx = jnp.full_like(target, float(len(vals) - 1))
    for i in range(len(vals) - 2, -1, -1):
        idx = jnp.where(vals[i] == target, float(i), idx)
    return idx


def _pick(vals, idx):
    out = vals[-1]
    for i in range(len(vals) - 2, -1, -1):
        out = jnp.where(idx == float(i), vals[i], out)
    return out


def _route(logits_t, bias_t):
    scores = jax.nn.sigmoid(logits_t)
    sel = scores + bias_t
    row = lambda a, i: a[i:i + 1, :]
    grp = [_top2_sum(*[row(sel, EXPERTS_PER_GROUP * g + i) for i in range(EXPERTS_PER_GROUP)])
           for g in range(N_EXPERT_GROUPS)]
    best = functools.reduce(jnp.maximum, grp)
    gidx = _first_match(grp, best)
    sel_g = [_pick([row(sel, EXPERTS_PER_GROUP * g + i) for g in range(N_EXPERT_GROUPS)], gidx)
             for i in range(EXPERTS_PER_GROUP)]
    sco_g = [_pick([row(scores, EXPERTS_PER_GROUP * g + i) for g in range(N_EXPERT_GROUPS)], gidx)
             for i in range(EXPERTS_PER_GROUP)]
    i0 = _first_match(sel_g, functools.reduce(jnp.maximum, sel_g))
    rest = [jnp.where(i0 == float(i), -jnp.inf, sel_g[i]) for i in range(EXPERTS_PER_GROUP)]
    i1 = _first_match(rest, functools.reduce(jnp.maximum, rest))
    s0, s1 = _pick(sco_g, i0), _pick(sco_g, i1)
    tot = s0 + s1
    rid = lax.broadcasted_iota(jnp.int32, (LANE, logits_t.shape[1]), 0)
    rows = (EXPERTS_PER_GROUP * gidx + i0, EXPERTS_PER_GROUP * gidx + i1, s0 / tot, s1 / tot)
    out = jnp.zeros(rid.shape, f32)
    for i, r in enumerate(rows):
        out = jnp.where(rid == i, r, out)
    return out


def _out_kernel(xc_ref, xl_ref, mod_ref, oac_ref, oal_ref, hf_ref, hb_ref, ob_ref, up_ref, uc_ref, un_ref, wp_ref, psc_ref,
                mln_ref, wo_ref, n2_ref, wr_ref, br_ref, x1_ref, h2_ref, rt_ref, rc_ref):
    tm = xc_ref.shape[0]
    i = pl.program_id(0)
    is_ctx = i < N_CTX // tm
    mod = mod_ref[0]
    out_a = jnp.where(is_ctx, oac_ref[...], oal_ref[...])
    out_c = _pool_rows(up_ref[...], uc_ref[...], un_ref[...], wp_ref[...], psc_ref[...], i * tm,
                       jnp.where(is_ctx, SEQ, DEC_SEQ)).astype(bf16)
    hsum = hf_ref[...] + hb_ref[...]
    outs_b = []
    for hd in range(ML_HEADS):
        sl = slice(hd * ML_PAD, (hd + 1) * ML_PAD)
        hh = hsum[:, sl]
        ms = jnp.sum(hh * hh, axis=-1, keepdims=True) * (1.0 / ML_DIM)
        hn = hh * lax.rsqrt(ms + EPS) * mln_ref[:, sl]
        outs_b.append((jax.nn.sigmoid(ob_ref[:, sl].astype(f32)) * hn).astype(bf16))
    out_b = jnp.concatenate(outs_b, axis=1)
    mixed = (jnp.dot(out_a, wo_ref[0:NA_WIDTH, :], preferred_element_type=f32)
             + jnp.dot(out_b, wo_ref[NA_WIDTH:NA_WIDTH + ML_PW, :], preferred_element_type=f32)
             + jnp.dot(out_c, wo_ref[NA_WIDTH + ML_PW:, :], preferred_element_type=f32))
    x1 = jnp.where(is_ctx, xc_ref[...], xl_ref[...]) + mod[2:3] * mixed
    x1_ref[...] = x1
    h2 = x1 * lax.rsqrt(jnp.mean(x1 * x1, axis=-1, keepdims=True) + EPS) * n2_ref[...]
    h2 = h2 * (1.0 + mod[4:5]) + mod[3:4]
    h2_ref[...] = h2.astype(bf16)
    h_hi = h2.astype(bf16)
    h_lo = (h2 - h_hi.astype(f32)).astype(bf16)
    w_hi = wr_ref[...].astype(bf16)
    w_lo = (wr_ref[...] - w_hi.astype(f32)).astype(bf16)
    route_t = _route(_nt(w_hi, h_hi) + (_nt(w_hi, h_lo) + _nt(w_lo, h_hi)), br_ref[...])
    rt_ref[...] = route_t[0:8]
    rc_ref[...] = route_t.T


def _out_proj(x_ctx, x_lat, x_lat_block0, mods, layer, oa_ctx, oa_lat, hf, hb, qvo, pin, w_bd, psc, mln, wo, n2,
              wr_t, br_t):
    tm = TOK_TILE
    const = lambda i: (0, 0)
    lyr = lambda shape: pl.BlockSpec((None,) + shape, lambda i: (layer, 0, 0))
    row = lambda i: (i, 0)
    n_ctx_tiles = N_CTX // tm
    halo_blocks = tm // POOL_HALO
    return pl.pallas_call(
        _out_kernel,
        grid=(N_TOK // tm,),
        in_specs=[pl.BlockSpec((tm, D_MODEL), lambda i: (jnp.minimum(i, n_ctx_tiles - 1), 0)),
                  pl.BlockSpec((tm, D_MODEL), lambda i: (jnp.maximum(i - n_ctx_tiles, 0) + x_lat_block0, 0)),
                  pl.BlockSpec((1, 6, D_MODEL), lambda i: (_mod_row(i, tm), 0, 0)),
                  pl.BlockSpec((tm, NA_WIDTH), lambda i: (jnp.minimum(i, n_ctx_tiles - 1), 0)),
                  pl.BlockSpec((tm, NA_WIDTH), lambda i: (jnp.maximum(i - n_ctx_tiles, 0), 0)),
                  pl.BlockSpec((tm, ML_PW), row),
                  pl.BlockSpec((tm, ML_PW), row),
                  pl.BlockSpec((tm, ML_PW), lambda i: (i, 2)),
                  pl.BlockSpec((POOL_HALO, POOL_WIDTH), lambda i: (jnp.maximum(i * halo_blocks - 1, 0), 0)),
                  pl.BlockSpec((tm, POOL_WIDTH), row),
                  pl.BlockSpec((POOL_HALO, POOL_WIDTH),
                               lambda i: (jnp.minimum((i + 1) * halo_blocks, N_TOK // POOL_HALO - 1), 0)),
                  lyr((POOL_WIDTH, POOL_WIDTH)), lyr((1, POOL_WIDTH)), lyr((1, ML_PW)),
                  lyr((NA_WIDTH + ML_PW + POOL_WIDTH, D_MODEL)), lyr((1, D_MODEL)),
                  pl.BlockSpec((N_EXPERTS, D_MODEL), const),
                  pl.BlockSpec((N_EXPERTS, 1), const)],
        out_specs=[pl.BlockSpec((tm, D_MODEL), row),
                   pl.BlockSpec((tm, D_MODEL), row),
                   pl.BlockSpec((8, tm), lambda i: (0, i)),
                   pl.BlockSpec((tm, LANE), row)],
        out_shape=[jax.ShapeDtypeStruct((N_TOK, D_MODEL), f32),
                   jax.ShapeDtypeStruct((N_TOK, D_MODEL), bf16),
                   jax.ShapeDtypeStruct((8, N_TOK), f32),
                   jax.ShapeDtypeStruct((N_TOK, LANE), f32)],
        compiler_params=pltpu.CompilerParams(dimension_semantics=("arbitrary",), vmem_limit_bytes=VMEM_LIMIT),
        name="out_proj_router",
    )(x_ctx, x_lat, mods, oa_ctx, oa_lat, hf, hb, qvo, pin, pin, pin, w_bd, psc, mln, wo, n2, wr_t, br_t)


def _ceil_to(x, m):
    return jnp.floor((x + (m - 1)) * (1.0 / m)) * m


def _prefix_over_experts(v):
    er = lax.broadcasted_iota(jnp.int32, (N_EXPERTS, N_EXPERTS), 0)
    ec = lax.broadcasted_iota(jnp.int32, (N_EXPERTS, N_EXPERTS), 1)
    return jnp.dot(jnp.where(ec < er, 1.0, 0.0), v, preferred_element_type=f32, precision=HI)


def _experts_to_lanes(v):
    sub = lax.broadcasted_iota(jnp.int32, (N_EXPERTS, LANE), 0)
    lane = lax.broadcasted_iota(jnp.int32, (N_EXPERTS, LANE), 1)
    return jnp.sum(jnp.where(sub == lane, v, 0.0), axis=0, keepdims=True)


def _expert_hits(rt):
    rid = lax.broadcasted_iota(jnp.int32, (N_EXPERTS, rt.shape[1]), 0).astype(f32)
    oh0 = rid == rt[0:1, :]
    oh1 = rid == rt[1:2, :]
    both = jnp.where(oh0 | oh1, 1.0, 0.0)
    runs = jnp.broadcast_to(_ceil_to(jnp.sum(both, axis=1, keepdims=True), MOE_CHUNK), (N_EXPERTS, LANE))
    return oh0, oh1, both, runs


def _rank_kernel(rt_all_ref, rt_ref, pos_ref, te_ref, tab_ref, carry_ref):
    tm = rt_ref.shape[1]
    step = pl.program_id(0)

    @pl.when(step == 0)
    def _():
        totals = jnp.zeros((N_EXPERTS, LANE), f32)
        for i in range(rt_all_ref.shape[1] // tm):
            totals = totals + _expert_hits(rt_all_ref[:, i * tm:(i + 1) * tm])[3]
        padded = _ceil_to(totals, MOE_TILE)
        off = _prefix_over_experts(padded)
        carry_ref[...] = off
        total = jnp.sum(padded, axis=0, keepdims=True)
        n_used = total * (1.0 / MOE_TILE)
        tile = lax.broadcasted_iota(jnp.int32, (1, LANE), 1).astype(f32)
        row0 = jnp.minimum(tile, n_used - 1.0) * MOE_TILE
        expert = jnp.sum(jnp.where(off <= row0, 1.0, 0.0), axis=0, keepdims=True) - 1.0
        sub = lax.broadcasted_iota(jnp.int32, (8, LANE), 0)
        te_ref[...] = jnp.where(sub == 0, expert, jnp.where(sub == 1, n_used, 0.0)).astype(jnp.int32)

    @pl.when(step > 0)
    def _():
        oh0, oh1, both, runs = _expert_hits(rt_ref[...])
        sr = lax.broadcasted_iota(jnp.int32, (tm, tm), 0)
        sc = lax.broadcasted_iota(jnp.int32, (tm, tm), 1)
        earlier = jnp.dot(both.astype(bf16), jnp.where(sr < sc, 1.0, 0.0).astype(bf16),
                          preferred_element_type=f32)
        g_off = carry_ref[...]
        l_off = _prefix_over_experts(runs)
        g_row = g_off[:, 0:1] + earlier
        l_row = l_off[:, 0:1] + earlier
        pick = lambda oh, v: jnp.sum(jnp.where(oh, v, 0.0), axis=0, keepdims=True)
        rows = (pick(oh0, g_row), pick(oh1, g_row), pick(oh0, l_row), pick(oh1, l_row))
        sub = lax.broadcasted_iota(jnp.int32, (8, tm), 0)
        out = jnp.zeros((8, tm), f32)
        for k, r in enumerate(rows):
            out = jnp.where(sub == k, r, out)
        pos_ref[...] = out.astype(jnp.int32)
        sub = lax.broadcasted_iota(jnp.int32, (8, LANE), 0)
        tab = jnp.zeros((8, LANE), f32)
        for k, v in enumerate((runs * (1.0 / MOE_CHUNK), l_off, g_off)):
            tab = jnp.where(sub == k, _experts_to_lanes(v), tab)
        tab_ref[0] = tab.astype(jnp.int32)
        carry_ref[...] = g_off + runs


def _rank(route_t):
    tm = TOK_TILE
    n_tiles = N_TOK // tm
    tile_of = lambda s: jnp.maximum(s - 1, 0)
    return pl.pallas_call(
        _rank_kernel,
        grid=(1 + n_tiles,),
        in_specs=[pl.BlockSpec((8, N_TOK), lambda s: (0, 0)),
                  pl.BlockSpec((8, tm), lambda s: (0, tile_of(s)))],
        out_specs=[pl.BlockSpec((8, tm), lambda s: (0, tile_of(s))),
                   pl.BlockSpec((8, LANE), lambda s: (0, 0)),
                   pl.BlockSpec((1, 8, LANE), lambda s: (tile_of(s), 0, 0))],
        out_shape=[jax.ShapeDtypeStruct((8, N_TOK), jnp.int32),
                   jax.ShapeDtypeStruct((8, LANE), jnp.int32),
                   jax.ShapeDtypeStruct((n_tiles, 8, LANE), jnp.int32)],
        scratch_shapes=[pltpu.VMEM((N_EXPERTS, LANE), f32)],
        compiler_params=pltpu.CompilerParams(dimension_semantics=("arbitrary",)),
        name="moe_rank",
    )(route_t, route_t)


def _dispatch_kernel(tab_ref, h_ref, rows_ref, xs_in_ref, xs_ref, loc, sem):
    del xs_in_ref
    tm = h_ref.shape[0]
    i = pl.program_id(0)
    rid = lax.broadcasted_iota(jnp.int32, (MOE_LOCAL_ROWS, tm), 0)
    sel = (rid == rows_ref[2:3, :]) | (rid == rows_ref[3:4, :])
    loc[...] = jnp.dot(jnp.where(sel, 1.0, 0.0).astype(bf16), h_ref[...], preferred_element_type=f32).astype(bf16)

    def chunk_copy(src_row, dst_row):
        return pltpu.make_async_copy(loc.at[pl.ds(pl.multiple_of(src_row, MOE_CHUNK), MOE_CHUNK)],
                                     xs_ref.at[pl.ds(pl.multiple_of(dst_row, MOE_CHUNK), MOE_CHUNK)], sem)

    n_total = 0
    for e in range(N_EXPERTS):
        n_chunks, l_off, g_off = (tab_ref[(3 * i + k) * N_EXPERTS + e] for k in range(3))

        def issue(c, carry, l_off=l_off, g_off=g_off):
            chunk_copy(l_off + c * MOE_CHUNK, g_off + c * MOE_CHUNK).start()
            return carry

        lax.fori_loop(0, n_chunks, issue, 0)
        n_total = n_total + n_chunks

    def wait_one(c, carry):
        chunk_copy(0, 0).wait()
        return carry

    lax.fori_loop(0, n_total, wait_one, 0)


def _dispatch(run_table, h2, rows, xs_init):
    tm = TOK_TILE
    return pl.pallas_call(
        _dispatch_kernel,
        grid_spec=pltpu.PrefetchScalarGridSpec(
            num_scalar_prefetch=1,
            grid=(N_TOK // tm,),
            in_specs=[pl.BlockSpec((tm, D_MODEL), lambda i, tab: (i, 0)),
                      pl.BlockSpec((8, tm), lambda i, tab: (0, i)),
                      pl.BlockSpec(memory_space=pl.ANY)],
            out_specs=pl.BlockSpec(memory_space=pl.ANY),
            scratch_shapes=[pltpu.VMEM((MOE_LOCAL_ROWS, D_MODEL), bf16), pltpu.SemaphoreType.DMA(())]),
        out_shape=jax.ShapeDtypeStruct(xs_init.shape, xs_init.dtype),
        input_output_aliases={3: 0},
        compiler_params=pltpu.CompilerParams(dimension_semantics=("arbitrary",), vmem_limit_bytes=VMEM_LIMIT),
        name="moe_dispatch",
    )(run_table, h2, rows, xs_init)


def _expert_kernel(te_ref, xs_ref, wg_hbm, wu_hbm, wd_hbm, ys_ref, wg_f32, wu_f32, wd_f32, wg_bf, wu_bf, wd_bf,
                   slot_ref, sem, *, layer):
    j = pl.program_id(0)
    n_used = te_ref[1, 0]
    used = j < n_used
    expert = te_ref[0, j]
    new_expert = jnp.logical_or(j == 0, expert != te_ref[0, jnp.maximum(j - 1, 0)])

    def weight_copies(e, slot):
        return [pltpu.make_async_copy(hbm.at[layer, e], buf.at[slot], sem.at[slot])
                for hbm, buf in ((wg_hbm, wg_f32), (wu_hbm, wu_f32), (wd_hbm, wd_f32))]

    @pl.when(j == 0)
    def _():
        slot_ref[0] = 1
        for cp in weight_copies(expert, 0):
            cp.start()

    @pl.when(jnp.logical_not(used))
    def _():
        ys_ref[...] = jnp.zeros_like(ys_ref)

    @pl.when(used & new_expert)
    def _():
        slot = 1 - slot_ref[0]
        slot_ref[0] = slot
        for cp in weight_copies(expert, slot):
            cp.wait()
        wg_bf[...] = wg_f32[slot].astype(bf16)
        wu_bf[...] = wu_f32[slot].astype(bf16)
        wd_bf[...] = wd_f32[slot].astype(bf16)
        nxt = lax.while_loop(lambda t: (t < n_used) & (te_ref[0, jnp.minimum(t, LANE - 1)] == expert),
                             lambda t: t + 1, j + 1)

        @pl.when(nxt < n_used)
        def _():
            for cp in weight_copies(te_ref[0, nxt], 1 - slot):
                cp.start()

    @pl.when(used)
    def _():
        x = xs_ref[...]
        hg = jnp.dot(x, wg_bf[...], preferred_element_type=f32)
        hu = jnp.dot(x, wu_bf[...], preferred_element_type=f32)
        hid = (hg * jax.nn.sigmoid(hg) * hu).astype(bf16)
        ys_ref[...] = jnp.dot(hid, wd_bf[...], preferred_element_type=f32)


def _experts(te, xs, w_gate, w_up, w_down, layer):
    tm = MOE_TILE
    row = lambda j, te: (jnp.minimum(j, te[1, 0] - 1), 0)
    hbm = pl.BlockSpec(memory_space=pl.ANY)
    return pl.pallas_call(
        functools.partial(_expert_kernel, layer=layer),
        grid_spec=pltpu.PrefetchScalarGridSpec(
            num_scalar_prefetch=1,
            grid=(MOE_ROWS // tm,),
            in_specs=[pl.BlockSpec((tm, D_MODEL), row), hbm, hbm, hbm],
            out_specs=pl.BlockSpec((tm, D_MODEL), lambda j, te: (j, 0)),
            scratch_shapes=[pltpu.VMEM((2, D_MODEL, D_EXPERT), f32), pltpu.VMEM((2, D_MODEL, D_EXPERT), f32),
                            pltpu.VMEM((2, D_EXPERT, D_MODEL), f32),
                            pltpu.VMEM((D_MODEL, D_EXPERT), bf16), pltpu.VMEM((D_MODEL, D_EXPERT), bf16),
                            pltpu.VMEM((D_EXPERT, D_MODEL), bf16),
                            pltpu.SMEM((1,), jnp.int32), pltpu.SemaphoreType.DMA((2,))]),
        out_shape=jax.ShapeDtypeStruct((MOE_ROWS, D_MODEL), f32),
        compiler_params=pltpu.CompilerParams(dimension_semantics=("arbitrary",), vmem_limit_bytes=VMEM_LIMIT),
        name="moe_experts",
    )(te, xs, w_gate, w_up, w_down)


def _gather_expert_rows(pos0_ref, pos1_ref, ys_ref, buf, sem):
    rows = buf.shape[2]
    i = pl.program_id(0)
    slot = i % 2

    def issue(tile, sl):
        base = tile * rows

        def body(t, carry):
            for s, pos_ref in enumerate((pos0_ref, pos1_ref)):
                pltpu.make_async_copy(ys_ref.at[pl.ds(pos_ref[base + t], 1)], buf.at[sl, s, pl.ds(t, 1)],
                                      sem.at[sl]).start()
            return carry

        lax.fori_loop(0, rows, body, 0, unroll=8)

    @pl.when(i == 0)
    def _():
        issue(0, 0)

    @pl.when(i + 1 < pl.num_programs(0))
    def _():
        issue(i + 1, 1 - slot)

    for s in range(2):
        pltpu.make_async_copy(ys_ref.at[pl.ds(0, rows)], buf.at[slot, s], sem.at[slot]).wait()
    return buf[slot, 0], buf[slot, 1]


def _moe_residual(pos0_ref, pos1_ref, ys_ref, x1_ref, rc_ref, mod_ref, buf, sem):
    y0, y1 = _gather_expert_rows(pos0_ref, pos1_ref, ys_ref, buf, sem)
    rc = rc_ref[...]
    return x1_ref[...] + mod_ref[0][5:6] * (rc[:, 2:3] * y0 + rc[:, 3:4] * y1)


def _final_kernel(pos0_ref, pos1_ref, ys_ref, x1_ref, rc_ref, mod_ref, fn_ref, yc_ref, yl_ref, buf, sem):
    x2 = _moe_residual(pos0_ref, pos1_ref, ys_ref, x1_ref, rc_ref, mod_ref, buf, sem)
    y = x2 * lax.rsqrt(jnp.mean(x2 * x2, axis=-1, keepdims=True) + EPS) * fn_ref[...]
    is_ctx = pl.program_id(0) < N_CTX // x1_ref.shape[0]

    @pl.when(is_ctx)
    def _():
        yc_ref[...] = y

    @pl.when(jnp.logical_not(is_ctx))
    def _():
        yl_ref[...] = y


def _final_combine(pos0, pos1, ys, x1, rc, mods, fn):
    tc = COMBINE_TILE
    row = lambda i, p0, p1: (i, 0)
    n_ctx_tiles = N_CTX // tc
    return pl.pallas_call(
        _final_kernel,
        grid_spec=pltpu.PrefetchScalarGridSpec(
            num_scalar_prefetch=2,
            grid=(N_TOK // tc,),
            in_specs=[pl.BlockSpec(memory_space=pl.ANY),
                      pl.BlockSpec((tc, D_MODEL), row),
                      pl.BlockSpec((tc, LANE), row),
                      pl.BlockSpec((1, 6, D_MODEL), lambda i, p0, p1: (_mod_row(i, tc), 0, 0)),
                      pl.BlockSpec((1, D_MODEL), lambda i, p0, p1: (0, 0))],
            out_specs=[pl.BlockSpec((tc, D_MODEL), lambda i, p0, p1: (jnp.minimum(i, n_ctx_tiles - 1), 0)),
                       pl.BlockSpec((tc, D_MODEL), lambda i, p0, p1: (jnp.maximum(i - n_ctx_tiles, 0), 0))],
            scratch_shapes=[pltpu.VMEM((2, 2, tc, D_MODEL), f32), pltpu.SemaphoreType.DMA((2,))]),
        out_shape=[jax.ShapeDtypeStruct((N_CTX, D_MODEL), f32), jax.ShapeDtypeStruct((N_LAT, D_MODEL), f32)],
        compiler_params=pltpu.CompilerParams(dimension_semantics=("arbitrary",), vmem_limit_bytes=VMEM_LIMIT),
        name="moe_combine_final",
    )(pos0, pos1, ys, x1, rc, mods, fn)


def _moe_experts(h2, route_t, w_gate, w_up, w_down, layer, xs_buf):
    rows, te, runs = _rank(route_t)
    xs = _dispatch(runs[:, :3, :N_EXPERTS].reshape(-1), h2, rows, xs_buf)
    return rows[0], rows[1], _experts(te, xs, w_gate, w_up, w_down, layer), xs


def _pad_heads(w):
    lead = w.shape[:-1]
    w = w.reshape(*lead, ML_HEADS, ML_DIM)
    w = jnp.pad(w, [(0, 0)] * len(lead) + [(0, 0), (0, ML_PAD - ML_DIM)])
    return w.reshape(*lead, ML_PW)


def _pack_in_cols(wb):
    o = 0
    qa = wb[..., o:o + NA_WIDTH] * (NA_DIM ** -0.5)
    ka = wb[..., o + NA_WIDTH:o + 2 * NA_WIDTH]
    va = wb[..., o + 2 * NA_WIDTH:o + 3 * NA_WIDTH]
    o += 3 * NA_WIDTH
    qb, kb, vb, ob = [_pad_heads(wb[..., o + j * ML_WIDTH:o + (j + 1) * ML_WIDTH]) for j in range(4)]
    o += 4 * ML_WIDTH
    gates = wb[..., o:o + N_GATE_COLS]
    o += N_GATE_COLS
    pool = wb[..., o:o + POOL_WIDTH]
    main = jnp.concatenate([qa, ka, va, qb, vb, ob, pool], axis=-1)
    gates_p = jnp.pad(gates, [(0, 0)] * (gates.ndim - 1) + [(0, LANE - N_GATE_COLS)])
    return main, gates_p, jnp.concatenate([kb, gates], axis=-1)


def _pack_w_in(w, b):
    w_main, w_gates, w_feat = _pack_in_cols(w)
    b_main, b_gates, b_feat = _pack_in_cols(b.astype(f32))
    return (w_main.astype(bf16), b_main[:, None], w_gates.astype(bf16), b_gates[:, None],
            jnp.swapaxes(w_feat, 1, 2).astype(bf16), b_feat[:, :, None])


def _pack_w_out(w):
    n_l = w.shape[0]
    wb = w[:, NA_WIDTH:NA_WIDTH + ML_WIDTH].reshape(n_l, ML_HEADS, ML_DIM, D_MODEL)
    wb = jnp.pad(wb, ((0, 0), (0, 0), (0, ML_PAD - ML_DIM), (0, 0))).reshape(n_l, ML_PW, D_MODEL)
    return jnp.concatenate([w[:, :NA_WIDTH], wb, w[:, NA_WIDTH + ML_WIDTH:]], axis=1).astype(bf16)


def _block_diag(w):
    n_l, g, c, _ = w.shape
    eye = jnp.eye(g, dtype=w.dtype)
    return (eye[None, :, None, :, None] * w[:, :, :, None, :]).reshape(n_l, g * c, g * c)


def kernel(x_prompt, x_sample, cache_k_attn, cache_v_attn, state_mlstm_C, state_mlstm_n, state_mlstm_m, c, c_ctx,
           w_ada, b_ada, norm1, w_in, b_in, rpb, ml_norm, w_pool, pool_scale, w_out, norm2, w_router, b_router,
           w_gate, w_up, w_down, final_norm):
    dt = x_prompt.dtype
    x_ctx = x_prompt.reshape(N_CTX, D_MODEL).astype(f32)
    x_lat = x_sample.reshape(N_LAT, D_MODEL).astype(f32)
    x_lat_block0 = 0
    cvec = jnp.concatenate([c_ctx[None], c, jnp.zeros((8 - 1 - DEC_BATCH, D_MODEL), c.dtype)], axis=0).astype(f32)
    mods_all = _ada(cvec, w_ada.astype(f32), b_ada.astype(f32))
    mods_all = mods_all[:, :1 + DEC_BATCH].reshape(DEPTH, 1 + DEC_BATCH, 6, D_MODEL)

    wr_t = w_router.astype(f32).T
    br_t = b_router.astype(f32)[:, None]
    fn = final_norm.astype(f32)[None]

    na_bias = _na_bias_tables(rpb)
    xs_buf = jnp.zeros((MOE_ROWS, D_MODEL), bf16)
    in_params = (norm1.astype(f32)[:, None],) + _pack_w_in(w_in, b_in)
    out_params = (_block_diag(w_pool.astype(f32)).astype(bf16), pool_scale.astype(f32)[:, None],
                  _pad_heads(ml_norm.astype(f32))[:, None], _pack_w_out(w_out), norm2.astype(f32)[:, None])

    new_k = jnp.zeros((BATCH + TOK_TILE // SEQ, DEPTH, SEQ, NA_WIDTH), f32)
    new_v = jnp.zeros_like(new_k)
    Cs, ns, ms = [], [], []
    pending = None
    for l in range(DEPTH):
        mods = mods_all[l]
        if pending is None:
            qkva, new_k, new_v, qvo, kt, gates, gates_t, pin = _in_proj(x_ctx, x_lat, mods, l, *in_params,
                                                                        new_k, new_v)
        else:
            x, qkva, new_k, new_v, qvo, kt, gates, gates_t, pin = _moe_in_proj(
                *pending, mods_all[l - 1], mods, l, *in_params, new_k, new_v)
            x_ctx, x_lat, x_lat_block0 = x, x, N_CTX // TOK_TILE

        oa_ctx = _ctx_attention(qkva.reshape(N_TOK // SEQ, SEQ, W_A))
        ck = (cache_k_attn[:, l].reshape(DEC_BATCH, PAST_LEN, NA_WIDTH)).astype(bf16)
        cv = (cache_v_attn[:, l].reshape(DEC_BATCH, PAST_LEN, NA_WIDTH)).astype(bf16)
        oa_lat = _neighborhood_attention(qkva.reshape(N_TOK // DEC_SEQ, DEC_SEQ, W_A), ck, cv, na_bias, l)

        c_l, m_l = _pack_ml_state(state_mlstm_C[:, l], state_mlstm_n[:, l], state_mlstm_m[:, l])
        hf, hb, c_fin, m_fin = _mlstm(qvo, kt, gates, gates_t, c_l, m_l)
        C_l, n_l, m_l2 = _unpack_ml_state(c_fin[:BATCH], m_fin[:BATCH])
        Cs.append(C_l)
        ns.append(n_l)
        ms.append(m_l2)

        x1, h2, route_t, rc = _out_proj(x_ctx, x_lat, x_lat_block0, mods, l,
                                        oa_ctx.reshape(N_CTX, NA_WIDTH), oa_lat.reshape(N_LAT, NA_WIDTH),
                                        hf.reshape(N_TOK, ML_PW), hb.reshape(N_TOK, ML_PW), qvo, pin,
                                        *out_params, wr_t, br_t)
        pos0, pos1, ys, xs_buf = _moe_experts(h2, route_t, w_gate, w_up, w_down, l, xs_buf)
        pending = (pos0, pos1, ys, x1, rc)

    x = _final_combine(*pending, mods_all[DEPTH - 1], fn)
    y_prompt = x[0].reshape(BATCH, SEQ, D_MODEL).astype(dt)
    y_sample = x[1].reshape(DEC_BATCH, DEC_SEQ, D_MODEL).astype(dt)
    new_k, new_v = (a[:BATCH].reshape(BATCH, DEPTH, SEQ, NA_HEADS, NA_DIM).astype(dt) for a in (new_k, new_v))
    return (y_prompt, y_sample, new_k, new_v,
            jnp.stack(Cs, axis=1).astype(dt), jnp.stack(ns, axis=1).astype(dt), jnp.stack(ms, axis=1).astype(dt))
```

```python
import functools

import numpy as np
import jax
import jax.numpy as jnp
from jax import lax
from jax.experimental import pallas as pl
from jax.experimental.pallas import tpu as pltpu

D_MODEL = 1024
BATCH = 16
SEQ = 256
DEPTH = 4
DEC_BATCH = 2
DEC_SEQ = 4096
PAST_LEN = 256
GRID_W = 64
EPS = 1e-6
NEG_INF = -1e30
NA_HEADS = 6
NA_DIM = 64
NA_WIDTH = NA_HEADS * NA_DIM
NA_ROWS = 8
NA_COLS = 16
RPB_ROWS = 2 * NA_ROWS - 1
RPB_COLS = 2 * NA_COLS - 1
ML_HEADS = 4
ML_DIM = 96
ML_WIDTH = ML_HEADS * ML_DIM
POOL_WINDOWS = (2, 4, 8, 16)
POOL_GROUPS = 4
POOL_DIM = 64
POOL_WIDTH = POOL_GROUPS * POOL_DIM
N_GATE_COLS = 4 * ML_HEADS
N_EXPERTS = 16
N_EXPERT_GROUPS = 4
EXPERTS_PER_GROUP = N_EXPERTS // N_EXPERT_GROUPS
D_EXPERT = 512
ADA_DIM = 6 * D_MODEL

N_CTX = BATCH * SEQ
N_LAT = DEC_BATCH * DEC_SEQ
N_TOK = N_CTX + N_LAT
LANE = 128
ML_PAD = LANE
ML_PW = ML_HEADS * ML_PAD
CAUG = ML_PAD
NA_PAIRS = NA_HEADS // 2
TOK_TILE = 512
ML_CHUNK = 256
NA_QROWS = 4
NA_KROWS = NA_QROWS + NA_ROWS - 1
POOL_HALO = max(POOL_WINDOWS) // 2
POOL_BLOCK = 128
MOE_TILE = 512
MOE_CHUNK = 16
MOE_LOCAL_ROWS = -(-(2 * TOK_TILE + N_EXPERTS * (MOE_CHUNK - 1)) // LANE) * LANE
MOE_ROWS = -(-(2 * N_TOK + (N_TOK // TOK_TILE) * N_EXPERTS * (MOE_CHUNK - 1) + N_EXPERTS * (MOE_TILE - 1))
             // MOE_TILE) * MOE_TILE
COMBINE_TILE = 256
VMEM_LIMIT = 56 * 1024 * 1024

W_A = 3 * NA_WIDTH
W_B = 3 * ML_PW
N_TCOLS = ML_PW + N_GATE_COLS
W_MAIN = W_A + W_B + POOL_WIDTH

f32 = jnp.float32
bf16 = jnp.bfloat16
HI = lax.Precision.HIGHEST


def _nt(a, b, **kw):
    return lax.dot_general(a, b, (((1,), (1,)), ((), ())), preferred_element_type=f32, **kw)


def _mod_row(i, tile):
    n_ctx_tiles = N_CTX // tile
    per_batch = DEC_SEQ // tile
    return jnp.where(i < n_ctx_tiles, 0, 1 + (i - n_ctx_tiles) // per_batch)


def _ada_kernel(c_ref, w_ref, b_ref, o_ref):
    s = c_ref[...]
    s = s * jax.nn.sigmoid(s)
    o_ref[0] = jnp.dot(s.astype(bf16), w_ref[0].astype(bf16), preferred_element_type=f32) + b_ref[0]


def _ada(cvec, w_ada, b_ada):
    nj = ADA_DIM // D_MODEL
    return pl.pallas_call(
        _ada_kernel,
        grid=(DEPTH, nj),
        in_specs=[pl.BlockSpec((8, D_MODEL), lambda l, j: (0, 0)),
                  pl.BlockSpec((1, D_MODEL, D_MODEL), lambda l, j: (l, 0, j)),
                  pl.BlockSpec((1, 1, D_MODEL), lambda l, j: (l, 0, j))],
        out_specs=pl.BlockSpec((1, 8, D_MODEL), lambda l, j: (l, 0, j)),
        out_shape=jax.ShapeDtypeStruct((DEPTH, 8, ADA_DIM), f32),
        name="ada_mod",
    )(cvec, w_ada, b_ada.reshape(DEPTH, 1, ADA_DIM))


def _in_tile(step):
    return (step + N_CTX // TOK_TILE) % (N_TOK // TOK_TILE)


def _in_kernel(xc_ref, xl_ref, *refs):
    is_ctx = _in_tile(pl.program_id(0)) < N_CTX // xc_ref.shape[0]
    _in_body(jnp.where(is_ctx, xc_ref[...], xl_ref[...]), *refs)


def _moe_in_kernel(pos0_ref, pos1_ref, ys_ref, x1_ref, rc_ref, mod_prev_ref, *refs):
    in_refs, x_out_ref, out_refs, (buf, sem) = refs[:10], refs[10], refs[11:-2], refs[-2:]
    x = _moe_residual(pos0_ref, pos1_ref, ys_ref, x1_ref, rc_ref, mod_prev_ref, buf, sem, tile_of=_in_tile)
    x_out_ref[...] = x
    _in_body(x, *in_refs, *out_refs)


def _in_body(x, mod_ref, n1_ref, w_ref, b_ref, wg_ref, bg_ref, wt_ref, bt_ref, k_in_ref, v_in_ref,
             a_ref, k_ref, v_ref, b_out_ref, kt_ref, g_ref, gt_ref, pin_ref):
    del k_in_ref, v_in_ref
    mod = mod_ref[0]
    h = x * lax.rsqrt(jnp.mean(x * x, axis=-1, keepdims=True) + EPS) * n1_ref[...]
    h = (h * (1.0 + mod[1:2]) + mod[0:1]).astype(bf16)
    pa = jnp.dot(h, w_ref[:, 0:W_A], preferred_element_type=f32) + b_ref[:, 0:W_A]
    a_ref[...] = pa.astype(bf16)
    k_ref[...] = pa[:, NA_WIDTH:2 * NA_WIDTH].reshape(k_ref.shape)
    v_ref[...] = pa[:, 2 * NA_WIDTH:W_A].reshape(v_ref.shape)
    for j in range(3):
        lo = W_A + j * ML_PW
        pb = jnp.dot(h, w_ref[:, lo:lo + ML_PW], preferred_element_type=f32) + b_ref[:, lo:lo + ML_PW]
        if j == 0:
            pb = pb * (ML_DIM ** -0.5)
        b_out_ref[:, j * ML_PW:(j + 1) * ML_PW] = pb.astype(bf16)
    lo = W_A + W_B
    pin_ref[...] = jnp.dot(h, w_ref[:, lo:lo + POOL_WIDTH], preferred_element_type=f32) + b_ref[:, lo:lo + POOL_WIDTH]
    g_ref[...] = jnp.dot(h, wg_ref[...], preferred_element_type=f32) + bg_ref[...]
    t = _nt(wt_ref[...], h) + bt_ref[...]
    kt_ref[...] = t[0:ML_PW].astype(bf16)
    gt_ref[...] = t[ML_PW:N_TCOLS]


def _in_proj_specs(layer):
    tm = TOK_TILE
    lyr = lambda shape: pl.BlockSpec((None,) + shape, lambda i, *_: (layer, 0, 0))
    rows = lambda width: pl.BlockSpec((tm, width), lambda i, *_: (_in_tile(i), 0))
    cols = lambda height: pl.BlockSpec((height, tm), lambda i, *_: (0, _in_tile(i)))
    param_specs = [pl.BlockSpec((1, 6, D_MODEL), lambda i, *_: (_mod_row(_in_tile(i), tm), 0, 0)),
                   lyr((1, D_MODEL)), lyr((D_MODEL, W_MAIN)), lyr((1, W_MAIN)), lyr((D_MODEL, LANE)), lyr((1, LANE)),
                   lyr((N_TCOLS, D_MODEL)), lyr((N_TCOLS, 1)),
                   pl.BlockSpec(memory_space=pl.ANY), pl.BlockSpec(memory_space=pl.ANY)]
    n_ctx_tiles = N_CTX // tm
    kv_spec = pl.BlockSpec((tm // SEQ, None, SEQ, NA_WIDTH),
                           lambda i, *_: (jnp.where(_in_tile(i) < n_ctx_tiles, _in_tile(i), 0), layer, 0, 0))
    kv_shape = jax.ShapeDtypeStruct((BATCH, DEPTH, SEQ, NA_WIDTH), f32)
    out_specs = [rows(W_A), kv_spec, kv_spec, rows(W_B), cols(ML_PW), rows(LANE), cols(N_GATE_COLS),
                 rows(POOL_WIDTH)]
    out_shape = [jax.ShapeDtypeStruct((N_TOK, W_A), bf16), kv_shape, kv_shape,
                 jax.ShapeDtypeStruct((N_TOK, W_B), bf16),
                 jax.ShapeDtypeStruct((ML_PW, N_TOK), bf16),
                 jax.ShapeDtypeStruct((N_TOK, LANE), f32),
                 jax.ShapeDtypeStruct((N_GATE_COLS, N_TOK), f32),
                 jax.ShapeDtypeStruct((N_TOK, POOL_WIDTH), f32)]
    return rows, param_specs, out_specs, out_shape


def _in_proj(x_ctx, x_lat, mods, layer, *params):
    tm = TOK_TILE
    n_ctx_tiles = N_CTX // tm
    rows, param_specs, out_specs, out_shape = _in_proj_specs(layer)
    return pl.pallas_call(
        _in_kernel,
        grid=(N_TOK // tm,),
        in_specs=[pl.BlockSpec((tm, D_MODEL), lambda i: (jnp.where(_in_tile(i) < n_ctx_tiles, _in_tile(i), 0), 0)),
                  pl.BlockSpec((tm, D_MODEL), lambda i: (jnp.maximum(_in_tile(i) - n_ctx_tiles, 0), 0))] + param_specs,
        out_specs=out_specs,
        out_shape=out_shape,
        input_output_aliases={2 + len(param_specs) - 2: 1, 2 + len(param_specs) - 1: 2},
        compiler_params=pltpu.CompilerParams(dimension_semantics=("arbitrary",), vmem_limit_bytes=VMEM_LIMIT),
        name="in_proj",
    )(x_ctx, x_lat, mods, *params)


def _moe_in_proj(pos0, pos1, ys, x1, rc, mods_prev, mods, layer, *params):
    tm = TOK_TILE
    rows, param_specs, out_specs, out_shape = _in_proj_specs(layer)
    return pl.pallas_call(
        _moe_in_kernel,
        grid_spec=pltpu.PrefetchScalarGridSpec(
            num_scalar_prefetch=2,
            grid=(N_TOK // tm,),
            in_specs=[pl.BlockSpec(memory_space=pl.ANY), rows(D_MODEL), rows(LANE),
                      pl.BlockSpec((1, 6, D_MODEL), lambda i, *_: (_mod_row(_in_tile(i), tm), 0, 0))] + param_specs,
            out_specs=[rows(D_MODEL)] + out_specs,
            scratch_shapes=[pltpu.VMEM((2, 2, tm, D_MODEL), f32), pltpu.SemaphoreType.DMA((2,))]),
        out_shape=[jax.ShapeDtypeStruct((N_TOK, D_MODEL), f32)] + out_shape,
        input_output_aliases={6 + len(param_specs) - 2: 2, 6 + len(param_specs) - 1: 3},
        compiler_params=pltpu.CompilerParams(dimension_semantics=("arbitrary",), vmem_limit_bytes=VMEM_LIMIT),
        name="moe_combine_in_proj",
    )(pos0, pos1, ys, x1, rc, mods_prev, mods, *params)


def _pair_attention(qp, parts):
    lane = lax.broadcasted_iota(jnp.int32, (1, LANE), 1)
    outs = []
    for j in range(2):
        in_half = (lane >= j * NA_DIM) & (lane < (j + 1) * NA_DIM)
        qm = jnp.where(in_half, qp, jnp.zeros_like(qp))
        scores = []
        for k, _, bias in parts:
            s = _nt(qm, k)
            if bias is not None:
                s = s + bias[j]
            scores.append(s)
        m = scores[0].max(axis=-1, keepdims=True)
        for s in scores[1:]:
            m = jnp.maximum(m, s.max(axis=-1, keepdims=True))
        den = None
        acc = None
        for s, (_, v, _) in zip(scores, parts):
            p = jnp.exp(s - m)
            ps = p.sum(axis=-1, keepdims=True)
            den = ps if den is None else den + ps
            o = jnp.dot(p.astype(bf16), v, preferred_element_type=f32)
            acc = o if acc is None else acc + o
        outs.append(acc / den)
    return jnp.where(lane < NA_DIM, outs[0], outs[1])


def _ctx_attn_kernel(q_ref, k_ref, v_ref, o_ref):
    for p in range(NA_PAIRS):
        sl = slice(p * LANE, (p + 1) * LANE)
        o = _pair_attention(q_ref[0, :, sl], [(k_ref[0, :, sl], v_ref[0, :, sl], None)])
        o_ref[0, :, sl] = o.astype(bf16)


def _ctx_attention(qkv):
    blk = lambda c: pl.BlockSpec((1, SEQ, NA_WIDTH), lambda b, c=c: (b, 0, c))
    return pl.pallas_call(
        _ctx_attn_kernel,
        grid=(BATCH,),
        in_specs=[blk(0), blk(1), blk(2)],
        out_specs=pl.BlockSpec((1, SEQ, NA_WIDTH), lambda b: (b, 0, 0)),
        out_shape=jax.ShapeDtypeStruct((BATCH, SEQ, NA_WIDTH), bf16),
        name="ctx_attention",
    )(qkv, qkv, qkv)


def _na_window_start(rb):
    return jnp.clip(rb * NA_QROWS - NA_ROWS // 2, 0, DEC_SEQ // GRID_W - NA_KROWS)


def _na_bias(tab_ref, head, rb):
    rows = DEC_SEQ // GRID_W
    ws = _na_window_start(rb)
    lane = lax.broadcasted_iota(jnp.int32, (1, NA_KROWS * GRID_W), 1)
    per_qrow = []
    for dq in range(NA_QROWS):
        qr = rb * NA_QROWS + dq
        a0 = ws - qr + (NA_ROWS - 1) + NA_KROWS
        tiles = [tab_ref[head, a0 + 2 * j] for j in range((NA_KROWS + 1) // 2)]
        t = jnp.concatenate(tiles, axis=1)[:, :NA_KROWS * GRID_W]
        lo = (jnp.clip(qr - NA_ROWS // 2, 0, rows - NA_ROWS) - ws) * GRID_W
        ok = (lane >= lo) & (lane < lo + NA_ROWS * GRID_W)
        per_qrow.append(jnp.where(ok, t, NEG_INF))
    return jnp.concatenate(per_qrow, axis=0)


def _na_kernel(q_ref, k_ref, v_ref, ck_ref, cv_ref, tab_ref, o_ref):
    rb = pl.program_id(1)
    start = pl.multiple_of(_na_window_start(rb) * GRID_W, GRID_W)
    nk = NA_KROWS * GRID_W
    for p in range(NA_PAIRS):
        sl = slice(p * LANE, (p + 1) * LANE)
        bias = [_na_bias(tab_ref.at[0], 2 * p + j, rb) for j in range(2)]
        parts = [(k_ref[0, pl.ds(start, nk), sl], v_ref[0, pl.ds(start, nk), sl], bias),
                 (ck_ref[0, :, sl], cv_ref[0, :, sl], None)]
        o = _pair_attention(q_ref[0, :, sl], parts)
        o_ref[0, :, sl] = o.astype(bf16)


def _neighborhood_attention(qkv, ck, cv, tables, layer):
    nq = NA_QROWS * GRID_W
    n_rb = DEC_SEQ // nq
    return pl.pallas_call(
        _na_kernel,
        grid=(DEC_BATCH, n_rb),
        in_specs=[pl.BlockSpec((1, nq, NA_WIDTH), lambda b, r: (1 + b, r, 0)),
                  pl.BlockSpec((1, DEC_SEQ, NA_WIDTH), lambda b, r: (1 + b, 0, 1)),
                  pl.BlockSpec((1, DEC_SEQ, NA_WIDTH), lambda b, r: (1 + b, 0, 2)),
                  pl.BlockSpec((1, PAST_LEN, NA_WIDTH), lambda b, r: (b, 0, 0)),
                  pl.BlockSpec((1, PAST_LEN, NA_WIDTH), lambda b, r: (b, 0, 0)),
                  pl.BlockSpec((1,) + tables.shape[1:], lambda b, r: (layer, 0, 0, 0, 0))],
        out_specs=pl.BlockSpec((1, nq, NA_WIDTH), lambda b, r: (b, r, 0)),
        out_shape=jax.ShapeDtypeStruct((DEC_BATCH, DEC_SEQ, NA_WIDTH), bf16),
        compiler_params=pltpu.CompilerParams(dimension_semantics=("arbitrary", "arbitrary"),
                                             vmem_limit_bytes=VMEM_LIMIT),
        name="neighborhood_attention",
    )(qkv, qkv, qkv, ck, cv, tables)


def _na_bias_tables(rpb):
    qc = np.arange(GRID_W)[:, None]
    kc = np.arange(GRID_W)[None, :]
    dc = np.clip(kc - qc + NA_COLS - 1, 0, RPB_COLS - 1)
    col_start = np.clip(qc - NA_COLS // 2, 0, GRID_W - NA_COLS)
    col_ok = (kc >= col_start) & (kc < col_start + NA_COLS)
    pick_col = (dc[None] == np.arange(RPB_COLS)[:, None, None]).astype(np.float32)
    rpb_pad = jnp.pad(rpb.astype(f32), ((0, 0), (0, 0), (NA_KROWS, NA_KROWS + 1), (0, 0)))
    n_a = rpb_pad.shape[2] - 1
    rows2 = jnp.stack([rpb_pad[:, :, :-1], rpb_pad[:, :, 1:]], axis=3)
    pick2 = np.zeros((2, RPB_COLS, GRID_W, 2 * GRID_W), np.float32)
    for j in range(2):
        pick2[j, :, :, j * GRID_W:(j + 1) * GRID_W] = pick_col
    tiles = jnp.einsum('lhajb,jbqc->lhaqc', rows2, pick2, precision=HI)
    a_pad = np.arange(n_a)[:, None] + np.arange(2)[None, :]
    row_ok = (a_pad >= NA_KROWS) & (a_pad < NA_KROWS + RPB_ROWS)
    ok = (row_ok[:, None, :, None] & col_ok[None, :, None, :]).reshape(n_a, GRID_W, 2 * GRID_W)
    return jnp.where(ok[None, None], tiles, NEG_INF)


def _log_sigmoid(x):
    return -(jnp.maximum(-x, 0.0) + jnp.log(1.0 + jnp.exp(-jnp.abs(x))))


def _split3(x):
    hi = x.astype(bf16)
    r1 = x - hi.astype(f32)
    mid = r1.astype(bf16)
    lo = (r1 - mid.astype(f32)).astype(bf16)
    return hi, mid, lo


def _mlstm_kernel(qf_ref, vf_ref, ktf_ref, gf_ref, gtf_ref, qb_ref, vb_ref, ktb_ref, gb_ref, gtb_ref,
                  c0_ref, m0_ref, hf_ref, hb_ref, c_out_ref, m_out_ref, c_scr, m_scr):
    L = ML_CHUNK
    seq, c, n_chunks, _ = _ml_schedule(pl.program_id(0))

    @pl.when(c == 0)
    def _():
        is_ctx = seq < BATCH
        c_scr[...] = jnp.where(is_ctx, 0.0, c0_ref[0])
        m_scr[...] = jnp.where(is_ctx, 0.0, m0_ref[0])

    ri = lax.broadcasted_iota(jnp.int32, (L, L), 0)
    ci = lax.broadcasted_iota(jnp.int32, (L, L), 1)
    lane = lax.broadcasted_iota(jnp.int32, (L, ML_PAD), 1)
    is_ncol = lane == ML_DIM
    lower = ri >= ci
    upper = ri <= ci
    lower_b = jnp.where(lower, 1.0, 0.0).astype(bf16)
    upper_b = jnp.where(upper, 1.0, 0.0).astype(bf16)
    dirs = ((qf_ref, ktf_ref, vf_ref, gf_ref, gtf_ref, hf_ref), (qb_ref, ktb_ref, vb_ref, gb_ref, gtb_ref, hb_ref))
    for d, (q_ref, kt_ref, v_ref, g_ref, gt_ref, h_ref) in enumerate(dirs):
        g = g_ref[...][:, 0:N_GATE_COLS]
        gt = gt_ref[...]
        lf_c = _log_sigmoid(g)
        lf_r = _log_sigmoid(gt)
        b_cols = sum(jnp.dot(lower_b, part, preferred_element_type=f32) for part in _split3(lf_c))
        b_rows = sum(jnp.dot(part, upper_b, preferred_element_type=f32) for part in _split3(lf_r))
        tot_c = jnp.sum(lf_c, axis=0, keepdims=True)
        tot_r = jnp.sum(lf_r, axis=1, keepdims=True)
        visible = lower
        if d == 1:
            b_cols = tot_c - b_cols + lf_c
            b_rows = tot_r - b_rows + lf_r
            visible = upper
        for hd in range(ML_HEADS):
            st = d * ML_HEADS + hd
            ci_ = 2 * ML_HEADS * d + hd
            cf_ = ci_ + ML_HEADS
            sl = slice(hd * ML_PAD, (hd + 1) * ML_PAD)
            bc = b_cols[:, cf_:cf_ + 1]
            br = b_rows[cf_:cf_ + 1, :]
            li_r = gt[ci_:ci_ + 1, :]
            m_prev = m_scr[st:st + 1, 0:1]
            dmat = jnp.where(visible, bc - br + li_r, NEG_INF)
            inter = bc + m_prev
            m_t = jnp.maximum(inter, dmat.max(axis=-1, keepdims=True))
            w_intra = jnp.exp(dmat - m_t)
            w_inter = jnp.exp(inter - m_t)
            qh = q_ref[0, :, sl]
            kht = kt_ref[sl, :]
            v_aug = jnp.where(is_ncol, jnp.ones((), bf16), v_ref[0, :, sl])
            s = (jnp.dot(qh, kht, preferred_element_type=f32) * w_intra).astype(bf16)
            c_aug = c_scr[st]
            na = (w_inter * jnp.dot(qh, c_aug.astype(bf16), preferred_element_type=f32)
                  + jnp.dot(s, v_aug, preferred_element_type=f32))
            den = na[:, ML_DIM:ML_DIM + 1]
            h_ref[0, :, sl] = jnp.where(lane < ML_DIM, na / jnp.maximum(jnp.abs(den), jnp.exp(-m_t)), 0.0)
            b_end = tot_r[cf_:cf_ + 1, :]
            g_row = b_end - br + li_r
            m_new = jnp.maximum(b_end + m_prev, g_row.max(axis=1, keepdims=True))
            decay = jnp.exp(b_end + m_prev - m_new)
            kwt = (kht.astype(f32) * jnp.exp(g_row - m_new)).astype(bf16)
            c_scr[st] = decay * c_aug + jnp.dot(kwt, v_aug, preferred_element_type=f32)
            m_scr[st:st + 1, :] = jnp.broadcast_to(m_new, (1, LANE))

    @pl.when(c == n_chunks - 1)
    def _():
        c_out_ref[0] = c_scr[...]
        m_out_ref[0] = m_scr[...]


def _ml_schedule(s):
    nc_ctx, nc_lat = SEQ // ML_CHUNK, DEC_SEQ // ML_CHUNK
    n_ctx_steps = BATCH * nc_ctx
    is_ctx = s < n_ctx_steps
    t = s - n_ctx_steps
    seq = jnp.where(is_ctx, s // nc_ctx, BATCH + t // nc_lat)
    c = jnp.where(is_ctx, s % nc_ctx, t % nc_lat)
    nc = jnp.where(is_ctx, nc_ctx, nc_lat)
    base = jnp.where(is_ctx, (s // nc_ctx) * nc_ctx, n_ctx_steps + (t // nc_lat) * nc_lat)
    return seq, c, nc, base


def _mlstm(qvo, kt, gates, gates_t, c0, m0):
    L = ML_CHUNK
    n_seq = BATCH + DEC_BATCH

    def fwd(s):
        _, c, _, base = _ml_schedule(s)
        return base + c

    def bwd(s):
        _, c, nc, base = _ml_schedule(s)
        return base + nc - 1 - c

    seq_of = lambda s: _ml_schedule(s)[0]
    lat_of = lambda s: jnp.maximum(seq_of(s) - BATCH, 0)

    def specs(pos):
        return [pl.BlockSpec((1, L, ML_PW), lambda s, j=j: (pos(s), 0, j)) for j in range(2)] + [
            pl.BlockSpec((ML_PW, L), lambda s: (0, pos(s))),
            pl.BlockSpec((L, LANE), lambda s: (pos(s), 0)),
            pl.BlockSpec((N_GATE_COLS, L), lambda s: (0, pos(s)))]

    q3 = qvo.reshape(N_TOK // L, L, W_B)
    n_str = 2 * ML_HEADS
    return pl.pallas_call(
        _mlstm_kernel,
        grid=(N_TOK // L,),
        in_specs=specs(fwd) + specs(bwd) + [
            pl.BlockSpec((1, n_str, ML_PAD, CAUG), lambda s: (lat_of(s), 0, 0, 0)),
            pl.BlockSpec((1, n_str, LANE), lambda s: (lat_of(s), 0, 0))],
        out_specs=[pl.BlockSpec((1, L, ML_PW), lambda s: (fwd(s), 0, 0)),
                   pl.BlockSpec((1, L, ML_PW), lambda s: (bwd(s), 0, 0)),
                   pl.BlockSpec((1, n_str, ML_PAD, CAUG), lambda s: (seq_of(s), 0, 0, 0)),
                   pl.BlockSpec((1, n_str, LANE), lambda s: (seq_of(s), 0, 0))],
        out_shape=[jax.ShapeDtypeStruct((N_TOK // L, L, ML_PW), f32),
                   jax.ShapeDtypeStruct((N_TOK // L, L, ML_PW), f32),
                   jax.ShapeDtypeStruct((n_seq, n_str, ML_PAD, CAUG), f32),
                   jax.ShapeDtypeStruct((n_seq, n_str, LANE), f32)],
        scratch_shapes=[pltpu.VMEM((n_str, ML_PAD, CAUG), f32), pltpu.VMEM((n_str, LANE), f32)],
        compiler_params=pltpu.CompilerParams(dimension_semantics=("arbitrary",), vmem_limit_bytes=VMEM_LIMIT),
        name="mlstm",
    )(q3, q3, kt, gates, gates_t, q3, q3, kt, gates, gates_t, c0, m0)


def _pack_ml_state(C, n, m):
    B = C.shape[0]
    c_aug = jnp.zeros((B, 2, ML_HEADS, ML_PAD, CAUG), f32)
    c_aug = c_aug.at[:, :, :, :ML_DIM, :ML_DIM].set(C.astype(f32))
    c_aug = c_aug.at[:, :, :, :ML_DIM, ML_DIM].set(n.astype(f32))
    m_b = jnp.broadcast_to(m.astype(f32)[..., None], (B, 2, ML_HEADS, LANE))
    return c_aug.reshape(B, 2 * ML_HEADS, ML_PAD, CAUG), m_b.reshape(B, 2 * ML_HEADS, LANE)


def _unpack_ml_state(c_aug, m_b):
    B = c_aug.shape[0]
    c_aug = c_aug.reshape(B, 2, ML_HEADS, ML_PAD, CAUG)
    return (c_aug[:, :, :, :ML_DIM, :ML_DIM], c_aug[:, :, :, :ML_DIM, ML_DIM],
            m_b.reshape(B, 2, ML_HEADS, LANE)[..., 0])


def _pool_rows(u_prev, u_cur, u_next, w_bd, scale, t0, seq_len):
    tm = u_cur.shape[0]
    u_win = jnp.concatenate([u_prev, u_cur, u_next], axis=0)
    u_hi = u_win.astype(bf16)
    u_lo = (u_win - u_hi.astype(f32)).astype(bf16)
    lane = lax.broadcasted_iota(jnp.int32, (1, LANE), 1)
    blocks = []
    for r0 in range(0, tm, POOL_BLOCK):
        win = slice(r0, r0 + POOL_BLOCK + 2 * POOL_HALO)
        t_abs = t0 + r0 + lax.broadcasted_iota(jnp.int32, (POOL_BLOCK, 1), 0)
        s_abs = t0 + r0 - POOL_HALO + lax.broadcasted_iota(jnp.int32, (1, POOL_BLOCK + 2 * POOL_HALO), 1)
        t_loc = t_abs & (seq_len - 1)
        seq_start = t_abs - t_loc
        means = []
        for w in POOL_WINDOWS:
            lo = jnp.maximum(t_loc - w // 2, 0)
            hi = jnp.minimum(t_loc - w // 2 + w, seq_len)
            in_win = (s_abs >= seq_start + lo) & (s_abs < seq_start + hi)
            means.append((jnp.where(in_win, 1.0, 0.0).astype(bf16), 1.0 / (hi - lo).astype(f32)))
        pooled = []
        for p in range(POOL_GROUPS // 2):
            sl = slice(p * LANE, (p + 1) * LANE)
            halves = []
            for a, inv_cnt in means[2 * p:2 * p + 2]:
                tot = (jnp.dot(a, u_hi[win, sl], preferred_element_type=f32)
                       + jnp.dot(a, u_lo[win, sl], preferred_element_type=f32))
                halves.append(tot * inv_cnt)
            pooled.append(jnp.where(lane < POOL_DIM, halves[0], halves[1]) - u_cur[r0:r0 + POOL_BLOCK, sl])
        blocks.append(jnp.concatenate(pooled, axis=1))
    pooled = jnp.concatenate(blocks, axis=0).astype(bf16)
    return jnp.dot(pooled, w_bd, preferred_element_type=f32) * scale


def _top2_sum(a, b, c, d):
    hi1, lo1 = jnp.maximum(a, b), jnp.minimum(a, b)
    hi2, lo2 = jnp.maximum(c, d), jnp.minimum(c, d)
    return jnp.maximum(hi1, hi2) + jnp.maximum(jnp.minimum(hi1, hi2), jnp.maximum(lo1, lo2))


def _first_match(vals, target):
    idx = jnp.full_like(target, float(len(vals) - 1))
    for i in range(len(vals) - 2, -1, -1):
        idx = jnp.where(vals[i] == target, float(i), idx)
    return idx


def _pick(vals, idx):
    out = vals[-1]
    for i in range(len(vals) - 2, -1, -1):
        out = jnp.where(idx == float(i), vals[i], out)
    return out


def _route(logits_t, bias_t):
    scores = jax.nn.sigmoid(logits_t)
    sel = scores + bias_t
    row = lambda a, i: a[i:i + 1, :]
    grp = [_top2_sum(*[row(sel, EXPERTS_PER_GROUP * g + i) for i in range(EXPERTS_PER_GROUP)])
           for g in range(N_EXPERT_GROUPS)]
    best = functools.reduce(jnp.maximum, grp)
    gidx = _first_match(grp, best)
    sel_g = [_pick([row(sel, EXPERTS_PER_GROUP * g + i) for g in range(N_EXPERT_GROUPS)], gidx)
             for i in range(EXPERTS_PER_GROUP)]
    sco_g = [_pick([row(scores, EXPERTS_PER_GROUP * g + i) for g in range(N_EXPERT_GROUPS)], gidx)
             for i in range(EXPERTS_PER_GROUP)]
    i0 = _first_match(sel_g, functools.reduce(jnp.maximum, sel_g))
    rest = [jnp.where(i0 == float(i), -jnp.inf, sel_g[i]) for i in range(EXPERTS_PER_GROUP)]
    i1 = _first_match(rest, functools.reduce(jnp.maximum, rest))
    s0, s1 = _pick(sco_g, i0), _pick(sco_g, i1)
    tot = s0 + s1
    rid = lax.broadcasted_iota(jnp.int32, (LANE, logits_t.shape[1]), 0)
    rows = (EXPERTS_PER_GROUP * gidx + i0, EXPERTS_PER_GROUP * gidx + i1, s0 / tot, s1 / tot)
    out = jnp.zeros(rid.shape, f32)
    for i, r in enumerate(rows):
        out = jnp.where(rid == i, r, out)
    return out


def _out_kernel(xc_ref, xl_ref, mod_ref, oac_ref, oal_ref, hf_ref, hb_ref, ob_ref, up_ref, uc_ref, un_ref, wp_ref, psc_ref,
                mln_ref, wo_ref, n2_ref, wr_ref, br_ref, x1_ref, h2_ref, rt_ref, rc_ref):
    tm = xc_ref.shape[0]
    i = pl.program_id(0)
    is_ctx = i < N_CTX // tm
    mod = mod_ref[0]
    out_a = jnp.where(is_ctx, oac_ref[...], oal_ref[...])
    out_c = _pool_rows(up_ref[...], uc_ref[...], un_ref[...], wp_ref[...], psc_ref[...], i * tm,
                       jnp.where(is_ctx, SEQ, DEC_SEQ)).astype(bf16)
    hsum = hf_ref[...] + hb_ref[...]
    outs_b = []
    for hd in range(ML_HEADS):
        sl = slice(hd * ML_PAD, (hd + 1) * ML_PAD)
        hh = hsum[:, sl]
        ms = jnp.sum(hh * hh, axis=-1, keepdims=True) * (1.0 / ML_DIM)
        hn = hh * lax.rsqrt(ms + EPS) * mln_ref[:, sl]
        outs_b.append((jax.nn.sigmoid(ob_ref[:, sl].astype(f32)) * hn).astype(bf16))
    out_b = jnp.concatenate(outs_b, axis=1)
    mixed = (jnp.dot(out_a, wo_ref[0:NA_WIDTH, :], preferred_element_type=f32)
             + jnp.dot(out_b, wo_ref[NA_WIDTH:NA_WIDTH + ML_PW, :], preferred_element_type=f32)
             + jnp.dot(out_c, wo_ref[NA_WIDTH + ML_PW:, :], preferred_element_type=f32))
    x1 = jnp.where(is_ctx, xc_ref[...], xl_ref[...]) + mod[2:3] * mixed
    x1_ref[...] = x1
    h2 = x1 * lax.rsqrt(jnp.mean(x1 * x1, axis=-1, keepdims=True) + EPS) * n2_ref[...]
    h2 = h2 * (1.0 + mod[4:5]) + mod[3:4]
    h2_ref[...] = h2.astype(bf16)
    h_hi = h2.astype(bf16)
    h_lo = (h2 - h_hi.astype(f32)).astype(bf16)
    w_hi = wr_ref[...].astype(bf16)
    w_lo = (wr_ref[...] - w_hi.astype(f32)).astype(bf16)
    route_t = _route(_nt(w_hi, h_hi) + (_nt(w_hi, h_lo) + _nt(w_lo, h_hi)), br_ref[...])
    rt_ref[...] = route_t[0:8]
    rc_ref[...] = route_t.T


def _out_proj(x_ctx, x_lat, x_lat_block0, mods, layer, oa_ctx, oa_lat, hf, hb, qvo, pin, w_bd, psc, mln, wo, n2,
              wr_t, br_t):
    tm = TOK_TILE
    const = lambda i: (0, 0)
    lyr = lambda shape: pl.BlockSpec((None,) + shape, lambda i: (layer, 0, 0))
    row = lambda i: (i, 0)
    n_ctx_tiles = N_CTX // tm
    halo_blocks = tm // POOL_HALO
    return pl.pallas_call(
        _out_kernel,
        grid=(N_TOK // tm,),
        in_specs=[pl.BlockSpec((tm, D_MODEL), lambda i: (jnp.minimum(i, n_ctx_tiles - 1), 0)),
                  pl.BlockSpec((tm, D_MODEL), lambda i: (jnp.maximum(i - n_ctx_tiles, 0) + x_lat_block0, 0)),
                  pl.BlockSpec((1, 6, D_MODEL), lambda i: (_mod_row(i, tm), 0, 0)),
                  pl.BlockSpec((tm, NA_WIDTH), lambda i: (jnp.minimum(i, n_ctx_tiles - 1), 0)),
                  pl.BlockSpec((tm, NA_WIDTH), lambda i: (jnp.maximum(i - n_ctx_tiles, 0), 0)),
                  pl.BlockSpec((tm, ML_PW), row),
                  pl.BlockSpec((tm, ML_PW), row),
                  pl.BlockSpec((tm, ML_PW), lambda i: (i, 2)),
                  pl.BlockSpec((POOL_HALO, POOL_WIDTH), lambda i: (jnp.maximum(i * halo_blocks - 1, 0), 0)),
                  pl.BlockSpec((tm, POOL_WIDTH), row),
                  pl.BlockSpec((POOL_HALO, POOL_WIDTH),
                               lambda i: (jnp.minimum((i + 1) * halo_blocks, N_TOK // POOL_HALO - 1), 0)),
                  lyr((POOL_WIDTH, POOL_WIDTH)), lyr((1, POOL_WIDTH)), lyr((1, ML_PW)),
                  lyr((NA_WIDTH + ML_PW + POOL_WIDTH, D_MODEL)), lyr((1, D_MODEL)),
                  pl.BlockSpec((N_EXPERTS, D_MODEL), const),
                  pl.BlockSpec((N_EXPERTS, 1), const)],
        out_specs=[pl.BlockSpec((tm, D_MODEL), row),
                   pl.BlockSpec((tm, D_MODEL), row),
                   pl.BlockSpec((8, tm), lambda i: (0, i)),
                   pl.BlockSpec((tm, LANE), row)],
        out_shape=[jax.ShapeDtypeStruct((N_TOK, D_MODEL), f32),
                   jax.ShapeDtypeStruct((N_TOK, D_MODEL), bf16),
                   jax.ShapeDtypeStruct((8, N_TOK), f32),
                   jax.ShapeDtypeStruct((N_TOK, LANE), f32)],
        compiler_params=pltpu.CompilerParams(dimension_semantics=("arbitrary",), vmem_limit_bytes=VMEM_LIMIT),
        name="out_proj_router",
    )(x_ctx, x_lat, mods, oa_ctx, oa_lat, hf, hb, qvo, pin, pin, pin, w_bd, psc, mln, wo, n2, wr_t, br_t)


def _ceil_to(x, m):
    return jnp.floor((x + (m - 1)) * (1.0 / m)) * m


def _prefix_over_experts(v):
    er = lax.broadcasted_iota(jnp.int32, (N_EXPERTS, N_EXPERTS), 0)
    ec = lax.broadcasted_iota(jnp.int32, (N_EXPERTS, N_EXPERTS), 1)
    return jnp.dot(jnp.where(ec < er, 1.0, 0.0), v, preferred_element_type=f32, precision=HI)


def _experts_to_lanes(v):
    sub = lax.broadcasted_iota(jnp.int32, (N_EXPERTS, LANE), 0)
    lane = lax.broadcasted_iota(jnp.int32, (N_EXPERTS, LANE), 1)
    return jnp.sum(jnp.where(sub == lane, v, 0.0), axis=0, keepdims=True)


def _expert_hits(rt):
    rid = lax.broadcasted_iota(jnp.int32, (N_EXPERTS, rt.shape[1]), 0).astype(f32)
    oh0 = rid == rt[0:1, :]
    oh1 = rid == rt[1:2, :]
    both = jnp.where(oh0 | oh1, 1.0, 0.0)
    runs = jnp.broadcast_to(_ceil_to(jnp.sum(both, axis=1, keepdims=True), MOE_CHUNK), (N_EXPERTS, LANE))
    return oh0, oh1, both, runs


def _rank_kernel(rt_all_ref, rt_ref, pos_ref, te_ref, tab_ref, carry_ref):
    tm = rt_ref.shape[1]
    step = pl.program_id(0)

    @pl.when(step == 0)
    def _():
        totals = jnp.zeros((N_EXPERTS, LANE), f32)
        for i in range(rt_all_ref.shape[1] // tm):
            totals = totals + _expert_hits(rt_all_ref[:, i * tm:(i + 1) * tm])[3]
        padded = _ceil_to(totals, MOE_TILE)
        off = _prefix_over_experts(padded)
        carry_ref[...] = off
        total = jnp.sum(padded, axis=0, keepdims=True)
        n_used = total * (1.0 / MOE_TILE)
        tile = lax.broadcasted_iota(jnp.int32, (1, LANE), 1).astype(f32)
        row0 = jnp.minimum(tile, n_used - 1.0) * MOE_TILE
        expert = jnp.sum(jnp.where(off <= row0, 1.0, 0.0), axis=0, keepdims=True) - 1.0
        sub = lax.broadcasted_iota(jnp.int32, (8, LANE), 0)
        te_ref[...] = jnp.where(sub == 0, expert, jnp.where(sub == 1, n_used, 0.0)).astype(jnp.int32)

    @pl.when(step > 0)
    def _():
        oh0, oh1, both, runs = _expert_hits(rt_ref[...])
        sr = lax.broadcasted_iota(jnp.int32, (tm, tm), 0)
        sc = lax.broadcasted_iota(jnp.int32, (tm, tm), 1)
        earlier = jnp.dot(both.astype(bf16), jnp.where(sr < sc, 1.0, 0.0).astype(bf16),
                          preferred_element_type=f32)
        g_off = carry_ref[...]
        l_off = _prefix_over_experts(runs)
        g_row = g_off[:, 0:1] + earlier
        l_row = l_off[:, 0:1] + earlier
        pick = lambda oh, v: jnp.sum(jnp.where(oh, v, 0.0), axis=0, keepdims=True)
        rows = (pick(oh0, g_row), pick(oh1, g_row), pick(oh0, l_row), pick(oh1, l_row))
        sub = lax.broadcasted_iota(jnp.int32, (8, tm), 0)
        out = jnp.zeros((8, tm), f32)
        for k, r in enumerate(rows):
            out = jnp.where(sub == k, r, out)
        pos_ref[...] = out.astype(jnp.int32)
        sub = lax.broadcasted_iota(jnp.int32, (8, LANE), 0)
        tab = jnp.zeros((8, LANE), f32)
        for k, v in enumerate((runs * (1.0 / MOE_CHUNK), l_off, g_off)):
            tab = jnp.where(sub == k, _experts_to_lanes(v), tab)
        tab_ref[0] = tab.astype(jnp.int32)
        carry_ref[...] = g_off + runs


def _rank(route_t):
    tm = TOK_TILE
    n_tiles = N_TOK // tm
    tile_of = lambda s: jnp.maximum(s - 1, 0)
    return pl.pallas_call(
        _rank_kernel,
        grid=(1 + n_tiles,),
        in_specs=[pl.BlockSpec((8, N_TOK), lambda s: (0, 0)),
                  pl.BlockSpec((8, tm), lambda s: (0, tile_of(s)))],
        out_specs=[pl.BlockSpec((8, tm), lambda s: (0, tile_of(s))),
                   pl.BlockSpec((8, LANE), lambda s: (0, 0)),
                   pl.BlockSpec((1, 8, LANE), lambda s: (tile_of(s), 0, 0))],
        out_shape=[jax.ShapeDtypeStruct((8, N_TOK), jnp.int32),
                   jax.ShapeDtypeStruct((8, LANE), jnp.int32),
                   jax.ShapeDtypeStruct((n_tiles, 8, LANE), jnp.int32)],
        scratch_shapes=[pltpu.VMEM((N_EXPERTS, LANE), f32)],
        compiler_params=pltpu.CompilerParams(dimension_semantics=("arbitrary",)),
        name="moe_rank",
    )(route_t, route_t)


def _dispatch_kernel(tab_ref, h_ref, rows_ref, xs_in_ref, xs_ref, loc, sem):
    del xs_in_ref
    tm = h_ref.shape[0]
    i = pl.program_id(0)
    rid = lax.broadcasted_iota(jnp.int32, (MOE_LOCAL_ROWS, tm), 0)
    sel = (rid == rows_ref[2:3, :]) | (rid == rows_ref[3:4, :])
    loc[...] = jnp.dot(jnp.where(sel, 1.0, 0.0).astype(bf16), h_ref[...], preferred_element_type=f32).astype(bf16)

    def chunk_copy(src_row, dst_row):
        return pltpu.make_async_copy(loc.at[pl.ds(pl.multiple_of(src_row, MOE_CHUNK), MOE_CHUNK)],
                                     xs_ref.at[pl.ds(pl.multiple_of(dst_row, MOE_CHUNK), MOE_CHUNK)], sem)

    n_total = 0
    for e in range(N_EXPERTS):
        n_chunks, l_off, g_off = (tab_ref[(3 * i + k) * N_EXPERTS + e] for k in range(3))

        def issue(c, carry, l_off=l_off, g_off=g_off):
            chunk_copy(l_off + c * MOE_CHUNK, g_off + c * MOE_CHUNK).start()
            return carry

        lax.fori_loop(0, n_chunks, issue, 0)
        n_total = n_total + n_chunks

    def wait_one(c, carry):
        chunk_copy(0, 0).wait()
        return carry

    lax.fori_loop(0, n_total, wait_one, 0)


def _dispatch(run_table, h2, rows, xs_init):
    tm = TOK_TILE
    return pl.pallas_call(
        _dispatch_kernel,
        grid_spec=pltpu.PrefetchScalarGridSpec(
            num_scalar_prefetch=1,
            grid=(N_TOK // tm,),
            in_specs=[pl.BlockSpec((tm, D_MODEL), lambda i, tab: (i, 0)),
                      pl.BlockSpec((8, tm), lambda i, tab: (0, i)),
                      pl.BlockSpec(memory_space=pl.ANY)],
            out_specs=pl.BlockSpec(memory_space=pl.ANY),
            scratch_shapes=[pltpu.VMEM((MOE_LOCAL_ROWS, D_MODEL), bf16), pltpu.SemaphoreType.DMA(())]),
        out_shape=jax.ShapeDtypeStruct(xs_init.shape, xs_init.dtype),
        input_output_aliases={3: 0},
        compiler_params=pltpu.CompilerParams(dimension_semantics=("arbitrary",), vmem_limit_bytes=VMEM_LIMIT),
        name="moe_dispatch",
    )(run_table, h2, rows, xs_init)


def _expert_kernel(te_ref, xs_ref, wg_hbm, wu_hbm, wd_hbm, ys_ref, wg_f32, wu_f32, wd_f32, wg_bf, wu_bf, wd_bf,
                   slot_ref, sem, *, layer):
    j = pl.program_id(0)
    n_used = te_ref[1, 0]
    used = j < n_used
    expert = te_ref[0, j]
    new_expert = jnp.logical_or(j == 0, expert != te_ref[0, jnp.maximum(j - 1, 0)])

    def weight_copies(e, slot):
        return [pltpu.make_async_copy(hbm.at[layer, e], buf.at[slot], sem.at[slot])
                for hbm, buf in ((wg_hbm, wg_f32), (wu_hbm, wu_f32), (wd_hbm, wd_f32))]

    @pl.when(j == 0)
    def _():
        slot_ref[0] = 1
        for cp in weight_copies(expert, 0):
            cp.start()

    @pl.when(jnp.logical_not(used))
    def _():
        ys_ref[...] = jnp.zeros_like(ys_ref)

    @pl.when(used & new_expert)
    def _():
        slot = 1 - slot_ref[0]
        slot_ref[0] = slot
        for cp in weight_copies(expert, slot):
            cp.wait()
        wg_bf[...] = wg_f32[slot].astype(bf16)
        wu_bf[...] = wu_f32[slot].astype(bf16)
        wd_bf[...] = wd_f32[slot].astype(bf16)
        nxt = lax.while_loop(lambda t: (t < n_used) & (te_ref[0, jnp.minimum(t, LANE - 1)] == expert),
                             lambda t: t + 1, j + 1)

        @pl.when(nxt < n_used)
        def _():
            for cp in weight_copies(te_ref[0, nxt], 1 - slot):
                cp.start()

    @pl.when(used)
    def _():
        x = xs_ref[...]
        hg = jnp.dot(x, wg_bf[...], preferred_element_type=f32)
        hu = jnp.dot(x, wu_bf[...], preferred_element_type=f32)
        hid = (hg * jax.nn.sigmoid(hg) * hu).astype(bf16)
        ys_ref[...] = jnp.dot(hid, wd_bf[...], preferred_element_type=f32)


def _experts(te, xs, w_gate, w_up, w_down, layer):
    tm = MOE_TILE
    row = lambda j, te: (jnp.minimum(j, te[1, 0] - 1), 0)
    hbm = pl.BlockSpec(memory_space=pl.ANY)
    return pl.pallas_call(
        functools.partial(_expert_kernel, layer=layer),
        grid_spec=pltpu.PrefetchScalarGridSpec(
            num_scalar_prefetch=1,
            grid=(MOE_ROWS // tm,),
            in_specs=[pl.BlockSpec((tm, D_MODEL), row), hbm, hbm, hbm],
            out_specs=pl.BlockSpec((tm, D_MODEL), lambda j, te: (j, 0)),
            scratch_shapes=[pltpu.VMEM((2, D_MODEL, D_EXPERT), f32), pltpu.VMEM((2, D_MODEL, D_EXPERT), f32),
                            pltpu.VMEM((2, D_EXPERT, D_MODEL), f32),
                            pltpu.VMEM((D_MODEL, D_EXPERT), bf16), pltpu.VMEM((D_MODEL, D_EXPERT), bf16),
                            pltpu.VMEM((D_EXPERT, D_MODEL), bf16),
                            pltpu.SMEM((1,), jnp.int32), pltpu.SemaphoreType.DMA((2,))]),
        out_shape=jax.ShapeDtypeStruct((MOE_ROWS, D_MODEL), f32),
        compiler_params=pltpu.CompilerParams(dimension_semantics=("arbitrary",), vmem_limit_bytes=VMEM_LIMIT),
        name="moe_experts",
    )(te, xs, w_gate, w_up, w_down)


def _gather_expert_rows(pos0_ref, pos1_ref, ys_ref, buf, sem, tile_of):
    rows = buf.shape[2]
    i = pl.program_id(0)
    slot = i % 2

    def issue(tile, sl):
        base = tile * rows

        def body(t, carry):
            for s, pos_ref in enumerate((pos0_ref, pos1_ref)):
                pltpu.make_async_copy(ys_ref.at[pl.ds(pos_ref[base + t], 1)], buf.at[sl, s, pl.ds(t, 1)],
                                      sem.at[sl]).start()
            return carry

        lax.fori_loop(0, rows, body, 0, unroll=8)

    @pl.when(i == 0)
    def _():
        issue(tile_of(0), 0)

    @pl.when(i + 1 < pl.num_programs(0))
    def _():
        issue(tile_of(i + 1), 1 - slot)

    for s in range(2):
        pltpu.make_async_copy(ys_ref.at[pl.ds(0, rows)], buf.at[slot, s], sem.at[slot]).wait()
    return buf[slot, 0], buf[slot, 1]


def _moe_residual(pos0_ref, pos1_ref, ys_ref, x1_ref, rc_ref, mod_ref, buf, sem, tile_of=lambda step: step):
    y0, y1 = _gather_expert_rows(pos0_ref, pos1_ref, ys_ref, buf, sem, tile_of)
    rc = rc_ref[...]
    return x1_ref[...] + mod_ref[0][5:6] * (rc[:, 2:3] * y0 + rc[:, 3:4] * y1)


def _final_kernel(pos0_ref, pos1_ref, ys_ref, x1_ref, rc_ref, mod_ref, fn_ref, yc_ref, yl_ref, buf, sem):
    x2 = _moe_residual(pos0_ref, pos1_ref, ys_ref, x1_ref, rc_ref, mod_ref, buf, sem)
    y = x2 * lax.rsqrt(jnp.mean(x2 * x2, axis=-1, keepdims=True) + EPS) * fn_ref[...]
    is_ctx = pl.program_id(0) < N_CTX // x1_ref.shape[0]

    @pl.when(is_ctx)
    def _():
        yc_ref[...] = y

    @pl.when(jnp.logical_not(is_ctx))
    def _():
        yl_ref[...] = y


def _final_combine(pos0, pos1, ys, x1, rc, mods, fn):
    tc = COMBINE_TILE
    row = lambda i, p0, p1: (i, 0)
    n_ctx_tiles = N_CTX // tc
    return pl.pallas_call(
        _final_kernel,
        grid_spec=pltpu.PrefetchScalarGridSpec(
            num_scalar_prefetch=2,
            grid=(N_TOK // tc,),
            in_specs=[pl.BlockSpec(memory_space=pl.ANY),
                      pl.BlockSpec((tc, D_MODEL), row),
                      pl.BlockSpec((tc, LANE), row),
                      pl.BlockSpec((1, 6, D_MODEL), lambda i, p0, p1: (_mod_row(i, tc), 0, 0)),
                      pl.BlockSpec((1, D_MODEL), lambda i, p0, p1: (0, 0))],
            out_specs=[pl.BlockSpec((tc, D_MODEL), lambda i, p0, p1: (jnp.minimum(i, n_ctx_tiles - 1), 0)),
                       pl.BlockSpec((tc, D_MODEL), lambda i, p0, p1: (jnp.maximum(i - n_ctx_tiles, 0), 0))],
            scratch_shapes=[pltpu.VMEM((2, 2, tc, D_MODEL), f32), pltpu.SemaphoreType.DMA((2,))]),
        out_shape=[jax.ShapeDtypeStruct((N_CTX, D_MODEL), f32), jax.ShapeDtypeStruct((N_LAT, D_MODEL), f32)],
        compiler_params=pltpu.CompilerParams(dimension_semantics=("arbitrary",), vmem_limit_bytes=VMEM_LIMIT),
        name="moe_combine_final",
    )(pos0, pos1, ys, x1, rc, mods, fn)


def _moe_experts(h2, route_t, w_gate, w_up, w_down, layer, xs_buf):
    rows, te, runs = _rank(route_t)
    xs = _dispatch(runs[:, :3, :N_EXPERTS].reshape(-1), h2, rows, xs_buf)
    return rows[0], rows[1], _experts(te, xs, w_gate, w_up, w_down, layer), xs


def _pad_heads(w):
    lead = w.shape[:-1]
    w = w.reshape(*lead, ML_HEADS, ML_DIM)
    w = jnp.pad(w, [(0, 0)] * len(lead) + [(0, 0), (0, ML_PAD - ML_DIM)])
    return w.reshape(*lead, ML_PW)


def _pack_in_cols(wb):
    o = 0
    qa = wb[..., o:o + NA_WIDTH] * (NA_DIM ** -0.5)
    ka = wb[..., o + NA_WIDTH:o + 2 * NA_WIDTH]
    va = wb[..., o + 2 * NA_WIDTH:o + 3 * NA_WIDTH]
    o += 3 * NA_WIDTH
    qb, kb, vb, ob = [_pad_heads(wb[..., o + j * ML_WIDTH:o + (j + 1) * ML_WIDTH]) for j in range(4)]
    o += 4 * ML_WIDTH
    gates = wb[..., o:o + N_GATE_COLS]
    o += N_GATE_COLS
    pool = wb[..., o:o + POOL_WIDTH]
    main = jnp.concatenate([qa, ka, va, qb, vb, ob, pool], axis=-1)
    gates_p = jnp.pad(gates, [(0, 0)] * (gates.ndim - 1) + [(0, LANE - N_GATE_COLS)])
    return main, gates_p, jnp.concatenate([kb, gates], axis=-1)


def _pack_w_in(w, b):
    w_main, w_gates, w_feat = _pack_in_cols(w)
    b_main, b_gates, b_feat = _pack_in_cols(b.astype(f32))
    return (w_main.astype(bf16), b_main[:, None], w_gates.astype(bf16), b_gates[:, None],
            jnp.swapaxes(w_feat, 1, 2).astype(bf16), b_feat[:, :, None])


def _pack_w_out(w):
    n_l = w.shape[0]
    wb = w[:, NA_WIDTH:NA_WIDTH + ML_WIDTH].reshape(n_l, ML_HEADS, ML_DIM, D_MODEL)
    wb = jnp.pad(wb, ((0, 0), (0, 0), (0, ML_PAD - ML_DIM), (0, 0))).reshape(n_l, ML_PW, D_MODEL)
    return jnp.concatenate([w[:, :NA_WIDTH], wb, w[:, NA_WIDTH + ML_WIDTH:]], axis=1).astype(bf16)


def _block_diag(w):
    n_l, g, c, _ = w.shape
    eye = jnp.eye(g, dtype=w.dtype)
    return (eye[None, :, None, :, None] * w[:, :, :, None, :]).reshape(n_l, g * c, g * c)


def kernel(x_prompt, x_sample, cache_k_attn, cache_v_attn, state_mlstm_C, state_mlstm_n, state_mlstm_m, c, c_ctx,
           w_ada, b_ada, norm1, w_in, b_in, rpb, ml_norm, w_pool, pool_scale, w_out, norm2, w_router, b_router,
           w_gate, w_up, w_down, final_norm):
    dt = x_prompt.dtype
    x_ctx = x_prompt.reshape(N_CTX, D_MODEL).astype(f32)
    x_lat = x_sample.reshape(N_LAT, D_MODEL).astype(f32)
    x_lat_block0 = 0
    cvec = jnp.concatenate([c_ctx[None], c, jnp.zeros((8 - 1 - DEC_BATCH, D_MODEL), c.dtype)], axis=0).astype(f32)
    mods_all = _ada(cvec, w_ada.astype(f32), b_ada.astype(f32))
    mods_all = mods_all[:, :1 + DEC_BATCH].reshape(DEPTH, 1 + DEC_BATCH, 6, D_MODEL)

    wr_t = w_router.astype(f32).T
    br_t = b_router.astype(f32)[:, None]
    fn = final_norm.astype(f32)[None]

    na_bias = _na_bias_tables(rpb)
    xs_buf = jnp.zeros((MOE_ROWS, D_MODEL), bf16)
    in_params = (norm1.astype(f32)[:, None],) + _pack_w_in(w_in, b_in)
    out_params = (_block_diag(w_pool.astype(f32)).astype(bf16), pool_scale.astype(f32)[:, None],
                  _pad_heads(ml_norm.astype(f32))[:, None], _pack_w_out(w_out), norm2.astype(f32)[:, None])

    new_k = jnp.zeros((BATCH, DEPTH, SEQ, NA_WIDTH), f32)
    new_v = jnp.zeros_like(new_k)
    Cs, ns, ms = [], [], []
    pending = None
    for l in range(DEPTH):
        mods = mods_all[l]
        if pending is None:
            qkva, new_k, new_v, qvo, kt, gates, gates_t, pin = _in_proj(x_ctx, x_lat, mods, l, *in_params,
                                                                        new_k, new_v)
        else:
            x, qkva, new_k, new_v, qvo, kt, gates, gates_t, pin = _moe_in_proj(
                *pending, mods_all[l - 1], mods, l, *in_params, new_k, new_v)
            x_ctx, x_lat, x_lat_block0 = x, x, N_CTX // TOK_TILE

        oa_ctx = _ctx_attention(qkva.reshape(N_TOK // SEQ, SEQ, W_A))
        ck = (cache_k_attn[:, l].reshape(DEC_BATCH, PAST_LEN, NA_WIDTH)).astype(bf16)
        cv = (cache_v_attn[:, l].reshape(DEC_BATCH, PAST_LEN, NA_WIDTH)).astype(bf16)
        oa_lat = _neighborhood_attention(qkva.reshape(N_TOK // DEC_SEQ, DEC_SEQ, W_A), ck, cv, na_bias, l)

        c_l, m_l = _pack_ml_state(state_mlstm_C[:, l], state_mlstm_n[:, l], state_mlstm_m[:, l])
        hf, hb, c_fin, m_fin = _mlstm(qvo, kt, gates, gates_t, c_l, m_l)
        C_l, n_l, m_l2 = _unpack_ml_state(c_fin[:BATCH], m_fin[:BATCH])
        Cs.append(C_l)
        ns.append(n_l)
        ms.append(m_l2)

        x1, h2, route_t, rc = _out_proj(x_ctx, x_lat, x_lat_block0, mods, l,
                                        oa_ctx.reshape(N_CTX, NA_WIDTH), oa_lat.reshape(N_LAT, NA_WIDTH),
                                        hf.reshape(N_TOK, ML_PW), hb.reshape(N_TOK, ML_PW), qvo, pin,
                                        *out_params, wr_t, br_t)
        pos0, pos1, ys, xs_buf = _moe_experts(h2, route_t, w_gate, w_up, w_down, l, xs_buf)
        pending = (pos0, pos1, ys, x1, rc)

    x = _final_combine(*pending, mods_all[DEPTH - 1], fn)
    y_prompt = x[0].reshape(BATCH, SEQ, D_MODEL).astype(dt)
    y_sample = x[1].reshape(DEC_BATCH, DEC_SEQ, D_MODEL).astype(dt)
    new_k, new_v = (a.reshape(BATCH, DEPTH, SEQ, NA_HEADS, NA_DIM).astype(dt) for a in (new_k, new_v))
    return (y_prompt, y_sample, new_k, new_v,
            jnp.stack(Cs, axis=1).astype(dt), jnp.stack(ns, axis=1).astype(dt), jnp.stack(ms, axis=1).astype(dt))
```

```python
import functools

import numpy as np
import jax
import jax.numpy as jnp
from jax import lax
from jax.experimental import pallas as pl
from jax.experimental.pallas import tpu as pltpu

D_MODEL = 1024
BATCH = 16
SEQ = 256
DEPTH = 4
DEC_BATCH = 2
DEC_SEQ = 4096
PAST_LEN = 256
GRID_W = 64
EPS = 1e-6
NEG_INF = -1e30
NA_HEADS = 6
NA_DIM = 64
NA_WIDTH = NA_HEADS * NA_DIM
NA_ROWS = 8
NA_COLS = 16
RPB_ROWS = 2 * NA_ROWS - 1
RPB_COLS = 2 * NA_COLS - 1
ML_HEADS = 4
ML_DIM = 96
ML_WIDTH = ML_HEADS * ML_DIM
POOL_WINDOWS = (2, 4, 8, 16)
POOL_GROUPS = 4
POOL_DIM = 64
POOL_WIDTH = POOL_GROUPS * POOL_DIM
N_GATE_COLS = 4 * ML_HEADS
N_EXPERTS = 16
N_EXPERT_GROUPS = 4
EXPERTS_PER_GROUP = N_EXPERTS // N_EXPERT_GROUPS
D_EXPERT = 512
ADA_DIM = 6 * D_MODEL

N_CTX = BATCH * SEQ
N_LAT = DEC_BATCH * DEC_SEQ
N_TOK = N_CTX + N_LAT
LANE = 128
ML_PAD = LANE
ML_PW = ML_HEADS * ML_PAD
CAUG = ML_PAD
NA_PAIRS = NA_HEADS // 2
TOK_TILE = 512
ML_CHUNK = 256
NA_QROWS = 4
NA_KROWS = NA_QROWS + NA_ROWS - 1
POOL_HALO = max(POOL_WINDOWS) // 2
POOL_BLOCK = 128
MOE_TILE = 512
MOE_CHUNK = 16
MOE_LOCAL_ROWS = -(-(2 * TOK_TILE + N_EXPERTS * (MOE_CHUNK - 1)) // LANE) * LANE
MOE_ROWS = -(-(2 * N_TOK + (N_TOK // TOK_TILE) * N_EXPERTS * (MOE_CHUNK - 1) + N_EXPERTS * (MOE_TILE - 1))
             // MOE_TILE) * MOE_TILE
COMBINE_TILE = 256
VMEM_LIMIT = 56 * 1024 * 1024

W_A = 3 * NA_WIDTH
W_B = 3 * ML_PW
N_TCOLS = ML_PW + N_GATE_COLS
W_MAIN = W_A + W_B + POOL_WIDTH

f32 = jnp.float32
bf16 = jnp.bfloat16
HI = lax.Precision.HIGHEST


def _nt(a, b, **kw):
    return lax.dot_general(a, b, (((1,), (1,)), ((), ())), preferred_element_type=f32, **kw)


def _mod_row(i, tile):
    n_ctx_tiles = N_CTX // tile
    per_batch = DEC_SEQ // tile
    return jnp.where(i < n_ctx_tiles, 0, 1 + (i - n_ctx_tiles) // per_batch)


def _ada_kernel(c_ref, w_ref, b_ref, o_ref):
    s = c_ref[...]
    s = s * jax.nn.sigmoid(s)
    o_ref[0] = jnp.dot(s.astype(bf16), w_ref[0].astype(bf16), preferred_element_type=f32) + b_ref[0]


def _ada(cvec, w_ada, b_ada):
    nj = ADA_DIM // D_MODEL
    return pl.pallas_call(
        _ada_kernel,
        grid=(DEPTH, nj),
        in_specs=[pl.BlockSpec((8, D_MODEL), lambda l, j: (0, 0)),
                  pl.BlockSpec((1, D_MODEL, D_MODEL), lambda l, j: (l, 0, j)),
                  pl.BlockSpec((1, 1, D_MODEL), lambda l, j: (l, 0, j))],
        out_specs=pl.BlockSpec((1, 8, D_MODEL), lambda l, j: (l, 0, j)),
        out_shape=jax.ShapeDtypeStruct((DEPTH, 8, ADA_DIM), f32),
        name="ada_mod",
    )(cvec, w_ada, b_ada.reshape(DEPTH, 1, ADA_DIM))


def _in_tile(step):
    return (step + N_CTX // TOK_TILE) % (N_TOK // TOK_TILE)


def _in_kernel(xc_ref, xl_ref, *refs):
    is_ctx = _in_tile(pl.program_id(0)) < N_CTX // xc_ref.shape[0]
    _in_body(jnp.where(is_ctx, xc_ref[...], xl_ref[...]), *refs)


def _moe_in_kernel(pos0_ref, pos1_ref, ys_ref, x1_ref, rc_ref, mod_prev_ref, *refs):
    in_refs, x_out_ref, out_refs, (buf, sem) = refs[:10], refs[10], refs[11:-2], refs[-2:]
    x = _moe_residual(pos0_ref, pos1_ref, ys_ref, x1_ref, rc_ref, mod_prev_ref, buf, sem, tile_of=_in_tile)
    x_out_ref[...] = x
    _in_body(x, *in_refs, *out_refs)


def _in_body(x, mod_ref, n1_ref, w_ref, b_ref, wg_ref, bg_ref, wt_ref, bt_ref, k_in_ref, v_in_ref,
             a_ref, k_ref, v_ref, b_out_ref, kt_ref, g_ref, gt_ref, pin_ref):
    del k_in_ref, v_in_ref
    mod = mod_ref[0]
    h = x * lax.rsqrt(jnp.mean(x * x, axis=-1, keepdims=True) + EPS) * n1_ref[...]
    h = (h * (1.0 + mod[1:2]) + mod[0:1]).astype(bf16)
    pa = jnp.dot(h, w_ref[:, 0:W_A], preferred_element_type=f32) + b_ref[:, 0:W_A]
    a_ref[...] = pa.astype(bf16)
    k_ref[...] = pa[:, NA_WIDTH:2 * NA_WIDTH].reshape(k_ref.shape)
    v_ref[...] = pa[:, 2 * NA_WIDTH:W_A].reshape(v_ref.shape)
    for j in range(3):
        lo = W_A + j * ML_PW
        pb = jnp.dot(h, w_ref[:, lo:lo + ML_PW], preferred_element_type=f32) + b_ref[:, lo:lo + ML_PW]
        if j == 0:
            pb = pb * (ML_DIM ** -0.5)
        b_out_ref[:, j * ML_PW:(j + 1) * ML_PW] = pb.astype(bf16)
    lo = W_A + W_B
    pin_ref[...] = jnp.dot(h, w_ref[:, lo:lo + POOL_WIDTH], preferred_element_type=f32) + b_ref[:, lo:lo + POOL_WIDTH]
    g_ref[...] = jnp.dot(h, wg_ref[...], preferred_element_type=f32) + bg_ref[...]
    t = _nt(wt_ref[...], h) + bt_ref[...]
    kt_ref[...] = t[0:ML_PW].astype(bf16)
    gt_ref[...] = t[ML_PW:N_TCOLS]


def _in_proj_specs(layer):
    tm = TOK_TILE
    lyr = lambda shape: pl.BlockSpec((None,) + shape, lambda i, *_: (layer, 0, 0))
    rows = lambda width: pl.BlockSpec((tm, width), lambda i, *_: (_in_tile(i), 0))
    cols = lambda height: pl.BlockSpec((height, tm), lambda i, *_: (0, _in_tile(i)))
    param_specs = [pl.BlockSpec((1, 6, D_MODEL), lambda i, *_: (_mod_row(_in_tile(i), tm), 0, 0)),
                   lyr((1, D_MODEL)), lyr((D_MODEL, W_MAIN)), lyr((1, W_MAIN)), lyr((D_MODEL, LANE)), lyr((1, LANE)),
                   lyr((N_TCOLS, D_MODEL)), lyr((N_TCOLS, 1)),
                   pl.BlockSpec(memory_space=pl.ANY), pl.BlockSpec(memory_space=pl.ANY)]
    n_ctx_tiles = N_CTX // tm
    kv_spec = pl.BlockSpec((tm // SEQ, None, SEQ, NA_WIDTH),
                           lambda i, *_: (jnp.where(_in_tile(i) < n_ctx_tiles, _in_tile(i), 0), layer, 0, 0))
    kv_shape = jax.ShapeDtypeStruct((BATCH, DEPTH, SEQ, NA_WIDTH), f32)
    out_specs = [rows(W_A), kv_spec, kv_spec, rows(W_B), cols(ML_PW), rows(LANE), cols(N_GATE_COLS),
                 rows(POOL_WIDTH)]
    out_shape = [jax.ShapeDtypeStruct((N_TOK, W_A), bf16), kv_shape, kv_shape,
                 jax.ShapeDtypeStruct((N_TOK, W_B), bf16),
                 jax.ShapeDtypeStruct((ML_PW, N_TOK), bf16),
                 jax.ShapeDtypeStruct((N_TOK, LANE), f32),
                 jax.ShapeDtypeStruct((N_GATE_COLS, N_TOK), f32),
                 jax.ShapeDtypeStruct((N_TOK, POOL_WIDTH), f32)]
    return rows, param_specs, out_specs, out_shape


def _in_proj(x_ctx, x_lat, mods, layer, *params):
    tm = TOK_TILE
    n_ctx_tiles = N_CTX // tm
    rows, param_specs, out_specs, out_shape = _in_proj_specs(layer)
    return pl.pallas_call(
        _in_kernel,
        grid=(N_TOK // tm,),
        in_specs=[pl.BlockSpec((tm, D_MODEL), lambda i: (jnp.where(_in_tile(i) < n_ctx_tiles, _in_tile(i), 0), 0)),
                  pl.BlockSpec((tm, D_MODEL), lambda i: (jnp.maximum(_in_tile(i) - n_ctx_tiles, 0), 0))] + param_specs,
        out_specs=out_specs,
        out_shape=out_shape,
        input_output_aliases={2 + len(param_specs) - 2: 1, 2 + len(param_specs) - 1: 2},
        compiler_params=pltpu.CompilerParams(dimension_semantics=("arbitrary",), vmem_limit_bytes=VMEM_LIMIT),
        name="in_proj",
    )(x_ctx, x_lat, mods, *params)


def _moe_in_proj(pos0, pos1, ys, x1, rc, mods_prev, mods, layer, *params):
    tm = TOK_TILE
    rows, param_specs, out_specs, out_shape = _in_proj_specs(layer)
    return pl.pallas_call(
        _moe_in_kernel,
        grid_spec=pltpu.PrefetchScalarGridSpec(
            num_scalar_prefetch=2,
            grid=(N_TOK // tm,),
            in_specs=[pl.BlockSpec(memory_space=pl.ANY), rows(D_MODEL), rows(LANE),
                      pl.BlockSpec((1, 6, D_MODEL), lambda i, *_: (_mod_row(_in_tile(i), tm), 0, 0))] + param_specs,
            out_specs=[rows(D_MODEL)] + out_specs,
            scratch_shapes=[pltpu.VMEM((2, 2, tm, D_MODEL), f32), pltpu.SemaphoreType.DMA((2,))]),
        out_shape=[jax.ShapeDtypeStruct((N_TOK, D_MODEL), f32)] + out_shape,
        input_output_aliases={6 + len(param_specs) - 2: 2, 6 + len(param_specs) - 1: 3},
        compiler_params=pltpu.CompilerParams(dimension_semantics=("arbitrary",), vmem_limit_bytes=VMEM_LIMIT),
        name="moe_combine_in_proj",
    )(pos0, pos1, ys, x1, rc, mods_prev, mods, *params)


def _pair_attention(qp, parts):
    lane = lax.broadcasted_iota(jnp.int32, (1, LANE), 1)
    outs = []
    for j in range(2):
        in_half = (lane >= j * NA_DIM) & (lane < (j + 1) * NA_DIM)
        qm = jnp.where(in_half, qp, jnp.zeros_like(qp))
        scores = []
        for k, _, bias in parts:
            s = _nt(qm, k)
            if bias is not None:
                s = s + bias[j]
            scores.append(s)
        m = scores[0].max(axis=-1, keepdims=True)
        for s in scores[1:]:
            m = jnp.maximum(m, s.max(axis=-1, keepdims=True))
        den = None
        acc = None
        for s, (_, v, _) in zip(scores, parts):
            p = jnp.exp(s - m)
            ps = p.sum(axis=-1, keepdims=True)
            den = ps if den is None else den + ps
            o = jnp.dot(p.astype(bf16), v, preferred_element_type=f32)
            acc = o if acc is None else acc + o
        outs.append(acc / den)
    return jnp.where(lane < NA_DIM, outs[0], outs[1])


def _ctx_attn_kernel(q_ref, k_ref, v_ref, o_ref):
    for p in range(NA_PAIRS):
        sl = slice(p * LANE, (p + 1) * LANE)
        o = _pair_attention(q_ref[0, :, sl], [(k_ref[0, :, sl], v_ref[0, :, sl], None)])
        o_ref[0, :, sl] = o.astype(bf16)


def _ctx_attention(qkv):
    blk = lambda c: pl.BlockSpec((1, SEQ, NA_WIDTH), lambda b, c=c: (b, 0, c))
    return pl.pallas_call(
        _ctx_attn_kernel,
        grid=(BATCH,),
        in_specs=[blk(0), blk(1), blk(2)],
        out_specs=pl.BlockSpec((1, SEQ, NA_WIDTH), lambda b: (b, 0, 0)),
        out_shape=jax.ShapeDtypeStruct((BATCH, SEQ, NA_WIDTH), bf16),
        name="ctx_attention",
    )(qkv, qkv, qkv)


def _na_window_start(rb):
    return jnp.clip(rb * NA_QROWS - NA_ROWS // 2, 0, DEC_SEQ // GRID_W - NA_KROWS)


def _na_bias(tab_ref, head, rb):
    rows = DEC_SEQ // GRID_W
    ws = _na_window_start(rb)
    lane = lax.broadcasted_iota(jnp.int32, (1, NA_KROWS * GRID_W), 1)
    per_qrow = []
    for dq in range(NA_QROWS):
        qr = rb * NA_QROWS + dq
        a0 = ws - qr + (NA_ROWS - 1) + NA_KROWS
        tiles = [tab_ref[head, a0 + 2 * j] for j in range((NA_KROWS + 1) // 2)]
        t = jnp.concatenate(tiles, axis=1)[:, :NA_KROWS * GRID_W]
        lo = (jnp.clip(qr - NA_ROWS // 2, 0, rows - NA_ROWS) - ws) * GRID_W
        ok = (lane >= lo) & (lane < lo + NA_ROWS * GRID_W)
        per_qrow.append(jnp.where(ok, t, NEG_INF))
    return jnp.concatenate(per_qrow, axis=0)


def _na_kernel(q_ref, k_ref, v_ref, ck_ref, cv_ref, tab_ref, o_ref):
    rb = pl.program_id(1)
    start = pl.multiple_of(_na_window_start(rb) * GRID_W, GRID_W)
    nk = NA_KROWS * GRID_W
    for p in range(NA_PAIRS):
        sl = slice(p * LANE, (p + 1) * LANE)
        bias = [_na_bias(tab_ref.at[0], 2 * p + j, rb) for j in range(2)]
        parts = [(k_ref[0, pl.ds(start, nk), sl], v_ref[0, pl.ds(start, nk), sl], bias),
                 (ck_ref[0, :, sl], cv_ref[0, :, sl], None)]
        o = _pair_attention(q_ref[0, :, sl], parts)
        o_ref[0, :, sl] = o.astype(bf16)


def _neighborhood_attention(qkv, ck, cv, tables, layer):
    nq = NA_QROWS * GRID_W
    n_rb = DEC_SEQ // nq
    return pl.pallas_call(
        _na_kernel,
        grid=(DEC_BATCH, n_rb),
        in_specs=[pl.BlockSpec((1, nq, NA_WIDTH), lambda b, r: (1 + b, r, 0)),
                  pl.BlockSpec((1, DEC_SEQ, NA_WIDTH), lambda b, r: (1 + b, 0, 1)),
                  pl.BlockSpec((1, DEC_SEQ, NA_WIDTH), lambda b, r: (1 + b, 0, 2)),
                  pl.BlockSpec((1, PAST_LEN, NA_WIDTH), lambda b, r: (b, 0, 0)),
                  pl.BlockSpec((1, PAST_LEN, NA_WIDTH), lambda b, r: (b, 0, 0)),
                  pl.BlockSpec((1,) + tables.shape[1:], lambda b, r: (layer, 0, 0, 0, 0))],
        out_specs=pl.BlockSpec((1, nq, NA_WIDTH), lambda b, r: (b, r, 0)),
        out_shape=jax.ShapeDtypeStruct((DEC_BATCH, DEC_SEQ, NA_WIDTH), bf16),
        compiler_params=pltpu.CompilerParams(dimension_semantics=("arbitrary", "arbitrary"),
                                             vmem_limit_bytes=VMEM_LIMIT),
        name="neighborhood_attention",
    )(qkv, qkv, qkv, ck, cv, tables)


def _na_bias_tables(rpb):
    qc = np.arange(GRID_W)[:, None]
    kc = np.arange(GRID_W)[None, :]
    dc = np.clip(kc - qc + NA_COLS - 1, 0, RPB_COLS - 1)
    col_start = np.clip(qc - NA_COLS // 2, 0, GRID_W - NA_COLS)
    col_ok = (kc >= col_start) & (kc < col_start + NA_COLS)
    pick_col = (dc[None] == np.arange(RPB_COLS)[:, None, None]).astype(np.float32)
    rpb_pad = jnp.pad(rpb.astype(f32), ((0, 0), (0, 0), (NA_KROWS, NA_KROWS + 1), (0, 0)))
    n_a = rpb_pad.shape[2] - 1
    rows2 = jnp.stack([rpb_pad[:, :, :-1], rpb_pad[:, :, 1:]], axis=3)
    pick2 = np.zeros((2, RPB_COLS, GRID_W, 2 * GRID_W), np.float32)
    for j in range(2):
        pick2[j, :, :, j * GRID_W:(j + 1) * GRID_W] = pick_col
    tiles = jnp.einsum('lhajb,jbqc->lhaqc', rows2, pick2, precision=HI)
    a_pad = np.arange(n_a)[:, None] + np.arange(2)[None, :]
    row_ok = (a_pad >= NA_KROWS) & (a_pad < NA_KROWS + RPB_ROWS)
    ok = (row_ok[:, None, :, None] & col_ok[None, :, None, :]).reshape(n_a, GRID_W, 2 * GRID_W)
    return jnp.where(ok[None, None], tiles, NEG_INF)


def _log_sigmoid(x):
    return -(jnp.maximum(-x, 0.0) + jnp.log(1.0 + jnp.exp(-jnp.abs(x))))


def _split3(x):
    hi = x.astype(bf16)
    r1 = x - hi.astype(f32)
    mid = r1.astype(bf16)
    lo = (r1 - mid.astype(f32)).astype(bf16)
    return hi, mid, lo


def _mlstm_kernel(qf_ref, vf_ref, ktf_ref, gf_ref, gtf_ref, qb_ref, vb_ref, ktb_ref, gb_ref, gtb_ref,
                  c0_ref, m0_ref, hf_ref, hb_ref, c_out_ref, m_out_ref, c_scr, m_scr):
    L = ML_CHUNK
    seq, c, n_chunks, _ = _ml_schedule(pl.program_id(0))

    @pl.when(c == 0)
    def _():
        is_ctx = seq < BATCH
        c_scr[...] = jnp.where(is_ctx, 0.0, c0_ref[0])
        m_scr[...] = jnp.where(is_ctx, 0.0, m0_ref[0])

    ri = lax.broadcasted_iota(jnp.int32, (L, L), 0)
    ci = lax.broadcasted_iota(jnp.int32, (L, L), 1)
    lane = lax.broadcasted_iota(jnp.int32, (L, ML_PAD), 1)
    is_ncol = lane == ML_DIM
    lower = ri >= ci
    upper = ri <= ci
    lower_b = jnp.where(lower, 1.0, 0.0).astype(bf16)
    upper_b = jnp.where(upper, 1.0, 0.0).astype(bf16)
    dirs = ((qf_ref, ktf_ref, vf_ref, gf_ref, gtf_ref, hf_ref), (qb_ref, ktb_ref, vb_ref, gb_ref, gtb_ref, hb_ref))
    for d, (q_ref, kt_ref, v_ref, g_ref, gt_ref, h_ref) in enumerate(dirs):
        g = g_ref[...][:, 0:N_GATE_COLS]
        gt = gt_ref[...]
        lf_c = _log_sigmoid(g)
        lf_r = _log_sigmoid(gt)
        b_cols = sum(jnp.dot(lower_b, part, preferred_element_type=f32) for part in _split3(lf_c))
        b_rows = sum(jnp.dot(part, upper_b, preferred_element_type=f32) for part in _split3(lf_r))
        tot_c = jnp.sum(lf_c, axis=0, keepdims=True)
        tot_r = jnp.sum(lf_r, axis=1, keepdims=True)
        visible = lower
        if d == 1:
            b_cols = tot_c - b_cols + lf_c
            b_rows = tot_r - b_rows + lf_r
            visible = upper
        for hd in range(ML_HEADS):
            st = d * ML_HEADS + hd
            ci_ = 2 * ML_HEADS * d + hd
            cf_ = ci_ + ML_HEADS
            sl = slice(hd * ML_PAD, (hd + 1) * ML_PAD)
            bc = b_cols[:, cf_:cf_ + 1]
            br = b_rows[cf_:cf_ + 1, :]
            li_r = gt[ci_:ci_ + 1, :]
            m_prev = m_scr[st:st + 1, 0:1]
            dmat = jnp.where(visible, bc - br + li_r, NEG_INF)
            inter = bc + m_prev
            m_t = jnp.maximum(inter, dmat.max(axis=-1, keepdims=True))
            w_intra = jnp.exp(dmat - m_t)
            w_inter = jnp.exp(inter - m_t)
            qh = q_ref[0, :, sl]
            kht = kt_ref[sl, :]
            v_aug = jnp.where(is_ncol, jnp.ones((), bf16), v_ref[0, :, sl])
            s = (jnp.dot(qh, kht, preferred_element_type=f32) * w_intra).astype(bf16)
            c_aug = c_scr[st]
            na = (w_inter * jnp.dot(qh, c_aug.astype(bf16), preferred_element_type=f32)
                  + jnp.dot(s, v_aug, preferred_element_type=f32))
            den = na[:, ML_DIM:ML_DIM + 1]
            h_ref[0, :, sl] = jnp.where(lane < ML_DIM, na / jnp.maximum(jnp.abs(den), jnp.exp(-m_t)), 0.0)
            b_end = tot_r[cf_:cf_ + 1, :]
            g_row = b_end - br + li_r
            m_new = jnp.maximum(b_end + m_prev, g_row.max(axis=1, keepdims=True))
            decay = jnp.exp(b_end + m_prev - m_new)
            kwt = (kht.astype(f32) * jnp.exp(g_row - m_new)).astype(bf16)
            c_scr[st] = decay * c_aug + jnp.dot(kwt, v_aug, preferred_element_type=f32)
            m_scr[st:st + 1, :] = jnp.broadcast_to(m_new, (1, LANE))

    @pl.when(c == n_chunks - 1)
    def _():
        c_out_ref[0] = c_scr[...]
        m_out_ref[0] = m_scr[...]


def _ml_schedule(s):
    nc_ctx, nc_lat = SEQ // ML_CHUNK, DEC_SEQ // ML_CHUNK
    n_ctx_steps = BATCH * nc_ctx
    is_ctx = s < n_ctx_steps
    t = s - n_ctx_steps
    seq = jnp.where(is_ctx, s // nc_ctx, BATCH + t // nc_lat)
    c = jnp.where(is_ctx, s % nc_ctx, t % nc_lat)
    nc = jnp.where(is_ctx, nc_ctx, nc_lat)
    base = jnp.where(is_ctx, (s // nc_ctx) * nc_ctx, n_ctx_steps + (t // nc_lat) * nc_lat)
    return seq, c, nc, base


def _mlstm(qvo, kt, gates, gates_t, c0, m0):
    L = ML_CHUNK
    n_seq = BATCH + DEC_BATCH

    def fwd(s):
        _, c, _, base = _ml_schedule(s)
        return base + c

    def bwd(s):
        _, c, nc, base = _ml_schedule(s)
        return base + nc - 1 - c

    seq_of = lambda s: _ml_schedule(s)[0]
    lat_of = lambda s: jnp.maximum(seq_of(s) - BATCH, 0)

    def specs(pos):
        return [pl.BlockSpec((1, L, ML_PW), lambda s, j=j: (pos(s), 0, j)) for j in range(2)] + [
            pl.BlockSpec((ML_PW, L), lambda s: (0, pos(s))),
            pl.BlockSpec((L, LANE), lambda s: (pos(s), 0)),
            pl.BlockSpec((N_GATE_COLS, L), lambda s: (0, pos(s)))]

    q3 = qvo.reshape(N_TOK // L, L, W_B)
    n_str = 2 * ML_HEADS
    return pl.pallas_call(
        _mlstm_kernel,
        grid=(N_TOK // L,),
        in_specs=specs(fwd) + specs(bwd) + [
            pl.BlockSpec((1, n_str, ML_PAD, CAUG), lambda s: (lat_of(s), 0, 0, 0)),
            pl.BlockSpec((1, n_str, LANE), lambda s: (lat_of(s), 0, 0))],
        out_specs=[pl.BlockSpec((1, L, ML_PW), lambda s: (fwd(s), 0, 0)),
                   pl.BlockSpec((1, L, ML_PW), lambda s: (bwd(s), 0, 0)),
                   pl.BlockSpec((1, n_str, ML_PAD, CAUG), lambda s: (seq_of(s), 0, 0, 0)),
                   pl.BlockSpec((1, n_str, LANE), lambda s: (seq_of(s), 0, 0))],
        out_shape=[jax.ShapeDtypeStruct((N_TOK // L, L, ML_PW), f32),
                   jax.ShapeDtypeStruct((N_TOK // L, L, ML_PW), f32),
                   jax.ShapeDtypeStruct((n_seq, n_str, ML_PAD, CAUG), f32),
                   jax.ShapeDtypeStruct((n_seq, n_str, LANE), f32)],
        scratch_shapes=[pltpu.VMEM((n_str, ML_PAD, CAUG), f32), pltpu.VMEM((n_str, LANE), f32)],
        compiler_params=pltpu.CompilerParams(dimension_semantics=("arbitrary",), vmem_limit_bytes=VMEM_LIMIT),
        name="mlstm",
    )(q3, q3, kt, gates, gates_t, q3, q3, kt, gates, gates_t, c0, m0)


def _pack_ml_state(C, n, m):
    B = C.shape[0]
    c_aug = jnp.zeros((B, 2, ML_HEADS, ML_PAD, CAUG), f32)
    c_aug = c_aug.at[:, :, :, :ML_DIM, :ML_DIM].set(C.astype(f32))
    c_aug = c_aug.at[:, :, :, :ML_DIM, ML_DIM].set(n.astype(f32))
    m_b = jnp.broadcast_to(m.astype(f32)[..., None], (B, 2, ML_HEADS, LANE))
    return c_aug.reshape(B, 2 * ML_HEADS, ML_PAD, CAUG), m_b.reshape(B, 2 * ML_HEADS, LANE)


def _unpack_ml_state(c_aug, m_b):
    B = c_aug.shape[0]
    c_aug = c_aug.reshape(B, 2, ML_HEADS, ML_PAD, CAUG)
    return (c_aug[:, :, :, :ML_DIM, :ML_DIM], c_aug[:, :, :, :ML_DIM, ML_DIM],
            m_b.reshape(B, 2, ML_HEADS, LANE)[..., 0])


def _pool_rows(u_prev, u_cur, u_next, w_bd, scale, t0, seq_len):
    tm = u_cur.shape[0]
    u_win = jnp.concatenate([u_prev, u_cur, u_next], axis=0)
    u_hi = u_win.astype(bf16)
    u_lo = (u_win - u_hi.astype(f32)).astype(bf16)
    lane = lax.broadcasted_iota(jnp.int32, (1, LANE), 1)
    blocks = []
    for r0 in range(0, tm, POOL_BLOCK):
        win = slice(r0, r0 + POOL_BLOCK + 2 * POOL_HALO)
        t_abs = t0 + r0 + lax.broadcasted_iota(jnp.int32, (POOL_BLOCK, 1), 0)
        s_abs = t0 + r0 - POOL_HALO + lax.broadcasted_iota(jnp.int32, (1, POOL_BLOCK + 2 * POOL_HALO), 1)
        t_loc = t_abs & (seq_len - 1)
        seq_start = t_abs - t_loc
        means = []
        for w in POOL_WINDOWS:
            lo = jnp.maximum(t_loc - w // 2, 0)
            hi = jnp.minimum(t_loc - w // 2 + w, seq_len)
            in_win = (s_abs >= seq_start + lo) & (s_abs < seq_start + hi)
            means.append((jnp.where(in_win, 1.0, 0.0).astype(bf16), 1.0 / (hi - lo).astype(f32)))
        pooled = []
        for p in range(POOL_GROUPS // 2):
            sl = slice(p * LANE, (p + 1) * LANE)
            halves = []
            for a, inv_cnt in means[2 * p:2 * p + 2]:
                tot = (jnp.dot(a, u_hi[win, sl], preferred_element_type=f32)
                       + jnp.dot(a, u_lo[win, sl], preferred_element_type=f32))
                halves.append(tot * inv_cnt)
            pooled.append(jnp.where(lane < POOL_DIM, halves[0], halves[1]) - u_cur[r0:r0 + POOL_BLOCK, sl])
        blocks.append(jnp.concatenate(pooled, axis=1))
    pooled = jnp.concatenate(blocks, axis=0).astype(bf16)
    return jnp.dot(pooled, w_bd, preferred_element_type=f32) * scale


def _top2_sum(a, b, c, d):
    hi1, lo1 = jnp.maximum(a, b), jnp.minimum(a, b)
    hi2, lo2 = jnp.maximum(c, d), jnp.minimum(c, d)
    return jnp.maximum(hi1, hi2) + jnp.maximum(jnp.minimum(hi1, hi2), jnp.maximum(lo1, lo2))


def _first_match(vals, target):
    idx = jnp.full_like(target, float(len(vals) - 1))
    for i in range(len(vals) - 2, -1, -1):
        idx = jnp.where(vals[i] == target, float(i), idx)
    return idx


def _pick(vals, idx):
    out = vals[-1]
    for i in range(len(vals) - 2, -1, -1):
        out = jnp.where(idx == float(i), vals[i], out)
    return out


def _route(logits_t, bias_t):
    scores = jax.nn.sigmoid(logits_t)
    sel = scores + bias_t
    row = lambda a, i: a[i:i + 1, :]
    grp = [_top2_sum(*[row(sel, EXPERTS_PER_GROUP * g + i) for i in range(EXPERTS_PER_GROUP)])
           for g in range(N_EXPERT_GROUPS)]
    best = functools.reduce(jnp.maximum, grp)
    gidx = _first_match(grp, best)
    sel_g = [_pick([row(sel, EXPERTS_PER_GROUP * g + i) for g in range(N_EXPERT_GROUPS)], gidx)
             for i in range(EXPERTS_PER_GROUP)]
    sco_g = [_pick([row(scores, EXPERTS_PER_GROUP * g + i) for g in range(N_EXPERT_GROUPS)], gidx)
             for i in range(EXPERTS_PER_GROUP)]
    i0 = _first_match(sel_g, functools.reduce(jnp.maximum, sel_g))
    rest = [jnp.where(i0 == float(i), -jnp.inf, sel_g[i]) for i in range(EXPERTS_PER_GROUP)]
    i1 = _first_match(rest, functools.reduce(jnp.maximum, rest))
    s0, s1 = _pick(sco_g, i0), _pick(sco_g, i1)
    tot = s0 + s1
    rid = lax.broadcasted_iota(jnp.int32, (LANE, logits_t.shape[1]), 0)
    rows = (EXPERTS_PER_GROUP * gidx + i0, EXPERTS_PER_GROUP * gidx + i1, s0 / tot, s1 / tot)
    out = jnp.zeros(rid.shape, f32)
    for i, r in enumerate(rows):
        out = jnp.where(rid == i, r, out)
    return out


def _out_kernel(xc_ref, xl_ref, mod_ref, oac_ref, oal_ref, hf_ref, hb_ref, ob_ref, up_ref, uc_ref, un_ref, wp_ref, psc_ref,
                mln_ref, wo_ref, n2_ref, wr_ref, br_ref, x1_ref, h2_ref, rt_ref, rc_ref):
    tm = xc_ref.shape[0]
    i = pl.program_id(0)
    is_ctx = i < N_CTX // tm
    mod = mod_ref[0]
    out_a = jnp.where(is_ctx, oac_ref[...], oal_ref[...])
    out_c = _pool_rows(up_ref[...], uc_ref[...], un_ref[...], wp_ref[...], psc_ref[...], i * tm,
                       jnp.where(is_ctx, SEQ, DEC_SEQ)).astype(bf16)
    hsum = hf_ref[...] + hb_ref[...]
    outs_b = []
    for hd in range(ML_HEADS):
        sl = slice(hd * ML_PAD, (hd + 1) * ML_PAD)
        hh = hsum[:, sl]
        ms = jnp.sum(hh * hh, axis=-1, keepdims=True) * (1.0 / ML_DIM)
        hn = hh * lax.rsqrt(ms + EPS) * mln_ref[:, sl]
        outs_b.append((jax.nn.sigmoid(ob_ref[:, sl].astype(f32)) * hn).astype(bf16))
    out_b = jnp.concatenate(outs_b, axis=1)
    mixed = (jnp.dot(out_a, wo_ref[0:NA_WIDTH, :], preferred_element_type=f32)
             + jnp.dot(out_b, wo_ref[NA_WIDTH:NA_WIDTH + ML_PW, :], preferred_element_type=f32)
             + jnp.dot(out_c, wo_ref[NA_WIDTH + ML_PW:, :], preferred_element_type=f32))
    x1 = jnp.where(is_ctx, xc_ref[...], xl_ref[...]) + mod[2:3] * mixed
    x1_ref[...] = x1
    h2 = x1 * lax.rsqrt(jnp.mean(x1 * x1, axis=-1, keepdims=True) + EPS) * n2_ref[...]
    h2 = h2 * (1.0 + mod[4:5]) + mod[3:4]
    h2_ref[...] = h2.astype(bf16)
    h_hi = h2.astype(bf16)
    h_lo = (h2 - h_hi.astype(f32)).astype(bf16)
    w_hi = wr_ref[...].astype(bf16)
    w_lo = (wr_ref[...] - w_hi.astype(f32)).astype(bf16)
    route_t = _route(_nt(w_hi, h_hi) + (_nt(w_hi, h_lo) + _nt(w_lo, h_hi)), br_ref[...])
    rt_ref[...] = route_t[0:8]
    rc_ref[...] = route_t.T


def _out_proj(x_ctx, x_lat, x_lat_block0, mods, layer, oa_ctx, oa_lat, hf, hb, qvo, pin, w_bd, psc, mln, wo, n2,
              wr_t, br_t):
    tm = TOK_TILE
    const = lambda i: (0, 0)
    lyr = lambda shape: pl.BlockSpec((None,) + shape, lambda i: (layer, 0, 0))
    row = lambda i: (i, 0)
    n_ctx_tiles = N_CTX // tm
    halo_blocks = tm // POOL_HALO
    return pl.pallas_call(
        _out_kernel,
        grid=(N_TOK // tm,),
        in_specs=[pl.BlockSpec((tm, D_MODEL), lambda i: (jnp.minimum(i, n_ctx_tiles - 1), 0)),
                  pl.BlockSpec((tm, D_MODEL), lambda i: (jnp.maximum(i - n_ctx_tiles, 0) + x_lat_block0, 0)),
                  pl.BlockSpec((1, 6, D_MODEL), lambda i: (_mod_row(i, tm), 0, 0)),
                  pl.BlockSpec((tm, NA_WIDTH), lambda i: (jnp.minimum(i, n_ctx_tiles - 1), 0)),
                  pl.BlockSpec((tm, NA_WIDTH), lambda i: (jnp.maximum(i - n_ctx_tiles, 0), 0)),
                  pl.BlockSpec((tm, ML_PW), row),
                  pl.BlockSpec((tm, ML_PW), row),
                  pl.BlockSpec((tm, ML_PW), lambda i: (i, 2)),
                  pl.BlockSpec((POOL_HALO, POOL_WIDTH), lambda i: (jnp.maximum(i * halo_blocks - 1, 0), 0)),
                  pl.BlockSpec((tm, POOL_WIDTH), row),
                  pl.BlockSpec((POOL_HALO, POOL_WIDTH),
                               lambda i: (jnp.minimum((i + 1) * halo_blocks, N_TOK // POOL_HALO - 1), 0)),
                  lyr((POOL_WIDTH, POOL_WIDTH)), lyr((1, POOL_WIDTH)), lyr((1, ML_PW)),
                  lyr((NA_WIDTH + ML_PW + POOL_WIDTH, D_MODEL)), lyr((1, D_MODEL)),
                  pl.BlockSpec((N_EXPERTS, D_MODEL), const),
                  pl.BlockSpec((N_EXPERTS, 1), const)],
        out_specs=[pl.BlockSpec((tm, D_MODEL), row),
                   pl.BlockSpec((tm, D_MODEL), row),
                   pl.BlockSpec((8, tm), lambda i: (0, i)),
                   pl.BlockSpec((tm, LANE), row)],
        out_shape=[jax.ShapeDtypeStruct((N_TOK, D_MODEL), f32),
                   jax.ShapeDtypeStruct((N_TOK, D_MODEL), bf16),
                   jax.ShapeDtypeStruct((8, N_TOK), f32),
                   jax.ShapeDtypeStruct((N_TOK, LANE), f32)],
        compiler_params=pltpu.CompilerParams(dimension_semantics=("arbitrary",), vmem_limit_bytes=VMEM_LIMIT),
        name="out_proj_router",
    )(x_ctx, x_lat, mods, oa_ctx, oa_lat, hf, hb, qvo, pin, pin, pin, w_bd, psc, mln, wo, n2, wr_t, br_t)


def _ceil_to(x, m):
    return jnp.floor((x + (m - 1)) * (1.0 / m)) * m


def _prefix_over_experts(v):
    er = lax.broadcasted_iota(jnp.int32, (N_EXPERTS, N_EXPERTS), 0)
    ec = lax.broadcasted_iota(jnp.int32, (N_EXPERTS, N_EXPERTS), 1)
    return jnp.dot(jnp.where(ec < er, 1.0, 0.0), v, preferred_element_type=f32, precision=HI)


def _experts_to_lanes(v):
    sub = lax.broadcasted_iota(jnp.int32, (N_EXPERTS, LANE), 0)
    lane = lax.broadcasted_iota(jnp.int32, (N_EXPERTS, LANE), 1)
    return jnp.sum(jnp.where(sub == lane, v, 0.0), axis=0, keepdims=True)


def _expert_hits(rt):
    rid = lax.broadcasted_iota(jnp.int32, (N_EXPERTS, rt.shape[1]), 0).astype(f32)
    oh0 = rid == rt[0:1, :]
    oh1 = rid == rt[1:2, :]
    both = jnp.where(oh0 | oh1, 1.0, 0.0)
    runs = jnp.broadcast_to(_ceil_to(jnp.sum(both, axis=1, keepdims=True), MOE_CHUNK), (N_EXPERTS, LANE))
    return oh0, oh1, both, runs


def _rank_kernel(rt_all_ref, rt_ref, pos_ref, te_ref, tab_ref, carry_ref):
    tm = rt_ref.shape[1]
    step = pl.program_id(0)

    @pl.when(step == 0)
    def _():
        totals = jnp.zeros((N_EXPERTS, LANE), f32)
        for i in range(rt_all_ref.shape[1] // tm):
            totals = totals + _expert_hits(rt_all_ref[:, i * tm:(i + 1) * tm])[3]
        padded = _ceil_to(totals, MOE_TILE)
        off = _prefix_over_experts(padded)
        carry_ref[...] = off
        total = jnp.sum(padded, axis=0, keepdims=True)
        n_used = total * (1.0 / MOE_TILE)
        tile = lax.broadcasted_iota(jnp.int32, (1, LANE), 1).astype(f32)
        row0 = jnp.minimum(tile, n_used - 1.0) * MOE_TILE
        expert = jnp.sum(jnp.where(off <= row0, 1.0, 0.0), axis=0, keepdims=True) - 1.0
        sub = lax.broadcasted_iota(jnp.int32, (8, LANE), 0)
        te_ref[...] = jnp.where(sub == 0, expert, jnp.where(sub == 1, n_used, 0.0)).astype(jnp.int32)

    @pl.when(step > 0)
    def _():
        oh0, oh1, both, runs = _expert_hits(rt_ref[...])
        sr = lax.broadcasted_iota(jnp.int32, (tm, tm), 0)
        sc = lax.broadcasted_iota(jnp.int32, (tm, tm), 1)
        earlier = jnp.dot(both.astype(bf16), jnp.where(sr < sc, 1.0, 0.0).astype(bf16),
                          preferred_element_type=f32)
        g_off = carry_ref[...]
        l_off = _prefix_over_experts(runs)
        g_row = g_off[:, 0:1] + earlier
        l_row = l_off[:, 0:1] + earlier
        pick = lambda oh, v: jnp.sum(jnp.where(oh, v, 0.0), axis=0, keepdims=True)
        rows = (pick(oh0, g_row), pick(oh1, g_row), pick(oh0, l_row), pick(oh1, l_row))
        sub = lax.broadcasted_iota(jnp.int32, (8, tm), 0)
        out = jnp.zeros((8, tm), f32)
        for k, r in enumerate(rows):
            out = jnp.where(sub == k, r, out)
        pos_ref[...] = out.astype(jnp.int32)
        sub = lax.broadcasted_iota(jnp.int32, (8, LANE), 0)
        tab = jnp.zeros((8, LANE), f32)
        for k, v in enumerate((runs * (1.0 / MOE_CHUNK), l_off, g_off)):
            tab = jnp.where(sub == k, _experts_to_lanes(v), tab)
        tab_ref[0] = tab.astype(jnp.int32)
        carry_ref[...] = g_off + runs


def _rank(route_t):
    tm = TOK_TILE
    n_tiles = N_TOK // tm
    tile_of = lambda s: jnp.maximum(s - 1, 0)
    return pl.pallas_call(
        _rank_kernel,
        grid=(1 + n_tiles,),
        in_specs=[pl.BlockSpec((8, N_TOK), lambda s: (0, 0)),
                  pl.BlockSpec((8, tm), lambda s: (0, tile_of(s)))],
        out_specs=[pl.BlockSpec((8, tm), lambda s: (0, tile_of(s))),
                   pl.BlockSpec((8, LANE), lambda s: (0, 0)),
                   pl.BlockSpec((1, 8, LANE), lambda s: (tile_of(s), 0, 0))],
        out_shape=[jax.ShapeDtypeStruct((8, N_TOK), jnp.int32),
                   jax.ShapeDtypeStruct((8, LANE), jnp.int32),
                   jax.ShapeDtypeStruct((n_tiles, 8, LANE), jnp.int32)],
        scratch_shapes=[pltpu.VMEM((N_EXPERTS, LANE), f32)],
        compiler_params=pltpu.CompilerParams(dimension_semantics=("arbitrary",)),
        name="moe_rank",
    )(route_t, route_t)


def _dispatch_kernel(tab_ref, h_ref, rows_ref, xs_in_ref, xs_ref, loc, sem):
    del xs_in_ref
    tm = h_ref.shape[0]
    i = pl.program_id(0)
    slot = i % 2
    rid = lax.broadcasted_iota(jnp.int32, (MOE_LOCAL_ROWS, tm), 0)
    sel = (rid == rows_ref[2:3, :]) | (rid == rows_ref[3:4, :])
    loc[slot] = jnp.dot(jnp.where(sel, 1.0, 0.0).astype(bf16), h_ref[...], preferred_element_type=f32).astype(bf16)

    def chunk_copy(sl, src_row, dst_row):
        return pltpu.make_async_copy(loc.at[sl, pl.ds(pl.multiple_of(src_row, MOE_CHUNK), MOE_CHUNK)],
                                     xs_ref.at[pl.ds(pl.multiple_of(dst_row, MOE_CHUNK), MOE_CHUNK)], sem.at[sl])

    def chunks_of(tile):
        return sum(tab_ref[3 * tile * N_EXPERTS + e] for e in range(N_EXPERTS))

    def wait_chunks(sl, n):
        def wait_one(c, carry):
            chunk_copy(sl, 0, 0).wait()
            return carry

        lax.fori_loop(0, n, wait_one, 0)

    for e in range(N_EXPERTS):
        n_chunks, l_off, g_off = (tab_ref[(3 * i + k) * N_EXPERTS + e] for k in range(3))

        def issue(c, carry, l_off=l_off, g_off=g_off):
            chunk_copy(slot, l_off + c * MOE_CHUNK, g_off + c * MOE_CHUNK).start()
            return carry

        lax.fori_loop(0, n_chunks, issue, 0)

    @pl.when(i > 0)
    def _():
        wait_chunks(1 - slot, chunks_of(i - 1))

    @pl.when(i == pl.num_programs(0) - 1)
    def _():
        wait_chunks(slot, chunks_of(i))


def _dispatch(run_table, h2, rows, xs_init):
    tm = TOK_TILE
    return pl.pallas_call(
        _dispatch_kernel,
        grid_spec=pltpu.PrefetchScalarGridSpec(
            num_scalar_prefetch=1,
            grid=(N_TOK // tm,),
            in_specs=[pl.BlockSpec((tm, D_MODEL), lambda i, tab: (i, 0)),
                      pl.BlockSpec((8, tm), lambda i, tab: (0, i)),
                      pl.BlockSpec(memory_space=pl.ANY)],
            out_specs=pl.BlockSpec(memory_space=pl.ANY),
            scratch_shapes=[pltpu.VMEM((2, MOE_LOCAL_ROWS, D_MODEL), bf16), pltpu.SemaphoreType.DMA((2,))]),
        out_shape=jax.ShapeDtypeStruct(xs_init.shape, xs_init.dtype),
        input_output_aliases={3: 0},
        compiler_params=pltpu.CompilerParams(dimension_semantics=("arbitrary",), vmem_limit_bytes=VMEM_LIMIT),
        name="moe_dispatch",
    )(run_table, h2, rows, xs_init)


def _expert_kernel(te_ref, xs_ref, wg_hbm, wu_hbm, wd_hbm, ys_ref, wg_f32, wu_f32, wd_f32, wg_bf, wu_bf, wd_bf,
                   slot_ref, sem, *, layer):
    j = pl.program_id(0)
    n_used = te_ref[1, 0]
    used = j < n_used
    expert = te_ref[0, j]
    new_expert = jnp.logical_or(j == 0, expert != te_ref[0, jnp.maximum(j - 1, 0)])

    def weight_copies(e, slot):
        return [pltpu.make_async_copy(hbm.at[layer, e], buf.at[slot], sem.at[slot])
                for hbm, buf in ((wg_hbm, wg_f32), (wu_hbm, wu_f32), (wd_hbm, wd_f32))]

    @pl.when(j == 0)
    def _():
        slot_ref[0] = 1
        for cp in weight_copies(expert, 0):
            cp.start()

    @pl.when(jnp.logical_not(used))
    def _():
        ys_ref[...] = jnp.zeros_like(ys_ref)

    @pl.when(used & new_expert)
    def _():
        slot = 1 - slot_ref[0]
        slot_ref[0] = slot
        for cp in weight_copies(expert, slot):
            cp.wait()
        wg_bf[...] = wg_f32[slot].astype(bf16)
        wu_bf[...] = wu_f32[slot].astype(bf16)
        wd_bf[...] = wd_f32[slot].astype(bf16)
        nxt = lax.while_loop(lambda t: (t < n_used) & (te_ref[0, jnp.minimum(t, LANE - 1)] == expert),
                             lambda t: t + 1, j + 1)

        @pl.when(nxt < n_used)
        def _():
            for cp in weight_copies(te_ref[0, nxt], 1 - slot):
                cp.start()

    @pl.when(used)
    def _():
        x = xs_ref[...]
        hg = jnp.dot(x, wg_bf[...], preferred_element_type=f32)
        hu = jnp.dot(x, wu_bf[...], preferred_element_type=f32)
        hid = (hg * jax.nn.sigmoid(hg) * hu).astype(bf16)
        ys_ref[...] = jnp.dot(hid, wd_bf[...], preferred_element_type=f32)


def _experts(te, xs, w_gate, w_up, w_down, layer):
    tm = MOE_TILE
    row = lambda j, te: (jnp.minimum(j, te[1, 0] - 1), 0)
    hbm = pl.BlockSpec(memory_space=pl.ANY)
    return pl.pallas_call(
        functools.partial(_expert_kernel, layer=layer),
        grid_spec=pltpu.PrefetchScalarGridSpec(
            num_scalar_prefetch=1,
            grid=(MOE_ROWS // tm,),
            in_specs=[pl.BlockSpec((tm, D_MODEL), row), hbm, hbm, hbm],
            out_specs=pl.BlockSpec((tm, D_MODEL), lambda j, te: (j, 0)),
            scratch_shapes=[pltpu.VMEM((2, D_MODEL, D_EXPERT), f32), pltpu.VMEM((2, D_MODEL, D_EXPERT), f32),
                            pltpu.VMEM((2, D_EXPERT, D_MODEL), f32),
                            pltpu.VMEM((D_MODEL, D_EXPERT), bf16), pltpu.VMEM((D_MODEL, D_EXPERT), bf16),
                            pltpu.VMEM((D_EXPERT, D_MODEL), bf16),
                            pltpu.SMEM((1,), jnp.int32), pltpu.SemaphoreType.DMA((2,))]),
        out_shape=jax.ShapeDtypeStruct((MOE_ROWS, D_MODEL), f32),
        compiler_params=pltpu.CompilerParams(dimension_semantics=("arbitrary",), vmem_limit_bytes=VMEM_LIMIT),
        name="moe_experts",
    )(te, xs, w_gate, w_up, w_down)


def _gather_expert_rows(pos0_ref, pos1_ref, ys_ref, buf, sem, tile_of):
    rows = buf.shape[2]
    i = pl.program_id(0)
    slot = i % 2

    def issue(tile, sl):
        base = tile * rows

        def body(t, carry):
            for s, pos_ref in enumerate((pos0_ref, pos1_ref)):
                pltpu.make_async_copy(ys_ref.at[pl.ds(pos_ref[base + t], 1)], buf.at[sl, s, pl.ds(t, 1)],
                                      sem.at[sl]).start()
            return carry

        lax.fori_loop(0, rows, body, 0, unroll=8)

    @pl.when(i == 0)
    def _():
        issue(tile_of(0), 0)

    @pl.when(i + 1 < pl.num_programs(0))
    def _():
        issue(tile_of(i + 1), 1 - slot)

    for s in range(2):
        pltpu.make_async_copy(ys_ref.at[pl.ds(0, rows)], buf.at[slot, s], sem.at[slot]).wait()
    return buf[slot, 0], buf[slot, 1]


def _moe_residual(pos0_ref, pos1_ref, ys_ref, x1_ref, rc_ref, mod_ref, buf, sem, tile_of=lambda step: step):
    y0, y1 = _gather_expert_rows(pos0_ref, pos1_ref, ys_ref, buf, sem, tile_of)
    rc = rc_ref[...]
    return x1_ref[...] + mod_ref[0][5:6] * (rc[:, 2:3] * y0 + rc[:, 3:4] * y1)


def _final_kernel(pos0_ref, pos1_ref, ys_ref, x1_ref, rc_ref, mod_ref, fn_ref, yc_ref, yl_ref, buf, sem):
    x2 = _moe_residual(pos0_ref, pos1_ref, ys_ref, x1_ref, rc_ref, mod_ref, buf, sem)
    y = x2 * lax.rsqrt(jnp.mean(x2 * x2, axis=-1, keepdims=True) + EPS) * fn_ref[...]
    is_ctx = pl.program_id(0) < N_CTX // x1_ref.shape[0]

    @pl.when(is_ctx)
    def _():
        yc_ref[...] = y

    @pl.when(jnp.logical_not(is_ctx))
    def _():
        yl_ref[...] = y


def _final_combine(pos0, pos1, ys, x1, rc, mods, fn):
    tc = COMBINE_TILE
    row = lambda i, p0, p1: (i, 0)
    n_ctx_tiles = N_CTX // tc
    return pl.pallas_call(
        _final_kernel,
        grid_spec=pltpu.PrefetchScalarGridSpec(
            num_scalar_prefetch=2,
            grid=(N_TOK // tc,),
            in_specs=[pl.BlockSpec(memory_space=pl.ANY),
                      pl.BlockSpec((tc, D_MODEL), row),
                      pl.BlockSpec((tc, LANE), row),
                      pl.BlockSpec((1, 6, D_MODEL), lambda i, p0, p1: (_mod_row(i, tc), 0, 0)),
                      pl.BlockSpec((1, D_MODEL), lambda i, p0, p1: (0, 0))],
            out_specs=[pl.BlockSpec((tc, D_MODEL), lambda i, p0, p1: (jnp.minimum(i, n_ctx_tiles - 1), 0)),
                       pl.BlockSpec((tc, D_MODEL), lambda i, p0, p1: (jnp.maximum(i - n_ctx_tiles, 0), 0))],
            scratch_shapes=[pltpu.VMEM((2, 2, tc, D_MODEL), f32), pltpu.SemaphoreType.DMA((2,))]),
        out_shape=[jax.ShapeDtypeStruct((N_CTX, D_MODEL), f32), jax.ShapeDtypeStruct((N_LAT, D_MODEL), f32)],
        compiler_params=pltpu.CompilerParams(dimension_semantics=("arbitrary",), vmem_limit_bytes=VMEM_LIMIT),
        name="moe_combine_final",
    )(pos0, pos1, ys, x1, rc, mods, fn)


def _moe_experts(h2, route_t, w_gate, w_up, w_down, layer, xs_buf):
    rows, te, runs = _rank(route_t)
    xs = _dispatch(runs[:, :3, :N_EXPERTS].reshape(-1), h2, rows, xs_buf)
    return rows[0], rows[1], _experts(te, xs, w_gate, w_up, w_down, layer), xs


def _pad_heads(w):
    lead = w.shape[:-1]
    w = w.reshape(*lead, ML_HEADS, ML_DIM)
    w = jnp.pad(w, [(0, 0)] * len(lead) + [(0, 0), (0, ML_PAD - ML_DIM)])
    return w.reshape(*lead, ML_PW)


def _pack_in_cols(wb):
    o = 0
    qa = wb[..., o:o + NA_WIDTH] * (NA_DIM ** -0.5)
    ka = wb[..., o + NA_WIDTH:o + 2 * NA_WIDTH]
    va = wb[..., o + 2 * NA_WIDTH:o + 3 * NA_WIDTH]
    o += 3 * NA_WIDTH
    qb, kb, vb, ob = [_pad_heads(wb[..., o + j * ML_WIDTH:o + (j + 1) * ML_WIDTH]) for j in range(4)]
    o += 4 * ML_WIDTH
    gates = wb[..., o:o + N_GATE_COLS]
    o += N_GATE_COLS
    pool = wb[..., o:o + POOL_WIDTH]
    main = jnp.concatenate([qa, ka, va, qb, vb, ob, pool], axis=-1)
    gates_p = jnp.pad(gates, [(0, 0)] * (gates.ndim - 1) + [(0, LANE - N_GATE_COLS)])
    return main, gates_p, jnp.concatenate([kb, gates], axis=-1)


def _pack_w_in(w, b):
    w_main, w_gates, w_feat = _pack_in_cols(w)
    b_main, b_gates, b_feat = _pack_in_cols(b.astype(f32))
    return (w_main.astype(bf16), b_main[:, None], w_gates.astype(bf16), b_gates[:, None],
            jnp.swapaxes(w_feat, 1, 2).astype(bf16), b_feat[:, :, None])


def _pack_w_out(w):
    n_l = w.shape[0]
    wb = w[:, NA_WIDTH:NA_WIDTH + ML_WIDTH].reshape(n_l, ML_HEADS, ML_DIM, D_MODEL)
    wb = jnp.pad(wb, ((0, 0), (0, 0), (0, ML_PAD - ML_DIM), (0, 0))).reshape(n_l, ML_PW, D_MODEL)
    return jnp.concatenate([w[:, :NA_WIDTH], wb, w[:, NA_WIDTH + ML_WIDTH:]], axis=1).astype(bf16)


def _block_diag(w):
    n_l, g, c, _ = w.shape
    eye = jnp.eye(g, dtype=w.dtype)
    return (eye[None, :, None, :, None] * w[:, :, :, None, :]).reshape(n_l, g * c, g * c)


def kernel(x_prompt, x_sample, cache_k_attn, cache_v_attn, state_mlstm_C, state_mlstm_n, state_mlstm_m, c, c_ctx,
           w_ada, b_ada, norm1, w_in, b_in, rpb, ml_norm, w_pool, pool_scale, w_out, norm2, w_router, b_router,
           w_gate, w_up, w_down, final_norm):
    dt = x_prompt.dtype
    x_ctx = x_prompt.reshape(N_CTX, D_MODEL).astype(f32)
    x_lat = x_sample.reshape(N_LAT, D_MODEL).astype(f32)
    x_lat_block0 = 0
    cvec = jnp.concatenate([c_ctx[None], c, jnp.zeros((8 - 1 - DEC_BATCH, D_MODEL), c.dtype)], axis=0).astype(f32)
    mods_all = _ada(cvec, w_ada.astype(f32), b_ada.astype(f32))
    mods_all = mods_all[:, :1 + DEC_BATCH].reshape(DEPTH, 1 + DEC_BATCH, 6, D_MODEL)

    wr_t = w_router.astype(f32).T
    br_t = b_router.astype(f32)[:, None]
    fn = final_norm.astype(f32)[None]

    na_bias = _na_bias_tables(rpb)
    xs_buf = jnp.zeros((MOE_ROWS, D_MODEL), bf16)
    in_params = (norm1.astype(f32)[:, None],) + _pack_w_in(w_in, b_in)
    out_params = (_block_diag(w_pool.astype(f32)).astype(bf16), pool_scale.astype(f32)[:, None],
                  _pad_heads(ml_norm.astype(f32))[:, None], _pack_w_out(w_out), norm2.astype(f32)[:, None])

    new_k = jnp.zeros((BATCH, DEPTH, SEQ, NA_WIDTH), f32)
    new_v = jnp.zeros_like(new_k)
    Cs, ns, ms = [], [], []
    pending = None
    for l in range(DEPTH):
        mods = mods_all[l]
        if pending is None:
            qkva, new_k, new_v, qvo, kt, gates, gates_t, pin = _in_proj(x_ctx, x_lat, mods, l, *in_params,
                                                                        new_k, new_v)
        else:
            x, qkva, new_k, new_v, qvo, kt, gates, gates_t, pin = _moe_in_proj(
                *pending, mods_all[l - 1], mods, l, *in_params, new_k, new_v)
            x_ctx, x_lat, x_lat_block0 = x, x, N_CTX // TOK_TILE

        oa_ctx = _ctx_attention(qkva.reshape(N_TOK // SEQ, SEQ, W_A))
        ck = (cache_k_attn[:, l].reshape(DEC_BATCH, PAST_LEN, NA_WIDTH)).astype(bf16)
        cv = (cache_v_attn[:, l].reshape(DEC_BATCH, PAST_LEN, NA_WIDTH)).astype(bf16)
        oa_lat = _neighborhood_attention(qkva.reshape(N_TOK // DEC_SEQ, DEC_SEQ, W_A), ck, cv, na_bias, l)

        c_l, m_l = _pack_ml_state(state_mlstm_C[:, l], state_mlstm_n[:, l], state_mlstm_m[:, l])
        hf, hb, c_fin, m_fin = _mlstm(qvo, kt, gates, gates_t, c_l, m_l)
        C_l, n_l, m_l2 = _unpack_ml_state(c_fin[:BATCH], m_fin[:BATCH])
        Cs.append(C_l)
        ns.append(n_l)
        ms.append(m_l2)

        x1, h2, route_t, rc = _out_proj(x_ctx, x_lat, x_lat_block0, mods, l,
                                        oa_ctx.reshape(N_CTX, NA_WIDTH), oa_lat.reshape(N_LAT, NA_WIDTH),
                                        hf.reshape(N_TOK, ML_PW), hb.reshape(N_TOK, ML_PW), qvo, pin,
                                        *out_params, wr_t, br_t)
        pos0, pos1, ys, xs_buf = _moe_experts(h2, route_t, w_gate, w_up, w_down, l, xs_buf)
        pending = (pos0, pos1, ys, x1, rc)

    x = _final_combine(*pending, mods_all[DEPTH - 1], fn)
    y_prompt = x[0].reshape(BATCH, SEQ, D_MODEL).astype(dt)
    y_sample = x[1].reshape(DEC_BATCH, DEC_SEQ, D_MODEL).astype(dt)
    new_k, new_v = (a.reshape(BATCH, DEPTH, SEQ, NA_HEADS, NA_DIM).astype(dt) for a in (new_k, new_v))
    return (y_prompt, y_sample, new_k, new_v,
            jnp.stack(Cs, axis=1).astype(dt), jnp.stack(ns, axis=1).astype(dt), jnp.stack(ms, axis=1).astype(dt))
```

```python
import functools

import numpy as np
import jax
import jax.numpy as jnp
from jax import lax
from jax.experimental import pallas as pl
from jax.experimental.pallas import tpu as pltpu

D_MODEL = 1024
BATCH = 16
SEQ = 256
DEPTH = 4
DEC_BATCH = 2
DEC_SEQ = 4096
PAST_LEN = 256
GRID_W = 64
EPS = 1e-6
NEG_INF = -1e30
NA_HEADS = 6
NA_DIM = 64
NA_WIDTH = NA_HEADS * NA_DIM
NA_ROWS = 8
NA_COLS = 16
RPB_ROWS = 2 * NA_ROWS - 1
RPB_COLS = 2 * NA_COLS - 1
ML_HEADS = 4
ML_DIM = 96
ML_WIDTH = ML_HEADS * ML_DIM
POOL_WINDOWS = (2, 4, 8, 16)
POOL_GROUPS = 4
POOL_DIM = 64
POOL_WIDTH = POOL_GROUPS * POOL_DIM
N_GATE_COLS = 4 * ML_HEADS
N_EXPERTS = 16
N_EXPERT_GROUPS = 4
EXPERTS_PER_GROUP = N_EXPERTS // N_EXPERT_GROUPS
D_EXPERT = 512
ADA_DIM = 6 * D_MODEL

N_CTX = BATCH * SEQ
N_LAT = DEC_BATCH * DEC_SEQ
N_TOK = N_CTX + N_LAT
LANE = 128
ML_PAD = LANE
ML_PW = ML_HEADS * ML_PAD
CAUG = ML_PAD
NA_PAIRS = NA_HEADS // 2
TOK_TILE = 512
ML_CHUNK = 256
NA_QROWS = 4
NA_KROWS = NA_QROWS + NA_ROWS - 1
POOL_HALO = max(POOL_WINDOWS) // 2
POOL_BLOCK = 128
MOE_TILE = 512
MOE_CHUNK = 16
MOE_LOCAL_ROWS = -(-(2 * TOK_TILE + N_EXPERTS * (MOE_CHUNK - 1)) // LANE) * LANE
MOE_ROWS = -(-(2 * N_TOK + (N_TOK // TOK_TILE) * N_EXPERTS * (MOE_CHUNK - 1) + N_EXPERTS * (MOE_TILE - 1))
             // MOE_TILE) * MOE_TILE
VMEM_LIMIT = 56 * 1024 * 1024

W_A = 3 * NA_WIDTH
W_B = 3 * ML_PW
N_TCOLS = ML_PW + N_GATE_COLS
W_MAIN = W_A + W_B + POOL_WIDTH

f32 = jnp.float32
bf16 = jnp.bfloat16
HI = lax.Precision.HIGHEST


def _nt(a, b, **kw):
    return lax.dot_general(a, b, (((1,), (1,)), ((), ())), preferred_element_type=f32, **kw)


def _mod_row(i, tile):
    n_ctx_tiles = N_CTX // tile
    per_batch = DEC_SEQ // tile
    return jnp.where(i < n_ctx_tiles, 0, 1 + (i - n_ctx_tiles) // per_batch)


def _ada_kernel(c_ref, w_ref, b_ref, o_ref):
    s = c_ref[...]
    s = s * jax.nn.sigmoid(s)
    o_ref[0] = jnp.dot(s.astype(bf16), w_ref[0].astype(bf16), preferred_element_type=f32) + b_ref[0]


def _ada(cvec, w_ada, b_ada):
    nj = ADA_DIM // D_MODEL
    return pl.pallas_call(
        _ada_kernel,
        grid=(DEPTH, nj),
        in_specs=[pl.BlockSpec((8, D_MODEL), lambda l, j: (0, 0)),
                  pl.BlockSpec((1, D_MODEL, D_MODEL), lambda l, j: (l, 0, j)),
                  pl.BlockSpec((1, 1, D_MODEL), lambda l, j: (l, 0, j))],
        out_specs=pl.BlockSpec((1, 8, D_MODEL), lambda l, j: (l, 0, j)),
        out_shape=jax.ShapeDtypeStruct((DEPTH, 8, ADA_DIM), f32),
        name="ada_mod",
    )(cvec, w_ada, b_ada.reshape(DEPTH, 1, ADA_DIM))


def _in_tile(step):
    return (step + N_CTX // TOK_TILE) % (N_TOK // TOK_TILE)


def _in_kernel(xc_ref, xl_ref, *refs):
    is_ctx = _in_tile(pl.program_id(0)) < N_CTX // xc_ref.shape[0]
    _in_body(jnp.where(is_ctx, xc_ref[...], xl_ref[...]), *refs)


def _moe_in_kernel(pos0_ref, pos1_ref, ys_ref, x1_ref, rc_ref, mod_prev_ref, *refs):
    in_refs, x_out_ref, out_refs, (buf, sem) = refs[:10], refs[10], refs[11:-2], refs[-2:]
    x = _moe_residual(pos0_ref, pos1_ref, ys_ref, x1_ref, rc_ref, mod_prev_ref, buf, sem, tile_of=_in_tile)
    x_out_ref[...] = x
    _in_body(x, *in_refs, *out_refs)


def _in_body(x, mod_ref, n1_ref, w_ref, b_ref, wg_ref, bg_ref, wt_ref, bt_ref, k_in_ref, v_in_ref,
             a_ref, k_ref, v_ref, b_out_ref, kt_ref, g_ref, gt_ref, pin_ref):
    del k_in_ref, v_in_ref
    mod = mod_ref[0]
    h = x * lax.rsqrt(jnp.mean(x * x, axis=-1, keepdims=True) + EPS) * n1_ref[...]
    h = (h * (1.0 + mod[1:2]) + mod[0:1]).astype(bf16)
    pa = jnp.dot(h, w_ref[:, 0:W_A], preferred_element_type=f32) + b_ref[:, 0:W_A]
    a_ref[...] = pa.astype(bf16)
    k_ref[...] = pa[:, NA_WIDTH:2 * NA_WIDTH].reshape(k_ref.shape)
    v_ref[...] = pa[:, 2 * NA_WIDTH:W_A].reshape(v_ref.shape)
    for j in range(3):
        lo = W_A + j * ML_PW
        pb = jnp.dot(h, w_ref[:, lo:lo + ML_PW], preferred_element_type=f32) + b_ref[:, lo:lo + ML_PW]
        if j == 0:
            pb = pb * (ML_DIM ** -0.5)
        b_out_ref[:, j * ML_PW:(j + 1) * ML_PW] = pb.astype(bf16)
    lo = W_A + W_B
    pin_ref[...] = jnp.dot(h, w_ref[:, lo:lo + POOL_WIDTH], preferred_element_type=f32) + b_ref[:, lo:lo + POOL_WIDTH]
    g_ref[...] = jnp.dot(h, wg_ref[...], preferred_element_type=f32) + bg_ref[...]
    t = _nt(wt_ref[...], h) + bt_ref[...]
    kt_ref[...] = t[0:ML_PW].astype(bf16)
    gt_ref[...] = t[ML_PW:N_TCOLS]


def _in_proj_specs(layer):
    tm = TOK_TILE
    lyr = lambda shape: pl.BlockSpec((None,) + shape, lambda i, *_: (layer, 0, 0))
    rows = lambda width: pl.BlockSpec((tm, width), lambda i, *_: (_in_tile(i), 0))
    cols = lambda height: pl.BlockSpec((height, tm), lambda i, *_: (0, _in_tile(i)))
    param_specs = [pl.BlockSpec((1, 6, D_MODEL), lambda i, *_: (_mod_row(_in_tile(i), tm), 0, 0)),
                   lyr((1, D_MODEL)), lyr((D_MODEL, W_MAIN)), lyr((1, W_MAIN)), lyr((D_MODEL, LANE)), lyr((1, LANE)),
                   lyr((N_TCOLS, D_MODEL)), lyr((N_TCOLS, 1)),
                   pl.BlockSpec(memory_space=pl.ANY), pl.BlockSpec(memory_space=pl.ANY)]
    n_ctx_tiles = N_CTX // tm
    kv_spec = pl.BlockSpec((tm // SEQ, None, SEQ, NA_WIDTH),
                           lambda i, *_: (jnp.where(_in_tile(i) < n_ctx_tiles, _in_tile(i), 0), layer, 0, 0))
    kv_shape = jax.ShapeDtypeStruct((BATCH, DEPTH, SEQ, NA_WIDTH), f32)
    out_specs = [rows(W_A), kv_spec, kv_spec, rows(W_B), cols(ML_PW), rows(LANE), cols(N_GATE_COLS),
                 rows(POOL_WIDTH)]
    out_shape = [jax.ShapeDtypeStruct((N_TOK, W_A), bf16), kv_shape, kv_shape,
                 jax.ShapeDtypeStruct((N_TOK, W_B), bf16),
                 jax.ShapeDtypeStruct((ML_PW, N_TOK), bf16),
                 jax.ShapeDtypeStruct((N_TOK, LANE), f32),
                 jax.ShapeDtypeStruct((N_GATE_COLS, N_TOK), f32),
                 jax.ShapeDtypeStruct((N_TOK, POOL_WIDTH), f32)]
    return rows, param_specs, out_specs, out_shape


def _in_proj(x_ctx, x_lat, mods, layer, *params):
    tm = TOK_TILE
    n_ctx_tiles = N_CTX // tm
    rows, param_specs, out_specs, out_shape = _in_proj_specs(layer)
    return pl.pallas_call(
        _in_kernel,
        grid=(N_TOK // tm,),
        in_specs=[pl.BlockSpec((tm, D_MODEL), lambda i: (jnp.where(_in_tile(i) < n_ctx_tiles, _in_tile(i), 0), 0)),
                  pl.BlockSpec((tm, D_MODEL), lambda i: (jnp.maximum(_in_tile(i) - n_ctx_tiles, 0), 0))] + param_specs,
        out_specs=out_specs,
        out_shape=out_shape,
        input_output_aliases={2 + len(param_specs) - 2: 1, 2 + len(param_specs) - 1: 2},
        compiler_params=pltpu.CompilerParams(dimension_semantics=("arbitrary",), vmem_limit_bytes=VMEM_LIMIT),
        name="in_proj",
    )(x_ctx, x_lat, mods, *params)


def _moe_in_proj(pos0, pos1, ys, x1, rc, mods_prev, mods, layer, *params):
    tm = TOK_TILE
    rows, param_specs, out_specs, out_shape = _in_proj_specs(layer)
    return pl.pallas_call(
        _moe_in_kernel,
        grid_spec=pltpu.PrefetchScalarGridSpec(
            num_scalar_prefetch=2,
            grid=(N_TOK // tm,),
            in_specs=[pl.BlockSpec(memory_space=pl.ANY), rows(D_MODEL), rows(LANE),
                      pl.BlockSpec((1, 6, D_MODEL), lambda i, *_: (_mod_row(_in_tile(i), tm), 0, 0))] + param_specs,
            out_specs=[rows(D_MODEL)] + out_specs,
            scratch_shapes=[pltpu.VMEM((2, 2, tm, D_MODEL), f32), pltpu.SemaphoreType.DMA((2,))]),
        out_shape=[jax.ShapeDtypeStruct((N_TOK, D_MODEL), f32)] + out_shape,
        input_output_aliases={6 + len(param_specs) - 2: 2, 6 + len(param_specs) - 1: 3},
        compiler_params=pltpu.CompilerParams(dimension_semantics=("arbitrary",), vmem_limit_bytes=VMEM_LIMIT),
        name="moe_combine_in_proj",
    )(pos0, pos1, ys, x1, rc, mods_prev, mods, *params)


def _pair_attention(qp, parts):
    lane = lax.broadcasted_iota(jnp.int32, (1, LANE), 1)
    outs = []
    for j in range(2):
        in_half = (lane >= j * NA_DIM) & (lane < (j + 1) * NA_DIM)
        qm = jnp.where(in_half, qp, jnp.zeros_like(qp))
        scores = []
        for k, _, bias in parts:
            s = _nt(qm, k)
            if bias is not None:
                s = s + bias[j]
            scores.append(s)
        m = scores[0].max(axis=-1, keepdims=True)
        for s in scores[1:]:
            m = jnp.maximum(m, s.max(axis=-1, keepdims=True))
        den = None
        acc = None
        for s, (_, v, _) in zip(scores, parts):
            p = jnp.exp(s - m)
            ps = p.sum(axis=-1, keepdims=True)
            den = ps if den is None else den + ps
            o = jnp.dot(p.astype(bf16), v, preferred_element_type=f32)
            acc = o if acc is None else acc + o
        outs.append(acc / den)
    return jnp.where(lane < NA_DIM, outs[0], outs[1])


def _ctx_attn_kernel(q_ref, k_ref, v_ref, o_ref):
    for p in range(NA_PAIRS):
        sl = slice(p * LANE, (p + 1) * LANE)
        o = _pair_attention(q_ref[0, :, sl], [(k_ref[0, :, sl], v_ref[0, :, sl], None)])
        o_ref[0, :, sl] = o.astype(bf16)


def _ctx_attention(qkv):
    blk = lambda c: pl.BlockSpec((1, SEQ, NA_WIDTH), lambda b, c=c: (b, 0, c))
    return pl.pallas_call(
        _ctx_attn_kernel,
        grid=(BATCH,),
        in_specs=[blk(0), blk(1), blk(2)],
        out_specs=pl.BlockSpec((1, SEQ, NA_WIDTH), lambda b: (b, 0, 0)),
        out_shape=jax.ShapeDtypeStruct((BATCH, SEQ, NA_WIDTH), bf16),
        name="ctx_attention",
    )(qkv, qkv, qkv)


def _na_window_start(rb):
    return jnp.clip(rb * NA_QROWS - NA_ROWS // 2, 0, DEC_SEQ // GRID_W - NA_KROWS)


def _na_bias(tab_ref, head, rb):
    rows = DEC_SEQ // GRID_W
    ws = _na_window_start(rb)
    lane = lax.broadcasted_iota(jnp.int32, (1, NA_KROWS * GRID_W), 1)
    per_qrow = []
    for dq in range(NA_QROWS):
        qr = rb * NA_QROWS + dq
        a0 = ws - qr + (NA_ROWS - 1) + NA_KROWS
        tiles = [tab_ref[head, a0 + 2 * j] for j in range((NA_KROWS + 1) // 2)]
        t = jnp.concatenate(tiles, axis=1)[:, :NA_KROWS * GRID_W]
        lo = (jnp.clip(qr - NA_ROWS // 2, 0, rows - NA_ROWS) - ws) * GRID_W
        ok = (lane >= lo) & (lane < lo + NA_ROWS * GRID_W)
        per_qrow.append(jnp.where(ok, t, NEG_INF))
    return jnp.concatenate(per_qrow, axis=0)


def _na_kernel(q_ref, k_ref, v_ref, ck_ref, cv_ref, tab_ref, o_ref):
    rb = pl.program_id(1)
    start = pl.multiple_of(_na_window_start(rb) * GRID_W, GRID_W)
    nk = NA_KROWS * GRID_W
    for p in range(NA_PAIRS):
        sl = slice(p * LANE, (p + 1) * LANE)
        bias = [_na_bias(tab_ref.at[0], 2 * p + j, rb) for j in range(2)]
        parts = [(k_ref[0, pl.ds(start, nk), sl], v_ref[0, pl.ds(start, nk), sl], bias),
                 (ck_ref[0, :, sl], cv_ref[0, :, sl], None)]
        o = _pair_attention(q_ref[0, :, sl], parts)
        o_ref[0, :, sl] = o.astype(bf16)


def _neighborhood_attention(qkv, ck, cv, tables, layer):
    nq = NA_QROWS * GRID_W
    n_rb = DEC_SEQ // nq
    return pl.pallas_call(
        _na_kernel,
        grid=(DEC_BATCH, n_rb),
        in_specs=[pl.BlockSpec((1, nq, NA_WIDTH), lambda b, r: (1 + b, r, 0)),
                  pl.BlockSpec((1, DEC_SEQ, NA_WIDTH), lambda b, r: (1 + b, 0, 1)),
                  pl.BlockSpec((1, DEC_SEQ, NA_WIDTH), lambda b, r: (1 + b, 0, 2)),
                  pl.BlockSpec((1, PAST_LEN, NA_WIDTH), lambda b, r: (b, 0, 0)),
                  pl.BlockSpec((1, PAST_LEN, NA_WIDTH), lambda b, r: (b, 0, 0)),
                  pl.BlockSpec((1,) + tables.shape[1:], lambda b, r: (layer, 0, 0, 0, 0))],
        out_specs=pl.BlockSpec((1, nq, NA_WIDTH), lambda b, r: (b, r, 0)),
        out_shape=jax.ShapeDtypeStruct((DEC_BATCH, DEC_SEQ, NA_WIDTH), bf16),
        compiler_params=pltpu.CompilerParams(dimension_semantics=("arbitrary", "arbitrary"),
                                             vmem_limit_bytes=VMEM_LIMIT),
        name="neighborhood_attention",
    )(qkv, qkv, qkv, ck, cv, tables)


def _na_bias_tables(rpb):
    qc = np.arange(GRID_W)[:, None]
    kc = np.arange(GRID_W)[None, :]
    dc = np.clip(kc - qc + NA_COLS - 1, 0, RPB_COLS - 1)
    col_start = np.clip(qc - NA_COLS // 2, 0, GRID_W - NA_COLS)
    col_ok = (kc >= col_start) & (kc < col_start + NA_COLS)
    pick_col = (dc[None] == np.arange(RPB_COLS)[:, None, None]).astype(np.float32)
    rpb_pad = jnp.pad(rpb.astype(f32), ((0, 0), (0, 0), (NA_KROWS, NA_KROWS + 1), (0, 0)))
    n_a = rpb_pad.shape[2] - 1
    rows2 = jnp.stack([rpb_pad[:, :, :-1], rpb_pad[:, :, 1:]], axis=3)
    pick2 = np.zeros((2, RPB_COLS, GRID_W, 2 * GRID_W), np.float32)
    for j in range(2):
        pick2[j, :, :, j * GRID_W:(j + 1) * GRID_W] = pick_col
    tiles = jnp.einsum('lhajb,jbqc->lhaqc', rows2, pick2, precision=HI)
    a_pad = np.arange(n_a)[:, None] + np.arange(2)[None, :]
    row_ok = (a_pad >= NA_KROWS) & (a_pad < NA_KROWS + RPB_ROWS)
    ok = (row_ok[:, None, :, None] & col_ok[None, :, None, :]).reshape(n_a, GRID_W, 2 * GRID_W)
    return jnp.where(ok[None, None], tiles, NEG_INF)


def _log_sigmoid(x):
    return -(jnp.maximum(-x, 0.0) + jnp.log(1.0 + jnp.exp(-jnp.abs(x))))


def _split3(x):
    hi = x.astype(bf16)
    r1 = x - hi.astype(f32)
    mid = r1.astype(bf16)
    lo = (r1 - mid.astype(f32)).astype(bf16)
    return hi, mid, lo


def _mlstm_kernel(qf_ref, vf_ref, ktf_ref, gf_ref, gtf_ref, qb_ref, vb_ref, ktb_ref, gb_ref, gtb_ref,
                  c0_ref, m0_ref, hf_ref, hb_ref, c_out_ref, m_out_ref, c_scr, m_scr):
    L = ML_CHUNK
    seq, c, n_chunks, _ = _ml_schedule(pl.program_id(0))

    @pl.when(c == 0)
    def _():
        is_ctx = seq < BATCH
        c_scr[...] = jnp.where(is_ctx, 0.0, c0_ref[0])
        m_scr[...] = jnp.where(is_ctx, 0.0, m0_ref[0])

    ri = lax.broadcasted_iota(jnp.int32, (L, L), 0)
    ci = lax.broadcasted_iota(jnp.int32, (L, L), 1)
    lane = lax.broadcasted_iota(jnp.int32, (L, ML_PAD), 1)
    is_ncol = lane == ML_DIM
    lower = ri >= ci
    upper = ri <= ci
    lower_b = jnp.where(lower, 1.0, 0.0).astype(bf16)
    upper_b = jnp.where(upper, 1.0, 0.0).astype(bf16)
    dirs = ((qf_ref, ktf_ref, vf_ref, gf_ref, gtf_ref, hf_ref), (qb_ref, ktb_ref, vb_ref, gb_ref, gtb_ref, hb_ref))
    for d, (q_ref, kt_ref, v_ref, g_ref, gt_ref, h_ref) in enumerate(dirs):
        g = g_ref[...][:, 0:N_GATE_COLS]
        gt = gt_ref[...]
        lf_c = _log_sigmoid(g)
        lf_r = _log_sigmoid(gt)
        b_cols = sum(jnp.dot(lower_b, part, preferred_element_type=f32) for part in _split3(lf_c))
        b_rows = sum(jnp.dot(part, upper_b, preferred_element_type=f32) for part in _split3(lf_r))
        tot_c = jnp.sum(lf_c, axis=0, keepdims=True)
        tot_r = jnp.sum(lf_r, axis=1, keepdims=True)
        visible = lower
        if d == 1:
            b_cols = tot_c - b_cols + lf_c
            b_rows = tot_r - b_rows + lf_r
            visible = upper
        for hd in range(ML_HEADS):
            st = d * ML_HEADS + hd
            ci_ = 2 * ML_HEADS * d + hd
            cf_ = ci_ + ML_HEADS
            sl = slice(hd * ML_PAD, (hd + 1) * ML_PAD)
            bc = b_cols[:, cf_:cf_ + 1]
            br = b_rows[cf_:cf_ + 1, :]
            li_r = gt[ci_:ci_ + 1, :]
            m_prev = m_scr[st:st + 1, 0:1]
            dmat = jnp.where(visible, bc - br + li_r, NEG_INF)
            inter = bc + m_prev
            m_t = jnp.maximum(inter, dmat.max(axis=-1, keepdims=True))
            w_intra = jnp.exp(dmat - m_t)
            w_inter = jnp.exp(inter - m_t)
            qh = q_ref[0, :, sl]
            kht = kt_ref[sl, :]
            v_aug = jnp.where(is_ncol, jnp.ones((), bf16), v_ref[0, :, sl])
            s = (jnp.dot(qh, kht, preferred_element_type=f32) * w_intra).astype(bf16)
            c_aug = c_scr[st]
            na = (w_inter * jnp.dot(qh, c_aug.astype(bf16), preferred_element_type=f32)
                  + jnp.dot(s, v_aug, preferred_element_type=f32))
            den = na[:, ML_DIM:ML_DIM + 1]
            h_ref[0, :, sl] = jnp.where(lane < ML_DIM, na / jnp.maximum(jnp.abs(den), jnp.exp(-m_t)), 0.0)
            b_end = tot_r[cf_:cf_ + 1, :]
            g_row = b_end - br + li_r
            m_new = jnp.maximum(b_end + m_prev, g_row.max(axis=1, keepdims=True))
            decay = jnp.exp(b_end + m_prev - m_new)
            kwt = (kht.astype(f32) * jnp.exp(g_row - m_new)).astype(bf16)
            c_scr[st] = decay * c_aug + jnp.dot(kwt, v_aug, preferred_element_type=f32)
            m_scr[st:st + 1, :] = jnp.broadcast_to(m_new, (1, LANE))

    @pl.when(c == n_chunks - 1)
    def _():
        c_out_ref[0] = c_scr[...]
        m_out_ref[0] = m_scr[...]


def _ml_schedule(s):
    nc_ctx, nc_lat = SEQ // ML_CHUNK, DEC_SEQ // ML_CHUNK
    n_ctx_steps = BATCH * nc_ctx
    is_ctx = s < n_ctx_steps
    t = s - n_ctx_steps
    seq = jnp.where(is_ctx, s // nc_ctx, BATCH + t // nc_lat)
    c = jnp.where(is_ctx, s % nc_ctx, t % nc_lat)
    nc = jnp.where(is_ctx, nc_ctx, nc_lat)
    base = jnp.where(is_ctx, (s // nc_ctx) * nc_ctx, n_ctx_steps + (t // nc_lat) * nc_lat)
    return seq, c, nc, base


def _mlstm(qvo, kt, gates, gates_t, c0, m0):
    L = ML_CHUNK
    n_seq = BATCH + DEC_BATCH

    def fwd(s):
        _, c, _, base = _ml_schedule(s)
        return base + c

    def bwd(s):
        _, c, nc, base = _ml_schedule(s)
        return base + nc - 1 - c

    seq_of = lambda s: _ml_schedule(s)[0]
    lat_of = lambda s: jnp.maximum(seq_of(s) - BATCH, 0)

    def specs(pos):
        return [pl.BlockSpec((1, L, ML_PW), lambda s, j=j: (pos(s), 0, j)) for j in range(2)] + [
            pl.BlockSpec((ML_PW, L), lambda s: (0, pos(s))),
            pl.BlockSpec((L, LANE), lambda s: (pos(s), 0)),
            pl.BlockSpec((N_GATE_COLS, L), lambda s: (0, pos(s)))]

    q3 = qvo.reshape(N_TOK // L, L, W_B)
    n_str = 2 * ML_HEADS
    return pl.pallas_call(
        _mlstm_kernel,
        grid=(N_TOK // L,),
        in_specs=specs(fwd) + specs(bwd) + [
            pl.BlockSpec((1, n_str, ML_PAD, CAUG), lambda s: (lat_of(s), 0, 0, 0)),
            pl.BlockSpec((1, n_str, LANE), lambda s: (lat_of(s), 0, 0))],
        out_specs=[pl.BlockSpec((1, L, ML_PW), lambda s: (fwd(s), 0, 0)),
                   pl.BlockSpec((1, L, ML_PW), lambda s: (bwd(s), 0, 0)),
                   pl.BlockSpec((1, n_str, ML_PAD, CAUG), lambda s: (seq_of(s), 0, 0, 0)),
                   pl.BlockSpec((1, n_str, LANE), lambda s: (seq_of(s), 0, 0))],
        out_shape=[jax.ShapeDtypeStruct((N_TOK // L, L, ML_PW), f32),
                   jax.ShapeDtypeStruct((N_TOK // L, L, ML_PW), f32),
                   jax.ShapeDtypeStruct((n_seq, n_str, ML_PAD, CAUG), f32),
                   jax.ShapeDtypeStruct((n_seq, n_str, LANE), f32)],
        scratch_shapes=[pltpu.VMEM((n_str, ML_PAD, CAUG), f32), pltpu.VMEM((n_str, LANE), f32)],
        compiler_params=pltpu.CompilerParams(dimension_semantics=("arbitrary",), vmem_limit_bytes=VMEM_LIMIT),
        name="mlstm",
    )(q3, q3, kt, gates, gates_t, q3, q3, kt, gates, gates_t, c0, m0)


def _pack_ml_state(C, n, m):
    B = C.shape[0]
    c_aug = jnp.zeros((B, 2, ML_HEADS, ML_PAD, CAUG), f32)
    c_aug = c_aug.at[:, :, :, :ML_DIM, :ML_DIM].set(C.astype(f32))
    c_aug = c_aug.at[:, :, :, :ML_DIM, ML_DIM].set(n.astype(f32))
    m_b = jnp.broadcast_to(m.astype(f32)[..., None], (B, 2, ML_HEADS, LANE))
    return c_aug.reshape(B, 2 * ML_HEADS, ML_PAD, CAUG), m_b.reshape(B, 2 * ML_HEADS, LANE)


def _unpack_ml_state(c_aug, m_b):
    B = c_aug.shape[0]
    c_aug = c_aug.reshape(B, 2, ML_HEADS, ML_PAD, CAUG)
    return (c_aug[:, :, :, :ML_DIM, :ML_DIM], c_aug[:, :, :, :ML_DIM, ML_DIM],
            m_b.reshape(B, 2, ML_HEADS, LANE)[..., 0])


def _pool_rows(u_prev, u_cur, u_next, w_bd, scale, t0, seq_len):
    tm = u_cur.shape[0]
    u_win = jnp.concatenate([u_prev, u_cur, u_next], axis=0)
    u_hi = u_win.astype(bf16)
    u_lo = (u_win - u_hi.astype(f32)).astype(bf16)
    lane = lax.broadcasted_iota(jnp.int32, (1, LANE), 1)
    blocks = []
    for r0 in range(0, tm, POOL_BLOCK):
        win = slice(r0, r0 + POOL_BLOCK + 2 * POOL_HALO)
        t_abs = t0 + r0 + lax.broadcasted_iota(jnp.int32, (POOL_BLOCK, 1), 0)
        s_abs = t0 + r0 - POOL_HALO + lax.broadcasted_iota(jnp.int32, (1, POOL_BLOCK + 2 * POOL_HALO), 1)
        t_loc = t_abs & (seq_len - 1)
        seq_start = t_abs - t_loc
        means = []
        for w in POOL_WINDOWS:
            lo = jnp.maximum(t_loc - w // 2, 0)
            hi = jnp.minimum(t_loc - w // 2 + w, seq_len)
            in_win = (s_abs >= seq_start + lo) & (s_abs < seq_start + hi)
            means.append((jnp.where(in_win, 1.0, 0.0).astype(bf16), 1.0 / (hi - lo).astype(f32)))
        pooled = []
        for p in range(POOL_GROUPS // 2):
            sl = slice(p * LANE, (p + 1) * LANE)
            halves = []
            for a, inv_cnt in means[2 * p:2 * p + 2]:
                tot = (jnp.dot(a, u_hi[win, sl], preferred_element_type=f32)
                       + jnp.dot(a, u_lo[win, sl], preferred_element_type=f32))
                halves.append(tot * inv_cnt)
            pooled.append(jnp.where(lane < POOL_DIM, halves[0], halves[1]) - u_cur[r0:r0 + POOL_BLOCK, sl])
        blocks.append(jnp.concatenate(pooled, axis=1))
    pooled = jnp.concatenate(blocks, axis=0).astype(bf16)
    return jnp.dot(pooled, w_bd, preferred_element_type=f32) * scale


def _top2_sum(a, b, c, d):
    hi1, lo1 = jnp.maximum(a, b), jnp.minimum(a, b)
    hi2, lo2 = jnp.maximum(c, d), jnp.minimum(c, d)
    return jnp.maximum(hi1, hi2) + jnp.maximum(jnp.minimum(hi1, hi2), jnp.maximum(lo1, lo2))


def _first_match(vals, target):
    idx = jnp.full_like(target, float(len(vals) - 1))
    for i in range(len(vals) - 2, -1, -1):
        idx = jnp.where(vals[i] == target, float(i), idx)
    return idx


def _pick(vals, idx):
    out = vals[-1]
    for i in range(len(vals) - 2, -1, -1):
        out = jnp.where(idx == float(i), vals[i], out)
    return out


def _route(logits_t, bias_t):
    scores = jax.nn.sigmoid(logits_t)
    sel = scores + bias_t
    row = lambda a, i: a[i:i + 1, :]
    grp = [_top2_sum(*[row(sel, EXPERTS_PER_GROUP * g + i) for i in range(EXPERTS_PER_GROUP)])
           for g in range(N_EXPERT_GROUPS)]
    best = functools.reduce(jnp.maximum, grp)
    gidx = _first_match(grp, best)
    sel_g = [_pick([row(sel, EXPERTS_PER_GROUP * g + i) for g in range(N_EXPERT_GROUPS)], gidx)
             for i in range(EXPERTS_PER_GROUP)]
    sco_g = [_pick([row(scores, EXPERTS_PER_GROUP * g + i) for g in range(N_EXPERT_GROUPS)], gidx)
             for i in range(EXPERTS_PER_GROUP)]
    i0 = _first_match(sel_g, functools.reduce(jnp.maximum, sel_g))
    rest = [jnp.where(i0 == float(i), -jnp.inf, sel_g[i]) for i in range(EXPERTS_PER_GROUP)]
    i1 = _first_match(rest, functools.reduce(jnp.maximum, rest))
    s0, s1 = _pick(sco_g, i0), _pick(sco_g, i1)
    tot = s0 + s1
    rid = lax.broadcasted_iota(jnp.int32, (LANE, logits_t.shape[1]), 0)
    rows = (EXPERTS_PER_GROUP * gidx + i0, EXPERTS_PER_GROUP * gidx + i1, s0 / tot, s1 / tot)
    out = jnp.zeros(rid.shape, f32)
    for i, r in enumerate(rows):
        out = jnp.where(rid == i, r, out)
    return out


def _out_kernel(xc_ref, xl_ref, mod_ref, oac_ref, oal_ref, hf_ref, hb_ref, ob_ref, up_ref, uc_ref, un_ref, wp_ref,
                psc_ref, mln_ref, wo_ref, n2_ref, wr_ref, br_ref, x1_ref, h2_ref, rt_ref, rc_ref):
    tm = xc_ref.shape[0]
    i = pl.program_id(0)
    is_ctx = i < N_CTX // tm
    mod = mod_ref[0]
    out_a = jnp.where(is_ctx, oac_ref[...], oal_ref[...])
    out_c = _pool_rows(up_ref[...], uc_ref[...], un_ref[...], wp_ref[...], psc_ref[...], i * tm,
                       jnp.where(is_ctx, SEQ, DEC_SEQ)).astype(bf16)
    hsum = hf_ref[...] + hb_ref[...]
    outs_b = []
    for hd in range(ML_HEADS):
        sl = slice(hd * ML_PAD, (hd + 1) * ML_PAD)
        hh = hsum[:, sl]
        ms = jnp.sum(hh * hh, axis=-1, keepdims=True) * (1.0 / ML_DIM)
        hn = hh * lax.rsqrt(ms + EPS) * mln_ref[:, sl]
        outs_b.append((jax.nn.sigmoid(ob_ref[:, sl].astype(f32)) * hn).astype(bf16))
    out_b = jnp.concatenate(outs_b, axis=1)
    mixed = (jnp.dot(out_a, wo_ref[0:NA_WIDTH, :], preferred_element_type=f32)
             + jnp.dot(out_b, wo_ref[NA_WIDTH:NA_WIDTH + ML_PW, :], preferred_element_type=f32)
             + jnp.dot(out_c, wo_ref[NA_WIDTH + ML_PW:, :], preferred_element_type=f32))
    x1 = jnp.where(is_ctx, xc_ref[...], xl_ref[...]) + mod[2:3] * mixed
    x1_ref[...] = x1
    h2 = x1 * lax.rsqrt(jnp.mean(x1 * x1, axis=-1, keepdims=True) + EPS) * n2_ref[...]
    h2 = h2 * (1.0 + mod[4:5]) + mod[3:4]
    h2_ref[...] = h2.astype(bf16)
    h_hi = h2.astype(bf16)
    h_lo = (h2 - h_hi.astype(f32)).astype(bf16)
    w_hi = wr_ref[...].astype(bf16)
    w_lo = (wr_ref[...] - w_hi.astype(f32)).astype(bf16)
    route_t = _route(_nt(w_hi, h_hi) + (_nt(w_hi, h_lo) + _nt(w_lo, h_hi)), br_ref[...])
    rt_ref[...] = route_t[0:8]
    rc_ref[...] = route_t.T


def _out_proj(x_ctx, x_lat, x_lat_block0, mods, layer, oa_ctx, oa_lat, hf, hb, qvo, pin, w_bd, psc, mln, wo, n2,
              wr_t, br_t):
    tm = TOK_TILE
    const = lambda i: (0, 0)
    lyr = lambda shape: pl.BlockSpec((None,) + shape, lambda i: (layer, 0, 0))
    row = lambda i: (i, 0)
    n_ctx_tiles = N_CTX // tm
    halo_blocks = tm // POOL_HALO
    return pl.pallas_call(
        _out_kernel,
        grid=(N_TOK // tm,),
        in_specs=[pl.BlockSpec((tm, D_MODEL), lambda i: (jnp.minimum(i, n_ctx_tiles - 1), 0)),
                  pl.BlockSpec((tm, D_MODEL), lambda i: (jnp.maximum(i - n_ctx_tiles, 0) + x_lat_block0, 0)),
                  pl.BlockSpec((1, 6, D_MODEL), lambda i: (_mod_row(i, tm), 0, 0)),
                  pl.BlockSpec((tm, NA_WIDTH), lambda i: (jnp.minimum(i, n_ctx_tiles - 1), 0)),
                  pl.BlockSpec((tm, NA_WIDTH), lambda i: (jnp.maximum(i - n_ctx_tiles, 0), 0)),
                  pl.BlockSpec((tm, ML_PW), row),
                  pl.BlockSpec((tm, ML_PW), row),
                  pl.BlockSpec((tm, ML_PW), lambda i: (i, 2)),
                  pl.BlockSpec((POOL_HALO, POOL_WIDTH), lambda i: (jnp.maximum(i * halo_blocks - 1, 0), 0)),
                  pl.BlockSpec((tm, POOL_WIDTH), row),
                  pl.BlockSpec((POOL_HALO, POOL_WIDTH),
                               lambda i: (jnp.minimum((i + 1) * halo_blocks, N_TOK // POOL_HALO - 1), 0)),
                  lyr((POOL_WIDTH, POOL_WIDTH)), lyr((1, POOL_WIDTH)), lyr((1, ML_PW)),
                  lyr((NA_WIDTH + ML_PW + POOL_WIDTH, D_MODEL)), lyr((1, D_MODEL)),
                  pl.BlockSpec((N_EXPERTS, D_MODEL), const),
                  pl.BlockSpec((N_EXPERTS, 1), const)],
        out_specs=[pl.BlockSpec((tm, D_MODEL), row),
                   pl.BlockSpec((tm, D_MODEL), row),
                   pl.BlockSpec((8, tm), lambda i: (0, i)),
                   pl.BlockSpec((tm, LANE), row)],
        out_shape=[jax.ShapeDtypeStruct((N_TOK, D_MODEL), f32),
                   jax.ShapeDtypeStruct((N_TOK, D_MODEL), bf16),
                   jax.ShapeDtypeStruct((8, N_TOK), f32),
                   jax.ShapeDtypeStruct((N_TOK, LANE), f32)],
        compiler_params=pltpu.CompilerParams(dimension_semantics=("arbitrary",), vmem_limit_bytes=VMEM_LIMIT),
        name="out_proj_router",
    )(x_ctx, x_lat, mods, oa_ctx, oa_lat, hf, hb, qvo, pin, pin, pin, w_bd, psc, mln, wo, n2, wr_t, br_t)


def _ceil_to(x, m):
    return jnp.floor((x + (m - 1)) * (1.0 / m)) * m


def _prefix_over_experts(v):
    er = lax.broadcasted_iota(jnp.int32, (N_EXPERTS, N_EXPERTS), 0)
    ec = lax.broadcasted_iota(jnp.int32, (N_EXPERTS, N_EXPERTS), 1)
    return jnp.dot(jnp.where(ec < er, 1.0, 0.0), v, preferred_element_type=f32, precision=HI)


def _experts_to_lanes(v):
    sub = lax.broadcasted_iota(jnp.int32, (N_EXPERTS, LANE), 0)
    lane = lax.broadcasted_iota(jnp.int32, (N_EXPERTS, LANE), 1)
    return jnp.sum(jnp.where(sub == lane, v, 0.0), axis=0, keepdims=True)


def _expert_hits(rt):
    rid = lax.broadcasted_iota(jnp.int32, (N_EXPERTS, rt.shape[1]), 0).astype(f32)
    oh0 = rid == rt[0:1, :]
    oh1 = rid == rt[1:2, :]
    both = jnp.where(oh0 | oh1, 1.0, 0.0)
    runs = jnp.broadcast_to(_ceil_to(jnp.sum(both, axis=1, keepdims=True), MOE_CHUNK), (N_EXPERTS, LANE))
    return oh0, oh1, both, runs


def _rank_kernel(rt_all_ref, rt_ref, pos_ref, te_ref, tab_ref, carry_ref):
    tm = rt_ref.shape[1]
    step = pl.program_id(0)

    @pl.when(step == 0)
    def _():
        totals = jnp.zeros((N_EXPERTS, LANE), f32)
        for i in range(rt_all_ref.shape[1] // tm):
            totals = totals + _expert_hits(rt_all_ref[:, i * tm:(i + 1) * tm])[3]
        padded = _ceil_to(totals, MOE_TILE)
        off = _prefix_over_experts(padded)
        carry_ref[...] = off
        total = jnp.sum(padded, axis=0, keepdims=True)
        n_used = total * (1.0 / MOE_TILE)
        tile = lax.broadcasted_iota(jnp.int32, (1, LANE), 1).astype(f32)
        row0 = jnp.minimum(tile, n_used - 1.0) * MOE_TILE
        expert = jnp.sum(jnp.where(off <= row0, 1.0, 0.0), axis=0, keepdims=True) - 1.0
        sub = lax.broadcasted_iota(jnp.int32, (8, LANE), 0)
        te_ref[...] = jnp.where(sub == 0, expert, jnp.where(sub == 1, n_used, 0.0)).astype(jnp.int32)

    @pl.when(step > 0)
    def _():
        oh0, oh1, both, runs = _expert_hits(rt_ref[...])
        sr = lax.broadcasted_iota(jnp.int32, (tm, tm), 0)
        sc = lax.broadcasted_iota(jnp.int32, (tm, tm), 1)
        earlier = jnp.dot(both.astype(bf16), jnp.where(sr < sc, 1.0, 0.0).astype(bf16),
                          preferred_element_type=f32)
        g_off = carry_ref[...]
        l_off = _prefix_over_experts(runs)
        g_row = g_off[:, 0:1] + earlier
        l_row = l_off[:, 0:1] + earlier
        pick = lambda oh, v: jnp.sum(jnp.where(oh, v, 0.0), axis=0, keepdims=True)
        rows = (pick(oh0, g_row), pick(oh1, g_row), pick(oh0, l_row), pick(oh1, l_row))
        sub = lax.broadcasted_iota(jnp.int32, (8, tm), 0)
        out = jnp.zeros((8, tm), f32)
        for k, r in enumerate(rows):
            out = jnp.where(sub == k, r, out)
        pos_ref[...] = out.astype(jnp.int32)
        sub = lax.broadcasted_iota(jnp.int32, (8, LANE), 0)
        tab = jnp.zeros((8, LANE), f32)
        for k, v in enumerate((runs * (1.0 / MOE_CHUNK), l_off, g_off)):
            tab = jnp.where(sub == k, _experts_to_lanes(v), tab)
        tab_ref[0] = tab.astype(jnp.int32)
        carry_ref[...] = g_off + runs


def _rank(route_t):
    tm = TOK_TILE
    n_tiles = N_TOK // tm
    tile_of = lambda s: jnp.maximum(s - 1, 0)
    return pl.pallas_call(
        _rank_kernel,
        grid=(1 + n_tiles,),
        in_specs=[pl.BlockSpec((8, N_TOK), lambda s: (0, 0)),
                  pl.BlockSpec((8, tm), lambda s: (0, tile_of(s)))],
        out_specs=[pl.BlockSpec((8, tm), lambda s: (0, tile_of(s))),
                   pl.BlockSpec((8, LANE), lambda s: (0, 0)),
                   pl.BlockSpec((1, 8, LANE), lambda s: (tile_of(s), 0, 0))],
        out_shape=[jax.ShapeDtypeStruct((8, N_TOK), jnp.int32),
                   jax.ShapeDtypeStruct((8, LANE), jnp.int32),
                   jax.ShapeDtypeStruct((n_tiles, 8, LANE), jnp.int32)],
        scratch_shapes=[pltpu.VMEM((N_EXPERTS, LANE), f32)],
        compiler_params=pltpu.CompilerParams(dimension_semantics=("arbitrary",)),
        name="moe_rank",
    )(route_t, route_t)


def _dispatch_kernel(tab_ref, h_ref, rows_ref, xs_in_ref, xs_ref, loc, sem):
    del xs_in_ref
    tm = h_ref.shape[0]
    i = pl.program_id(0)
    slot = i % 2
    rid = lax.broadcasted_iota(jnp.int32, (MOE_LOCAL_ROWS, tm), 0)
    sel = (rid == rows_ref[2:3, :]) | (rid == rows_ref[3:4, :])
    loc[slot] = jnp.dot(jnp.where(sel, 1.0, 0.0).astype(bf16), h_ref[...], preferred_element_type=f32).astype(bf16)

    def chunk_copy(sl, src_row, dst_row):
        return pltpu.make_async_copy(loc.at[sl, pl.ds(pl.multiple_of(src_row, MOE_CHUNK), MOE_CHUNK)],
                                     xs_ref.at[pl.ds(pl.multiple_of(dst_row, MOE_CHUNK), MOE_CHUNK)], sem.at[sl])

    def chunks_of(tile):
        return sum(tab_ref[3 * tile * N_EXPERTS + e] for e in range(N_EXPERTS))

    def wait_chunks(sl, n):
        def wait_one(c, carry):
            chunk_copy(sl, 0, 0).wait()
            return carry

        lax.fori_loop(0, n, wait_one, 0)

    for e in range(N_EXPERTS):
        n_chunks, l_off, g_off = (tab_ref[(3 * i + k) * N_EXPERTS + e] for k in range(3))

        def issue(c, carry, l_off=l_off, g_off=g_off):
            chunk_copy(slot, l_off + c * MOE_CHUNK, g_off + c * MOE_CHUNK).start()
            return carry

        lax.fori_loop(0, n_chunks, issue, 0)

    @pl.when(i > 0)
    def _():
        wait_chunks(1 - slot, chunks_of(i - 1))

    @pl.when(i == pl.num_programs(0) - 1)
    def _():
        wait_chunks(slot, chunks_of(i))


def _dispatch(run_table, h2, rows, xs_init):
    tm = TOK_TILE
    return pl.pallas_call(
        _dispatch_kernel,
        grid_spec=pltpu.PrefetchScalarGridSpec(
            num_scalar_prefetch=1,
            grid=(N_TOK // tm,),
            in_specs=[pl.BlockSpec((tm, D_MODEL), lambda i, tab: (i, 0)),
                      pl.BlockSpec((8, tm), lambda i, tab: (0, i)),
                      pl.BlockSpec(memory_space=pl.ANY)],
            out_specs=pl.BlockSpec(memory_space=pl.ANY),
            scratch_shapes=[pltpu.VMEM((2, MOE_LOCAL_ROWS, D_MODEL), bf16), pltpu.SemaphoreType.DMA((2,))]),
        out_shape=jax.ShapeDtypeStruct(xs_init.shape, xs_init.dtype),
        input_output_aliases={3: 0},
        compiler_params=pltpu.CompilerParams(dimension_semantics=("arbitrary",), vmem_limit_bytes=VMEM_LIMIT),
        name="moe_dispatch",
    )(run_table, h2, rows, xs_init)


def _expert_kernel(te_ref, xs_ref, wg_hbm, wu_hbm, wd_hbm, ys_ref, wg_f32, wu_f32, wd_f32, wg_bf, wu_bf, wd_bf,
                   slot_ref, sem, *, layer):
    j = pl.program_id(0)
    n_used = te_ref[1, 0]
    used = j < n_used
    expert = te_ref[0, j]
    new_expert = jnp.logical_or(j == 0, expert != te_ref[0, jnp.maximum(j - 1, 0)])

    def weight_copies(e, slot):
        return [pltpu.make_async_copy(hbm.at[layer, e], buf.at[slot], sem.at[slot])
                for hbm, buf in ((wg_hbm, wg_f32), (wu_hbm, wu_f32), (wd_hbm, wd_f32))]

    @pl.when(j == 0)
    def _():
        slot_ref[0] = 1
        for cp in weight_copies(expert, 0):
            cp.start()

    @pl.when(jnp.logical_not(used))
    def _():
        ys_ref[...] = jnp.zeros_like(ys_ref)

    @pl.when(used & new_expert)
    def _():
        slot = 1 - slot_ref[0]
        slot_ref[0] = slot
        for cp in weight_copies(expert, slot):
            cp.wait()
        wg_bf[...] = wg_f32[slot].astype(bf16)
        wu_bf[...] = wu_f32[slot].astype(bf16)
        wd_bf[...] = wd_f32[slot].astype(bf16)
        nxt = lax.while_loop(lambda t: (t < n_used) & (te_ref[0, jnp.minimum(t, LANE - 1)] == expert),
                             lambda t: t + 1, j + 1)

        @pl.when(nxt < n_used)
        def _():
            for cp in weight_copies(te_ref[0, nxt], 1 - slot):
                cp.start()

    @pl.when(used)
    def _():
        x = xs_ref[...]
        hg = jnp.dot(x, wg_bf[...], preferred_element_type=f32)
        hu = jnp.dot(x, wu_bf[...], preferred_element_type=f32)
        hid = (hg * jax.nn.sigmoid(hg) * hu).astype(bf16)
        ys_ref[...] = jnp.dot(hid, wd_bf[...], preferred_element_type=f32)


def _experts(te, xs, w_gate, w_up, w_down, layer):
    tm = MOE_TILE
    row = lambda j, te: (jnp.minimum(j, te[1, 0] - 1), 0)
    hbm = pl.BlockSpec(memory_space=pl.ANY)
    return pl.pallas_call(
        functools.partial(_expert_kernel, layer=layer),
        grid_spec=pltpu.PrefetchScalarGridSpec(
            num_scalar_prefetch=1,
            grid=(MOE_ROWS // tm,),
            in_specs=[pl.BlockSpec((tm, D_MODEL), row), hbm, hbm, hbm],
            out_specs=pl.BlockSpec((tm, D_MODEL), lambda j, te: (j, 0)),
            scratch_shapes=[pltpu.VMEM((2, D_MODEL, D_EXPERT), f32), pltpu.VMEM((2, D_MODEL, D_EXPERT), f32),
                            pltpu.VMEM((2, D_EXPERT, D_MODEL), f32),
                            pltpu.VMEM((D_MODEL, D_EXPERT), bf16), pltpu.VMEM((D_MODEL, D_EXPERT), bf16),
                            pltpu.VMEM((D_EXPERT, D_MODEL), bf16),
                            pltpu.SMEM((1,), jnp.int32), pltpu.SemaphoreType.DMA((2,))]),
        out_shape=jax.ShapeDtypeStruct((MOE_ROWS, D_MODEL), f32),
        compiler_params=pltpu.CompilerParams(dimension_semantics=("arbitrary",), vmem_limit_bytes=VMEM_LIMIT),
        name="moe_experts",
    )(te, xs, w_gate, w_up, w_down)


def _gather_expert_rows(pos0_ref, pos1_ref, ys_ref, buf, sem, tile_of):
    rows = buf.shape[2]
    i = pl.program_id(0)
    slot = i % 2

    def issue(tile, sl):
        base = tile * rows

        def body(t, carry):
            for s, pos_ref in enumerate((pos0_ref, pos1_ref)):
                pltpu.make_async_copy(ys_ref.at[pl.ds(pos_ref[base + t], 1)], buf.at[sl, s, pl.ds(t, 1)],
                                      sem.at[sl]).start()
            return carry

        lax.fori_loop(0, rows, body, 0, unroll=8)

    @pl.when(i == 0)
    def _():
        issue(tile_of(0), 0)

    @pl.when(i + 1 < pl.num_programs(0))
    def _():
        issue(tile_of(i + 1), 1 - slot)

    for s in range(2):
        pltpu.make_async_copy(ys_ref.at[pl.ds(0, rows)], buf.at[slot, s], sem.at[slot]).wait()
    return buf[slot, 0], buf[slot, 1]


def _moe_residual(pos0_ref, pos1_ref, ys_ref, x1_ref, rc_ref, mod_ref, buf, sem, tile_of=lambda step: step):
    y0, y1 = _gather_expert_rows(pos0_ref, pos1_ref, ys_ref, buf, sem, tile_of)
    rc = rc_ref[...]
    return x1_ref[...] + mod_ref[0][5:6] * (rc[:, 2:3] * y0 + rc[:, 3:4] * y1)


def _final_kernel(pos0_ref, pos1_ref, ys_ref, x1_ref, rc_ref, mod_ref, fn_ref, yc_ref, yl_ref, buf, sem):
    x2 = _moe_residual(pos0_ref, pos1_ref, ys_ref, x1_ref, rc_ref, mod_ref, buf, sem)
    y = x2 * lax.rsqrt(jnp.mean(x2 * x2, axis=-1, keepdims=True) + EPS) * fn_ref[...]
    is_ctx = pl.program_id(0) < N_CTX // x1_ref.shape[0]

    @pl.when(is_ctx)
    def _():
        yc_ref[...] = y

    @pl.when(jnp.logical_not(is_ctx))
    def _():
        yl_ref[...] = y


def _final_combine(pos0, pos1, ys, x1, rc, mods, fn):
    tc = TOK_TILE
    row = lambda i, p0, p1: (i, 0)
    n_ctx_tiles = N_CTX // tc
    return pl.pallas_call(
        _final_kernel,
        grid_spec=pltpu.PrefetchScalarGridSpec(
            num_scalar_prefetch=2,
            grid=(N_TOK // tc,),
            in_specs=[pl.BlockSpec(memory_space=pl.ANY),
                      pl.BlockSpec((tc, D_MODEL), row),
                      pl.BlockSpec((tc, LANE), row),
                      pl.BlockSpec((1, 6, D_MODEL), lambda i, p0, p1: (_mod_row(i, tc), 0, 0)),
                      pl.BlockSpec((1, D_MODEL), lambda i, p0, p1: (0, 0))],
            out_specs=[pl.BlockSpec((tc, D_MODEL), lambda i, p0, p1: (jnp.minimum(i, n_ctx_tiles - 1), 0)),
                       pl.BlockSpec((tc, D_MODEL), lambda i, p0, p1: (jnp.maximum(i - n_ctx_tiles, 0), 0))],
            scratch_shapes=[pltpu.VMEM((2, 2, tc, D_MODEL), f32), pltpu.SemaphoreType.DMA((2,))]),
        out_shape=[jax.ShapeDtypeStruct((N_CTX, D_MODEL), f32), jax.ShapeDtypeStruct((N_LAT, D_MODEL), f32)],
        compiler_params=pltpu.CompilerParams(dimension_semantics=("arbitrary",), vmem_limit_bytes=VMEM_LIMIT),
        name="moe_combine_final",
    )(pos0, pos1, ys, x1, rc, mods, fn)


def _moe_experts(h2, route_t, w_gate, w_up, w_down, layer, xs_buf):
    rows, te, runs = _rank(route_t)
    xs = _dispatch(runs[:, :3, :N_EXPERTS].reshape(-1), h2, rows, xs_buf)
    return rows[0], rows[1], _experts(te, xs, w_gate, w_up, w_down, layer), xs


def _pad_heads(w):
    lead = w.shape[:-1]
    w = w.reshape(*lead, ML_HEADS, ML_DIM)
    w = jnp.pad(w, [(0, 0)] * len(lead) + [(0, 0), (0, ML_PAD - ML_DIM)])
    return w.reshape(*lead, ML_PW)


def _pack_in_cols(wb):
    o = 0
    qa = wb[..., o:o + NA_WIDTH] * (NA_DIM ** -0.5)
    ka = wb[..., o + NA_WIDTH:o + 2 * NA_WIDTH]
    va = wb[..., o + 2 * NA_WIDTH:o + 3 * NA_WIDTH]
    o += 3 * NA_WIDTH
    qb, kb, vb, ob = [_pad_heads(wb[..., o + j * ML_WIDTH:o + (j + 1) * ML_WIDTH]) for j in range(4)]
    o += 4 * ML_WIDTH
    gates = wb[..., o:o + N_GATE_COLS]
    o += N_GATE_COLS
    pool = wb[..., o:o + POOL_WIDTH]
    main = jnp.concatenate([qa, ka, va, qb, vb, ob, pool], axis=-1)
    gates_p = jnp.pad(gates, [(0, 0)] * (gates.ndim - 1) + [(0, LANE - N_GATE_COLS)])
    return main, gates_p, jnp.concatenate([kb, gates], axis=-1)


def _pack_w_in(w, b):
    w_main, w_gates, w_feat = _pack_in_cols(w)
    b_main, b_gates, b_feat = _pack_in_cols(b.astype(f32))
    return (w_main.astype(bf16), b_main[:, None], w_gates.astype(bf16), b_gates[:, None],
            jnp.swapaxes(w_feat, 1, 2).astype(bf16), b_feat[:, :, None])


def _pack_w_out(w):
    n_l = w.shape[0]
    wb = w[:, NA_WIDTH:NA_WIDTH + ML_WIDTH].reshape(n_l, ML_HEADS, ML_DIM, D_MODEL)
    wb = jnp.pad(wb, ((0, 0), (0, 0), (0, ML_PAD - ML_DIM), (0, 0))).reshape(n_l, ML_PW, D_MODEL)
    return jnp.concatenate([w[:, :NA_WIDTH], wb, w[:, NA_WIDTH + ML_WIDTH:]], axis=1).astype(bf16)


def _block_diag(w):
    n_l, g, c, _ = w.shape
    eye = jnp.eye(g, dtype=w.dtype)
    return (eye[None, :, None, :, None] * w[:, :, :, None, :]).reshape(n_l, g * c, g * c)


def kernel(x_prompt, x_sample, cache_k_attn, cache_v_attn, state_mlstm_C, state_mlstm_n, state_mlstm_m, c, c_ctx,
           w_ada, b_ada, norm1, w_in, b_in, rpb, ml_norm, w_pool, pool_scale, w_out, norm2, w_router, b_router,
           w_gate, w_up, w_down, final_norm):
    dt = x_prompt.dtype
    x_ctx = x_prompt.reshape(N_CTX, D_MODEL).astype(f32)
    x_lat = x_sample.reshape(N_LAT, D_MODEL).astype(f32)
    x_lat_block0 = 0
    cvec = jnp.concatenate([c_ctx[None], c, jnp.zeros((8 - 1 - DEC_BATCH, D_MODEL), c.dtype)], axis=0).astype(f32)
    mods_all = _ada(cvec, w_ada.astype(f32), b_ada.astype(f32))
    mods_all = mods_all[:, :1 + DEC_BATCH].reshape(DEPTH, 1 + DEC_BATCH, 6, D_MODEL)

    wr_t = w_router.astype(f32).T
    br_t = b_router.astype(f32)[:, None]
    fn = final_norm.astype(f32)[None]

    na_bias = _na_bias_tables(rpb)
    xs_buf = jnp.zeros((MOE_ROWS, D_MODEL), bf16)
    in_params = (norm1.astype(f32)[:, None],) + _pack_w_in(w_in, b_in)
    out_params = (_block_diag(w_pool.astype(f32)).astype(bf16), pool_scale.astype(f32)[:, None],
                  _pad_heads(ml_norm.astype(f32))[:, None], _pack_w_out(w_out), norm2.astype(f32)[:, None])

    new_k = jnp.zeros((BATCH, DEPTH, SEQ, NA_WIDTH), f32)
    new_v = jnp.zeros_like(new_k)
    Cs, ns, ms = [], [], []
    pending = None
    for l in range(DEPTH):
        mods = mods_all[l]
        if pending is None:
            qkva, new_k, new_v, qvo, kt, gates, gates_t, pin = _in_proj(x_ctx, x_lat, mods, l, *in_params,
                                                                        new_k, new_v)
        else:
            x, qkva, new_k, new_v, qvo, kt, gates, gates_t, pin = _moe_in_proj(
                *pending, mods_all[l - 1], mods, l, *in_params, new_k, new_v)
            x_ctx, x_lat, x_lat_block0 = x, x, N_CTX // TOK_TILE

        oa_ctx = _ctx_attention(qkva.reshape(N_TOK // SEQ, SEQ, W_A))
        ck = (cache_k_attn[:, l].reshape(DEC_BATCH, PAST_LEN, NA_WIDTH)).astype(bf16)
        cv = (cache_v_attn[:, l].reshape(DEC_BATCH, PAST_LEN, NA_WIDTH)).astype(bf16)
        oa_lat = _neighborhood_attention(qkva.reshape(N_TOK // DEC_SEQ, DEC_SEQ, W_A), ck, cv, na_bias, l)

        c_l, m_l = _pack_ml_state(state_mlstm_C[:, l], state_mlstm_n[:, l], state_mlstm_m[:, l])
        hf, hb, c_fin, m_fin = _mlstm(qvo, kt, gates, gates_t, c_l, m_l)
        C_l, n_l, m_l2 = _unpack_ml_state(c_fin[:BATCH], m_fin[:BATCH])
        Cs.append(C_l)
        ns.append(n_l)
        ms.append(m_l2)

        x1, h2, route_t, rc = _out_proj(x_ctx, x_lat, x_lat_block0, mods, l,
                                        oa_ctx.reshape(N_CTX, NA_WIDTH), oa_lat.reshape(N_LAT, NA_WIDTH),
                                        hf.reshape(N_TOK, ML_PW), hb.reshape(N_TOK, ML_PW), qvo, pin,
                                        *out_params, wr_t, br_t)
        pos0, pos1, ys, xs_buf = _moe_experts(h2, route_t, w_gate, w_up, w_down, l, xs_buf)
        pending = (pos0, pos1, ys, x1, rc)

    x = _final_combine(*pending, mods_all[DEPTH - 1], fn)
    y_prompt = x[0].reshape(BATCH, SEQ, D_MODEL).astype(dt)
    y_sample = x[1].reshape(DEC_BATCH, DEC_SEQ, D_MODEL).astype(dt)
    new_k, new_v = (a.reshape(BATCH, DEPTH, SEQ, NA_HEADS, NA_DIM).astype(dt) for a in (new_k, new_v))
    return (y_prompt, y_sample, new_k, new_v,
            jnp.stack(Cs, axis=1).astype(dt), jnp.stack(ns, axis=1).astype(dt), jnp.stack(ms, axis=1).astype(dt))
```

```python
import functools

import numpy as np
import jax
import jax.numpy as jnp
from jax import lax
from jax.experimental import pallas as pl
from jax.experimental.pallas import tpu as pltpu

D_MODEL = 1024
BATCH = 16
SEQ = 256
DEPTH = 4
DEC_BATCH = 2
DEC_SEQ = 4096
PAST_LEN = 256
GRID_W = 64
EPS = 1e-6
NEG_INF = -1e30
NA_HEADS = 6
NA_DIM = 64
NA_WIDTH = NA_HEADS * NA_DIM
NA_ROWS = 8
NA_COLS = 16
RPB_ROWS = 2 * NA_ROWS - 1
RPB_COLS = 2 * NA_COLS - 1
ML_HEADS = 4
ML_DIM = 96
ML_WIDTH = ML_HEADS * ML_DIM
POOL_WINDOWS = (2, 4, 8, 16)
POOL_GROUPS = 4
POOL_DIM = 64
POOL_WIDTH = POOL_GROUPS * POOL_DIM
N_GATE_COLS = 4 * ML_HEADS
N_EXPERTS = 16
N_EXPERT_GROUPS = 4
EXPERTS_PER_GROUP = N_EXPERTS // N_EXPERT_GROUPS
D_EXPERT = 512
ADA_DIM = 6 * D_MODEL

N_CTX = BATCH * SEQ
N_LAT = DEC_BATCH * DEC_SEQ
N_TOK = N_CTX + N_LAT
LANE = 128
ML_PAD = LANE
ML_PW = ML_HEADS * ML_PAD
CAUG = ML_PAD
NA_PAIRS = NA_HEADS // 2
TOK_TILE = 512
ML_CHUNK = 256
NA_QROWS = 4
NA_KROWS = NA_QROWS + NA_ROWS - 1
POOL_HALO = max(POOL_WINDOWS) // 2
POOL_BLOCK = 128
MOE_TILE = 512
MOE_CHUNK = 16
MOE_LOCAL_ROWS = -(-(2 * TOK_TILE + N_EXPERTS * (MOE_CHUNK - 1)) // LANE) * LANE
MOE_ROWS = -(-(2 * N_TOK + (N_TOK // TOK_TILE) * N_EXPERTS * (MOE_CHUNK - 1) + N_EXPERTS * (MOE_TILE - 1))
             // MOE_TILE) * MOE_TILE
VMEM_LIMIT = 56 * 1024 * 1024

W_A = 3 * NA_WIDTH
W_B = 3 * ML_PW
N_TCOLS = ML_PW + N_GATE_COLS
W_MAIN = W_A + W_B + POOL_WIDTH

f32 = jnp.float32
bf16 = jnp.bfloat16
HI = lax.Precision.HIGHEST


def _nt(a, b, **kw):
    return lax.dot_general(a, b, (((1,), (1,)), ((), ())), preferred_element_type=f32, **kw)


def _mod_row(i, tile):
    n_ctx_tiles = N_CTX // tile
    per_batch = DEC_SEQ // tile
    return jnp.where(i < n_ctx_tiles, 0, 1 + (i - n_ctx_tiles) // per_batch)


def _ada_kernel(c_ref, w_ref, b_ref, o_ref):
    s = c_ref[...]
    s = s * jax.nn.sigmoid(s)
    o_ref[0] = jnp.dot(s.astype(bf16), w_ref[0].astype(bf16), preferred_element_type=f32) + b_ref[0]


def _ada(cvec, w_ada, b_ada):
    nj = ADA_DIM // D_MODEL
    return pl.pallas_call(
        _ada_kernel,
        grid=(DEPTH, nj),
        in_specs=[pl.BlockSpec((8, D_MODEL), lambda l, j: (0, 0)),
                  pl.BlockSpec((1, D_MODEL, D_MODEL), lambda l, j: (l, 0, j)),
                  pl.BlockSpec((1, 1, D_MODEL), lambda l, j: (l, 0, j))],
        out_specs=pl.BlockSpec((1, 8, D_MODEL), lambda l, j: (l, 0, j)),
        out_shape=jax.ShapeDtypeStruct((DEPTH, 8, ADA_DIM), f32),
        name="ada_mod",
    )(cvec, w_ada, b_ada.reshape(DEPTH, 1, ADA_DIM))


def _in_tile(step):
    return (step + N_CTX // TOK_TILE) % (N_TOK // TOK_TILE)


def _in_kernel(xc_ref, xl_ref, *refs):
    is_ctx = _in_tile(pl.program_id(0)) < N_CTX // xc_ref.shape[0]
    _in_body(jnp.where(is_ctx, xc_ref[...], xl_ref[...]), *refs)


def _moe_in_kernel(pos0_ref, pos1_ref, ys_ref, x1_ref, rc_ref, mod_prev_ref, *refs):
    in_refs, x_out_ref, out_refs, (buf, sem) = refs[:10], refs[10], refs[11:-2], refs[-2:]
    x = _moe_residual(pos0_ref, pos1_ref, ys_ref, x1_ref, rc_ref, mod_prev_ref, buf, sem, tile_of=_in_tile)
    x_out_ref[...] = x
    _in_body(x, *in_refs, *out_refs)


def _in_body(x, mod_ref, n1_ref, w_ref, b_ref, wg_ref, bg_ref, wt_ref, bt_ref, k_in_ref, v_in_ref,
             a_ref, k_ref, v_ref, b_out_ref, kt_ref, g_ref, gt_ref, pin_ref):
    del k_in_ref, v_in_ref
    mod = mod_ref[0]
    h = x * lax.rsqrt(jnp.mean(x * x, axis=-1, keepdims=True) + EPS) * n1_ref[...]
    h = (h * (1.0 + mod[1:2]) + mod[0:1]).astype(bf16)
    pa = jnp.dot(h, w_ref[:, 0:W_A], preferred_element_type=f32) + b_ref[:, 0:W_A]
    a_ref[...] = pa.astype(bf16)
    k_ref[...] = pa[:, NA_WIDTH:2 * NA_WIDTH].reshape(k_ref.shape)
    v_ref[...] = pa[:, 2 * NA_WIDTH:W_A].reshape(v_ref.shape)
    for j in range(3):
        lo = W_A + j * ML_PW
        pb = jnp.dot(h, w_ref[:, lo:lo + ML_PW], preferred_element_type=f32) + b_ref[:, lo:lo + ML_PW]
        if j == 0:
            pb = pb * (ML_DIM ** -0.5)
        b_out_ref[:, j * ML_PW:(j + 1) * ML_PW] = pb.astype(bf16)
    lo = W_A + W_B
    pin_ref[...] = jnp.dot(h, w_ref[:, lo:lo + POOL_WIDTH], preferred_element_type=f32) + b_ref[:, lo:lo + POOL_WIDTH]
    g_ref[...] = jnp.dot(h, wg_ref[...], preferred_element_type=f32) + bg_ref[...]
    t = _nt(wt_ref[...], h) + bt_ref[...]
    kt_ref[...] = t[0:ML_PW].astype(bf16)
    gt_ref[...] = t[ML_PW:N_TCOLS]


def _in_proj_specs(layer):
    tm = TOK_TILE
    lyr = lambda shape: pl.BlockSpec((None,) + shape, lambda i, *_: (layer, 0, 0))
    rows = lambda width: pl.BlockSpec((tm, width), lambda i, *_: (_in_tile(i), 0))
    cols = lambda height: pl.BlockSpec((height, tm), lambda i, *_: (0, _in_tile(i)))
    param_specs = [pl.BlockSpec((1, 6, D_MODEL), lambda i, *_: (_mod_row(_in_tile(i), tm), 0, 0)),
                   lyr((1, D_MODEL)), lyr((D_MODEL, W_MAIN)), lyr((1, W_MAIN)), lyr((D_MODEL, LANE)), lyr((1, LANE)),
                   lyr((N_TCOLS, D_MODEL)), lyr((N_TCOLS, 1)),
                   pl.BlockSpec(memory_space=pl.ANY), pl.BlockSpec(memory_space=pl.ANY)]
    n_ctx_tiles = N_CTX // tm
    kv_spec = pl.BlockSpec((tm // SEQ, None, SEQ, NA_WIDTH),
                           lambda i, *_: (jnp.where(_in_tile(i) < n_ctx_tiles, _in_tile(i), 0), layer, 0, 0))
    kv_shape = jax.ShapeDtypeStruct((BATCH, DEPTH, SEQ, NA_WIDTH), f32)
    out_specs = [rows(W_A), kv_spec, kv_spec, rows(W_B), cols(ML_PW), rows(LANE), cols(N_GATE_COLS),
                 rows(POOL_WIDTH)]
    out_shape = [jax.ShapeDtypeStruct((N_TOK, W_A), bf16), kv_shape, kv_shape,
                 jax.ShapeDtypeStruct((N_TOK, W_B), bf16),
                 jax.ShapeDtypeStruct((ML_PW, N_TOK), bf16),
                 jax.ShapeDtypeStruct((N_TOK, LANE), f32),
                 jax.ShapeDtypeStruct((N_GATE_COLS, N_TOK), f32),
                 jax.ShapeDtypeStruct((N_TOK, POOL_WIDTH), f32)]
    return rows, param_specs, out_specs, out_shape


def _in_proj(x_ctx, x_lat, mods, layer, *params):
    tm = TOK_TILE
    n_ctx_tiles = N_CTX // tm
    rows, param_specs, out_specs, out_shape = _in_proj_specs(layer)
    return pl.pallas_call(
        _in_kernel,
        grid=(N_TOK // tm,),
        in_specs=[pl.BlockSpec((tm, D_MODEL), lambda i: (jnp.where(_in_tile(i) < n_ctx_tiles, _in_tile(i), 0), 0)),
                  pl.BlockSpec((tm, D_MODEL), lambda i: (jnp.maximum(_in_tile(i) - n_ctx_tiles, 0), 0))] + param_specs,
        out_specs=out_specs,
        out_shape=out_shape,
        input_output_aliases={2 + len(param_specs) - 2: 1, 2 + len(param_specs) - 1: 2},
        compiler_params=pltpu.CompilerParams(dimension_semantics=("arbitrary",), vmem_limit_bytes=VMEM_LIMIT),
        name="in_proj",
    )(x_ctx, x_lat, mods, *params)


def _moe_in_proj(pos0, pos1, ys, x1, rc, mods_prev, mods, layer, *params):
    tm = TOK_TILE
    rows, param_specs, out_specs, out_shape = _in_proj_specs(layer)
    return pl.pallas_call(
        _moe_in_kernel,
        grid_spec=pltpu.PrefetchScalarGridSpec(
            num_scalar_prefetch=2,
            grid=(N_TOK // tm,),
            in_specs=[pl.BlockSpec(memory_space=pl.ANY), rows(D_MODEL), rows(LANE),
                      pl.BlockSpec((1, 6, D_MODEL), lambda i, *_: (_mod_row(_in_tile(i), tm), 0, 0))] + param_specs,
            out_specs=[rows(D_MODEL)] + out_specs,
            scratch_shapes=[pltpu.VMEM((2, 2, tm, D_MODEL), f32), pltpu.SemaphoreType.DMA((2,))]),
        out_shape=[jax.ShapeDtypeStruct((N_TOK, D_MODEL), f32)] + out_shape,
        input_output_aliases={6 + len(param_specs) - 2: 2, 6 + len(param_specs) - 1: 3},
        compiler_params=pltpu.CompilerParams(dimension_semantics=("arbitrary",), vmem_limit_bytes=VMEM_LIMIT),
        name="moe_combine_in_proj",
    )(pos0, pos1, ys, x1, rc, mods_prev, mods, *params)


def _pair_attention(qp, parts):
    lane = lax.broadcasted_iota(jnp.int32, (1, LANE), 1)
    outs = []
    for j in range(2):
        in_half = (lane >= j * NA_DIM) & (lane < (j + 1) * NA_DIM)
        qm = jnp.where(in_half, qp, jnp.zeros_like(qp))
        scores = []
        for k, _, bias in parts:
            s = _nt(qm, k)
            if bias is not None:
                s = s + bias[j]
            scores.append(s)
        m = scores[0].max(axis=-1, keepdims=True)
        for s in scores[1:]:
            m = jnp.maximum(m, s.max(axis=-1, keepdims=True))
        den = None
        acc = None
        for s, (_, v, _) in zip(scores, parts):
            p = jnp.exp(s - m)
            ps = p.sum(axis=-1, keepdims=True)
            den = ps if den is None else den + ps
            o = jnp.dot(p.astype(bf16), v, preferred_element_type=f32)
            acc = o if acc is None else acc + o
        outs.append(acc / den)
    return jnp.where(lane < NA_DIM, outs[0], outs[1])


def _ctx_attn_kernel(q_ref, k_ref, v_ref, o_ref):
    for p in range(NA_PAIRS):
        sl = slice(p * LANE, (p + 1) * LANE)
        o = _pair_attention(q_ref[0, :, sl], [(k_ref[0, :, sl], v_ref[0, :, sl], None)])
        o_ref[0, :, sl] = o.astype(bf16)


def _ctx_attention(qkv):
    blk = lambda c: pl.BlockSpec((1, SEQ, NA_WIDTH), lambda b, c=c: (b, 0, c))
    return pl.pallas_call(
        _ctx_attn_kernel,
        grid=(BATCH,),
        in_specs=[blk(0), blk(1), blk(2)],
        out_specs=pl.BlockSpec((1, SEQ, NA_WIDTH), lambda b: (b, 0, 0)),
        out_shape=jax.ShapeDtypeStruct((BATCH, SEQ, NA_WIDTH), bf16),
        name="ctx_attention",
    )(qkv, qkv, qkv)


def _na_window_start(rb):
    return jnp.clip(rb * NA_QROWS - NA_ROWS // 2, 0, DEC_SEQ // GRID_W - NA_KROWS)


def _na_bias(tab_ref, head, rb):
    rows = DEC_SEQ // GRID_W
    ws = _na_window_start(rb)
    lane = lax.broadcasted_iota(jnp.int32, (1, NA_KROWS * GRID_W), 1)
    per_qrow = []
    for dq in range(NA_QROWS):
        qr = rb * NA_QROWS + dq
        a0 = ws - qr + (NA_ROWS - 1) + NA_KROWS
        tiles = [tab_ref[head, a0 + 2 * j] for j in range((NA_KROWS + 1) // 2)]
        t = jnp.concatenate(tiles, axis=1)[:, :NA_KROWS * GRID_W]
        lo = (jnp.clip(qr - NA_ROWS // 2, 0, rows - NA_ROWS) - ws) * GRID_W
        ok = (lane >= lo) & (lane < lo + NA_ROWS * GRID_W)
        per_qrow.append(jnp.where(ok, t, NEG_INF))
    return jnp.concatenate(per_qrow, axis=0)


def _na_kernel(q_ref, k_ref, v_ref, ck_ref, cv_ref, tab_ref, o_ref):
    rb = pl.program_id(1)
    start = pl.multiple_of(_na_window_start(rb) * GRID_W, GRID_W)
    nk = NA_KROWS * GRID_W
    for p in range(NA_PAIRS):
        sl = slice(p * LANE, (p + 1) * LANE)
        bias = [_na_bias(tab_ref.at[0], 2 * p + j, rb) for j in range(2)]
        parts = [(k_ref[0, pl.ds(start, nk), sl], v_ref[0, pl.ds(start, nk), sl], bias),
                 (ck_ref[0, :, sl], cv_ref[0, :, sl], None)]
        o = _pair_attention(q_ref[0, :, sl], parts)
        o_ref[0, :, sl] = o.astype(bf16)


def _neighborhood_attention(qkv, ck, cv, tables, layer):
    nq = NA_QROWS * GRID_W
    n_rb = DEC_SEQ // nq
    return pl.pallas_call(
        _na_kernel,
        grid=(DEC_BATCH, n_rb),
        in_specs=[pl.BlockSpec((1, nq, NA_WIDTH), lambda b, r: (1 + b, r, 0)),
                  pl.BlockSpec((1, DEC_SEQ, NA_WIDTH), lambda b, r: (1 + b, 0, 1)),
                  pl.BlockSpec((1, DEC_SEQ, NA_WIDTH), lambda b, r: (1 + b, 0, 2)),
                  pl.BlockSpec((1, PAST_LEN, NA_WIDTH), lambda b, r: (b, 0, 0)),
                  pl.BlockSpec((1, PAST_LEN, NA_WIDTH), lambda b, r: (b, 0, 0)),
                  pl.BlockSpec((1,) + tables.shape[1:], lambda b, r: (layer, 0, 0, 0, 0))],
        out_specs=pl.BlockSpec((1, nq, NA_WIDTH), lambda b, r: (b, r, 0)),
        out_shape=jax.ShapeDtypeStruct((DEC_BATCH, DEC_SEQ, NA_WIDTH), bf16),
        compiler_params=pltpu.CompilerParams(dimension_semantics=("arbitrary", "arbitrary"),
                                             vmem_limit_bytes=VMEM_LIMIT),
        name="neighborhood_attention",
    )(qkv, qkv, qkv, ck, cv, tables)


def _na_bias_tables(rpb):
    qc = np.arange(GRID_W)[:, None]
    kc = np.arange(GRID_W)[None, :]
    dc = np.clip(kc - qc + NA_COLS - 1, 0, RPB_COLS - 1)
    col_start = np.clip(qc - NA_COLS // 2, 0, GRID_W - NA_COLS)
    col_ok = (kc >= col_start) & (kc < col_start + NA_COLS)
    pick_col = (dc[None] == np.arange(RPB_COLS)[:, None, None]).astype(np.float32)
    rpb_pad = jnp.pad(rpb.astype(f32), ((0, 0), (0, 0), (NA_KROWS, NA_KROWS + 1), (0, 0)))
    n_a = rpb_pad.shape[2] - 1
    rows2 = jnp.stack([rpb_pad[:, :, :-1], rpb_pad[:, :, 1:]], axis=3)
    pick2 = np.zeros((2, RPB_COLS, GRID_W, 2 * GRID_W), np.float32)
    for j in range(2):
        pick2[j, :, :, j * GRID_W:(j + 1) * GRID_W] = pick_col
    tiles = jnp.einsum('lhajb,jbqc->lhaqc', rows2, pick2, precision=HI)
    a_pad = np.arange(n_a)[:, None] + np.arange(2)[None, :]
    row_ok = (a_pad >= NA_KROWS) & (a_pad < NA_KROWS + RPB_ROWS)
    ok = (row_ok[:, None, :, None] & col_ok[None, :, None, :]).reshape(n_a, GRID_W, 2 * GRID_W)
    return jnp.where(ok[None, None], tiles, NEG_INF)


def _log_sigmoid(x):
    return -(jnp.maximum(-x, 0.0) + jnp.log(1.0 + jnp.exp(-jnp.abs(x))))


def _split3(x):
    hi = x.astype(bf16)
    r1 = x - hi.astype(f32)
    mid = r1.astype(bf16)
    lo = (r1 - mid.astype(f32)).astype(bf16)
    return hi, mid, lo


def _mlstm_kernel(qf_ref, vf_ref, ktf_ref, gf_ref, gtf_ref, qb_ref, vb_ref, ktb_ref, gb_ref, gtb_ref,
                  c0_ref, m0_ref, hf_ref, hb_ref, c_out_ref, m_out_ref, c_scr, m_scr):
    L = ML_CHUNK
    seq, c, n_chunks, _ = _ml_schedule(pl.program_id(0))

    @pl.when(c == 0)
    def _():
        is_ctx = seq < BATCH
        c_scr[...] = jnp.where(is_ctx, 0.0, c0_ref[0])
        m_scr[...] = jnp.where(is_ctx, 0.0, m0_ref[0])

    ri = lax.broadcasted_iota(jnp.int32, (L, L), 0)
    ci = lax.broadcasted_iota(jnp.int32, (L, L), 1)
    lane = lax.broadcasted_iota(jnp.int32, (L, ML_PAD), 1)
    is_ncol = lane == ML_DIM
    lower = ri >= ci
    upper = ri <= ci
    lower_b = jnp.where(lower, 1.0, 0.0).astype(bf16)
    upper_b = jnp.where(upper, 1.0, 0.0).astype(bf16)
    dirs = ((qf_ref, ktf_ref, vf_ref, gf_ref, gtf_ref, hf_ref), (qb_ref, ktb_ref, vb_ref, gb_ref, gtb_ref, hb_ref))
    for d, (q_ref, kt_ref, v_ref, g_ref, gt_ref, h_ref) in enumerate(dirs):
        g = g_ref[...][:, 0:N_GATE_COLS]
        gt = gt_ref[...]
        lf_c = _log_sigmoid(g)
        lf_r = _log_sigmoid(gt)
        b_cols = sum(jnp.dot(lower_b, part, preferred_element_type=f32) for part in _split3(lf_c))
        b_rows = sum(jnp.dot(part, upper_b, preferred_element_type=f32) for part in _split3(lf_r))
        tot_c = jnp.sum(lf_c, axis=0, keepdims=True)
        tot_r = jnp.sum(lf_r, axis=1, keepdims=True)
        visible = lower
        if d == 1:
            b_cols = tot_c - b_cols + lf_c
            b_rows = tot_r - b_rows + lf_r
            visible = upper
        for hd in range(ML_HEADS):
            st = d * ML_HEADS + hd
            ci_ = 2 * ML_HEADS * d + hd
            cf_ = ci_ + ML_HEADS
            sl = slice(hd * ML_PAD, (hd + 1) * ML_PAD)
            bc = b_cols[:, cf_:cf_ + 1]
            br = b_rows[cf_:cf_ + 1, :]
            li_r = gt[ci_:ci_ + 1, :]
            m_prev = m_scr[st:st + 1, 0:1]
            dmat = jnp.where(visible, bc - br + li_r, NEG_INF)
            inter = bc + m_prev
            m_t = jnp.maximum(inter, dmat.max(axis=-1, keepdims=True))
            w_intra = jnp.exp(dmat - m_t)
            w_inter = jnp.exp(inter - m_t)
            qh = q_ref[0, :, sl]
            kht = kt_ref[sl, :]
            v_aug = jnp.where(is_ncol, jnp.ones((), bf16), v_ref[0, :, sl])
            s = (jnp.dot(qh, kht, preferred_element_type=f32) * w_intra).astype(bf16)
            c_aug = c_scr[st]
            na = (w_inter * jnp.dot(qh, c_aug.astype(bf16), preferred_element_type=f32)
                  + jnp.dot(s, v_aug, preferred_element_type=f32))
            den = na[:, ML_DIM:ML_DIM + 1]
            h_ref[0, :, sl] = jnp.where(lane < ML_DIM, na / jnp.maximum(jnp.abs(den), jnp.exp(-m_t)), 0.0)
            b_end = tot_r[cf_:cf_ + 1, :]
            g_row = b_end - br + li_r
            m_new = jnp.maximum(b_end + m_prev, g_row.max(axis=1, keepdims=True))
            decay = jnp.exp(b_end + m_prev - m_new)
            kwt = (kht.astype(f32) * jnp.exp(g_row - m_new)).astype(bf16)
            c_scr[st] = decay * c_aug + jnp.dot(kwt, v_aug, preferred_element_type=f32)
            m_scr[st:st + 1, :] = jnp.broadcast_to(m_new, (1, LANE))

    @pl.when(c == n_chunks - 1)
    def _():
        c_out_ref[0] = c_scr[...]
        m_out_ref[0] = m_scr[...]


def _ml_schedule(s):
    nc_ctx, nc_lat = SEQ // ML_CHUNK, DEC_SEQ // ML_CHUNK
    n_ctx_steps = BATCH * nc_ctx
    is_ctx = s < n_ctx_steps
    t = s - n_ctx_steps
    seq = jnp.where(is_ctx, s // nc_ctx, BATCH + t // nc_lat)
    c = jnp.where(is_ctx, s % nc_ctx, t % nc_lat)
    nc = jnp.where(is_ctx, nc_ctx, nc_lat)
    base = jnp.where(is_ctx, (s // nc_ctx) * nc_ctx, n_ctx_steps + (t // nc_lat) * nc_lat)
    return seq, c, nc, base


def _mlstm(qvo, kt, gates, gates_t, c0, m0):
    L = ML_CHUNK
    n_seq = BATCH + DEC_BATCH

    def fwd(s):
        _, c, _, base = _ml_schedule(s)
        return base + c

    def bwd(s):
        _, c, nc, base = _ml_schedule(s)
        return base + nc - 1 - c

    seq_of = lambda s: _ml_schedule(s)[0]
    lat_of = lambda s: jnp.maximum(seq_of(s) - BATCH, 0)

    def specs(pos):
        return [pl.BlockSpec((1, L, ML_PW), lambda s, j=j: (pos(s), 0, j)) for j in range(2)] + [
            pl.BlockSpec((ML_PW, L), lambda s: (0, pos(s))),
            pl.BlockSpec((L, LANE), lambda s: (pos(s), 0)),
            pl.BlockSpec((N_GATE_COLS, L), lambda s: (0, pos(s)))]

    q3 = qvo.reshape(N_TOK // L, L, W_B)
    n_str = 2 * ML_HEADS
    return pl.pallas_call(
        _mlstm_kernel,
        grid=(N_TOK // L,),
        in_specs=specs(fwd) + specs(bwd) + [
            pl.BlockSpec((1, n_str, ML_PAD, CAUG), lambda s: (lat_of(s), 0, 0, 0)),
            pl.BlockSpec((1, n_str, LANE), lambda s: (lat_of(s), 0, 0))],
        out_specs=[pl.BlockSpec((1, L, ML_PW), lambda s: (fwd(s), 0, 0)),
                   pl.BlockSpec((1, L, ML_PW), lambda s: (bwd(s), 0, 0)),
                   pl.BlockSpec((1, n_str, ML_PAD, CAUG), lambda s: (seq_of(s), 0, 0, 0)),
                   pl.BlockSpec((1, n_str, LANE), lambda s: (seq_of(s), 0, 0))],
        out_shape=[jax.ShapeDtypeStruct((N_TOK // L, L, ML_PW), f32),
                   jax.ShapeDtypeStruct((N_TOK // L, L, ML_PW), f32),
                   jax.ShapeDtypeStruct((n_seq, n_str, ML_PAD, CAUG), f32),
                   jax.ShapeDtypeStruct((n_seq, n_str, LANE), f32)],
        scratch_shapes=[pltpu.VMEM((n_str, ML_PAD, CAUG), f32), pltpu.VMEM((n_str, LANE), f32)],
        compiler_params=pltpu.CompilerParams(dimension_semantics=("arbitrary",), vmem_limit_bytes=VMEM_LIMIT),
        name="mlstm",
    )(q3, q3, kt, gates, gates_t, q3, q3, kt, gates, gates_t, c0, m0)


def _pack_ml_state(C, n, m):
    B = C.shape[0]
    c_aug = jnp.zeros((B, 2, ML_HEADS, ML_PAD, CAUG), f32)
    c_aug = c_aug.at[:, :, :, :ML_DIM, :ML_DIM].set(C.astype(f32))
    c_aug = c_aug.at[:, :, :, :ML_DIM, ML_DIM].set(n.astype(f32))
    m_b = jnp.broadcast_to(m.astype(f32)[..., None], (B, 2, ML_HEADS, LANE))
    return c_aug.reshape(B, 2 * ML_HEADS, ML_PAD, CAUG), m_b.reshape(B, 2 * ML_HEADS, LANE)


def _unpack_ml_state(c_aug, m_b):
    B = c_aug.shape[0]
    c_aug = c_aug.reshape(B, 2, ML_HEADS, ML_PAD, CAUG)
    return (c_aug[:, :, :, :ML_DIM, :ML_DIM], c_aug[:, :, :, :ML_DIM, ML_DIM],
            m_b.reshape(B, 2, ML_HEADS, LANE)[..., 0])


def _pool_rows(u_prev, u_cur, u_next, w_bd, scale, t0, seq_len):
    tm = u_cur.shape[0]
    u_win = jnp.concatenate([u_prev, u_cur, u_next], axis=0)
    u_hi = u_win.astype(bf16)
    u_lo = (u_win - u_hi.astype(f32)).astype(bf16)
    lane = lax.broadcasted_iota(jnp.int32, (1, LANE), 1)
    blocks = []
    for r0 in range(0, tm, POOL_BLOCK):
        win = slice(r0, r0 + POOL_BLOCK + 2 * POOL_HALO)
        t_abs = t0 + r0 + lax.broadcasted_iota(jnp.int32, (POOL_BLOCK, 1), 0)
        s_abs = t0 + r0 - POOL_HALO + lax.broadcasted_iota(jnp.int32, (1, POOL_BLOCK + 2 * POOL_HALO), 1)
        t_loc = t_abs & (seq_len - 1)
        seq_start = t_abs - t_loc
        means = []
        for w in POOL_WINDOWS:
            lo = jnp.maximum(t_loc - w // 2, 0)
            hi = jnp.minimum(t_loc - w // 2 + w, seq_len)
            in_win = (s_abs >= seq_start + lo) & (s_abs < seq_start + hi)
            means.append((jnp.where(in_win, 1.0, 0.0).astype(bf16), 1.0 / (hi - lo).astype(f32)))
        pooled = []
        for p in range(POOL_GROUPS // 2):
            sl = slice(p * LANE, (p + 1) * LANE)
            halves = []
            for a, inv_cnt in means[2 * p:2 * p + 2]:
                tot = (jnp.dot(a, u_hi[win, sl], preferred_element_type=f32)
                       + jnp.dot(a, u_lo[win, sl], preferred_element_type=f32))
                halves.append(tot * inv_cnt)
            pooled.append(jnp.where(lane < POOL_DIM, halves[0], halves[1]) - u_cur[r0:r0 + POOL_BLOCK, sl])
        blocks.append(jnp.concatenate(pooled, axis=1))
    pooled = jnp.concatenate(blocks, axis=0).astype(bf16)
    return jnp.dot(pooled, w_bd, preferred_element_type=f32) * scale


def _top2_sum(a, b, c, d):
    hi1, lo1 = jnp.maximum(a, b), jnp.minimum(a, b)
    hi2, lo2 = jnp.maximum(c, d), jnp.minimum(c, d)
    return jnp.maximum(hi1, hi2) + jnp.maximum(jnp.minimum(hi1, hi2), jnp.maximum(lo1, lo2))


def _first_match(vals, target):
    idx = jnp.full_like(target, float(len(vals) - 1))
    for i in range(len(vals) - 2, -1, -1):
        idx = jnp.where(vals[i] == target, float(i), idx)
    return idx


def _pick(vals, idx):
    out = vals[-1]
    for i in range(len(vals) - 2, -1, -1):
        out = jnp.where(idx == float(i), vals[i], out)
    return out


def _route(logits_t, bias_t):
    scores = jax.nn.sigmoid(logits_t)
    sel = scores + bias_t
    row = lambda a, i: a[i:i + 1, :]
    grp = [_top2_sum(*[row(sel, EXPERTS_PER_GROUP * g + i) for i in range(EXPERTS_PER_GROUP)])
           for g in range(N_EXPERT_GROUPS)]
    best = functools.reduce(jnp.maximum, grp)
    gidx = _first_match(grp, best)
    sel_g = [_pick([row(sel, EXPERTS_PER_GROUP * g + i) for g in range(N_EXPERT_GROUPS)], gidx)
             for i in range(EXPERTS_PER_GROUP)]
    sco_g = [_pick([row(scores, EXPERTS_PER_GROUP * g + i) for g in range(N_EXPERT_GROUPS)], gidx)
             for i in range(EXPERTS_PER_GROUP)]
    i0 = _first_match(sel_g, functools.reduce(jnp.maximum, sel_g))
    rest = [jnp.where(i0 == float(i), -jnp.inf, sel_g[i]) for i in range(EXPERTS_PER_GROUP)]
    i1 = _first_match(rest, functools.reduce(jnp.maximum, rest))
    s0, s1 = _pick(sco_g, i0), _pick(sco_g, i1)
    tot = s0 + s1
    rid = lax.broadcasted_iota(jnp.int32, (LANE, logits_t.shape[1]), 0)
    rows = (EXPERTS_PER_GROUP * gidx + i0, EXPERTS_PER_GROUP * gidx + i1, s0 / tot, s1 / tot)
    out = jnp.zeros(rid.shape, f32)
    for i, r in enumerate(rows):
        out = jnp.where(rid == i, r, out)
    return out


def _out_kernel(xc_ref, xl_ref, mod_ref, oac_ref, oal_ref, hf_ref, hb_ref, ob_ref, up_ref, uc_ref, un_ref, wp_ref,
                psc_ref, mln_ref, wo_ref, n2_ref, wr_ref, br_ref, x1_ref, h2_ref, rt_ref, rc_ref):
    tm = xc_ref.shape[0]
    i = pl.program_id(0)
    is_ctx = i < N_CTX // tm
    mod = mod_ref[0]
    out_a = jnp.where(is_ctx, oac_ref[...], oal_ref[...])
    out_c = _pool_rows(up_ref[...], uc_ref[...], un_ref[...], wp_ref[...], psc_ref[...], i * tm,
                       jnp.where(is_ctx, SEQ, DEC_SEQ)).astype(bf16)
    hsum = hf_ref[...] + hb_ref[...]
    outs_b = []
    for hd in range(ML_HEADS):
        sl = slice(hd * ML_PAD, (hd + 1) * ML_PAD)
        hh = hsum[:, sl]
        ms = jnp.sum(hh * hh, axis=-1, keepdims=True) * (1.0 / ML_DIM)
        hn = hh * lax.rsqrt(ms + EPS) * mln_ref[:, sl]
        outs_b.append((jax.nn.sigmoid(ob_ref[:, sl].astype(f32)) * hn).astype(bf16))
    out_b = jnp.concatenate(outs_b, axis=1)
    mixed = (jnp.dot(out_a, wo_ref[0:NA_WIDTH, :], preferred_element_type=f32)
             + jnp.dot(out_b, wo_ref[NA_WIDTH:NA_WIDTH + ML_PW, :], preferred_element_type=f32)
             + jnp.dot(out_c, wo_ref[NA_WIDTH + ML_PW:, :], preferred_element_type=f32))
    x1 = jnp.where(is_ctx, xc_ref[...], xl_ref[...]) + mod[2:3] * mixed
    x1_ref[...] = x1
    h2 = x1 * lax.rsqrt(jnp.mean(x1 * x1, axis=-1, keepdims=True) + EPS) * n2_ref[...]
    h2 = h2 * (1.0 + mod[4:5]) + mod[3:4]
    h2_ref[...] = h2.astype(bf16)
    h_hi = h2.astype(bf16)
    h_lo = (h2 - h_hi.astype(f32)).astype(bf16)
    w_hi = wr_ref[...].astype(bf16)
    w_lo = (wr_ref[...] - w_hi.astype(f32)).astype(bf16)
    route_t = _route(_nt(w_hi, h_hi) + (_nt(w_hi, h_lo) + _nt(w_lo, h_hi)), br_ref[...])
    rt_ref[...] = route_t[0:8]
    rc_ref[...] = route_t.T


def _out_proj(x_ctx, x_lat, x_lat_block0, mods, layer, oa_ctx, oa_lat, hf, hb, qvo, pin, w_bd, psc, mln, wo, n2,
              wr_t, br_t):
    tm = TOK_TILE
    const = lambda i: (0, 0)
    lyr = lambda shape: pl.BlockSpec((None,) + shape, lambda i: (layer, 0, 0))
    row = lambda i: (i, 0)
    n_ctx_tiles = N_CTX // tm
    halo_blocks = tm // POOL_HALO
    return pl.pallas_call(
        _out_kernel,
        grid=(N_TOK // tm,),
        in_specs=[pl.BlockSpec((tm, D_MODEL), lambda i: (jnp.minimum(i, n_ctx_tiles - 1), 0)),
                  pl.BlockSpec((tm, D_MODEL), lambda i: (jnp.maximum(i - n_ctx_tiles, 0) + x_lat_block0, 0)),
                  pl.BlockSpec((1, 6, D_MODEL), lambda i: (_mod_row(i, tm), 0, 0)),
                  pl.BlockSpec((tm, NA_WIDTH), lambda i: (jnp.minimum(i, n_ctx_tiles - 1), 0)),
                  pl.BlockSpec((tm, NA_WIDTH), lambda i: (jnp.maximum(i - n_ctx_tiles, 0), 0)),
                  pl.BlockSpec((tm, ML_PW), row),
                  pl.BlockSpec((tm, ML_PW), row),
                  pl.BlockSpec((tm, ML_PW), lambda i: (i, 2)),
                  pl.BlockSpec((POOL_HALO, POOL_WIDTH), lambda i: (jnp.maximum(i * halo_blocks - 1, 0), 0)),
                  pl.BlockSpec((tm, POOL_WIDTH), row),
                  pl.BlockSpec((POOL_HALO, POOL_WIDTH),
                               lambda i: (jnp.minimum((i + 1) * halo_blocks, N_TOK // POOL_HALO - 1), 0)),
                  lyr((POOL_WIDTH, POOL_WIDTH)), lyr((1, POOL_WIDTH)), lyr((1, ML_PW)),
                  lyr((NA_WIDTH + ML_PW + POOL_WIDTH, D_MODEL)), lyr((1, D_MODEL)),
                  pl.BlockSpec((N_EXPERTS, D_MODEL), const),
                  pl.BlockSpec((N_EXPERTS, 1), const)],
        out_specs=[pl.BlockSpec((tm, D_MODEL), row),
                   pl.BlockSpec((tm, D_MODEL), row),
                   pl.BlockSpec((8, tm), lambda i: (0, i)),
                   pl.BlockSpec((tm, LANE), row)],
        out_shape=[jax.ShapeDtypeStruct((N_TOK, D_MODEL), f32),
                   jax.ShapeDtypeStruct((N_TOK, D_MODEL), bf16),
                   jax.ShapeDtypeStruct((8, N_TOK), f32),
                   jax.ShapeDtypeStruct((N_TOK, LANE), f32)],
        compiler_params=pltpu.CompilerParams(dimension_semantics=("arbitrary",), vmem_limit_bytes=VMEM_LIMIT),
        name="out_proj_router",
    )(x_ctx, x_lat, mods, oa_ctx, oa_lat, hf, hb, qvo, pin, pin, pin, w_bd, psc, mln, wo, n2, wr_t, br_t)


def _ceil_to(x, m):
    return jnp.floor((x + (m - 1)) * (1.0 / m)) * m


def _prefix_over_experts(v):
    er = lax.broadcasted_iota(jnp.int32, (N_EXPERTS, N_EXPERTS), 0)
    ec = lax.broadcasted_iota(jnp.int32, (N_EXPERTS, N_EXPERTS), 1)
    return jnp.dot(jnp.where(ec < er, 1.0, 0.0), v, preferred_element_type=f32, precision=HI)


def _experts_to_lanes(v):
    sub = lax.broadcasted_iota(jnp.int32, (N_EXPERTS, LANE), 0)
    lane = lax.broadcasted_iota(jnp.int32, (N_EXPERTS, LANE), 1)
    return jnp.sum(jnp.where(sub == lane, v, 0.0), axis=0, keepdims=True)


def _expert_hits(rt):
    rid = lax.broadcasted_iota(jnp.int32, (N_EXPERTS, rt.shape[1]), 0).astype(f32)
    oh0 = rid == rt[0:1, :]
    oh1 = rid == rt[1:2, :]
    both = jnp.where(oh0 | oh1, 1.0, 0.0)
    runs = jnp.broadcast_to(_ceil_to(jnp.sum(both, axis=1, keepdims=True), MOE_CHUNK), (N_EXPERTS, LANE))
    return oh0, oh1, both, runs


def _rank_kernel(rt_all_ref, rt_ref, pos_ref, te_ref, tab_ref, carry_ref):
    tm = rt_ref.shape[1]
    step = pl.program_id(0)

    @pl.when(step == 0)
    def _():
        totals = jnp.zeros((N_EXPERTS, LANE), f32)
        for i in range(rt_all_ref.shape[1] // tm):
            totals = totals + _expert_hits(rt_all_ref[:, i * tm:(i + 1) * tm])[3]
        padded = _ceil_to(totals, MOE_TILE)
        off = _prefix_over_experts(padded)
        carry_ref[...] = off
        total = jnp.sum(padded, axis=0, keepdims=True)
        n_used = total * (1.0 / MOE_TILE)
        tile = lax.broadcasted_iota(jnp.int32, (1, LANE), 1).astype(f32)
        row0 = jnp.minimum(tile, n_used - 1.0) * MOE_TILE
        expert = jnp.sum(jnp.where(off <= row0, 1.0, 0.0), axis=0, keepdims=True) - 1.0
        sub = lax.broadcasted_iota(jnp.int32, (8, LANE), 0)
        te_ref[...] = jnp.where(sub == 0, expert, jnp.where(sub == 1, n_used, 0.0)).astype(jnp.int32)

    @pl.when(step > 0)
    def _():
        oh0, oh1, both, runs = _expert_hits(rt_ref[...])
        sr = lax.broadcasted_iota(jnp.int32, (tm, tm), 0)
        sc = lax.broadcasted_iota(jnp.int32, (tm, tm), 1)
        earlier = jnp.dot(both.astype(bf16), jnp.where(sr < sc, 1.0, 0.0).astype(bf16),
                          preferred_element_type=f32)
        g_off = carry_ref[...]
        l_off = _prefix_over_experts(runs)
        g_row = g_off[:, 0:1] + earlier
        l_row = l_off[:, 0:1] + earlier
        pick = lambda oh, v: jnp.sum(jnp.where(oh, v, 0.0), axis=0, keepdims=True)
        rows = (pick(oh0, g_row), pick(oh1, g_row), pick(oh0, l_row), pick(oh1, l_row))
        sub = lax.broadcasted_iota(jnp.int32, (8, tm), 0)
        out = jnp.zeros((8, tm), f32)
        for k, r in enumerate(rows):
            out = jnp.where(sub == k, r, out)
        pos_ref[...] = out.astype(jnp.int32)
        sub = lax.broadcasted_iota(jnp.int32, (8, LANE), 0)
        tab = jnp.zeros((8, LANE), f32)
        for k, v in enumerate((runs * (1.0 / MOE_CHUNK), l_off, g_off)):
            tab = jnp.where(sub == k, _experts_to_lanes(v), tab)
        tab_ref[0] = tab.astype(jnp.int32)
        carry_ref[...] = g_off + runs


def _rank(route_t):
    tm = TOK_TILE
    n_tiles = N_TOK // tm
    tile_of = lambda s: jnp.maximum(s - 1, 0)
    return pl.pallas_call(
        _rank_kernel,
        grid=(1 + n_tiles,),
        in_specs=[pl.BlockSpec((8, N_TOK), lambda s: (0, 0)),
                  pl.BlockSpec((8, tm), lambda s: (0, tile_of(s)))],
        out_specs=[pl.BlockSpec((8, tm), lambda s: (0, tile_of(s))),
                   pl.BlockSpec((8, LANE), lambda s: (0, 0)),
                   pl.BlockSpec((1, 8, LANE), lambda s: (tile_of(s), 0, 0))],
        out_shape=[jax.ShapeDtypeStruct((8, N_TOK), jnp.int32),
                   jax.ShapeDtypeStruct((8, LANE), jnp.int32),
                   jax.ShapeDtypeStruct((n_tiles, 8, LANE), jnp.int32)],
        scratch_shapes=[pltpu.VMEM((N_EXPERTS, LANE), f32)],
        compiler_params=pltpu.CompilerParams(dimension_semantics=("arbitrary",)),
        name="moe_rank",
    )(route_t, route_t)


def _dispatch_kernel(tab_ref, h_ref, rows_ref, xs_in_ref, xs_ref, loc, sem):
    del xs_in_ref
    tm = h_ref.shape[0]
    i = pl.program_id(0)
    slot = i % 2
    rid = lax.broadcasted_iota(jnp.int32, (MOE_LOCAL_ROWS, tm), 0)
    sel = (rid == rows_ref[2:3, :]) | (rid == rows_ref[3:4, :])
    loc[slot] = jnp.dot(jnp.where(sel, 1.0, 0.0).astype(bf16), h_ref[...], preferred_element_type=f32).astype(bf16)

    def chunk_copy(sl, src_row, dst_row):
        return pltpu.make_async_copy(loc.at[sl, pl.ds(pl.multiple_of(src_row, MOE_CHUNK), MOE_CHUNK)],
                                     xs_ref.at[pl.ds(pl.multiple_of(dst_row, MOE_CHUNK), MOE_CHUNK)], sem.at[sl])

    def chunks_of(tile):
        return sum(tab_ref[3 * tile * N_EXPERTS + e] for e in range(N_EXPERTS))

    def wait_chunks(sl, n):
        def wait_one(c, carry):
            chunk_copy(sl, 0, 0).wait()
            return carry

        lax.fori_loop(0, n, wait_one, 0)

    for e in range(N_EXPERTS):
        n_chunks, l_off, g_off = (tab_ref[(3 * i + k) * N_EXPERTS + e] for k in range(3))

        def issue(c, carry, l_off=l_off, g_off=g_off):
            chunk_copy(slot, l_off + c * MOE_CHUNK, g_off + c * MOE_CHUNK).start()
            return carry

        lax.fori_loop(0, n_chunks, issue, 0)

    @pl.when(i > 0)
    def _():
        wait_chunks(1 - slot, chunks_of(i - 1))

    @pl.when(i == pl.num_programs(0) - 1)
    def _():
        wait_chunks(slot, chunks_of(i))


def _dispatch(run_table, h2, rows, xs_init):
    tm = TOK_TILE
    return pl.pallas_call(
        _dispatch_kernel,
        grid_spec=pltpu.PrefetchScalarGridSpec(
            num_scalar_prefetch=1,
            grid=(N_TOK // tm,),
            in_specs=[pl.BlockSpec((tm, D_MODEL), lambda i, tab: (i, 0)),
                      pl.BlockSpec((8, tm), lambda i, tab: (0, i)),
                      pl.BlockSpec(memory_space=pl.ANY)],
            out_specs=pl.BlockSpec(memory_space=pl.ANY),
            scratch_shapes=[pltpu.VMEM((2, MOE_LOCAL_ROWS, D_MODEL), bf16), pltpu.SemaphoreType.DMA((2,))]),
        out_shape=jax.ShapeDtypeStruct(xs_init.shape, xs_init.dtype),
        input_output_aliases={3: 0},
        compiler_params=pltpu.CompilerParams(dimension_semantics=("arbitrary",), vmem_limit_bytes=VMEM_LIMIT),
        name="moe_dispatch",
    )(run_table, h2, rows, xs_init)


def _expert_kernel(te_ref, xs_ref, wg_hbm, wu_hbm, wd_hbm, ys_ref, wg_f32, wu_f32, wd_f32, wg_bf, wu_bf, wd_bf,
                   slot_ref, sem, *, layer):
    j = pl.program_id(0)
    n_used = te_ref[1, 0]
    used = j < n_used
    expert = te_ref[0, j]
    new_expert = jnp.logical_or(j == 0, expert != te_ref[0, jnp.maximum(j - 1, 0)])

    def weight_copies(e, slot):
        return [pltpu.make_async_copy(hbm.at[layer, e], buf.at[slot], sem.at[slot])
                for hbm, buf in ((wg_hbm, wg_f32), (wu_hbm, wu_f32), (wd_hbm, wd_f32))]

    @pl.when(j == 0)
    def _():
        slot_ref[0] = 1
        for cp in weight_copies(expert, 0):
            cp.start()

    @pl.when(jnp.logical_not(used))
    def _():
        ys_ref[...] = jnp.zeros_like(ys_ref)

    @pl.when(used & new_expert)
    def _():
        slot = 1 - slot_ref[0]
        slot_ref[0] = slot
        for cp in weight_copies(expert, slot):
            cp.wait()
        wg_bf[...] = wg_f32[slot].astype(bf16)
        wu_bf[...] = wu_f32[slot].astype(bf16)
        wd_bf[...] = wd_f32[slot].astype(bf16)
        nxt = lax.while_loop(lambda t: (t < n_used) & (te_ref[0, jnp.minimum(t, LANE - 1)] == expert),
                             lambda t: t + 1, j + 1)

        @pl.when(nxt < n_used)
        def _():
            for cp in weight_copies(te_ref[0, nxt], 1 - slot):
                cp.start()

    @pl.when(used)
    def _():
        x = xs_ref[...]
        hg = jnp.dot(x, wg_bf[...], preferred_element_type=f32)
        hu = jnp.dot(x, wu_bf[...], preferred_element_type=f32)
        hid = (hg * jax.nn.sigmoid(hg) * hu).astype(bf16)
        ys_ref[...] = jnp.dot(hid, wd_bf[...], preferred_element_type=f32)


def _experts(te, xs, w_gate, w_up, w_down, layer):
    tm = MOE_TILE
    row = lambda j, te: (jnp.minimum(j, te[1, 0] - 1), 0)
    hbm = pl.BlockSpec(memory_space=pl.ANY)
    return pl.pallas_call(
        functools.partial(_expert_kernel, layer=layer),
        grid_spec=pltpu.PrefetchScalarGridSpec(
            num_scalar_prefetch=1,
            grid=(MOE_ROWS // tm,),
            in_specs=[pl.BlockSpec((tm, D_MODEL), row), hbm, hbm, hbm],
            out_specs=pl.BlockSpec((tm, D_MODEL), lambda j, te: (j, 0)),
            scratch_shapes=[pltpu.VMEM((2, D_MODEL, D_EXPERT), f32), pltpu.VMEM((2, D_MODEL, D_EXPERT), f32),
                            pltpu.VMEM((2, D_EXPERT, D_MODEL), f32),
                            pltpu.VMEM((D_MODEL, D_EXPERT), bf16), pltpu.VMEM((D_MODEL, D_EXPERT), bf16),
                            pltpu.VMEM((D_EXPERT, D_MODEL), bf16),
                            pltpu.SMEM((1,), jnp.int32), pltpu.SemaphoreType.DMA((2,))]),
        out_shape=jax.ShapeDtypeStruct((MOE_ROWS, D_MODEL), f32),
        compiler_params=pltpu.CompilerParams(dimension_semantics=("arbitrary",), vmem_limit_bytes=VMEM_LIMIT),
        name="moe_experts",
    )(te, xs, w_gate, w_up, w_down)


def _gather_expert_rows(pos0_ref, pos1_ref, ys_ref, buf, sem, tile_of):
    rows = buf.shape[2]
    i = pl.program_id(0)
    slot = i % 2

    def issue(tile, sl):
        base = tile * rows

        def body(t, carry):
            for s, pos_ref in enumerate((pos0_ref, pos1_ref)):
                pltpu.make_async_copy(ys_ref.at[pl.ds(pos_ref[base + t], 1)], buf.at[sl, s, pl.ds(t, 1)],
                                      sem.at[sl]).start(priority=s)
            return carry

        lax.fori_loop(0, rows, body, 0, unroll=8)

    @pl.when(i == 0)
    def _():
        issue(tile_of(0), 0)

    @pl.when(i + 1 < pl.num_programs(0))
    def _():
        issue(tile_of(i + 1), 1 - slot)

    for s in range(2):
        pltpu.make_async_copy(ys_ref.at[pl.ds(0, rows)], buf.at[slot, s], sem.at[slot]).wait()
    return buf[slot, 0], buf[slot, 1]


def _moe_residual(pos0_ref, pos1_ref, ys_ref, x1_ref, rc_ref, mod_ref, buf, sem, tile_of=lambda step: step):
    y0, y1 = _gather_expert_rows(pos0_ref, pos1_ref, ys_ref, buf, sem, tile_of)
    rc = rc_ref[...]
    return x1_ref[...] + mod_ref[0][5:6] * (rc[:, 2:3] * y0 + rc[:, 3:4] * y1)


def _final_kernel(pos0_ref, pos1_ref, ys_ref, x1_ref, rc_ref, mod_ref, fn_ref, yc_ref, yl_ref, buf, sem):
    x2 = _moe_residual(pos0_ref, pos1_ref, ys_ref, x1_ref, rc_ref, mod_ref, buf, sem)
    y = x2 * lax.rsqrt(jnp.mean(x2 * x2, axis=-1, keepdims=True) + EPS) * fn_ref[...]
    is_ctx = pl.program_id(0) < N_CTX // x1_ref.shape[0]

    @pl.when(is_ctx)
    def _():
        yc_ref[...] = y

    @pl.when(jnp.logical_not(is_ctx))
    def _():
        yl_ref[...] = y


def _final_combine(pos0, pos1, ys, x1, rc, mods, fn):
    tc = TOK_TILE
    row = lambda i, p0, p1: (i, 0)
    n_ctx_tiles = N_CTX // tc
    return pl.pallas_call(
        _final_kernel,
        grid_spec=pltpu.PrefetchScalarGridSpec(
            num_scalar_prefetch=2,
            grid=(N_TOK // tc,),
            in_specs=[pl.BlockSpec(memory_space=pl.ANY),
                      pl.BlockSpec((tc, D_MODEL), row),
                      pl.BlockSpec((tc, LANE), row),
                      pl.BlockSpec((1, 6, D_MODEL), lambda i, p0, p1: (_mod_row(i, tc), 0, 0)),
                      pl.BlockSpec((1, D_MODEL), lambda i, p0, p1: (0, 0))],
            out_specs=[pl.BlockSpec((tc, D_MODEL), lambda i, p0, p1: (jnp.minimum(i, n_ctx_tiles - 1), 0)),
                       pl.BlockSpec((tc, D_MODEL), lambda i, p0, p1: (jnp.maximum(i - n_ctx_tiles, 0), 0))],
            scratch_shapes=[pltpu.VMEM((2, 2, tc, D_MODEL), f32), pltpu.SemaphoreType.DMA((2,))]),
        out_shape=[jax.ShapeDtypeStruct((N_CTX, D_MODEL), f32), jax.ShapeDtypeStruct((N_LAT, D_MODEL), f32)],
        compiler_params=pltpu.CompilerParams(dimension_semantics=("arbitrary",), vmem_limit_bytes=VMEM_LIMIT),
        name="moe_combine_final",
    )(pos0, pos1, ys, x1, rc, mods, fn)


def _moe_experts(h2, route_t, w_gate, w_up, w_down, layer, xs_buf):
    rows, te, runs = _rank(route_t)
    xs = _dispatch(runs[:, :3, :N_EXPERTS].reshape(-1), h2, rows, xs_buf)
    return rows[0], rows[1], _experts(te, xs, w_gate, w_up, w_down, layer), xs


def _pad_heads(w):
    lead = w.shape[:-1]
    w = w.reshape(*lead, ML_HEADS, ML_DIM)
    w = jnp.pad(w, [(0, 0)] * len(lead) + [(0, 0), (0, ML_PAD - ML_DIM)])
    return w.reshape(*lead, ML_PW)


def _pack_in_cols(wb):
    o = 0
    qa = wb[..., o:o + NA_WIDTH] * (NA_DIM ** -0.5)
    ka = wb[..., o + NA_WIDTH:o + 2 * NA_WIDTH]
    va = wb[..., o + 2 * NA_WIDTH:o + 3 * NA_WIDTH]
    o += 3 * NA_WIDTH
    qb, kb, vb, ob = [_pad_heads(wb[..., o + j * ML_WIDTH:o + (j + 1) * ML_WIDTH]) for j in range(4)]
    o += 4 * ML_WIDTH
    gates = wb[..., o:o + N_GATE_COLS]
    o += N_GATE_COLS
    pool = wb[..., o:o + POOL_WIDTH]
    main = jnp.concatenate([qa, ka, va, qb, vb, ob, pool], axis=-1)
    gates_p = jnp.pad(gates, [(0, 0)] * (gates.ndim - 1) + [(0, LANE - N_GATE_COLS)])
    return main, gates_p, jnp.concatenate([kb, gates], axis=-1)


def _pack_w_in(w, b):
    w_main, w_gates, w_feat = _pack_in_cols(w)
    b_main, b_gates, b_feat = _pack_in_cols(b.astype(f32))
    return (w_main.astype(bf16), b_main[:, None], w_gates.astype(bf16), b_gates[:, None],
            jnp.swapaxes(w_feat, 1, 2).astype(bf16), b_feat[:, :, None])


def _pack_w_out(w):
    n_l = w.shape[0]
    wb = w[:, NA_WIDTH:NA_WIDTH + ML_WIDTH].reshape(n_l, ML_HEADS, ML_DIM, D_MODEL)
    wb = jnp.pad(wb, ((0, 0), (0, 0), (0, ML_PAD - ML_DIM), (0, 0))).reshape(n_l, ML_PW, D_MODEL)
    return jnp.concatenate([w[:, :NA_WIDTH], wb, w[:, NA_WIDTH + ML_WIDTH:]], axis=1).astype(bf16)


def _block_diag(w):
    n_l, g, c, _ = w.shape
    eye = jnp.eye(g, dtype=w.dtype)
    return (eye[None, :, None, :, None] * w[:, :, :, None, :]).reshape(n_l, g * c, g * c)


def kernel(x_prompt, x_sample, cache_k_attn, cache_v_attn, state_mlstm_C, state_mlstm_n, state_mlstm_m, c, c_ctx,
           w_ada, b_ada, norm1, w_in, b_in, rpb, ml_norm, w_pool, pool_scale, w_out, norm2, w_router, b_router,
           w_gate, w_up, w_down, final_norm):
    dt = x_prompt.dtype
    x_ctx = x_prompt.reshape(N_CTX, D_MODEL).astype(f32)
    x_lat = x_sample.reshape(N_LAT, D_MODEL).astype(f32)
    x_lat_block0 = 0
    cvec = jnp.concatenate([c_ctx[None], c, jnp.zeros((8 - 1 - DEC_BATCH, D_MODEL), c.dtype)], axis=0).astype(f32)
    mods_all = _ada(cvec, w_ada.astype(f32), b_ada.astype(f32))
    mods_all = mods_all[:, :1 + DEC_BATCH].reshape(DEPTH, 1 + DEC_BATCH, 6, D_MODEL)

    wr_t = w_router.astype(f32).T
    br_t = b_router.astype(f32)[:, None]
    fn = final_norm.astype(f32)[None]

    na_bias = _na_bias_tables(rpb)
    xs_buf = jnp.zeros((MOE_ROWS, D_MODEL), bf16)
    in_params = (norm1.astype(f32)[:, None],) + _pack_w_in(w_in, b_in)
    out_params = (_block_diag(w_pool.astype(f32)).astype(bf16), pool_scale.astype(f32)[:, None],
                  _pad_heads(ml_norm.astype(f32))[:, None], _pack_w_out(w_out), norm2.astype(f32)[:, None])

    new_k = jnp.zeros((BATCH, DEPTH, SEQ, NA_WIDTH), f32)
    new_v = jnp.zeros_like(new_k)
    Cs, ns, ms = [], [], []
    pending = None
    for l in range(DEPTH):
        mods = mods_all[l]
        if pending is None:
            qkva, new_k, new_v, qvo, kt, gates, gates_t, pin = _in_proj(x_ctx, x_lat, mods, l, *in_params,
                                                                        new_k, new_v)
        else:
            x, qkva, new_k, new_v, qvo, kt, gates, gates_t, pin = _moe_in_proj(
                *pending, mods_all[l - 1], mods, l, *in_params, new_k, new_v)
            x_ctx, x_lat, x_lat_block0 = x, x, N_CTX // TOK_TILE

        oa_ctx = _ctx_attention(qkva.reshape(N_TOK // SEQ, SEQ, W_A))
        ck = (cache_k_attn[:, l].reshape(DEC_BATCH, PAST_LEN, NA_WIDTH)).astype(bf16)
        cv = (cache_v_attn[:, l].reshape(DEC_BATCH, PAST_LEN, NA_WIDTH)).astype(bf16)
        oa_lat = _neighborhood_attention(qkva.reshape(N_TOK // DEC_SEQ, DEC_SEQ, W_A), ck, cv, na_bias, l)

        c_l, m_l = _pack_ml_state(state_mlstm_C[:, l], state_mlstm_n[:, l], state_mlstm_m[:, l])
        hf, hb, c_fin, m_fin = _mlstm(qvo, kt, gates, gates_t, c_l, m_l)
        C_l, n_l, m_l2 = _unpack_ml_state(c_fin[:BATCH], m_fin[:BATCH])
        Cs.append(C_l)
        ns.append(n_l)
        ms.append(m_l2)

        x1, h2, route_t, rc = _out_proj(x_ctx, x_lat, x_lat_block0, mods, l,
                                        oa_ctx.reshape(N_CTX, NA_WIDTH), oa_lat.reshape(N_LAT, NA_WIDTH),
                                        hf.reshape(N_TOK, ML_PW), hb.reshape(N_TOK, ML_PW), qvo, pin,
                                        *out_params, wr_t, br_t)
        pos0, pos1, ys, xs_buf = _moe_experts(h2, route_t, w_gate, w_up, w_down, l, xs_buf)
        pending = (pos0, pos1, ys, x1, rc)

    x = _final_combine(*pending, mods_all[DEPTH - 1], fn)
    y_prompt = x[0].reshape(BATCH, SEQ, D_MODEL).astype(dt)
    y_sample = x[1].reshape(DEC_BATCH, DEC_SEQ, D_MODEL).astype(dt)
    new_k, new_v = (a.reshape(BATCH, DEPTH, SEQ, NA_HEADS, NA_DIM).astype(dt) for a in (new_k, new_v))
    return (y_prompt, y_sample, new_k, new_v,
            jnp.stack(Cs, axis=1).astype(dt), jnp.stack(ns, axis=1).astype(dt), jnp.stack(ms, axis=1).astype(dt))
```

```python
import functools

import numpy as np
import jax
import jax.numpy as jnp
from jax import lax
from jax.experimental import pallas as pl
from jax.experimental.pallas import tpu as pltpu

D_MODEL = 1024
BATCH = 16
SEQ = 256
DEPTH = 4
DEC_BATCH = 2
DEC_SEQ = 4096
PAST_LEN = 256
GRID_W = 64
EPS = 1e-6
NEG_INF = -1e30
NA_HEADS = 6
NA_DIM = 64
NA_WIDTH = NA_HEADS * NA_DIM
NA_ROWS = 8
NA_COLS = 16
RPB_ROWS = 2 * NA_ROWS - 1
RPB_COLS = 2 * NA_COLS - 1
ML_HEADS = 4
ML_DIM = 96
ML_WIDTH = ML_HEADS * ML_DIM
POOL_WINDOWS = (2, 4, 8, 16)
POOL_GROUPS = 4
POOL_DIM = 64
POOL_WIDTH = POOL_GROUPS * POOL_DIM
N_GATE_COLS = 4 * ML_HEADS
N_EXPERTS = 16
N_EXPERT_GROUPS = 4
EXPERTS_PER_GROUP = N_EXPERTS // N_EXPERT_GROUPS
D_EXPERT = 512
ADA_DIM = 6 * D_MODEL

N_CTX = BATCH * SEQ
N_LAT = DEC_BATCH * DEC_SEQ
N_TOK = N_CTX + N_LAT
LANE = 128
ML_PAD = LANE
ML_PW = ML_HEADS * ML_PAD
CAUG = ML_PAD
NA_PAIRS = NA_HEADS // 2
TOK_TILE = 512
ML_CHUNK = 256
NA_QROWS = 4
NA_KROWS = NA_QROWS + NA_ROWS - 1
POOL_HALO = max(POOL_WINDOWS) // 2
POOL_BLOCK = 128
MOE_TILE = 512
MOE_CHUNK = 16
MOE_LOCAL_ROWS = -(-(2 * TOK_TILE + N_EXPERTS * (MOE_CHUNK - 1)) // LANE) * LANE
MOE_ROWS = -(-(2 * N_TOK + (N_TOK // TOK_TILE) * N_EXPERTS * (MOE_CHUNK - 1) + N_EXPERTS * (MOE_TILE - 1))
             // MOE_TILE) * MOE_TILE
VMEM_LIMIT = 56 * 1024 * 1024

W_A = 3 * NA_WIDTH
W_B = 3 * ML_PW
N_TCOLS = ML_PW + N_GATE_COLS
W_MAIN = W_A + W_B + POOL_WIDTH

f32 = jnp.float32
bf16 = jnp.bfloat16
HI = lax.Precision.HIGHEST


def _nt(a, b, **kw):
    return lax.dot_general(a, b, (((1,), (1,)), ((), ())), preferred_element_type=f32, **kw)


def _mod_row(i, tile):
    n_ctx_tiles = N_CTX // tile
    per_batch = DEC_SEQ // tile
    return jnp.where(i < n_ctx_tiles, 0, 1 + (i - n_ctx_tiles) // per_batch)


def _ada_kernel(c_ref, w_ref, b_ref, o_ref):
    s = c_ref[...]
    s = s * jax.nn.sigmoid(s)
    o_ref[0] = jnp.dot(s.astype(bf16), w_ref[0].astype(bf16), preferred_element_type=f32) + b_ref[0]


def _ada(cvec, w_ada, b_ada):
    nj = ADA_DIM // D_MODEL
    return pl.pallas_call(
        _ada_kernel,
        grid=(DEPTH, nj),
        in_specs=[pl.BlockSpec((8, D_MODEL), lambda l, j: (0, 0)),
                  pl.BlockSpec((1, D_MODEL, D_MODEL), lambda l, j: (l, 0, j)),
                  pl.BlockSpec((1, 1, D_MODEL), lambda l, j: (l, 0, j))],
        out_specs=pl.BlockSpec((1, 8, D_MODEL), lambda l, j: (l, 0, j)),
        out_shape=jax.ShapeDtypeStruct((DEPTH, 8, ADA_DIM), f32),
        name="ada_mod",
    )(cvec, w_ada, b_ada.reshape(DEPTH, 1, ADA_DIM))


def _in_tile(step):
    return (step + N_CTX // TOK_TILE) % (N_TOK // TOK_TILE)


def _in_kernel(xc_ref, xl_ref, *refs):
    is_ctx = _in_tile(pl.program_id(0)) < N_CTX // xc_ref.shape[0]
    _in_body(jnp.where(is_ctx, xc_ref[...], xl_ref[...]), *refs)


def _moe_in_kernel(pos0_ref, pos1_ref, ys_ref, x1_ref, rc_ref, mod_prev_ref, *refs):
    in_refs, x_out_ref, out_refs, (buf, sem) = refs[:10], refs[10], refs[11:-2], refs[-2:]
    x = _moe_residual(pos0_ref, pos1_ref, ys_ref, x1_ref, rc_ref, mod_prev_ref, buf, sem, tile_of=_in_tile)
    x_out_ref[...] = x
    _in_body(x, *in_refs, *out_refs)


def _in_body(x, mod_ref, n1_ref, w_ref, b_ref, wg_ref, bg_ref, wt_ref, bt_ref, k_in_ref, v_in_ref,
             a_ref, k_ref, v_ref, b_out_ref, kt_ref, g_ref, gt_ref, pin_ref):
    del k_in_ref, v_in_ref
    mod = mod_ref[0]
    h = x * lax.rsqrt(jnp.mean(x * x, axis=-1, keepdims=True) + EPS) * n1_ref[...]
    h = (h * (1.0 + mod[1:2]) + mod[0:1]).astype(bf16)
    pa = jnp.dot(h, w_ref[:, 0:W_A], preferred_element_type=f32) + b_ref[:, 0:W_A]
    a_ref[...] = pa.astype(bf16)
    k_ref[...] = pa[:, NA_WIDTH:2 * NA_WIDTH].reshape(k_ref.shape)
    v_ref[...] = pa[:, 2 * NA_WIDTH:W_A].reshape(v_ref.shape)
    for j in range(3):
        lo = W_A + j * ML_PW
        pb = jnp.dot(h, w_ref[:, lo:lo + ML_PW], preferred_element_type=f32) + b_ref[:, lo:lo + ML_PW]
        if j == 0:
            pb = pb * (ML_DIM ** -0.5)
        b_out_ref[:, j * ML_PW:(j + 1) * ML_PW] = pb.astype(bf16)
    lo = W_A + W_B
    pin_ref[...] = jnp.dot(h, w_ref[:, lo:lo + POOL_WIDTH], preferred_element_type=f32) + b_ref[:, lo:lo + POOL_WIDTH]
    g_ref[...] = jnp.dot(h, wg_ref[...], preferred_element_type=f32) + bg_ref[...]
    t = _nt(wt_ref[...], h) + bt_ref[...]
    kt_ref[...] = t[0:ML_PW].astype(bf16)
    gt_ref[...] = t[ML_PW:N_TCOLS]


def _in_proj_specs(layer):
    tm = TOK_TILE
    lyr = lambda shape: pl.BlockSpec((None,) + shape, lambda i, *_: (layer, 0, 0))
    rows = lambda width: pl.BlockSpec((tm, width), lambda i, *_: (_in_tile(i), 0))
    cols = lambda height: pl.BlockSpec((height, tm), lambda i, *_: (0, _in_tile(i)))
    param_specs = [pl.BlockSpec((1, 6, D_MODEL), lambda i, *_: (_mod_row(_in_tile(i), tm), 0, 0)),
                   lyr((1, D_MODEL)), lyr((D_MODEL, W_MAIN)), lyr((1, W_MAIN)), lyr((D_MODEL, LANE)), lyr((1, LANE)),
                   lyr((N_TCOLS, D_MODEL)), lyr((N_TCOLS, 1)),
                   pl.BlockSpec(memory_space=pl.ANY), pl.BlockSpec(memory_space=pl.ANY)]
    n_ctx_tiles = N_CTX // tm
    kv_spec = pl.BlockSpec((tm // SEQ, None, SEQ, NA_WIDTH),
                           lambda i, *_: (jnp.where(_in_tile(i) < n_ctx_tiles, _in_tile(i), 0), layer, 0, 0))
    kv_shape = jax.ShapeDtypeStruct((BATCH, DEPTH, SEQ, NA_WIDTH), f32)
    out_specs = [rows(W_A), kv_spec, kv_spec, rows(W_B), cols(ML_PW), rows(LANE), cols(N_GATE_COLS),
                 rows(POOL_WIDTH)]
    out_shape = [jax.ShapeDtypeStruct((N_TOK, W_A), bf16), kv_shape, kv_shape,
                 jax.ShapeDtypeStruct((N_TOK, W_B), bf16),
                 jax.ShapeDtypeStruct((ML_PW, N_TOK), bf16),
                 jax.ShapeDtypeStruct((N_TOK, LANE), f32),
                 jax.ShapeDtypeStruct((N_GATE_COLS, N_TOK), f32),
                 jax.ShapeDtypeStruct((N_TOK, POOL_WIDTH), f32)]
    return rows, param_specs, out_specs, out_shape


def _in_proj(x_ctx, x_lat, mods, layer, *params):
    tm = TOK_TILE
    n_ctx_tiles = N_CTX // tm
    rows, param_specs, out_specs, out_shape = _in_proj_specs(layer)
    return pl.pallas_call(
        _in_kernel,
        grid=(N_TOK // tm,),
        in_specs=[pl.BlockSpec((tm, D_MODEL), lambda i: (jnp.where(_in_tile(i) < n_ctx_tiles, _in_tile(i), 0), 0)),
                  pl.BlockSpec((tm, D_MODEL), lambda i: (jnp.maximum(_in_tile(i) - n_ctx_tiles, 0), 0))] + param_specs,
        out_specs=out_specs,
        out_shape=out_shape,
        input_output_aliases={2 + len(param_specs) - 2: 1, 2 + len(param_specs) - 1: 2},
        compiler_params=pltpu.CompilerParams(dimension_semantics=("arbitrary",), vmem_limit_bytes=VMEM_LIMIT),
        name="in_proj",
    )(x_ctx, x_lat, mods, *params)


def _moe_in_proj(pos0, pos1, ys, x1, rc, mods_prev, mods, layer, *params):
    tm = TOK_TILE
    rows, param_specs, out_specs, out_shape = _in_proj_specs(layer)
    return pl.pallas_call(
        _moe_in_kernel,
        grid_spec=pltpu.PrefetchScalarGridSpec(
            num_scalar_prefetch=2,
            grid=(N_TOK // tm,),
            in_specs=[pl.BlockSpec(memory_space=pl.ANY), rows(D_MODEL), rows(LANE),
                      pl.BlockSpec((1, 6, D_MODEL), lambda i, *_: (_mod_row(_in_tile(i), tm), 0, 0))] + param_specs,
            out_specs=[rows(D_MODEL)] + out_specs,
            scratch_shapes=[pltpu.VMEM((2, 2, tm, D_MODEL), f32), pltpu.SemaphoreType.DMA((2,))]),
        out_shape=[jax.ShapeDtypeStruct((N_TOK, D_MODEL), f32)] + out_shape,
        input_output_aliases={6 + len(param_specs) - 2: 2, 6 + len(param_specs) - 1: 3},
        compiler_params=pltpu.CompilerParams(dimension_semantics=("arbitrary",), vmem_limit_bytes=VMEM_LIMIT),
        name="moe_combine_in_proj",
    )(pos0, pos1, ys, x1, rc, mods_prev, mods, *params)


def _pair_attention(qp, parts):
    lane = lax.broadcasted_iota(jnp.int32, (1, LANE), 1)
    outs = []
    for j in range(2):
        in_half = (lane >= j * NA_DIM) & (lane < (j + 1) * NA_DIM)
        qm = jnp.where(in_half, qp, jnp.zeros_like(qp))
        scores = []
        for k, _, bias in parts:
            s = _nt(qm, k)
            if bias is not None:
                s = s + bias[j]
            scores.append(s)
        m = scores[0].max(axis=-1, keepdims=True)
        for s in scores[1:]:
            m = jnp.maximum(m, s.max(axis=-1, keepdims=True))
        den = None
        acc = None
        for s, (_, v, _) in zip(scores, parts):
            p = jnp.exp(s - m)
            ps = p.sum(axis=-1, keepdims=True)
            den = ps if den is None else den + ps
            o = jnp.dot(p.astype(bf16), v, preferred_element_type=f32)
            acc = o if acc is None else acc + o
        outs.append(acc / den)
    return jnp.where(lane < NA_DIM, outs[0], outs[1])


def _ctx_attn_kernel(q_ref, k_ref, v_ref, o_ref):
    for p in range(NA_PAIRS):
        sl = slice(p * LANE, (p + 1) * LANE)
        o = _pair_attention(q_ref[0, :, sl], [(k_ref[0, :, sl], v_ref[0, :, sl], None)])
        o_ref[0, :, sl] = o.astype(bf16)


def _ctx_attention(qkv):
    blk = lambda c: pl.BlockSpec((1, SEQ, NA_WIDTH), lambda b, c=c: (b, 0, c))
    return pl.pallas_call(
        _ctx_attn_kernel,
        grid=(BATCH,),
        in_specs=[blk(0), blk(1), blk(2)],
        out_specs=pl.BlockSpec((1, SEQ, NA_WIDTH), lambda b: (b, 0, 0)),
        out_shape=jax.ShapeDtypeStruct((BATCH, SEQ, NA_WIDTH), bf16),
        name="ctx_attention",
    )(qkv, qkv, qkv)


def _na_window_start(rb):
    return jnp.clip(rb * NA_QROWS - NA_ROWS // 2, 0, DEC_SEQ // GRID_W - NA_KROWS)


def _na_bias(tab_ref, head, rb):
    rows = DEC_SEQ // GRID_W
    ws = _na_window_start(rb)
    lane = lax.broadcasted_iota(jnp.int32, (1, NA_KROWS * GRID_W), 1)
    per_qrow = []
    for dq in range(NA_QROWS):
        qr = rb * NA_QROWS + dq
        a0 = ws - qr + (NA_ROWS - 1) + NA_KROWS
        tiles = [tab_ref[head, a0 + 2 * j] for j in range((NA_KROWS + 1) // 2)]
        t = jnp.concatenate(tiles, axis=1)[:, :NA_KROWS * GRID_W]
        lo = (jnp.clip(qr - NA_ROWS // 2, 0, rows - NA_ROWS) - ws) * GRID_W
        ok = (lane >= lo) & (lane < lo + NA_ROWS * GRID_W)
        per_qrow.append(jnp.where(ok, t, NEG_INF))
    return jnp.concatenate(per_qrow, axis=0)


def _na_kernel(q_ref, k_ref, v_ref, ck_ref, cv_ref, tab_ref, o_ref):
    rb = pl.program_id(1)
    start = pl.multiple_of(_na_window_start(rb) * GRID_W, GRID_W)
    nk = NA_KROWS * GRID_W
    for p in range(NA_PAIRS):
        sl = slice(p * LANE, (p + 1) * LANE)
        bias = [_na_bias(tab_ref.at[0], 2 * p + j, rb) for j in range(2)]
        parts = [(k_ref[0, pl.ds(start, nk), sl], v_ref[0, pl.ds(start, nk), sl], bias),
                 (ck_ref[0, :, sl], cv_ref[0, :, sl], None)]
        o = _pair_attention(q_ref[0, :, sl], parts)
        o_ref[0, :, sl] = o.astype(bf16)


def _neighborhood_attention(qkv, ck, cv, tables, layer):
    nq = NA_QROWS * GRID_W
    n_rb = DEC_SEQ // nq
    return pl.pallas_call(
        _na_kernel,
        grid=(DEC_BATCH, n_rb),
        in_specs=[pl.BlockSpec((1, nq, NA_WIDTH), lambda b, r: (1 + b, r, 0)),
                  pl.BlockSpec((1, DEC_SEQ, NA_WIDTH), lambda b, r: (1 + b, 0, 1)),
                  pl.BlockSpec((1, DEC_SEQ, NA_WIDTH), lambda b, r: (1 + b, 0, 2)),
                  pl.BlockSpec((1, PAST_LEN, NA_WIDTH), lambda b, r: (b, 0, 0)),
                  pl.BlockSpec((1, PAST_LEN, NA_WIDTH), lambda b, r: (b, 0, 0)),
                  pl.BlockSpec((1,) + tables.shape[1:], lambda b, r: (layer, 0, 0, 0, 0))],
        out_specs=pl.BlockSpec((1, nq, NA_WIDTH), lambda b, r: (b, r, 0)),
        out_shape=jax.ShapeDtypeStruct((DEC_BATCH, DEC_SEQ, NA_WIDTH), bf16),
        compiler_params=pltpu.CompilerParams(dimension_semantics=("arbitrary", "arbitrary"),
                                             vmem_limit_bytes=VMEM_LIMIT),
        name="neighborhood_attention",
    )(qkv, qkv, qkv, ck, cv, tables)


def _na_bias_tables(rpb):
    qc = np.arange(GRID_W)[:, None]
    kc = np.arange(GRID_W)[None, :]
    dc = np.clip(kc - qc + NA_COLS - 1, 0, RPB_COLS - 1)
    col_start = np.clip(qc - NA_COLS // 2, 0, GRID_W - NA_COLS)
    col_ok = (kc >= col_start) & (kc < col_start + NA_COLS)
    pick_col = (dc[None] == np.arange(RPB_COLS)[:, None, None]).astype(np.float32)
    rpb_pad = jnp.pad(rpb.astype(f32), ((0, 0), (0, 0), (NA_KROWS, NA_KROWS + 1), (0, 0)))
    n_a = rpb_pad.shape[2] - 1
    rows2 = jnp.stack([rpb_pad[:, :, :-1], rpb_pad[:, :, 1:]], axis=3)
    pick2 = np.zeros((2, RPB_COLS, GRID_W, 2 * GRID_W), np.float32)
    for j in range(2):
        pick2[j, :, :, j * GRID_W:(j + 1) * GRID_W] = pick_col
    tiles = jnp.einsum('lhajb,jbqc->lhaqc', rows2, pick2, precision=HI)
    a_pad = np.arange(n_a)[:, None] + np.arange(2)[None, :]
    row_ok = (a_pad >= NA_KROWS) & (a_pad < NA_KROWS + RPB_ROWS)
    ok = (row_ok[:, None, :, None] & col_ok[None, :, None, :]).reshape(n_a, GRID_W, 2 * GRID_W)
    return jnp.where(ok[None, None], tiles, NEG_INF)


def _log_sigmoid(x):
    return -(jnp.maximum(-x, 0.0) + jnp.log(1.0 + jnp.exp(-jnp.abs(x))))


def _split3(x):
    hi = x.astype(bf16)
    r1 = x - hi.astype(f32)
    mid = r1.astype(bf16)
    lo = (r1 - mid.astype(f32)).astype(bf16)
    return hi, mid, lo


def _mlstm_kernel(qf_ref, vf_ref, ktf_ref, gf_ref, gtf_ref, qb_ref, vb_ref, ktb_ref, gb_ref, gtb_ref,
                  c0_ref, m0_ref, hf_ref, hb_ref, c_out_ref, m_out_ref, c_scr, m_scr):
    L = ML_CHUNK
    seq, c, n_chunks, _ = _ml_schedule(pl.program_id(0))

    @pl.when(c == 0)
    def _():
        is_ctx = seq < BATCH
        c_scr[...] = jnp.where(is_ctx, 0.0, c0_ref[0])
        m_scr[...] = jnp.where(is_ctx, 0.0, m0_ref[0])

    ri = lax.broadcasted_iota(jnp.int32, (L, L), 0)
    ci = lax.broadcasted_iota(jnp.int32, (L, L), 1)
    lane = lax.broadcasted_iota(jnp.int32, (L, ML_PAD), 1)
    is_ncol = lane == ML_DIM
    lower = ri >= ci
    upper = ri <= ci
    lower_b = jnp.where(lower, 1.0, 0.0).astype(bf16)
    upper_b = jnp.where(upper, 1.0, 0.0).astype(bf16)
    dirs = ((qf_ref, ktf_ref, vf_ref, gf_ref, gtf_ref, hf_ref), (qb_ref, ktb_ref, vb_ref, gb_ref, gtb_ref, hb_ref))
    for d, (q_ref, kt_ref, v_ref, g_ref, gt_ref, h_ref) in enumerate(dirs):
        g = g_ref[...][:, 0:N_GATE_COLS]
        gt = gt_ref[...]
        lf_c = _log_sigmoid(g)
        lf_r = _log_sigmoid(gt)
        b_cols = sum(jnp.dot(lower_b, part, preferred_element_type=f32) for part in _split3(lf_c))
        b_rows = sum(jnp.dot(part, upper_b, preferred_element_type=f32) for part in _split3(lf_r))
        tot_c = jnp.sum(lf_c, axis=0, keepdims=True)
        tot_r = jnp.sum(lf_r, axis=1, keepdims=True)
        visible = lower
        if d == 1:
            b_cols = tot_c - b_cols + lf_c
            b_rows = tot_r - b_rows + lf_r
            visible = upper
        for hd in range(ML_HEADS):
            st = d * ML_HEADS + hd
            ci_ = 2 * ML_HEADS * d + hd
            cf_ = ci_ + ML_HEADS
            sl = slice(hd * ML_PAD, (hd + 1) * ML_PAD)
            bc = b_cols[:, cf_:cf_ + 1]
            br = b_rows[cf_:cf_ + 1, :]
            li_r = gt[ci_:ci_ + 1, :]
            m_prev = m_scr[st:st + 1, 0:1]
            dmat = jnp.where(visible, bc - br + li_r, NEG_INF)
            inter = bc + m_prev
            m_t = jnp.maximum(inter, dmat.max(axis=-1, keepdims=True))
            w_intra = jnp.exp(dmat - m_t)
            w_inter = jnp.exp(inter - m_t)
            qh = q_ref[0, :, sl]
            kht = kt_ref[sl, :]
            v_aug = jnp.where(is_ncol, jnp.ones((), bf16), v_ref[0, :, sl])
            s = (jnp.dot(qh, kht, preferred_element_type=f32) * w_intra).astype(bf16)
            c_aug = c_scr[st]
            na = (w_inter * jnp.dot(qh, c_aug.astype(bf16), preferred_element_type=f32)
                  + jnp.dot(s, v_aug, preferred_element_type=f32))
            den = na[:, ML_DIM:ML_DIM + 1]
            h_ref[0, :, sl] = jnp.where(lane < ML_DIM, na / jnp.maximum(jnp.abs(den), jnp.exp(-m_t)), 0.0)
            b_end = tot_r[cf_:cf_ + 1, :]
            g_row = b_end - br + li_r
            m_new = jnp.maximum(b_end + m_prev, g_row.max(axis=1, keepdims=True))
            decay = jnp.exp(b_end + m_prev - m_new)
            kwt = (kht.astype(f32) * jnp.exp(g_row - m_new)).astype(bf16)
            c_scr[st] = decay * c_aug + jnp.dot(kwt, v_aug, preferred_element_type=f32)
            m_scr[st:st + 1, :] = jnp.broadcast_to(m_new, (1, LANE))

    @pl.when(c == n_chunks - 1)
    def _():
        c_out_ref[0] = c_scr[...]
        m_out_ref[0] = m_scr[...]


def _ml_schedule(s):
    nc_ctx, nc_lat = SEQ // ML_CHUNK, DEC_SEQ // ML_CHUNK
    n_ctx_steps = BATCH * nc_ctx
    is_ctx = s < n_ctx_steps
    t = s - n_ctx_steps
    seq = jnp.where(is_ctx, s // nc_ctx, BATCH + t // nc_lat)
    c = jnp.where(is_ctx, s % nc_ctx, t % nc_lat)
    nc = jnp.where(is_ctx, nc_ctx, nc_lat)
    base = jnp.where(is_ctx, (s // nc_ctx) * nc_ctx, n_ctx_steps + (t // nc_lat) * nc_lat)
    return seq, c, nc, base


def _mlstm(qvo, kt, gates, gates_t, c0, m0):
    L = ML_CHUNK
    n_seq = BATCH + DEC_BATCH

    def fwd(s):
        _, c, _, base = _ml_schedule(s)
        return base + c

    def bwd(s):
        _, c, nc, base = _ml_schedule(s)
        return base + nc - 1 - c

    seq_of = lambda s: _ml_schedule(s)[0]
    lat_of = lambda s: jnp.maximum(seq_of(s) - BATCH, 0)

    def specs(pos):
        return [pl.BlockSpec((1, L, ML_PW), lambda s, j=j: (pos(s), 0, j)) for j in range(2)] + [
            pl.BlockSpec((ML_PW, L), lambda s: (0, pos(s))),
            pl.BlockSpec((L, LANE), lambda s: (pos(s), 0)),
            pl.BlockSpec((N_GATE_COLS, L), lambda s: (0, pos(s)))]

    q3 = qvo.reshape(N_TOK // L, L, W_B)
    n_str = 2 * ML_HEADS
    return pl.pallas_call(
        _mlstm_kernel,
        grid=(N_TOK // L,),
        in_specs=specs(fwd) + specs(bwd) + [
            pl.BlockSpec((1, n_str, ML_PAD, CAUG), lambda s: (lat_of(s), 0, 0, 0)),
            pl.BlockSpec((1, n_str, LANE), lambda s: (lat_of(s), 0, 0))],
        out_specs=[pl.BlockSpec((1, L, ML_PW), lambda s: (fwd(s), 0, 0)),
                   pl.BlockSpec((1, L, ML_PW), lambda s: (bwd(s), 0, 0)),
                   pl.BlockSpec((1, n_str, ML_PAD, CAUG), lambda s: (seq_of(s), 0, 0, 0)),
                   pl.BlockSpec((1, n_str, LANE), lambda s: (seq_of(s), 0, 0))],
        out_shape=[jax.ShapeDtypeStruct((N_TOK // L, L, ML_PW), f32),
                   jax.ShapeDtypeStruct((N_TOK // L, L, ML_PW), f32),
                   jax.ShapeDtypeStruct((n_seq, n_str, ML_PAD, CAUG), f32),
                   jax.ShapeDtypeStruct((n_seq, n_str, LANE), f32)],
        scratch_shapes=[pltpu.VMEM((n_str, ML_PAD, CAUG), f32), pltpu.VMEM((n_str, LANE), f32)],
        compiler_params=pltpu.CompilerParams(dimension_semantics=("arbitrary",), vmem_limit_bytes=VMEM_LIMIT),
        name="mlstm",
    )(q3, q3, kt, gates, gates_t, q3, q3, kt, gates, gates_t, c0, m0)


def _pack_ml_state(C, n, m):
    B = C.shape[0]
    c_aug = jnp.zeros((B, 2, ML_HEADS, ML_PAD, CAUG), f32)
    c_aug = c_aug.at[:, :, :, :ML_DIM, :ML_DIM].set(C.astype(f32))
    c_aug = c_aug.at[:, :, :, :ML_DIM, ML_DIM].set(n.astype(f32))
    m_b = jnp.broadcast_to(m.astype(f32)[..., None], (B, 2, ML_HEADS, LANE))
    return c_aug.reshape(B, 2 * ML_HEADS, ML_PAD, CAUG), m_b.reshape(B, 2 * ML_HEADS, LANE)


def _unpack_ml_state(c_aug, m_b):
    B = c_aug.shape[0]
    c_aug = c_aug.reshape(B, 2, ML_HEADS, ML_PAD, CAUG)
    return (c_aug[:, :, :, :ML_DIM, :ML_DIM], c_aug[:, :, :, :ML_DIM, ML_DIM],
            m_b.reshape(B, 2, ML_HEADS, LANE)[..., 0])


def _pool_rows(u_prev, u_cur, u_next, w_bd, scale, t0, seq_len):
    tm = u_cur.shape[0]
    u_win = jnp.concatenate([u_prev, u_cur, u_next], axis=0)
    u_hi = u_win.astype(bf16)
    u_lo = (u_win - u_hi.astype(f32)).astype(bf16)
    lane = lax.broadcasted_iota(jnp.int32, (1, LANE), 1)
    blocks = []
    for r0 in range(0, tm, POOL_BLOCK):
        win = slice(r0, r0 + POOL_BLOCK + 2 * POOL_HALO)
        t_abs = t0 + r0 + lax.broadcasted_iota(jnp.int32, (POOL_BLOCK, 1), 0)
        s_abs = t0 + r0 - POOL_HALO + lax.broadcasted_iota(jnp.int32, (1, POOL_BLOCK + 2 * POOL_HALO), 1)
        t_loc = t_abs & (seq_len - 1)
        seq_start = t_abs - t_loc
        means = []
        for w in POOL_WINDOWS:
            lo = jnp.maximum(t_loc - w // 2, 0)
            hi = jnp.minimum(t_loc - w // 2 + w, seq_len)
            in_win = (s_abs >= seq_start + lo) & (s_abs < seq_start + hi)
            means.append((jnp.where(in_win, 1.0, 0.0).astype(bf16), 1.0 / (hi - lo).astype(f32)))
        pooled = []
        for p in range(POOL_GROUPS // 2):
            sl = slice(p * LANE, (p + 1) * LANE)
            halves = []
            for a, inv_cnt in means[2 * p:2 * p + 2]:
                tot = (jnp.dot(a, u_hi[win, sl], preferred_element_type=f32)
                       + jnp.dot(a, u_lo[win, sl], preferred_element_type=f32))
                halves.append(tot * inv_cnt)
            pooled.append(jnp.where(lane < POOL_DIM, halves[0], halves[1]) - u_cur[r0:r0 + POOL_BLOCK, sl])
        blocks.append(jnp.concatenate(pooled, axis=1))
    pooled = jnp.concatenate(blocks, axis=0).astype(bf16)
    return jnp.dot(pooled, w_bd, preferred_element_type=f32) * scale


def _top2_sum(a, b, c, d):
    hi1, lo1 = jnp.maximum(a, b), jnp.minimum(a, b)
    hi2, lo2 = jnp.maximum(c, d), jnp.minimum(c, d)
    return jnp.maximum(hi1, hi2) + jnp.maximum(jnp.minimum(hi1, hi2), jnp.maximum(lo1, lo2))


def _first_match(vals, target):
    idx = jnp.full_like(target, float(len(vals) - 1))
    for i in range(len(vals) - 2, -1, -1):
        idx = jnp.where(vals[i] == target, float(i), idx)
    return idx


def _pick(vals, idx):
    out = vals[-1]
    for i in range(len(vals) - 2, -1, -1):
        out = jnp.where(idx == float(i), vals[i], out)
    return out


def _route(logits_t, bias_t):
    scores = jax.nn.sigmoid(logits_t)
    sel = scores + bias_t
    row = lambda a, i: a[i:i + 1, :]
    grp = [_top2_sum(*[row(sel, EXPERTS_PER_GROUP * g + i) for i in range(EXPERTS_PER_GROUP)])
           for g in range(N_EXPERT_GROUPS)]
    best = functools.reduce(jnp.maximum, grp)
    gidx = _first_match(grp, best)
    sel_g = [_pick([row(sel, EXPERTS_PER_GROUP * g + i) for g in range(N_EXPERT_GROUPS)], gidx)
             for i in range(EXPERTS_PER_GROUP)]
    sco_g = [_pick([row(scores, EXPERTS_PER_GROUP * g + i) for g in range(N_EXPERT_GROUPS)], gidx)
             for i in range(EXPERTS_PER_GROUP)]
    i0 = _first_match(sel_g, functools.reduce(jnp.maximum, sel_g))
    rest = [jnp.where(i0 == float(i), -jnp.inf, sel_g[i]) for i in range(EXPERTS_PER_GROUP)]
    i1 = _first_match(rest, functools.reduce(jnp.maximum, rest))
    s0, s1 = _pick(sco_g, i0), _pick(sco_g, i1)
    tot = s0 + s1
    rid = lax.broadcasted_iota(jnp.int32, (LANE, logits_t.shape[1]), 0)
    rows = (EXPERTS_PER_GROUP * gidx + i0, EXPERTS_PER_GROUP * gidx + i1, s0 / tot, s1 / tot)
    out = jnp.zeros(rid.shape, f32)
    for i, r in enumerate(rows):
        out = jnp.where(rid == i, r, out)
    return out


def _out_kernel(xc_ref, xl_ref, mod_ref, oac_ref, oal_ref, hf_ref, hb_ref, ob_ref, up_ref, uc_ref, un_ref, wp_ref,
                psc_ref, mln_ref, wo_ref, n2_ref, wr_ref, br_ref, x1_ref, h2_ref, rt_ref, rc_ref):
    tm = xc_ref.shape[0]
    i = pl.program_id(0)
    is_ctx = i < N_CTX // tm
    mod = mod_ref[0]
    out_a = jnp.where(is_ctx, oac_ref[...], oal_ref[...])
    out_c = _pool_rows(up_ref[...], uc_ref[...], un_ref[...], wp_ref[...], psc_ref[...], i * tm,
                       jnp.where(is_ctx, SEQ, DEC_SEQ)).astype(bf16)
    hsum = hf_ref[...] + hb_ref[...]
    outs_b = []
    for hd in range(ML_HEADS):
        sl = slice(hd * ML_PAD, (hd + 1) * ML_PAD)
        hh = hsum[:, sl]
        ms = jnp.sum(hh * hh, axis=-1, keepdims=True) * (1.0 / ML_DIM)
        hn = hh * lax.rsqrt(ms + EPS) * mln_ref[:, sl]
        outs_b.append((jax.nn.sigmoid(ob_ref[:, sl].astype(f32)) * hn).astype(bf16))
    out_b = jnp.concatenate(outs_b, axis=1)
    mixed = (jnp.dot(out_a, wo_ref[0:NA_WIDTH, :], preferred_element_type=f32)
             + jnp.dot(out_b, wo_ref[NA_WIDTH:NA_WIDTH + ML_PW, :], preferred_element_type=f32)
             + jnp.dot(out_c, wo_ref[NA_WIDTH + ML_PW:, :], preferred_element_type=f32))
    x1 = jnp.where(is_ctx, xc_ref[...], xl_ref[...]) + mod[2:3] * mixed
    x1_ref[...] = x1
    h2 = x1 * lax.rsqrt(jnp.mean(x1 * x1, axis=-1, keepdims=True) + EPS) * n2_ref[...]
    h2 = h2 * (1.0 + mod[4:5]) + mod[3:4]
    h2_ref[...] = h2.astype(bf16)
    h_hi = h2.astype(bf16)
    h_lo = (h2 - h_hi.astype(f32)).astype(bf16)
    w_hi = wr_ref[...].astype(bf16)
    w_lo = (wr_ref[...] - w_hi.astype(f32)).astype(bf16)
    route_t = _route(_nt(w_hi, h_hi) + (_nt(w_hi, h_lo) + _nt(w_lo, h_hi)), br_ref[...])
    rt_ref[...] = route_t[0:8]
    rc_ref[...] = route_t.T


def _out_proj(x_ctx, x_lat, x_lat_block0, mods, layer, oa_ctx, oa_lat, hf, hb, qvo, pin, w_bd, psc, mln, wo, n2,
              wr_t, br_t):
    tm = TOK_TILE
    const = lambda i: (0, 0)
    lyr = lambda shape: pl.BlockSpec((None,) + shape, lambda i: (layer, 0, 0))
    row = lambda i: (i, 0)
    n_ctx_tiles = N_CTX // tm
    halo_blocks = tm // POOL_HALO
    return pl.pallas_call(
        _out_kernel,
        grid=(N_TOK // tm,),
        in_specs=[pl.BlockSpec((tm, D_MODEL), lambda i: (jnp.minimum(i, n_ctx_tiles - 1), 0)),
                  pl.BlockSpec((tm, D_MODEL), lambda i: (jnp.maximum(i - n_ctx_tiles, 0) + x_lat_block0, 0)),
                  pl.BlockSpec((1, 6, D_MODEL), lambda i: (_mod_row(i, tm), 0, 0)),
                  pl.BlockSpec((tm, NA_WIDTH), lambda i: (jnp.minimum(i, n_ctx_tiles - 1), 0)),
                  pl.BlockSpec((tm, NA_WIDTH), lambda i: (jnp.maximum(i - n_ctx_tiles, 0), 0)),
                  pl.BlockSpec((tm, ML_PW), row),
                  pl.BlockSpec((tm, ML_PW), row),
                  pl.BlockSpec((tm, ML_PW), lambda i: (i, 2)),
                  pl.BlockSpec((POOL_HALO, POOL_WIDTH), lambda i: (jnp.maximum(i * halo_blocks - 1, 0), 0)),
                  pl.BlockSpec((tm, POOL_WIDTH), row),
                  pl.BlockSpec((POOL_HALO, POOL_WIDTH),
                               lambda i: (jnp.minimum((i + 1) * halo_blocks, N_TOK // POOL_HALO - 1), 0)),
                  lyr((POOL_WIDTH, POOL_WIDTH)), lyr((1, POOL_WIDTH)), lyr((1, ML_PW)),
                  lyr((NA_WIDTH + ML_PW + POOL_WIDTH, D_MODEL)), lyr((1, D_MODEL)),
                  pl.BlockSpec((N_EXPERTS, D_MODEL), const),
                  pl.BlockSpec((N_EXPERTS, 1), const)],
        out_specs=[pl.BlockSpec((tm, D_MODEL), row),
                   pl.BlockSpec((tm, D_MODEL), row),
                   pl.BlockSpec((8, tm), lambda i: (0, i)),
                   pl.BlockSpec((tm, LANE), row)],
        out_shape=[jax.ShapeDtypeStruct((N_TOK, D_MODEL), f32),
                   jax.ShapeDtypeStruct((N_TOK, D_MODEL), bf16),
                   jax.ShapeDtypeStruct((8, N_TOK), f32),
                   jax.ShapeDtypeStruct((N_TOK, LANE), f32)],
        compiler_params=pltpu.CompilerParams(dimension_semantics=("arbitrary",), vmem_limit_bytes=VMEM_LIMIT),
        name="out_proj_router",
    )(x_ctx, x_lat, mods, oa_ctx, oa_lat, hf, hb, qvo, pin, pin, pin, w_bd, psc, mln, wo, n2, wr_t, br_t)


def _ceil_to(x, m):
    return jnp.floor((x + (m - 1)) * (1.0 / m)) * m


def _prefix_over_experts(v):
    er = lax.broadcasted_iota(jnp.int32, (N_EXPERTS, N_EXPERTS), 0)
    ec = lax.broadcasted_iota(jnp.int32, (N_EXPERTS, N_EXPERTS), 1)
    return jnp.dot(jnp.where(ec < er, 1.0, 0.0), v, preferred_element_type=f32, precision=HI)


def _experts_to_lanes(v):
    sub = lax.broadcasted_iota(jnp.int32, (N_EXPERTS, LANE), 0)
    lane = lax.broadcasted_iota(jnp.int32, (N_EXPERTS, LANE), 1)
    return jnp.sum(jnp.where(sub == lane, v, 0.0), axis=0, keepdims=True)


def _expert_hits(rt):
    rid = lax.broadcasted_iota(jnp.int32, (N_EXPERTS, rt.shape[1]), 0).astype(f32)
    oh0 = rid == rt[0:1, :]
    oh1 = rid == rt[1:2, :]
    both = jnp.where(oh0 | oh1, 1.0, 0.0)
    runs = jnp.broadcast_to(_ceil_to(jnp.sum(both, axis=1, keepdims=True), MOE_CHUNK), (N_EXPERTS, LANE))
    return oh0, oh1, both, runs


def _rank_kernel(rt_all_ref, rt_ref, pos_ref, te_ref, tab_ref, carry_ref):
    tm = rt_ref.shape[1]
    step = pl.program_id(0)

    @pl.when(step == 0)
    def _():
        totals = jnp.zeros((N_EXPERTS, LANE), f32)
        for i in range(rt_all_ref.shape[1] // tm):
            totals = totals + _expert_hits(rt_all_ref[:, i * tm:(i + 1) * tm])[3]
        padded = _ceil_to(totals, MOE_TILE)
        off = _prefix_over_experts(padded)
        carry_ref[...] = off
        total = jnp.sum(padded, axis=0, keepdims=True)
        n_used = total * (1.0 / MOE_TILE)
        tile = lax.broadcasted_iota(jnp.int32, (1, LANE), 1).astype(f32)
        row0 = jnp.minimum(tile, n_used - 1.0) * MOE_TILE
        expert = jnp.sum(jnp.where(off <= row0, 1.0, 0.0), axis=0, keepdims=True) - 1.0
        sub = lax.broadcasted_iota(jnp.int32, (8, LANE), 0)
        te_ref[...] = jnp.where(sub == 0, expert, jnp.where(sub == 1, n_used, 0.0)).astype(jnp.int32)

    @pl.when(step > 0)
    def _():
        oh0, oh1, both, runs = _expert_hits(rt_ref[...])
        sr = lax.broadcasted_iota(jnp.int32, (tm, tm), 0)
        sc = lax.broadcasted_iota(jnp.int32, (tm, tm), 1)
        earlier = jnp.dot(both.astype(bf16), jnp.where(sr < sc, 1.0, 0.0).astype(bf16),
                          preferred_element_type=f32)
        g_off = carry_ref[...]
        l_off = _prefix_over_experts(runs)
        g_row = g_off[:, 0:1] + earlier
        l_row = l_off[:, 0:1] + earlier
        pick = lambda oh, v: jnp.sum(jnp.where(oh, v, 0.0), axis=0, keepdims=True)
        rows = (pick(oh0, g_row), pick(oh1, g_row), pick(oh0, l_row), pick(oh1, l_row))
        sub = lax.broadcasted_iota(jnp.int32, (8, tm), 0)
        out = jnp.zeros((8, tm), f32)
        for k, r in enumerate(rows):
            out = jnp.where(sub == k, r, out)
        pos_ref[...] = out.astype(jnp.int32)
        sub = lax.broadcasted_iota(jnp.int32, (8, LANE), 0)
        tab = jnp.zeros((8, LANE), f32)
        for k, v in enumerate((runs * (1.0 / MOE_CHUNK), l_off, g_off)):
            tab = jnp.where(sub == k, _experts_to_lanes(v), tab)
        tab_ref[0] = tab.astype(jnp.int32)
        carry_ref[...] = g_off + runs


def _rank(route_t):
    tm = TOK_TILE
    n_tiles = N_TOK // tm
    tile_of = lambda s: jnp.maximum(s - 1, 0)
    return pl.pallas_call(
        _rank_kernel,
        grid=(1 + n_tiles,),
        in_specs=[pl.BlockSpec((8, N_TOK), lambda s: (0, 0)),
                  pl.BlockSpec((8, tm), lambda s: (0, tile_of(s)))],
        out_specs=[pl.BlockSpec((8, tm), lambda s: (0, tile_of(s))),
                   pl.BlockSpec((8, LANE), lambda s: (0, 0)),
                   pl.BlockSpec((1, 8, LANE), lambda s: (tile_of(s), 0, 0))],
        out_shape=[jax.ShapeDtypeStruct((8, N_TOK), jnp.int32),
                   jax.ShapeDtypeStruct((8, LANE), jnp.int32),
                   jax.ShapeDtypeStruct((n_tiles, 8, LANE), jnp.int32)],
        scratch_shapes=[pltpu.VMEM((N_EXPERTS, LANE), f32)],
        compiler_params=pltpu.CompilerParams(dimension_semantics=("arbitrary",)),
        name="moe_rank",
    )(route_t, route_t)


def _dispatch_kernel(tab_ref, h_ref, rows_ref, xs_in_ref, xs_ref, loc, sem):
    del xs_in_ref
    tm = h_ref.shape[0]
    i = pl.program_id(0)
    slot = i % 2
    rid = lax.broadcasted_iota(jnp.int32, (MOE_LOCAL_ROWS, tm), 0)
    sel = (rid == rows_ref[2:3, :]) | (rid == rows_ref[3:4, :])
    loc[slot] = jnp.dot(jnp.where(sel, 1.0, 0.0).astype(bf16), h_ref[...], preferred_element_type=f32).astype(bf16)

    def chunk_copy(sl, src_row, dst_row):
        return pltpu.make_async_copy(loc.at[sl, pl.ds(pl.multiple_of(src_row, MOE_CHUNK), MOE_CHUNK)],
                                     xs_ref.at[pl.ds(pl.multiple_of(dst_row, MOE_CHUNK), MOE_CHUNK)], sem.at[sl])

    def chunks_of(tile):
        return sum(tab_ref[3 * tile * N_EXPERTS + e] for e in range(N_EXPERTS))

    def wait_chunks(sl, n):
        def wait_one(c, carry):
            chunk_copy(sl, 0, 0).wait()
            return carry

        lax.fori_loop(0, n, wait_one, 0)

    for e in range(N_EXPERTS):
        n_chunks, l_off, g_off = (tab_ref[(3 * i + k) * N_EXPERTS + e] for k in range(3))

        def issue(c, carry, l_off=l_off, g_off=g_off):
            chunk_copy(slot, l_off + c * MOE_CHUNK, g_off + c * MOE_CHUNK).start()
            return carry

        lax.fori_loop(0, n_chunks, issue, 0)

    @pl.when(i > 0)
    def _():
        wait_chunks(1 - slot, chunks_of(i - 1))

    @pl.when(i == pl.num_programs(0) - 1)
    def _():
        wait_chunks(slot, chunks_of(i))


def _dispatch(run_table, h2, rows, xs_init):
    tm = TOK_TILE
    return pl.pallas_call(
        _dispatch_kernel,
        grid_spec=pltpu.PrefetchScalarGridSpec(
            num_scalar_prefetch=1,
            grid=(N_TOK // tm,),
            in_specs=[pl.BlockSpec((tm, D_MODEL), lambda i, tab: (i, 0)),
                      pl.BlockSpec((8, tm), lambda i, tab: (0, i)),
                      pl.BlockSpec(memory_space=pl.ANY)],
            out_specs=pl.BlockSpec(memory_space=pl.ANY),
            scratch_shapes=[pltpu.VMEM((2, MOE_LOCAL_ROWS, D_MODEL), bf16), pltpu.SemaphoreType.DMA((2,))]),
        out_shape=jax.ShapeDtypeStruct(xs_init.shape, xs_init.dtype),
        input_output_aliases={3: 0},
        compiler_params=pltpu.CompilerParams(dimension_semantics=("arbitrary",), vmem_limit_bytes=VMEM_LIMIT),
        name="moe_dispatch",
    )(run_table, h2, rows, xs_init)


def _expert_kernel(te_ref, xs_ref, wg_hbm, wu_hbm, wd_hbm, *rest, layer, reuse):
    ys_ref, wg_f32, wu_f32, wd_f32, wg_bf, wu_bf, wd_bf, slot_ref, sem = rest[-9:]
    j = pl.program_id(0)
    n_used = te_ref[1, 0]
    used = j < n_used
    expert = te_ref[0, j]
    new_expert = jnp.logical_or(j == 0, expert != te_ref[0, jnp.maximum(j - 1, 0)])

    def weight_copies(e, slot):
        return [pltpu.make_async_copy(hbm.at[layer, e], buf.at[slot], sem.at[slot])
                for hbm, buf in ((wg_hbm, wg_f32), (wu_hbm, wu_f32), (wd_hbm, wd_f32))]

    @pl.when(j == 0)
    def _():
        slot_ref[0] = 1
        for cp in weight_copies(expert, 0):
            cp.start()

    if not reuse:
        @pl.when(jnp.logical_not(used))
        def _():
            ys_ref[...] = jnp.zeros_like(ys_ref)

    @pl.when(used & new_expert)
    def _():
        slot = 1 - slot_ref[0]
        slot_ref[0] = slot
        for cp in weight_copies(expert, slot):
            cp.wait()
        wg_bf[...] = wg_f32[slot].astype(bf16)
        wu_bf[...] = wu_f32[slot].astype(bf16)
        wd_bf[...] = wd_f32[slot].astype(bf16)
        nxt = lax.while_loop(lambda t: (t < n_used) & (te_ref[0, jnp.minimum(t, LANE - 1)] == expert),
                             lambda t: t + 1, j + 1)

        @pl.when(nxt < n_used)
        def _():
            for cp in weight_copies(te_ref[0, nxt], 1 - slot):
                cp.start()

    @pl.when(used)
    def _():
        x = xs_ref[...]
        hg = jnp.dot(x, wg_bf[...], preferred_element_type=f32)
        hu = jnp.dot(x, wu_bf[...], preferred_element_type=f32)
        hid = (hg * jax.nn.sigmoid(hg) * hu).astype(bf16)
        ys_ref[...] = jnp.dot(hid, wd_bf[...], preferred_element_type=f32)


def _experts(te, xs, w_gate, w_up, w_down, layer, ys_prev=None):
    tm = MOE_TILE
    row = lambda j, te: (jnp.minimum(j, te[1, 0] - 1), 0)
    hbm = pl.BlockSpec(memory_space=pl.ANY)
    reuse = ys_prev is not None
    return pl.pallas_call(
        functools.partial(_expert_kernel, layer=layer, reuse=reuse),
        grid_spec=pltpu.PrefetchScalarGridSpec(
            num_scalar_prefetch=1,
            grid=(MOE_ROWS // tm,),
            in_specs=[pl.BlockSpec((tm, D_MODEL), row), hbm, hbm, hbm] + ([hbm] if reuse else []),
            out_specs=pl.BlockSpec((tm, D_MODEL), row if reuse else (lambda j, te: (j, 0))),
            scratch_shapes=[pltpu.VMEM((2, D_MODEL, D_EXPERT), f32), pltpu.VMEM((2, D_MODEL, D_EXPERT), f32),
                            pltpu.VMEM((2, D_EXPERT, D_MODEL), f32),
                            pltpu.VMEM((D_MODEL, D_EXPERT), bf16), pltpu.VMEM((D_MODEL, D_EXPERT), bf16),
                            pltpu.VMEM((D_EXPERT, D_MODEL), bf16),
                            pltpu.SMEM((1,), jnp.int32), pltpu.SemaphoreType.DMA((2,))]),
        out_shape=jax.ShapeDtypeStruct((MOE_ROWS, D_MODEL), f32),
        input_output_aliases={5: 0} if reuse else {},
        compiler_params=pltpu.CompilerParams(dimension_semantics=("arbitrary",), vmem_limit_bytes=VMEM_LIMIT),
        name="moe_experts",
    )(te, xs, w_gate, w_up, w_down, *([ys_prev] if reuse else []))


def _gather_expert_rows(pos0_ref, pos1_ref, ys_ref, buf, sem, tile_of):
    rows = buf.shape[2]
    i = pl.program_id(0)
    slot = i % 2

    def issue(tile, sl):
        base = tile * rows

        def body(t, carry):
            for s, pos_ref in enumerate((pos0_ref, pos1_ref)):
                pltpu.make_async_copy(ys_ref.at[pl.ds(pos_ref[base + t], 1)], buf.at[sl, s, pl.ds(t, 1)],
                                      sem.at[sl]).start()
            return carry

        lax.fori_loop(0, rows, body, 0, unroll=8)

    @pl.when(i == 0)
    def _():
        issue(tile_of(0), 0)

    @pl.when(i + 1 < pl.num_programs(0))
    def _():
        issue(tile_of(i + 1), 1 - slot)

    for s in range(2):
        pltpu.make_async_copy(ys_ref.at[pl.ds(0, rows)], buf.at[slot, s], sem.at[slot]).wait()
    return buf[slot, 0], buf[slot, 1]


def _moe_residual(pos0_ref, pos1_ref, ys_ref, x1_ref, rc_ref, mod_ref, buf, sem, tile_of=lambda step: step):
    y0, y1 = _gather_expert_rows(pos0_ref, pos1_ref, ys_ref, buf, sem, tile_of)
    rc = rc_ref[...]
    return x1_ref[...] + mod_ref[0][5:6] * (rc[:, 2:3] * y0 + rc[:, 3:4] * y1)


def _final_kernel(pos0_ref, pos1_ref, ys_ref, x1_ref, rc_ref, mod_ref, fn_ref, yc_ref, yl_ref, buf, sem):
    x2 = _moe_residual(pos0_ref, pos1_ref, ys_ref, x1_ref, rc_ref, mod_ref, buf, sem)
    y = x2 * lax.rsqrt(jnp.mean(x2 * x2, axis=-1, keepdims=True) + EPS) * fn_ref[...]
    is_ctx = pl.program_id(0) < N_CTX // x1_ref.shape[0]

    @pl.when(is_ctx)
    def _():
        yc_ref[...] = y

    @pl.when(jnp.logical_not(is_ctx))
    def _():
        yl_ref[...] = y


def _final_combine(pos0, pos1, ys, x1, rc, mods, fn):
    tc = TOK_TILE
    row = lambda i, p0, p1: (i, 0)
    n_ctx_tiles = N_CTX // tc
    return pl.pallas_call(
        _final_kernel,
        grid_spec=pltpu.PrefetchScalarGridSpec(
            num_scalar_prefetch=2,
            grid=(N_TOK // tc,),
            in_specs=[pl.BlockSpec(memory_space=pl.ANY),
                      pl.BlockSpec((tc, D_MODEL), row),
                      pl.BlockSpec((tc, LANE), row),
                      pl.BlockSpec((1, 6, D_MODEL), lambda i, p0, p1: (_mod_row(i, tc), 0, 0)),
                      pl.BlockSpec((1, D_MODEL), lambda i, p0, p1: (0, 0))],
            out_specs=[pl.BlockSpec((tc, D_MODEL), lambda i, p0, p1: (jnp.minimum(i, n_ctx_tiles - 1), 0)),
                       pl.BlockSpec((tc, D_MODEL), lambda i, p0, p1: (jnp.maximum(i - n_ctx_tiles, 0), 0))],
            scratch_shapes=[pltpu.VMEM((2, 2, tc, D_MODEL), f32), pltpu.SemaphoreType.DMA((2,))]),
        out_shape=[jax.ShapeDtypeStruct((N_CTX, D_MODEL), f32), jax.ShapeDtypeStruct((N_LAT, D_MODEL), f32)],
        compiler_params=pltpu.CompilerParams(dimension_semantics=("arbitrary",), vmem_limit_bytes=VMEM_LIMIT),
        name="moe_combine_final",
    )(pos0, pos1, ys, x1, rc, mods, fn)


def _moe_experts(h2, route_t, w_gate, w_up, w_down, layer, xs_buf, ys_prev):
    rows, te, runs = _rank(route_t)
    xs = _dispatch(runs[:, :3, :N_EXPERTS].reshape(-1), h2, rows, xs_buf)
    return rows[0], rows[1], _experts(te, xs, w_gate, w_up, w_down, layer, ys_prev), xs


def _pad_heads(w):
    lead = w.shape[:-1]
    w = w.reshape(*lead, ML_HEADS, ML_DIM)
    w = jnp.pad(w, [(0, 0)] * len(lead) + [(0, 0), (0, ML_PAD - ML_DIM)])
    return w.reshape(*lead, ML_PW)


def _pack_in_cols(wb):
    o = 0
    qa = wb[..., o:o + NA_WIDTH] * (NA_DIM ** -0.5)
    ka = wb[..., o + NA_WIDTH:o + 2 * NA_WIDTH]
    va = wb[..., o + 2 * NA_WIDTH:o + 3 * NA_WIDTH]
    o += 3 * NA_WIDTH
    qb, kb, vb, ob = [_pad_heads(wb[..., o + j * ML_WIDTH:o + (j + 1) * ML_WIDTH]) for j in range(4)]
    o += 4 * ML_WIDTH
    gates = wb[..., o:o + N_GATE_COLS]
    o += N_GATE_COLS
    pool = wb[..., o:o + POOL_WIDTH]
    main = jnp.concatenate([qa, ka, va, qb, vb, ob, pool], axis=-1)
    gates_p = jnp.pad(gates, [(0, 0)] * (gates.ndim - 1) + [(0, LANE - N_GATE_COLS)])
    return main, gates_p, jnp.concatenate([kb, gates], axis=-1)


def _pack_w_in(w, b):
    w_main, w_gates, w_feat = _pack_in_cols(w)
    b_main, b_gates, b_feat = _pack_in_cols(b.astype(f32))
    return (w_main.astype(bf16), b_main[:, None], w_gates.astype(bf16), b_gates[:, None],
            jnp.swapaxes(w_feat, 1, 2).astype(bf16), b_feat[:, :, None])


def _pack_w_out(w):
    n_l = w.shape[0]
    wb = w[:, NA_WIDTH:NA_WIDTH + ML_WIDTH].reshape(n_l, ML_HEADS, ML_DIM, D_MODEL)
    wb = jnp.pad(wb, ((0, 0), (0, 0), (0, ML_PAD - ML_DIM), (0, 0))).reshape(n_l, ML_PW, D_MODEL)
    return jnp.concatenate([w[:, :NA_WIDTH], wb, w[:, NA_WIDTH + ML_WIDTH:]], axis=1).astype(bf16)


def _block_diag(w):
    n_l, g, c, _ = w.shape
    eye = jnp.eye(g, dtype=w.dtype)
    return (eye[None, :, None, :, None] * w[:, :, :, None, :]).reshape(n_l, g * c, g * c)


def kernel(x_prompt, x_sample, cache_k_attn, cache_v_attn, state_mlstm_C, state_mlstm_n, state_mlstm_m, c, c_ctx,
           w_ada, b_ada, norm1, w_in, b_in, rpb, ml_norm, w_pool, pool_scale, w_out, norm2, w_router, b_router,
           w_gate, w_up, w_down, final_norm):
    dt = x_prompt.dtype
    x_ctx = x_prompt.reshape(N_CTX, D_MODEL).astype(f32)
    x_lat = x_sample.reshape(N_LAT, D_MODEL).astype(f32)
    x_lat_block0 = 0
    cvec = jnp.concatenate([c_ctx[None], c, jnp.zeros((8 - 1 - DEC_BATCH, D_MODEL), c.dtype)], axis=0).astype(f32)
    mods_all = _ada(cvec, w_ada.astype(f32), b_ada.astype(f32))
    mods_all = mods_all[:, :1 + DEC_BATCH].reshape(DEPTH, 1 + DEC_BATCH, 6, D_MODEL)

    wr_t = w_router.astype(f32).T
    br_t = b_router.astype(f32)[:, None]
    fn = final_norm.astype(f32)[None]

    na_bias = _na_bias_tables(rpb)
    xs_buf = jnp.zeros((MOE_ROWS, D_MODEL), bf16)
    in_params = (norm1.astype(f32)[:, None],) + _pack_w_in(w_in, b_in)
    out_params = (_block_diag(w_pool.astype(f32)).astype(bf16), pool_scale.astype(f32)[:, None],
                  _pad_heads(ml_norm.astype(f32))[:, None], _pack_w_out(w_out), norm2.astype(f32)[:, None])

    new_k = jnp.zeros((BATCH, DEPTH, SEQ, NA_WIDTH), f32)
    new_v = jnp.zeros_like(new_k)
    Cs, ns, ms = [], [], []
    pending = None
    for l in range(DEPTH):
        mods = mods_all[l]
        if pending is None:
            qkva, new_k, new_v, qvo, kt, gates, gates_t, pin = _in_proj(x_ctx, x_lat, mods, l, *in_params,
                                                                        new_k, new_v)
        else:
            x, qkva, new_k, new_v, qvo, kt, gates, gates_t, pin = _moe_in_proj(
                *pending, mods_all[l - 1], mods, l, *in_params, new_k, new_v)
            x_ctx, x_lat, x_lat_block0 = x, x, N_CTX // TOK_TILE

        oa_ctx = _ctx_attention(qkva.reshape(N_TOK // SEQ, SEQ, W_A))
        ck = (cache_k_attn[:, l].reshape(DEC_BATCH, PAST_LEN, NA_WIDTH)).astype(bf16)
        cv = (cache_v_attn[:, l].reshape(DEC_BATCH, PAST_LEN, NA_WIDTH)).astype(bf16)
        oa_lat = _neighborhood_attention(qkva.reshape(N_TOK // DEC_SEQ, DEC_SEQ, W_A), ck, cv, na_bias, l)

        c_l, m_l = _pack_ml_state(state_mlstm_C[:, l], state_mlstm_n[:, l], state_mlstm_m[:, l])
        hf, hb, c_fin, m_fin = _mlstm(qvo, kt, gates, gates_t, c_l, m_l)
        C_l, n_l, m_l2 = _unpack_ml_state(c_fin[:BATCH], m_fin[:BATCH])
        Cs.append(C_l)
        ns.append(n_l)
        ms.append(m_l2)

        x1, h2, route_t, rc = _out_proj(x_ctx, x_lat, x_lat_block0, mods, l,
                                        oa_ctx.reshape(N_CTX, NA_WIDTH), oa_lat.reshape(N_LAT, NA_WIDTH),
                                        hf.reshape(N_TOK, ML_PW), hb.reshape(N_TOK, ML_PW), qvo, pin,
                                        *out_params, wr_t, br_t)
        ys_prev = None if pending is None else pending[2]
        pos0, pos1, ys, xs_buf = _moe_experts(h2, route_t, w_gate, w_up, w_down, l, xs_buf, ys_prev)
        pending = (pos0, pos1, ys, x1, rc)

    x = _final_combine(*pending, mods_all[DEPTH - 1], fn)
    y_prompt = x[0].reshape(BATCH, SEQ, D_MODEL).astype(dt)
    y_sample = x[1].reshape(DEC_BATCH, DEC_SEQ, D_MODEL).astype(dt)
    new_k, new_v = (a.reshape(BATCH, DEPTH, SEQ, NA_HEADS, NA_DIM).astype(dt) for a in (new_k, new_v))
    return (y_prompt, y_sample, new_k, new_v,
            jnp.stack(Cs, axis=1).astype(dt), jnp.stack(ns, axis=1).astype(dt), jnp.stack(ms, axis=1).astype(dt))
```

```python
import functools

import numpy as np
import jax
import jax.numpy as jnp
from jax import lax
from jax.experimental import pallas as pl
from jax.experimental.pallas import tpu as pltpu

D_MODEL = 1024
BATCH = 16
SEQ = 256
DEPTH = 4
DEC_BATCH = 2
DEC_SEQ = 4096
PAST_LEN = 256
GRID_W = 64
EPS = 1e-6
NEG_INF = -1e30
NA_HEADS = 6
NA_DIM = 64
NA_WIDTH = NA_HEADS * NA_DIM
NA_ROWS = 8
NA_COLS = 16
RPB_ROWS = 2 * NA_ROWS - 1
RPB_COLS = 2 * NA_COLS - 1
ML_HEADS = 4
ML_DIM = 96
ML_WIDTH = ML_HEADS * ML_DIM
POOL_WINDOWS = (2, 4, 8, 16)
POOL_GROUPS = 4
POOL_DIM = 64
POOL_WIDTH = POOL_GROUPS * POOL_DIM
N_GATE_COLS = 4 * ML_HEADS
N_EXPERTS = 16
N_EXPERT_GROUPS = 4
EXPERTS_PER_GROUP = N_EXPERTS // N_EXPERT_GROUPS
D_EXPERT = 512
ADA_DIM = 6 * D_MODEL

N_CTX = BATCH * SEQ
N_LAT = DEC_BATCH * DEC_SEQ
N_TOK = N_CTX + N_LAT
LANE = 128
ML_PAD = LANE
ML_PW = ML_HEADS * ML_PAD
CAUG = ML_PAD
NA_PAIRS = NA_HEADS // 2
TOK_TILE = 512
ML_CHUNK = 256
NA_QROWS = 4
NA_KROWS = NA_QROWS + NA_ROWS - 1
POOL_HALO = max(POOL_WINDOWS) // 2
POOL_BLOCK = 128
MOE_TILE = 512
MOE_CHUNK = 16
MOE_LOCAL_ROWS = -(-(2 * TOK_TILE + N_EXPERTS * (MOE_CHUNK - 1)) // LANE) * LANE
MOE_ROWS = -(-(2 * N_TOK + (N_TOK // TOK_TILE) * N_EXPERTS * (MOE_CHUNK - 1) + N_EXPERTS * (MOE_TILE - 1))
             // MOE_TILE) * MOE_TILE
VMEM_LIMIT = 56 * 1024 * 1024

W_A = 3 * NA_WIDTH
W_B = 3 * ML_PW
N_TCOLS = ML_PW + N_GATE_COLS
W_MAIN = W_A + W_B + POOL_WIDTH

f32 = jnp.float32
bf16 = jnp.bfloat16
HI = lax.Precision.HIGHEST


def _nt(a, b, **kw):
    return lax.dot_general(a, b, (((1,), (1,)), ((), ())), preferred_element_type=f32, **kw)


def _mod_row(i, tile):
    n_ctx_tiles = N_CTX // tile
    per_batch = DEC_SEQ // tile
    return jnp.where(i < n_ctx_tiles, 0, 1 + (i - n_ctx_tiles) // per_batch)


def _ada_kernel(c_ref, w_ref, b_ref, o_ref):
    s = c_ref[...]
    s = s * jax.nn.sigmoid(s)
    o_ref[0] = jnp.dot(s.astype(bf16), w_ref[0].astype(bf16), preferred_element_type=f32) + b_ref[0]


def _ada(cvec, w_ada, b_ada):
    nj = ADA_DIM // D_MODEL
    return pl.pallas_call(
        _ada_kernel,
        grid=(DEPTH, nj),
        in_specs=[pl.BlockSpec((8, D_MODEL), lambda l, j: (0, 0)),
                  pl.BlockSpec((1, D_MODEL, D_MODEL), lambda l, j: (l, 0, j)),
                  pl.BlockSpec((1, 1, D_MODEL), lambda l, j: (l, 0, j))],
        out_specs=pl.BlockSpec((1, 8, D_MODEL), lambda l, j: (l, 0, j)),
        out_shape=jax.ShapeDtypeStruct((DEPTH, 8, ADA_DIM), f32),
        name="ada_mod",
    )(cvec, w_ada, b_ada.reshape(DEPTH, 1, ADA_DIM))


def _in_tile(step):
    return (step + N_CTX // TOK_TILE) % (N_TOK // TOK_TILE)


def _in_kernel(xc_ref, xl_ref, *refs):
    is_ctx = _in_tile(pl.program_id(0)) < N_CTX // xc_ref.shape[0]
    _in_body(jnp.where(is_ctx, xc_ref[...], xl_ref[...]), *refs)


def _moe_in_kernel(pos0_ref, pos1_ref, ys_ref, x1_ref, rc_ref, mod_prev_ref, *refs):
    in_refs, x_out_ref, out_refs, (buf, sem) = refs[:10], refs[10], refs[11:-2], refs[-2:]
    x = _moe_residual(pos0_ref, pos1_ref, ys_ref, x1_ref, rc_ref, mod_prev_ref, buf, sem, tile_of=_in_tile)
    x_out_ref[...] = x
    _in_body(x, *in_refs, *out_refs)


def _in_body(x, mod_ref, n1_ref, w_ref, b_ref, wg_ref, bg_ref, wt_ref, bt_ref, k_in_ref, v_in_ref,
             a_ref, k_ref, v_ref, b_out_ref, kt_ref, g_ref, gt_ref, pin_ref):
    del k_in_ref, v_in_ref
    mod = mod_ref[0]
    h = x * lax.rsqrt(jnp.mean(x * x, axis=-1, keepdims=True) + EPS) * n1_ref[...]
    h = (h * (1.0 + mod[1:2]) + mod[0:1]).astype(bf16)
    pa = jnp.dot(h, w_ref[:, 0:W_A], preferred_element_type=f32) + b_ref[:, 0:W_A]
    a_ref[...] = pa.astype(bf16)
    k_ref[...] = pa[:, NA_WIDTH:2 * NA_WIDTH].reshape(k_ref.shape)
    v_ref[...] = pa[:, 2 * NA_WIDTH:W_A].reshape(v_ref.shape)
    for j in range(3):
        lo = W_A + j * ML_PW
        pb = jnp.dot(h, w_ref[:, lo:lo + ML_PW], preferred_element_type=f32) + b_ref[:, lo:lo + ML_PW]
        if j == 0:
            pb = pb * (ML_DIM ** -0.5)
        b_out_ref[:, j * ML_PW:(j + 1) * ML_PW] = pb.astype(bf16)
    lo = W_A + W_B
    pin_ref[...] = jnp.dot(h, w_ref[:, lo:lo + POOL_WIDTH], preferred_element_type=f32) + b_ref[:, lo:lo + POOL_WIDTH]
    g_ref[...] = jnp.dot(h, wg_ref[...], preferred_element_type=f32) + bg_ref[...]
    t = _nt(wt_ref[...], h) + bt_ref[...]
    kt_ref[...] = t[0:ML_PW].astype(bf16)
    gt_ref[...] = t[ML_PW:N_TCOLS]


def _in_proj_specs(layer):
    tm = TOK_TILE
    lyr = lambda shape: pl.BlockSpec((None,) + shape, lambda i, *_: (layer, 0, 0))
    rows = lambda width: pl.BlockSpec((tm, width), lambda i, *_: (_in_tile(i), 0))
    cols = lambda height: pl.BlockSpec((height, tm), lambda i, *_: (0, _in_tile(i)))
    param_specs = [pl.BlockSpec((1, 6, D_MODEL), lambda i, *_: (_mod_row(_in_tile(i), tm), 0, 0)),
                   lyr((1, D_MODEL)), lyr((D_MODEL, W_MAIN)), lyr((1, W_MAIN)), lyr((D_MODEL, LANE)), lyr((1, LANE)),
                   lyr((N_TCOLS, D_MODEL)), lyr((N_TCOLS, 1)),
                   pl.BlockSpec(memory_space=pl.ANY), pl.BlockSpec(memory_space=pl.ANY)]
    n_ctx_tiles = N_CTX // tm
    kv_spec = pl.BlockSpec((tm // SEQ, None, SEQ, NA_WIDTH),
                           lambda i, *_: (jnp.where(_in_tile(i) < n_ctx_tiles, _in_tile(i), 0), layer, 0, 0))
    kv_shape = jax.ShapeDtypeStruct((BATCH, DEPTH, SEQ, NA_WIDTH), f32)
    out_specs = [rows(W_A), kv_spec, kv_spec, rows(W_B), cols(ML_PW), rows(LANE), cols(N_GATE_COLS),
                 rows(POOL_WIDTH)]
    out_shape = [jax.ShapeDtypeStruct((N_TOK, W_A), bf16), kv_shape, kv_shape,
                 jax.ShapeDtypeStruct((N_TOK, W_B), bf16),
                 jax.ShapeDtypeStruct((ML_PW, N_TOK), bf16),
                 jax.ShapeDtypeStruct((N_TOK, LANE), f32),
                 jax.ShapeDtypeStruct((N_GATE_COLS, N_TOK), f32),
                 jax.ShapeDtypeStruct((N_TOK, POOL_WIDTH), f32)]
    return rows, param_specs, out_specs, out_shape


def _in_proj(x_ctx, x_lat, mods, layer, *params):
    tm = TOK_TILE
    n_ctx_tiles = N_CTX // tm
    rows, param_specs, out_specs, out_shape = _in_proj_specs(layer)
    return pl.pallas_call(
        _in_kernel,
        grid=(N_TOK // tm,),
        in_specs=[pl.BlockSpec((tm, D_MODEL), lambda i: (jnp.where(_in_tile(i) < n_ctx_tiles, _in_tile(i), 0), 0)),
                  pl.BlockSpec((tm, D_MODEL), lambda i: (jnp.maximum(_in_tile(i) - n_ctx_tiles, 0), 0))] + param_specs,
        out_specs=out_specs,
        out_shape=out_shape,
        input_output_aliases={2 + len(param_specs) - 2: 1, 2 + len(param_specs) - 1: 2},
        compiler_params=pltpu.CompilerParams(dimension_semantics=("arbitrary",), vmem_limit_bytes=VMEM_LIMIT),
        name="in_proj",
    )(x_ctx, x_lat, mods, *params)


def _moe_in_proj(pos0, pos1, ys, x1, rc, mods_prev, mods, layer, *params):
    tm = TOK_TILE
    rows, param_specs, out_specs, out_shape = _in_proj_specs(layer)
    return pl.pallas_call(
        _moe_in_kernel,
        grid_spec=pltpu.PrefetchScalarGridSpec(
            num_scalar_prefetch=2,
            grid=(N_TOK // tm,),
            in_specs=[pl.BlockSpec(memory_space=pl.ANY), rows(D_MODEL), rows(LANE),
                      pl.BlockSpec((1, 6, D_MODEL), lambda i, *_: (_mod_row(_in_tile(i), tm), 0, 0))] + param_specs,
            out_specs=[rows(D_MODEL)] + out_specs,
            scratch_shapes=[pltpu.VMEM((2, 2, tm, D_MODEL), f32), pltpu.SemaphoreType.DMA((2,))]),
        out_shape=[jax.ShapeDtypeStruct((N_TOK, D_MODEL), f32)] + out_shape,
        input_output_aliases={6 + len(param_specs) - 2: 2, 6 + len(param_specs) - 1: 3},
        compiler_params=pltpu.CompilerParams(dimension_semantics=("arbitrary",), vmem_limit_bytes=VMEM_LIMIT),
        name="moe_combine_in_proj",
    )(pos0, pos1, ys, x1, rc, mods_prev, mods, *params)


def _pair_attention(qp, parts):
    lane = lax.broadcasted_iota(jnp.int32, (1, LANE), 1)
    outs = []
    for j in range(2):
        in_half = (lane >= j * NA_DIM) & (lane < (j + 1) * NA_DIM)
        qm = jnp.where(in_half, qp, jnp.zeros_like(qp))
        scores = []
        for k, _, bias in parts:
            s = _nt(qm, k)
            if bias is not None:
                s = s + bias[j]
            scores.append(s)
        m = scores[0].max(axis=-1, keepdims=True)
        for s in scores[1:]:
            m = jnp.maximum(m, s.max(axis=-1, keepdims=True))
        den = None
        acc = None
        for s, (_, v, _) in zip(scores, parts):
            p = jnp.exp(s - m)
            ps = p.sum(axis=-1, keepdims=True)
            den = ps if den is None else den + ps
            o = jnp.dot(p.astype(bf16), v, preferred_element_type=f32)
            acc = o if acc is None else acc + o
        outs.append(acc / den)
    return jnp.where(lane < NA_DIM, outs[0], outs[1])


def _ctx_attn_kernel(q_ref, k_ref, v_ref, o_ref):
    for p in range(NA_PAIRS):
        sl = slice(p * LANE, (p + 1) * LANE)
        o = _pair_attention(q_ref[0, :, sl], [(k_ref[0, :, sl], v_ref[0, :, sl], None)])
        o_ref[0, :, sl] = o.astype(bf16)


def _ctx_attention(qkv):
    blk = lambda c: pl.BlockSpec((1, SEQ, NA_WIDTH), lambda b, c=c: (b, 0, c))
    return pl.pallas_call(
        _ctx_attn_kernel,
        grid=(BATCH,),
        in_specs=[blk(0), blk(1), blk(2)],
        out_specs=pl.BlockSpec((1, SEQ, NA_WIDTH), lambda b: (b, 0, 0)),
        out_shape=jax.ShapeDtypeStruct((BATCH, SEQ, NA_WIDTH), bf16),
        name="ctx_attention",
    )(qkv, qkv, qkv)


def _na_window_start(rb):
    return jnp.clip(rb * NA_QROWS - NA_ROWS // 2, 0, DEC_SEQ // GRID_W - NA_KROWS)


def _na_bias(tab_ref, head, rb):
    rows = DEC_SEQ // GRID_W
    ws = _na_window_start(rb)
    lane = lax.broadcasted_iota(jnp.int32, (1, NA_KROWS * GRID_W), 1)
    per_qrow = []
    for dq in range(NA_QROWS):
        qr = rb * NA_QROWS + dq
        a0 = ws - qr + (NA_ROWS - 1) + NA_KROWS
        tiles = [tab_ref[head, a0 + 2 * j] for j in range((NA_KROWS + 1) // 2)]
        t = jnp.concatenate(tiles, axis=1)[:, :NA_KROWS * GRID_W]
        lo = (jnp.clip(qr - NA_ROWS // 2, 0, rows - NA_ROWS) - ws) * GRID_W
        ok = (lane >= lo) & (lane < lo + NA_ROWS * GRID_W)
        per_qrow.append(jnp.where(ok, t, NEG_INF))
    return jnp.concatenate(per_qrow, axis=0)


def _na_kernel(q_ref, k_ref, v_ref, ck_ref, cv_ref, tab_ref, o_ref):
    rb = pl.program_id(1)
    start = pl.multiple_of(_na_window_start(rb) * GRID_W, GRID_W)
    nk = NA_KROWS * GRID_W
    for p in range(NA_PAIRS):
        sl = slice(p * LANE, (p + 1) * LANE)
        bias = [_na_bias(tab_ref.at[0], 2 * p + j, rb) for j in range(2)]
        parts = [(k_ref[0, pl.ds(start, nk), sl], v_ref[0, pl.ds(start, nk), sl], bias),
                 (ck_ref[0, :, sl], cv_ref[0, :, sl], None)]
        o = _pair_attention(q_ref[0, :, sl], parts)
        o_ref[0, :, sl] = o.astype(bf16)


def _neighborhood_attention(qkv, ck, cv, tables, layer):
    nq = NA_QROWS * GRID_W
    n_rb = DEC_SEQ // nq
    return pl.pallas_call(
        _na_kernel,
        grid=(DEC_BATCH, n_rb),
        in_specs=[pl.BlockSpec((1, nq, NA_WIDTH), lambda b, r: (1 + b, r, 0)),
                  pl.BlockSpec((1, DEC_SEQ, NA_WIDTH), lambda b, r: (1 + b, 0, 1)),
                  pl.BlockSpec((1, DEC_SEQ, NA_WIDTH), lambda b, r: (1 + b, 0, 2)),
                  pl.BlockSpec((1, PAST_LEN, NA_WIDTH), lambda b, r: (b, 0, 0)),
                  pl.BlockSpec((1, PAST_LEN, NA_WIDTH), lambda b, r: (b, 0, 0)),
                  pl.BlockSpec((1,) + tables.shape[1:], lambda b, r: (layer, 0, 0, 0, 0))],
        out_specs=pl.BlockSpec((1, nq, NA_WIDTH), lambda b, r: (b, r, 0)),
        out_shape=jax.ShapeDtypeStruct((DEC_BATCH, DEC_SEQ, NA_WIDTH), bf16),
        compiler_params=pltpu.CompilerParams(dimension_semantics=("arbitrary", "arbitrary"),
                                             vmem_limit_bytes=VMEM_LIMIT),
        name="neighborhood_attention",
    )(qkv, qkv, qkv, ck, cv, tables)


def _na_bias_tables(rpb):
    qc = np.arange(GRID_W)[:, None]
    kc = np.arange(GRID_W)[None, :]
    dc = np.clip(kc - qc + NA_COLS - 1, 0, RPB_COLS - 1)
    col_start = np.clip(qc - NA_COLS // 2, 0, GRID_W - NA_COLS)
    col_ok = (kc >= col_start) & (kc < col_start + NA_COLS)
    pick_col = (dc[None] == np.arange(RPB_COLS)[:, None, None]).astype(np.float32)
    rpb_pad = jnp.pad(rpb.astype(f32), ((0, 0), (0, 0), (NA_KROWS, NA_KROWS + 1), (0, 0)))
    n_a = rpb_pad.shape[2] - 1
    rows2 = jnp.stack([rpb_pad[:, :, :-1], rpb_pad[:, :, 1:]], axis=3)
    pick2 = np.zeros((2, RPB_COLS, GRID_W, 2 * GRID_W), np.float32)
    for j in range(2):
        pick2[j, :, :, j * GRID_W:(j + 1) * GRID_W] = pick_col
    tiles = jnp.einsum('lhajb,jbqc->lhaqc', rows2, pick2, precision=HI)
    a_pad = np.arange(n_a)[:, None] + np.arange(2)[None, :]
    row_ok = (a_pad >= NA_KROWS) & (a_pad < NA_KROWS + RPB_ROWS)
    ok = (row_ok[:, None, :, None] & col_ok[None, :, None, :]).reshape(n_a, GRID_W, 2 * GRID_W)
    return jnp.where(ok[None, None], tiles, NEG_INF)


def _log_sigmoid(x):
    return -(jnp.maximum(-x, 0.0) + jnp.log(1.0 + jnp.exp(-jnp.abs(x))))


def _split3(x):
    hi = x.astype(bf16)
    r1 = x - hi.astype(f32)
    mid = r1.astype(bf16)
    lo = (r1 - mid.astype(f32)).astype(bf16)
    return hi, mid, lo


def _mlstm_kernel(qf_ref, vf_ref, ktf_ref, gf_ref, gtf_ref, qb_ref, vb_ref, ktb_ref, gb_ref, gtb_ref,
                  c0_ref, m0_ref, hf_ref, hb_ref, c_out_ref, m_out_ref, c_scr, m_scr):
    L = ML_CHUNK
    seq, c, n_chunks, _ = _ml_schedule(pl.program_id(0))

    @pl.when(c == 0)
    def _():
        is_ctx = seq < BATCH
        c_scr[...] = jnp.where(is_ctx, 0.0, c0_ref[0])
        m_scr[...] = jnp.where(is_ctx, 0.0, m0_ref[0])

    ri = lax.broadcasted_iota(jnp.int32, (L, L), 0)
    ci = lax.broadcasted_iota(jnp.int32, (L, L), 1)
    lane = lax.broadcasted_iota(jnp.int32, (L, ML_PAD), 1)
    is_ncol = lane == ML_DIM
    lower = ri >= ci
    upper = ri <= ci
    lower_b = jnp.where(lower, 1.0, 0.0).astype(bf16)
    upper_b = jnp.where(upper, 1.0, 0.0).astype(bf16)
    dirs = ((qf_ref, ktf_ref, vf_ref, gf_ref, gtf_ref, hf_ref), (qb_ref, ktb_ref, vb_ref, gb_ref, gtb_ref, hb_ref))
    for d, (q_ref, kt_ref, v_ref, g_ref, gt_ref, h_ref) in enumerate(dirs):
        g = g_ref[...][:, 0:N_GATE_COLS]
        gt = gt_ref[...]
        lf_c = _log_sigmoid(g)
        lf_r = _log_sigmoid(gt)
        b_cols = sum(jnp.dot(lower_b, part, preferred_element_type=f32) for part in _split3(lf_c))
        b_rows = sum(jnp.dot(part, upper_b, preferred_element_type=f32) for part in _split3(lf_r))
        tot_c = jnp.sum(lf_c, axis=0, keepdims=True)
        tot_r = jnp.sum(lf_r, axis=1, keepdims=True)
        visible = lower
        if d == 1:
            b_cols = tot_c - b_cols + lf_c
            b_rows = tot_r - b_rows + lf_r
            visible = upper
        for hd in range(ML_HEADS):
            st = d * ML_HEADS + hd
            ci_ = 2 * ML_HEADS * d + hd
            cf_ = ci_ + ML_HEADS
            sl = slice(hd * ML_PAD, (hd + 1) * ML_PAD)
            bc = b_cols[:, cf_:cf_ + 1]
            br = b_rows[cf_:cf_ + 1, :]
            li_r = gt[ci_:ci_ + 1, :]
            m_prev = m_scr[st:st + 1, 0:1]
            dmat = jnp.where(visible, bc - br + li_r, NEG_INF)
            inter = bc + m_prev
            m_t = jnp.maximum(inter, dmat.max(axis=-1, keepdims=True))
            w_intra = jnp.exp(dmat - m_t)
            w_inter = jnp.exp(inter - m_t)
            qh = q_ref[0, :, sl]
            kht = kt_ref[sl, :]
            v_aug = jnp.where(is_ncol, jnp.ones((), bf16), v_ref[0, :, sl])
            s = (jnp.dot(qh, kht, preferred_element_type=f32) * w_intra).astype(bf16)
            c_aug = c_scr[st]
            na = (w_inter * jnp.dot(qh, c_aug.astype(bf16), preferred_element_type=f32)
                  + jnp.dot(s, v_aug, preferred_element_type=f32))
            den = na[:, ML_DIM:ML_DIM + 1]
            h_ref[0, :, sl] = jnp.where(lane < ML_DIM, na / jnp.maximum(jnp.abs(den), jnp.exp(-m_t)), 0.0)
            b_end = tot_r[cf_:cf_ + 1, :]
            g_row = b_end - br + li_r
            m_new = jnp.maximum(b_end + m_prev, g_row.max(axis=1, keepdims=True))
            decay = jnp.exp(b_end + m_prev - m_new)
            kwt = (kht.astype(f32) * jnp.exp(g_row - m_new)).astype(bf16)
            c_scr[st] = decay * c_aug + jnp.dot(kwt, v_aug, preferred_element_type=f32)
            m_scr[st:st + 1, :] = jnp.broadcast_to(m_new, (1, LANE))

    @pl.when(c == n_chunks - 1)
    def _():
        c_out_ref[0] = c_scr[...]
        m_out_ref[0] = m_scr[...]


def _ml_schedule(s):
    nc_ctx, nc_lat = SEQ // ML_CHUNK, DEC_SEQ // ML_CHUNK
    n_ctx_steps = BATCH * nc_ctx
    is_ctx = s < n_ctx_steps
    t = s - n_ctx_steps
    seq = jnp.where(is_ctx, s // nc_ctx, BATCH + t // nc_lat)
    c = jnp.where(is_ctx, s % nc_ctx, t % nc_lat)
    nc = jnp.where(is_ctx, nc_ctx, nc_lat)
    base = jnp.where(is_ctx, (s // nc_ctx) * nc_ctx, n_ctx_steps + (t // nc_lat) * nc_lat)
    return seq, c, nc, base


def _mlstm(qvo, kt, gates, gates_t, c0, m0):
    L = ML_CHUNK
    n_seq = BATCH + DEC_BATCH

    def fwd(s):
        _, c, _, base = _ml_schedule(s)
        return base + c

    def bwd(s):
        _, c, nc, base = _ml_schedule(s)
        return base + nc - 1 - c

    seq_of = lambda s: _ml_schedule(s)[0]
    lat_of = lambda s: jnp.maximum(seq_of(s) - BATCH, 0)

    def specs(pos):
        return [pl.BlockSpec((1, L, ML_PW), lambda s, j=j: (pos(s), 0, j)) for j in range(2)] + [
            pl.BlockSpec((ML_PW, L), lambda s: (0, pos(s))),
            pl.BlockSpec((L, LANE), lambda s: (pos(s), 0)),
            pl.BlockSpec((N_GATE_COLS, L), lambda s: (0, pos(s)))]

    q3 = qvo.reshape(N_TOK // L, L, W_B)
    n_str = 2 * ML_HEADS
    return pl.pallas_call(
        _mlstm_kernel,
        grid=(N_TOK // L,),
        in_specs=specs(fwd) + specs(bwd) + [
            pl.BlockSpec((1, n_str, ML_PAD, CAUG), lambda s: (lat_of(s), 0, 0, 0)),
            pl.BlockSpec((1, n_str, LANE), lambda s: (lat_of(s), 0, 0))],
        out_specs=[pl.BlockSpec((1, L, ML_PW), lambda s: (fwd(s), 0, 0)),
                   pl.BlockSpec((1, L, ML_PW), lambda s: (bwd(s), 0, 0)),
                   pl.BlockSpec((1, n_str, ML_PAD, CAUG), lambda s: (seq_of(s), 0, 0, 0)),
                   pl.BlockSpec((1, n_str, LANE), lambda s: (seq_of(s), 0, 0))],
        out_shape=[jax.ShapeDtypeStruct((N_TOK // L, L, ML_PW), f32),
                   jax.ShapeDtypeStruct((N_TOK // L, L, ML_PW), f32),
                   jax.ShapeDtypeStruct((n_seq, n_str, ML_PAD, CAUG), f32),
                   jax.ShapeDtypeStruct((n_seq, n_str, LANE), f32)],
        scratch_shapes=[pltpu.VMEM((n_str, ML_PAD, CAUG), f32), pltpu.VMEM((n_str, LANE), f32)],
        compiler_params=pltpu.CompilerParams(dimension_semantics=("arbitrary",), vmem_limit_bytes=VMEM_LIMIT),
        name="mlstm",
    )(q3, q3, kt, gates, gates_t, q3, q3, kt, gates, gates_t, c0, m0)


def _pack_ml_state(C, n, m):
    B = C.shape[0]
    c_aug = jnp.zeros((B, 2, ML_HEADS, ML_PAD, CAUG), f32)
    c_aug = c_aug.at[:, :, :, :ML_DIM, :ML_DIM].set(C.astype(f32))
    c_aug = c_aug.at[:, :, :, :ML_DIM, ML_DIM].set(n.astype(f32))
    m_b = jnp.broadcast_to(m.astype(f32)[..., None], (B, 2, ML_HEADS, LANE))
    return c_aug.reshape(B, 2 * ML_HEADS, ML_PAD, CAUG), m_b.reshape(B, 2 * ML_HEADS, LANE)


def _unpack_ml_state(c_aug, m_b):
    B = c_aug.shape[0]
    c_aug = c_aug.reshape(B, 2, ML_HEADS, ML_PAD, CAUG)
    return (c_aug[:, :, :, :ML_DIM, :ML_DIM], c_aug[:, :, :, :ML_DIM, ML_DIM],
            m_b.reshape(B, 2, ML_HEADS, LANE)[..., 0])


def _pool_rows(u_prev, u_cur, u_next, w_bd, scale, t0, seq_len):
    tm = u_cur.shape[0]
    u_win = jnp.concatenate([u_prev, u_cur, u_next], axis=0)
    u_hi = u_win.astype(bf16)
    u_lo = (u_win - u_hi.astype(f32)).astype(bf16)
    lane = lax.broadcasted_iota(jnp.int32, (1, LANE), 1)
    blocks = []
    for r0 in range(0, tm, POOL_BLOCK):
        win = slice(r0, r0 + POOL_BLOCK + 2 * POOL_HALO)
        t_abs = t0 + r0 + lax.broadcasted_iota(jnp.int32, (POOL_BLOCK, 1), 0)
        s_abs = t0 + r0 - POOL_HALO + lax.broadcasted_iota(jnp.int32, (1, POOL_BLOCK + 2 * POOL_HALO), 1)
        t_loc = t_abs & (seq_len - 1)
        seq_start = t_abs - t_loc
        means = []
        for w in POOL_WINDOWS:
            lo = jnp.maximum(t_loc - w // 2, 0)
            hi = jnp.minimum(t_loc - w // 2 + w, seq_len)
            in_win = (s_abs >= seq_start + lo) & (s_abs < seq_start + hi)
            means.append((jnp.where(in_win, 1.0, 0.0).astype(bf16), 1.0 / (hi - lo).astype(f32)))
        pooled = []
        for p in range(POOL_GROUPS // 2):
            sl = slice(p * LANE, (p + 1) * LANE)
            halves = []
            for a, inv_cnt in means[2 * p:2 * p + 2]:
                tot = (jnp.dot(a, u_hi[win, sl], preferred_element_type=f32)
                       + jnp.dot(a, u_lo[win, sl], preferred_element_type=f32))
                halves.append(tot * inv_cnt)
            pooled.append(jnp.where(lane < POOL_DIM, halves[0], halves[1]) - u_cur[r0:r0 + POOL_BLOCK, sl])
        blocks.append(jnp.concatenate(pooled, axis=1))
    pooled = jnp.concatenate(blocks, axis=0).astype(bf16)
    return jnp.dot(pooled, w_bd, preferred_element_type=f32) * scale


def _top2_sum(a, b, c, d):
    hi1, lo1 = jnp.maximum(a, b), jnp.minimum(a, b)
    hi2, lo2 = jnp.maximum(c, d), jnp.minimum(c, d)
    return jnp.maximum(hi1, hi2) + jnp.maximum(jnp.minimum(hi1, hi2), jnp.maximum(lo1, lo2))


def _first_match(vals, target):
    idx = jnp.full_like(target, float(len(vals) - 1))
    for i in range(len(vals) - 2, -1, -1):
        idx = jnp.where(vals[i] == target, float(i), idx)
    return idx


def _pick(vals, idx):
    out = vals[-1]
    for i in range(len(vals) - 2, -1, -1):
        out = jnp.where(idx == float(i), vals[i], out)
    return out


def _route(logits_t, bias_t):
    scores = jax.nn.sigmoid(logits_t)
    sel = scores + bias_t
    row = lambda a, i: a[i:i + 1, :]
    grp = [_top2_sum(*[row(sel, EXPERTS_PER_GROUP * g + i) for i in range(EXPERTS_PER_GROUP)])
           for g in range(N_EXPERT_GROUPS)]
    best = functools.reduce(jnp.maximum, grp)
    gidx = _first_match(grp, best)
    sel_g = [_pick([row(sel, EXPERTS_PER_GROUP * g + i) for g in range(N_EXPERT_GROUPS)], gidx)
             for i in range(EXPERTS_PER_GROUP)]
    sco_g = [_pick([row(scores, EXPERTS_PER_GROUP * g + i) for g in range(N_EXPERT_GROUPS)], gidx)
             for i in range(EXPERTS_PER_GROUP)]
    i0 = _first_match(sel_g, functools.reduce(jnp.maximum, sel_g))
    rest = [jnp.where(i0 == float(i), -jnp.inf, sel_g[i]) for i in range(EXPERTS_PER_GROUP)]
    i1 = _first_match(rest, functools.reduce(jnp.maximum, rest))
    s0, s1 = _pick(sco_g, i0), _pick(sco_g, i1)
    tot = s0 + s1
    rid = lax.broadcasted_iota(jnp.int32, (LANE, logits_t.shape[1]), 0)
    rows = (EXPERTS_PER_GROUP * gidx + i0, EXPERTS_PER_GROUP * gidx + i1, s0 / tot, s1 / tot)
    out = jnp.zeros(rid.shape, f32)
    for i, r in enumerate(rows):
        out = jnp.where(rid == i, r, out)
    return out


def _out_kernel(xc_ref, xl_ref, mod_ref, oac_ref, oal_ref, hf_ref, hb_ref, ob_ref, up_ref, uc_ref, un_ref, wp_ref,
                psc_ref, mln_ref, wo_ref, n2_ref, wr_ref, br_ref, x1_ref, h2_ref, rt_ref, rc_ref):
    tm = xc_ref.shape[0]
    i = pl.program_id(0)
    is_ctx = i < N_CTX // tm
    mod = mod_ref[0]
    out_a = jnp.where(is_ctx, oac_ref[...], oal_ref[...])
    out_c = _pool_rows(up_ref[...], uc_ref[...], un_ref[...], wp_ref[...], psc_ref[...], i * tm,
                       jnp.where(is_ctx, SEQ, DEC_SEQ)).astype(bf16)
    hsum = hf_ref[...] + hb_ref[...]
    outs_b = []
    for hd in range(ML_HEADS):
        sl = slice(hd * ML_PAD, (hd + 1) * ML_PAD)
        hh = hsum[:, sl]
        ms = jnp.sum(hh * hh, axis=-1, keepdims=True) * (1.0 / ML_DIM)
        hn = hh * lax.rsqrt(ms + EPS) * mln_ref[:, sl]
        outs_b.append((jax.nn.sigmoid(ob_ref[:, sl].astype(f32)) * hn).astype(bf16))
    out_b = jnp.concatenate(outs_b, axis=1)
    mixed = (jnp.dot(out_a, wo_ref[0:NA_WIDTH, :], preferred_element_type=f32)
             + jnp.dot(out_b, wo_ref[NA_WIDTH:NA_WIDTH + ML_PW, :], preferred_element_type=f32)
             + jnp.dot(out_c, wo_ref[NA_WIDTH + ML_PW:, :], preferred_element_type=f32))
    x1 = jnp.where(is_ctx, xc_ref[...], xl_ref[...]) + mod[2:3] * mixed
    x1_ref[...] = x1
    h2 = x1 * lax.rsqrt(jnp.mean(x1 * x1, axis=-1, keepdims=True) + EPS) * n2_ref[...]
    h2 = h2 * (1.0 + mod[4:5]) + mod[3:4]
    h2_ref[...] = h2.astype(bf16)
    h_hi = h2.astype(bf16)
    h_lo = (h2 - h_hi.astype(f32)).astype(bf16)
    w_hi = wr_ref[...].astype(bf16)
    w_lo = (wr_ref[...] - w_hi.astype(f32)).astype(bf16)
    route_t = _route(_nt(w_hi, h_hi) + (_nt(w_hi, h_lo) + _nt(w_lo, h_hi)), br_ref[...])
    rt_ref[...] = route_t[0:8]
    rc_ref[...] = route_t.T


def _out_proj(x_ctx, x_lat, x_lat_block0, mods, layer, oa_ctx, oa_lat, hf, hb, qvo, pin, w_bd, psc, mln, wo, n2,
              wr_t, br_t):
    tm = TOK_TILE
    const = lambda i: (0, 0)
    lyr = lambda shape: pl.BlockSpec((None,) + shape, lambda i: (layer, 0, 0))
    row = lambda i: (i, 0)
    n_ctx_tiles = N_CTX // tm
    halo_blocks = tm // POOL_HALO
    return pl.pallas_call(
        _out_kernel,
        grid=(N_TOK // tm,),
        in_specs=[pl.BlockSpec((tm, D_MODEL), lambda i: (jnp.minimum(i, n_ctx_tiles - 1), 0)),
                  pl.BlockSpec((tm, D_MODEL), lambda i: (jnp.maximum(i - n_ctx_tiles, 0) + x_lat_block0, 0)),
                  pl.BlockSpec((1, 6, D_MODEL), lambda i: (_mod_row(i, tm), 0, 0)),
                  pl.BlockSpec((tm, NA_WIDTH), lambda i: (jnp.minimum(i, n_ctx_tiles - 1), 0)),
                  pl.BlockSpec((tm, NA_WIDTH), lambda i: (jnp.maximum(i - n_ctx_tiles, 0), 0)),
                  pl.BlockSpec((tm, ML_PW), row),
                  pl.BlockSpec((tm, ML_PW), row),
                  pl.BlockSpec((tm, ML_PW), lambda i: (i, 2)),
                  pl.BlockSpec((POOL_HALO, POOL_WIDTH), lambda i: (jnp.maximum(i * halo_blocks - 1, 0), 0)),
                  pl.BlockSpec((tm, POOL_WIDTH), row),
                  pl.BlockSpec((POOL_HALO, POOL_WIDTH),
                               lambda i: (jnp.minimum((i + 1) * halo_blocks, N_TOK // POOL_HALO - 1), 0)),
                  lyr((POOL_WIDTH, POOL_WIDTH)), lyr((1, POOL_WIDTH)), lyr((1, ML_PW)),
                  lyr((NA_WIDTH + ML_PW + POOL_WIDTH, D_MODEL)), lyr((1, D_MODEL)),
                  pl.BlockSpec((N_EXPERTS, D_MODEL), const),
                  pl.BlockSpec((N_EXPERTS, 1), const)],
        out_specs=[pl.BlockSpec((tm, D_MODEL), row),
                   pl.BlockSpec((tm, D_MODEL), row),
                   pl.BlockSpec((8, tm), lambda i: (0, i)),
                   pl.BlockSpec((tm, LANE), row)],
        out_shape=[jax.ShapeDtypeStruct((N_TOK, D_MODEL), f32),
                   jax.ShapeDtypeStruct((N_TOK, D_MODEL), bf16),
                   jax.ShapeDtypeStruct((8, N_TOK), f32),
                   jax.ShapeDtypeStruct((N_TOK, LANE), f32)],
        compiler_params=pltpu.CompilerParams(dimension_semantics=("arbitrary",), vmem_limit_bytes=VMEM_LIMIT),
        name="out_proj_router",
    )(x_ctx, x_lat, mods, oa_ctx, oa_lat, hf, hb, qvo, pin, pin, pin, w_bd, psc, mln, wo, n2, wr_t, br_t)


def _ceil_to(x, m):
    return jnp.floor((x + (m - 1)) * (1.0 / m)) * m


def _prefix_over_experts(v):
    er = lax.broadcasted_iota(jnp.int32, (N_EXPERTS, N_EXPERTS), 0)
    ec = lax.broadcasted_iota(jnp.int32, (N_EXPERTS, N_EXPERTS), 1)
    return jnp.dot(jnp.where(ec < er, 1.0, 0.0), v, preferred_element_type=f32, precision=HI)


def _experts_to_lanes(v):
    sub = lax.broadcasted_iota(jnp.int32, (N_EXPERTS, LANE), 0)
    lane = lax.broadcasted_iota(jnp.int32, (N_EXPERTS, LANE), 1)
    return jnp.sum(jnp.where(sub == lane, v, 0.0), axis=0, keepdims=True)


def _expert_hits(rt):
    rid = lax.broadcasted_iota(jnp.int32, (N_EXPERTS, rt.shape[1]), 0).astype(f32)
    oh0 = rid == rt[0:1, :]
    oh1 = rid == rt[1:2, :]
    both = jnp.where(oh0 | oh1, 1.0, 0.0)
    runs = jnp.broadcast_to(_ceil_to(jnp.sum(both, axis=1, keepdims=True), MOE_CHUNK), (N_EXPERTS, LANE))
    return oh0, oh1, both, runs


def _rank_kernel(rt_all_ref, rt_ref, pos_ref, te_ref, tab_ref, carry_ref):
    tm = rt_ref.shape[1]
    step = pl.program_id(0)

    @pl.when(step == 0)
    def _():
        totals = jnp.zeros((N_EXPERTS, LANE), f32)
        for i in range(rt_all_ref.shape[1] // tm):
            totals = totals + _expert_hits(rt_all_ref[:, i * tm:(i + 1) * tm])[3]
        padded = _ceil_to(totals, MOE_TILE)
        off = _prefix_over_experts(padded)
        carry_ref[...] = off
        total = jnp.sum(padded, axis=0, keepdims=True)
        n_used = total * (1.0 / MOE_TILE)
        tile = lax.broadcasted_iota(jnp.int32, (1, LANE), 1).astype(f32)
        row0 = jnp.minimum(tile, n_used - 1.0) * MOE_TILE
        expert = jnp.sum(jnp.where(off <= row0, 1.0, 0.0), axis=0, keepdims=True) - 1.0
        sub = lax.broadcasted_iota(jnp.int32, (8, LANE), 0)
        te_ref[...] = jnp.where(sub == 0, expert, jnp.where(sub == 1, n_used, 0.0)).astype(jnp.int32)

    @pl.when(step > 0)
    def _():
        oh0, oh1, both, runs = _expert_hits(rt_ref[...])
        sr = lax.broadcasted_iota(jnp.int32, (tm, tm), 0)
        sc = lax.broadcasted_iota(jnp.int32, (tm, tm), 1)
        earlier = jnp.dot(both.astype(bf16), jnp.where(sr < sc, 1.0, 0.0).astype(bf16),
                          preferred_element_type=f32)
        g_off = carry_ref[...]
        l_off = _prefix_over_experts(runs)
        g_row = g_off[:, 0:1] + earlier
        l_row = l_off[:, 0:1] + earlier
        pick = lambda oh, v: jnp.sum(jnp.where(oh, v, 0.0), axis=0, keepdims=True)
        rows = (pick(oh0, g_row), pick(oh1, g_row), pick(oh0, l_row), pick(oh1, l_row))
        sub = lax.broadcasted_iota(jnp.int32, (8, tm), 0)
        out = jnp.zeros((8, tm), f32)
        for k, r in enumerate(rows):
            out = jnp.where(sub == k, r, out)
        pos_ref[...] = out.astype(jnp.int32)
        sub = lax.broadcasted_iota(jnp.int32, (8, LANE), 0)
        tab = jnp.zeros((8, LANE), f32)
        for k, v in enumerate((runs * (1.0 / MOE_CHUNK), l_off, g_off)):
            tab = jnp.where(sub == k, _experts_to_lanes(v), tab)
        tab_ref[0] = tab.astype(jnp.int32)
        carry_ref[...] = g_off + runs


def _rank(route_t):
    tm = TOK_TILE
    n_tiles = N_TOK // tm
    tile_of = lambda s: jnp.maximum(s - 1, 0)
    return pl.pallas_call(
        _rank_kernel,
        grid=(1 + n_tiles,),
        in_specs=[pl.BlockSpec((8, N_TOK), lambda s: (0, 0)),
                  pl.BlockSpec((8, tm), lambda s: (0, tile_of(s)))],
        out_specs=[pl.BlockSpec((8, tm), lambda s: (0, tile_of(s))),
                   pl.BlockSpec((8, LANE), lambda s: (0, 0)),
                   pl.BlockSpec((1, 8, LANE), lambda s: (tile_of(s), 0, 0))],
        out_shape=[jax.ShapeDtypeStruct((8, N_TOK), jnp.int32),
                   jax.ShapeDtypeStruct((8, LANE), jnp.int32),
                   jax.ShapeDtypeStruct((n_tiles, 8, LANE), jnp.int32)],
        scratch_shapes=[pltpu.VMEM((N_EXPERTS, LANE), f32)],
        compiler_params=pltpu.CompilerParams(dimension_semantics=("arbitrary",)),
        name="moe_rank",
    )(route_t, route_t)


def _dispatch_kernel(tab_ref, h_ref, rows_ref, xs_in_ref, xs_ref, loc, sem):
    del xs_in_ref
    tm = h_ref.shape[0]
    i = pl.program_id(0)
    slot = i % 2
    rid = lax.broadcasted_iota(jnp.int32, (MOE_LOCAL_ROWS, tm), 0)
    sel = (rid == rows_ref[2:3, :]) | (rid == rows_ref[3:4, :])
    loc[slot] = jnp.dot(jnp.where(sel, 1.0, 0.0).astype(bf16), h_ref[...], preferred_element_type=f32).astype(bf16)

    def chunk_copy(sl, src_row, dst_row):
        return pltpu.make_async_copy(loc.at[sl, pl.ds(pl.multiple_of(src_row, MOE_CHUNK), MOE_CHUNK)],
                                     xs_ref.at[pl.ds(pl.multiple_of(dst_row, MOE_CHUNK), MOE_CHUNK)], sem.at[sl])

    def chunks_of(tile):
        return sum(tab_ref[3 * tile * N_EXPERTS + e] for e in range(N_EXPERTS))

    def wait_chunks(sl, n):
        def wait_one(c, carry):
            chunk_copy(sl, 0, 0).wait()
            return carry

        lax.fori_loop(0, n, wait_one, 0)

    for e in range(N_EXPERTS):
        n_chunks, l_off, g_off = (tab_ref[(3 * i + k) * N_EXPERTS + e] for k in range(3))

        def issue(c, carry, l_off=l_off, g_off=g_off):
            chunk_copy(slot, l_off + c * MOE_CHUNK, g_off + c * MOE_CHUNK).start()
            return carry

        lax.fori_loop(0, n_chunks, issue, 0)

    @pl.when(i > 0)
    def _():
        wait_chunks(1 - slot, chunks_of(i - 1))

    @pl.when(i == pl.num_programs(0) - 1)
    def _():
        wait_chunks(slot, chunks_of(i))


def _dispatch(run_table, h2, rows, xs_init):
    tm = TOK_TILE
    return pl.pallas_call(
        _dispatch_kernel,
        grid_spec=pltpu.PrefetchScalarGridSpec(
            num_scalar_prefetch=1,
            grid=(N_TOK // tm,),
            in_specs=[pl.BlockSpec((tm, D_MODEL), lambda i, tab: (i, 0)),
                      pl.BlockSpec((8, tm), lambda i, tab: (0, i)),
                      pl.BlockSpec(memory_space=pl.ANY)],
            out_specs=pl.BlockSpec(memory_space=pl.ANY),
            scratch_shapes=[pltpu.VMEM((2, MOE_LOCAL_ROWS, D_MODEL), bf16), pltpu.SemaphoreType.DMA((2,))]),
        out_shape=jax.ShapeDtypeStruct(xs_init.shape, xs_init.dtype),
        input_output_aliases={3: 0},
        compiler_params=pltpu.CompilerParams(dimension_semantics=("arbitrary",), vmem_limit_bytes=VMEM_LIMIT),
        name="moe_dispatch",
    )(run_table, h2, rows, xs_init)


def _expert_kernel(te_ref, xs_hbm, wg_hbm, wu_hbm, wd_hbm, *rest, layer, reuse):
    ys_ref, wg_f32, wu_f32, wd_f32, wg_bf, wu_bf, wd_bf, x_ring, slot_ref, sem, x_sem = rest[-11:]
    n_ring, tm = x_ring.shape[0], x_ring.shape[1]
    j = pl.program_id(0)
    n_used = te_ref[1, 0]
    used = j < n_used

    def tile_copy(t):
        return pltpu.make_async_copy(xs_hbm.at[pl.ds(pl.multiple_of(t * tm, tm), tm)], x_ring.at[t % n_ring],
                                     x_sem.at[t % n_ring])

    @pl.when(j == 0)
    def _():
        for t in range(n_ring - 1):
            tile_copy(t).start()

    @pl.when(j + (n_ring - 1) < n_used)
    def _():
        tile_copy(j + (n_ring - 1)).start()

    expert = te_ref[0, j]
    new_expert = jnp.logical_or(j == 0, expert != te_ref[0, jnp.maximum(j - 1, 0)])

    def weight_copies(e, slot):
        return [pltpu.make_async_copy(hbm.at[layer, e], buf.at[slot], sem.at[slot])
                for hbm, buf in ((wg_hbm, wg_f32), (wu_hbm, wu_f32), (wd_hbm, wd_f32))]

    @pl.when(j == 0)
    def _():
        slot_ref[0] = 1
        for cp in weight_copies(expert, 0):
            cp.start()

    if not reuse:
        @pl.when(jnp.logical_not(used))
        def _():
            ys_ref[...] = jnp.zeros_like(ys_ref)

    @pl.when(used & new_expert)
    def _():
        slot = 1 - slot_ref[0]
        slot_ref[0] = slot
        for cp in weight_copies(expert, slot):
            cp.wait()
        wg_bf[...] = wg_f32[slot].astype(bf16)
        wu_bf[...] = wu_f32[slot].astype(bf16)
        wd_bf[...] = wd_f32[slot].astype(bf16)
        nxt = lax.while_loop(lambda t: (t < n_used) & (te_ref[0, jnp.minimum(t, LANE - 1)] == expert),
                             lambda t: t + 1, j + 1)

        @pl.when(nxt < n_used)
        def _():
            for cp in weight_copies(te_ref[0, nxt], 1 - slot):
                cp.start()

    @pl.when(used)
    def _():
        tile_copy(j).wait()
        x = x_ring[j % n_ring]
        hg = jnp.dot(x, wg_bf[...], preferred_element_type=f32)
        hu = jnp.dot(x, wu_bf[...], preferred_element_type=f32)
        hid = (hg * jax.nn.sigmoid(hg) * hu).astype(bf16)
        ys_ref[...] = jnp.dot(hid, wd_bf[...], preferred_element_type=f32)


def _experts(te, xs, w_gate, w_up, w_down, layer, ys_prev=None):
    tm = MOE_TILE
    row = lambda j, te: (jnp.minimum(j, te[1, 0] - 1), 0)
    hbm = pl.BlockSpec(memory_space=pl.ANY)
    reuse = ys_prev is not None
    return pl.pallas_call(
        functools.partial(_expert_kernel, layer=layer, reuse=reuse),
        grid_spec=pltpu.PrefetchScalarGridSpec(
            num_scalar_prefetch=1,
            grid=(MOE_ROWS // tm,),
            in_specs=[hbm, hbm, hbm, hbm] + ([hbm] if reuse else []),
            out_specs=pl.BlockSpec((tm, D_MODEL), row if reuse else (lambda j, te: (j, 0))),
            scratch_shapes=[pltpu.VMEM((2, D_MODEL, D_EXPERT), f32), pltpu.VMEM((2, D_MODEL, D_EXPERT), f32),
                            pltpu.VMEM((2, D_EXPERT, D_MODEL), f32),
                            pltpu.VMEM((D_MODEL, D_EXPERT), bf16), pltpu.VMEM((D_MODEL, D_EXPERT), bf16),
                            pltpu.VMEM((D_EXPERT, D_MODEL), bf16),
                            pltpu.VMEM((3, tm, D_MODEL), bf16),
                            pltpu.SMEM((1,), jnp.int32), pltpu.SemaphoreType.DMA((2,)),
                            pltpu.SemaphoreType.DMA((3,))]),
        out_shape=jax.ShapeDtypeStruct((MOE_ROWS, D_MODEL), f32),
        input_output_aliases={5: 0} if reuse else {},
        compiler_params=pltpu.CompilerParams(dimension_semantics=("arbitrary",), vmem_limit_bytes=VMEM_LIMIT),
        name="moe_experts",
    )(te, xs, w_gate, w_up, w_down, *([ys_prev] if reuse else []))


def _gather_expert_rows(pos0_ref, pos1_ref, ys_ref, buf, sem, tile_of):
    rows = buf.shape[2]
    i = pl.program_id(0)
    slot = i % 2

    def issue(tile, sl):
        base = tile * rows

        def body(t, carry):
            for s, pos_ref in enumerate((pos0_ref, pos1_ref)):
                pltpu.make_async_copy(ys_ref.at[pl.ds(pos_ref[base + t], 1)], buf.at[sl, s, pl.ds(t, 1)],
                                      sem.at[sl]).start()
            return carry

        lax.fori_loop(0, rows, body, 0, unroll=8)

    @pl.when(i == 0)
    def _():
        issue(tile_of(0), 0)

    @pl.when(i + 1 < pl.num_programs(0))
    def _():
        issue(tile_of(i + 1), 1 - slot)

    for s in range(2):
        pltpu.make_async_copy(ys_ref.at[pl.ds(0, rows)], buf.at[slot, s], sem.at[slot]).wait()
    return buf[slot, 0], buf[slot, 1]


def _moe_residual(pos0_ref, pos1_ref, ys_ref, x1_ref, rc_ref, mod_ref, buf, sem, tile_of=lambda step: step):
    y0, y1 = _gather_expert_rows(pos0_ref, pos1_ref, ys_ref, buf, sem, tile_of)
    rc = rc_ref[...]
    return x1_ref[...] + mod_ref[0][5:6] * (rc[:, 2:3] * y0 + rc[:, 3:4] * y1)


def _final_kernel(pos0_ref, pos1_ref, ys_ref, x1_ref, rc_ref, mod_ref, fn_ref, yc_ref, yl_ref, buf, sem):
    x2 = _moe_residual(pos0_ref, pos1_ref, ys_ref, x1_ref, rc_ref, mod_ref, buf, sem)
    y = x2 * lax.rsqrt(jnp.mean(x2 * x2, axis=-1, keepdims=True) + EPS) * fn_ref[...]
    is_ctx = pl.program_id(0) < N_CTX // x1_ref.shape[0]

    @pl.when(is_ctx)
    def _():
        yc_ref[...] = y

    @pl.when(jnp.logical_not(is_ctx))
    def _():
        yl_ref[...] = y


def _final_combine(pos0, pos1, ys, x1, rc, mods, fn):
    tc = TOK_TILE
    row = lambda i, p0, p1: (i, 0)
    n_ctx_tiles = N_CTX // tc
    return pl.pallas_call(
        _final_kernel,
        grid_spec=pltpu.PrefetchScalarGridSpec(
            num_scalar_prefetch=2,
            grid=(N_TOK // tc,),
            in_specs=[pl.BlockSpec(memory_space=pl.ANY),
                      pl.BlockSpec((tc, D_MODEL), row),
                      pl.BlockSpec((tc, LANE), row),
                      pl.BlockSpec((1, 6, D_MODEL), lambda i, p0, p1: (_mod_row(i, tc), 0, 0)),
                      pl.BlockSpec((1, D_MODEL), lambda i, p0, p1: (0, 0))],
            out_specs=[pl.BlockSpec((tc, D_MODEL), lambda i, p0, p1: (jnp.minimum(i, n_ctx_tiles - 1), 0)),
                       pl.BlockSpec((tc, D_MODEL), lambda i, p0, p1: (jnp.maximum(i - n_ctx_tiles, 0), 0))],
            scratch_shapes=[pltpu.VMEM((2, 2, tc, D_MODEL), f32), pltpu.SemaphoreType.DMA((2,))]),
        out_shape=[jax.ShapeDtypeStruct((N_CTX, D_MODEL), f32), jax.ShapeDtypeStruct((N_LAT, D_MODEL), f32)],
        compiler_params=pltpu.CompilerParams(dimension_semantics=("arbitrary",), vmem_limit_bytes=VMEM_LIMIT),
        name="moe_combine_final",
    )(pos0, pos1, ys, x1, rc, mods, fn)


def _moe_experts(h2, route_t, w_gate, w_up, w_down, layer, xs_buf, ys_prev):
    rows, te, runs = _rank(route_t)
    xs = _dispatch(runs[:, :3, :N_EXPERTS].reshape(-1), h2, rows, xs_buf)
    return rows[0], rows[1], _experts(te, xs, w_gate, w_up, w_down, layer, ys_prev), xs


def _pad_heads(w):
    lead = w.shape[:-1]
    w = w.reshape(*lead, ML_HEADS, ML_DIM)
    w = jnp.pad(w, [(0, 0)] * len(lead) + [(0, 0), (0, ML_PAD - ML_DIM)])
    return w.reshape(*lead, ML_PW)


def _pack_in_cols(wb):
    o = 0
    qa = wb[..., o:o + NA_WIDTH] * (NA_DIM ** -0.5)
    ka = wb[..., o + NA_WIDTH:o + 2 * NA_WIDTH]
    va = wb[..., o + 2 * NA_WIDTH:o + 3 * NA_WIDTH]
    o += 3 * NA_WIDTH
    qb, kb, vb, ob = [_pad_heads(wb[..., o + j * ML_WIDTH:o + (j + 1) * ML_WIDTH]) for j in range(4)]
    o += 4 * ML_WIDTH
    gates = wb[..., o:o + N_GATE_COLS]
    o += N_GATE_COLS
    pool = wb[..., o:o + POOL_WIDTH]
    main = jnp.concatenate([qa, ka, va, qb, vb, ob, pool], axis=-1)
    gates_p = jnp.pad(gates, [(0, 0)] * (gates.ndim - 1) + [(0, LANE - N_GATE_COLS)])
    return main, gates_p, jnp.concatenate([kb, gates], axis=-1)


def _pack_w_in(w, b):
    w_main, w_gates, w_feat = _pack_in_cols(w)
    b_main, b_gates, b_feat = _pack_in_cols(b.astype(f32))
    return (w_main.astype(bf16), b_main[:, None], w_gates.astype(bf16), b_gates[:, None],
            jnp.swapaxes(w_feat, 1, 2).astype(bf16), b_feat[:, :, None])


def _pack_w_out(w):
    n_l = w.shape[0]
    wb = w[:, NA_WIDTH:NA_WIDTH + ML_WIDTH].reshape(n_l, ML_HEADS, ML_DIM, D_MODEL)
    wb = jnp.pad(wb, ((0, 0), (0, 0), (0, ML_PAD - ML_DIM), (0, 0))).reshape(n_l, ML_PW, D_MODEL)
    return jnp.concatenate([w[:, :NA_WIDTH], wb, w[:, NA_WIDTH + ML_WIDTH:]], axis=1).astype(bf16)


def _block_diag(w):
    n_l, g, c, _ = w.shape
    eye = jnp.eye(g, dtype=w.dtype)
    return (eye[None, :, None, :, None] * w[:, :, :, None, :]).reshape(n_l, g * c, g * c)


def kernel(x_prompt, x_sample, cache_k_attn, cache_v_attn, state_mlstm_C, state_mlstm_n, state_mlstm_m, c, c_ctx,
           w_ada, b_ada, norm1, w_in, b_in, rpb, ml_norm, w_pool, pool_scale, w_out, norm2, w_router, b_router,
           w_gate, w_up, w_down, final_norm):
    dt = x_prompt.dtype
    x_ctx = x_prompt.reshape(N_CTX, D_MODEL).astype(f32)
    x_lat = x_sample.reshape(N_LAT, D_MODEL).astype(f32)
    x_lat_block0 = 0
    cvec = jnp.concatenate([c_ctx[None], c, jnp.zeros((8 - 1 - DEC_BATCH, D_MODEL), c.dtype)], axis=0).astype(f32)
    mods_all = _ada(cvec, w_ada.astype(f32), b_ada.astype(f32))
    mods_all = mods_all[:, :1 + DEC_BATCH].reshape(DEPTH, 1 + DEC_BATCH, 6, D_MODEL)

    wr_t = w_router.astype(f32).T
    br_t = b_router.astype(f32)[:, None]
    fn = final_norm.astype(f32)[None]

    na_bias = _na_bias_tables(rpb)
    xs_buf = jnp.zeros((MOE_ROWS, D_MODEL), bf16)
    in_params = (norm1.astype(f32)[:, None],) + _pack_w_in(w_in, b_in)
    out_params = (_block_diag(w_pool.astype(f32)).astype(bf16), pool_scale.astype(f32)[:, None],
                  _pad_heads(ml_norm.astype(f32))[:, None], _pack_w_out(w_out), norm2.astype(f32)[:, None])

    new_k = jnp.zeros((BATCH, DEPTH, SEQ, NA_WIDTH), f32)
    new_v = jnp.zeros_like(new_k)
    Cs, ns, ms = [], [], []
    pending = None
    for l in range(DEPTH):
        mods = mods_all[l]
        if pending is None:
            qkva, new_k, new_v, qvo, kt, gates, gates_t, pin = _in_proj(x_ctx, x_lat, mods, l, *in_params,
                                                                        new_k, new_v)
        else:
            x, qkva, new_k, new_v, qvo, kt, gates, gates_t, pin = _moe_in_proj(
                *pending, mods_all[l - 1], mods, l, *in_params, new_k, new_v)
            x_ctx, x_lat, x_lat_block0 = x, x, N_CTX // TOK_TILE

        oa_ctx = _ctx_attention(qkva.reshape(N_TOK // SEQ, SEQ, W_A))
        ck = (cache_k_attn[:, l].reshape(DEC_BATCH, PAST_LEN, NA_WIDTH)).astype(bf16)
        cv = (cache_v_attn[:, l].reshape(DEC_BATCH, PAST_LEN, NA_WIDTH)).astype(bf16)
        oa_lat = _neighborhood_attention(qkva.reshape(N_TOK // DEC_SEQ, DEC_SEQ, W_A), ck, cv, na_bias, l)

        c_l, m_l = _pack_ml_state(state_mlstm_C[:, l], state_mlstm_n[:, l], state_mlstm_m[:, l])
        hf, hb, c_fin, m_fin = _mlstm(qvo, kt, gates, gates_t, c_l, m_l)
        C_l, n_l, m_l2 = _unpack_ml_state(c_fin[:BATCH], m_fin[:BATCH])
        Cs.append(C_l)
        ns.append(n_l)
        ms.append(m_l2)

        x1, h2, route_t, rc = _out_proj(x_ctx, x_lat, x_lat_block0, mods, l,
                                        oa_ctx.reshape(N_CTX, NA_WIDTH), oa_lat.reshape(N_LAT, NA_WIDTH),
                                        hf.reshape(N_TOK, ML_PW), hb.reshape(N_TOK, ML_PW), qvo, pin,
                                        *out_params, wr_t, br_t)
        ys_prev = None if pending is None else pending[2]
        pos0, pos1, ys, xs_buf = _moe_experts(h2, route_t, w_gate, w_up, w_down, l, xs_buf, ys_prev)
        pending = (pos0, pos1, ys, x1, rc)

    x = _final_combine(*pending, mods_all[DEPTH - 1], fn)
    y_prompt = x[0].reshape(BATCH, SEQ, D_MODEL).astype(dt)
    y_sample = x[1].reshape(DEC_BATCH, DEC_SEQ, D_MODEL).astype(dt)
    new_k, new_v = (a.reshape(BATCH, DEPTH, SEQ, NA_HEADS, NA_DIM).astype(dt) for a in (new_k, new_v))
    return (y_prompt, y_sample, new_k, new_v,
            jnp.stack(Cs, axis=1).astype(dt), jnp.stack(ns, axis=1).astype(dt), jnp.stack(ms, axis=1).astype(dt))
```

```python
import functools

import numpy as np
import jax
import jax.numpy as jnp
from jax import lax
from jax.experimental import pallas as pl
from jax.experimental.pallas import tpu as pltpu

D_MODEL = 1024
BATCH = 16
SEQ = 256
DEPTH = 4
DEC_BATCH = 2
DEC_SEQ = 4096
PAST_LEN = 256
GRID_W = 64
EPS = 1e-6
NEG_INF = -1e30
NA_HEADS = 6
NA_DIM = 64
NA_WIDTH = NA_HEADS * NA_DIM
NA_ROWS = 8
NA_COLS = 16
RPB_ROWS = 2 * NA_ROWS - 1
RPB_COLS = 2 * NA_COLS - 1
ML_HEADS = 4
ML_DIM = 96
ML_WIDTH = ML_HEADS * ML_DIM
POOL_WINDOWS = (2, 4, 8, 16)
POOL_GROUPS = 4
POOL_DIM = 64
POOL_WIDTH = POOL_GROUPS * POOL_DIM
N_GATE_COLS = 4 * ML_HEADS
N_EXPERTS = 16
N_EXPERT_GROUPS = 4
EXPERTS_PER_GROUP = N_EXPERTS // N_EXPERT_GROUPS
D_EXPERT = 512
ADA_DIM = 6 * D_MODEL

N_CTX = BATCH * SEQ
N_LAT = DEC_BATCH * DEC_SEQ
N_TOK = N_CTX + N_LAT
LANE = 128
ML_PAD = LANE
ML_PW = ML_HEADS * ML_PAD
CAUG = ML_PAD
NA_PAIRS = NA_HEADS // 2
TOK_TILE = 512
ML_CHUNK = 256
NA_QROWS = 4
NA_KROWS = NA_QROWS + NA_ROWS - 1
POOL_HALO = max(POOL_WINDOWS) // 2
POOL_BLOCK = 128
MOE_TILE = 512
MOE_CHUNK = 16
MOE_LOCAL_ROWS = -(-(2 * TOK_TILE + N_EXPERTS * (MOE_CHUNK - 1)) // LANE) * LANE
MOE_ROWS = -(-(2 * N_TOK + (N_TOK // TOK_TILE) * N_EXPERTS * (MOE_CHUNK - 1) + N_EXPERTS * (MOE_TILE - 1))
             // MOE_TILE) * MOE_TILE
VMEM_LIMIT = 56 * 1024 * 1024

W_A = 3 * NA_WIDTH
W_B = 3 * ML_PW
N_TCOLS = ML_PW + N_GATE_COLS
W_MAIN = W_A + W_B + POOL_WIDTH

f32 = jnp.float32
bf16 = jnp.bfloat16
HI = lax.Precision.HIGHEST


def _nt(a, b, **kw):
    return lax.dot_general(a, b, (((1,), (1,)), ((), ())), preferred_element_type=f32, **kw)


def _mod_row(i, tile):
    n_ctx_tiles = N_CTX // tile
    per_batch = DEC_SEQ // tile
    return jnp.where(i < n_ctx_tiles, 0, 1 + (i - n_ctx_tiles) // per_batch)


def _ada_kernel(c_ref, w_ref, b_ref, o_ref):
    s = c_ref[...]
    s = s * jax.nn.sigmoid(s)
    o_ref[0] = jnp.dot(s.astype(bf16), w_ref[0].astype(bf16), preferred_element_type=f32) + b_ref[0]


def _ada(cvec, w_ada, b_ada):
    nj = ADA_DIM // D_MODEL
    return pl.pallas_call(
        _ada_kernel,
        grid=(DEPTH, nj),
        in_specs=[pl.BlockSpec((8, D_MODEL), lambda l, j: (0, 0)),
                  pl.BlockSpec((1, D_MODEL, D_MODEL), lambda l, j: (l, 0, j)),
                  pl.BlockSpec((1, 1, D_MODEL), lambda l, j: (l, 0, j))],
        out_specs=pl.BlockSpec((1, 8, D_MODEL), lambda l, j: (l, 0, j)),
        out_shape=jax.ShapeDtypeStruct((DEPTH, 8, ADA_DIM), f32),
        name="ada_mod",
    )(cvec, w_ada, b_ada.reshape(DEPTH, 1, ADA_DIM))


def _in_tile(step):
    return (step + N_CTX // TOK_TILE) % (N_TOK // TOK_TILE)


def _in_kernel(xc_ref, xl_ref, *refs):
    is_ctx = _in_tile(pl.program_id(0)) < N_CTX // xc_ref.shape[0]
    _in_body(jnp.where(is_ctx, xc_ref[...], xl_ref[...]), *refs)


def _moe_in_kernel(pos0_ref, pos1_ref, ys_ref, x1_ref, rc_ref, mod_prev_ref, *refs):
    in_refs, x_out_ref, out_refs, (buf, sem) = refs[:10], refs[10], refs[11:-2], refs[-2:]
    x = _moe_residual(pos0_ref, pos1_ref, ys_ref, x1_ref, rc_ref, mod_prev_ref, buf, sem, tile_of=_in_tile)
    x_out_ref[...] = x
    _in_body(x, *in_refs, *out_refs)


def _in_body(x, mod_ref, n1_ref, w_ref, b_ref, wg_ref, bg_ref, wt_ref, bt_ref, k_in_ref, v_in_ref,
             a_ref, k_ref, v_ref, b_out_ref, kt_ref, g_ref, gt_ref, pin_ref):
    del k_in_ref, v_in_ref
    mod = mod_ref[0]
    h = x * lax.rsqrt(jnp.mean(x * x, axis=-1, keepdims=True) + EPS) * n1_ref[...]
    h = (h * (1.0 + mod[1:2]) + mod[0:1]).astype(bf16)
    pa = jnp.dot(h, w_ref[:, 0:W_A], preferred_element_type=f32) + b_ref[:, 0:W_A]
    a_ref[...] = pa.astype(bf16)
    k_ref[...] = pa[:, NA_WIDTH:2 * NA_WIDTH].reshape(k_ref.shape)
    v_ref[...] = pa[:, 2 * NA_WIDTH:W_A].reshape(v_ref.shape)
    for j in range(3):
        lo = W_A + j * ML_PW
        pb = jnp.dot(h, w_ref[:, lo:lo + ML_PW], preferred_element_type=f32) + b_ref[:, lo:lo + ML_PW]
        if j == 0:
            pb = pb * (ML_DIM ** -0.5)
        b_out_ref[:, j * ML_PW:(j + 1) * ML_PW] = pb.astype(bf16)
    lo = W_A + W_B
    pin_ref[...] = jnp.dot(h, w_ref[:, lo:lo + POOL_WIDTH], preferred_element_type=f32) + b_ref[:, lo:lo + POOL_WIDTH]
    g_ref[...] = jnp.dot(h, wg_ref[...], preferred_element_type=f32) + bg_ref[...]
    t = _nt(wt_ref[...], h) + bt_ref[...]
    kt_ref[...] = t[0:ML_PW].astype(bf16)
    gt_ref[...] = t[ML_PW:N_TCOLS]


def _in_proj_specs(layer):
    tm = TOK_TILE
    lyr = lambda shape: pl.BlockSpec((None,) + shape, lambda i, *_: (layer, 0, 0))
    rows = lambda width: pl.BlockSpec((tm, width), lambda i, *_: (_in_tile(i), 0))
    cols = lambda height: pl.BlockSpec((height, tm), lambda i, *_: (0, _in_tile(i)))
    param_specs = [pl.BlockSpec((1, 6, D_MODEL), lambda i, *_: (_mod_row(_in_tile(i), tm), 0, 0)),
                   lyr((1, D_MODEL)), lyr((D_MODEL, W_MAIN)), lyr((1, W_MAIN)), lyr((D_MODEL, LANE)), lyr((1, LANE)),
                   lyr((N_TCOLS, D_MODEL)), lyr((N_TCOLS, 1)),
                   pl.BlockSpec(memory_space=pl.ANY), pl.BlockSpec(memory_space=pl.ANY)]
    n_ctx_tiles = N_CTX // tm
    kv_spec = pl.BlockSpec((tm // SEQ, None, SEQ, NA_WIDTH),
                           lambda i, *_: (jnp.where(_in_tile(i) < n_ctx_tiles, _in_tile(i), 0), layer, 0, 0))
    kv_shape = jax.ShapeDtypeStruct((BATCH, DEPTH, SEQ, NA_WIDTH), f32)
    out_specs = [rows(W_A), kv_spec, kv_spec, rows(W_B), cols(ML_PW), rows(LANE), cols(N_GATE_COLS),
                 rows(POOL_WIDTH)]
    out_shape = [jax.ShapeDtypeStruct((N_TOK, W_A), bf16), kv_shape, kv_shape,
                 jax.ShapeDtypeStruct((N_TOK, W_B), bf16),
                 jax.ShapeDtypeStruct((ML_PW, N_TOK), bf16),
                 jax.ShapeDtypeStruct((N_TOK, LANE), f32),
                 jax.ShapeDtypeStruct((N_GATE_COLS, N_TOK), f32),
                 jax.ShapeDtypeStruct((N_TOK, POOL_WIDTH), f32)]
    return rows, param_specs, out_specs, out_shape


def _in_proj(x_ctx, x_lat, mods, layer, *params):
    tm = TOK_TILE
    n_ctx_tiles = N_CTX // tm
    rows, param_specs, out_specs, out_shape = _in_proj_specs(layer)
    return pl.pallas_call(
        _in_kernel,
        grid=(N_TOK // tm,),
        in_specs=[pl.BlockSpec((tm, D_MODEL), lambda i: (jnp.where(_in_tile(i) < n_ctx_tiles, _in_tile(i), 0), 0)),
                  pl.BlockSpec((tm, D_MODEL), lambda i: (jnp.maximum(_in_tile(i) - n_ctx_tiles, 0), 0))] + param_specs,
        out_specs=out_specs,
        out_shape=out_shape,
        input_output_aliases={2 + len(param_specs) - 2: 1, 2 + len(param_specs) - 1: 2},
        compiler_params=pltpu.CompilerParams(dimension_semantics=("arbitrary",), vmem_limit_bytes=VMEM_LIMIT),
        name="in_proj",
    )(x_ctx, x_lat, mods, *params)


def _moe_in_proj(pos0, pos1, ys, x1, rc, mods_prev, mods, layer, *params):
    tm = TOK_TILE
    rows, param_specs, out_specs, out_shape = _in_proj_specs(layer)
    return pl.pallas_call(
        _moe_in_kernel,
        grid_spec=pltpu.PrefetchScalarGridSpec(
            num_scalar_prefetch=2,
            grid=(N_TOK // tm,),
            in_specs=[pl.BlockSpec(memory_space=pl.ANY), rows(D_MODEL), rows(LANE),
                      pl.BlockSpec((1, 6, D_MODEL), lambda i, *_: (_mod_row(_in_tile(i), tm), 0, 0))] + param_specs,
            out_specs=[rows(D_MODEL)] + out_specs,
            scratch_shapes=[pltpu.VMEM((2, 2, tm, D_MODEL), f32), pltpu.SemaphoreType.DMA((2,))]),
        out_shape=[jax.ShapeDtypeStruct((N_TOK, D_MODEL), f32)] + out_shape,
        input_output_aliases={6 + len(param_specs) - 2: 2, 6 + len(param_specs) - 1: 3},
        compiler_params=pltpu.CompilerParams(dimension_semantics=("arbitrary",), vmem_limit_bytes=VMEM_LIMIT),
        name="moe_combine_in_proj",
    )(pos0, pos1, ys, x1, rc, mods_prev, mods, *params)


def _pair_attention(qp, parts):
    lane = lax.broadcasted_iota(jnp.int32, (1, LANE), 1)
    outs = []
    for j in range(2):
        in_half = (lane >= j * NA_DIM) & (lane < (j + 1) * NA_DIM)
        qm = jnp.where(in_half, qp, jnp.zeros_like(qp))
        scores = []
        for k, _, bias in parts:
            s = _nt(qm, k)
            if bias is not None:
                s = s + bias[j]
            scores.append(s)
        m = scores[0].max(axis=-1, keepdims=True)
        for s in scores[1:]:
            m = jnp.maximum(m, s.max(axis=-1, keepdims=True))
        den = None
        acc = None
        for s, (_, v, _) in zip(scores, parts):
            p = jnp.exp(s - m)
            ps = p.sum(axis=-1, keepdims=True)
            den = ps if den is None else den + ps
            o = jnp.dot(p.astype(bf16), v, preferred_element_type=f32)
            acc = o if acc is None else acc + o
        outs.append(acc / den)
    return jnp.where(lane < NA_DIM, outs[0], outs[1])


def _ctx_attn_kernel(q_ref, k_ref, v_ref, o_ref):
    for p in range(NA_PAIRS):
        sl = slice(p * LANE, (p + 1) * LANE)
        o = _pair_attention(q_ref[0, :, sl], [(k_ref[0, :, sl], v_ref[0, :, sl], None)])
        o_ref[0, :, sl] = o.astype(bf16)


def _ctx_attention(qkv):
    blk = lambda c: pl.BlockSpec((1, SEQ, NA_WIDTH), lambda b, c=c: (b, 0, c))
    return pl.pallas_call(
        _ctx_attn_kernel,
        grid=(BATCH,),
        in_specs=[blk(0), blk(1), blk(2)],
        out_specs=pl.BlockSpec((1, SEQ, NA_WIDTH), lambda b: (b, 0, 0)),
        out_shape=jax.ShapeDtypeStruct((BATCH, SEQ, NA_WIDTH), bf16),
        name="ctx_attention",
    )(qkv, qkv, qkv)


def _na_window_start(rb):
    return jnp.clip(rb * NA_QROWS - NA_ROWS // 2, 0, DEC_SEQ // GRID_W - NA_KROWS)


def _na_bias(tab_ref, head, rb):
    rows = DEC_SEQ // GRID_W
    ws = _na_window_start(rb)
    lane = lax.broadcasted_iota(jnp.int32, (1, NA_KROWS * GRID_W), 1)
    per_qrow = []
    for dq in range(NA_QROWS):
        qr = rb * NA_QROWS + dq
        a0 = ws - qr + (NA_ROWS - 1) + NA_KROWS
        tiles = [tab_ref[head, a0 + 2 * j] for j in range((NA_KROWS + 1) // 2)]
        t = jnp.concatenate(tiles, axis=1)[:, :NA_KROWS * GRID_W]
        lo = (jnp.clip(qr - NA_ROWS // 2, 0, rows - NA_ROWS) - ws) * GRID_W
        ok = (lane >= lo) & (lane < lo + NA_ROWS * GRID_W)
        per_qrow.append(jnp.where(ok, t, NEG_INF))
    return jnp.concatenate(per_qrow, axis=0)


def _na_kernel(q_ref, k_ref, v_ref, ck_ref, cv_ref, tab_ref, o_ref):
    rb = pl.program_id(1)
    start = pl.multiple_of(_na_window_start(rb) * GRID_W, GRID_W)
    nk = NA_KROWS * GRID_W
    for p in range(NA_PAIRS):
        sl = slice(p * LANE, (p + 1) * LANE)
        bias = [_na_bias(tab_ref.at[0], 2 * p + j, rb) for j in range(2)]
        parts = [(k_ref[0, pl.ds(start, nk), sl], v_ref[0, pl.ds(start, nk), sl], bias),
                 (ck_ref[0, :, sl], cv_ref[0, :, sl], None)]
        o = _pair_attention(q_ref[0, :, sl], parts)
        o_ref[0, :, sl] = o.astype(bf16)


def _neighborhood_attention(qkv, ck, cv, tables, layer):
    nq = NA_QROWS * GRID_W
    n_rb = DEC_SEQ // nq
    return pl.pallas_call(
        _na_kernel,
        grid=(DEC_BATCH, n_rb),
        in_specs=[pl.BlockSpec((1, nq, NA_WIDTH), lambda b, r: (1 + b, r, 0)),
                  pl.BlockSpec((1, DEC_SEQ, NA_WIDTH), lambda b, r: (1 + b, 0, 1)),
                  pl.BlockSpec((1, DEC_SEQ, NA_WIDTH), lambda b, r: (1 + b, 0, 2)),
                  pl.BlockSpec((1, PAST_LEN, NA_WIDTH), lambda b, r: (b, 0, 0)),
                  pl.BlockSpec((1, PAST_LEN, NA_WIDTH), lambda b, r: (b, 0, 0)),
                  pl.BlockSpec((1,) + tables.shape[1:], lambda b, r: (layer, 0, 0, 0, 0))],
        out_specs=pl.BlockSpec((1, nq, NA_WIDTH), lambda b, r: (b, r, 0)),
        out_shape=jax.ShapeDtypeStruct((DEC_BATCH, DEC_SEQ, NA_WIDTH), bf16),
        compiler_params=pltpu.CompilerParams(dimension_semantics=("arbitrary", "arbitrary"),
                                             vmem_limit_bytes=VMEM_LIMIT),
        name="neighborhood_attention",
    )(qkv, qkv, qkv, ck, cv, tables)


def _na_bias_tables(rpb):
    qc = np.arange(GRID_W)[:, None]
    kc = np.arange(GRID_W)[None, :]
    dc = np.clip(kc - qc + NA_COLS - 1, 0, RPB_COLS - 1)
    col_start = np.clip(qc - NA_COLS // 2, 0, GRID_W - NA_COLS)
    col_ok = (kc >= col_start) & (kc < col_start + NA_COLS)
    pick_col = (dc[None] == np.arange(RPB_COLS)[:, None, None]).astype(np.float32)
    rpb_pad = jnp.pad(rpb.astype(f32), ((0, 0), (0, 0), (NA_KROWS, NA_KROWS + 1), (0, 0)))
    n_a = rpb_pad.shape[2] - 1
    rows2 = jnp.stack([rpb_pad[:, :, :-1], rpb_pad[:, :, 1:]], axis=3)
    pick2 = np.zeros((2, RPB_COLS, GRID_W, 2 * GRID_W), np.float32)
    for j in range(2):
        pick2[j, :, :, j * GRID_W:(j + 1) * GRID_W] = pick_col
    tiles = jnp.einsum('lhajb,jbqc->lhaqc', rows2, pick2, precision=HI)
    a_pad = np.arange(n_a)[:, None] + np.arange(2)[None, :]
    row_ok = (a_pad >= NA_KROWS) & (a_pad < NA_KROWS + RPB_ROWS)
    ok = (row_ok[:, None, :, None] & col_ok[None, :, None, :]).reshape(n_a, GRID_W, 2 * GRID_W)
    return jnp.where(ok[None, None], tiles, NEG_INF)


def _log_sigmoid(x):
    return -(jnp.maximum(-x, 0.0) + jnp.log(1.0 + jnp.exp(-jnp.abs(x))))


def _split3(x):
    hi = x.astype(bf16)
    r1 = x - hi.astype(f32)
    mid = r1.astype(bf16)
    lo = (r1 - mid.astype(f32)).astype(bf16)
    return hi, mid, lo


def _mlstm_kernel(qf_ref, vf_ref, ktf_ref, gf_ref, gtf_ref, qb_ref, vb_ref, ktb_ref, gb_ref, gtb_ref,
                  c0_ref, m0_ref, hf_ref, hb_ref, c_out_ref, m_out_ref, c_scr, m_scr):
    L = ML_CHUNK
    seq, c, n_chunks, _ = _ml_schedule(pl.program_id(0))

    @pl.when(c == 0)
    def _():
        is_ctx = seq < BATCH
        c_scr[...] = jnp.where(is_ctx, 0.0, c0_ref[0])
        m_scr[...] = jnp.where(is_ctx, 0.0, m0_ref[0])

    ri = lax.broadcasted_iota(jnp.int32, (L, L), 0)
    ci = lax.broadcasted_iota(jnp.int32, (L, L), 1)
    lane = lax.broadcasted_iota(jnp.int32, (L, ML_PAD), 1)
    is_ncol = lane == ML_DIM
    lower = ri >= ci
    upper = ri <= ci
    lower_b = jnp.where(lower, 1.0, 0.0).astype(bf16)
    upper_b = jnp.where(upper, 1.0, 0.0).astype(bf16)
    dirs = ((qf_ref, ktf_ref, vf_ref, gf_ref, gtf_ref, hf_ref), (qb_ref, ktb_ref, vb_ref, gb_ref, gtb_ref, hb_ref))
    for d, (q_ref, kt_ref, v_ref, g_ref, gt_ref, h_ref) in enumerate(dirs):
        g = g_ref[...][:, 0:N_GATE_COLS]
        gt = gt_ref[...]
        lf_c = _log_sigmoid(g)
        lf_r = _log_sigmoid(gt)
        b_cols = sum(jnp.dot(lower_b, part, preferred_element_type=f32) for part in _split3(lf_c))
        b_rows = sum(jnp.dot(part, upper_b, preferred_element_type=f32) for part in _split3(lf_r))
        tot_c = jnp.sum(lf_c, axis=0, keepdims=True)
        tot_r = jnp.sum(lf_r, axis=1, keepdims=True)
        visible = lower
        if d == 1:
            b_cols = tot_c - b_cols + lf_c
            b_rows = tot_r - b_rows + lf_r
            visible = upper
        for hd in range(ML_HEADS):
            st = d * ML_HEADS + hd
            ci_ = 2 * ML_HEADS * d + hd
            cf_ = ci_ + ML_HEADS
            sl = slice(hd * ML_PAD, (hd + 1) * ML_PAD)
            bc = b_cols[:, cf_:cf_ + 1]
            br = b_rows[cf_:cf_ + 1, :]
            li_r = gt[ci_:ci_ + 1, :]
            m_prev = m_scr[st:st + 1, 0:1]
            dmat = jnp.where(visible, bc - br + li_r, NEG_INF)
            inter = bc + m_prev
            m_t = jnp.maximum(inter, dmat.max(axis=-1, keepdims=True))
            w_intra = jnp.exp(dmat - m_t)
            w_inter = jnp.exp(inter - m_t)
            qh = q_ref[0, :, sl]
            kht = kt_ref[sl, :]
            v_aug = jnp.where(is_ncol, jnp.ones((), bf16), v_ref[0, :, sl])
            s = (jnp.dot(qh, kht, preferred_element_type=f32) * w_intra).astype(bf16)
            c_aug = c_scr[st]
            na = (w_inter * jnp.dot(qh, c_aug.astype(bf16), preferred_element_type=f32)
                  + jnp.dot(s, v_aug, preferred_element_type=f32))
            den = na[:, ML_DIM:ML_DIM + 1]
            h_ref[0, :, sl] = jnp.where(lane < ML_DIM, na / jnp.maximum(jnp.abs(den), jnp.exp(-m_t)), 0.0)
            b_end = tot_r[cf_:cf_ + 1, :]
            g_row = b_end - br + li_r
            m_new = jnp.maximum(b_end + m_prev, g_row.max(axis=1, keepdims=True))
            decay = jnp.exp(b_end + m_prev - m_new)
            kwt = (kht.astype(f32) * jnp.exp(g_row - m_new)).astype(bf16)
            c_scr[st] = decay * c_aug + jnp.dot(kwt, v_aug, preferred_element_type=f32)
            m_scr[st:st + 1, :] = jnp.broadcast_to(m_new, (1, LANE))

    @pl.when(c == n_chunks - 1)
    def _():
        c_out_ref[0] = c_scr[...]
        m_out_ref[0] = m_scr[...]


def _ml_schedule(s):
    nc_ctx, nc_lat = SEQ // ML_CHUNK, DEC_SEQ // ML_CHUNK
    n_ctx_steps = BATCH * nc_ctx
    is_ctx = s < n_ctx_steps
    t = s - n_ctx_steps
    seq = jnp.where(is_ctx, s // nc_ctx, BATCH + t // nc_lat)
    c = jnp.where(is_ctx, s % nc_ctx, t % nc_lat)
    nc = jnp.where(is_ctx, nc_ctx, nc_lat)
    base = jnp.where(is_ctx, (s // nc_ctx) * nc_ctx, n_ctx_steps + (t // nc_lat) * nc_lat)
    return seq, c, nc, base


def _mlstm(qvo, kt, gates, gates_t, c0, m0):
    L = ML_CHUNK
    n_seq = BATCH + DEC_BATCH

    def fwd(s):
        _, c, _, base = _ml_schedule(s)
        return base + c

    def bwd(s):
        _, c, nc, base = _ml_schedule(s)
        return base + nc - 1 - c

    seq_of = lambda s: _ml_schedule(s)[0]
    lat_of = lambda s: jnp.maximum(seq_of(s) - BATCH, 0)

    def specs(pos):
        return [pl.BlockSpec((1, L, ML_PW), lambda s, j=j: (pos(s), 0, j)) for j in range(2)] + [
            pl.BlockSpec((ML_PW, L), lambda s: (0, pos(s))),
            pl.BlockSpec((L, LANE), lambda s: (pos(s), 0)),
            pl.BlockSpec((N_GATE_COLS, L), lambda s: (0, pos(s)))]

    q3 = qvo.reshape(N_TOK // L, L, W_B)
    n_str = 2 * ML_HEADS
    return pl.pallas_call(
        _mlstm_kernel,
        grid=(N_TOK // L,),
        in_specs=specs(fwd) + specs(bwd) + [
            pl.BlockSpec((1, n_str, ML_PAD, CAUG), lambda s: (lat_of(s), 0, 0, 0)),
            pl.BlockSpec((1, n_str, LANE), lambda s: (lat_of(s), 0, 0))],
        out_specs=[pl.BlockSpec((1, L, ML_PW), lambda s: (fwd(s), 0, 0)),
                   pl.BlockSpec((1, L, ML_PW), lambda s: (bwd(s), 0, 0)),
                   pl.BlockSpec((1, n_str, ML_PAD, CAUG), lambda s: (seq_of(s), 0, 0, 0)),
                   pl.BlockSpec((1, n_str, LANE), lambda s: (seq_of(s), 0, 0))],
        out_shape=[jax.ShapeDtypeStruct((N_TOK // L, L, ML_PW), f32),
                   jax.ShapeDtypeStruct((N_TOK // L, L, ML_PW), f32),
                   jax.ShapeDtypeStruct((n_seq, n_str, ML_PAD, CAUG), f32),
                   jax.ShapeDtypeStruct((n_seq, n_str, LANE), f32)],
        scratch_shapes=[pltpu.VMEM((n_str, ML_PAD, CAUG), f32), pltpu.VMEM((n_str, LANE), f32)],
        compiler_params=pltpu.CompilerParams(dimension_semantics=("arbitrary",), vmem_limit_bytes=VMEM_LIMIT),
        name="mlstm",
    )(q3, q3, kt, gates, gates_t, q3, q3, kt, gates, gates_t, c0, m0)


def _pack_ml_state(C, n, m):
    B = C.shape[0]
    c_aug = jnp.zeros((B, 2, ML_HEADS, ML_PAD, CAUG), f32)
    c_aug = c_aug.at[:, :, :, :ML_DIM, :ML_DIM].set(C.astype(f32))
    c_aug = c_aug.at[:, :, :, :ML_DIM, ML_DIM].set(n.astype(f32))
    m_b = jnp.broadcast_to(m.astype(f32)[..., None], (B, 2, ML_HEADS, LANE))
    return c_aug.reshape(B, 2 * ML_HEADS, ML_PAD, CAUG), m_b.reshape(B, 2 * ML_HEADS, LANE)


def _unpack_ml_state(c_aug, m_b):
    B = c_aug.shape[0]
    c_aug = c_aug.reshape(B, 2, ML_HEADS, ML_PAD, CAUG)
    return (c_aug[:, :, :, :ML_DIM, :ML_DIM], c_aug[:, :, :, :ML_DIM, ML_DIM],
            m_b.reshape(B, 2, ML_HEADS, LANE)[..., 0])


def _pool_rows(u_prev, u_cur, u_next, w_bd, scale, t0, seq_len):
    tm = u_cur.shape[0]
    u_win = jnp.concatenate([u_prev, u_cur, u_next], axis=0)
    u_hi = u_win.astype(bf16)
    u_lo = (u_win - u_hi.astype(f32)).astype(bf16)
    lane = lax.broadcasted_iota(jnp.int32, (1, LANE), 1)
    blocks = []
    for r0 in range(0, tm, POOL_BLOCK):
        win = slice(r0, r0 + POOL_BLOCK + 2 * POOL_HALO)
        t_abs = t0 + r0 + lax.broadcasted_iota(jnp.int32, (POOL_BLOCK, 1), 0)
        s_abs = t0 + r0 - POOL_HALO + lax.broadcasted_iota(jnp.int32, (1, POOL_BLOCK + 2 * POOL_HALO), 1)
        t_loc = t_abs & (seq_len - 1)
        seq_start = t_abs - t_loc
        means = []
        for w in POOL_WINDOWS:
            lo = jnp.maximum(t_loc - w // 2, 0)
            hi = jnp.minimum(t_loc - w // 2 + w, seq_len)
            in_win = (s_abs >= seq_start + lo) & (s_abs < seq_start + hi)
            means.append((jnp.where(in_win, 1.0, 0.0).astype(bf16), 1.0 / (hi - lo).astype(f32)))
        pooled = []
        for p in range(POOL_GROUPS // 2):
            sl = slice(p * LANE, (p + 1) * LANE)
            halves = []
            for a, inv_cnt in means[2 * p:2 * p + 2]:
                tot = (jnp.dot(a, u_hi[win, sl], preferred_element_type=f32)
                       + jnp.dot(a, u_lo[win, sl], preferred_element_type=f32))
                halves.append(tot * inv_cnt)
            pooled.append(jnp.where(lane < POOL_DIM, halves[0], halves[1]) - u_cur[r0:r0 + POOL_BLOCK, sl])
        blocks.append(jnp.concatenate(pooled, axis=1))
    pooled = jnp.concatenate(blocks, axis=0).astype(bf16)
    return jnp.dot(pooled, w_bd, preferred_element_type=f32) * scale


def _top2_sum(a, b, c, d):
    hi1, lo1 = jnp.maximum(a, b), jnp.minimum(a, b)
    hi2, lo2 = jnp.maximum(c, d), jnp.minimum(c, d)
    return jnp.maximum(hi1, hi2) + jnp.maximum(jnp.minimum(hi1, hi2), jnp.maximum(lo1, lo2))


def _first_match(vals, target):
    idx = jnp.full_like(target, float(len(vals) - 1))
    for i in range(len(vals) - 2, -1, -1):
        idx = jnp.where(vals[i] == target, float(i), idx)
    return idx


def _pick(vals, idx):
    out = vals[-1]
    for i in range(len(vals) - 2, -1, -1):
        out = jnp.where(idx == float(i), vals[i], out)
    return out


def _route(logits_t, bias_t):
    scores = jax.nn.sigmoid(logits_t)
    sel = scores + bias_t
    row = lambda a, i: a[i:i + 1, :]
    grp = [_top2_sum(*[row(sel, EXPERTS_PER_GROUP * g + i) for i in range(EXPERTS_PER_GROUP)])
           for g in range(N_EXPERT_GROUPS)]
    best = functools.reduce(jnp.maximum, grp)
    gidx = _first_match(grp, best)
    sel_g = [_pick([row(sel, EXPERTS_PER_GROUP * g + i) for g in range(N_EXPERT_GROUPS)], gidx)
             for i in range(EXPERTS_PER_GROUP)]
    sco_g = [_pick([row(scores, EXPERTS_PER_GROUP * g + i) for g in range(N_EXPERT_GROUPS)], gidx)
             for i in range(EXPERTS_PER_GROUP)]
    i0 = _first_match(sel_g, functools.reduce(jnp.maximum, sel_g))
    rest = [jnp.where(i0 == float(i), -jnp.inf, sel_g[i]) for i in range(EXPERTS_PER_GROUP)]
    i1 = _first_match(rest, functools.reduce(jnp.maximum, rest))
    s0, s1 = _pick(sco_g, i0), _pick(sco_g, i1)
    tot = s0 + s1
    rid = lax.broadcasted_iota(jnp.int32, (LANE, logits_t.shape[1]), 0)
    rows = (EXPERTS_PER_GROUP * gidx + i0, EXPERTS_PER_GROUP * gidx + i1, s0 / tot, s1 / tot)
    out = jnp.zeros(rid.shape, f32)
    for i, r in enumerate(rows):
        out = jnp.where(rid == i, r, out)
    return out


def _out_kernel(xc_ref, xl_ref, mod_ref, oac_ref, oal_ref, hf_ref, hb_ref, ob_ref, up_ref, uc_ref, un_ref, wp_ref,
                psc_ref, mln_ref, wo_ref, n2_ref, wr_ref, br_ref, x1_ref, h2_ref, rt_ref, rc_ref):
    tm = xc_ref.shape[0]
    i = pl.program_id(0)
    is_ctx = i < N_CTX // tm
    mod = mod_ref[0]
    out_a = jnp.where(is_ctx, oac_ref[...], oal_ref[...])
    out_c = _pool_rows(up_ref[...], uc_ref[...], un_ref[...], wp_ref[...], psc_ref[...], i * tm,
                       jnp.where(is_ctx, SEQ, DEC_SEQ)).astype(bf16)
    hsum = hf_ref[...] + hb_ref[...]
    outs_b = []
    for hd in range(ML_HEADS):
        sl = slice(hd * ML_PAD, (hd + 1) * ML_PAD)
        hh = hsum[:, sl]
        ms = jnp.sum(hh * hh, axis=-1, keepdims=True) * (1.0 / ML_DIM)
        hn = hh * lax.rsqrt(ms + EPS) * mln_ref[:, sl]
        outs_b.append((jax.nn.sigmoid(ob_ref[:, sl].astype(f32)) * hn).astype(bf16))
    out_b = jnp.concatenate(outs_b, axis=1)
    mixed = (jnp.dot(out_a, wo_ref[0:NA_WIDTH, :], preferred_element_type=f32)
             + jnp.dot(out_b, wo_ref[NA_WIDTH:NA_WIDTH + ML_PW, :], preferred_element_type=f32)
             + jnp.dot(out_c, wo_ref[NA_WIDTH + ML_PW:, :], preferred_element_type=f32))
    x1 = jnp.where(is_ctx, xc_ref[...], xl_ref[...]) + mod[2:3] * mixed
    x1_ref[...] = x1
    h2 = x1 * lax.rsqrt(jnp.mean(x1 * x1, axis=-1, keepdims=True) + EPS) * n2_ref[...]
    h2 = h2 * (1.0 + mod[4:5]) + mod[3:4]
    h2_ref[...] = h2.astype(bf16)
    h_hi = h2.astype(bf16)
    h_lo = (h2 - h_hi.astype(f32)).astype(bf16)
    w_hi = wr_ref[...].astype(bf16)
    w_lo = (wr_ref[...] - w_hi.astype(f32)).astype(bf16)
    route_t = _route(_nt(w_hi, h_hi) + (_nt(w_hi, h_lo) + _nt(w_lo, h_hi)), br_ref[...])
    rt_ref[...] = route_t[0:8]
    rc_ref[...] = route_t.T


def _out_proj(x_ctx, x_lat, x_lat_block0, mods, layer, oa_ctx, oa_lat, hf, hb, qvo, pin, w_bd, psc, mln, wo, n2,
              wr_t, br_t):
    tm = TOK_TILE
    const = lambda i: (0, 0)
    lyr = lambda shape: pl.BlockSpec((None,) + shape, lambda i: (layer, 0, 0))
    row = lambda i: (i, 0)
    n_ctx_tiles = N_CTX // tm
    halo_blocks = tm // POOL_HALO
    return pl.pallas_call(
        _out_kernel,
        grid=(N_TOK // tm,),
        in_specs=[pl.BlockSpec((tm, D_MODEL), lambda i: (jnp.minimum(i, n_ctx_tiles - 1), 0)),
                  pl.BlockSpec((tm, D_MODEL), lambda i: (jnp.maximum(i - n_ctx_tiles, 0) + x_lat_block0, 0)),
                  pl.BlockSpec((1, 6, D_MODEL), lambda i: (_mod_row(i, tm), 0, 0)),
                  pl.BlockSpec((tm, NA_WIDTH), lambda i: (jnp.minimum(i, n_ctx_tiles - 1), 0)),
                  pl.BlockSpec((tm, NA_WIDTH), lambda i: (jnp.maximum(i - n_ctx_tiles, 0), 0)),
                  pl.BlockSpec((tm, ML_PW), row),
                  pl.BlockSpec((tm, ML_PW), row),
                  pl.BlockSpec((tm, ML_PW), lambda i: (i, 2)),
                  pl.BlockSpec((POOL_HALO, POOL_WIDTH), lambda i: (jnp.maximum(i * halo_blocks - 1, 0), 0)),
                  pl.BlockSpec((tm, POOL_WIDTH), row),
                  pl.BlockSpec((POOL_HALO, POOL_WIDTH),
                               lambda i: (jnp.minimum((i + 1) * halo_blocks, N_TOK // POOL_HALO - 1), 0)),
                  lyr((POOL_WIDTH, POOL_WIDTH)), lyr((1, POOL_WIDTH)), lyr((1, ML_PW)),
                  lyr((NA_WIDTH + ML_PW + POOL_WIDTH, D_MODEL)), lyr((1, D_MODEL)),
                  pl.BlockSpec((N_EXPERTS, D_MODEL), const),
                  pl.BlockSpec((N_EXPERTS, 1), const)],
        out_specs=[pl.BlockSpec((tm, D_MODEL), row),
                   pl.BlockSpec((tm, D_MODEL), row),
                   pl.BlockSpec((8, tm), lambda i: (0, i)),
                   pl.BlockSpec((tm, LANE), row)],
        out_shape=[jax.ShapeDtypeStruct((N_TOK, D_MODEL), f32),
                   jax.ShapeDtypeStruct((N_TOK, D_MODEL), bf16),
                   jax.ShapeDtypeStruct((8, N_TOK), f32),
                   jax.ShapeDtypeStruct((N_TOK, LANE), f32)],
        compiler_params=pltpu.CompilerParams(dimension_semantics=("arbitrary",), vmem_limit_bytes=VMEM_LIMIT),
        name="out_proj_router",
    )(x_ctx, x_lat, mods, oa_ctx, oa_lat, hf, hb, qvo, pin, pin, pin, w_bd, psc, mln, wo, n2, wr_t, br_t)


def _ceil_to(x, m):
    return jnp.floor((x + (m - 1)) * (1.0 / m)) * m


def _prefix_over_experts(v):
    er = lax.broadcasted_iota(jnp.int32, (N_EXPERTS, N_EXPERTS), 0)
    ec = lax.broadcasted_iota(jnp.int32, (N_EXPERTS, N_EXPERTS), 1)
    return jnp.dot(jnp.where(ec < er, 1.0, 0.0), v, preferred_element_type=f32, precision=HI)


def _experts_to_lanes(v):
    sub = lax.broadcasted_iota(jnp.int32, (N_EXPERTS, LANE), 0)
    lane = lax.broadcasted_iota(jnp.int32, (N_EXPERTS, LANE), 1)
    return jnp.sum(jnp.where(sub == lane, v, 0.0), axis=0, keepdims=True)


def _expert_hits(rt):
    rid = lax.broadcasted_iota(jnp.int32, (N_EXPERTS, rt.shape[1]), 0).astype(f32)
    oh0 = rid == rt[0:1, :]
    oh1 = rid == rt[1:2, :]
    both = jnp.where(oh0 | oh1, 1.0, 0.0)
    runs = jnp.broadcast_to(_ceil_to(jnp.sum(both, axis=1, keepdims=True), MOE_CHUNK), (N_EXPERTS, LANE))
    return oh0, oh1, both, runs


def _rank_kernel(rt_all_ref, rt_ref, pos_ref, te_ref, tab_ref, carry_ref):
    tm = rt_ref.shape[1]
    step = pl.program_id(0)

    @pl.when(step == 0)
    def _():
        totals = jnp.zeros((N_EXPERTS, LANE), f32)
        for i in range(rt_all_ref.shape[1] // tm):
            totals = totals + _expert_hits(rt_all_ref[:, i * tm:(i + 1) * tm])[3]
        padded = _ceil_to(totals, MOE_TILE)
        off = _prefix_over_experts(padded)
        carry_ref[...] = off
        total = jnp.sum(padded, axis=0, keepdims=True)
        n_used = total * (1.0 / MOE_TILE)
        tile = lax.broadcasted_iota(jnp.int32, (1, LANE), 1).astype(f32)
        row0 = jnp.minimum(tile, n_used - 1.0) * MOE_TILE
        expert = jnp.sum(jnp.where(off <= row0, 1.0, 0.0), axis=0, keepdims=True) - 1.0
        sub = lax.broadcasted_iota(jnp.int32, (8, LANE), 0)
        te_ref[...] = jnp.where(sub == 0, expert, jnp.where(sub == 1, n_used, 0.0)).astype(jnp.int32)

    @pl.when(step > 0)
    def _():
        oh0, oh1, both, runs = _expert_hits(rt_ref[...])
        sr = lax.broadcasted_iota(jnp.int32, (tm, tm), 0)
        sc = lax.broadcasted_iota(jnp.int32, (tm, tm), 1)
        earlier = jnp.dot(both.astype(bf16), jnp.where(sr < sc, 1.0, 0.0).astype(bf16),
                          preferred_element_type=f32)
        g_off = carry_ref[...]
        l_off = _prefix_over_experts(runs)
        g_row = g_off[:, 0:1] + earlier
        l_row = l_off[:, 0:1] + earlier
        pick = lambda oh, v: jnp.sum(jnp.where(oh, v, 0.0), axis=0, keepdims=True)
        rows = (pick(oh0, g_row), pick(oh1, g_row), pick(oh0, l_row), pick(oh1, l_row))
        sub = lax.broadcasted_iota(jnp.int32, (8, tm), 0)
        out = jnp.zeros((8, tm), f32)
        for k, r in enumerate(rows):
            out = jnp.where(sub == k, r, out)
        pos_ref[...] = out.astype(jnp.int32)
        sub = lax.broadcasted_iota(jnp.int32, (8, LANE), 0)
        tab = jnp.zeros((8, LANE), f32)
        for k, v in enumerate((runs * (1.0 / MOE_CHUNK), l_off, g_off)):
            tab = jnp.where(sub == k, _experts_to_lanes(v), tab)
        tab_ref[0] = tab.astype(jnp.int32)
        carry_ref[...] = g_off + runs


def _rank(route_t):
    tm = TOK_TILE
    n_tiles = N_TOK // tm
    tile_of = lambda s: jnp.maximum(s - 1, 0)
    return pl.pallas_call(
        _rank_kernel,
        grid=(1 + n_tiles,),
        in_specs=[pl.BlockSpec((8, N_TOK), lambda s: (0, 0)),
                  pl.BlockSpec((8, tm), lambda s: (0, tile_of(s)))],
        out_specs=[pl.BlockSpec((8, tm), lambda s: (0, tile_of(s))),
                   pl.BlockSpec((8, LANE), lambda s: (0, 0)),
                   pl.BlockSpec((1, 8, LANE), lambda s: (tile_of(s), 0, 0))],
        out_shape=[jax.ShapeDtypeStruct((8, N_TOK), jnp.int32),
                   jax.ShapeDtypeStruct((8, LANE), jnp.int32),
                   jax.ShapeDtypeStruct((n_tiles, 8, LANE), jnp.int32)],
        scratch_shapes=[pltpu.VMEM((N_EXPERTS, LANE), f32)],
        compiler_params=pltpu.CompilerParams(dimension_semantics=("arbitrary",)),
        name="moe_rank",
    )(route_t, route_t)


def _dispatch_kernel(tab_ref, h_ref, rows_ref, xs_in_ref, xs_ref, loc, sem):
    del xs_in_ref
    tm = h_ref.shape[0]
    i = pl.program_id(0)
    slot = i % 2
    rid = lax.broadcasted_iota(jnp.int32, (MOE_LOCAL_ROWS, tm), 0)
    sel = (rid == rows_ref[2:3, :]) | (rid == rows_ref[3:4, :])
    loc[slot] = jnp.dot(jnp.where(sel, 1.0, 0.0).astype(bf16), h_ref[...], preferred_element_type=f32).astype(bf16)

    def chunk_copy(sl, src_row, dst_row):
        return pltpu.make_async_copy(loc.at[sl, pl.ds(pl.multiple_of(src_row, MOE_CHUNK), MOE_CHUNK)],
                                     xs_ref.at[pl.ds(pl.multiple_of(dst_row, MOE_CHUNK), MOE_CHUNK)], sem.at[sl])

    def chunks_of(tile):
        return sum(tab_ref[3 * tile * N_EXPERTS + e] for e in range(N_EXPERTS))

    def wait_chunks(sl, n):
        def wait_one(c, carry):
            chunk_copy(sl, 0, 0).wait()
            return carry

        lax.fori_loop(0, n, wait_one, 0)

    for e in range(N_EXPERTS):
        n_chunks, l_off, g_off = (tab_ref[(3 * i + k) * N_EXPERTS + e] for k in range(3))

        def issue(c, carry, l_off=l_off, g_off=g_off):
            chunk_copy(slot, l_off + c * MOE_CHUNK, g_off + c * MOE_CHUNK).start()
            return carry

        lax.fori_loop(0, n_chunks, issue, 0)

    @pl.when(i > 0)
    def _():
        wait_chunks(1 - slot, chunks_of(i - 1))

    @pl.when(i == pl.num_programs(0) - 1)
    def _():
        wait_chunks(slot, chunks_of(i))


def _dispatch(run_table, h2, rows, xs_init):
    tm = TOK_TILE
    return pl.pallas_call(
        _dispatch_kernel,
        grid_spec=pltpu.PrefetchScalarGridSpec(
            num_scalar_prefetch=1,
            grid=(N_TOK // tm,),
            in_specs=[pl.BlockSpec((tm, D_MODEL), lambda i, tab: (i, 0)),
                      pl.BlockSpec((8, tm), lambda i, tab: (0, i)),
                      pl.BlockSpec(memory_space=pl.ANY)],
            out_specs=pl.BlockSpec(memory_space=pl.ANY),
            scratch_shapes=[pltpu.VMEM((2, MOE_LOCAL_ROWS, D_MODEL), bf16), pltpu.SemaphoreType.DMA((2,))]),
        out_shape=jax.ShapeDtypeStruct(xs_init.shape, xs_init.dtype),
        input_output_aliases={3: 0},
        compiler_params=pltpu.CompilerParams(dimension_semantics=("arbitrary",), vmem_limit_bytes=VMEM_LIMIT),
        name="moe_dispatch",
    )(run_table, h2, rows, xs_init)


def _expert_kernel(te_ref, xs_hbm, wg_hbm, wu_hbm, wd_hbm, *rest, layer, reuse):
    ys_ref, wg_f32, wu_f32, wd_f32, wg_bf, wu_bf, wd_bf, x_ring, slot_ref, sem, x_sem = rest[-11:]
    n_ring, tm = x_ring.shape[0], x_ring.shape[1]
    j = pl.program_id(0)
    n_used = te_ref[1, 0]
    used = j < n_used

    def tile_copy(t):
        return pltpu.make_async_copy(xs_hbm.at[pl.ds(pl.multiple_of(t * tm, tm), tm)], x_ring.at[t % n_ring],
                                     x_sem.at[t % n_ring])

    @pl.when(j == 0)
    def _():
        for t in range(n_ring - 1):
            tile_copy(t).start()

    @pl.when(j + (n_ring - 1) < n_used)
    def _():
        tile_copy(j + (n_ring - 1)).start()

    expert = te_ref[0, j]
    new_expert = jnp.logical_or(j == 0, expert != te_ref[0, jnp.maximum(j - 1, 0)])

    def weight_copies(e, slot):
        return [pltpu.make_async_copy(hbm.at[layer, e], buf.at[slot], sem.at[slot])
                for hbm, buf in ((wg_hbm, wg_f32), (wu_hbm, wu_f32), (wd_hbm, wd_f32))]

    @pl.when(j == 0)
    def _():
        slot_ref[0] = 1
        for cp in weight_copies(expert, 0):
            cp.start()

    if not reuse:
        @pl.when(jnp.logical_not(used))
        def _():
            ys_ref[...] = jnp.zeros_like(ys_ref)

    @pl.when(used & new_expert)
    def _():
        slot = 1 - slot_ref[0]
        slot_ref[0] = slot
        for cp in weight_copies(expert, slot):
            cp.wait()
        wg_bf[...] = wg_f32[slot].astype(bf16)
        wu_bf[...] = wu_f32[slot].astype(bf16)
        wd_bf[...] = wd_f32[slot].astype(bf16)
        nxt = lax.while_loop(lambda t: (t < n_used) & (te_ref[0, jnp.minimum(t, LANE - 1)] == expert),
                             lambda t: t + 1, j + 1)

        @pl.when(nxt < n_used)
        def _():
            for cp in weight_copies(te_ref[0, nxt], 1 - slot):
                cp.start()

    @pl.when(used)
    def _():
        tile_copy(j).wait()
        x = x_ring[j % n_ring]
        hg = jnp.dot(x, wg_bf[...], preferred_element_type=f32)
        hu = jnp.dot(x, wu_bf[...], preferred_element_type=f32)
        hid = (hg * jax.nn.sigmoid(hg) * hu).astype(bf16)
        ys_ref[...] = jnp.dot(hid, wd_bf[...], preferred_element_type=f32)


def _experts(te, xs, w_gate, w_up, w_down, layer, ys_prev=None):
    tm = MOE_TILE
    row = lambda j, te: (jnp.minimum(j, te[1, 0] - 1), 0)
    hbm = pl.BlockSpec(memory_space=pl.ANY)
    reuse = ys_prev is not None
    return pl.pallas_call(
        functools.partial(_expert_kernel, layer=layer, reuse=reuse),
        grid_spec=pltpu.PrefetchScalarGridSpec(
            num_scalar_prefetch=1,
            grid=(MOE_ROWS // tm,),
            in_specs=[hbm, hbm, hbm, hbm] + ([hbm] if reuse else []),
            out_specs=pl.BlockSpec((tm, D_MODEL), row if reuse else (lambda j, te: (j, 0))),
            scratch_shapes=[pltpu.VMEM((2, D_MODEL, D_EXPERT), f32), pltpu.VMEM((2, D_MODEL, D_EXPERT), f32),
                            pltpu.VMEM((2, D_EXPERT, D_MODEL), f32),
                            pltpu.VMEM((D_MODEL, D_EXPERT), bf16), pltpu.VMEM((D_MODEL, D_EXPERT), bf16),
                            pltpu.VMEM((D_EXPERT, D_MODEL), bf16),
                            pltpu.VMEM((3, tm, D_MODEL), bf16),
                            pltpu.SMEM((1,), jnp.int32), pltpu.SemaphoreType.DMA((2,)),
                            pltpu.SemaphoreType.DMA((3,))]),
        out_shape=jax.ShapeDtypeStruct((MOE_ROWS, D_MODEL), f32),
        input_output_aliases={5: 0} if reuse else {},
        compiler_params=pltpu.CompilerParams(dimension_semantics=("arbitrary",), vmem_limit_bytes=VMEM_LIMIT),
        name="moe_experts",
    )(te, xs, w_gate, w_up, w_down, *([ys_prev] if reuse else []))


def _gather_expert_rows(pos0_ref, pos1_ref, ys_ref, buf, sem, tile_of):
    rows = buf.shape[2]
    i = pl.program_id(0)
    slot = i % 2

    def issue(tile, sl):
        base = tile * rows

        def body(t, carry):
            for s, pos_ref in enumerate((pos0_ref, pos1_ref)):
                pltpu.make_async_copy(ys_ref.at[pl.ds(pos_ref[base + t], 1)], buf.at[sl, s, pl.ds(t, 1)],
                                      sem.at[sl]).start(priority=1)
            return carry

        lax.fori_loop(0, rows, body, 0, unroll=8)

    @pl.when(i == 0)
    def _():
        issue(tile_of(0), 0)

    @pl.when(i + 1 < pl.num_programs(0))
    def _():
        issue(tile_of(i + 1), 1 - slot)

    for s in range(2):
        pltpu.make_async_copy(ys_ref.at[pl.ds(0, rows)], buf.at[slot, s], sem.at[slot]).wait()
    return buf[slot, 0], buf[slot, 1]


def _moe_residual(pos0_ref, pos1_ref, ys_ref, x1_ref, rc_ref, mod_ref, buf, sem, tile_of=lambda step: step):
    y0, y1 = _gather_expert_rows(pos0_ref, pos1_ref, ys_ref, buf, sem, tile_of)
    rc = rc_ref[...]
    return x1_ref[...] + mod_ref[0][5:6] * (rc[:, 2:3] * y0 + rc[:, 3:4] * y1)


def _final_kernel(pos0_ref, pos1_ref, ys_ref, x1_ref, rc_ref, mod_ref, fn_ref, yc_ref, yl_ref, buf, sem):
    x2 = _moe_residual(pos0_ref, pos1_ref, ys_ref, x1_ref, rc_ref, mod_ref, buf, sem)
    y = x2 * lax.rsqrt(jnp.mean(x2 * x2, axis=-1, keepdims=True) + EPS) * fn_ref[...]
    is_ctx = pl.program_id(0) < N_CTX // x1_ref.shape[0]

    @pl.when(is_ctx)
    def _():
        yc_ref[...] = y

    @pl.when(jnp.logical_not(is_ctx))
    def _():
        yl_ref[...] = y


def _final_combine(pos0, pos1, ys, x1, rc, mods, fn):
    tc = TOK_TILE
    row = lambda i, p0, p1: (i, 0)
    n_ctx_tiles = N_CTX // tc
    return pl.pallas_call(
        _final_kernel,
        grid_spec=pltpu.PrefetchScalarGridSpec(
            num_scalar_prefetch=2,
            grid=(N_TOK // tc,),
            in_specs=[pl.BlockSpec(memory_space=pl.ANY),
                      pl.BlockSpec((tc, D_MODEL), row),
                      pl.BlockSpec((tc, LANE), row),
                      pl.BlockSpec((1, 6, D_MODEL), lambda i, p0, p1: (_mod_row(i, tc), 0, 0)),
                      pl.BlockSpec((1, D_MODEL), lambda i, p0, p1: (0, 0))],
            out_specs=[pl.BlockSpec((tc, D_MODEL), lambda i, p0, p1: (jnp.minimum(i, n_ctx_tiles - 1), 0)),
                       pl.BlockSpec((tc, D_MODEL), lambda i, p0, p1: (jnp.maximum(i - n_ctx_tiles, 0), 0))],
            scratch_shapes=[pltpu.VMEM((2, 2, tc, D_MODEL), f32), pltpu.SemaphoreType.DMA((2,))]),
        out_shape=[jax.ShapeDtypeStruct((N_CTX, D_MODEL), f32), jax.ShapeDtypeStruct((N_LAT, D_MODEL), f32)],
        compiler_params=pltpu.CompilerParams(dimension_semantics=("arbitrary",), vmem_limit_bytes=VMEM_LIMIT),
        name="moe_combine_final",
    )(pos0, pos1, ys, x1, rc, mods, fn)


def _moe_experts(h2, route_t, w_gate, w_up, w_down, layer, xs_buf, ys_prev):
    rows, te, runs = _rank(route_t)
    xs = _dispatch(runs[:, :3, :N_EXPERTS].reshape(-1), h2, rows, xs_buf)
    return rows[0], rows[1], _experts(te, xs, w_gate, w_up, w_down, layer, ys_prev), xs


def _pad_heads(w):
    lead = w.shape[:-1]
    w = w.reshape(*lead, ML_HEADS, ML_DIM)
    w = jnp.pad(w, [(0, 0)] * len(lead) + [(0, 0), (0, ML_PAD - ML_DIM)])
    return w.reshape(*lead, ML_PW)


def _pack_in_cols(wb):
    o = 0
    qa = wb[..., o:o + NA_WIDTH] * (NA_DIM ** -0.5)
    ka = wb[..., o + NA_WIDTH:o + 2 * NA_WIDTH]
    va = wb[..., o + 2 * NA_WIDTH:o + 3 * NA_WIDTH]
    o += 3 * NA_WIDTH
    qb, kb, vb, ob = [_pad_heads(wb[..., o + j * ML_WIDTH:o + (j + 1) * ML_WIDTH]) for j in range(4)]
    o += 4 * ML_WIDTH
    gates = wb[..., o:o + N_GATE_COLS]
    o += N_GATE_COLS
    pool = wb[..., o:o + POOL_WIDTH]
    main = jnp.concatenate([qa, ka, va, qb, vb, ob, pool], axis=-1)
    gates_p = jnp.pad(gates, [(0, 0)] * (gates.ndim - 1) + [(0, LANE - N_GATE_COLS)])
    return main, gates_p, jnp.concatenate([kb, gates], axis=-1)


def _pack_w_in(w, b):
    w_main, w_gates, w_feat = _pack_in_cols(w)
    b_main, b_gates, b_feat = _pack_in_cols(b.astype(f32))
    return (w_main.astype(bf16), b_main[:, None], w_gates.astype(bf16), b_gates[:, None],
            jnp.swapaxes(w_feat, 1, 2).astype(bf16), b_feat[:, :, None])


def _pack_w_out(w):
    n_l = w.shape[0]
    wb = w[:, NA_WIDTH:NA_WIDTH + ML_WIDTH].reshape(n_l, ML_HEADS, ML_DIM, D_MODEL)
    wb = jnp.pad(wb, ((0, 0), (0, 0), (0, ML_PAD - ML_DIM), (0, 0))).reshape(n_l, ML_PW, D_MODEL)
    return jnp.concatenate([w[:, :NA_WIDTH], wb, w[:, NA_WIDTH + ML_WIDTH:]], axis=1).astype(bf16)


def _block_diag(w):
    n_l, g, c, _ = w.shape
    eye = jnp.eye(g, dtype=w.dtype)
    return (eye[None, :, None, :, None] * w[:, :, :, None, :]).reshape(n_l, g * c, g * c)


def kernel(x_prompt, x_sample, cache_k_attn, cache_v_attn, state_mlstm_C, state_mlstm_n, state_mlstm_m, c, c_ctx,
           w_ada, b_ada, norm1, w_in, b_in, rpb, ml_norm, w_pool, pool_scale, w_out, norm2, w_router, b_router,
           w_gate, w_up, w_down, final_norm):
    dt = x_prompt.dtype
    x_ctx = x_prompt.reshape(N_CTX, D_MODEL).astype(f32)
    x_lat = x_sample.reshape(N_LAT, D_MODEL).astype(f32)
    x_lat_block0 = 0
    cvec = jnp.concatenate([c_ctx[None], c, jnp.zeros((8 - 1 - DEC_BATCH, D_MODEL), c.dtype)], axis=0).astype(f32)
    mods_all = _ada(cvec, w_ada.astype(f32), b_ada.astype(f32))
    mods_all = mods_all[:, :1 + DEC_BATCH].reshape(DEPTH, 1 + DEC_BATCH, 6, D_MODEL)

    wr_t = w_router.astype(f32).T
    br_t = b_router.astype(f32)[:, None]
    fn = final_norm.astype(f32)[None]

    na_bias = _na_bias_tables(rpb)
    xs_buf = jnp.zeros((MOE_ROWS, D_MODEL), bf16)
    in_params = (norm1.astype(f32)[:, None],) + _pack_w_in(w_in, b_in)
    out_params = (_block_diag(w_pool.astype(f32)).astype(bf16), pool_scale.astype(f32)[:, None],
                  _pad_heads(ml_norm.astype(f32))[:, None], _pack_w_out(w_out), norm2.astype(f32)[:, None])

    new_k = jnp.zeros((BATCH, DEPTH, SEQ, NA_WIDTH), f32)
    new_v = jnp.zeros_like(new_k)
    Cs, ns, ms = [], [], []
    pending = None
    for l in range(DEPTH):
        mods = mods_all[l]
        if pending is None:
            qkva, new_k, new_v, qvo, kt, gates, gates_t, pin = _in_proj(x_ctx, x_lat, mods, l, *in_params,
                                                                        new_k, new_v)
        else:
            x, qkva, new_k, new_v, qvo, kt, gates, gates_t, pin = _moe_in_proj(
                *pending, mods_all[l - 1], mods, l, *in_params, new_k, new_v)
            x_ctx, x_lat, x_lat_block0 = x, x, N_CTX // TOK_TILE

        oa_ctx = _ctx_attention(qkva.reshape(N_TOK // SEQ, SEQ, W_A))
        ck = (cache_k_attn[:, l].reshape(DEC_BATCH, PAST_LEN, NA_WIDTH)).astype(bf16)
        cv = (cache_v_attn[:, l].reshape(DEC_BATCH, PAST_LEN, NA_WIDTH)).astype(bf16)
        oa_lat = _neighborhood_attention(qkva.reshape(N_TOK // DEC_SEQ, DEC_SEQ, W_A), ck, cv, na_bias, l)

        c_l, m_l = _pack_ml_state(state_mlstm_C[:, l], state_mlstm_n[:, l], state_mlstm_m[:, l])
        hf, hb, c_fin, m_fin = _mlstm(qvo, kt, gates, gates_t, c_l, m_l)
        C_l, n_l, m_l2 = _unpack_ml_state(c_fin[:BATCH], m_fin[:BATCH])
        Cs.append(C_l)
        ns.append(n_l)
        ms.append(m_l2)

        x1, h2, route_t, rc = _out_proj(x_ctx, x_lat, x_lat_block0, mods, l,
                                        oa_ctx.reshape(N_CTX, NA_WIDTH), oa_lat.reshape(N_LAT, NA_WIDTH),
                                        hf.reshape(N_TOK, ML_PW), hb.reshape(N_TOK, ML_PW), qvo, pin,
                                        *out_params, wr_t, br_t)
        ys_prev = None if pending is None else pending[2]
        pos0, pos1, ys, xs_buf = _moe_experts(h2, route_t, w_gate, w_up, w_down, l, xs_buf, ys_prev)
        pending = (pos0, pos1, ys, x1, rc)

    x = _final_combine(*pending, mods_all[DEPTH - 1], fn)
    y_prompt = x[0].reshape(BATCH, SEQ, D_MODEL).astype(dt)
    y_sample = x[1].reshape(DEC_BATCH, DEC_SEQ, D_MODEL).astype(dt)
    new_k, new_v = (a.reshape(BATCH, DEPTH, SEQ, NA_HEADS, NA_DIM).astype(dt) for a in (new_k, new_v))
    return (y_prompt, y_sample, new_k, new_v,
            jnp.stack(Cs, axis=1).astype(dt), jnp.stack(ns, axis=1).astype(dt), jnp.stack(ms, axis=1).astype(dt))
```

```python
import functools

import numpy as np
import jax
import jax.numpy as jnp
from jax import lax
from jax.experimental import pallas as pl
from jax.experimental.pallas import tpu as pltpu

D_MODEL = 1024
BATCH = 16
SEQ = 256
DEPTH = 4
DEC_BATCH = 2
DEC_SEQ = 4096
PAST_LEN = 256
GRID_W = 64
EPS = 1e-6
NEG_INF = -1e30
NA_HEADS = 6
NA_DIM = 64
NA_WIDTH = NA_HEADS * NA_DIM
NA_ROWS = 8
NA_COLS = 16
RPB_ROWS = 2 * NA_ROWS - 1
RPB_COLS = 2 * NA_COLS - 1
ML_HEADS = 4
ML_DIM = 96
ML_WIDTH = ML_HEADS * ML_DIM
POOL_WINDOWS = (2, 4, 8, 16)
POOL_GROUPS = 4
POOL_DIM = 64
POOL_WIDTH = POOL_GROUPS * POOL_DIM
N_GATE_COLS = 4 * ML_HEADS
N_EXPERTS = 16
N_EXPERT_GROUPS = 4
EXPERTS_PER_GROUP = N_EXPERTS // N_EXPERT_GROUPS
D_EXPERT = 512
ADA_DIM = 6 * D_MODEL

N_CTX = BATCH * SEQ
N_LAT = DEC_BATCH * DEC_SEQ
N_TOK = N_CTX + N_LAT
LANE = 128
ML_PAD = LANE
ML_PW = ML_HEADS * ML_PAD
CAUG = ML_PAD
NA_PAIRS = NA_HEADS // 2
TOK_TILE = 512
ML_CHUNK = 256
NA_QROWS = 4
NA_KROWS = NA_QROWS + NA_ROWS - 1
POOL_HALO = max(POOL_WINDOWS) // 2
POOL_BLOCK = 128
MOE_TILE = 512
MOE_CHUNK = 16
MOE_LOCAL_ROWS = -(-(2 * TOK_TILE + N_EXPERTS * (MOE_CHUNK - 1)) // LANE) * LANE
MOE_ROWS = -(-(2 * N_TOK + (N_TOK // TOK_TILE) * N_EXPERTS * (MOE_CHUNK - 1) + N_EXPERTS * (MOE_TILE - 1))
             // MOE_TILE) * MOE_TILE
VMEM_LIMIT = 56 * 1024 * 1024

W_A = 3 * NA_WIDTH
W_B = 3 * ML_PW
N_TCOLS = ML_PW + N_GATE_COLS
W_MAIN = W_A + W_B + POOL_WIDTH

f32 = jnp.float32
bf16 = jnp.bfloat16
HI = lax.Precision.HIGHEST


def _nt(a, b, **kw):
    return lax.dot_general(a, b, (((1,), (1,)), ((), ())), preferred_element_type=f32, **kw)


def _mod_row(i, tile):
    n_ctx_tiles = N_CTX // tile
    per_batch = DEC_SEQ // tile
    return jnp.where(i < n_ctx_tiles, 0, 1 + (i - n_ctx_tiles) // per_batch)


def _ada_kernel(c_ref, w_ref, b_ref, o_ref):
    s = c_ref[...]
    s = s * jax.nn.sigmoid(s)
    o_ref[0] = jnp.dot(s.astype(bf16), w_ref[0].astype(bf16), preferred_element_type=f32) + b_ref[0]


def _ada(cvec, w_ada, b_ada):
    nj = ADA_DIM // D_MODEL
    return pl.pallas_call(
        _ada_kernel,
        grid=(DEPTH, nj),
        in_specs=[pl.BlockSpec((8, D_MODEL), lambda l, j: (0, 0)),
                  pl.BlockSpec((1, D_MODEL, D_MODEL), lambda l, j: (l, 0, j)),
                  pl.BlockSpec((1, 1, D_MODEL), lambda l, j: (l, 0, j))],
        out_specs=pl.BlockSpec((1, 8, D_MODEL), lambda l, j: (l, 0, j)),
        out_shape=jax.ShapeDtypeStruct((DEPTH, 8, ADA_DIM), f32),
        name="ada_mod",
    )(cvec, w_ada, b_ada.reshape(DEPTH, 1, ADA_DIM))


def _in_tile(step):
    return (step + N_CTX // TOK_TILE) % (N_TOK // TOK_TILE)


def _in_kernel(xc_ref, xl_ref, *refs):
    is_ctx = _in_tile(pl.program_id(0)) < N_CTX // xc_ref.shape[0]
    _in_body(jnp.where(is_ctx, xc_ref[...], xl_ref[...]), *refs)


def _moe_in_kernel(pos0_ref, pos1_ref, ys_ref, x1_ref, rc_ref, mod_prev_ref, *refs):
    in_refs, x_out_ref, out_refs, (buf, sem) = refs[:10], refs[10], refs[11:-2], refs[-2:]
    x = _moe_residual(pos0_ref, pos1_ref, ys_ref, x1_ref, rc_ref, mod_prev_ref, buf, sem, tile_of=_in_tile)
    x_out_ref[...] = x
    _in_body(x, *in_refs, *out_refs)


def _in_body(x, mod_ref, n1_ref, w_ref, b_ref, wg_ref, bg_ref, wt_ref, bt_ref, k_in_ref, v_in_ref,
             a_ref, k_ref, v_ref, b_out_ref, kt_ref, g_ref, gt_ref, pin_ref):
    del k_in_ref, v_in_ref
    mod = mod_ref[0]
    h = x * lax.rsqrt(jnp.mean(x * x, axis=-1, keepdims=True) + EPS) * n1_ref[...]
    h = (h * (1.0 + mod[1:2]) + mod[0:1]).astype(bf16)
    pa = jnp.dot(h, w_ref[:, 0:W_A], preferred_element_type=f32) + b_ref[:, 0:W_A]
    a_ref[...] = pa.astype(bf16)
    k_ref[...] = pa[:, NA_WIDTH:2 * NA_WIDTH].reshape(k_ref.shape)
    v_ref[...] = pa[:, 2 * NA_WIDTH:W_A].reshape(v_ref.shape)
    for j in range(3):
        lo = W_A + j * ML_PW
        pb = jnp.dot(h, w_ref[:, lo:lo + ML_PW], preferred_element_type=f32) + b_ref[:, lo:lo + ML_PW]
        if j == 0:
            pb = pb * (ML_DIM ** -0.5)
        b_out_ref[:, j * ML_PW:(j + 1) * ML_PW] = pb.astype(bf16)
    lo = W_A + W_B
    pin_ref[...] = jnp.dot(h, w_ref[:, lo:lo + POOL_WIDTH], preferred_element_type=f32) + b_ref[:, lo:lo + POOL_WIDTH]
    g_ref[...] = jnp.dot(h, wg_ref[...], preferred_element_type=f32) + bg_ref[...]
    t = _nt(wt_ref[...], h) + bt_ref[...]
    kt_ref[...] = t[0:ML_PW].astype(bf16)
    gt_ref[...] = t[ML_PW:N_TCOLS]


def _in_proj_specs(layer):
    tm = TOK_TILE
    lyr = lambda shape: pl.BlockSpec((None,) + shape, lambda i, *_: (layer, 0, 0))
    rows = lambda width: pl.BlockSpec((tm, width), lambda i, *_: (_in_tile(i), 0))
    cols = lambda height: pl.BlockSpec((height, tm), lambda i, *_: (0, _in_tile(i)))
    param_specs = [pl.BlockSpec((1, 6, D_MODEL), lambda i, *_: (_mod_row(_in_tile(i), tm), 0, 0)),
                   lyr((1, D_MODEL)), lyr((D_MODEL, W_MAIN)), lyr((1, W_MAIN)), lyr((D_MODEL, LANE)), lyr((1, LANE)),
                   lyr((N_TCOLS, D_MODEL)), lyr((N_TCOLS, 1)),
                   pl.BlockSpec(memory_space=pl.ANY), pl.BlockSpec(memory_space=pl.ANY)]
    n_ctx_tiles = N_CTX // tm
    kv_spec = pl.BlockSpec((tm // SEQ, None, SEQ, NA_WIDTH),
                           lambda i, *_: (jnp.where(_in_tile(i) < n_ctx_tiles, _in_tile(i), 0), layer, 0, 0))
    kv_shape = jax.ShapeDtypeStruct((BATCH, DEPTH, SEQ, NA_WIDTH), f32)
    out_specs = [rows(W_A), kv_spec, kv_spec, rows(W_B), cols(ML_PW), rows(LANE), cols(N_GATE_COLS),
                 rows(POOL_WIDTH)]
    out_shape = [jax.ShapeDtypeStruct((N_TOK, W_A), bf16), kv_shape, kv_shape,
                 jax.ShapeDtypeStruct((N_TOK, W_B), bf16),
                 jax.ShapeDtypeStruct((ML_PW, N_TOK), bf16),
                 jax.ShapeDtypeStruct((N_TOK, LANE), f32),
                 jax.ShapeDtypeStruct((N_GATE_COLS, N_TOK), f32),
                 jax.ShapeDtypeStruct((N_TOK, POOL_WIDTH), f32)]
    return rows, param_specs, out_specs, out_shape


def _in_proj(x_ctx, x_lat, mods, layer, *params):
    tm = TOK_TILE
    n_ctx_tiles = N_CTX // tm
    rows, param_specs, out_specs, out_shape = _in_proj_specs(layer)
    return pl.pallas_call(
        _in_kernel,
        grid=(N_TOK // tm,),
        in_specs=[pl.BlockSpec((tm, D_MODEL), lambda i: (jnp.where(_in_tile(i) < n_ctx_tiles, _in_tile(i), 0), 0)),
                  pl.BlockSpec((tm, D_MODEL), lambda i: (jnp.maximum(_in_tile(i) - n_ctx_tiles, 0), 0))] + param_specs,
        out_specs=out_specs,
        out_shape=out_shape,
        input_output_aliases={2 + len(param_specs) - 2: 1, 2 + len(param_specs) - 1: 2},
        compiler_params=pltpu.CompilerParams(dimension_semantics=("arbitrary",), vmem_limit_bytes=VMEM_LIMIT),
        name="in_proj",
    )(x_ctx, x_lat, mods, *params)


def _moe_in_proj(pos0, pos1, ys, x1, rc, mods_prev, mods, layer, *params):
    tm = TOK_TILE
    rows, param_specs, out_specs, out_shape = _in_proj_specs(layer)
    return pl.pallas_call(
        _moe_in_kernel,
        grid_spec=pltpu.PrefetchScalarGridSpec(
            num_scalar_prefetch=2,
            grid=(N_TOK // tm,),
            in_specs=[pl.BlockSpec(memory_space=pl.ANY), rows(D_MODEL), rows(LANE),
                      pl.BlockSpec((1, 6, D_MODEL), lambda i, *_: (_mod_row(_in_tile(i), tm), 0, 0))] + param_specs,
            out_specs=[rows(D_MODEL)] + out_specs,
            scratch_shapes=[pltpu.VMEM((2, 2, tm, D_MODEL), f32), pltpu.SemaphoreType.DMA((2,))]),
        out_shape=[jax.ShapeDtypeStruct((N_TOK, D_MODEL), f32)] + out_shape,
        input_output_aliases={6 + len(param_specs) - 2: 2, 6 + len(param_specs) - 1: 3},
        compiler_params=pltpu.CompilerParams(dimension_semantics=("arbitrary",), vmem_limit_bytes=VMEM_LIMIT),
        name="moe_combine_in_proj",
    )(pos0, pos1, ys, x1, rc, mods_prev, mods, *params)


def _pair_attention(qp, parts):
    lane = lax.broadcasted_iota(jnp.int32, (1, LANE), 1)
    outs = []
    for j in range(2):
        in_half = (lane >= j * NA_DIM) & (lane < (j + 1) * NA_DIM)
        qm = jnp.where(in_half, qp, jnp.zeros_like(qp))
        scores = []
        for k, _, bias in parts:
            s = _nt(qm, k)
            if bias is not None:
                s = s + bias[j]
            scores.append(s)
        m = scores[0].max(axis=-1, keepdims=True)
        for s in scores[1:]:
            m = jnp.maximum(m, s.max(axis=-1, keepdims=True))
        den = None
        acc = None
        for s, (_, v, _) in zip(scores, parts):
            p = jnp.exp(s - m)
            ps = p.sum(axis=-1, keepdims=True)
            den = ps if den is None else den + ps
            o = jnp.dot(p.astype(bf16), v, preferred_element_type=f32)
            acc = o if acc is None else acc + o
        outs.append(acc / den)
    return jnp.where(lane < NA_DIM, outs[0], outs[1])


def _ctx_attn_kernel(q_ref, k_ref, v_ref, o_ref):
    for p in range(NA_PAIRS):
        sl = slice(p * LANE, (p + 1) * LANE)
        o = _pair_attention(q_ref[0, :, sl], [(k_ref[0, :, sl], v_ref[0, :, sl], None)])
        o_ref[0, :, sl] = o.astype(bf16)


def _ctx_attention(qkv):
    blk = lambda c: pl.BlockSpec((1, SEQ, NA_WIDTH), lambda b, c=c: (b, 0, c))
    return pl.pallas_call(
        _ctx_attn_kernel,
        grid=(BATCH,),
        in_specs=[blk(0), blk(1), blk(2)],
        out_specs=pl.BlockSpec((1, SEQ, NA_WIDTH), lambda b: (b, 0, 0)),
        out_shape=jax.ShapeDtypeStruct((BATCH, SEQ, NA_WIDTH), bf16),
        name="ctx_attention",
    )(qkv, qkv, qkv)


def _na_window_start(rb):
    return jnp.clip(rb * NA_QROWS - NA_ROWS // 2, 0, DEC_SEQ // GRID_W - NA_KROWS)


def _na_bias(tab_ref, head, rb):
    rows = DEC_SEQ // GRID_W
    ws = _na_window_start(rb)
    lane = lax.broadcasted_iota(jnp.int32, (1, NA_KROWS * GRID_W), 1)
    per_qrow = []
    for dq in range(NA_QROWS):
        qr = rb * NA_QROWS + dq
        a0 = ws - qr + (NA_ROWS - 1) + NA_KROWS
        tiles = [tab_ref[head, a0 + 2 * j] for j in range((NA_KROWS + 1) // 2)]
        t = jnp.concatenate(tiles, axis=1)[:, :NA_KROWS * GRID_W]
        lo = (jnp.clip(qr - NA_ROWS // 2, 0, rows - NA_ROWS) - ws) * GRID_W
        ok = (lane >= lo) & (lane < lo + NA_ROWS * GRID_W)
        per_qrow.append(jnp.where(ok, t, NEG_INF))
    return jnp.concatenate(per_qrow, axis=0)


def _na_kernel(q_ref, k_ref, v_ref, ck_ref, cv_ref, tab_ref, o_ref):
    rb = pl.program_id(1)
    start = pl.multiple_of(_na_window_start(rb) * GRID_W, GRID_W)
    nk = NA_KROWS * GRID_W
    for p in range(NA_PAIRS):
        sl = slice(p * LANE, (p + 1) * LANE)
        bias = [_na_bias(tab_ref.at[0], 2 * p + j, rb) for j in range(2)]
        parts = [(k_ref[0, pl.ds(start, nk), sl], v_ref[0, pl.ds(start, nk), sl], bias),
                 (ck_ref[0, :, sl], cv_ref[0, :, sl], None)]
        o = _pair_attention(q_ref[0, :, sl], parts)
        o_ref[0, :, sl] = o.astype(bf16)


def _neighborhood_attention(qkv, ck, cv, tables, layer):
    nq = NA_QROWS * GRID_W
    n_rb = DEC_SEQ // nq
    return pl.pallas_call(
        _na_kernel,
        grid=(DEC_BATCH, n_rb),
        in_specs=[pl.BlockSpec((1, nq, NA_WIDTH), lambda b, r: (1 + b, r, 0)),
                  pl.BlockSpec((1, DEC_SEQ, NA_WIDTH), lambda b, r: (1 + b, 0, 1)),
                  pl.BlockSpec((1, DEC_SEQ, NA_WIDTH), lambda b, r: (1 + b, 0, 2)),
                  pl.BlockSpec((1, PAST_LEN, NA_WIDTH), lambda b, r: (b, 0, 0)),
                  pl.BlockSpec((1, PAST_LEN, NA_WIDTH), lambda b, r: (b, 0, 0)),
                  pl.BlockSpec((1,) + tables.shape[1:], lambda b, r: (layer, 0, 0, 0, 0))],
        out_specs=pl.BlockSpec((1, nq, NA_WIDTH), lambda b, r: (b, r, 0)),
        out_shape=jax.ShapeDtypeStruct((DEC_BATCH, DEC_SEQ, NA_WIDTH), bf16),
        compiler_params=pltpu.CompilerParams(dimension_semantics=("arbitrary", "arbitrary"),
                                             vmem_limit_bytes=VMEM_LIMIT),
        name="neighborhood_attention",
    )(qkv, qkv, qkv, ck, cv, tables)


def _na_bias_tables(rpb):
    qc = np.arange(GRID_W)[:, None]
    kc = np.arange(GRID_W)[None, :]
    dc = np.clip(kc - qc + NA_COLS - 1, 0, RPB_COLS - 1)
    col_start = np.clip(qc - NA_COLS // 2, 0, GRID_W - NA_COLS)
    col_ok = (kc >= col_start) & (kc < col_start + NA_COLS)
    pick_col = (dc[None] == np.arange(RPB_COLS)[:, None, None]).astype(np.float32)
    rpb_pad = jnp.pad(rpb.astype(f32), ((0, 0), (0, 0), (NA_KROWS, NA_KROWS + 1), (0, 0)))
    n_a = rpb_pad.shape[2] - 1
    rows2 = jnp.stack([rpb_pad[:, :, :-1], rpb_pad[:, :, 1:]], axis=3)
    pick2 = np.zeros((2, RPB_COLS, GRID_W, 2 * GRID_W), np.float32)
    for j in range(2):
        pick2[j, :, :, j * GRID_W:(j + 1) * GRID_W] = pick_col
    tiles = jnp.einsum('lhajb,jbqc->lhaqc', rows2, pick2, precision=HI)
    a_pad = np.arange(n_a)[:, None] + np.arange(2)[None, :]
    row_ok = (a_pad >= NA_KROWS) & (a_pad < NA_KROWS + RPB_ROWS)
    ok = (row_ok[:, None, :, None] & col_ok[None, :, None, :]).reshape(n_a, GRID_W, 2 * GRID_W)
    return jnp.where(ok[None, None], tiles, NEG_INF)


def _log_sigmoid(x):
    return -(jnp.maximum(-x, 0.0) + jnp.log(1.0 + jnp.exp(-jnp.abs(x))))


def _split3(x):
    hi = x.astype(bf16)
    r1 = x - hi.astype(f32)
    mid = r1.astype(bf16)
    lo = (r1 - mid.astype(f32)).astype(bf16)
    return hi, mid, lo


def _mlstm_kernel(qf_ref, vf_ref, ktf_ref, gf_ref, gtf_ref, qb_ref, vb_ref, ktb_ref, gb_ref, gtb_ref,
                  c0_ref, m0_ref, hf_ref, hb_ref, c_out_ref, m_out_ref, c_scr, m_scr):
    L = ML_CHUNK
    seq, c, n_chunks, _ = _ml_schedule(pl.program_id(0))

    @pl.when(c == 0)
    def _():
        is_ctx = seq < BATCH
        c_scr[...] = jnp.where(is_ctx, 0.0, c0_ref[0])
        m_scr[...] = jnp.where(is_ctx, 0.0, m0_ref[0])

    ri = lax.broadcasted_iota(jnp.int32, (L, L), 0)
    ci = lax.broadcasted_iota(jnp.int32, (L, L), 1)
    lane = lax.broadcasted_iota(jnp.int32, (L, ML_PAD), 1)
    is_ncol = lane == ML_DIM
    lower = ri >= ci
    upper = ri <= ci
    lower_b = jnp.where(lower, 1.0, 0.0).astype(bf16)
    upper_b = jnp.where(upper, 1.0, 0.0).astype(bf16)
    dirs = ((qf_ref, ktf_ref, vf_ref, gf_ref, gtf_ref, hf_ref), (qb_ref, ktb_ref, vb_ref, gb_ref, gtb_ref, hb_ref))
    for d, (q_ref, kt_ref, v_ref, g_ref, gt_ref, h_ref) in enumerate(dirs):
        g = g_ref[...][:, 0:N_GATE_COLS]
        gt = gt_ref[...]
        lf_c = _log_sigmoid(g)
        lf_r = _log_sigmoid(gt)
        b_cols = sum(jnp.dot(lower_b, part, preferred_element_type=f32) for part in _split3(lf_c))
        b_rows = sum(jnp.dot(part, upper_b, preferred_element_type=f32) for part in _split3(lf_r))
        tot_c = jnp.sum(lf_c, axis=0, keepdims=True)
        tot_r = jnp.sum(lf_r, axis=1, keepdims=True)
        visible = lower
        if d == 1:
            b_cols = tot_c - b_cols + lf_c
            b_rows = tot_r - b_rows + lf_r
            visible = upper
        for hd in range(ML_HEADS):
            st = d * ML_HEADS + hd
            ci_ = 2 * ML_HEADS * d + hd
            cf_ = ci_ + ML_HEADS
            sl = slice(hd * ML_PAD, (hd + 1) * ML_PAD)
            bc = b_cols[:, cf_:cf_ + 1]
            br = b_rows[cf_:cf_ + 1, :]
            li_r = gt[ci_:ci_ + 1, :]
            m_prev = m_scr[st:st + 1, 0:1]
            dmat = jnp.where(visible, bc - br + li_r, NEG_INF)
            inter = bc + m_prev
            m_t = jnp.maximum(inter, dmat.max(axis=-1, keepdims=True))
            w_intra = jnp.exp(dmat - m_t)
            w_inter = jnp.exp(inter - m_t)
            qh = q_ref[0, :, sl]
            kht = kt_ref[sl, :]
            v_aug = jnp.where(is_ncol, jnp.ones((), bf16), v_ref[0, :, sl])
            s = (jnp.dot(qh, kht, preferred_element_type=f32) * w_intra).astype(bf16)
            c_aug = c_scr[st]
            na = (w_inter * jnp.dot(qh, c_aug.astype(bf16), preferred_element_type=f32)
                  + jnp.dot(s, v_aug, preferred_element_type=f32))
            den = na[:, ML_DIM:ML_DIM + 1]
            h_ref[0, :, sl] = jnp.where(lane < ML_DIM, na / jnp.maximum(jnp.abs(den), jnp.exp(-m_t)), 0.0)
            b_end = tot_r[cf_:cf_ + 1, :]
            g_row = b_end - br + li_r
            m_new = jnp.maximum(b_end + m_prev, g_row.max(axis=1, keepdims=True))
            decay = jnp.exp(b_end + m_prev - m_new)
            kwt = (kht.astype(f32) * jnp.exp(g_row - m_new)).astype(bf16)
            c_scr[st] = decay * c_aug + jnp.dot(kwt, v_aug, preferred_element_type=f32)
            m_scr[st:st + 1, :] = jnp.broadcast_to(m_new, (1, LANE))

    @pl.when(c == n_chunks - 1)
    def _():
        c_out_ref[0] = c_scr[...]
        m_out_ref[0] = m_scr[...]


def _ml_schedule(s):
    nc_ctx, nc_lat = SEQ // ML_CHUNK, DEC_SEQ // ML_CHUNK
    n_ctx_steps = BATCH * nc_ctx
    is_ctx = s < n_ctx_steps
    t = s - n_ctx_steps
    seq = jnp.where(is_ctx, s // nc_ctx, BATCH + t // nc_lat)
    c = jnp.where(is_ctx, s % nc_ctx, t % nc_lat)
    nc = jnp.where(is_ctx, nc_ctx, nc_lat)
    base = jnp.where(is_ctx, (s // nc_ctx) * nc_ctx, n_ctx_steps + (t // nc_lat) * nc_lat)
    return seq, c, nc, base


def _mlstm(qvo, kt, gates, gates_t, c0, m0):
    L = ML_CHUNK
    n_seq = BATCH + DEC_BATCH

    def fwd(s):
        _, c, _, base = _ml_schedule(s)
        return base + c

    def bwd(s):
        _, c, nc, base = _ml_schedule(s)
        return base + nc - 1 - c

    seq_of = lambda s: _ml_schedule(s)[0]
    lat_of = lambda s: jnp.maximum(seq_of(s) - BATCH, 0)

    def specs(pos):
        return [pl.BlockSpec((1, L, ML_PW), lambda s, j=j: (pos(s), 0, j)) for j in range(2)] + [
            pl.BlockSpec((ML_PW, L), lambda s: (0, pos(s))),
            pl.BlockSpec((L, LANE), lambda s: (pos(s), 0)),
            pl.BlockSpec((N_GATE_COLS, L), lambda s: (0, pos(s)))]

    q3 = qvo.reshape(N_TOK // L, L, W_B)
    n_str = 2 * ML_HEADS
    return pl.pallas_call(
        _mlstm_kernel,
        grid=(N_TOK // L,),
        in_specs=specs(fwd) + specs(bwd) + [
            pl.BlockSpec((1, n_str, ML_PAD, CAUG), lambda s: (lat_of(s), 0, 0, 0)),
            pl.BlockSpec((1, n_str, LANE), lambda s: (lat_of(s), 0, 0))],
        out_specs=[pl.BlockSpec((1, L, ML_PW), lambda s: (fwd(s), 0, 0)),
                   pl.BlockSpec((1, L, ML_PW), lambda s: (bwd(s), 0, 0)),
                   pl.BlockSpec((1, n_str, ML_PAD, CAUG), lambda s: (seq_of(s), 0, 0, 0)),
                   pl.BlockSpec((1, n_str, LANE), lambda s: (seq_of(s), 0, 0))],
        out_shape=[jax.ShapeDtypeStruct((N_TOK // L, L, ML_PW), f32),
                   jax.ShapeDtypeStruct((N_TOK // L, L, ML_PW), f32),
                   jax.ShapeDtypeStruct((n_seq, n_str, ML_PAD, CAUG), f32),
                   jax.ShapeDtypeStruct((n_seq, n_str, LANE), f32)],
        scratch_shapes=[pltpu.VMEM((n_str, ML_PAD, CAUG), f32), pltpu.VMEM((n_str, LANE), f32)],
        compiler_params=pltpu.CompilerParams(dimension_semantics=("arbitrary",), vmem_limit_bytes=VMEM_LIMIT),
        name="mlstm",
    )(q3, q3, kt, gates, gates_t, q3, q3, kt, gates, gates_t, c0, m0)


def _pack_ml_state(C, n, m):
    B = C.shape[0]
    c_aug = jnp.zeros((B, 2, ML_HEADS, ML_PAD, CAUG), f32)
    c_aug = c_aug.at[:, :, :, :ML_DIM, :ML_DIM].set(C.astype(f32))
    c_aug = c_aug.at[:, :, :, :ML_DIM, ML_DIM].set(n.astype(f32))
    m_b = jnp.broadcast_to(m.astype(f32)[..., None], (B, 2, ML_HEADS, LANE))
    return c_aug.reshape(B, 2 * ML_HEADS, ML_PAD, CAUG), m_b.reshape(B, 2 * ML_HEADS, LANE)


def _unpack_ml_state(c_aug, m_b):
    B = c_aug.shape[0]
    c_aug = c_aug.reshape(B, 2, ML_HEADS, ML_PAD, CAUG)
    return (c_aug[:, :, :, :ML_DIM, :ML_DIM], c_aug[:, :, :, :ML_DIM, ML_DIM],
            m_b.reshape(B, 2, ML_HEADS, LANE)[..., 0])


def _pool_rows(u_prev, u_cur, u_next, w_bd, scale, t0, seq_len):
    tm = u_cur.shape[0]
    u_win = jnp.concatenate([u_prev, u_cur, u_next], axis=0)
    u_hi = u_win.astype(bf16)
    u_lo = (u_win - u_hi.astype(f32)).astype(bf16)
    lane = lax.broadcasted_iota(jnp.int32, (1, LANE), 1)
    blocks = []
    for r0 in range(0, tm, POOL_BLOCK):
        win = slice(r0, r0 + POOL_BLOCK + 2 * POOL_HALO)
        t_abs = t0 + r0 + lax.broadcasted_iota(jnp.int32, (POOL_BLOCK, 1), 0)
        s_abs = t0 + r0 - POOL_HALO + lax.broadcasted_iota(jnp.int32, (1, POOL_BLOCK + 2 * POOL_HALO), 1)
        t_loc = t_abs & (seq_len - 1)
        seq_start = t_abs - t_loc
        means = []
        for w in POOL_WINDOWS:
            lo = jnp.maximum(t_loc - w // 2, 0)
            hi = jnp.minimum(t_loc - w // 2 + w, seq_len)
            in_win = (s_abs >= seq_start + lo) & (s_abs < seq_start + hi)
            means.append((jnp.where(in_win, 1.0, 0.0).astype(bf16), 1.0 / (hi - lo).astype(f32)))
        pooled = []
        for p in range(POOL_GROUPS // 2):
            sl = slice(p * LANE, (p + 1) * LANE)
            halves = []
            for a, inv_cnt in means[2 * p:2 * p + 2]:
                tot = (jnp.dot(a, u_hi[win, sl], preferred_element_type=f32)
                       + jnp.dot(a, u_lo[win, sl], preferred_element_type=f32))
                halves.append(tot * inv_cnt)
            pooled.append(jnp.where(lane < POOL_DIM, halves[0], halves[1]) - u_cur[r0:r0 + POOL_BLOCK, sl])
        blocks.append(jnp.concatenate(pooled, axis=1))
    pooled = jnp.concatenate(blocks, axis=0).astype(bf16)
    return jnp.dot(pooled, w_bd, preferred_element_type=f32) * scale


def _top2_sum(a, b, c, d):
    hi1, lo1 = jnp.maximum(a, b), jnp.minimum(a, b)
    hi2, lo2 = jnp.maximum(c, d), jnp.minimum(c, d)
    return jnp.maximum(hi1, hi2) + jnp.maximum(jnp.minimum(hi1, hi2), jnp.maximum(lo1, lo2))


def _first_match(vals, target):
    idx = jnp.full_like(target, float(len(vals) - 1))
    for i in range(len(vals) - 2, -1, -1):
        idx = jnp.where(vals[i] == target, float(i), idx)
    return idx


def _pick(vals, idx):
    out = vals[-1]
    for i in range(len(vals) - 2, -1, -1):
        out = jnp.where(idx == float(i), vals[i], out)
    return out


def _route(logits_t, bias_t):
    scores = jax.nn.sigmoid(logits_t)
    sel = scores + bias_t
    row = lambda a, i: a[i:i + 1, :]
    grp = [_top2_sum(*[row(sel, EXPERTS_PER_GROUP * g + i) for i in range(EXPERTS_PER_GROUP)])
           for g in range(N_EXPERT_GROUPS)]
    best = functools.reduce(jnp.maximum, grp)
    gidx = _first_match(grp, best)
    sel_g = [_pick([row(sel, EXPERTS_PER_GROUP * g + i) for g in range(N_EXPERT_GROUPS)], gidx)
             for i in range(EXPERTS_PER_GROUP)]
    sco_g = [_pick([row(scores, EXPERTS_PER_GROUP * g + i) for g in range(N_EXPERT_GROUPS)], gidx)
             for i in range(EXPERTS_PER_GROUP)]
    i0 = _first_match(sel_g, functools.reduce(jnp.maximum, sel_g))
    rest = [jnp.where(i0 == float(i), -jnp.inf, sel_g[i]) for i in range(EXPERTS_PER_GROUP)]
    i1 = _first_match(rest, functools.reduce(jnp.maximum, rest))
    s0, s1 = _pick(sco_g, i0), _pick(sco_g, i1)
    tot = s0 + s1
    rid = lax.broadcasted_iota(jnp.int32, (LANE, logits_t.shape[1]), 0)
    rows = (EXPERTS_PER_GROUP * gidx + i0, EXPERTS_PER_GROUP * gidx + i1, s0 / tot, s1 / tot)
    out = jnp.zeros(rid.shape, f32)
    for i, r in enumerate(rows):
        out = jnp.where(rid == i, r, out)
    return out


def _out_kernel(xc_ref, xl_ref, mod_ref, oac_ref, oal_ref, hf_ref, hb_ref, ob_ref, up_ref, uc_ref, un_ref, wp_ref,
                psc_ref, mln_ref, wo_ref, n2_ref, wr_ref, br_ref, x1_ref, h2_ref, rt_ref, rc_ref):
    tm = xc_ref.shape[0]
    i = pl.program_id(0)
    is_ctx = i < N_CTX // tm
    mod = mod_ref[0]
    out_a = jnp.where(is_ctx, oac_ref[...], oal_ref[...])
    out_c = _pool_rows(up_ref[...], uc_ref[...], un_ref[...], wp_ref[...], psc_ref[...], i * tm,
                       jnp.where(is_ctx, SEQ, DEC_SEQ)).astype(bf16)
    hsum = hf_ref[...] + hb_ref[...]
    outs_b = []
    for hd in range(ML_HEADS):
        sl = slice(hd * ML_PAD, (hd + 1) * ML_PAD)
        hh = hsum[:, sl]
        ms = jnp.sum(hh * hh, axis=-1, keepdims=True) * (1.0 / ML_DIM)
        hn = hh * lax.rsqrt(ms + EPS) * mln_ref[:, sl]
        outs_b.append((jax.nn.sigmoid(ob_ref[:, sl].astype(f32)) * hn).astype(bf16))
    out_b = jnp.concatenate(outs_b, axis=1)
    mixed = (jnp.dot(out_a, wo_ref[0:NA_WIDTH, :], preferred_element_type=f32)
             + jnp.dot(out_b, wo_ref[NA_WIDTH:NA_WIDTH + ML_PW, :], preferred_element_type=f32)
             + jnp.dot(out_c, wo_ref[NA_WIDTH + ML_PW:, :], preferred_element_type=f32))
    x1 = jnp.where(is_ctx, xc_ref[...], xl_ref[...]) + mod[2:3] * mixed
    x1_ref[...] = x1
    h2 = x1 * lax.rsqrt(jnp.mean(x1 * x1, axis=-1, keepdims=True) + EPS) * n2_ref[...]
    h2 = h2 * (1.0 + mod[4:5]) + mod[3:4]
    h2_ref[...] = h2.astype(bf16)
    h_hi = h2.astype(bf16)
    h_lo = (h2 - h_hi.astype(f32)).astype(bf16)
    w_hi = wr_ref[...].astype(bf16)
    w_lo = (wr_ref[...] - w_hi.astype(f32)).astype(bf16)
    route_t = _route(_nt(w_hi, h_hi) + (_nt(w_hi, h_lo) + _nt(w_lo, h_hi)), br_ref[...])
    rt_ref[...] = route_t[0:8]
    rc_ref[...] = route_t.T


def _out_proj(x_ctx, x_lat, x_lat_block0, mods, layer, oa_ctx, oa_lat, hf, hb, qvo, pin, w_bd, psc, mln, wo, n2,
              wr_t, br_t):
    tm = TOK_TILE
    const = lambda i: (0, 0)
    lyr = lambda shape: pl.BlockSpec((None,) + shape, lambda i: (layer, 0, 0))
    row = lambda i: (i, 0)
    n_ctx_tiles = N_CTX // tm
    halo_blocks = tm // POOL_HALO
    return pl.pallas_call(
        _out_kernel,
        grid=(N_TOK // tm,),
        in_specs=[pl.BlockSpec((tm, D_MODEL), lambda i: (jnp.minimum(i, n_ctx_tiles - 1), 0)),
                  pl.BlockSpec((tm, D_MODEL), lambda i: (jnp.maximum(i - n_ctx_tiles, 0) + x_lat_block0, 0)),
                  pl.BlockSpec((1, 6, D_MODEL), lambda i: (_mod_row(i, tm), 0, 0)),
                  pl.BlockSpec((tm, NA_WIDTH), lambda i: (jnp.minimum(i, n_ctx_tiles - 1), 0)),
                  pl.BlockSpec((tm, NA_WIDTH), lambda i: (jnp.maximum(i - n_ctx_tiles, 0), 0)),
                  pl.BlockSpec((tm, ML_PW), row),
                  pl.BlockSpec((tm, ML_PW), row),
                  pl.BlockSpec((tm, ML_PW), lambda i: (i, 2)),
                  pl.BlockSpec((POOL_HALO, POOL_WIDTH), lambda i: (jnp.maximum(i * halo_blocks - 1, 0), 0)),
                  pl.BlockSpec((tm, POOL_WIDTH), row),
                  pl.BlockSpec((POOL_HALO, POOL_WIDTH),
                               lambda i: (jnp.minimum((i + 1) * halo_blocks, N_TOK // POOL_HALO - 1), 0)),
                  lyr((POOL_WIDTH, POOL_WIDTH)), lyr((1, POOL_WIDTH)), lyr((1, ML_PW)),
                  lyr((NA_WIDTH + ML_PW + POOL_WIDTH, D_MODEL)), lyr((1, D_MODEL)),
                  pl.BlockSpec((N_EXPERTS, D_MODEL), const),
                  pl.BlockSpec((N_EXPERTS, 1), const)],
        out_specs=[pl.BlockSpec((tm, D_MODEL), row),
                   pl.BlockSpec((tm, D_MODEL), row),
                   pl.BlockSpec((8, tm), lambda i: (0, i)),
                   pl.BlockSpec((tm, LANE), row)],
        out_shape=[jax.ShapeDtypeStruct((N_TOK, D_MODEL), f32),
                   jax.ShapeDtypeStruct((N_TOK, D_MODEL), bf16),
                   jax.ShapeDtypeStruct((8, N_TOK), f32),
                   jax.ShapeDtypeStruct((N_TOK, LANE), f32)],
        compiler_params=pltpu.CompilerParams(dimension_semantics=("arbitrary",), vmem_limit_bytes=VMEM_LIMIT),
        name="out_proj_router",
    )(x_ctx, x_lat, mods, oa_ctx, oa_lat, hf, hb, qvo, pin, pin, pin, w_bd, psc, mln, wo, n2, wr_t, br_t)


def _ceil_to(x, m):
    return jnp.floor((x + (m - 1)) * (1.0 / m)) * m


def _prefix_over_experts(v):
    er = lax.broadcasted_iota(jnp.int32, (N_EXPERTS, N_EXPERTS), 0)
    ec = lax.broadcasted_iota(jnp.int32, (N_EXPERTS, N_EXPERTS), 1)
    return jnp.dot(jnp.where(ec < er, 1.0, 0.0), v, preferred_element_type=f32, precision=HI)


def _experts_to_lanes(v):
    sub = lax.broadcasted_iota(jnp.int32, (N_EXPERTS, LANE), 0)
    lane = lax.broadcasted_iota(jnp.int32, (N_EXPERTS, LANE), 1)
    return jnp.sum(jnp.where(sub == lane, v, 0.0), axis=0, keepdims=True)


def _expert_hits(rt):
    rid = lax.broadcasted_iota(jnp.int32, (N_EXPERTS, rt.shape[1]), 0).astype(f32)
    oh0 = rid == rt[0:1, :]
    oh1 = rid == rt[1:2, :]
    both = jnp.where(oh0 | oh1, 1.0, 0.0)
    runs = jnp.broadcast_to(_ceil_to(jnp.sum(both, axis=1, keepdims=True), MOE_CHUNK), (N_EXPERTS, LANE))
    return oh0, oh1, both, runs


def _rank_kernel(rt_all_ref, rt_ref, pos_ref, te_ref, tab_ref, carry_ref):
    tm = rt_ref.shape[1]
    step = pl.program_id(0)

    @pl.when(step == 0)
    def _():
        totals = jnp.zeros((N_EXPERTS, LANE), f32)
        for i in range(rt_all_ref.shape[1] // tm):
            totals = totals + _expert_hits(rt_all_ref[:, i * tm:(i + 1) * tm])[3]
        padded = _ceil_to(totals, MOE_TILE)
        off = _prefix_over_experts(padded)
        carry_ref[...] = off
        total = jnp.sum(padded, axis=0, keepdims=True)
        n_used = total * (1.0 / MOE_TILE)
        tile = lax.broadcasted_iota(jnp.int32, (1, LANE), 1).astype(f32)
        row0 = jnp.minimum(tile, n_used - 1.0) * MOE_TILE
        expert = jnp.sum(jnp.where(off <= row0, 1.0, 0.0), axis=0, keepdims=True) - 1.0
        sub = lax.broadcasted_iota(jnp.int32, (8, LANE), 0)
        te_ref[...] = jnp.where(sub == 0, expert, jnp.where(sub == 1, n_used, 0.0)).astype(jnp.int32)

    @pl.when(step > 0)
    def _():
        oh0, oh1, both, runs = _expert_hits(rt_ref[...])
        sr = lax.broadcasted_iota(jnp.int32, (tm, tm), 0)
        sc = lax.broadcasted_iota(jnp.int32, (tm, tm), 1)
        earlier = jnp.dot(both.astype(bf16), jnp.where(sr < sc, 1.0, 0.0).astype(bf16),
                          preferred_element_type=f32)
        g_off = carry_ref[...]
        l_off = _prefix_over_experts(runs)
        g_row = g_off[:, 0:1] + earlier
        l_row = l_off[:, 0:1] + earlier
        pick = lambda oh, v: jnp.sum(jnp.where(oh, v, 0.0), axis=0, keepdims=True)
        rows = (pick(oh0, g_row), pick(oh1, g_row), pick(oh0, l_row), pick(oh1, l_row))
        sub = lax.broadcasted_iota(jnp.int32, (8, tm), 0)
        out = jnp.zeros((8, tm), f32)
        for k, r in enumerate(rows):
            out = jnp.where(sub == k, r, out)
        pos_ref[...] = out.astype(jnp.int32)
        sub = lax.broadcasted_iota(jnp.int32, (8, LANE), 0)
        tab = jnp.zeros((8, LANE), f32)
        for k, v in enumerate((runs * (1.0 / MOE_CHUNK), l_off, g_off)):
            tab = jnp.where(sub == k, _experts_to_lanes(v), tab)
        tab_ref[0] = tab.astype(jnp.int32)
        carry_ref[...] = g_off + runs


def _rank(route_t):
    tm = TOK_TILE
    n_tiles = N_TOK // tm
    tile_of = lambda s: jnp.maximum(s - 1, 0)
    return pl.pallas_call(
        _rank_kernel,
        grid=(1 + n_tiles,),
        in_specs=[pl.BlockSpec((8, N_TOK), lambda s: (0, 0)),
                  pl.BlockSpec((8, tm), lambda s: (0, tile_of(s)))],
        out_specs=[pl.BlockSpec((8, tm), lambda s: (0, tile_of(s))),
                   pl.BlockSpec((8, LANE), lambda s: (0, 0)),
                   pl.BlockSpec((1, 8, LANE), lambda s: (tile_of(s), 0, 0))],
        out_shape=[jax.ShapeDtypeStruct((8, N_TOK), jnp.int32),
                   jax.ShapeDtypeStruct((8, LANE), jnp.int32),
                   jax.ShapeDtypeStruct((n_tiles, 8, LANE), jnp.int32)],
        scratch_shapes=[pltpu.VMEM((N_EXPERTS, LANE), f32)],
        compiler_params=pltpu.CompilerParams(dimension_semantics=("arbitrary",)),
        name="moe_rank",
    )(route_t, route_t)


def _dispatch_kernel(tab_ref, h_ref, rows_ref, xs_in_ref, xs_ref, loc, sem):
    del xs_in_ref
    tm = h_ref.shape[0]
    i = pl.program_id(0)
    slot = i % 2
    rid = lax.broadcasted_iota(jnp.int32, (MOE_LOCAL_ROWS, tm), 0)
    sel = (rid == rows_ref[2:3, :]) | (rid == rows_ref[3:4, :])
    loc[slot] = jnp.dot(jnp.where(sel, 1.0, 0.0).astype(bf16), h_ref[...], preferred_element_type=f32).astype(bf16)

    def chunk_copy(sl, src_row, dst_row):
        return pltpu.make_async_copy(loc.at[sl, pl.ds(pl.multiple_of(src_row, MOE_CHUNK), MOE_CHUNK)],
                                     xs_ref.at[pl.ds(pl.multiple_of(dst_row, MOE_CHUNK), MOE_CHUNK)], sem.at[sl])

    def chunks_of(tile):
        return sum(tab_ref[3 * tile * N_EXPERTS + e] for e in range(N_EXPERTS))

    def wait_chunks(sl, n):
        def wait_one(c, carry):
            chunk_copy(sl, 0, 0).wait()
            return carry

        lax.fori_loop(0, n, wait_one, 0)

    for e in range(N_EXPERTS):
        n_chunks, l_off, g_off = (tab_ref[(3 * i + k) * N_EXPERTS + e] for k in range(3))

        def issue(c, carry, l_off=l_off, g_off=g_off):
            chunk_copy(slot, l_off + c * MOE_CHUNK, g_off + c * MOE_CHUNK).start()
            return carry

        lax.fori_loop(0, n_chunks, issue, 0)

    @pl.when(i > 0)
    def _():
        wait_chunks(1 - slot, chunks_of(i - 1))

    @pl.when(i == pl.num_programs(0) - 1)
    def _():
        wait_chunks(slot, chunks_of(i))


def _dispatch(run_table, h2, rows, xs_init):
    tm = TOK_TILE
    return pl.pallas_call(
        _dispatch_kernel,
        grid_spec=pltpu.PrefetchScalarGridSpec(
            num_scalar_prefetch=1,
            grid=(N_TOK // tm,),
            in_specs=[pl.BlockSpec((tm, D_MODEL), lambda i, tab: (i, 0)),
                      pl.BlockSpec((8, tm), lambda i, tab: (0, i)),
                      pl.BlockSpec(memory_space=pl.ANY)],
            out_specs=pl.BlockSpec(memory_space=pl.ANY),
            scratch_shapes=[pltpu.VMEM((2, MOE_LOCAL_ROWS, D_MODEL), bf16), pltpu.SemaphoreType.DMA((2,))]),
        out_shape=jax.ShapeDtypeStruct(xs_init.shape, xs_init.dtype),
        input_output_aliases={3: 0},
        compiler_params=pltpu.CompilerParams(dimension_semantics=("arbitrary",), vmem_limit_bytes=VMEM_LIMIT),
        name="moe_dispatch",
    )(run_table, h2, rows, xs_init)


def _expert_kernel(te_ref, xs_hbm, wg_hbm, wu_hbm, wd_hbm, *rest, layer, reuse):
    ys_ref, wg_f32, wu_f32, wd_f32, wg_bf, wu_bf, wd_bf, x_ring, slot_ref, sem, x_sem = rest[-11:]
    n_ring, tm = x_ring.shape[0], x_ring.shape[1]
    j = pl.program_id(0)
    n_used = te_ref[1, 0]
    used = j < n_used

    def tile_copy(t):
        return pltpu.make_async_copy(xs_hbm.at[pl.ds(pl.multiple_of(t * tm, tm), tm)], x_ring.at[t % n_ring],
                                     x_sem.at[t % n_ring])

    @pl.when(j == 0)
    def _():
        for t in range(n_ring - 1):
            tile_copy(t).start()

    @pl.when(j + (n_ring - 1) < n_used)
    def _():
        tile_copy(j + (n_ring - 1)).start()

    expert = te_ref[0, j]
    new_expert = jnp.logical_or(j == 0, expert != te_ref[0, jnp.maximum(j - 1, 0)])

    def weight_copies(e, slot):
        return [pltpu.make_async_copy(hbm.at[layer, e], buf.at[slot], sem.at[slot])
                for hbm, buf in ((wg_hbm, wg_f32), (wu_hbm, wu_f32), (wd_hbm, wd_f32))]

    @pl.when(j == 0)
    def _():
        slot_ref[0] = 1
        for cp in weight_copies(expert, 0):
            cp.start()

    if not reuse:
        @pl.when(jnp.logical_not(used))
        def _():
            ys_ref[...] = jnp.zeros_like(ys_ref)

    @pl.when(used & new_expert)
    def _():
        slot = 1 - slot_ref[0]
        slot_ref[0] = slot
        for cp in weight_copies(expert, slot):
            cp.wait()
        wg_bf[...] = wg_f32[slot].astype(bf16)
        wu_bf[...] = wu_f32[slot].astype(bf16)
        wd_bf[...] = wd_f32[slot].astype(bf16)
        nxt = lax.while_loop(lambda t: (t < n_used) & (te_ref[0, jnp.minimum(t, LANE - 1)] == expert),
                             lambda t: t + 1, j + 1)

        @pl.when(nxt < n_used)
        def _():
            for cp in weight_copies(te_ref[0, nxt], 1 - slot):
                cp.start()

    @pl.when(used)
    def _():
        tile_copy(j).wait()
        x = x_ring[j % n_ring]
        hg = jnp.dot(x, wg_bf[...], preferred_element_type=f32)
        hu = jnp.dot(x, wu_bf[...], preferred_element_type=f32)
        hid = (hg * jax.nn.sigmoid(hg) * hu).astype(bf16)
        ys_ref[...] = jnp.dot(hid, wd_bf[...], preferred_element_type=f32)


def _experts(te, xs, w_gate, w_up, w_down, layer, ys_prev=None):
    tm = MOE_TILE
    row = lambda j, te: (jnp.minimum(j, te[1, 0] - 1), 0)
    hbm = pl.BlockSpec(memory_space=pl.ANY)
    reuse = ys_prev is not None
    return pl.pallas_call(
        functools.partial(_expert_kernel, layer=layer, reuse=reuse),
        grid_spec=pltpu.PrefetchScalarGridSpec(
            num_scalar_prefetch=1,
            grid=(MOE_ROWS // tm,),
            in_specs=[hbm, hbm, hbm, hbm] + ([hbm] if reuse else []),
            out_specs=pl.BlockSpec((tm, D_MODEL), row if reuse else (lambda j, te: (j, 0))),
            scratch_shapes=[pltpu.VMEM((2, D_MODEL, D_EXPERT), f32), pltpu.VMEM((2, D_MODEL, D_EXPERT), f32),
                            pltpu.VMEM((2, D_EXPERT, D_MODEL), f32),
                            pltpu.VMEM((D_MODEL, D_EXPERT), bf16), pltpu.VMEM((D_MODEL, D_EXPERT), bf16),
                            pltpu.VMEM((D_EXPERT, D_MODEL), bf16),
                            pltpu.VMEM((4, tm, D_MODEL), bf16),
                            pltpu.SMEM((1,), jnp.int32), pltpu.SemaphoreType.DMA((2,)),
                            pltpu.SemaphoreType.DMA((4,))]),
        out_shape=jax.ShapeDtypeStruct((MOE_ROWS, D_MODEL), f32),
        input_output_aliases={5: 0} if reuse else {},
        compiler_params=pltpu.CompilerParams(dimension_semantics=("arbitrary",), vmem_limit_bytes=VMEM_LIMIT),
        name="moe_experts",
    )(te, xs, w_gate, w_up, w_down, *([ys_prev] if reuse else []))


def _gather_expert_rows(pos0_ref, pos1_ref, ys_ref, buf, sem, tile_of):
    rows = buf.shape[2]
    i = pl.program_id(0)
    slot = i % 2

    def issue(tile, sl):
        base = tile * rows

        def body(t, carry):
            for s, pos_ref in enumerate((pos0_ref, pos1_ref)):
                pltpu.make_async_copy(ys_ref.at[pl.ds(pos_ref[base + t], 1)], buf.at[sl, s, pl.ds(t, 1)],
                                      sem.at[sl]).start()
            return carry

        lax.fori_loop(0, rows, body, 0, unroll=8)

    @pl.when(i == 0)
    def _():
        issue(tile_of(0), 0)

    @pl.when(i + 1 < pl.num_programs(0))
    def _():
        issue(tile_of(i + 1), 1 - slot)

    for s in range(2):
        pltpu.make_async_copy(ys_ref.at[pl.ds(0, rows)], buf.at[slot, s], sem.at[slot]).wait()
    return buf[slot, 0], buf[slot, 1]


def _moe_residual(pos0_ref, pos1_ref, ys_ref, x1_ref, rc_ref, mod_ref, buf, sem, tile_of=lambda step: step):
    y0, y1 = _gather_expert_rows(pos0_ref, pos1_ref, ys_ref, buf, sem, tile_of)
    rc = rc_ref[...]
    return x1_ref[...] + mod_ref[0][5:6] * (rc[:, 2:3] * y0 + rc[:, 3:4] * y1)


def _final_kernel(pos0_ref, pos1_ref, ys_ref, x1_ref, rc_ref, mod_ref, fn_ref, yc_ref, yl_ref, buf, sem):
    x2 = _moe_residual(pos0_ref, pos1_ref, ys_ref, x1_ref, rc_ref, mod_ref, buf, sem)
    y = x2 * lax.rsqrt(jnp.mean(x2 * x2, axis=-1, keepdims=True) + EPS) * fn_ref[...]
    is_ctx = pl.program_id(0) < N_CTX // x1_ref.shape[0]

    @pl.when(is_ctx)
    def _():
        yc_ref[...] = y

    @pl.when(jnp.logical_not(is_ctx))
    def _():
        yl_ref[...] = y


def _final_combine(pos0, pos1, ys, x1, rc, mods, fn):
    tc = TOK_TILE
    row = lambda i, p0, p1: (i, 0)
    n_ctx_tiles = N_CTX // tc
    return pl.pallas_call(
        _final_kernel,
        grid_spec=pltpu.PrefetchScalarGridSpec(
            num_scalar_prefetch=2,
            grid=(N_TOK // tc,),
            in_specs=[pl.BlockSpec(memory_space=pl.ANY),
                      pl.BlockSpec((tc, D_MODEL), row),
                      pl.BlockSpec((tc, LANE), row),
                      pl.BlockSpec((1, 6, D_MODEL), lambda i, p0, p1: (_mod_row(i, tc), 0, 0)),
                      pl.BlockSpec((1, D_MODEL), lambda i, p0, p1: (0, 0))],
            out_specs=[pl.BlockSpec((tc, D_MODEL), lambda i, p0, p1: (jnp.minimum(i, n_ctx_tiles - 1), 0)),
                       pl.BlockSpec((tc, D_MODEL), lambda i, p0, p1: (jnp.maximum(i - n_ctx_tiles, 0), 0))],
            scratch_shapes=[pltpu.VMEM((2, 2, tc, D_MODEL), f32), pltpu.SemaphoreType.DMA((2,))]),
        out_shape=[jax.ShapeDtypeStruct((N_CTX, D_MODEL), f32), jax.ShapeDtypeStruct((N_LAT, D_MODEL), f32)],
        compiler_params=pltpu.CompilerParams(dimension_semantics=("arbitrary",), vmem_limit_bytes=VMEM_LIMIT),
        name="moe_combine_final",
    )(pos0, pos1, ys, x1, rc, mods, fn)


def _moe_experts(h2, route_t, w_gate, w_up, w_down, layer, xs_buf, ys_prev):
    rows, te, runs = _rank(route_t)
    xs = _dispatch(runs[:, :3, :N_EXPERTS].reshape(-1), h2, rows, xs_buf)
    return rows[0], rows[1], _experts(te, xs, w_gate, w_up, w_down, layer, ys_prev), xs


def _pad_heads(w):
    lead = w.shape[:-1]
    w = w.reshape(*lead, ML_HEADS, ML_DIM)
    w = jnp.pad(w, [(0, 0)] * len(lead) + [(0, 0), (0, ML_PAD - ML_DIM)])
    return w.reshape(*lead, ML_PW)


def _pack_in_cols(wb):
    o = 0
    qa = wb[..., o:o + NA_WIDTH] * (NA_DIM ** -0.5)
    ka = wb[..., o + NA_WIDTH:o + 2 * NA_WIDTH]
    va = wb[..., o + 2 * NA_WIDTH:o + 3 * NA_WIDTH]
    o += 3 * NA_WIDTH
    qb, kb, vb, ob = [_pad_heads(wb[..., o + j * ML_WIDTH:o + (j + 1) * ML_WIDTH]) for j in range(4)]
    o += 4 * ML_WIDTH
    gates = wb[..., o:o + N_GATE_COLS]
    o += N_GATE_COLS
    pool = wb[..., o:o + POOL_WIDTH]
    main = jnp.concatenate([qa, ka, va, qb, vb, ob, pool], axis=-1)
    gates_p = jnp.pad(gates, [(0, 0)] * (gates.ndim - 1) + [(0, LANE - N_GATE_COLS)])
    return main, gates_p, jnp.concatenate([kb, gates], axis=-1)


def _pack_w_in(w, b):
    w_main, w_gates, w_feat = _pack_in_cols(w)
    b_main, b_gates, b_feat = _pack_in_cols(b.astype(f32))
    return (w_main.astype(bf16), b_main[:, None], w_gates.astype(bf16), b_gates[:, None],
            jnp.swapaxes(w_feat, 1, 2).astype(bf16), b_feat[:, :, None])


def _pack_w_out(w):
    n_l = w.shape[0]
    wb = w[:, NA_WIDTH:NA_WIDTH + ML_WIDTH].reshape(n_l, ML_HEADS, ML_DIM, D_MODEL)
    wb = jnp.pad(wb, ((0, 0), (0, 0), (0, ML_PAD - ML_DIM), (0, 0))).reshape(n_l, ML_PW, D_MODEL)
    return jnp.concatenate([w[:, :NA_WIDTH], wb, w[:, NA_WIDTH + ML_WIDTH:]], axis=1).astype(bf16)


def _block_diag(w):
    n_l, g, c, _ = w.shape
    eye = jnp.eye(g, dtype=w.dtype)
    return (eye[None, :, None, :, None] * w[:, :, :, None, :]).reshape(n_l, g * c, g * c)


def kernel(x_prompt, x_sample, cache_k_attn, cache_v_attn, state_mlstm_C, state_mlstm_n, state_mlstm_m, c, c_ctx,
           w_ada, b_ada, norm1, w_in, b_in, rpb, ml_norm, w_pool, pool_scale, w_out, norm2, w_router, b_router,
           w_gate, w_up, w_down, final_norm):
    dt = x_prompt.dtype
    x_ctx = x_prompt.reshape(N_CTX, D_MODEL).astype(f32)
    x_lat = x_sample.reshape(N_LAT, D_MODEL).astype(f32)
    x_lat_block0 = 0
    cvec = jnp.concatenate([c_ctx[None], c, jnp.zeros((8 - 1 - DEC_BATCH, D_MODEL), c.dtype)], axis=0).astype(f32)
    mods_all = _ada(cvec, w_ada.astype(f32), b_ada.astype(f32))
    mods_all = mods_all[:, :1 + DEC_BATCH].reshape(DEPTH, 1 + DEC_BATCH, 6, D_MODEL)

    wr_t = w_router.astype(f32).T
    br_t = b_router.astype(f32)[:, None]
    fn = final_norm.astype(f32)[None]

    na_bias = _na_bias_tables(rpb)
    xs_buf = jnp.zeros((MOE_ROWS, D_MODEL), bf16)
    in_params = (norm1.astype(f32)[:, None],) + _pack_w_in(w_in, b_in)
    out_params = (_block_diag(w_pool.astype(f32)).astype(bf16), pool_scale.astype(f32)[:, None],
                  _pad_heads(ml_norm.astype(f32))[:, None], _pack_w_out(w_out), norm2.astype(f32)[:, None])

    new_k = jnp.zeros((BATCH, DEPTH, SEQ, NA_WIDTH), f32)
    new_v = jnp.zeros_like(new_k)
    Cs, ns, ms = [], [], []
    pending = None
    for l in range(DEPTH):
        mods = mods_all[l]
        if pending is None:
            qkva, new_k, new_v, qvo, kt, gates, gates_t, pin = _in_proj(x_ctx, x_lat, mods, l, *in_params,
                                                                        new_k, new_v)
        else:
            x, qkva, new_k, new_v, qvo, kt, gates, gates_t, pin = _moe_in_proj(
                *pending, mods_all[l - 1], mods, l, *in_params, new_k, new_v)
            x_ctx, x_lat, x_lat_block0 = x, x, N_CTX // TOK_TILE

        oa_ctx = _ctx_attention(qkva.reshape(N_TOK // SEQ, SEQ, W_A))
        ck = (cache_k_attn[:, l].reshape(DEC_BATCH, PAST_LEN, NA_WIDTH)).astype(bf16)
        cv = (cache_v_attn[:, l].reshape(DEC_BATCH, PAST_LEN, NA_WIDTH)).astype(bf16)
        oa_lat = _neighborhood_attention(qkva.reshape(N_TOK // DEC_SEQ, DEC_SEQ, W_A), ck, cv, na_bias, l)

        c_l, m_l = _pack_ml_state(state_mlstm_C[:, l], state_mlstm_n[:, l], state_mlstm_m[:, l])
        hf, hb, c_fin, m_fin = _mlstm(qvo, kt, gates, gates_t, c_l, m_l)
        C_l, n_l, m_l2 = _unpack_ml_state(c_fin[:BATCH], m_fin[:BATCH])
        Cs.append(C_l)
        ns.append(n_l)
        ms.append(m_l2)

        x1, h2, route_t, rc = _out_proj(x_ctx, x_lat, x_lat_block0, mods, l,
                                        oa_ctx.reshape(N_CTX, NA_WIDTH), oa_lat.reshape(N_LAT, NA_WIDTH),
                                        hf.reshape(N_TOK, ML_PW), hb.reshape(N_TOK, ML_PW), qvo, pin,
                                        *out_params, wr_t, br_t)
        ys_prev = None if pending is None else pending[2]
        pos0, pos1, ys, xs_buf = _moe_experts(h2, route_t, w_gate, w_up, w_down, l, xs_buf, ys_prev)
        pending = (pos0, pos1, ys, x1, rc)

    x = _final_combine(*pending, mods_all[DEPTH - 1], fn)
    y_prompt = x[0].reshape(BATCH, SEQ, D_MODEL).astype(dt)
    y_sample = x[1].reshape(DEC_BATCH, DEC_SEQ, D_MODEL).astype(dt)
    new_k, new_v = (a.reshape(BATCH, DEPTH, SEQ, NA_HEADS, NA_DIM).astype(dt) for a in (new_k, new_v))
    return (y_prompt, y_sample, new_k, new_v,
            jnp.stack(Cs, axis=1).astype(dt), jnp.stack(ns, axis=1).astype(dt), jnp.stack(ms, axis=1).astype(dt))
```
